```python
import jax, jax.numpy as jnp
from jax import lax
import numpy as np

D_MODEL = 1024
BATCH = 16
SEQ = 256
DEPTH = 2
DEC_BATCH = 2
DEC_SEQ = 2048
PAST_LEN = 256

GRID_W = 64
ATTN_BLOCK = 128
WINDOW = 128
SGU_CHUNK = 128
RET_CHUNK = 128
HEAD_DIM = 64
GROUP_WIDTH = D_MODEL // 4
A_HEADS = GROUP_WIDTH // HEAD_DIM
A_KV_HEADS = A_HEADS // 2
A_GROUP = A_HEADS // A_KV_HEADS
B_HEADS = 4
B_HEAD_DIM = GROUP_WIDTH // B_HEADS
C_HEADS = 4
C_HEAD_DIM = GROUP_WIDTH // C_HEADS
D_GROUPS = 4
D_GROUP_DIM = GROUP_WIDTH // D_GROUPS
POOL_WINDOWS = (2, 4, 8, 16)
D_FF = 2816
CONV_W = 3
ROPE_BASE = 10000.0
LN_EPS = 1e-5
NEG_INF = -1e30
SPLIT_SIZES = (GROUP_WIDTH, A_KV_HEADS * HEAD_DIM, A_KV_HEADS * HEAD_DIM,
               GROUP_WIDTH, GROUP_WIDTH,
               GROUP_WIDTH, GROUP_WIDTH, GROUP_WIDTH, GROUP_WIDTH, GROUP_WIDTH,
               GROUP_WIDTH)
IN_WIDTH = sum(SPLIT_SIZES)

kernel_name = "hybrid_prefix_diffusion_step"

F32 = jnp.float32


def layer_norm(x, w, b):
    xf = x.astype(F32)
    mu = xf.mean(-1, keepdims=True)
    var = jnp.square(xf - mu).mean(-1, keepdims=True)
    return ((xf - mu) * lax.rsqrt(var + LN_EPS) * w.astype(F32) + b.astype(F32)).astype(x.dtype)


def modulation(cond, w, b):
    m = jax.nn.silu(cond) @ w + b
    return jnp.split(m[..., None, :], 6, axis=-1)


def project(h, w_in):
    z = h @ w_in
    points, acc = [], 0
    for s in SPLIT_SIZES[:-1]:
        acc += s
        points.append(acc)
    return jnp.split(z, points, axis=-1)


def axial_rope(x):
    n = x.shape[1]
    rows = n // GRID_W
    r, col = jnp.meshgrid(jnp.arange(rows), jnp.arange(GRID_W), indexing="ij")
    half = HEAD_DIM // 2
    freqs = ROPE_BASE ** (-jnp.arange(0, half, 2, dtype=F32) / half)

    def rot(xa, pos):
        ang = pos.reshape(-1).astype(F32)[:, None] * freqs[None, :]
        cos = jnp.cos(ang)[None, :, None, :].astype(x.dtype)
        sin = jnp.sin(ang)[None, :, None, :].astype(x.dtype)
        x1, x2 = jnp.split(xa, 2, axis=-1)
        return jnp.concatenate([x1 * cos - x2 * sin, x1 * sin + x2 * cos], axis=-1)

    return jnp.concatenate([rot(x[..., :half], r), rot(x[..., half:], col)], axis=-1)


def softmax_with_sink(s, sink):
    col = jnp.broadcast_to(sink.astype(F32)[:, :, None, None], s.shape[:-1] + (1,))
    return jax.nn.softmax(jnp.concatenate([s, col], axis=-1), axis=-1)[..., :-1]


def attn_context(q, k, v, sink):
    b, l = q.shape[:2]
    nb = l // ATTN_BLOCK
    scale = HEAD_DIM ** -0.5
    qb = q.reshape(b, nb, ATTN_BLOCK, A_KV_HEADS, A_GROUP, HEAD_DIM).swapaxes(0, 1)

    def one_block(qi):
        s = jnp.einsum("bqkgd,bjkd->bkgqj", qi, k).astype(F32) * scale
        p = softmax_with_sink(s, sink)
        return jnp.einsum("bkgqj,bjkd->bqkgd", p.astype(v.dtype), v)

    o = lax.map(one_block, qb)
    return o.swapaxes(0, 1).reshape(b, l, A_HEADS * HEAD_DIM)


def attn_latent(q, k, v, ck, cv, sink):
    b, n = q.shape[:2]
    nb = n // ATTN_BLOCK
    scale = HEAD_DIM ** -0.5
    qb = q.reshape(b, nb, ATTN_BLOCK, A_KV_HEADS, A_GROUP, HEAD_DIM)

    def band(a):
        ap = jnp.pad(a, ((0, 0), (ATTN_BLOCK, ATTN_BLOCK), (0, 0), (0, 0)))
        ap = ap.reshape(b, nb + 2, ATTN_BLOCK, A_KV_HEADS, HEAD_DIM)
        return jnp.concatenate([ap[:, :-2], ap[:, 1:-1], ap[:, 2:]], axis=2)

    kw, vw = band(k), band(v)
    s_loc = jnp.einsum("bnqkgd,bnjkd->bnkgqj", qb, kw).astype(F32) * scale
    blk = jnp.arange(nb)[:, None, None]
    qpos = blk * ATTN_BLOCK + jnp.arange(ATTN_BLOCK)[None, :, None]
    kpos = (blk - 1) * ATTN_BLOCK + jnp.arange(3 * ATTN_BLOCK)[None, None, :]
    valid = (kpos >= 0) & (kpos < n) & (jnp.abs(kpos - qpos) <= WINDOW)
    s_loc = jnp.where(valid[None, :, None, None], s_loc, NEG_INF)
    s_ctx = jnp.einsum("bnqkgd,bjkd->bnkgqj", qb, ck).astype(F32) * scale
    p = softmax_with_sink(jnp.concatenate([s_loc, s_ctx], axis=-1), sink).astype(v.dtype)
    o = (jnp.einsum("bnkgqj,bnjkd->bnqkgd", p[..., :3 * ATTN_BLOCK], vw)
         + jnp.einsum("bnkgqj,bjkd->bnqkgd", p[..., 3 * ATTN_BLOCK:], cv))
    return o.reshape(b, n, A_HEADS * HEAD_DIM)


def spatial_gating(u, v, nw, nb_, ws, bs):
    b, n, _ = v.shape
    v = layer_norm(v, nw, nb_)
    vb = v.reshape(b, n // SGU_CHUNK, SGU_CHUNK, B_HEADS, B_HEAD_DIM)
    s = jnp.einsum("hij,bcjhd->bcihd", ws, vb) + bs.T[:, :, None]
    return u * s.reshape(b, n, GROUP_WIDTH)


def retention_scan(q, k, v, log_g, s0):
    b, n = q.shape[:2]
    nc = n // RET_CHUNK
    i = jnp.arange(RET_CHUNK, dtype=F32)
    rel = i[:, None] - i[None, :]
    dmat = jnp.where(rel >= 0, jnp.exp(jnp.maximum(rel, 0.0)[None] * log_g[:, None, None]), 0.0)
    q_dec = jnp.exp((i + 1.0)[:, None] * log_g[None, :])[:, :, None]
    k_dec = jnp.exp((RET_CHUNK - 1.0 - i)[:, None] * log_g[None, :])[:, :, None]
    c_dec = jnp.exp(RET_CHUNK * log_g)[:, None, None]
    chunks = tuple(a.reshape(b, nc, RET_CHUNK, C_HEADS, C_HEAD_DIM).swapaxes(0, 1) for a in (q, k, v))

    def step(state, inp):
        qc, kc, vc = inp
        sc = jnp.einsum("bihd,bjhd->bhij", qc, kc) * dmat
        o = (jnp.einsum("bhij,bjhe->bihe", sc, vc)
             + jnp.einsum("bihd,bhde->bihe", qc * q_dec, state))
        state = state * c_dec + jnp.einsum("bjhd,bjhe->bhde", kc * k_dec, vc)
        return state, o

    state, o = lax.scan(step, s0, chunks)
    return o.swapaxes(0, 1).reshape(b, n, C_HEADS, C_HEAD_DIM), state


def head_group_norm(o, w):
    mu = o.mean(-1, keepdims=True)
    var = jnp.square(o - mu).mean(-1, keepdims=True)
    y = (o - mu) * lax.rsqrt(var + LN_EPS) * w.astype(F32).reshape(C_HEADS, C_HEAD_DIM)
    return y.reshape(o.shape[0], o.shape[1], GROUP_WIDTH)


def retention(q, k, v, g_f, g_b, decay, gn_w, s0_f, s0_b):
    b, n = q.shape[:2]
    heads = lambda a: a.astype(F32).reshape(b, n, C_HEADS, C_HEAD_DIM)
    qf, kf, vf = heads(q), heads(k) * (C_HEAD_DIM ** -0.5), heads(v)
    log_g = jax.nn.log_sigmoid(decay.astype(F32))
    o_f, s_f = retention_scan(qf, kf, vf, log_g[0], s0_f.astype(F32))
    o_b, s_b = retention_scan(qf[:, ::-1], kf[:, ::-1], vf[:, ::-1], log_g[1], s0_b.astype(F32))
    o_b = o_b[:, ::-1]
    y = (jax.nn.silu(g_f.astype(F32)) * head_group_norm(o_f, gn_w[0])
         + jax.nn.silu(g_b.astype(F32)) * head_group_norm(o_b, gn_w[1]))
    return y.astype(q.dtype), s_f, s_b


def multiscale_pool(p, w, scale):
    b, n, _ = p.shape
    pf = p.astype(F32)
    cs = jnp.concatenate([jnp.zeros((b, 1, GROUP_WIDTH), F32), jnp.cumsum(pf, axis=1)], axis=1)
    t = jnp.arange(n)
    outs = []
    for g, win in enumerate(POOL_WINDOWS):
        lo = jnp.clip(t - win // 2, 0, n)
        hi = jnp.clip(t + win // 2, 0, n)
        sl = slice(g * D_GROUP_DIM, (g + 1) * D_GROUP_DIM)
        mean = (cs[:, hi, sl] - cs[:, lo, sl]) / (hi - lo).astype(F32)[None, :, None]
        outs.append(mean - pf[:, :, sl])
    y = jnp.stack(outs, axis=2)
    y = jnp.einsum("bngc,gcd->bngd", y, w.astype(F32)).reshape(b, n, GROUP_WIDTH)
    return (y * scale.astype(F32)).astype(p.dtype)


def conv_ffn(h, up, cw, cb, down):
    u = h @ up
    up_ = jnp.pad(u, ((0, 0), (1, 1), (0, 0)))
    u = up_[:, :-2] * cw[0] + up_[:, 1:-1] * cw[1] + up_[:, 2:] * cw[2] + cb
    a, g = jnp.split(u, 2, axis=-1)
    return (jax.nn.silu(a) * g) @ down


def trunk_layer(x, cond, ctx_k, ctx_v, s0_f, s0_b, lp):
    (w_ada, b_ada, w_in, w_out, attn_sink, sgu_nw, sgu_nb, sgu_ws, sgu_bs, ret_decay, ret_gn_w,
     pool_w, pool_scale, ffn_up, ffn_conv_w, ffn_conv_b, ffn_down, ln_w, ln_b) = lp
    alpha = (2.0 * DEPTH) ** 0.25
    sh_a, sc_a, g_a, sh_f, sc_f, g_f = modulation(cond, w_ada, b_ada)
    bsz, n = x.shape[:2]
    h = x * (1 + sc_a) + sh_a
    qa, ka, va, ub, vb, qc, kc, vc, gcf, gcb, pd = project(h, w_in)
    qa = qa.reshape(bsz, n, A_HEADS, HEAD_DIM)
    ka = ka.reshape(bsz, n, A_KV_HEADS, HEAD_DIM)
    va = va.reshape(bsz, n, A_KV_HEADS, HEAD_DIM)
    sink = attn_sink.reshape(A_KV_HEADS, A_GROUP)
    if ctx_k is None:
        y_a = attn_context(qa.reshape(bsz, n, A_KV_HEADS, A_GROUP, HEAD_DIM), ka, va, sink)
        k_out = ka
    else:
        qr = axial_rope(qa).reshape(bsz, n, A_KV_HEADS, A_GROUP, HEAD_DIM)
        k_out = axial_rope(ka)
        y_a = attn_latent(qr, k_out, va, ctx_k, ctx_v, sink)
    y_b = spatial_gating(ub, vb, sgu_nw, sgu_nb, sgu_ws, sgu_bs)
    y_c, s_f, s_b = retention(qc, kc, vc, gcf, gcb, ret_decay, ret_gn_w, s0_f, s0_b)
    y_d = multiscale_pool(pd, pool_w, pool_scale)
    mix = jnp.concatenate([y_a, y_b, y_c, y_d], axis=-1) @ w_out
    x = layer_norm(alpha * x + g_a * mix, ln_w[0], ln_b[0])
    h = x * (1 + sc_f) + sh_f
    x = layer_norm(alpha * x + g_f * conv_ffn(h, ffn_up, ffn_conv_w, ffn_conv_b, ffn_down), ln_w[1], ln_b[1])
    return x, k_out, va, s_f, s_b


def setup_inputs(seed: int = 0) -> dict:
    key = jax.random.key(seed)
    ks = jax.random.split(key, 26)
    nrm = lambda k, shape, s: jax.random.normal(k, shape, F32) * s
    beta = (8.0 * DEPTH) ** -0.25
    gam = np.stack([1.0 - 2.0 ** (-5.0 - np.arange(C_HEADS)),
                    1.0 - 2.0 ** (-5.5 - np.arange(C_HEADS))])
    decay0 = jnp.asarray(np.log(gam / (1.0 - gam)), F32)
    return {
        "x_prompt": nrm(ks[0], (BATCH, SEQ, D_MODEL), 1.0),
        "x_sample": nrm(ks[1], (DEC_BATCH, DEC_SEQ, D_MODEL), 1.0),
        "cache_attn_k": nrm(ks[2], (DEC_BATCH, DEPTH, PAST_LEN, A_KV_HEADS, HEAD_DIM), 1.0),
        "cache_attn_v": nrm(ks[3], (DEC_BATCH, DEPTH, PAST_LEN, A_KV_HEADS, HEAD_DIM), 1.0),
        "state_ret": nrm(ks[4], (DEC_BATCH, DEPTH, 2, C_HEADS, C_HEAD_DIM, C_HEAD_DIM), 0.5),
        "c": nrm(ks[5], (DEC_BATCH, D_MODEL), 1.0),
        "c_ctx": nrm(ks[6], (D_MODEL,), 1.0),
        "w_ada": nrm(ks[7], (DEPTH, D_MODEL, 6 * D_MODEL), D_MODEL ** -0.5),
        "b_ada": nrm(ks[8], (DEPTH, 6 * D_MODEL), 0.02),
        "w_in": nrm(ks[9], (DEPTH, D_MODEL, IN_WIDTH), D_MODEL ** -0.5),
        "w_out": nrm(ks[10], (DEPTH, D_MODEL, D_MODEL), beta * D_MODEL ** -0.5),
        "attn_sink": nrm(ks[11], (DEPTH, A_HEADS), 0.5),
        "sgu_norm_w": 1.0 + nrm(ks[12], (DEPTH, GROUP_WIDTH), 0.02),
        "sgu_norm_b": nrm(ks[13], (DEPTH, GROUP_WIDTH), 0.02),
        "sgu_ws": nrm(ks[14], (DEPTH, B_HEADS, SGU_CHUNK, SGU_CHUNK), SGU_CHUNK ** -0.5),
        "sgu_bs": 1.0 + nrm(ks[15], (DEPTH, B_HEADS, SGU_CHUNK), 0.02),
        "ret_decay": decay0[None] + nrm(ks[16], (DEPTH, 2, C_HEADS), 0.01),
        "ret_gn_w": 1.0 + nrm(ks[17], (DEPTH, 2, GROUP_WIDTH), 0.02),
        "pool_w": nrm(ks[18], (DEPTH, D_GROUPS, D_GROUP_DIM, D_GROUP_DIM), D_GROUP_DIM ** -0.5),
        "pool_scale": 1.0 + nrm(ks[19], (DEPTH, GROUP_WIDTH), 0.02),
        "ffn_up": nrm(ks[20], (DEPTH, D_MODEL, 2 * D_FF), D_MODEL ** -0.5),
        "ffn_conv_w": nrm(ks[21], (DEPTH, CONV_W, 2 * D_FF), CONV_W ** -0.5),
        "ffn_conv_b": nrm(ks[22], (DEPTH, 2 * D_FF), 0.02),
        "ffn_down": nrm(ks[23], (DEPTH, D_FF, D_MODEL), beta * D_FF ** -0.5),
        "ln_w": 1.0 + nrm(ks[24], (DEPTH, 2, D_MODEL), 0.02),
        "ln_b": nrm(ks[25], (DEPTH, 2, D_MODEL), 0.02),
    }


def reference(x_prompt, x_sample, cache_attn_k, cache_attn_v, state_ret, c, c_ctx,
              w_ada, b_ada, w_in, w_out, attn_sink, sgu_norm_w, sgu_norm_b, sgu_ws, sgu_bs,
              ret_decay, ret_gn_w, pool_w, pool_scale, ffn_up, ffn_conv_w, ffn_conv_b, ffn_down,
              ln_w, ln_b):
    def layer_params(l):
        return (w_ada[l], b_ada[l], w_in[l], w_out[l], attn_sink[l], sgu_norm_w[l], sgu_norm_b[l],
                sgu_ws[l], sgu_bs[l], ret_decay[l], ret_gn_w[l], pool_w[l], pool_scale[l],
                ffn_up[l], ffn_conv_w[l], ffn_conv_b[l], ffn_down[l], ln_w[l], ln_b[l])

    xp, xs = x_prompt, x_sample
    zero_state = jnp.zeros((x_prompt.shape[0], C_HEADS, C_HEAD_DIM, C_HEAD_DIM), F32)
    new_k, new_v, new_s = [], [], []
    for l in range(DEPTH):
        lp = layer_params(l)
        xp, ka, va, s_f, s_b = trunk_layer(xp, c_ctx[None], None, None, zero_state, zero_state, lp)
        new_k.append(ka)
        new_v.append(va)
        new_s.append(jnp.stack([s_f, s_b], axis=1).astype(x_prompt.dtype))
        xs, _, _, _, _ = trunk_layer(xs, c, cache_attn_k[:, l], cache_attn_v[:, l],
                                     state_ret[:, l, 0], state_ret[:, l, 1], lp)
    new_cache_attn_k = jnp.stack(new_k, axis=1)
    new_cache_attn_v = jnp.stack(new_v, axis=1)
    new_state_ret = jnp.stack(new_s, axis=1)
    return (xp, xs, new_cache_attn_k, new_cache_attn_v, new_state_ret)
```

```python
import functools

import numpy as np
import jax
import jax.numpy as jnp
from jax import lax
from jax.experimental import pallas as pl
from jax.experimental.pallas import tpu as pltpu

F32 = jnp.float32
BF16 = jnp.bfloat16

D_MODEL = 1024
BATCH = 16
SEQ = 256
DEPTH = 2
DEC_BATCH = 2
DEC_SEQ = 2048
PAST_LEN = 256
GRID_W = 64
CHUNK = 128
HEAD_DIM = 64
GW = D_MODEL // 4
N_HEADS = 4
POOL_WINDOWS = (2, 4, 8, 16)
POOL_HALO = 8
D_FF = 2816
ROPE_BASE = 10000.0
LN_EPS = 1e-5
NEG_INF = -1e30
IN_WIDTH = 10 * GW
ALPHA = (2.0 * DEPTH) ** 0.25

N_CTX = BATCH * SEQ
N_LAT = DEC_BATCH * DEC_SEQ
ROWS = N_CTX + N_LAT

TM = 1024
CTX_TILES = N_CTX // TM
LAT_TILES_PER_SEQ = DEC_SEQ // TM
NB_IN = 512
FC = 256
NB_ADA = 1536
VMEM_LIMIT = 56 * 1024 * 1024


def _cond_of_tile(i):
    return jnp.where(i < CTX_TILES, 0, 1 + (i - CTX_TILES) // LAT_TILES_PER_SEQ)


def _layer_norm(x, w, b):
    mu = jnp.mean(x, axis=-1, keepdims=True)
    d = x - mu
    var = jnp.mean(d * d, axis=-1, keepdims=True)
    return d * lax.rsqrt(var + LN_EPS) * w + b


def _silu(x):
    return x * jax.nn.sigmoid(x)


def _bdot(a, b):
    return jnp.dot(a.astype(BF16), b.astype(BF16), preferred_element_type=F32)


def _mod_kernel(c_ref, w_ref, b_ref, o_ref):
    o_ref[...] = _bdot(_silu(c_ref[...]), w_ref[...]) + b_ref[...]


def _modulation(cond8, w_ada, b_ada):
    return pl.pallas_call(
        _mod_kernel,
        grid=(DEPTH, 6 * D_MODEL // NB_ADA),
        in_specs=[
            pl.BlockSpec((8, D_MODEL), lambda l, j: (0, 0)),
            pl.BlockSpec((None, D_MODEL, NB_ADA), lambda l, j: (l, 0, j)),
            pl.BlockSpec((None, 1, NB_ADA), lambda l, j: (l, 0, j)),
        ],
        out_specs=pl.BlockSpec((None, 8, NB_ADA), lambda l, j: (l, 0, j)),
        out_shape=jax.ShapeDtypeStruct((DEPTH, 8, 6 * D_MODEL), F32),
        compiler_params=pltpu.CompilerParams(
            dimension_semantics=("arbitrary", "arbitrary"), vmem_limit_bytes=VMEM_LIMIT),
        name="modulation",
    )(cond8, w_ada, b_ada.reshape(DEPTH, 1, 6 * D_MODEL))


def _inproj_kernel(x_ref, mod_ref, w_ref, z_ref, h_scr):
    @pl.when(pl.program_id(1) == 0)
    def _():
        m = mod_ref[...]
        h_scr[...] = (x_ref[...] * (1.0 + m[1:2]) + m[0:1]).astype(BF16)

    z_ref[...] = jnp.dot(h_scr[...], w_ref[...].astype(BF16), preferred_element_type=F32)


def _inproj(x, mod_l, w_in_l):
    return pl.pallas_call(
        _inproj_kernel,
        grid=(ROWS // TM, IN_WIDTH // NB_IN),
        in_specs=[
            pl.BlockSpec((TM, D_MODEL), lambda i, j: (i, 0)),
            pl.BlockSpec((None, 8, D_MODEL), lambda i, j: (_cond_of_tile(i), 0, 0)),
            pl.BlockSpec((D_MODEL, NB_IN), lambda i, j: (0, j)),
        ],
        out_specs=pl.BlockSpec((TM, NB_IN), lambda i, j: (i, j)),
        out_shape=jax.ShapeDtypeStruct((ROWS, IN_WIDTH), F32),
        scratch_shapes=[pltpu.VMEM((TM, D_MODEL), BF16)],
        compiler_params=pltpu.CompilerParams(
            dimension_semantics=("arbitrary", "arbitrary"), vmem_limit_bytes=VMEM_LIMIT),
        name="inproj",
    )(x, mod_l, w_in_l)


def _outproj_kernel(mix_ref, x_ref, mod_ref, w_ref, ln_ref, o_ref):
    y = _bdot(mix_ref[...], w_ref[...])
    m = mod_ref[...]
    ln = ln_ref[...]
    o_ref[...] = _layer_norm(ALPHA * x_ref[...] + m[2:3] * y, ln[0:1], ln[1:2])


def _outproj(mix, x, mod_l, w_out_l, ln8):
    return pl.pallas_call(
        _outproj_kernel,
        grid=(ROWS // TM,),
        in_specs=[
            pl.BlockSpec((TM, D_MODEL), lambda i: (i, 0)),
            pl.BlockSpec((TM, D_MODEL), lambda i: (i, 0)),
            pl.BlockSpec((None, 8, D_MODEL), lambda i: (_cond_of_tile(i), 0, 0)),
            pl.BlockSpec((D_MODEL, D_MODEL), lambda i: (0, 0)),
            pl.BlockSpec((8, D_MODEL), lambda i: (0, 0)),
        ],
        out_specs=pl.BlockSpec((TM, D_MODEL), lambda i: (i, 0)),
        out_shape=jax.ShapeDtypeStruct((ROWS, D_MODEL), F32),
        compiler_params=pltpu.CompilerParams(
            dimension_semantics=("arbitrary",), vmem_limit_bytes=VMEM_LIMIT),
        name="outproj",
    )(mix, x, mod_l, w_out_l, ln8)


def _ffn_kernel(x_ref, xp_ref, xn_ref, mod_ref, upa_ref, upg_ref, cva_ref, cvg_ref, dn_ref,
                ln_ref, o_ref, h_scr, acc_scr):
    i = pl.program_id(0)
    j = pl.program_id(1)
    ext = TM + 2 * POOL_HALO

    @pl.when(j == 0)
    def _():
        m = mod_ref[...]
        scale = 1.0 + m[4:5]
        shift = m[3:4]
        h_scr[0:POOL_HALO, :] = (xp_ref[...] * scale + shift).astype(BF16)
        h_scr[POOL_HALO:POOL_HALO + TM, :] = (x_ref[...] * scale + shift).astype(BF16)
        h_scr[POOL_HALO + TM:ext, :] = (xn_ref[...] * scale + shift).astype(BF16)

    h = h_scr[...]
    seq = jnp.where(i < CTX_TILES, SEQ, DEC_SEQ)
    pos = (lax.broadcasted_iota(jnp.int32, (TM, FC), 0) + i * TM) & (seq - 1)
    first = pos == 0
    last = pos == seq - 1

    def conv(u, cv):
        um1 = pltpu.roll(u, 1, axis=0)[POOL_HALO:POOL_HALO + TM]
        up1 = pltpu.roll(u, ext - 1, axis=0)[POOL_HALO:POOL_HALO + TM]
        u0 = u[POOL_HALO:POOL_HALO + TM]
        um1 = jnp.where(first, 0.0, um1)
        up1 = jnp.where(last, 0.0, up1)
        return um1 * cv[0:1] + u0 * cv[1:2] + up1 * cv[2:3] + cv[3:4]

    a = conv(jnp.dot(h, upa_ref[...].astype(BF16), preferred_element_type=F32), cva_ref[...])
    g = conv(jnp.dot(h, upg_ref[...].astype(BF16), preferred_element_type=F32), cvg_ref[...])
    contrib = _bdot(_silu(a) * g, dn_ref[...])

    @pl.when(j == 0)
    def _():
        acc_scr[...] = contrib

    @pl.when(j > 0)
    def _():
        acc_scr[...] += contrib

    @pl.when(j == D_FF // FC - 1)
    def _():
        m = mod_ref[...]
        ln = ln_ref[...]
        o_ref[...] = _layer_norm(ALPHA * x_ref[...] + m[5:6] * acc_scr[...], ln[2:3], ln[3:4])


def _ffn(x, mod_l, up_l, conv8, down_l, ln8):
    nfc = D_FF // FC
    halo_blocks = TM // POOL_HALO
    last_halo = ROWS // POOL_HALO - 1
    return pl.pallas_call(
        _ffn_kernel,
        grid=(ROWS // TM, nfc),
        in_specs=[
            pl.BlockSpec((TM, D_MODEL), lambda i, j: (i, 0)),
            pl.BlockSpec((POOL_HALO, D_MODEL), lambda i, j: (jnp.maximum(i * halo_blocks - 1, 0), 0)),
            pl.BlockSpec((POOL_HALO, D_MODEL),
                         lambda i, j: (jnp.minimum((i + 1) * halo_blocks, last_halo), 0)),
            pl.BlockSpec((None, 8, D_MODEL), lambda i, j: (_cond_of_tile(i), 0, 0)),
            pl.BlockSpec((D_MODEL, FC), lambda i, j: (0, j)),
            pl.BlockSpec((D_MODEL, FC), lambda i, j: (0, nfc + j)),
            pl.BlockSpec((8, FC), lambda i, j: (0, j)),
            pl.BlockSpec((8, FC), lambda i, j: (0, nfc + j)),
            pl.BlockSpec((FC, D_MODEL), lambda i, j: (j, 0)),
            pl.BlockSpec((8, D_MODEL), lambda i, j: (0, 0)),
        ],
        out_specs=pl.BlockSpec((TM, D_MODEL), lambda i, j: (i, 0)),
        out_shape=jax.ShapeDtypeStruct((ROWS, D_MODEL), F32),
        scratch_shapes=[pltpu.VMEM((TM + 2 * POOL_HALO, D_MODEL), BF16),
                        pltpu.VMEM((TM, D_MODEL), F32)],
        compiler_params=pltpu.CompilerParams(
            dimension_semantics=("arbitrary", "arbitrary"), vmem_limit_bytes=VMEM_LIMIT),
        name="convffn",
    )(x, x, x, mod_l, up_l, up_l, conv8, conv8, down_l, ln8)


def _head_masks(width):
    lane = lax.broadcasted_iota(jnp.int32, (1, width), 1)
    return [(lane >= h * HEAD_DIM) & (lane < (h + 1) * HEAD_DIM) for h in range(width // HEAD_DIM)]


def _stack_heads(x, masks):
    return jnp.concatenate([jnp.where(m, x, jnp.zeros_like(x)) for m in masks], axis=0)


def _rope(x, cos, sin):
    lane = lax.broadcasted_iota(jnp.int32, (1, 128), 1)
    lower = (lane & 31) < 16
    outs = []
    for k in range(x.shape[1] // 128):
        xb = x[:, k * 128:(k + 1) * 128]
        partner = jnp.where(lower, pltpu.roll(xb, 112, axis=1), pltpu.roll(xb, 16, axis=1))
        outs.append(xb * cos + partner * sin)
    return outs[0] if len(outs) == 1 else jnp.concatenate(outs, axis=1)


def _retention_chunk(q, k, v, dmat, qdec, kdec, cdec, state_scr, masks, bd_mask):
    qb = q.astype(BF16)
    kb = k.astype(BF16)
    vb = v.astype(BF16)
    s = lax.dot_general(_stack_heads(qb, masks), kb, (((1,), (1,)), ((), ())),
                        preferred_element_type=F32)
    p = (s * dmat).astype(BF16)
    p_cat = jnp.concatenate([p[h * CHUNK:(h + 1) * CHUNK] for h in range(N_HEADS)], axis=1)
    state = state_scr[...]
    o = (jnp.dot(p_cat, _stack_heads(vb, masks), preferred_element_type=F32)
         + _bdot(q * qdec, state))
    upd = lax.dot_general((k * kdec).astype(BF16), vb, (((0,), (0,)), ((), ())),
                          preferred_element_type=F32)
    state_scr[...] = state * cdec + jnp.where(bd_mask, upd, 0.0)
    return o


def _group_norm(o, gmat):
    mu = jnp.dot(o.astype(BF16), gmat, preferred_element_type=F32)
    d = o - mu
    var = jnp.dot((d * d).astype(BF16), gmat, preferred_element_type=F32)
    return d * lax.rsqrt(var + LN_EPS)


def _mixer_kernel(*refs, latent, nc):
    if latent:
        (qa_ref, ka_ref, va_ref, kx_ref, vx_ref, ub_ref, vb_ref, qc_ref, kc_ref, vc_ref, gf_ref,
         gb_ref, pd_ref, pdp_ref, pdn_ref, cos_ref, sin_ref, s0_ref, dmat_ref, dec_ref, sink_ref,
         ws_ref, bias_ref, vec_ref, gmat_ref, cnt_ref, wpool_ref,
         mix_ref, sf_scr, sb_scr, ob_scr, pext_scr, k_scr, v_scr) = refs
    else:
        (qa_ref, kx_ref, vx_ref, ub_ref, vb_ref, qc_ref, kc_ref, vc_ref, gf_ref,
         gb_ref, pd_ref, pdp_ref, pdn_ref, dmat_ref, dec_ref, sink_ref,
         ws_ref, bias_ref, vec_ref, gmat_ref, cnt_ref, wpool_ref,
         mix_ref, st_ref, sf_scr, sb_scr, ob_scr, pext_scr) = refs

    p = pl.program_id(1)
    c = pl.program_id(2)
    masks = _head_masks(GW)
    row = lax.broadcasted_iota(jnp.int32, (GW, GW), 0)
    col = lax.broadcasted_iota(jnp.int32, (GW, GW), 1)
    bd_mask = (row // HEAD_DIM) == (col // HEAD_DIM)
    vec = vec_ref[...]

    @pl.when(p == 0)
    def _():
        rc = nc - 1 - c

        @pl.when(c == 0)
        def _():
            if latent:
                sb_scr[...] = s0_ref[1]
                zero_blk = jnp.zeros((CHUNK, 2 * HEAD_DIM), BF16)
                k_scr[0:CHUNK, :] = zero_blk
                v_scr[0:CHUNK, :] = zero_blk
                k_scr[(nc + 1) * CHUNK:(nc + 2) * CHUNK, :] = zero_blk
                v_scr[(nc + 1) * CHUNK:(nc + 2) * CHUNK, :] = zero_blk
            else:
                sb_scr[...] = jnp.zeros((GW, GW), F32)

        if latent:
            dst = pl.ds(pl.multiple_of((rc + 1) * CHUNK, CHUNK), CHUNK)
            k_scr[dst, :] = _rope(ka_ref[...], cos_ref[...], sin_ref[...]).astype(BF16)
            v_scr[dst, :] = va_ref[...].astype(BF16)

        dec = dec_ref[...]
        o_b = _retention_chunk(qc_ref[...], kc_ref[...], vc_ref[...], dmat_ref[1],
                               dec[1], dec[3], vec[6:7], sb_scr, masks, bd_mask)
        ob_scr[pl.ds(pl.multiple_of(rc * CHUNK, CHUNK), CHUNK), :] = o_b

        if not latent:
            @pl.when(c == nc - 1)
            def _():
                st_ref[1] = sb_scr[...]

    @pl.when(p == 1)
    def _():
        @pl.when(c == 0)
        def _():
            if latent:
                sf_scr[...] = s0_ref[0]
            else:
                sf_scr[...] = jnp.zeros((GW, GW), F32)

        q = qa_ref[...]
        if latent:
            q = _rope(q, cos_ref[...], sin_ref[...])
        q = q * (HEAD_DIM ** -0.5)
        lane = lax.broadcasted_iota(jnp.int32, (1, 2 * HEAD_DIM), 1)
        lo = lane < HEAD_DIM
        q0, q1 = q[:, :128], q[:, 128:]
        zero = jnp.zeros_like(q0)
        q_st = jnp.concatenate([
            jnp.where(lo, q0, zero),
            jnp.where(lo, pltpu.roll(q0, HEAD_DIM, axis=1), zero),
            jnp.where(lo, zero, pltpu.roll(q1, HEAD_DIM, axis=1)),
            jnp.where(lo, zero, q1)], axis=0).astype(BF16)
        if latent:
            band = pl.ds(pl.multiple_of(c * CHUNK, CHUNK), 3 * CHUNK)
            k_all = jnp.concatenate([k_scr[band, :], kx_ref[...].astype(BF16)], axis=0)
            v_all = jnp.concatenate([v_scr[band, :], vx_ref[...].astype(BF16)], axis=0)
        else:
            k_all = kx_ref[...].astype(BF16)
            v_all = vx_ref[...].astype(BF16)
        s = lax.dot_general(q_st, k_all, (((1,), (1,)), ((), ())), preferred_element_type=F32)
        if latent:
            nk = 3 * CHUNK + PAST_LEN
            qi = lax.broadcasted_iota(jnp.int32, (N_HEADS * CHUNK, nk), 0) & (CHUNK - 1)
            kj = lax.broadcasted_iota(jnp.int32, (N_HEADS * CHUNK, nk), 1)
            kpos = kj + (c - 1) * CHUNK
            valid = (kj >= 3 * CHUNK) | ((kj >= qi) & (kj <= qi + 2 * CHUNK)
                                         & (kpos >= 0) & (kpos < nc * CHUNK))
            s = jnp.where(valid, s, NEG_INF)
        sink = sink_ref[...][:, 0:1]
        mx = jnp.maximum(jnp.max(s, axis=-1, keepdims=True), sink)
        e = jnp.exp(s - mx)
        den = jnp.sum(e, axis=-1, keepdims=True) + jnp.exp(sink - mx)
        o = jnp.dot(e.astype(BF16), v_all, preferred_element_type=F32) / den
        mix_ref[:, 0:128] = jnp.where(lo, o[0:CHUNK], pltpu.roll(o[CHUNK:2 * CHUNK], HEAD_DIM, axis=1))
        mix_ref[:, 128:256] = jnp.where(lo, pltpu.roll(o[2 * CHUNK:3 * CHUNK], HEAD_DIM, axis=1),
                                        o[3 * CHUNK:4 * CHUNK])

        vn = _layer_norm(vb_ref[...], vec[0:1], vec[1:2]).astype(BF16)
        sg = jnp.dot(ws_ref[...], _stack_heads(vn, masks), preferred_element_type=F32) + bias_ref[...]
        mix_ref[:, GW:2 * GW] = ub_ref[...] * sg

        dec = dec_ref[...]
        o_f = _retention_chunk(qc_ref[...], kc_ref[...], vc_ref[...], dmat_ref[0],
                               dec[0], dec[2], vec[5:6], sf_scr, masks, bd_mask)
        o_b = ob_scr[pl.ds(pl.multiple_of(c * CHUNK, CHUNK), CHUNK), :]
        gmat = gmat_ref[...]
        mix_ref[:, 2 * GW:3 * GW] = (_silu(gf_ref[...]) * (_group_norm(o_f, gmat) * vec[3:4])
                                     + _silu(gb_ref[...]) * (_group_norm(o_b, gmat) * vec[4:5]))
        if not latent:
            @pl.when(c == nc - 1)
            def _():
                st_ref[0] = sf_scr[...]

        pd = pd_ref[...]
        pext_scr[0:POOL_HALO, :] = jnp.where(c > 0, pdp_ref[...], 0.0)
        pext_scr[POOL_HALO:POOL_HALO + CHUNK, :] = pd
        pext_scr[POOL_HALO + CHUNK:2 * POOL_HALO + CHUNK, :] = jnp.where(c < nc - 1, pdn_ref[...], 0.0)

        def win(d, half):
            return pext_scr[pl.ds(POOL_HALO + d, CHUNK), half * 128:(half + 1) * 128]

        a2 = win(-1, 0) + win(0, 0)
        a4 = a2 + win(-2, 0) + win(1, 0)
        a8 = win(-4, 1)
        for d in range(-3, 4):
            a8 = a8 + win(d, 1)
        a16 = a8
        for d in list(range(-8, -4)) + list(range(4, 8)):
            a16 = a16 + win(d, 1)
        sums = jnp.concatenate([jnp.where(lo, a2, a4), jnp.where(lo, a8, a16)], axis=1)
        yd = sums * cnt_ref[...] - pd
        mix_ref[:, 3 * GW:4 * GW] = _bdot(yd, wpool_ref[...]) * vec[2:3]


def _mixer(z, tabs, latent, extra=None):
    nb = DEC_BATCH if latent else BATCH
    nc = (DEC_SEQ if latent else SEQ) // CHUNK
    base = (N_CTX // CHUNK) if latent else 0
    last_halo = ROWS // POOL_HALO - 1
    per8 = CHUNK // POOL_HALO

    def fwd(b, p, c):
        return base + b * nc + c * p

    def both(b, p, c):
        return base + b * nc + jnp.where(p == 0, nc - 1 - c, c)

    def col(width, idx, rowmap):
        return pl.BlockSpec((CHUNK, width), lambda b, p, c: (rowmap(b, p, c), idx))

    def const(shape):
        return pl.BlockSpec(shape, lambda b, p, c: (0,) * len(shape))

    specs, args = [], []

    def add(spec, arr):
        specs.append(spec)
        args.append(arr)

    add(col(GW, 0, fwd), z)
    if latent:
        add(pl.BlockSpec((CHUNK, 128), lambda b, p, c: (base + b * nc + (nc - 1 - c) * (1 - p), 2)), z)
        add(pl.BlockSpec((CHUNK, 128), lambda b, p, c: (base + b * nc + (nc - 1 - c) * (1 - p), 3)), z)
        add(pl.BlockSpec((None, PAST_LEN, 128), lambda b, p, c: (b, 0, 0)), extra["ck"])
        add(pl.BlockSpec((None, PAST_LEN, 128), lambda b, p, c: (b, 0, 0)), extra["cv"])
    else:
        add(pl.BlockSpec((SEQ, 128), lambda b, p, c: (b, 2)), z)
        add(pl.BlockSpec((SEQ, 128), lambda b, p, c: (b, 3)), z)
    add(col(GW, 2, fwd), z)
    add(col(GW, 3, fwd), z)
    add(col(GW, 4, both), z)
    add(col(GW, 5, both), z)
    add(col(GW, 6, both), z)
    add(col(GW, 7, fwd), z)
    add(col(GW, 8, fwd), z)
    add(col(GW, 9, fwd), z)
    add(pl.BlockSpec((POOL_HALO, GW),
                     lambda b, p, c: (jnp.maximum(fwd(b, p, c) * per8 - 1, 0), 9 * GW // GW)), z)
    add(pl.BlockSpec((POOL_HALO, GW),
                     lambda b, p, c: (jnp.minimum((fwd(b, p, c) + 1) * per8, last_halo), 9)), z)
    if latent:
        rope_map = lambda b, p, c: (jnp.where(p == 0, nc - 1 - c, c), 0)
        add(pl.BlockSpec((CHUNK, 128), rope_map), extra["cos"])
        add(pl.BlockSpec((CHUNK, 128), rope_map), extra["sin"])
        add(pl.BlockSpec((None, 2, GW, GW), lambda b, p, c: (b, 0, 0, 0)), extra["s0"])
    add(const((2, N_HEADS * CHUNK, CHUNK)), tabs["dmat"])
    add(const((4, CHUNK, GW)), tabs["dec"])
    add(const((N_HEADS * CHUNK, 128)), tabs["sink"])
    add(const((CHUNK, N_HEADS * CHUNK)), tabs["ws"])
    add(const((CHUNK, GW)), tabs["bias"])
    add(const((8, GW)), tabs["vec"])
    add(const((GW, GW)), tabs["gmat"])
    add(pl.BlockSpec((CHUNK, GW), lambda b, p, c: (c * p, 0)), extra["cnt"] if latent else tabs["cnt_ctx"])
    add(const((GW, GW)), tabs["wpool"])

    out_shape = [jax.ShapeDtypeStruct((nb * nc * CHUNK, D_MODEL), F32)]
    out_specs = [pl.BlockSpec((CHUNK, D_MODEL), lambda b, p, c: (b * nc + c * p, 0))]
    scratch = [pltpu.VMEM((GW, GW), F32), pltpu.VMEM((GW, GW), F32),
               pltpu.VMEM((nc * CHUNK, GW), F32),
               pltpu.VMEM((CHUNK + 2 * POOL_HALO, GW), F32)]
    if latent:
        scratch += [pltpu.VMEM(((nc + 2) * CHUNK, 128), BF16), pltpu.VMEM(((nc + 2) * CHUNK, 128), BF16)]
    else:
        out_shape.append(jax.ShapeDtypeStruct((nb, 2, GW, GW), F32))
        out_specs.append(pl.BlockSpec((None, 2, GW, GW), lambda b, p, c: (b, 0, 0, 0)))

    return pl.pallas_call(
        functools.partial(_mixer_kernel, latent=latent, nc=nc),
        grid=(nb, 2, nc),
        in_specs=specs,
        out_specs=out_specs,
        out_shape=out_shape,
        scratch_shapes=scratch,
        compiler_params=pltpu.CompilerParams(
            dimension_semantics=("arbitrary", "arbitrary", "arbitrary"), vmem_limit_bytes=VMEM_LIMIT),
        name="mixer_latent" if latent else "mixer_context",
    )(*args)


def _pad_rows(rows, n=8):
    a = jnp.stack(rows)
    return jnp.concatenate([a, jnp.zeros((n - a.shape[0],) + a.shape[1:], a.dtype)], axis=0)


def _block_diag(blocks):
    g, n, _ = blocks.shape
    eye = jnp.eye(g, dtype=blocks.dtype)
    return (eye[:, None, :, None] * blocks[:, :, None, :]).reshape(g * n, g * n)


def _inv_count(n):
    t = np.arange(n)
    cols = []
    for w in POOL_WINDOWS:
        cnt = np.clip(t + w // 2, 0, n) - np.clip(t - w // 2, 0, n)
        cols.append(np.repeat((1.0 / cnt)[:, None], HEAD_DIM, axis=1))
    return jnp.asarray(np.concatenate(cols, axis=1), F32)


def _rope_tables():
    rows = DEC_SEQ // GRID_W
    r, cc = jnp.meshgrid(jnp.arange(rows), jnp.arange(GRID_W), indexing="ij")
    half = HEAD_DIM // 2
    freqs = ROPE_BASE ** (-jnp.arange(0, half, 2, dtype=F32) / half)
    quarter = half // 2

    def tables(pos):
        ang = pos.reshape(-1).astype(F32)[:, None] * freqs[None, :]
        cos, sin = jnp.cos(ang), jnp.sin(ang)
        return jnp.concatenate([cos, cos], axis=1), jnp.concatenate([-sin, sin], axis=1)

    cr, sr = tables(r)
    ccol, scol = tables(cc)
    cos = jnp.concatenate([cr, ccol], axis=1)
    sin = jnp.concatenate([sr, scol], axis=1)
    del quarter
    return jnp.tile(cos, (1, 2)), jnp.tile(sin, (1, 2))


def _layer_tables(l, attn_sink, sgu_norm_w, sgu_norm_b, sgu_ws, sgu_bs, ret_decay, ret_gn_w,
                  pool_w, pool_scale):
    log_g = jax.nn.log_sigmoid(ret_decay[l].astype(F32))
    i = jnp.arange(CHUNK, dtype=F32)
    rel = i[:, None] - i[None, :]
    kscale = HEAD_DIM ** -0.5
    d_f = jnp.where(rel >= 0, jnp.exp(jnp.maximum(rel, 0.0)[None] * log_g[0][:, None, None]), 0.0)
    d_b = jnp.where(rel <= 0, jnp.exp(jnp.maximum(-rel, 0.0)[None] * log_g[1][:, None, None]), 0.0)
    dmat = jnp.stack([d_f.reshape(N_HEADS * CHUNK, CHUNK), d_b.reshape(N_HEADS * CHUNK, CHUNK)]) * kscale

    def lanes(per_head):
        return jnp.repeat(per_head, HEAD_DIM, axis=1)

    qdec_f = lanes(jnp.exp((i + 1.0)[:, None] * log_g[0][None, :]))
    qdec_b = lanes(jnp.exp((CHUNK - i)[:, None] * log_g[1][None, :]))
    kdec_f = lanes(jnp.exp((CHUNK - 1.0 - i)[:, None] * log_g[0][None, :])) * kscale
    kdec_b = lanes(jnp.exp(i[:, None] * log_g[1][None, :])) * kscale
    cdec = jnp.repeat(jnp.exp(CHUNK * log_g), HEAD_DIM, axis=1)
    vec = _pad_rows([sgu_norm_w[l], sgu_norm_b[l], pool_scale[l], ret_gn_w[l, 0], ret_gn_w[l, 1],
                     cdec[0], cdec[1]])
    return {
        "dmat": dmat,
        "dec": jnp.stack([qdec_f, qdec_b, kdec_f, kdec_b]),
        "sink": jnp.broadcast_to(jnp.repeat(attn_sink[l], CHUNK)[:, None], (N_HEADS * CHUNK, 128)),
        "ws": jnp.concatenate([sgu_ws[l, h] for h in range(N_HEADS)], axis=1).astype(BF16),
        "bias": jnp.repeat(sgu_bs[l].T, HEAD_DIM, axis=1),
        "vec": vec,
        "gmat": _block_diag(jnp.full((N_HEADS, HEAD_DIM, HEAD_DIM), 1.0 / HEAD_DIM, F32)).astype(BF16),
        "cnt_ctx": _inv_count(SEQ),
        "wpool": _block_diag(pool_w[l]).astype(BF16),
    }


def kernel(x_prompt, x_sample, cache_attn_k, cache_attn_v, state_ret, c, c_ctx, w_ada, b_ada, w_in,
           w_out, attn_sink, sgu_norm_w, sgu_norm_b, sgu_ws, sgu_bs, ret_decay, ret_gn_w, pool_w,
           pool_scale, ffn_up, ffn_conv_w, ffn_conv_b, ffn_down, ln_w, ln_b):
    x = jnp.concatenate([x_prompt.reshape(N_CTX, D_MODEL), x_sample.reshape(N_LAT, D_MODEL)], axis=0)

    cond8 = jnp.concatenate([c_ctx[None], c, jnp.zeros((8 - 1 - DEC_BATCH, D_MODEL), F32)], axis=0)
    mod = _modulation(cond8, w_ada, b_ada).reshape(DEPTH, 8, 6, D_MODEL)[:, :1 + DEC_BATCH]
    mod = jnp.concatenate([mod, jnp.zeros((DEPTH, 1 + DEC_BATCH, 2, D_MODEL), F32)], axis=2)

    cos, sin = _rope_tables()
    cnt_lat = _inv_count(DEC_SEQ)
    new_k, new_v, new_s = [], [], []
    for l in range(DEPTH):
        tabs = _layer_tables(l, attn_sink, sgu_norm_w, sgu_norm_b, sgu_ws, sgu_bs, ret_decay, ret_gn_w,
                             pool_w, pool_scale)
        z = _inproj(x, mod[l], w_in[l])
        mix_ctx, st = _mixer(z, tabs, latent=False)
        s0 = jax.vmap(jax.vmap(_block_diag))(state_ret[:, l].astype(F32))
        extra = {"ck": cache_attn_k[:, l].reshape(DEC_BATCH, PAST_LEN, 128),
                 "cv": cache_attn_v[:, l].reshape(DEC_BATCH, PAST_LEN, 128),
                 "cos": cos, "sin": sin, "s0": s0, "cnt": cnt_lat}
        (mix_lat,) = _mixer(z, tabs, latent=True, extra=extra)
        mix = jnp.concatenate([mix_ctx, mix_lat], axis=0)
        ln8 = _pad_rows([ln_w[l, 0], ln_b[l, 0], ln_w[l, 1], ln_b[l, 1]])
        x = _outproj(mix, x, mod[l], w_out[l], ln8)
        conv8 = _pad_rows([ffn_conv_w[l, 0], ffn_conv_w[l, 1], ffn_conv_w[l, 2], ffn_conv_b[l]])
        x = _ffn(x, mod[l], ffn_up[l], conv8, ffn_down[l], ln8)

        new_k.append(z[:N_CTX, GW:GW + 128].reshape(BATCH, SEQ, 2, HEAD_DIM))
        new_v.append(z[:N_CTX, GW + 128:GW + 256].reshape(BATCH, SEQ, 2, HEAD_DIM))
        st = st.reshape(BATCH, 2, N_HEADS, HEAD_DIM, N_HEADS, HEAD_DIM)
        new_s.append(jnp.stack([st[:, :, h, :, h, :] for h in range(N_HEADS)], axis=2))

    y_prompt = x[:N_CTX].reshape(BATCH, SEQ, D_MODEL)
    y_sample = x[N_CTX:].reshape(DEC_BATCH, DEC_SEQ, D_MODEL)
    return (y_prompt, y_sample, jnp.stack(new_k, axis=1), jnp.stack(new_v, axis=1),
            jnp.stack(new_s, axis=1))
```

```python
import functools

import numpy as np
import jax
import jax.numpy as jnp
from jax import lax
from jax.experimental import pallas as pl
from jax.experimental.pallas import tpu as pltpu

F32 = jnp.float32
BF16 = jnp.bfloat16

D_MODEL = 1024
BATCH = 16
SEQ = 256
DEPTH = 2
DEC_BATCH = 2
DEC_SEQ = 2048
PAST_LEN = 256
GRID_W = 64
CHUNK = 128
HEAD_DIM = 64
GW = D_MODEL // 4
N_HEADS = 4
POOL_WINDOWS = (2, 4, 8, 16)
POOL_HALO = 8
D_FF = 2816
ROPE_BASE = 10000.0
LN_EPS = 1e-5
NEG_INF = -1e30
IN_WIDTH = 10 * GW
ALPHA = (2.0 * DEPTH) ** 0.25

N_CTX = BATCH * SEQ
N_LAT = DEC_BATCH * DEC_SEQ
ROWS = N_CTX + N_LAT

TM = 1024
CTX_TILES = N_CTX // TM
LAT_TILES_PER_SEQ = DEC_SEQ // TM
NB_IN = 512
FC = 256
NB_ADA = 1536
VMEM_LIMIT = 56 * 1024 * 1024


def _cond_of_tile(i):
    return jnp.where(i < CTX_TILES, 0, 1 + (i - CTX_TILES) // LAT_TILES_PER_SEQ)


def _layer_norm(x, w, b):
    mu = jnp.mean(x, axis=-1, keepdims=True)
    d = x - mu
    var = jnp.mean(d * d, axis=-1, keepdims=True)
    return d * lax.rsqrt(var + LN_EPS) * w + b


def _silu(x):
    return x * jax.nn.sigmoid(x)


def _bdot(a, b):
    return jnp.dot(a.astype(BF16), b.astype(BF16), preferred_element_type=F32)


def _mod_kernel(c_ref, w_ref, b_ref, o_ref):
    o_ref[...] = _bdot(_silu(c_ref[...]), w_ref[...]) + b_ref[...]


def _modulation(cond8, w_ada, b_ada):
    return pl.pallas_call(
        _mod_kernel,
        grid=(DEPTH, 6 * D_MODEL // NB_ADA),
        in_specs=[
            pl.BlockSpec((8, D_MODEL), lambda l, j: (0, 0)),
            pl.BlockSpec((None, D_MODEL, NB_ADA), lambda l, j: (l, 0, j)),
            pl.BlockSpec((None, 1, NB_ADA), lambda l, j: (l, 0, j)),
        ],
        out_specs=pl.BlockSpec((None, 8, NB_ADA), lambda l, j: (l, 0, j)),
        out_shape=jax.ShapeDtypeStruct((DEPTH, 8, 6 * D_MODEL), F32),
        compiler_params=pltpu.CompilerParams(
            dimension_semantics=("arbitrary", "arbitrary"), vmem_limit_bytes=VMEM_LIMIT),
        name="modulation",
    )(cond8, w_ada, b_ada.reshape(DEPTH, 1, 6 * D_MODEL))


def _inproj_kernel(x_ref, mod_ref, w_ref, z_ref, h_scr):
    @pl.when(pl.program_id(1) == 0)
    def _():
        m = mod_ref[...]
        h_scr[...] = (x_ref[...] * (1.0 + m[1:2]) + m[0:1]).astype(BF16)

    z_ref[...] = jnp.dot(h_scr[...], w_ref[...].astype(BF16), preferred_element_type=F32)


def _inproj(x, mod_l, w_in_l):
    return pl.pallas_call(
        _inproj_kernel,
        grid=(ROWS // TM, IN_WIDTH // NB_IN),
        in_specs=[
            pl.BlockSpec((TM, D_MODEL), lambda i, j: (i, 0)),
            pl.BlockSpec((None, 8, D_MODEL), lambda i, j: (_cond_of_tile(i), 0, 0)),
            pl.BlockSpec((D_MODEL, NB_IN), lambda i, j: (0, j)),
        ],
        out_specs=pl.BlockSpec((TM, NB_IN), lambda i, j: (i, j)),
        out_shape=jax.ShapeDtypeStruct((ROWS, IN_WIDTH), F32),
        scratch_shapes=[pltpu.VMEM((TM, D_MODEL), BF16)],
        compiler_params=pltpu.CompilerParams(
            dimension_semantics=("arbitrary", "arbitrary"), vmem_limit_bytes=VMEM_LIMIT),
        name="inproj",
    )(x, mod_l, w_in_l)


def _outproj_kernel(mix_ref, x_ref, mod_ref, w_ref, ln_ref, o_ref):
    y = _bdot(mix_ref[...], w_ref[...])
    m = mod_ref[...]
    ln = ln_ref[...]
    o_ref[...] = _layer_norm(ALPHA * x_ref[...] + m[2:3] * y, ln[0:1], ln[1:2])


def _outproj(mix, x, mod_l, w_out_l, ln8):
    return pl.pallas_call(
        _outproj_kernel,
        grid=(ROWS // TM,),
        in_specs=[
            pl.BlockSpec((TM, D_MODEL), lambda i: (i, 0)),
            pl.BlockSpec((TM, D_MODEL), lambda i: (i, 0)),
            pl.BlockSpec((None, 8, D_MODEL), lambda i: (_cond_of_tile(i), 0, 0)),
            pl.BlockSpec((D_MODEL, D_MODEL), lambda i: (0, 0)),
            pl.BlockSpec((8, D_MODEL), lambda i: (0, 0)),
        ],
        out_specs=pl.BlockSpec((TM, D_MODEL), lambda i: (i, 0)),
        out_shape=jax.ShapeDtypeStruct((ROWS, D_MODEL), F32),
        compiler_params=pltpu.CompilerParams(
            dimension_semantics=("arbitrary",), vmem_limit_bytes=VMEM_LIMIT),
        name="outproj",
    )(mix, x, mod_l, w_out_l, ln8)


SEG = TM // 8
HALO_ROWS = 16
RB = 256
NFC = D_FF // FC
LANE_BLOCKS = D_MODEL // 128
SEG_PITCH = SEG + 8


def _seg_rows(xc_ref, k):
    return jnp.concatenate([xc_ref[cb, pl.ds(k, 8, stride=SEG_PITCH), :] for cb in range(LANE_BLOCKS)], axis=1)


def _ffn_kernel(x_ref, xp_ref, xn_ref, mod_ref, upa_ref, upg_ref, cva_ref, cvg_ref, dn_ref,
                ln_ref, o_ref, h_scr, act_scr, dn_scr, xc_scr):
    i = pl.program_id(0)
    j = pl.program_id(1)
    is_ctx = i < CTX_TILES
    lat_pos = (i - CTX_TILES) % LAT_TILES_PER_SEQ

    @pl.when((i == 0) & (j == 0))
    def _():
        dn_scr[...] = dn_ref[...].astype(BF16)

    @pl.when(j == 0)
    def _():
        m = mod_ref[...]
        scale = 1.0 + m[4:5]
        shift = m[3:4]
        for cb in range(LANE_BLOCKS):
            for s in range(8):
                xc_scr[cb, s * SEG_PITCH:s * SEG_PITCH + SEG, :] = x_ref[s * SEG:(s + 1) * SEG,
                                                                         cb * 128:(cb + 1) * 128]
        for k in range(0, SEG, 2):
            rows = jnp.concatenate([_seg_rows(xc_scr, k), _seg_rows(xc_scr, k + 1)], axis=0)
            h_scr[8 * k:8 * k + 16, :] = (rows * scale + shift).astype(BF16)
        sub = lax.broadcasted_iota(jnp.int32, (HALO_ROWS, D_MODEL), 0)
        prev_ok = jnp.logical_not(is_ctx) & (lat_pos > 0)
        next_ok = jnp.logical_not(is_ctx) & (lat_pos < LAT_TILES_PER_SEQ - 1)
        halo_x = jnp.where(sub == 0, xp_ref[POOL_HALO - 1:POOL_HALO, :], xn_ref[0:1, :])
        keep = ((sub == 0) & prev_ok) | ((sub == 1) & next_ok)
        h_scr[TM:TM + HALO_ROWS, :] = jnp.where(keep, halo_x * scale + shift, 0.0).astype(BF16)

    h = h_scr[...]
    sub = lax.broadcasted_iota(jnp.int32, (8, FC), 0)
    seg_per_seq = SEQ // SEG
    ctx_first = is_ctx & (sub % seg_per_seq == 0)
    ctx_last = is_ctx & (sub % seg_per_seq == seg_per_seq - 1)

    def conv(ue, cv):
        u = ue[0:TM]
        b_first = jnp.where(sub == 0, ue[TM:TM + 1], pltpu.roll(u[TM - 8:TM], 1, axis=0))
        b_first = jnp.where(ctx_first, 0.0, b_first)
        b_last = jnp.where(sub == 7, ue[TM + 1:TM + 2], pltpu.roll(u[0:8], 7, axis=0))
        b_last = jnp.where(ctx_last, 0.0, b_last)
        um1 = jnp.concatenate([b_first, u[0:TM - 8]], axis=0)
        up1 = jnp.concatenate([u[8:TM], b_last], axis=0)
        return um1 * cv[0:1] + u * cv[1:2] + up1 * cv[2:3] + cv[3:4]

    a = conv(jnp.dot(h, upa_ref[...].astype(BF16), preferred_element_type=F32), cva_ref[...])
    g = conv(jnp.dot(h, upg_ref[...].astype(BF16), preferred_element_type=F32), cvg_ref[...])
    act_scr[j] = (_silu(a) * g).astype(BF16)

    @pl.when(j == NFC - 1)
    def _():
        m = mod_ref[...]
        ln = ln_ref[...]
        for rb in range(TM // RB):
            vrows = range(rb * RB // 8, (rb + 1) * RB // 8)
            lhs = jnp.concatenate([act_scr[jj, rb * RB:(rb + 1) * RB, :] for jj in range(NFC)], axis=1)
            y = jnp.dot(lhs, dn_scr[...], preferred_element_type=F32)
            xr = jnp.concatenate([_seg_rows(xc_scr, k) for k in vrows], axis=0)
            out = _layer_norm(ALPHA * xr + m[5:6] * y, ln[2:3], ln[3:4])
            for kk, k in enumerate(vrows):
                for cb in range(LANE_BLOCKS):
                    xc_scr[cb, pl.ds(k, 8, stride=SEG_PITCH), :] = out[8 * kk:8 * kk + 8,
                                                                       cb * 128:(cb + 1) * 128]
        for cb in range(LANE_BLOCKS):
            for s in range(8):
                o_ref[s * SEG:(s + 1) * SEG, cb * 128:(cb + 1) * 128] = xc_scr[cb, s * SEG_PITCH:
                                                                               s * SEG_PITCH + SEG, :]


def _ffn(x, mod_l, up_l, conv8, down_l, ln8):
    halo_blocks = TM // POOL_HALO
    last_halo = ROWS // POOL_HALO - 1
    return pl.pallas_call(
        _ffn_kernel,
        grid=(ROWS // TM, NFC),
        in_specs=[
            pl.BlockSpec((TM, D_MODEL), lambda i, j: (i, 0)),
            pl.BlockSpec((POOL_HALO, D_MODEL), lambda i, j: (jnp.maximum(i * halo_blocks - 1, 0), 0)),
            pl.BlockSpec((POOL_HALO, D_MODEL),
                         lambda i, j: (jnp.minimum((i + 1) * halo_blocks, last_halo), 0)),
            pl.BlockSpec((None, 8, D_MODEL), lambda i, j: (_cond_of_tile(i), 0, 0)),
            pl.BlockSpec((D_MODEL, FC), lambda i, j: (0, j)),
            pl.BlockSpec((D_MODEL, FC), lambda i, j: (0, NFC + j)),
            pl.BlockSpec((8, FC), lambda i, j: (0, j)),
            pl.BlockSpec((8, FC), lambda i, j: (0, NFC + j)),
            pl.BlockSpec((D_FF, D_MODEL), lambda i, j: (0, 0), pipeline_mode=pl.Buffered(1)),
            pl.BlockSpec((8, D_MODEL), lambda i, j: (0, 0)),
        ],
        out_specs=pl.BlockSpec((TM, D_MODEL), lambda i, j: (i, 0)),
        out_shape=jax.ShapeDtypeStruct((ROWS, D_MODEL), F32),
        scratch_shapes=[pltpu.VMEM((TM + HALO_ROWS, D_MODEL), BF16),
                        pltpu.VMEM((NFC, TM, FC), BF16),
                        pltpu.VMEM((D_FF, D_MODEL), BF16),
                        pltpu.VMEM((LANE_BLOCKS, 8 * SEG_PITCH, 128), F32)],
        compiler_params=pltpu.CompilerParams(
            dimension_semantics=("arbitrary", "arbitrary"), vmem_limit_bytes=VMEM_LIMIT),
        name="convffn",
    )(x, x, x, mod_l, up_l, up_l, conv8, conv8, down_l, ln8)


def _head_masks(width):
    lane = lax.broadcasted_iota(jnp.int32, (1, width), 1)
    return [(lane >= h * HEAD_DIM) & (lane < (h + 1) * HEAD_DIM) for h in range(width // HEAD_DIM)]


def _stack_heads(x, masks):
    return jnp.concatenate([jnp.where(m, x, jnp.zeros_like(x)) for m in masks], axis=0)


def _rope(x, cos, sin):
    lane = lax.broadcasted_iota(jnp.int32, (1, 128), 1)
    lower = (lane & 31) < 16
    outs = []
    for k in range(x.shape[1] // 128):
        xb = x[:, k * 128:(k + 1) * 128]
        partner = jnp.where(lower, pltpu.roll(xb, 112, axis=1), pltpu.roll(xb, 16, axis=1))
        outs.append(xb * cos + partner * sin)
    return outs[0] if len(outs) == 1 else jnp.concatenate(outs, axis=1)


def _retention_chunk(q, k, v, dmat, qdec, kdec, cdec, state_scr, masks, bd_mask):
    qb = q.astype(BF16)
    kb = k.astype(BF16)
    vb = v.astype(BF16)
    s = lax.dot_general(_stack_heads(qb, masks), kb, (((1,), (1,)), ((), ())),
                        preferred_element_type=F32)
    p = (s * dmat).astype(BF16)
    p_cat = jnp.concatenate([p[h * CHUNK:(h + 1) * CHUNK] for h in range(N_HEADS)], axis=1)
    state = state_scr[...]
    o = (jnp.dot(p_cat, _stack_heads(vb, masks), preferred_element_type=F32)
         + _bdot(q * qdec, state))
    upd = lax.dot_general((k * kdec).astype(BF16), vb, (((0,), (0,)), ((), ())),
                          preferred_element_type=F32)
    state_scr[...] = state * cdec + jnp.where(bd_mask, upd, 0.0)
    return o


def _group_norm(o, gmat):
    mu = jnp.dot(o.astype(BF16), gmat, preferred_element_type=F32)
    d = o - mu
    var = jnp.dot((d * d).astype(BF16), gmat, preferred_element_type=F32)
    return d * lax.rsqrt(var + LN_EPS)


def _mixer_kernel(*refs, latent, nc):
    if latent:
        (qa_ref, ka_ref, va_ref, kx_ref, vx_ref, ub_ref, vb_ref, qc_ref, kc_ref, vc_ref, gf_ref,
         gb_ref, pd_ref, pdp_ref, pdn_ref, cos_ref, sin_ref, s0_ref, dmat_ref, dec_ref, sink_ref,
         ws_ref, bias_ref, vec_ref, gmat_ref, cnt_ref, wpool_ref,
         mix_ref, sf_scr, sb_scr, ob_scr, pext_scr, k_scr, v_scr) = refs
    else:
        (qa_ref, kx_ref, vx_ref, ub_ref, vb_ref, qc_ref, kc_ref, vc_ref, gf_ref,
         gb_ref, pd_ref, pdp_ref, pdn_ref, dmat_ref, dec_ref, sink_ref,
         ws_ref, bias_ref, vec_ref, gmat_ref, cnt_ref, wpool_ref,
         mix_ref, st_ref, sf_scr, sb_scr, ob_scr, pext_scr) = refs

    p = pl.program_id(1)
    c = pl.program_id(2)
    masks = _head_masks(GW)
    row = lax.broadcasted_iota(jnp.int32, (GW, GW), 0)
    col = lax.broadcasted_iota(jnp.int32, (GW, GW), 1)
    bd_mask = (row // HEAD_DIM) == (col // HEAD_DIM)
    vec = vec_ref[...]

    @pl.when(p == 0)
    def _():
        rc = nc - 1 - c

        @pl.when(c == 0)
        def _():
            if latent:
                sb_scr[...] = s0_ref[1]
                zero_blk = jnp.zeros((CHUNK, 2 * HEAD_DIM), BF16)
                k_scr[0:CHUNK, :] = zero_blk
                v_scr[0:CHUNK, :] = zero_blk
                k_scr[(nc + 1) * CHUNK:(nc + 2) * CHUNK, :] = zero_blk
                v_scr[(nc + 1) * CHUNK:(nc + 2) * CHUNK, :] = zero_blk
            else:
                sb_scr[...] = jnp.zeros((GW, GW), F32)

        if latent:
            dst = pl.ds(pl.multiple_of((rc + 1) * CHUNK, CHUNK), CHUNK)
            k_scr[dst, :] = _rope(ka_ref[...], cos_ref[...], sin_ref[...]).astype(BF16)
            v_scr[dst, :] = va_ref[...].astype(BF16)

        dec = dec_ref[...]
        o_b = _retention_chunk(qc_ref[...], kc_ref[...], vc_ref[...], dmat_ref[1],
                               dec[1], dec[3], vec[6:7], sb_scr, masks, bd_mask)
        ob_scr[pl.ds(pl.multiple_of(rc * CHUNK, CHUNK), CHUNK), :] = o_b

        if not latent:
            @pl.when(c == nc - 1)
            def _():
                st_ref[1] = sb_scr[...]

    @pl.when(p == 1)
    def _():
        @pl.when(c == 0)
        def _():
            if latent:
                sf_scr[...] = s0_ref[0]
            else:
                sf_scr[...] = jnp.zeros((GW, GW), F32)

        q = qa_ref[...]
        if latent:
            q = _rope(q, cos_ref[...], sin_ref[...])
        q = q * (HEAD_DIM ** -0.5)
        lane = lax.broadcasted_iota(jnp.int32, (1, 2 * HEAD_DIM), 1)
        lo = lane < HEAD_DIM
        q0, q1 = q[:, :128], q[:, 128:]
        zero = jnp.zeros_like(q0)
        q_st = jnp.concatenate([
            jnp.where(lo, q0, zero),
            jnp.where(lo, pltpu.roll(q0, HEAD_DIM, axis=1), zero),
            jnp.where(lo, zero, pltpu.roll(q1, HEAD_DIM, axis=1)),
            jnp.where(lo, zero, q1)], axis=0).astype(BF16)
        if latent:
            band = pl.ds(pl.multiple_of(c * CHUNK, CHUNK), 3 * CHUNK)
            k_all = jnp.concatenate([k_scr[band, :], kx_ref[...].astype(BF16)], axis=0)
            v_all = jnp.concatenate([v_scr[band, :], vx_ref[...].astype(BF16)], axis=0)
        else:
            k_all = kx_ref[...].astype(BF16)
            v_all = vx_ref[...].astype(BF16)
        s = lax.dot_general(q_st, k_all, (((1,), (1,)), ((), ())), preferred_element_type=F32)
        if latent:
            nk = 3 * CHUNK + PAST_LEN
            qi = lax.broadcasted_iota(jnp.int32, (N_HEADS * CHUNK, nk), 0) & (CHUNK - 1)
            kj = lax.broadcasted_iota(jnp.int32, (N_HEADS * CHUNK, nk), 1)
            kpos = kj + (c - 1) * CHUNK
            valid = (kj >= 3 * CHUNK) | ((kj >= qi) & (kj <= qi + 2 * CHUNK)
                                         & (kpos >= 0) & (kpos < nc * CHUNK))
            s = jnp.where(valid, s, NEG_INF)
        sink = sink_ref[...][:, 0:1]
        mx = jnp.maximum(jnp.max(s, axis=-1, keepdims=True), sink)
        e = jnp.exp(s - mx)
        den = jnp.sum(e, axis=-1, keepdims=True) + jnp.exp(sink - mx)
        o = jnp.dot(e.astype(BF16), v_all, preferred_element_type=F32) / den
        mix_ref[:, 0:128] = jnp.where(lo, o[0:CHUNK], pltpu.roll(o[CHUNK:2 * CHUNK], HEAD_DIM, axis=1))
        mix_ref[:, 128:256] = jnp.where(lo, pltpu.roll(o[2 * CHUNK:3 * CHUNK], HEAD_DIM, axis=1),
                                        o[3 * CHUNK:4 * CHUNK])

        vn = _layer_norm(vb_ref[...], vec[0:1], vec[1:2]).astype(BF16)
        sg = jnp.dot(ws_ref[...], _stack_heads(vn, masks), preferred_element_type=F32) + bias_ref[...]
        mix_ref[:, GW:2 * GW] = ub_ref[...] * sg

        dec = dec_ref[...]
        o_f = _retention_chunk(qc_ref[...], kc_ref[...], vc_ref[...], dmat_ref[0],
                               dec[0], dec[2], vec[5:6], sf_scr, masks, bd_mask)
        o_b = ob_scr[pl.ds(pl.multiple_of(c * CHUNK, CHUNK), CHUNK), :]
        gmat = gmat_ref[...]
        mix_ref[:, 2 * GW:3 * GW] = (_silu(gf_ref[...]) * (_group_norm(o_f, gmat) * vec[3:4])
                                     + _silu(gb_ref[...]) * (_group_norm(o_b, gmat) * vec[4:5]))
        if not latent:
            @pl.when(c == nc - 1)
            def _():
                st_ref[0] = sf_scr[...]

        pd = pd_ref[...]
        pext_scr[0:POOL_HALO, :] = jnp.where(c > 0, pdp_ref[...], 0.0)
        pext_scr[POOL_HALO:POOL_HALO + CHUNK, :] = pd
        pext_scr[POOL_HALO + CHUNK:2 * POOL_HALO + CHUNK, :] = jnp.where(c < nc - 1, pdn_ref[...], 0.0)

        def win(d, half):
            return pext_scr[pl.ds(POOL_HALO + d, CHUNK), half * 128:(half + 1) * 128]

        a2 = win(-1, 0) + win(0, 0)
        a4 = a2 + win(-2, 0) + win(1, 0)
        a8 = win(-4, 1)
        for d in range(-3, 4):
            a8 = a8 + win(d, 1)
        a16 = a8
        for d in list(range(-8, -4)) + list(range(4, 8)):
            a16 = a16 + win(d, 1)
        sums = jnp.concatenate([jnp.where(lo, a2, a4), jnp.where(lo, a8, a16)], axis=1)
        yd = sums * cnt_ref[...] - pd
        mix_ref[:, 3 * GW:4 * GW] = _bdot(yd, wpool_ref[...]) * vec[2:3]


def _mixer(z, tabs, latent, extra=None):
    nb = DEC_BATCH if latent else BATCH
    nc = (DEC_SEQ if latent else SEQ) // CHUNK
    base = (N_CTX // CHUNK) if latent else 0
    last_halo = ROWS // POOL_HALO - 1
    per8 = CHUNK // POOL_HALO

    def fwd(b, p, c):
        return base + b * nc + c * p

    def both(b, p, c):
        return base + b * nc + jnp.where(p == 0, nc - 1 - c, c)

    def col(width, idx, rowmap):
        return pl.BlockSpec((CHUNK, width), lambda b, p, c: (rowmap(b, p, c), idx))

    def const(shape):
        return pl.BlockSpec(shape, lambda b, p, c: (0,) * len(shape))

    specs, args = [], []

    def add(spec, arr):
        specs.append(spec)
        args.append(arr)

    add(col(GW, 0, fwd), z)
    if latent:
        add(pl.BlockSpec((CHUNK, 128), lambda b, p, c: (base + b * nc + (nc - 1 - c) * (1 - p), 2)), z)
        add(pl.BlockSpec((CHUNK, 128), lambda b, p, c: (base + b * nc + (nc - 1 - c) * (1 - p), 3)), z)
        add(pl.BlockSpec((None, PAST_LEN, 128), lambda b, p, c: (b, 0, 0)), extra["ck"])
        add(pl.BlockSpec((None, PAST_LEN, 128), lambda b, p, c: (b, 0, 0)), extra["cv"])
    else:
        add(pl.BlockSpec((SEQ, 128), lambda b, p, c: (b, 2)), z)
        add(pl.BlockSpec((SEQ, 128), lambda b, p, c: (b, 3)), z)
    add(col(GW, 2, fwd), z)
    add(col(GW, 3, fwd), z)
    add(col(GW, 4, both), z)
    add(col(GW, 5, both), z)
    add(col(GW, 6, both), z)
    add(col(GW, 7, fwd), z)
    add(col(GW, 8, fwd), z)
    add(col(GW, 9, fwd), z)
    add(pl.BlockSpec((POOL_HALO, GW),
                     lambda b, p, c: (jnp.maximum(fwd(b, p, c) * per8 - 1, 0), 9)), z)
    add(pl.BlockSpec((POOL_HALO, GW),
                     lambda b, p, c: (jnp.minimum((fwd(b, p, c) + 1) * per8, last_halo), 9)), z)
    if latent:
        rope_map = lambda b, p, c: (jnp.where(p == 0, nc - 1 - c, c), 0)
        add(pl.BlockSpec((CHUNK, 128), rope_map), extra["cos"])
        add(pl.BlockSpec((CHUNK, 128), rope_map), extra["sin"])
        add(pl.BlockSpec((None, 2, GW, GW), lambda b, p, c: (b, 0, 0, 0)), extra["s0"])
    add(const((2, N_HEADS * CHUNK, CHUNK)), tabs["dmat"])
    add(const((4, CHUNK, GW)), tabs["dec"])
    add(const((N_HEADS * CHUNK, 128)), tabs["sink"])
    add(const((CHUNK, N_HEADS * CHUNK)), tabs["ws"])
    add(const((CHUNK, GW)), tabs["bias"])
    add(const((8, GW)), tabs["vec"])
    add(const((GW, GW)), tabs["gmat"])
    add(pl.BlockSpec((CHUNK, GW), lambda b, p, c: (c * p, 0)), extra["cnt"] if latent else tabs["cnt_ctx"])
    add(const((GW, GW)), tabs["wpool"])

    out_shape = [jax.ShapeDtypeStruct((nb * nc * CHUNK, D_MODEL), F32)]
    out_specs = [pl.BlockSpec((CHUNK, D_MODEL), lambda b, p, c: (b * nc + c * p, 0))]
    scratch = [pltpu.VMEM((GW, GW), F32), pltpu.VMEM((GW, GW), F32),
               pltpu.VMEM((nc * CHUNK, GW), F32),
               pltpu.VMEM((CHUNK + 2 * POOL_HALO, GW), F32)]
    if latent:
        scratch += [pltpu.VMEM(((nc + 2) * CHUNK, 128), BF16), pltpu.VMEM(((nc + 2) * CHUNK, 128), BF16)]
    else:
        out_shape.append(jax.ShapeDtypeStruct((nb, 2, GW, GW), F32))
        out_specs.append(pl.BlockSpec((None, 2, GW, GW), lambda b, p, c: (b, 0, 0, 0)))

    return pl.pallas_call(
        functools.partial(_mixer_kernel, latent=latent, nc=nc),
        grid=(nb, 2, nc),
        in_specs=specs,
        out_specs=out_specs,
        out_shape=out_shape,
        scratch_shapes=scratch,
        compiler_params=pltpu.CompilerParams(
            dimension_semantics=("arbitrary", "arbitrary", "arbitrary"), vmem_limit_bytes=VMEM_LIMIT),
        name="mixer_latent" if latent else "mixer_context",
    )(*args)


def _pad_rows(rows, n=8):
    a = jnp.stack(rows)
    return jnp.concatenate([a, jnp.zeros((n - a.shape[0],) + a.shape[1:], a.dtype)], axis=0)


def _block_diag(blocks):
    g, n, _ = blocks.shape
    eye = jnp.eye(g, dtype=blocks.dtype)
    return (eye[:, None, :, None] * blocks[:, :, None, :]).reshape(g * n, g * n)


def _inv_count(n):
    t = np.arange(n)
    cols = []
    for w in POOL_WINDOWS:
        cnt = np.clip(t + w // 2, 0, n) - np.clip(t - w // 2, 0, n)
        cols.append(np.repeat((1.0 / cnt)[:, None], HEAD_DIM, axis=1))
    return jnp.asarray(np.concatenate(cols, axis=1), F32)


def _rope_tables():
    rows = DEC_SEQ // GRID_W
    r, cc = jnp.meshgrid(jnp.arange(rows), jnp.arange(GRID_W), indexing="ij")
    half = HEAD_DIM // 2
    freqs = ROPE_BASE ** (-jnp.arange(0, half, 2, dtype=F32) / half)

    def tables(pos):
        ang = pos.reshape(-1).astype(F32)[:, None] * freqs[None, :]
        cos, sin = jnp.cos(ang), jnp.sin(ang)
        return jnp.concatenate([cos, cos], axis=1), jnp.concatenate([-sin, sin], axis=1)

    cr, sr = tables(r)
    ccol, scol = tables(cc)
    cos = jnp.concatenate([cr, ccol], axis=1)
    sin = jnp.concatenate([sr, scol], axis=1)
    return jnp.tile(cos, (1, 2)), jnp.tile(sin, (1, 2))


def _layer_tables(l, attn_sink, sgu_norm_w, sgu_norm_b, sgu_ws, sgu_bs, ret_decay, ret_gn_w,
                  pool_w, pool_scale):
    log_g = jax.nn.log_sigmoid(ret_decay[l].astype(F32))
    i = jnp.arange(CHUNK, dtype=F32)
    rel = i[:, None] - i[None, :]
    kscale = HEAD_DIM ** -0.5
    d_f = jnp.where(rel >= 0, jnp.exp(jnp.maximum(rel, 0.0)[None] * log_g[0][:, None, None]), 0.0)
    d_b = jnp.where(rel <= 0, jnp.exp(jnp.maximum(-rel, 0.0)[None] * log_g[1][:, None, None]), 0.0)
    dmat = jnp.stack([d_f.reshape(N_HEADS * CHUNK, CHUNK), d_b.reshape(N_HEADS * CHUNK, CHUNK)]) * kscale

    def lanes(per_head):
        return jnp.repeat(per_head, HEAD_DIM, axis=1)

    qdec_f = lanes(jnp.exp((i + 1.0)[:, None] * log_g[0][None, :]))
    qdec_b = lanes(jnp.exp((CHUNK - i)[:, None] * log_g[1][None, :]))
    kdec_f = lanes(jnp.exp((CHUNK - 1.0 - i)[:, None] * log_g[0][None, :])) * kscale
    kdec_b = lanes(jnp.exp(i[:, None] * log_g[1][None, :])) * kscale
    cdec = jnp.repeat(jnp.exp(CHUNK * log_g), HEAD_DIM, axis=1)
    vec = _pad_rows([sgu_norm_w[l], sgu_norm_b[l], pool_scale[l], ret_gn_w[l, 0], ret_gn_w[l, 1],
                     cdec[0], cdec[1]])
    return {
        "dmat": dmat,
        "dec": jnp.stack([qdec_f, qdec_b, kdec_f, kdec_b]),
        "sink": jnp.broadcast_to(jnp.repeat(attn_sink[l], CHUNK)[:, None], (N_HEADS * CHUNK, 128)),
        "ws": jnp.concatenate([sgu_ws[l, h] for h in range(N_HEADS)], axis=1).astype(BF16),
        "bias": jnp.repeat(sgu_bs[l].T, HEAD_DIM, axis=1),
        "vec": vec,
        "gmat": _block_diag(jnp.full((N_HEADS, HEAD_DIM, HEAD_DIM), 1.0 / HEAD_DIM, F32)).astype(BF16),
        "cnt_ctx": _inv_count(SEQ),
        "wpool": _block_diag(pool_w[l]).astype(BF16),
    }


def kernel(x_prompt, x_sample, cache_attn_k, cache_attn_v, state_ret, c, c_ctx, w_ada, b_ada, w_in,
           w_out, attn_sink, sgu_norm_w, sgu_norm_b, sgu_ws, sgu_bs, ret_decay, ret_gn_w, pool_w,
           pool_scale, ffn_up, ffn_conv_w, ffn_conv_b, ffn_down, ln_w, ln_b):
    x = jnp.concatenate([x_prompt.reshape(N_CTX, D_MODEL), x_sample.reshape(N_LAT, D_MODEL)], axis=0)

    cond8 = jnp.concatenate([c_ctx[None], c, jnp.zeros((8 - 1 - DEC_BATCH, D_MODEL), F32)], axis=0)
    mod = _modulation(cond8, w_ada, b_ada).reshape(DEPTH, 8, 6, D_MODEL)[:, :1 + DEC_BATCH]
    mod = jnp.concatenate([mod, jnp.zeros((DEPTH, 1 + DEC_BATCH, 2, D_MODEL), F32)], axis=2)

    cos, sin = _rope_tables()
    cnt_lat = _inv_count(DEC_SEQ)
    new_k, new_v, new_s = [], [], []
    for l in range(DEPTH):
        tabs = _layer_tables(l, attn_sink, sgu_norm_w, sgu_norm_b, sgu_ws, sgu_bs, ret_decay, ret_gn_w,
                             pool_w, pool_scale)
        z = _inproj(x, mod[l], w_in[l])
        mix_ctx, st = _mixer(z, tabs, latent=False)
        s0 = jax.vmap(jax.vmap(_block_diag))(state_ret[:, l].astype(F32))
        extra = {"ck": cache_attn_k[:, l].reshape(DEC_BATCH, PAST_LEN, 128),
                 "cv": cache_attn_v[:, l].reshape(DEC_BATCH, PAST_LEN, 128),
                 "cos": cos, "sin": sin, "s0": s0, "cnt": cnt_lat}
        (mix_lat,) = _mixer(z, tabs, latent=True, extra=extra)
        mix = jnp.concatenate([mix_ctx, mix_lat], axis=0)
        ln8 = _pad_rows([ln_w[l, 0], ln_b[l, 0], ln_w[l, 1], ln_b[l, 1]])
        x = _outproj(mix, x, mod[l], w_out[l], ln8)
        conv8 = _pad_rows([ffn_conv_w[l, 0], ffn_conv_w[l, 1], ffn_conv_w[l, 2], ffn_conv_b[l]])
        x = _ffn(x, mod[l], ffn_up[l], conv8, ffn_down[l], ln8)

        new_k.append(z[:N_CTX, GW:GW + 128].reshape(BATCH, SEQ, 2, HEAD_DIM))
        new_v.append(z[:N_CTX, GW + 128:GW + 256].reshape(BATCH, SEQ, 2, HEAD_DIM))
        st = st.reshape(BATCH, 2, N_HEADS, HEAD_DIM, N_HEADS, HEAD_DIM)
        new_s.append(jnp.stack([st[:, :, h, :, h, :] for h in range(N_HEADS)], axis=2))

    y_prompt = x[:N_CTX].reshape(BATCH, SEQ, D_MODEL)
    y_sample = x[N_CTX:].reshape(DEC_BATCH, DEC_SEQ, D_MODEL)
    return (y_prompt, y_sample, jnp.stack(new_k, axis=1), jnp.stack(new_v, axis=1),
            jnp.stack(new_s, axis=1))
```

```python
import functools

import numpy as np
import jax
import jax.numpy as jnp
from jax import lax
from jax.experimental import pallas as pl
from jax.experimental.pallas import tpu as pltpu

F32 = jnp.float32
BF16 = jnp.bfloat16

D_MODEL = 1024
BATCH = 16
SEQ = 256
DEPTH = 2
DEC_BATCH = 2
DEC_SEQ = 2048
PAST_LEN = 256
GRID_W = 64
CHUNK = 128
HEAD_DIM = 64
GW = D_MODEL // 4
N_HEADS = 4
POOL_WINDOWS = (2, 4, 8, 16)
POOL_HALO = 8
D_FF = 2816
ROPE_BASE = 10000.0
LN_EPS = 1e-5
NEG_INF = -1e30
IN_WIDTH = 10 * GW
ALPHA = (2.0 * DEPTH) ** 0.25

N_CTX = BATCH * SEQ
N_LAT = DEC_BATCH * DEC_SEQ
ROWS = N_CTX + N_LAT

TM = 1024
CTX_TILES = N_CTX // TM
LAT_TILES_PER_SEQ = DEC_SEQ // TM
NB_IN = 512
FC = 256
NB_ADA = 1536
VMEM_LIMIT = 56 * 1024 * 1024


def _cond_of_tile(i, tm=TM):
    ctx_tiles = N_CTX // tm
    return jnp.where(i < ctx_tiles, 0, 1 + (i - ctx_tiles) // (DEC_SEQ // tm))


def _tile_specs(n_src, tm, grid_rank):
    ctx_tiles = N_CTX // tm
    if n_src == 1:
        rows = [lambda i: i]
    else:
        rows = [lambda i: jnp.minimum(i, ctx_tiles - 1), lambda i: jnp.maximum(i - ctx_tiles, 0)]
    if grid_rank == 1:
        return [pl.BlockSpec((tm, D_MODEL), lambda i, f=f: (f(i), 0)) for f in rows]
    return [pl.BlockSpec((tm, D_MODEL), lambda i, j, f=f: (f(i), 0)) for f in rows]


def _per_half(i, tm, n_max, fn):
    if n_max == 1:
        fn(0)
        return
    ctx_tiles = N_CTX // tm

    @pl.when(i < ctx_tiles)
    def _():
        fn(0)

    @pl.when(i >= ctx_tiles)
    def _():
        fn(1)


def _layer_norm(x, w, b):
    mu = jnp.mean(x, axis=-1, keepdims=True)
    d = x - mu
    var = jnp.mean(d * d, axis=-1, keepdims=True)
    return d * lax.rsqrt(var + LN_EPS) * w + b


def _silu(x):
    return x * jax.nn.sigmoid(x)


def _bdot(a, b):
    return jnp.dot(a.astype(BF16), b.astype(BF16), preferred_element_type=F32)


def _mod_kernel(c_ref, w_ref, b_ref, o_ref):
    o_ref[...] = _bdot(_silu(c_ref[...]), w_ref[...]) + b_ref[...]


def _modulation(cond8, w_ada, b_ada):
    return pl.pallas_call(
        _mod_kernel,
        grid=(DEPTH, 6 * D_MODEL // NB_ADA),
        in_specs=[
            pl.BlockSpec((8, D_MODEL), lambda l, j: (0, 0)),
            pl.BlockSpec((None, D_MODEL, NB_ADA), lambda l, j: (l, 0, j)),
            pl.BlockSpec((None, 1, NB_ADA), lambda l, j: (l, 0, j)),
        ],
        out_specs=pl.BlockSpec((None, 8, NB_ADA), lambda l, j: (l, 0, j)),
        out_shape=jax.ShapeDtypeStruct((DEPTH, 8, 6 * D_MODEL), F32),
        compiler_params=pltpu.CompilerParams(
            dimension_semantics=("arbitrary", "arbitrary"), vmem_limit_bytes=VMEM_LIMIT),
        name="modulation",
    )(cond8, w_ada, b_ada.reshape(DEPTH, 1, 6 * D_MODEL))


CAST_ROWS = 704


def _cast_kernel(w_ref, o_ref):
    o_ref[...] = w_ref[...].astype(BF16)


def _to_bf16(w):
    depth, rows, cols = w.shape
    return pl.pallas_call(
        _cast_kernel,
        grid=(depth, rows // CAST_ROWS),
        in_specs=[pl.BlockSpec((None, CAST_ROWS, cols), lambda l, r: (l, r, 0))],
        out_specs=pl.BlockSpec((None, CAST_ROWS, cols), lambda l, r: (l, r, 0)),
        out_shape=jax.ShapeDtypeStruct(w.shape, BF16),
        compiler_params=pltpu.CompilerParams(dimension_semantics=("arbitrary", "arbitrary")),
        name="cast_bf16",
    )(w)


def _inproj_kernel(*refs, n_x):
    x_refs = refs[:n_x]
    mod_ref, w_ref, z_ref, h_scr = refs[n_x:]

    @pl.when(pl.program_id(1) == 0)
    def _():
        m = mod_ref[...]

        def build(side):
            h_scr[...] = (x_refs[side][...] * (1.0 + m[1:2]) + m[0:1]).astype(BF16)

        _per_half(pl.program_id(0), TM, n_x, build)

    z_ref[...] = jnp.dot(h_scr[...], w_ref[...].astype(BF16), preferred_element_type=F32)


def _inproj(xs, mod, w_in, l):
    return pl.pallas_call(
        functools.partial(_inproj_kernel, n_x=len(xs)),
        grid=(ROWS // TM, IN_WIDTH // NB_IN),
        in_specs=_tile_specs(len(xs), TM, 2) + [
            pl.BlockSpec((None, None, 6, D_MODEL), lambda i, j: (l, _cond_of_tile(i), 0, 0)),
            pl.BlockSpec((None, D_MODEL, NB_IN), lambda i, j: (l, 0, j)),
        ],
        out_specs=pl.BlockSpec((TM, NB_IN), lambda i, j: (i, j)),
        out_shape=jax.ShapeDtypeStruct((ROWS, IN_WIDTH), F32),
        scratch_shapes=[pltpu.VMEM((TM, D_MODEL), BF16)],
        compiler_params=pltpu.CompilerParams(
            dimension_semantics=("arbitrary", "arbitrary"), vmem_limit_bytes=VMEM_LIMIT),
        name="inproj",
    )(*xs, mod, w_in)


TM_OUT = 512


def _outproj_kernel(*refs, n_x):
    mix_refs = refs[:2]
    x_refs = refs[2:2 + n_x]
    mod_ref, w_ref, ln_ref, o_ref = refs[2 + n_x:]
    m = mod_ref[...]
    ln = ln_ref[...]

    def body(side):
        y = _bdot(mix_refs[side][...], w_ref[...])
        x = x_refs[min(side, n_x - 1)][...]
        o_ref[...] = _layer_norm(ALPHA * x + m[2:3] * y, ln[0:1], ln[2:3])

    _per_half(pl.program_id(0), TM_OUT, 2, body)


def _outproj(mixes, xs, mod, w_out, ln8, l):
    return pl.pallas_call(
        functools.partial(_outproj_kernel, n_x=len(xs)),
        grid=(ROWS // TM_OUT,),
        in_specs=_tile_specs(2, TM_OUT, 1) + _tile_specs(len(xs), TM_OUT, 1) + [
            pl.BlockSpec((None, None, 6, D_MODEL), lambda i: (l, _cond_of_tile(i, TM_OUT), 0, 0)),
            pl.BlockSpec((None, D_MODEL, D_MODEL), lambda i: (l, 0, 0)),
            pl.BlockSpec((None, 8, D_MODEL), lambda i: (l, 0, 0)),
        ],
        out_specs=pl.BlockSpec((TM_OUT, D_MODEL), lambda i: (i, 0)),
        out_shape=jax.ShapeDtypeStruct((ROWS, D_MODEL), F32),
        compiler_params=pltpu.CompilerParams(
            dimension_semantics=("arbitrary",), vmem_limit_bytes=VMEM_LIMIT),
        name="outproj",
    )(*mixes, *xs, mod, w_out, ln8)


SEG = TM // 8
HALO_ROWS = 16
RB = 256
NFC = D_FF // FC
LANE_BLOCKS = D_MODEL // 128
SEG_PITCH = SEG + 8


def _seg_rows(xc_ref, k):
    return jnp.concatenate([xc_ref[cb, pl.ds(k, 8, stride=SEG_PITCH), :] for cb in range(LANE_BLOCKS)], axis=1)


def _ffn_kernel(*refs, n_out):
    (x_ref, xp_ref, xn_ref, mod_ref, upa_ref, upg_ref, cva_ref, cvg_ref, dn_ref, ln_ref) = refs[:10]
    o_refs = refs[10:10 + n_out]
    h_scr, act_scr, xc_scr = refs[10 + n_out:]
    i = pl.program_id(0)
    j = pl.program_id(1)
    is_ctx = i < CTX_TILES
    lat_pos = (i - CTX_TILES) % LAT_TILES_PER_SEQ

    @pl.when(j == 0)
    def _():
        m = mod_ref[...]
        scale = 1.0 + m[4:5]
        shift = m[3:4]
        for cb in range(LANE_BLOCKS):
            for s in range(8):
                xc_scr[cb, s * SEG_PITCH:s * SEG_PITCH + SEG, :] = x_ref[s * SEG:(s + 1) * SEG,
                                                                         cb * 128:(cb + 1) * 128]
        for k in range(0, SEG, 2):
            rows = jnp.concatenate([_seg_rows(xc_scr, k), _seg_rows(xc_scr, k + 1)], axis=0)
            h_scr[8 * k:8 * k + 16, :] = (rows * scale + shift).astype(BF16)
        sub = lax.broadcasted_iota(jnp.int32, (HALO_ROWS, D_MODEL), 0)
        prev_ok = jnp.logical_not(is_ctx) & (lat_pos > 0)
        next_ok = jnp.logical_not(is_ctx) & (lat_pos < LAT_TILES_PER_SEQ - 1)
        halo_x = jnp.where(sub == 0, xp_ref[POOL_HALO - 1:POOL_HALO, :], xn_ref[0:1, :])
        keep = ((sub == 0) & prev_ok) | ((sub == 1) & next_ok)
        h_scr[TM:TM + HALO_ROWS, :] = jnp.where(keep, halo_x * scale + shift, 0.0).astype(BF16)

    h = h_scr[...]
    sub = lax.broadcasted_iota(jnp.int32, (8, FC), 0)
    seg_per_seq = SEQ // SEG
    ctx_first = is_ctx & (sub % seg_per_seq == 0)
    ctx_last = is_ctx & (sub % seg_per_seq == seg_per_seq - 1)

    def conv(ue, cv):
        u = ue[0:TM]
        b_first = jnp.where(sub == 0, ue[TM:TM + 1], pltpu.roll(u[TM - 8:TM], 1, axis=0))
        b_first = jnp.where(ctx_first, 0.0, b_first)
        b_last = jnp.where(sub == 7, ue[TM + 1:TM + 2], pltpu.roll(u[0:8], 7, axis=0))
        b_last = jnp.where(ctx_last, 0.0, b_last)
        um1 = jnp.concatenate([b_first, u[0:TM - 8]], axis=0)
        up1 = jnp.concatenate([u[8:TM], b_last], axis=0)
        return um1 * cv[0:1] + u * cv[1:2] + up1 * cv[2:3] + cv[3:4]

    a = conv(jnp.dot(h, upa_ref[...].astype(BF16), preferred_element_type=F32), cva_ref[...])
    g = conv(jnp.dot(h, upg_ref[...].astype(BF16), preferred_element_type=F32), cvg_ref[...])
    act_scr[j] = (_silu(a) * g).astype(BF16)

    @pl.when(j == NFC - 1)
    def _():
        m = mod_ref[...]
        ln = ln_ref[...]
        for rb in range(TM // RB):
            vrows = range(rb * RB // 8, (rb + 1) * RB // 8)
            lhs = jnp.concatenate([act_scr[jj, rb * RB:(rb + 1) * RB, :] for jj in range(NFC)], axis=1)
            y = jnp.dot(lhs, dn_ref[...], preferred_element_type=F32)
            xr = jnp.concatenate([_seg_rows(xc_scr, k) for k in vrows], axis=0)
            out = _layer_norm(ALPHA * xr + m[5:6] * y, ln[1:2], ln[3:4])
            for kk, k in enumerate(vrows):
                for cb in range(LANE_BLOCKS):
                    xc_scr[cb, pl.ds(k, 8, stride=SEG_PITCH), :] = out[8 * kk:8 * kk + 8,
                                                                       cb * 128:(cb + 1) * 128]
        def write_out(side):
            for cb in range(LANE_BLOCKS):
                for s in range(8):
                    o_refs[side][s * SEG:(s + 1) * SEG, cb * 128:(cb + 1) * 128] = xc_scr[
                        cb, s * SEG_PITCH:s * SEG_PITCH + SEG, :]

        _per_half(i, TM, n_out, write_out)


def _ffn(x, mod, up, conv8, down, ln8, l, split_out):
    halo_blocks = TM // POOL_HALO
    last_halo = ROWS // POOL_HALO - 1
    if split_out:
        out_specs = _tile_specs(2, TM, 2)
        out_shape = [jax.ShapeDtypeStruct((N_CTX, D_MODEL), F32), jax.ShapeDtypeStruct((N_LAT, D_MODEL), F32)]
    else:
        out_specs = _tile_specs(1, TM, 2)
        out_shape = [jax.ShapeDtypeStruct((ROWS, D_MODEL), F32)]
    return pl.pallas_call(
        functools.partial(_ffn_kernel, n_out=len(out_shape)),
        grid=(ROWS // TM, NFC),
        in_specs=[
            pl.BlockSpec((TM, D_MODEL), lambda i, j: (i, 0)),
            pl.BlockSpec((POOL_HALO, D_MODEL), lambda i, j: (jnp.maximum(i * halo_blocks - 1, 0), 0)),
            pl.BlockSpec((POOL_HALO, D_MODEL),
                         lambda i, j: (jnp.minimum((i + 1) * halo_blocks, last_halo), 0)),
            pl.BlockSpec((None, None, 6, D_MODEL), lambda i, j: (l, _cond_of_tile(i), 0, 0)),
            pl.BlockSpec((None, D_MODEL, FC), lambda i, j: (l, 0, j)),
            pl.BlockSpec((None, D_MODEL, FC), lambda i, j: (l, 0, NFC + j)),
            pl.BlockSpec((None, 8, FC), lambda i, j: (l, 0, j)),
            pl.BlockSpec((None, 8, FC), lambda i, j: (l, 0, NFC + j)),
            pl.BlockSpec((None, D_FF, D_MODEL), lambda i, j: (l, 0, 0), pipeline_mode=pl.Buffered(1)),
            pl.BlockSpec((None, 8, D_MODEL), lambda i, j: (l, 0, 0)),
        ],
        out_specs=out_specs,
        out_shape=out_shape,
        scratch_shapes=[pltpu.VMEM((TM + HALO_ROWS, D_MODEL), BF16),
                        pltpu.VMEM((NFC, TM, FC), BF16),
                        pltpu.VMEM((LANE_BLOCKS, 8 * SEG_PITCH, 128), F32)],
        compiler_params=pltpu.CompilerParams(
            dimension_semantics=("arbitrary", "arbitrary"), vmem_limit_bytes=VMEM_LIMIT),
        name="convffn",
    )(x, x, x, mod, up, up, conv8, conv8, down, ln8)


def _head_masks(width):
    lane = lax.broadcasted_iota(jnp.int32, (1, width), 1)
    return [(lane >= h * HEAD_DIM) & (lane < (h + 1) * HEAD_DIM) for h in range(width // HEAD_DIM)]


def _stack_heads(x, masks):
    return jnp.concatenate([jnp.where(m, x, jnp.zeros_like(x)) for m in masks], axis=0)


def _rope(x, cos, sin):
    lane = lax.broadcasted_iota(jnp.int32, (1, 128), 1)
    lower = (lane & 31) < 16
    outs = []
    for k in range(x.shape[1] // 128):
        xb = x[:, k * 128:(k + 1) * 128]
        partner = jnp.where(lower, pltpu.roll(xb, 112, axis=1), pltpu.roll(xb, 16, axis=1))
        outs.append(xb * cos + partner * sin)
    return outs[0] if len(outs) == 1 else jnp.concatenate(outs, axis=1)


def _retention_chunk(q, k, v, dmat, qdec, kdec, cdec, state_scr, masks, bd_mask):
    qb = q.astype(BF16)
    kb = k.astype(BF16)
    vb = v.astype(BF16)
    s = lax.dot_general(_stack_heads(qb, masks), kb, (((1,), (1,)), ((), ())),
                        preferred_element_type=F32)
    p = (s * dmat).astype(BF16)
    p_cat = jnp.concatenate([p[h * CHUNK:(h + 1) * CHUNK] for h in range(N_HEADS)], axis=1)
    state = state_scr[...]
    o = (jnp.dot(p_cat, _stack_heads(vb, masks), preferred_element_type=F32)
         + _bdot(q * qdec, state))
    upd = lax.dot_general((k * kdec).astype(BF16), vb, (((0,), (0,)), ((), ())),
                          preferred_element_type=F32)
    state_scr[...] = state * cdec + jnp.where(bd_mask, upd, 0.0)
    return o


def _load_state(state_scr, blocks_ref, d):
    state_scr[...] = jnp.zeros((GW, GW), F32)
    for h in range(N_HEADS):
        sl = slice(h * HEAD_DIM, (h + 1) * HEAD_DIM)
        state_scr[sl, sl] = blocks_ref[d, h]


def _store_state(blocks_ref, d, state_scr):
    for h in range(N_HEADS):
        sl = slice(h * HEAD_DIM, (h + 1) * HEAD_DIM)
        blocks_ref[d, h] = state_scr[sl, sl]


def _group_norm(o, gmat):
    mu = jnp.dot(o.astype(BF16), gmat, preferred_element_type=F32)
    d = o - mu
    var = jnp.dot((d * d).astype(BF16), gmat, preferred_element_type=F32)
    return d * lax.rsqrt(var + LN_EPS)


def _mixer_kernel(*refs, latent, nc):
    if latent:
        (qa_ref, ka_ref, va_ref, kx_ref, vx_ref, ub_ref, vb_ref, qc_ref, kc_ref, vc_ref, gf_ref,
         gb_ref, pd_ref, pdp_ref, pdn_ref, cos_ref, sin_ref, s0_ref, dmat_ref, dec_ref, sink_ref,
         ws_ref, bias_ref, vec_ref, gmat_ref, cnt_ref, wpool_ref,
         mix_ref, sf_scr, sb_scr, ob_scr, pext_scr, k_scr, v_scr) = refs
    else:
        (qa_ref, kx_ref, vx_ref, ub_ref, vb_ref, qc_ref, kc_ref, vc_ref, gf_ref,
         gb_ref, pd_ref, pdp_ref, pdn_ref, dmat_ref, dec_ref, sink_ref,
         ws_ref, bias_ref, vec_ref, gmat_ref, cnt_ref, wpool_ref,
         mix_ref, st_ref, kn_ref, vn_ref, sf_scr, sb_scr, ob_scr, pext_scr) = refs

    p = pl.program_id(1)
    c = pl.program_id(2)
    masks = _head_masks(GW)
    row = lax.broadcasted_iota(jnp.int32, (GW, GW), 0)
    col = lax.broadcasted_iota(jnp.int32, (GW, GW), 1)
    bd_mask = (row // HEAD_DIM) == (col // HEAD_DIM)
    vec = vec_ref[...]

    @pl.when(p == 0)
    def _():
        rc = nc - 1 - c

        @pl.when(c == 0)
        def _():
            if latent:
                _load_state(sb_scr, s0_ref, 1)
                zero_blk = jnp.zeros((CHUNK, 2 * HEAD_DIM), BF16)
                k_scr[0:CHUNK, :] = zero_blk
                v_scr[0:CHUNK, :] = zero_blk
                k_scr[(nc + 1) * CHUNK:(nc + 2) * CHUNK, :] = zero_blk
                v_scr[(nc + 1) * CHUNK:(nc + 2) * CHUNK, :] = zero_blk
            else:
                sb_scr[...] = jnp.zeros((GW, GW), F32)

        if latent:
            dst = pl.ds(pl.multiple_of((rc + 1) * CHUNK, CHUNK), CHUNK)
            k_scr[dst, :] = _rope(ka_ref[...], cos_ref[...], sin_ref[...]).astype(BF16)
            v_scr[dst, :] = va_ref[...].astype(BF16)

        dec = dec_ref[...]
        o_b = _retention_chunk(qc_ref[...], kc_ref[...], vc_ref[...], dmat_ref[1],
                               dec[1], dec[3], vec[6:7], sb_scr, masks, bd_mask)
        ob_scr[pl.ds(pl.multiple_of(rc * CHUNK, CHUNK), CHUNK), :] = o_b

        if not latent:
            @pl.when(c == nc - 1)
            def _():
                _store_state(st_ref, 1, sb_scr)

    @pl.when(p == 1)
    def _():
        @pl.when(c == 0)
        def _():
            if latent:
                _load_state(sf_scr, s0_ref, 0)
            else:
                sf_scr[...] = jnp.zeros((GW, GW), F32)
                kn_ref[...] = kx_ref[...]
                vn_ref[...] = vx_ref[...]

        q = qa_ref[...]
        if latent:
            q = _rope(q, cos_ref[...], sin_ref[...])
        q = q * (HEAD_DIM ** -0.5)
        lane = lax.broadcasted_iota(jnp.int32, (1, 2 * HEAD_DIM), 1)
        lo = lane < HEAD_DIM
        q0, q1 = q[:, :128], q[:, 128:]
        zero = jnp.zeros_like(q0)
        q_st = jnp.concatenate([
            jnp.where(lo, q0, zero),
            jnp.where(lo, pltpu.roll(q0, HEAD_DIM, axis=1), zero),
            jnp.where(lo, zero, pltpu.roll(q1, HEAD_DIM, axis=1)),
            jnp.where(lo, zero, q1)], axis=0).astype(BF16)
        if latent:
            band = pl.ds(pl.multiple_of(c * CHUNK, CHUNK), 3 * CHUNK)
            k_all = jnp.concatenate([k_scr[band, :], kx_ref[...].astype(BF16)], axis=0)
            v_all = jnp.concatenate([v_scr[band, :], vx_ref[...].astype(BF16)], axis=0)
        else:
            k_all = kx_ref[...].astype(BF16)
            v_all = vx_ref[...].astype(BF16)
        s = lax.dot_general(q_st, k_all, (((1,), (1,)), ((), ())), preferred_element_type=F32)
        if latent:
            nk = 3 * CHUNK + PAST_LEN
            qi = lax.broadcasted_iota(jnp.int32, (N_HEADS * CHUNK, nk), 0) & (CHUNK - 1)
            kj = lax.broadcasted_iota(jnp.int32, (N_HEADS * CHUNK, nk), 1)
            kpos = kj + (c - 1) * CHUNK
            valid = (kj >= 3 * CHUNK) | ((kj >= qi) & (kj <= qi + 2 * CHUNK)
                                         & (kpos >= 0) & (kpos < nc * CHUNK))
            s = jnp.where(valid, s, NEG_INF)
        sink = sink_ref[...][:, 0:1]
        mx = jnp.maximum(jnp.max(s, axis=-1, keepdims=True), sink)
        e = jnp.exp(s - mx)
        den = jnp.sum(e, axis=-1, keepdims=True) + jnp.exp(sink - mx)
        o = jnp.dot(e.astype(BF16), v_all, preferred_element_type=F32) / den
        mix_ref[:, 0:128] = jnp.where(lo, o[0:CHUNK], pltpu.roll(o[CHUNK:2 * CHUNK], HEAD_DIM, axis=1))
        mix_ref[:, 128:256] = jnp.where(lo, pltpu.roll(o[2 * CHUNK:3 * CHUNK], HEAD_DIM, axis=1),
                                        o[3 * CHUNK:4 * CHUNK])

        vn = _layer_norm(vb_ref[...], vec[0:1], vec[1:2]).astype(BF16)
        sg = jnp.dot(ws_ref[...], _stack_heads(vn, masks), preferred_element_type=F32) + bias_ref[...]
        mix_ref[:, GW:2 * GW] = ub_ref[...] * sg

        dec = dec_ref[...]
        o_f = _retention_chunk(qc_ref[...], kc_ref[...], vc_ref[...], dmat_ref[0],
                               dec[0], dec[2], vec[5:6], sf_scr, masks, bd_mask)
        o_b = ob_scr[pl.ds(pl.multiple_of(c * CHUNK, CHUNK), CHUNK), :]
        gmat = gmat_ref[...]
        mix_ref[:, 2 * GW:3 * GW] = (_silu(gf_ref[...]) * (_group_norm(o_f, gmat) * vec[3:4])
                                     + _silu(gb_ref[...]) * (_group_norm(o_b, gmat) * vec[4:5]))
        if not latent:
            @pl.when(c == nc - 1)
            def _():
                _store_state(st_ref, 0, sf_scr)

        pd = pd_ref[...]
        pext_scr[0:POOL_HALO, :] = jnp.where(c > 0, pdp_ref[...], 0.0)
        pext_scr[POOL_HALO:POOL_HALO + CHUNK, :] = pd
        pext_scr[POOL_HALO + CHUNK:2 * POOL_HALO + CHUNK, :] = jnp.where(c < nc - 1, pdn_ref[...], 0.0)

        def win(d, half):
            return pext_scr[pl.ds(POOL_HALO + d, CHUNK), half * 128:(half + 1) * 128]

        a2 = win(-1, 0) + win(0, 0)
        a4 = a2 + win(-2, 0) + win(1, 0)
        a8 = win(-4, 1)
        for d in range(-3, 4):
            a8 = a8 + win(d, 1)
        a16 = a8
        for d in list(range(-8, -4)) + list(range(4, 8)):
            a16 = a16 + win(d, 1)
        sums = jnp.concatenate([jnp.where(lo, a2, a4), jnp.where(lo, a8, a16)], axis=1)
        yd = sums * cnt_ref[...] - pd
        mix_ref[:, 3 * GW:4 * GW] = _bdot(yd, wpool_ref[...]) * vec[2:3]


def _mixer(z, tabs, l, latent, extra=None):
    nb = DEC_BATCH if latent else BATCH
    nc = (DEC_SEQ if latent else SEQ) // CHUNK
    base = (N_CTX // CHUNK) if latent else 0
    last_halo = ROWS // POOL_HALO - 1
    per8 = CHUNK // POOL_HALO

    def fwd(b, p, c):
        return base + b * nc + c * p

    def both(b, p, c):
        return base + b * nc + jnp.where(p == 0, nc - 1 - c, c)

    def col(width, idx, rowmap):
        return pl.BlockSpec((CHUNK, width), lambda b, p, c: (rowmap(b, p, c), idx))

    def const(shape):
        return pl.BlockSpec(shape, lambda b, p, c: (0,) * len(shape))

    def layer(shape):
        return pl.BlockSpec((None,) + shape, lambda b, p, c: (l,) + (0,) * len(shape))

    specs, args = [], []

    def add(spec, arr):
        specs.append(spec)
        args.append(arr)

    add(col(GW, 0, fwd), z)
    if latent:
        add(pl.BlockSpec((CHUNK, 128), lambda b, p, c: (base + b * nc + (nc - 1 - c) * (1 - p), 2)), z)
        add(pl.BlockSpec((CHUNK, 128), lambda b, p, c: (base + b * nc + (nc - 1 - c) * (1 - p), 3)), z)
        add(pl.BlockSpec((None, None, PAST_LEN, 128), lambda b, p, c: (b, l, 0, 0)), extra["ck"])
        add(pl.BlockSpec((None, None, PAST_LEN, 128), lambda b, p, c: (b, l, 0, 0)), extra["cv"])
    else:
        add(pl.BlockSpec((SEQ, 128), lambda b, p, c: (b, 2)), z)
        add(pl.BlockSpec((SEQ, 128), lambda b, p, c: (b, 3)), z)
    add(col(GW, 2, fwd), z)
    add(col(GW, 3, fwd), z)
    add(col(GW, 4, both), z)
    add(col(GW, 5, both), z)
    add(col(GW, 6, both), z)
    add(col(GW, 7, fwd), z)
    add(col(GW, 8, fwd), z)
    add(col(GW, 9, fwd), z)
    add(pl.BlockSpec((POOL_HALO, GW),
                     lambda b, p, c: (jnp.maximum(fwd(b, p, c) * per8 - 1, 0), 9)), z)
    add(pl.BlockSpec((POOL_HALO, GW),
                     lambda b, p, c: (jnp.minimum((fwd(b, p, c) + 1) * per8, last_halo), 9)), z)
    if latent:
        rope_map = lambda b, p, c: (jnp.where(p == 0, nc - 1 - c, c), 0)
        add(pl.BlockSpec((CHUNK, 128), rope_map), extra["cos"])
        add(pl.BlockSpec((CHUNK, 128), rope_map), extra["sin"])
        add(pl.BlockSpec((None, None, 2, N_HEADS, HEAD_DIM, HEAD_DIM),
                         lambda b, p, c: (b, l, 0, 0, 0, 0)), extra["s0"])
    add(layer((2, N_HEADS * CHUNK, CHUNK)), tabs["dmat"])
    add(layer((4, CHUNK, GW)), tabs["dec"])
    add(layer((N_HEADS * CHUNK, 128)), tabs["sink"])
    add(layer((CHUNK, N_HEADS * CHUNK)), tabs["ws"])
    add(layer((CHUNK, GW)), tabs["bias"])
    add(layer((8, GW)), tabs["vec"])
    add(const((GW, GW)), tabs["gmat"])
    add(pl.BlockSpec((CHUNK, GW), lambda b, p, c: (c * p, 0)), tabs["cnt_lat"] if latent else tabs["cnt_ctx"])
    add(layer((GW, GW)), tabs["wpool"])

    out_shape = [jax.ShapeDtypeStruct((nb * nc * CHUNK, D_MODEL), F32)]
    out_specs = [pl.BlockSpec((CHUNK, D_MODEL), lambda b, p, c: (b * nc + c * p, 0))]
    scratch = [pltpu.VMEM((GW, GW), F32), pltpu.VMEM((GW, GW), F32),
               pltpu.VMEM((nc * CHUNK, GW), F32),
               pltpu.VMEM((CHUNK + 2 * POOL_HALO, GW), F32)]
    if latent:
        scratch += [pltpu.VMEM(((nc + 2) * CHUNK, 128), BF16), pltpu.VMEM(((nc + 2) * CHUNK, 128), BF16)]
    else:
        out_shape.append(jax.ShapeDtypeStruct((nb, 2, N_HEADS, HEAD_DIM, HEAD_DIM), F32))
        out_specs.append(pl.BlockSpec((None, 2, N_HEADS, HEAD_DIM, HEAD_DIM), lambda b, p, c: (b, 0, 0, 0, 0)))
        for _ in range(2):
            out_shape.append(jax.ShapeDtypeStruct((nb, SEQ, 128), F32))
            out_specs.append(pl.BlockSpec((None, SEQ, 128), lambda b, p, c: (b, 0, 0)))

    return pl.pallas_call(
        functools.partial(_mixer_kernel, latent=latent, nc=nc),
        grid=(nb, 2, nc),
        in_specs=specs,
        out_specs=out_specs,
        out_shape=out_shape,
        scratch_shapes=scratch,
        compiler_params=pltpu.CompilerParams(
            dimension_semantics=("arbitrary", "arbitrary", "arbitrary"), vmem_limit_bytes=VMEM_LIMIT),
        name="mixer_latent" if latent else "mixer_context",
    )(*args)


def _pad_rows(rows, n=8):
    a = jnp.stack(rows)
    return jnp.concatenate([a, jnp.zeros((n - a.shape[0],) + a.shape[1:], a.dtype)], axis=0)


def _block_diag(blocks):
    g, n, _ = blocks.shape
    eye = jnp.eye(g, dtype=blocks.dtype)
    return (eye[:, None, :, None] * blocks[:, :, None, :]).reshape(g * n, g * n)


def _inv_count(n):
    t = np.arange(n)
    cols = []
    for w in POOL_WINDOWS:
        cnt = np.clip(t + w // 2, 0, n) - np.clip(t - w // 2, 0, n)
        cols.append(np.repeat((1.0 / cnt)[:, None], HEAD_DIM, axis=1))
    return jnp.asarray(np.concatenate(cols, axis=1), F32)


def _rope_tables():
    rows = DEC_SEQ // GRID_W
    r, cc = jnp.meshgrid(jnp.arange(rows), jnp.arange(GRID_W), indexing="ij")
    half = HEAD_DIM // 2
    freqs = ROPE_BASE ** (-jnp.arange(0, half, 2, dtype=F32) / half)

    def tables(pos):
        ang = pos.reshape(-1).astype(F32)[:, None] * freqs[None, :]
        cos, sin = jnp.cos(ang), jnp.sin(ang)
        return jnp.concatenate([cos, cos], axis=1), jnp.concatenate([-sin, sin], axis=1)

    cr, sr = tables(r)
    ccol, scol = tables(cc)
    cos = jnp.concatenate([cr, ccol], axis=1)
    sin = jnp.concatenate([sr, scol], axis=1)
    return jnp.tile(cos, (1, 2)), jnp.tile(sin, (1, 2))


def _layer_tables(attn_sink, sgu_norm_w, sgu_norm_b, sgu_ws, sgu_bs, ret_decay, ret_gn_w, pool_w, pool_scale):
    log_g = jax.nn.log_sigmoid(ret_decay.astype(F32))
    i = jnp.arange(CHUNK, dtype=F32)
    rel = i[:, None] - i[None, :]
    kscale = HEAD_DIM ** -0.5
    d_f = jnp.where(rel >= 0, jnp.exp(jnp.maximum(rel, 0.0)[None] * log_g[0][:, None, None]), 0.0)
    d_b = jnp.where(rel <= 0, jnp.exp(jnp.maximum(-rel, 0.0)[None] * log_g[1][:, None, None]), 0.0)
    dmat = jnp.stack([d_f.reshape(N_HEADS * CHUNK, CHUNK), d_b.reshape(N_HEADS * CHUNK, CHUNK)]) * kscale

    def lanes(per_head):
        return jnp.repeat(per_head, HEAD_DIM, axis=1)

    qdec_f = lanes(jnp.exp((i + 1.0)[:, None] * log_g[0][None, :]))
    qdec_b = lanes(jnp.exp((CHUNK - i)[:, None] * log_g[1][None, :]))
    kdec_f = lanes(jnp.exp((CHUNK - 1.0 - i)[:, None] * log_g[0][None, :])) * kscale
    kdec_b = lanes(jnp.exp(i[:, None] * log_g[1][None, :])) * kscale
    cdec = jnp.repeat(jnp.exp(CHUNK * log_g), HEAD_DIM, axis=1)
    vec = _pad_rows([sgu_norm_w, sgu_norm_b, pool_scale, ret_gn_w[0], ret_gn_w[1], cdec[0], cdec[1]])
    return {
        "dmat": dmat,
        "dec": jnp.stack([qdec_f, qdec_b, kdec_f, kdec_b]),
        "sink": jnp.broadcast_to(jnp.repeat(attn_sink, CHUNK)[:, None], (N_HEADS * CHUNK, 128)),
        "ws": jnp.concatenate([sgu_ws[h] for h in range(N_HEADS)], axis=1).astype(BF16),
        "bias": jnp.repeat(sgu_bs.T, HEAD_DIM, axis=1),
        "vec": vec,
        "wpool": _block_diag(pool_w).astype(BF16),
    }


def kernel(x_prompt, x_sample, cache_attn_k, cache_attn_v, state_ret, c, c_ctx, w_ada, b_ada, w_in,
           w_out, attn_sink, sgu_norm_w, sgu_norm_b, sgu_ws, sgu_bs, ret_decay, ret_gn_w, pool_w,
           pool_scale, ffn_up, ffn_conv_w, ffn_conv_b, ffn_down, ln_w, ln_b):
    cond8 = jnp.concatenate([c_ctx[None], c, jnp.zeros((8 - 1 - DEC_BATCH, D_MODEL), F32)], axis=0)
    mod = _modulation(cond8, w_ada, b_ada).reshape(DEPTH, 8, 6, D_MODEL)

    tabs = jax.vmap(_layer_tables)(attn_sink, sgu_norm_w, sgu_norm_b, sgu_ws, sgu_bs, ret_decay, ret_gn_w,
                                   pool_w, pool_scale)
    tabs["gmat"] = _block_diag(jnp.full((N_HEADS, HEAD_DIM, HEAD_DIM), 1.0 / HEAD_DIM, F32)).astype(BF16)
    tabs["cnt_ctx"] = _inv_count(SEQ)
    tabs["cnt_lat"] = _inv_count(DEC_SEQ)
    cos, sin = _rope_tables()
    extra = {"ck": cache_attn_k.reshape(DEC_BATCH, DEPTH, PAST_LEN, 128),
             "cv": cache_attn_v.reshape(DEC_BATCH, DEPTH, PAST_LEN, 128),
             "cos": cos, "sin": sin, "s0": state_ret}
    ln8 = jnp.concatenate([ln_w, ln_b, jnp.zeros((DEPTH, 4, D_MODEL), F32)], axis=1)
    conv8 = jnp.concatenate([ffn_conv_w, ffn_conv_b[:, None], jnp.zeros((DEPTH, 4, 2 * D_FF), F32)], axis=1)

    down_bf16 = _to_bf16(ffn_down)

    xs = [x_prompt.reshape(N_CTX, D_MODEL), x_sample.reshape(N_LAT, D_MODEL)]
    new_k, new_v, new_s = [], [], []
    for l in range(DEPTH):
        z = _inproj(xs, mod, w_in, l)
        mix_ctx, st, kn, vn = _mixer(z, tabs, l, latent=False)
        (mix_lat,) = _mixer(z, tabs, l, latent=True, extra=extra)
        x1 = _outproj([mix_ctx, mix_lat], xs, mod, w_out, ln8, l)
        xs = _ffn(x1, mod, ffn_up, conv8, down_bf16, ln8, l, split_out=(l == DEPTH - 1))
        new_k.append(kn.reshape(BATCH, SEQ, 2, HEAD_DIM))
        new_v.append(vn.reshape(BATCH, SEQ, 2, HEAD_DIM))
        new_s.append(st)

    y_prompt = xs[0].reshape(BATCH, SEQ, D_MODEL)
    y_sample = xs[1].reshape(DEC_BATCH, DEC_SEQ, D_MODEL)
    return (y_prompt, y_sample, jnp.stack(new_k, axis=1), jnp.stack(new_v, axis=1),
            jnp.stack(new_s, axis=1))
```

```python
import functools

import numpy as np
import jax
import jax.numpy as jnp
from jax import lax
from jax.experimental import pallas as pl
from jax.experimental.pallas import tpu as pltpu

F32 = jnp.float32
BF16 = jnp.bfloat16

D_MODEL = 1024
BATCH = 16
SEQ = 256
DEPTH = 2
DEC_BATCH = 2
DEC_SEQ = 2048
PAST_LEN = 256
GRID_W = 64
CHUNK = 128
HEAD_DIM = 64
GW = D_MODEL // 4
N_HEADS = 4
POOL_WINDOWS = (2, 4, 8, 16)
POOL_HALO = 8
D_FF = 2816
ROPE_BASE = 10000.0
LN_EPS = 1e-5
NEG_INF = -1e30
IN_WIDTH = 10 * GW
ALPHA = (2.0 * DEPTH) ** 0.25

N_CTX = BATCH * SEQ
N_LAT = DEC_BATCH * DEC_SEQ
ROWS = N_CTX + N_LAT

TM = 1024
CTX_TILES = N_CTX // TM
LAT_TILES_PER_SEQ = DEC_SEQ // TM
NB_IN = 512
FC = 256
NB_ADA = 1536
VMEM_LIMIT = 56 * 1024 * 1024


def _cond_of_tile(i, tm=TM):
    ctx_tiles = N_CTX // tm
    return jnp.where(i < ctx_tiles, 0, 1 + (i - ctx_tiles) // (DEC_SEQ // tm))


def _tile_specs(n_src, tm, grid_rank):
    ctx_tiles = N_CTX // tm
    if n_src == 1:
        rows = [lambda i: i]
    else:
        rows = [lambda i: jnp.minimum(i, ctx_tiles - 1), lambda i: jnp.maximum(i - ctx_tiles, 0)]
    if grid_rank == 1:
        return [pl.BlockSpec((tm, D_MODEL), lambda i, f=f: (f(i), 0)) for f in rows]
    return [pl.BlockSpec((tm, D_MODEL), lambda i, j, f=f: (f(i), 0)) for f in rows]


def _per_half(i, tm, n_max, fn):
    if n_max == 1:
        fn(0)
        return
    ctx_tiles = N_CTX // tm

    @pl.when(i < ctx_tiles)
    def _():
        fn(0)

    @pl.when(i >= ctx_tiles)
    def _():
        fn(1)


def _layer_norm(x, w, b):
    mu = jnp.mean(x, axis=-1, keepdims=True)
    d = x - mu
    var = jnp.mean(d * d, axis=-1, keepdims=True)
    return d * lax.rsqrt(var + LN_EPS) * w + b


def _silu(x):
    return x * jax.nn.sigmoid(x)


def _bdot(a, b):
    return jnp.dot(a.astype(BF16), b.astype(BF16), preferred_element_type=F32)


def _mod_kernel(c_ref, w_ref, b_ref, o_ref):
    o_ref[...] = _bdot(_silu(c_ref[...]), w_ref[...]) + b_ref[...]


def _modulation(cond8, w_ada, b_ada):
    return pl.pallas_call(
        _mod_kernel,
        grid=(DEPTH, 6 * D_MODEL // NB_ADA),
        in_specs=[
            pl.BlockSpec((8, D_MODEL), lambda l, j: (0, 0)),
            pl.BlockSpec((None, D_MODEL, NB_ADA), lambda l, j: (l, 0, j)),
            pl.BlockSpec((None, 1, NB_ADA), lambda l, j: (l, 0, j)),
        ],
        out_specs=pl.BlockSpec((None, 8, NB_ADA), lambda l, j: (l, 0, j)),
        out_shape=jax.ShapeDtypeStruct((DEPTH, 8, 6 * D_MODEL), F32),
        compiler_params=pltpu.CompilerParams(
            dimension_semantics=("arbitrary", "arbitrary"), vmem_limit_bytes=VMEM_LIMIT),
        name="modulation",
    )(cond8, w_ada, b_ada.reshape(DEPTH, 1, 6 * D_MODEL))


def _cast_kernel(w_ref, o_ref):
    o_ref[...] = w_ref[...].astype(BF16)


def _to_bf16(w, block_rows):
    depth, rows, cols = w.shape
    return pl.pallas_call(
        _cast_kernel,
        grid=(depth, rows // block_rows),
        in_specs=[pl.BlockSpec((None, block_rows, cols), lambda l, r: (l, r, 0))],
        out_specs=pl.BlockSpec((None, block_rows, cols), lambda l, r: (l, r, 0)),
        out_shape=jax.ShapeDtypeStruct(w.shape, BF16),
        compiler_params=pltpu.CompilerParams(dimension_semantics=("arbitrary", "arbitrary")),
        name="cast_bf16",
    )(w)


def _inproj_kernel(*refs, n_x):
    x_refs = refs[:n_x]
    mod_ref, w_ref, z_ref, h_scr = refs[n_x:]

    m = mod_ref[...]

    def build(side):
        h_scr[...] = (x_refs[side][...] * (1.0 + m[1:2]) + m[0:1]).astype(BF16)

    _per_half(pl.program_id(0), TM, n_x, build)
    h = h_scr[...]
    for jb in range(IN_WIDTH // NB_IN):
        cols = slice(jb * NB_IN, (jb + 1) * NB_IN)
        z_ref[:, cols] = jnp.dot(h, w_ref[:, cols], preferred_element_type=F32)


def _inproj(xs, mod, w_in_bf16, l):
    return pl.pallas_call(
        functools.partial(_inproj_kernel, n_x=len(xs)),
        grid=(ROWS // TM,),
        in_specs=_tile_specs(len(xs), TM, 1) + [
            pl.BlockSpec((None, None, 6, D_MODEL), lambda i: (l, _cond_of_tile(i), 0, 0)),
            pl.BlockSpec((None, D_MODEL, IN_WIDTH), lambda i: (l, 0, 0), pipeline_mode=pl.Buffered(1)),
        ],
        out_specs=pl.BlockSpec((TM, IN_WIDTH), lambda i: (i, 0)),
        out_shape=jax.ShapeDtypeStruct((ROWS, IN_WIDTH), F32),
        scratch_shapes=[pltpu.VMEM((TM, D_MODEL), BF16)],
        compiler_params=pltpu.CompilerParams(
            dimension_semantics=("arbitrary",), vmem_limit_bytes=VMEM_LIMIT),
        name="inproj",
    )(*xs, mod, w_in_bf16)


TM_OUT = 512


def _outproj_kernel(*refs, n_x):
    mix_refs = refs[:2]
    x_refs = refs[2:2 + n_x]
    mod_ref, w_ref, ln_ref, o_ref = refs[2 + n_x:]
    m = mod_ref[...]
    ln = ln_ref[...]

    def body(side):
        y = jnp.dot(mix_refs[side][...].astype(BF16), w_ref[...], preferred_element_type=F32)
        x = x_refs[min(side, n_x - 1)][...]
        o_ref[...] = _layer_norm(ALPHA * x + m[2:3] * y, ln[0:1], ln[2:3])

    _per_half(pl.program_id(0), TM_OUT, 2, body)


def _outproj(mixes, xs, mod, w_out, ln8, l):
    return pl.pallas_call(
        functools.partial(_outproj_kernel, n_x=len(xs)),
        grid=(ROWS // TM_OUT,),
        in_specs=_tile_specs(2, TM_OUT, 1) + _tile_specs(len(xs), TM_OUT, 1) + [
            pl.BlockSpec((None, None, 6, D_MODEL), lambda i: (l, _cond_of_tile(i, TM_OUT), 0, 0)),
            pl.BlockSpec((None, D_MODEL, D_MODEL), lambda i: (l, 0, 0)),
            pl.BlockSpec((None, 8, D_MODEL), lambda i: (l, 0, 0)),
        ],
        out_specs=pl.BlockSpec((TM_OUT, D_MODEL), lambda i: (i, 0)),
        out_shape=jax.ShapeDtypeStruct((ROWS, D_MODEL), F32),
        compiler_params=pltpu.CompilerParams(
            dimension_semantics=("arbitrary",), vmem_limit_bytes=VMEM_LIMIT),
        name="outproj",
    )(*mixes, *xs, mod, w_out, ln8)


SEG = TM // 8
HALO_ROWS = 16
RB = 256
GB = 64
NFC = D_FF // FC
LANE_BLOCKS = D_MODEL // 128
SEG_PITCH = SEG + 8


def _seg_rows(xc_ref, k):
    return jnp.concatenate([xc_ref[cb, pl.ds(k, 8, stride=SEG_PITCH), :] for cb in range(LANE_BLOCKS)], axis=1)


def _ffn_kernel(*refs, n_out):
    (x_ref, xp_ref, xn_ref, mod_ref, upa_ref, upg_ref, cva_ref, cvg_ref, dn_ref, ln_ref) = refs[:10]
    o_refs = refs[10:10 + n_out]
    h_scr, act_scr, xc_scr, u0_scr, u1_scr, wbf_scr = refs[10 + n_out:]
    u_scrs = (u0_scr, u1_scr)
    i = pl.program_id(0)
    j = pl.program_id(1)
    is_ctx = i < CTX_TILES
    lat_pos = (i - CTX_TILES) % LAT_TILES_PER_SEQ

    def build_h():
        m = mod_ref[...]
        scale = 1.0 + m[4:5]
        shift = m[3:4]
        for cb in range(LANE_BLOCKS):
            for s in range(8):
                xc_scr[cb, s * SEG_PITCH:s * SEG_PITCH + SEG, :] = x_ref[s * SEG:(s + 1) * SEG,
                                                                         cb * 128:(cb + 1) * 128]
        for k in range(0, SEG, 2):
            rows = jnp.concatenate([_seg_rows(xc_scr, k), _seg_rows(xc_scr, k + 1)], axis=0)
            h_scr[8 * k:8 * k + 16, :] = (rows * scale + shift).astype(BF16)
        sub = lax.broadcasted_iota(jnp.int32, (HALO_ROWS, D_MODEL), 0)
        prev_ok = jnp.logical_not(is_ctx) & (lat_pos > 0)
        next_ok = jnp.logical_not(is_ctx) & (lat_pos < LAT_TILES_PER_SEQ - 1)
        halo_x = jnp.where(sub == 0, xp_ref[POOL_HALO - 1:POOL_HALO, :], xn_ref[0:1, :])
        keep = ((sub == 0) & prev_ok) | ((sub == 1) & next_ok)
        h_scr[TM:TM + HALO_ROWS, :] = jnp.where(keep, halo_x * scale + shift, 0.0).astype(BF16)

    n_rb = TM // RB

    def cast_up_weights():
        wbf_scr[0] = upa_ref[...].astype(BF16)
        wbf_scr[1] = upg_ref[...].astype(BF16)

    def up_proj(slot, rb):
        rows = slice(rb * RB, (rb + 1) * RB + (HALO_ROWS if rb == n_rb - 1 else 0))
        h = h_scr[rows, :]
        u_scrs[slot][0, rows, :] = jnp.dot(h, wbf_scr[0], preferred_element_type=F32)
        u_scrs[slot][1, rows, :] = jnp.dot(h, wbf_scr[1], preferred_element_type=F32)

    def conv(u_ref, cv, r0):
        u = u_ref[r0:r0 + GB, :]
        if r0 == 0 or r0 == TM - GB:
            sub = lax.broadcasted_iota(jnp.int32, (8, FC), 0)
            seg_per_seq = SEQ // SEG
        if r0 == 0:
            ctx_first = is_ctx & (sub % seg_per_seq == 0)
            b_first = jnp.where(sub == 0, u_ref[TM:TM + 1, :], pltpu.roll(u_ref[TM - 8:TM, :], 1, axis=0))
            um1 = jnp.concatenate([jnp.where(ctx_first, 0.0, b_first), u[0:GB - 8]], axis=0)
        else:
            um1 = u_ref[r0 - 8:r0 + GB - 8, :]
        if r0 == TM - GB:
            ctx_last = is_ctx & (sub % seg_per_seq == seg_per_seq - 1)
            b_last = jnp.where(sub == 7, u_ref[TM + 1:TM + 2, :], pltpu.roll(u_ref[0:8, :], 7, axis=0))
            up1 = jnp.concatenate([u[8:GB], jnp.where(ctx_last, 0.0, b_last)], axis=0)
        else:
            up1 = u_ref[r0 + 8:r0 + GB + 8, :]
        return um1 * cv[0:1] + u * cv[1:2] + up1 * cv[2:3] + cv[3:4]

    def gate(slot, chunk, rb):
        for r0 in range(rb * RB, (rb + 1) * RB, GB):
            a = conv(u_scrs[slot].at[0], cva_ref[...], r0)
            g = conv(u_scrs[slot].at[1], cvg_ref[...], r0)
            act_scr[chunk, r0:r0 + GB, :] = (_silu(a) * g).astype(BF16)

    @pl.when(j == 0)
    def _():
        build_h()
        cast_up_weights()
        for rb in range(n_rb):
            up_proj(0, rb)

    for slot in range(2):
        @pl.when((j >= 1) & (j < NFC) & (j % 2 == slot))
        def _(slot=slot):
            cast_up_weights()
            for rb in range(n_rb):
                up_proj(slot, rb)
                gate(1 - slot, j - 1, rb)

    @pl.when(j == NFC)
    def _():
        for rb in range(n_rb):
            gate((NFC - 1) % 2, NFC - 1, rb)
        m = mod_ref[...]
        ln = ln_ref[...]
        for rb in range(TM // RB):
            vrows = range(rb * RB // 8, (rb + 1) * RB // 8)
            lhs = jnp.concatenate([act_scr[jj, rb * RB:(rb + 1) * RB, :] for jj in range(NFC)], axis=1)
            y = jnp.dot(lhs, dn_ref[...], preferred_element_type=F32)
            xr = jnp.concatenate([_seg_rows(xc_scr, k) for k in vrows], axis=0)
            out = _layer_norm(ALPHA * xr + m[5:6] * y, ln[1:2], ln[3:4])
            for kk, k in enumerate(vrows):
                for cb in range(LANE_BLOCKS):
                    xc_scr[cb, pl.ds(k, 8, stride=SEG_PITCH), :] = out[8 * kk:8 * kk + 8,
                                                                       cb * 128:(cb + 1) * 128]
        def write_out(side):
            for cb in range(LANE_BLOCKS):
                for s in range(8):
                    o_refs[side][s * SEG:(s + 1) * SEG, cb * 128:(cb + 1) * 128] = xc_scr[
                        cb, s * SEG_PITCH:s * SEG_PITCH + SEG, :]

        _per_half(i, TM, n_out, write_out)


def _ffn(x, mod, up, conv8, down, ln8, l, split_out):
    halo_blocks = TM // POOL_HALO
    last_halo = ROWS // POOL_HALO - 1

    def up_chunk(j):
        return jnp.minimum(j, NFC - 1)

    def gate_chunk(j):
        return jnp.maximum(j - 1, 0)

    if split_out:
        out_specs = _tile_specs(2, TM, 2)
        out_shape = [jax.ShapeDtypeStruct((N_CTX, D_MODEL), F32), jax.ShapeDtypeStruct((N_LAT, D_MODEL), F32)]
    else:
        out_specs = _tile_specs(1, TM, 2)
        out_shape = [jax.ShapeDtypeStruct((ROWS, D_MODEL), F32)]
    return pl.pallas_call(
        functools.partial(_ffn_kernel, n_out=len(out_shape)),
        grid=(ROWS // TM, NFC + 1),
        in_specs=[
            pl.BlockSpec((TM, D_MODEL), lambda i, j: (i, 0)),
            pl.BlockSpec((POOL_HALO, D_MODEL), lambda i, j: (jnp.maximum(i * halo_blocks - 1, 0), 0)),
            pl.BlockSpec((POOL_HALO, D_MODEL),
                         lambda i, j: (jnp.minimum((i + 1) * halo_blocks, last_halo), 0)),
            pl.BlockSpec((None, None, 6, D_MODEL), lambda i, j: (l, _cond_of_tile(i), 0, 0)),
            pl.BlockSpec((None, D_MODEL, FC), lambda i, j: (l, 0, up_chunk(j))),
            pl.BlockSpec((None, D_MODEL, FC), lambda i, j: (l, 0, NFC + up_chunk(j))),
            pl.BlockSpec((None, 8, FC), lambda i, j: (l, 0, gate_chunk(j))),
            pl.BlockSpec((None, 8, FC), lambda i, j: (l, 0, NFC + gate_chunk(j))),
            pl.BlockSpec((None, D_FF, D_MODEL), lambda i, j: (l, 0, 0), pipeline_mode=pl.Buffered(1)),
            pl.BlockSpec((None, 8, D_MODEL), lambda i, j: (l, 0, 0)),
        ],
        out_specs=out_specs,
        out_shape=out_shape,
        scratch_shapes=[pltpu.VMEM((TM + HALO_ROWS, D_MODEL), BF16),
                        pltpu.VMEM((NFC, TM, FC), BF16),
                        pltpu.VMEM((LANE_BLOCKS, 8 * SEG_PITCH, 128), F32),
                        pltpu.VMEM((2, TM + HALO_ROWS, FC), F32),
                        pltpu.VMEM((2, TM + HALO_ROWS, FC), F32),
                        pltpu.VMEM((2, D_MODEL, FC), BF16)],
        compiler_params=pltpu.CompilerParams(
            dimension_semantics=("arbitrary", "arbitrary"), vmem_limit_bytes=VMEM_LIMIT),
        name="convffn",
    )(x, x, x, mod, up, up, conv8, conv8, down, ln8)


def _head_masks(width):
    lane = lax.broadcasted_iota(jnp.int32, (1, width), 1)
    return [(lane >= h * HEAD_DIM) & (lane < (h + 1) * HEAD_DIM) for h in range(width // HEAD_DIM)]


def _stack_heads(x, masks):
    return jnp.concatenate([jnp.where(m, x, jnp.zeros_like(x)) for m in masks], axis=0)


def _rope(x, cos, sin):
    lane = lax.broadcasted_iota(jnp.int32, (1, 128), 1)
    lower = (lane & 31) < 16
    outs = []
    for k in range(x.shape[1] // 128):
        xb = x[:, k * 128:(k + 1) * 128]
        partner = jnp.where(lower, pltpu.roll(xb, 112, axis=1), pltpu.roll(xb, 16, axis=1))
        outs.append(xb * cos + partner * sin)
    return outs[0] if len(outs) == 1 else jnp.concatenate(outs, axis=1)


def _retention_chunk(q, k, v, dmat, qdec, kdec, cdec, state_scr, masks, bd_mask):
    qb = q.astype(BF16)
    kb = k.astype(BF16)
    vb = v.astype(BF16)
    s = lax.dot_general(_stack_heads(qb, masks), kb, (((1,), (1,)), ((), ())),
                        preferred_element_type=F32)
    p = (s * dmat).astype(BF16)
    p_cat = jnp.concatenate([p[h * CHUNK:(h + 1) * CHUNK] for h in range(N_HEADS)], axis=1)
    state = state_scr[...]
    o = (jnp.dot(p_cat, _stack_heads(vb, masks), preferred_element_type=F32)
         + _bdot(q * qdec, state))
    upd = lax.dot_general((k * kdec).astype(BF16), vb, (((0,), (0,)), ((), ())),
                          preferred_element_type=F32)
    state_scr[...] = state * cdec + jnp.where(bd_mask, upd, 0.0)
    return o


def _load_state(state_scr, blocks_ref, d):
    state_scr[...] = jnp.zeros((GW, GW), F32)
    for h in range(N_HEADS):
        sl = slice(h * HEAD_DIM, (h + 1) * HEAD_DIM)
        state_scr[sl, sl] = blocks_ref[d, h]


def _store_state(blocks_ref, d, state_scr):
    for h in range(N_HEADS):
        sl = slice(h * HEAD_DIM, (h + 1) * HEAD_DIM)
        blocks_ref[d, h] = state_scr[sl, sl]


def _group_norm(o, gmat):
    mu = jnp.dot(o.astype(BF16), gmat, preferred_element_type=F32)
    d = o - mu
    var = jnp.dot((d * d).astype(BF16), gmat, preferred_element_type=F32)
    return d * lax.rsqrt(var + LN_EPS)


def _mixer_kernel(*refs, latent, nc):
    if latent:
        (qa_ref, ka_ref, va_ref, kx_ref, vx_ref, ub_ref, vb_ref, qc_ref, kc_ref, vc_ref, gf_ref,
         gb_ref, pd_ref, pdp_ref, pdn_ref, cos_ref, sin_ref, s0_ref, dmat_ref, dec_ref, sink_ref,
         ws_ref, bias_ref, vec_ref, gmat_ref, cnt_ref, wpool_ref,
         mix_ref, sf_scr, sb_scr, ob_scr, pext_scr, k_scr, v_scr) = refs
    else:
        (qa_ref, kx_ref, vx_ref, ub_ref, vb_ref, qc_ref, kc_ref, vc_ref, gf_ref,
         gb_ref, pd_ref, pdp_ref, pdn_ref, dmat_ref, dec_ref, sink_ref,
         ws_ref, bias_ref, vec_ref, gmat_ref, cnt_ref, wpool_ref,
         mix_ref, st_ref, kn_ref, vn_ref, sf_scr, sb_scr, ob_scr, pext_scr) = refs

    p = pl.program_id(1)
    c = pl.program_id(2)
    masks = _head_masks(GW)
    row = lax.broadcasted_iota(jnp.int32, (GW, GW), 0)
    col = lax.broadcasted_iota(jnp.int32, (GW, GW), 1)
    bd_mask = (row // HEAD_DIM) == (col // HEAD_DIM)
    vec = vec_ref[...]

    @pl.when(p == 0)
    def _():
        rc = nc - 1 - c

        @pl.when(c == 0)
        def _():
            if latent:
                _load_state(sb_scr, s0_ref, 1)
                zero_blk = jnp.zeros((CHUNK, 2 * HEAD_DIM), BF16)
                k_scr[0:CHUNK, :] = zero_blk
                v_scr[0:CHUNK, :] = zero_blk
                k_scr[(nc + 1) * CHUNK:(nc + 2) * CHUNK, :] = zero_blk
                v_scr[(nc + 1) * CHUNK:(nc + 2) * CHUNK, :] = zero_blk
            else:
                sb_scr[...] = jnp.zeros((GW, GW), F32)

        if latent:
            dst = pl.ds(pl.multiple_of((rc + 1) * CHUNK, CHUNK), CHUNK)
            k_scr[dst, :] = _rope(ka_ref[...], cos_ref[...], sin_ref[...]).astype(BF16)
            v_scr[dst, :] = va_ref[...].astype(BF16)

        dec = dec_ref[...]
        o_b = _retention_chunk(qc_ref[...], kc_ref[...], vc_ref[...], dmat_ref[1],
                               dec[1], dec[3], vec[6:7], sb_scr, masks, bd_mask)
        ob_scr[pl.ds(pl.multiple_of(rc * CHUNK, CHUNK), CHUNK), :] = o_b

        if not latent:
            @pl.when(c == nc - 1)
            def _():
                _store_state(st_ref, 1, sb_scr)

    @pl.when(p == 1)
    def _():
        @pl.when(c == 0)
        def _():
            if latent:
                _load_state(sf_scr, s0_ref, 0)
            else:
                sf_scr[...] = jnp.zeros((GW, GW), F32)
                kn_ref[...] = kx_ref[...]
                vn_ref[...] = vx_ref[...]

        q = qa_ref[...]
        if latent:
            q = _rope(q, cos_ref[...], sin_ref[...])
        q = q * (HEAD_DIM ** -0.5)
        lane = lax.broadcasted_iota(jnp.int32, (1, 2 * HEAD_DIM), 1)
        lo = lane < HEAD_DIM
        q0, q1 = q[:, :128], q[:, 128:]
        zero = jnp.zeros_like(q0)
        q_st = jnp.concatenate([
            jnp.where(lo, q0, zero),
            jnp.where(lo, pltpu.roll(q0, HEAD_DIM, axis=1), zero),
            jnp.where(lo, zero, pltpu.roll(q1, HEAD_DIM, axis=1)),
            jnp.where(lo, zero, q1)], axis=0).astype(BF16)
        if latent:
            band = pl.ds(pl.multiple_of(c * CHUNK, CHUNK), 3 * CHUNK)
            k_all = jnp.concatenate([k_scr[band, :], kx_ref[...].astype(BF16)], axis=0)
            v_all = jnp.concatenate([v_scr[band, :], vx_ref[...].astype(BF16)], axis=0)
        else:
            k_all = kx_ref[...].astype(BF16)
            v_all = vx_ref[...].astype(BF16)
        s = lax.dot_general(q_st, k_all, (((1,), (1,)), ((), ())), preferred_element_type=F32)
        if latent:
            nk = 3 * CHUNK + PAST_LEN
            qi = lax.broadcasted_iota(jnp.int32, (N_HEADS * CHUNK, nk), 0) & (CHUNK - 1)
            kj = lax.broadcasted_iota(jnp.int32, (N_HEADS * CHUNK, nk), 1)
            kpos = kj + (c - 1) * CHUNK
            valid = (kj >= 3 * CHUNK) | ((kj >= qi) & (kj <= qi + 2 * CHUNK)
                                         & (kpos >= 0) & (kpos < nc * CHUNK))
            s = jnp.where(valid, s, NEG_INF)
        sink = sink_ref[...][:, 0:1]
        mx = jnp.maximum(jnp.max(s, axis=-1, keepdims=True), sink)
        e = jnp.exp(s - mx)
        den = jnp.sum(e, axis=-1, keepdims=True) + jnp.exp(sink - mx)
        o = jnp.dot(e.astype(BF16), v_all, preferred_element_type=F32) / den
        mix_ref[:, 0:128] = jnp.where(lo, o[0:CHUNK], pltpu.roll(o[CHUNK:2 * CHUNK], HEAD_DIM, axis=1))
        mix_ref[:, 128:256] = jnp.where(lo, pltpu.roll(o[2 * CHUNK:3 * CHUNK], HEAD_DIM, axis=1),
                                        o[3 * CHUNK:4 * CHUNK])

        vn = _layer_norm(vb_ref[...], vec[0:1], vec[1:2]).astype(BF16)
        sg = jnp.dot(ws_ref[...], _stack_heads(vn, masks), preferred_element_type=F32) + bias_ref[...]
        mix_ref[:, GW:2 * GW] = ub_ref[...] * sg

        dec = dec_ref[...]
        o_f = _retention_chunk(qc_ref[...], kc_ref[...], vc_ref[...], dmat_ref[0],
                               dec[0], dec[2], vec[5:6], sf_scr, masks, bd_mask)
        o_b = ob_scr[pl.ds(pl.multiple_of(c * CHUNK, CHUNK), CHUNK), :]
        gmat = gmat_ref[...]
        mix_ref[:, 2 * GW:3 * GW] = (_silu(gf_ref[...]) * (_group_norm(o_f, gmat) * vec[3:4])
                                     + _silu(gb_ref[...]) * (_group_norm(o_b, gmat) * vec[4:5]))
        if not latent:
            @pl.when(c == nc - 1)
            def _():
                _store_state(st_ref, 0, sf_scr)

        pd = pd_ref[...]
        pext_scr[0:POOL_HALO, :] = jnp.where(c > 0, pdp_ref[...], 0.0)
        pext_scr[POOL_HALO:POOL_HALO + CHUNK, :] = pd
        pext_scr[POOL_HALO + CHUNK:2 * POOL_HALO + CHUNK, :] = jnp.where(c < nc - 1, pdn_ref[...], 0.0)

        def win(d, half):
            return pext_scr[pl.ds(POOL_HALO + d, CHUNK), half * 128:(half + 1) * 128]

        a2 = win(-1, 0) + win(0, 0)
        a4 = a2 + win(-2, 0) + win(1, 0)
        a8 = win(-4, 1)
        for d in range(-3, 4):
            a8 = a8 + win(d, 1)
        a16 = a8
        for d in list(range(-8, -4)) + list(range(4, 8)):
            a16 = a16 + win(d, 1)
        sums = jnp.concatenate([jnp.where(lo, a2, a4), jnp.where(lo, a8, a16)], axis=1)
        yd = sums * cnt_ref[...] - pd
        mix_ref[:, 3 * GW:4 * GW] = _bdot(yd, wpool_ref[...]) * vec[2:3]


def _mixer(z, tabs, l, latent, extra=None):
    nb = DEC_BATCH if latent else BATCH
    nc = (DEC_SEQ if latent else SEQ) // CHUNK
    base = (N_CTX // CHUNK) if latent else 0
    last_halo = ROWS // POOL_HALO - 1
    per8 = CHUNK // POOL_HALO

    def fwd(b, p, c):
        return base + b * nc + c * p

    def both(b, p, c):
        return base + b * nc + jnp.where(p == 0, nc - 1 - c, c)

    def col(width, idx, rowmap):
        return pl.BlockSpec((CHUNK, width), lambda b, p, c: (rowmap(b, p, c), idx))

    def const(shape):
        return pl.BlockSpec(shape, lambda b, p, c: (0,) * len(shape))

    def layer(shape):
        return pl.BlockSpec((None,) + shape, lambda b, p, c: (l,) + (0,) * len(shape))

    specs, args = [], []

    def add(spec, arr):
        specs.append(spec)
        args.append(arr)

    add(col(GW, 0, fwd), z)
    if latent:
        add(pl.BlockSpec((CHUNK, 128), lambda b, p, c: (base + b * nc + (nc - 1 - c) * (1 - p), 2)), z)
        add(pl.BlockSpec((CHUNK, 128), lambda b, p, c: (base + b * nc + (nc - 1 - c) * (1 - p), 3)), z)
        add(pl.BlockSpec((None, None, PAST_LEN, 128), lambda b, p, c: (b, l, 0, 0)), extra["ck"])
        add(pl.BlockSpec((None, None, PAST_LEN, 128), lambda b, p, c: (b, l, 0, 0)), extra["cv"])
    else:
        add(pl.BlockSpec((SEQ, 128), lambda b, p, c: (b, 2)), z)
        add(pl.BlockSpec((SEQ, 128), lambda b, p, c: (b, 3)), z)
    add(col(GW, 2, fwd), z)
    add(col(GW, 3, fwd), z)
    add(col(GW, 4, both), z)
    add(col(GW, 5, both), z)
    add(col(GW, 6, both), z)
    add(col(GW, 7, fwd), z)
    add(col(GW, 8, fwd), z)
    add(col(GW, 9, fwd), z)
    add(pl.BlockSpec((POOL_HALO, GW),
                     lambda b, p, c: (jnp.maximum(fwd(b, p, c) * per8 - 1, 0), 9)), z)
    add(pl.BlockSpec((POOL_HALO, GW),
                     lambda b, p, c: (jnp.minimum((fwd(b, p, c) + 1) * per8, last_halo), 9)), z)
    if latent:
        rope_map = lambda b, p, c: (jnp.where(p == 0, nc - 1 - c, c), 0)
        add(pl.BlockSpec((CHUNK, 128), rope_map), extra["cos"])
        add(pl.BlockSpec((CHUNK, 128), rope_map), extra["sin"])
        add(pl.BlockSpec((None, None, 2, N_HEADS, HEAD_DIM, HEAD_DIM),
                         lambda b, p, c: (b, l, 0, 0, 0, 0)), extra["s0"])
    add(layer((2, N_HEADS * CHUNK, CHUNK)), tabs["dmat"])
    add(layer((4, CHUNK, GW)), tabs["dec"])
    add(layer((N_HEADS * CHUNK, 128)), tabs["sink"])
    add(layer((CHUNK, N_HEADS * CHUNK)), tabs["ws"])
    add(layer((CHUNK, GW)), tabs["bias"])
    add(layer((8, GW)), tabs["vec"])
    add(const((GW, GW)), tabs["gmat"])
    add(pl.BlockSpec((CHUNK, GW), lambda b, p, c: (c * p, 0)), tabs["cnt_lat"] if latent else tabs["cnt_ctx"])
    add(layer((GW, GW)), tabs["wpool"])

    out_shape = [jax.ShapeDtypeStruct((nb * nc * CHUNK, D_MODEL), F32)]
    out_specs = [pl.BlockSpec((CHUNK, D_MODEL), lambda b, p, c: (b * nc + c * p, 0))]
    scratch = [pltpu.VMEM((GW, GW), F32), pltpu.VMEM((GW, GW), F32),
               pltpu.VMEM((nc * CHUNK, GW), F32),
               pltpu.VMEM((CHUNK + 2 * POOL_HALO, GW), F32)]
    if latent:
        scratch += [pltpu.VMEM(((nc + 2) * CHUNK, 128), BF16), pltpu.VMEM(((nc + 2) * CHUNK, 128), BF16)]
    else:
        out_shape.append(jax.ShapeDtypeStruct((nb, 2, N_HEADS, HEAD_DIM, HEAD_DIM), F32))
        out_specs.append(pl.BlockSpec((None, 2, N_HEADS, HEAD_DIM, HEAD_DIM), lambda b, p, c: (b, 0, 0, 0, 0)))
        for _ in range(2):
            out_shape.append(jax.ShapeDtypeStruct((nb, SEQ, 128), F32))
            out_specs.append(pl.BlockSpec((None, SEQ, 128), lambda b, p, c: (b, 0, 0)))

    return pl.pallas_call(
        functools.partial(_mixer_kernel, latent=latent, nc=nc),
        grid=(nb, 2, nc),
        in_specs=specs,
        out_specs=out_specs,
        out_shape=out_shape,
        scratch_shapes=scratch,
        compiler_params=pltpu.CompilerParams(
            dimension_semantics=("arbitrary", "arbitrary", "arbitrary"), vmem_limit_bytes=VMEM_LIMIT),
        name="mixer_latent" if latent else "mixer_context",
    )(*args)


def _pad_rows(rows, n=8):
    a = jnp.stack(rows)
    return jnp.concatenate([a, jnp.zeros((n - a.shape[0],) + a.shape[1:], a.dtype)], axis=0)


def _block_diag(blocks):
    g, n, _ = blocks.shape
    eye = jnp.eye(g, dtype=blocks.dtype)
    return (eye[:, None, :, None] * blocks[:, :, None, :]).reshape(g * n, g * n)


def _inv_count(n):
    t = np.arange(n)
    cols = []
    for w in POOL_WINDOWS:
        cnt = np.clip(t + w // 2, 0, n) - np.clip(t - w // 2, 0, n)
        cols.append(np.repeat((1.0 / cnt)[:, None], HEAD_DIM, axis=1))
    return jnp.asarray(np.concatenate(cols, axis=1), F32)


def _rope_tables():
    rows = DEC_SEQ // GRID_W
    r, cc = jnp.meshgrid(jnp.arange(rows), jnp.arange(GRID_W), indexing="ij")
    half = HEAD_DIM // 2
    freqs = ROPE_BASE ** (-jnp.arange(0, half, 2, dtype=F32) / half)

    def tables(pos):
        ang = pos.reshape(-1).astype(F32)[:, None] * freqs[None, :]
        cos, sin = jnp.cos(ang), jnp.sin(ang)
        return jnp.concatenate([cos, cos], axis=1), jnp.concatenate([-sin, sin], axis=1)

    cr, sr = tables(r)
    ccol, scol = tables(cc)
    cos = jnp.concatenate([cr, ccol], axis=1)
    sin = jnp.concatenate([sr, scol], axis=1)
    return jnp.tile(cos, (1, 2)), jnp.tile(sin, (1, 2))


def _layer_tables(attn_sink, sgu_norm_w, sgu_norm_b, sgu_ws, sgu_bs, ret_decay, ret_gn_w, pool_w, pool_scale):
    log_g = jax.nn.log_sigmoid(ret_decay.astype(F32))
    i = jnp.arange(CHUNK, dtype=F32)
    rel = i[:, None] - i[None, :]
    kscale = HEAD_DIM ** -0.5
    d_f = jnp.where(rel >= 0, jnp.exp(jnp.maximum(rel, 0.0)[None] * log_g[0][:, None, None]), 0.0)
    d_b = jnp.where(rel <= 0, jnp.exp(jnp.maximum(-rel, 0.0)[None] * log_g[1][:, None, None]), 0.0)
    dmat = jnp.stack([d_f.reshape(N_HEADS * CHUNK, CHUNK), d_b.reshape(N_HEADS * CHUNK, CHUNK)]) * kscale

    def lanes(per_head):
        return jnp.repeat(per_head, HEAD_DIM, axis=1)

    qdec_f = lanes(jnp.exp((i + 1.0)[:, None] * log_g[0][None, :]))
    qdec_b = lanes(jnp.exp((CHUNK - i)[:, None] * log_g[1][None, :]))
    kdec_f = lanes(jnp.exp((CHUNK - 1.0 - i)[:, None] * log_g[0][None, :])) * kscale
    kdec_b = lanes(jnp.exp(i[:, None] * log_g[1][None, :])) * kscale
    cdec = jnp.repeat(jnp.exp(CHUNK * log_g), HEAD_DIM, axis=1)
    vec = _pad_rows([sgu_norm_w, sgu_norm_b, pool_scale, ret_gn_w[0], ret_gn_w[1], cdec[0], cdec[1]])
    return {
        "dmat": dmat,
        "dec": jnp.stack([qdec_f, qdec_b, kdec_f, kdec_b]),
        "sink": jnp.broadcast_to(jnp.repeat(attn_sink, CHUNK)[:, None], (N_HEADS * CHUNK, 128)),
        "ws": jnp.concatenate([sgu_ws[h] for h in range(N_HEADS)], axis=1).astype(BF16),
        "bias": jnp.repeat(sgu_bs.T, HEAD_DIM, axis=1),
        "vec": vec,
        "wpool": _block_diag(pool_w).astype(BF16),
    }


def kernel(x_prompt, x_sample, cache_attn_k, cache_attn_v, state_ret, c, c_ctx, w_ada, b_ada, w_in,
           w_out, attn_sink, sgu_norm_w, sgu_norm_b, sgu_ws, sgu_bs, ret_decay, ret_gn_w, pool_w,
           pool_scale, ffn_up, ffn_conv_w, ffn_conv_b, ffn_down, ln_w, ln_b):
    cond8 = jnp.concatenate([c_ctx[None], c, jnp.zeros((8 - 1 - DEC_BATCH, D_MODEL), F32)], axis=0)
    mod = _modulation(cond8, w_ada, b_ada).reshape(DEPTH, 8, 6, D_MODEL)

    tabs = jax.vmap(_layer_tables)(attn_sink, sgu_norm_w, sgu_norm_b, sgu_ws, sgu_bs, ret_decay, ret_gn_w,
                                   pool_w, pool_scale)
    tabs["gmat"] = _block_diag(jnp.full((N_HEADS, HEAD_DIM, HEAD_DIM), 1.0 / HEAD_DIM, F32)).astype(BF16)
    tabs["cnt_ctx"] = _inv_count(SEQ)
    tabs["cnt_lat"] = _inv_count(DEC_SEQ)
    cos, sin = _rope_tables()
    extra = {"ck": cache_attn_k.reshape(DEC_BATCH, DEPTH, PAST_LEN, 128),
             "cv": cache_attn_v.reshape(DEC_BATCH, DEPTH, PAST_LEN, 128),
             "cos": cos, "sin": sin, "s0": state_ret}
    ln8 = jnp.concatenate([ln_w, ln_b, jnp.zeros((DEPTH, 4, D_MODEL), F32)], axis=1)
    conv8 = jnp.concatenate([ffn_conv_w, ffn_conv_b[:, None], jnp.zeros((DEPTH, 4, 2 * D_FF), F32)], axis=1)

    down_bf16 = _to_bf16(ffn_down, D_FF // 4)
    w_in_bf16 = _to_bf16(w_in, D_MODEL // 2)
    w_out_bf16 = _to_bf16(w_out, D_MODEL)

    xs = [x_prompt.reshape(N_CTX, D_MODEL), x_sample.reshape(N_LAT, D_MODEL)]
    new_k, new_v, new_s = [], [], []
    for l in range(DEPTH):
        z = _inproj(xs, mod, w_in_bf16, l)
        mix_ctx, st, kn, vn = _mixer(z, tabs, l, latent=False)
        (mix_lat,) = _mixer(z, tabs, l, latent=True, extra=extra)
        x1 = _outproj([mix_ctx, mix_lat], xs, mod, w_out_bf16, ln8, l)
        xs = _ffn(x1, mod, ffn_up, conv8, down_bf16, ln8, l, split_out=(l == DEPTH - 1))
        new_k.append(kn.reshape(BATCH, SEQ, 2, HEAD_DIM))
        new_v.append(vn.reshape(BATCH, SEQ, 2, HEAD_DIM))
        new_s.append(st)

    y_prompt = xs[0].reshape(BATCH, SEQ, D_MODEL)
    y_sample = xs[1].reshape(DEC_BATCH, DEC_SEQ, D_MODEL)
    return (y_prompt, y_sample, jnp.stack(new_k, axis=1), jnp.stack(new_v, axis=1),
            jnp.stack(new_s, axis=1))
```

```python
import functools

import numpy as np
import jax
import jax.numpy as jnp
from jax import lax
from jax.experimental import pallas as pl
from jax.experimental.pallas import tpu as pltpu

F32 = jnp.float32
BF16 = jnp.bfloat16

D_MODEL = 1024
BATCH = 16
SEQ = 256
DEPTH = 2
DEC_BATCH = 2
DEC_SEQ = 2048
PAST_LEN = 256
GRID_W = 64
CHUNK = 128
HEAD_DIM = 64
GW = D_MODEL // 4
N_HEADS = 4
POOL_WINDOWS = (2, 4, 8, 16)
POOL_HALO = 8
D_FF = 2816
ROPE_BASE = 10000.0
LN_EPS = 1e-5
NEG_INF = -1e30
IN_WIDTH = 10 * GW
ALPHA = (2.0 * DEPTH) ** 0.25

N_CTX = BATCH * SEQ
N_LAT = DEC_BATCH * DEC_SEQ
ROWS = N_CTX + N_LAT

TM = 1024
CTX_TILES = N_CTX // TM
LAT_TILES_PER_SEQ = DEC_SEQ // TM
NB_IN = 512
FC = 256
NB_ADA = 1536
VMEM_LIMIT = 56 * 1024 * 1024


def _cond_of_tile(i, tm=TM):
    ctx_tiles = N_CTX // tm
    return jnp.where(i < ctx_tiles, 0, 1 + (i - ctx_tiles) // (DEC_SEQ // tm))


def _tile_specs(n_src, tm, grid_rank):
    ctx_tiles = N_CTX // tm
    if n_src == 1:
        rows = [lambda i: i]
    else:
        rows = [lambda i: jnp.minimum(i, ctx_tiles - 1), lambda i: jnp.maximum(i - ctx_tiles, 0)]
    if grid_rank == 1:
        return [pl.BlockSpec((tm, D_MODEL), lambda i, f=f: (f(i), 0)) for f in rows]
    return [pl.BlockSpec((tm, D_MODEL), lambda i, j, f=f: (f(i), 0)) for f in rows]


def _per_half(i, tm, n_max, fn):
    if n_max == 1:
        fn(0)
        return
    ctx_tiles = N_CTX // tm

    @pl.when(i < ctx_tiles)
    def _():
        fn(0)

    @pl.when(i >= ctx_tiles)
    def _():
        fn(1)


def _layer_norm(x, w, b):
    mu = jnp.mean(x, axis=-1, keepdims=True)
    d = x - mu
    var = jnp.mean(d * d, axis=-1, keepdims=True)
    return d * lax.rsqrt(var + LN_EPS) * w + b


def _silu(x):
    return x * jax.nn.sigmoid(x)


def _bdot(a, b):
    return jnp.dot(a.astype(BF16), b.astype(BF16), preferred_element_type=F32)


def _mod_kernel(c_ref, w_ref, b_ref, o_ref):
    o_ref[...] = _bdot(_silu(c_ref[...]), w_ref[...]) + b_ref[...]


def _modulation(cond8, w_ada, b_ada):
    return pl.pallas_call(
        _mod_kernel,
        grid=(DEPTH, 6 * D_MODEL // NB_ADA),
        in_specs=[
            pl.BlockSpec((8, D_MODEL), lambda l, j: (0, 0)),
            pl.BlockSpec((None, D_MODEL, NB_ADA), lambda l, j: (l, 0, j)),
            pl.BlockSpec((None, 1, NB_ADA), lambda l, j: (l, 0, j)),
        ],
        out_specs=pl.BlockSpec((None, 8, NB_ADA), lambda l, j: (l, 0, j)),
        out_shape=jax.ShapeDtypeStruct((DEPTH, 8, 6 * D_MODEL), F32),
        compiler_params=pltpu.CompilerParams(
            dimension_semantics=("arbitrary", "arbitrary"), vmem_limit_bytes=VMEM_LIMIT),
        name="modulation",
    )(cond8, w_ada, b_ada.reshape(DEPTH, 1, 6 * D_MODEL))


def _cast_kernel(w_ref, o_ref):
    o_ref[...] = w_ref[...].astype(BF16)


def _to_bf16(w, block_rows):
    depth, rows, cols = w.shape
    return pl.pallas_call(
        _cast_kernel,
        grid=(depth, rows // block_rows),
        in_specs=[pl.BlockSpec((None, block_rows, cols), lambda l, r: (l, r, 0))],
        out_specs=pl.BlockSpec((None, block_rows, cols), lambda l, r: (l, r, 0)),
        out_shape=jax.ShapeDtypeStruct(w.shape, BF16),
        compiler_params=pltpu.CompilerParams(dimension_semantics=("arbitrary", "arbitrary")),
        name="cast_bf16",
    )(w)


def _to_bf16_chunks(w, chunk):
    depth, rows, cols = w.shape
    return pl.pallas_call(
        _cast_kernel,
        grid=(depth, cols // chunk),
        in_specs=[pl.BlockSpec((None, rows, chunk), lambda l, c: (l, 0, c))],
        out_specs=pl.BlockSpec((None, None, rows, chunk), lambda l, c: (l, c, 0, 0)),
        out_shape=jax.ShapeDtypeStruct((depth, cols // chunk, rows, chunk), BF16),
        compiler_params=pltpu.CompilerParams(dimension_semantics=("arbitrary", "arbitrary")),
        name="cast_bf16_chunks",
    )(w)


def _inproj_kernel(*refs, n_x):
    x_refs = refs[:n_x]
    mod_ref, w_ref, z_ref, h_scr = refs[n_x:]

    m = mod_ref[...]

    def build(side):
        h_scr[...] = (x_refs[side][...] * (1.0 + m[1:2]) + m[0:1]).astype(BF16)

    _per_half(pl.program_id(0), TM, n_x, build)
    h = h_scr[...]
    for jb in range(IN_WIDTH // NB_IN):
        cols = slice(jb * NB_IN, (jb + 1) * NB_IN)
        z_ref[:, cols] = jnp.dot(h, w_ref[:, cols], preferred_element_type=F32)


def _inproj(xs, mod, w_in_bf16, l):
    return pl.pallas_call(
        functools.partial(_inproj_kernel, n_x=len(xs)),
        grid=(ROWS // TM,),
        in_specs=_tile_specs(len(xs), TM, 1) + [
            pl.BlockSpec((None, None, 6, D_MODEL), lambda i: (l, _cond_of_tile(i), 0, 0)),
            pl.BlockSpec((None, D_MODEL, IN_WIDTH), lambda i: (l, 0, 0), pipeline_mode=pl.Buffered(1)),
        ],
        out_specs=pl.BlockSpec((TM, IN_WIDTH), lambda i: (i, 0)),
        out_shape=jax.ShapeDtypeStruct((ROWS, IN_WIDTH), F32),
        scratch_shapes=[pltpu.VMEM((TM, D_MODEL), BF16)],
        compiler_params=pltpu.CompilerParams(
            dimension_semantics=("arbitrary",), vmem_limit_bytes=VMEM_LIMIT),
        name="inproj",
    )(*xs, mod, w_in_bf16)


TM_OUT = 512


def _outproj_kernel(*refs, n_x):
    mix_refs = refs[:2]
    x_refs = refs[2:2 + n_x]
    mod_ref, w_ref, ln_ref, o_ref = refs[2 + n_x:]
    m = mod_ref[...]
    ln = ln_ref[...]

    def body(side):
        y = jnp.dot(mix_refs[side][...].astype(BF16), w_ref[...], preferred_element_type=F32)
        x = x_refs[min(side, n_x - 1)][...]
        o_ref[...] = _layer_norm(ALPHA * x + m[2:3] * y, ln[0:1], ln[2:3])

    _per_half(pl.program_id(0), TM_OUT, 2, body)


def _outproj(mixes, xs, mod, w_out, ln8, l):
    return pl.pallas_call(
        functools.partial(_outproj_kernel, n_x=len(xs)),
        grid=(ROWS // TM_OUT,),
        in_specs=_tile_specs(2, TM_OUT, 1) + _tile_specs(len(xs), TM_OUT, 1) + [
            pl.BlockSpec((None, None, 6, D_MODEL), lambda i: (l, _cond_of_tile(i, TM_OUT), 0, 0)),
            pl.BlockSpec((None, D_MODEL, D_MODEL), lambda i: (l, 0, 0)),
            pl.BlockSpec((None, 8, D_MODEL), lambda i: (l, 0, 0)),
        ],
        out_specs=pl.BlockSpec((TM_OUT, D_MODEL), lambda i: (i, 0)),
        out_shape=jax.ShapeDtypeStruct((ROWS, D_MODEL), F32),
        compiler_params=pltpu.CompilerParams(
            dimension_semantics=("arbitrary",), vmem_limit_bytes=VMEM_LIMIT),
        name="outproj",
    )(*mixes, *xs, mod, w_out, ln8)


SEG = TM // 8
HALO_ROWS = 16
RB = 256
GB = 64
NFC = D_FF // FC
LANE_BLOCKS = D_MODEL // 128
SEG_PITCH = SEG + 8


def _seg_rows(xc_ref, k):
    return jnp.concatenate([xc_ref[cb, pl.ds(k, 8, stride=SEG_PITCH), :] for cb in range(LANE_BLOCKS)], axis=1)


def _ffn_kernel(*refs, n_out):
    (x_ref, xp_ref, xn_ref, mod_ref, upa_ref, upg_ref, cva_ref, cvg_ref, dn_ref, ln_ref) = refs[:10]
    o_refs = refs[10:10 + n_out]
    h_scr, act_scr, xc_scr, u0_scr, u1_scr = refs[10 + n_out:]
    u_scrs = (u0_scr, u1_scr)
    i = pl.program_id(0)
    j = pl.program_id(1)
    is_ctx = i < CTX_TILES
    lat_pos = (i - CTX_TILES) % LAT_TILES_PER_SEQ

    def build_h():
        m = mod_ref[...]
        scale = 1.0 + m[4:5]
        shift = m[3:4]
        for cb in range(LANE_BLOCKS):
            for s in range(8):
                xc_scr[cb, s * SEG_PITCH:s * SEG_PITCH + SEG, :] = x_ref[s * SEG:(s + 1) * SEG,
                                                                         cb * 128:(cb + 1) * 128]
        for k in range(0, SEG, 2):
            rows = jnp.concatenate([_seg_rows(xc_scr, k), _seg_rows(xc_scr, k + 1)], axis=0)
            h_scr[8 * k:8 * k + 16, :] = (rows * scale + shift).astype(BF16)
        sub = lax.broadcasted_iota(jnp.int32, (HALO_ROWS, D_MODEL), 0)
        prev_ok = jnp.logical_not(is_ctx) & (lat_pos > 0)
        next_ok = jnp.logical_not(is_ctx) & (lat_pos < LAT_TILES_PER_SEQ - 1)
        halo_x = jnp.where(sub == 0, xp_ref[POOL_HALO - 1:POOL_HALO, :], xn_ref[0:1, :])
        keep = ((sub == 0) & prev_ok) | ((sub == 1) & next_ok)
        h_scr[TM:TM + HALO_ROWS, :] = jnp.where(keep, halo_x * scale + shift, 0.0).astype(BF16)

    n_rb = TM // RB

    def up_proj(slot, rb):
        rows = slice(rb * RB, (rb + 1) * RB + (HALO_ROWS if rb == n_rb - 1 else 0))
        h = h_scr[rows, :]
        u_scrs[slot][0, rows, :] = jnp.dot(h, upa_ref[...], preferred_element_type=F32)
        u_scrs[slot][1, rows, :] = jnp.dot(h, upg_ref[...], preferred_element_type=F32)

    def conv(u_ref, cv, r0):
        u = u_ref[r0:r0 + GB, :]
        if r0 == 0 or r0 == TM - GB:
            sub = lax.broadcasted_iota(jnp.int32, (8, FC), 0)
            seg_per_seq = SEQ // SEG
        if r0 == 0:
            ctx_first = is_ctx & (sub % seg_per_seq == 0)
            b_first = jnp.where(sub == 0, u_ref[TM:TM + 1, :], pltpu.roll(u_ref[TM - 8:TM, :], 1, axis=0))
            um1 = jnp.concatenate([jnp.where(ctx_first, 0.0, b_first), u[0:GB - 8]], axis=0)
        else:
            um1 = u_ref[r0 - 8:r0 + GB - 8, :]
        if r0 == TM - GB:
            ctx_last = is_ctx & (sub % seg_per_seq == seg_per_seq - 1)
            b_last = jnp.where(sub == 7, u_ref[TM + 1:TM + 2, :], pltpu.roll(u_ref[0:8, :], 7, axis=0))
            up1 = jnp.concatenate([u[8:GB], jnp.where(ctx_last, 0.0, b_last)], axis=0)
        else:
            up1 = u_ref[r0 + 8:r0 + GB + 8, :]
        return um1 * cv[0:1] + u * cv[1:2] + up1 * cv[2:3] + cv[3:4]

    def gate(slot, chunk, rb):
        for r0 in range(rb * RB, (rb + 1) * RB, GB):
            a = conv(u_scrs[slot].at[0], cva_ref[...], r0)
            g = conv(u_scrs[slot].at[1], cvg_ref[...], r0)
            act_scr[chunk, r0:r0 + GB, :] = (_silu(a) * g).astype(BF16)

    @pl.when(j == 0)
    def _():
        build_h()
        for rb in range(n_rb):
            up_proj(0, rb)

    for slot in range(2):
        @pl.when((j >= 1) & (j < NFC) & (j % 2 == slot))
        def _(slot=slot):
            for rb in range(n_rb):
                up_proj(slot, rb)
                gate(1 - slot, j - 1, rb)

    @pl.when(j == NFC)
    def _():
        for rb in range(n_rb):
            gate((NFC - 1) % 2, NFC - 1, rb)
        m = mod_ref[...]
        ln = ln_ref[...]
        for rb in range(TM // RB):
            vrows = range(rb * RB // 8, (rb + 1) * RB // 8)
            lhs = jnp.concatenate([act_scr[jj, rb * RB:(rb + 1) * RB, :] for jj in range(NFC)], axis=1)
            y = jnp.dot(lhs, dn_ref[...], preferred_element_type=F32)
            xr = jnp.concatenate([_seg_rows(xc_scr, k) for k in vrows], axis=0)
            out = _layer_norm(ALPHA * xr + m[5:6] * y, ln[1:2], ln[3:4])
            for kk, k in enumerate(vrows):
                for cb in range(LANE_BLOCKS):
                    xc_scr[cb, pl.ds(k, 8, stride=SEG_PITCH), :] = out[8 * kk:8 * kk + 8,
                                                                       cb * 128:(cb + 1) * 128]
        def write_out(side):
            for cb in range(LANE_BLOCKS):
                for s in range(8):
                    o_refs[side][s * SEG:(s + 1) * SEG, cb * 128:(cb + 1) * 128] = xc_scr[
                        cb, s * SEG_PITCH:s * SEG_PITCH + SEG, :]

        _per_half(i, TM, n_out, write_out)


def _ffn(x, mod, up, conv8, down, ln8, l, split_out):
    halo_blocks = TM // POOL_HALO
    last_halo = ROWS // POOL_HALO - 1

    def up_chunk(j):
        return jnp.minimum(j, NFC - 1)

    def gate_chunk(j):
        return jnp.maximum(j - 1, 0)

    if split_out:
        out_specs = _tile_specs(2, TM, 2)
        out_shape = [jax.ShapeDtypeStruct((N_CTX, D_MODEL), F32), jax.ShapeDtypeStruct((N_LAT, D_MODEL), F32)]
    else:
        out_specs = _tile_specs(1, TM, 2)
        out_shape = [jax.ShapeDtypeStruct((ROWS, D_MODEL), F32)]
    return pl.pallas_call(
        functools.partial(_ffn_kernel, n_out=len(out_shape)),
        grid=(ROWS // TM, NFC + 1),
        in_specs=[
            pl.BlockSpec((TM, D_MODEL), lambda i, j: (i, 0)),
            pl.BlockSpec((POOL_HALO, D_MODEL), lambda i, j: (jnp.maximum(i * halo_blocks - 1, 0), 0)),
            pl.BlockSpec((POOL_HALO, D_MODEL),
                         lambda i, j: (jnp.minimum((i + 1) * halo_blocks, last_halo), 0)),
            pl.BlockSpec((None, None, 6, D_MODEL), lambda i, j: (l, _cond_of_tile(i), 0, 0)),
            pl.BlockSpec((None, None, D_MODEL, FC), lambda i, j: (l, up_chunk(j), 0, 0)),
            pl.BlockSpec((None, None, D_MODEL, FC), lambda i, j: (l, NFC + up_chunk(j), 0, 0)),
            pl.BlockSpec((None, 8, FC), lambda i, j: (l, 0, gate_chunk(j))),
            pl.BlockSpec((None, 8, FC), lambda i, j: (l, 0, NFC + gate_chunk(j))),
            pl.BlockSpec((None, D_FF, D_MODEL), lambda i, j: (l, 0, 0), pipeline_mode=pl.Buffered(1)),
            pl.BlockSpec((None, 8, D_MODEL), lambda i, j: (l, 0, 0)),
        ],
        out_specs=out_specs,
        out_shape=out_shape,
        scratch_shapes=[pltpu.VMEM((TM + HALO_ROWS, D_MODEL), BF16),
                        pltpu.VMEM((NFC, TM, FC), BF16),
                        pltpu.VMEM((LANE_BLOCKS, 8 * SEG_PITCH, 128), F32),
                        pltpu.VMEM((2, TM + HALO_ROWS, FC), F32),
                        pltpu.VMEM((2, TM + HALO_ROWS, FC), F32)],
        compiler_params=pltpu.CompilerParams(
            dimension_semantics=("arbitrary", "arbitrary"), vmem_limit_bytes=VMEM_LIMIT),
        name="convffn",
    )(x, x, x, mod, up, up, conv8, conv8, down, ln8)


def _head_masks(width):
    lane = lax.broadcasted_iota(jnp.int32, (1, width), 1)
    return [(lane >= h * HEAD_DIM) & (lane < (h + 1) * HEAD_DIM) for h in range(width // HEAD_DIM)]


def _stack_heads(x, masks):
    return jnp.concatenate([jnp.where(m, x, jnp.zeros_like(x)) for m in masks], axis=0)


def _rope(x, cos, sin):
    lane = lax.broadcasted_iota(jnp.int32, (1, 128), 1)
    lower = (lane & 31) < 16
    outs = []
    for k in range(x.shape[1] // 128):
        xb = x[:, k * 128:(k + 1) * 128]
        partner = jnp.where(lower, pltpu.roll(xb, 112, axis=1), pltpu.roll(xb, 16, axis=1))
        outs.append(xb * cos + partner * sin)
    return outs[0] if len(outs) == 1 else jnp.concatenate(outs, axis=1)


def _retention_chunk(q, k, v, dmat, qdec, kdec, cdec, state_scr, masks, bd_mask):
    qb = q.astype(BF16)
    kb = k.astype(BF16)
    vb = v.astype(BF16)
    s = lax.dot_general(_stack_heads(qb, masks), kb, (((1,), (1,)), ((), ())),
                        preferred_element_type=F32)
    p = (s * dmat).astype(BF16)
    p_cat = jnp.concatenate([p[h * CHUNK:(h + 1) * CHUNK] for h in range(N_HEADS)], axis=1)
    state = state_scr[...]
    o = (jnp.dot(p_cat, _stack_heads(vb, masks), preferred_element_type=F32)
         + _bdot(q * qdec, state))
    upd = lax.dot_general((k * kdec).astype(BF16), vb, (((0,), (0,)), ((), ())),
                          preferred_element_type=F32)
    state_scr[...] = state * cdec + jnp.where(bd_mask, upd, 0.0)
    return o


def _load_state(state_scr, blocks_ref, d):
    state_scr[...] = jnp.zeros((GW, GW), F32)
    for h in range(N_HEADS):
        sl = slice(h * HEAD_DIM, (h + 1) * HEAD_DIM)
        state_scr[sl, sl] = blocks_ref[d, h]


def _store_state(blocks_ref, d, state_scr):
    for h in range(N_HEADS):
        sl = slice(h * HEAD_DIM, (h + 1) * HEAD_DIM)
        blocks_ref[d, h] = state_scr[sl, sl]


def _group_norm(o, gmat):
    mu = jnp.dot(o.astype(BF16), gmat, preferred_element_type=F32)
    d = o - mu
    var = jnp.dot((d * d).astype(BF16), gmat, preferred_element_type=F32)
    return d * lax.rsqrt(var + LN_EPS)


def _mixer_kernel(*refs, latent, nc):
    if latent:
        (qa_ref, ka_ref, va_ref, kx_ref, vx_ref, ub_ref, vb_ref, qc_ref, kc_ref, vc_ref, gf_ref,
         gb_ref, pd_ref, pdp_ref, pdn_ref, cos_ref, sin_ref, s0_ref, dmat_ref, dec_ref, sink_ref,
         ws_ref, bias_ref, vec_ref, gmat_ref, cnt_ref, wpool_ref,
         mix_ref, sf_scr, sb_scr, ob_scr, pext_scr, k_scr, v_scr) = refs
    else:
        (qa_ref, kx_ref, vx_ref, ub_ref, vb_ref, qc_ref, kc_ref, vc_ref, gf_ref,
         gb_ref, pd_ref, pdp_ref, pdn_ref, dmat_ref, dec_ref, sink_ref,
         ws_ref, bias_ref, vec_ref, gmat_ref, cnt_ref, wpool_ref,
         mix_ref, st_ref, kn_ref, vn_ref, sf_scr, sb_scr, ob_scr, pext_scr) = refs

    p = pl.program_id(1)
    c = pl.program_id(2)
    masks = _head_masks(GW)
    row = lax.broadcasted_iota(jnp.int32, (GW, GW), 0)
    col = lax.broadcasted_iota(jnp.int32, (GW, GW), 1)
    bd_mask = (row // HEAD_DIM) == (col // HEAD_DIM)
    vec = vec_ref[...]

    @pl.when(p == 0)
    def _():
        rc = nc - 1 - c

        @pl.when(c == 0)
        def _():
            if latent:
                _load_state(sb_scr, s0_ref, 1)
                zero_blk = jnp.zeros((CHUNK, 2 * HEAD_DIM), BF16)
                k_scr[0:CHUNK, :] = zero_blk
                v_scr[0:CHUNK, :] = zero_blk
                k_scr[(nc + 1) * CHUNK:(nc + 2) * CHUNK, :] = zero_blk
                v_scr[(nc + 1) * CHUNK:(nc + 2) * CHUNK, :] = zero_blk
            else:
                sb_scr[...] = jnp.zeros((GW, GW), F32)

        if latent:
            dst = pl.ds(pl.multiple_of((rc + 1) * CHUNK, CHUNK), CHUNK)
            k_scr[dst, :] = _rope(ka_ref[...], cos_ref[...], sin_ref[...]).astype(BF16)
            v_scr[dst, :] = va_ref[...].astype(BF16)

        dec = dec_ref[...]
        o_b = _retention_chunk(qc_ref[...], kc_ref[...], vc_ref[...], dmat_ref[1],
                               dec[1], dec[3], vec[6:7], sb_scr, masks, bd_mask)
        ob_scr[pl.ds(pl.multiple_of(rc * CHUNK, CHUNK), CHUNK), :] = o_b

        if not latent:
            @pl.when(c == nc - 1)
            def _():
                _store_state(st_ref, 1, sb_scr)

    @pl.when(p == 1)
    def _():
        @pl.when(c == 0)
        def _():
            if latent:
                _load_state(sf_scr, s0_ref, 0)
            else:
                sf_scr[...] = jnp.zeros((GW, GW), F32)
                kn_ref[...] = kx_ref[...]
                vn_ref[...] = vx_ref[...]

        q = qa_ref[...]
        if latent:
            q = _rope(q, cos_ref[...], sin_ref[...])
        q = q * (HEAD_DIM ** -0.5)
        lane = lax.broadcasted_iota(jnp.int32, (1, 2 * HEAD_DIM), 1)
        lo = lane < HEAD_DIM
        q0, q1 = q[:, :128], q[:, 128:]
        zero = jnp.zeros_like(q0)
        q_st = jnp.concatenate([
            jnp.where(lo, q0, zero),
            jnp.where(lo, pltpu.roll(q0, HEAD_DIM, axis=1), zero),
            jnp.where(lo, zero, pltpu.roll(q1, HEAD_DIM, axis=1)),
            jnp.where(lo, zero, q1)], axis=0).astype(BF16)
        if latent:
            band = pl.ds(pl.multiple_of(c * CHUNK, CHUNK), 3 * CHUNK)
            k_all = jnp.concatenate([k_scr[band, :], kx_ref[...].astype(BF16)], axis=0)
            v_all = jnp.concatenate([v_scr[band, :], vx_ref[...].astype(BF16)], axis=0)
        else:
            k_all = kx_ref[...].astype(BF16)
            v_all = vx_ref[...].astype(BF16)
        s = lax.dot_general(q_st, k_all, (((1,), (1,)), ((), ())), preferred_element_type=F32)
        if latent:
            nk = 3 * CHUNK + PAST_LEN
            qi = lax.broadcasted_iota(jnp.int32, (N_HEADS * CHUNK, nk), 0) & (CHUNK - 1)
            kj = lax.broadcasted_iota(jnp.int32, (N_HEADS * CHUNK, nk), 1)
            kpos = kj + (c - 1) * CHUNK
            valid = (kj >= 3 * CHUNK) | ((kj >= qi) & (kj <= qi + 2 * CHUNK)
                                         & (kpos >= 0) & (kpos < nc * CHUNK))
            s = jnp.where(valid, s, NEG_INF)
        sink = sink_ref[...][:, 0:1]
        mx = jnp.maximum(jnp.max(s, axis=-1, keepdims=True), sink)
        e = jnp.exp(s - mx)
        den = jnp.sum(e, axis=-1, keepdims=True) + jnp.exp(sink - mx)
        o = jnp.dot(e.astype(BF16), v_all, preferred_element_type=F32) / den
        mix_ref[:, 0:128] = jnp.where(lo, o[0:CHUNK], pltpu.roll(o[CHUNK:2 * CHUNK], HEAD_DIM, axis=1))
        mix_ref[:, 128:256] = jnp.where(lo, pltpu.roll(o[2 * CHUNK:3 * CHUNK], HEAD_DIM, axis=1),
                                        o[3 * CHUNK:4 * CHUNK])

        vn = _layer_norm(vb_ref[...], vec[0:1], vec[1:2]).astype(BF16)
        sg = jnp.dot(ws_ref[...], _stack_heads(vn, masks), preferred_element_type=F32) + bias_ref[...]
        mix_ref[:, GW:2 * GW] = ub_ref[...] * sg

        dec = dec_ref[...]
        o_f = _retention_chunk(qc_ref[...], kc_ref[...], vc_ref[...], dmat_ref[0],
                               dec[0], dec[2], vec[5:6], sf_scr, masks, bd_mask)
        o_b = ob_scr[pl.ds(pl.multiple_of(c * CHUNK, CHUNK), CHUNK), :]
        gmat = gmat_ref[...]
        mix_ref[:, 2 * GW:3 * GW] = (_silu(gf_ref[...]) * (_group_norm(o_f, gmat) * vec[3:4])
                                     + _silu(gb_ref[...]) * (_group_norm(o_b, gmat) * vec[4:5]))
        if not latent:
            @pl.when(c == nc - 1)
            def _():
                _store_state(st_ref, 0, sf_scr)

        pd = pd_ref[...]
        pext_scr[0:POOL_HALO, :] = jnp.where(c > 0, pdp_ref[...], 0.0)
        pext_scr[POOL_HALO:POOL_HALO + CHUNK, :] = pd
        pext_scr[POOL_HALO + CHUNK:2 * POOL_HALO + CHUNK, :] = jnp.where(c < nc - 1, pdn_ref[...], 0.0)

        def win(d, half):
            return pext_scr[pl.ds(POOL_HALO + d, CHUNK), half * 128:(half + 1) * 128]

        a2 = win(-1, 0) + win(0, 0)
        a4 = a2 + win(-2, 0) + win(1, 0)
        a8 = win(-4, 1)
        for d in range(-3, 4):
            a8 = a8 + win(d, 1)
        a16 = a8
        for d in list(range(-8, -4)) + list(range(4, 8)):
            a16 = a16 + win(d, 1)
        sums = jnp.concatenate([jnp.where(lo, a2, a4), jnp.where(lo, a8, a16)], axis=1)
        yd = sums * cnt_ref[...] - pd
        mix_ref[:, 3 * GW:4 * GW] = _bdot(yd, wpool_ref[...]) * vec[2:3]


def _mixer(z, tabs, l, latent, extra=None):
    nb = DEC_BATCH if latent else BATCH
    nc = (DEC_SEQ if latent else SEQ) // CHUNK
    base = (N_CTX // CHUNK) if latent else 0
    last_halo = ROWS // POOL_HALO - 1
    per8 = CHUNK // POOL_HALO

    def fwd(b, p, c):
        return base + b * nc + c * p

    def both(b, p, c):
        return base + b * nc + jnp.where(p == 0, nc - 1 - c, c)

    def col(width, idx, rowmap):
        return pl.BlockSpec((CHUNK, width), lambda b, p, c: (rowmap(b, p, c), idx))

    def const(shape):
        return pl.BlockSpec(shape, lambda b, p, c: (0,) * len(shape))

    def layer(shape):
        return pl.BlockSpec((None,) + shape, lambda b, p, c: (l,) + (0,) * len(shape))

    specs, args = [], []

    def add(spec, arr):
        specs.append(spec)
        args.append(arr)

    add(col(GW, 0, fwd), z)
    if latent:
        add(pl.BlockSpec((CHUNK, 128), lambda b, p, c: (base + b * nc + (nc - 1 - c) * (1 - p), 2)), z)
        add(pl.BlockSpec((CHUNK, 128), lambda b, p, c: (base + b * nc + (nc - 1 - c) * (1 - p), 3)), z)
        add(pl.BlockSpec((None, None, PAST_LEN, 128), lambda b, p, c: (b, l, 0, 0)), extra["ck"])
        add(pl.BlockSpec((None, None, PAST_LEN, 128), lambda b, p, c: (b, l, 0, 0)), extra["cv"])
    else:
        add(pl.BlockSpec((SEQ, 128), lambda b, p, c: (b, 2)), z)
        add(pl.BlockSpec((SEQ, 128), lambda b, p, c: (b, 3)), z)
    add(col(GW, 2, fwd), z)
    add(col(GW, 3, fwd), z)
    add(col(GW, 4, both), z)
    add(col(GW, 5, both), z)
    add(col(GW, 6, both), z)
    add(col(GW, 7, fwd), z)
    add(col(GW, 8, fwd), z)
    add(col(GW, 9, fwd), z)
    add(pl.BlockSpec((POOL_HALO, GW),
                     lambda b, p, c: (jnp.maximum(fwd(b, p, c) * per8 - 1, 0), 9)), z)
    add(pl.BlockSpec((POOL_HALO, GW),
                     lambda b, p, c: (jnp.minimum((fwd(b, p, c) + 1) * per8, last_halo), 9)), z)
    if latent:
        rope_map = lambda b, p, c: (jnp.where(p == 0, nc - 1 - c, c), 0)
        add(pl.BlockSpec((CHUNK, 128), rope_map), extra["cos"])
        add(pl.BlockSpec((CHUNK, 128), rope_map), extra["sin"])
        add(pl.BlockSpec((None, None, 2, N_HEADS, HEAD_DIM, HEAD_DIM),
                         lambda b, p, c: (b, l, 0, 0, 0, 0)), extra["s0"])
    add(layer((2, N_HEADS * CHUNK, CHUNK)), tabs["dmat"])
    add(layer((4, CHUNK, GW)), tabs["dec"])
    add(layer((N_HEADS * CHUNK, 128)), tabs["sink"])
    add(layer((CHUNK, N_HEADS * CHUNK)), tabs["ws"])
    add(layer((CHUNK, GW)), tabs["bias"])
    add(layer((8, GW)), tabs["vec"])
    add(const((GW, GW)), tabs["gmat"])
    add(pl.BlockSpec((CHUNK, GW), lambda b, p, c: (c * p, 0)), tabs["cnt_lat"] if latent else tabs["cnt_ctx"])
    add(layer((GW, GW)), tabs["wpool"])

    out_shape = [jax.ShapeDtypeStruct((nb * nc * CHUNK, D_MODEL), F32)]
    out_specs = [pl.BlockSpec((CHUNK, D_MODEL), lambda b, p, c: (b * nc + c * p, 0))]
    scratch = [pltpu.VMEM((GW, GW), F32), pltpu.VMEM((GW, GW), F32),
               pltpu.VMEM((nc * CHUNK, GW), F32),
               pltpu.VMEM((CHUNK + 2 * POOL_HALO, GW), F32)]
    if latent:
        scratch += [pltpu.VMEM(((nc + 2) * CHUNK, 128), BF16), pltpu.VMEM(((nc + 2) * CHUNK, 128), BF16)]
    else:
        out_shape.append(jax.ShapeDtypeStruct((nb, 2, N_HEADS, HEAD_DIM, HEAD_DIM), F32))
        out_specs.append(pl.BlockSpec((None, 2, N_HEADS, HEAD_DIM, HEAD_DIM), lambda b, p, c: (b, 0, 0, 0, 0)))
        for _ in range(2):
            out_shape.append(jax.ShapeDtypeStruct((nb, SEQ, 128), F32))
            out_specs.append(pl.BlockSpec((None, SEQ, 128), lambda b, p, c: (b, 0, 0)))

    return pl.pallas_call(
        functools.partial(_mixer_kernel, latent=latent, nc=nc),
        grid=(nb, 2, nc),
        in_specs=specs,
        out_specs=out_specs,
        out_shape=out_shape,
        scratch_shapes=scratch,
        compiler_params=pltpu.CompilerParams(
            dimension_semantics=("arbitrary", "arbitrary", "arbitrary"), vmem_limit_bytes=VMEM_LIMIT),
        name="mixer_latent" if latent else "mixer_context",
    )(*args)


def _pad_rows(rows, n=8):
    a = jnp.stack(rows)
    return jnp.concatenate([a, jnp.zeros((n - a.shape[0],) + a.shape[1:], a.dtype)], axis=0)


def _block_diag(blocks):
    g, n, _ = blocks.shape
    eye = jnp.eye(g, dtype=blocks.dtype)
    return (eye[:, None, :, None] * blocks[:, :, None, :]).reshape(g * n, g * n)


def _inv_count(n):
    t = np.arange(n)
    cols = []
    for w in POOL_WINDOWS:
        cnt = np.clip(t + w // 2, 0, n) - np.clip(t - w // 2, 0, n)
        cols.append(np.repeat((1.0 / cnt)[:, None], HEAD_DIM, axis=1))
    return jnp.asarray(np.concatenate(cols, axis=1), F32)


def _rope_tables():
    rows = DEC_SEQ // GRID_W
    r, cc = jnp.meshgrid(jnp.arange(rows), jnp.arange(GRID_W), indexing="ij")
    half = HEAD_DIM // 2
    freqs = ROPE_BASE ** (-jnp.arange(0, half, 2, dtype=F32) / half)

    def tables(pos):
        ang = pos.reshape(-1).astype(F32)[:, None] * freqs[None, :]
        cos, sin = jnp.cos(ang), jnp.sin(ang)
        return jnp.concatenate([cos, cos], axis=1), jnp.concatenate([-sin, sin], axis=1)

    cr, sr = tables(r)
    ccol, scol = tables(cc)
    cos = jnp.concatenate([cr, ccol], axis=1)
    sin = jnp.concatenate([sr, scol], axis=1)
    return jnp.tile(cos, (1, 2)), jnp.tile(sin, (1, 2))


def _layer_tables(attn_sink, sgu_norm_w, sgu_norm_b, sgu_ws, sgu_bs, ret_decay, ret_gn_w, pool_w, pool_scale):
    log_g = jax.nn.log_sigmoid(ret_decay.astype(F32))
    i = jnp.arange(CHUNK, dtype=F32)
    rel = i[:, None] - i[None, :]
    kscale = HEAD_DIM ** -0.5
    d_f = jnp.where(rel >= 0, jnp.exp(jnp.maximum(rel, 0.0)[None] * log_g[0][:, None, None]), 0.0)
    d_b = jnp.where(rel <= 0, jnp.exp(jnp.maximum(-rel, 0.0)[None] * log_g[1][:, None, None]), 0.0)
    dmat = jnp.stack([d_f.reshape(N_HEADS * CHUNK, CHUNK), d_b.reshape(N_HEADS * CHUNK, CHUNK)]) * kscale

    def lanes(per_head):
        return jnp.repeat(per_head, HEAD_DIM, axis=1)

    qdec_f = lanes(jnp.exp((i + 1.0)[:, None] * log_g[0][None, :]))
    qdec_b = lanes(jnp.exp((CHUNK - i)[:, None] * log_g[1][None, :]))
    kdec_f = lanes(jnp.exp((CHUNK - 1.0 - i)[:, None] * log_g[0][None, :])) * kscale
    kdec_b = lanes(jnp.exp(i[:, None] * log_g[1][None, :])) * kscale
    cdec = jnp.repeat(jnp.exp(CHUNK * log_g), HEAD_DIM, axis=1)
    vec = _pad_rows([sgu_norm_w, sgu_norm_b, pool_scale, ret_gn_w[0], ret_gn_w[1], cdec[0], cdec[1]])
    return {
        "dmat": dmat,
        "dec": jnp.stack([qdec_f, qdec_b, kdec_f, kdec_b]),
        "sink": jnp.broadcast_to(jnp.repeat(attn_sink, CHUNK)[:, None], (N_HEADS * CHUNK, 128)),
        "ws": jnp.concatenate([sgu_ws[h] for h in range(N_HEADS)], axis=1).astype(BF16),
        "bias": jnp.repeat(sgu_bs.T, HEAD_DIM, axis=1),
        "vec": vec,
        "wpool": _block_diag(pool_w).astype(BF16),
    }


def kernel(x_prompt, x_sample, cache_attn_k, cache_attn_v, state_ret, c, c_ctx, w_ada, b_ada, w_in,
           w_out, attn_sink, sgu_norm_w, sgu_norm_b, sgu_ws, sgu_bs, ret_decay, ret_gn_w, pool_w,
           pool_scale, ffn_up, ffn_conv_w, ffn_conv_b, ffn_down, ln_w, ln_b):
    cond8 = jnp.concatenate([c_ctx[None], c, jnp.zeros((8 - 1 - DEC_BATCH, D_MODEL), F32)], axis=0)
    mod = _modulation(cond8, w_ada, b_ada).reshape(DEPTH, 8, 6, D_MODEL)

    tabs = jax.vmap(_layer_tables)(attn_sink, sgu_norm_w, sgu_norm_b, sgu_ws, sgu_bs, ret_decay, ret_gn_w,
                                   pool_w, pool_scale)
    tabs["gmat"] = _block_diag(jnp.full((N_HEADS, HEAD_DIM, HEAD_DIM), 1.0 / HEAD_DIM, F32)).astype(BF16)
    tabs["cnt_ctx"] = _inv_count(SEQ)
    tabs["cnt_lat"] = _inv_count(DEC_SEQ)
    cos, sin = _rope_tables()
    extra = {"ck": cache_attn_k.reshape(DEC_BATCH, DEPTH, PAST_LEN, 128),
             "cv": cache_attn_v.reshape(DEC_BATCH, DEPTH, PAST_LEN, 128),
             "cos": cos, "sin": sin, "s0": state_ret}
    ln8 = jnp.concatenate([ln_w, ln_b, jnp.zeros((DEPTH, 4, D_MODEL), F32)], axis=1)
    conv8 = jnp.concatenate([ffn_conv_w, ffn_conv_b[:, None], jnp.zeros((DEPTH, 4, 2 * D_FF), F32)], axis=1)

    down_bf16 = _to_bf16(ffn_down, D_FF // 4)
    w_in_bf16 = _to_bf16(w_in, D_MODEL // 2)
    w_out_bf16 = _to_bf16(w_out, D_MODEL)
    up_chunks = _to_bf16_chunks(ffn_up, FC)

    xs = [x_prompt.reshape(N_CTX, D_MODEL), x_sample.reshape(N_LAT, D_MODEL)]
    new_k, new_v, new_s = [], [], []
    for l in range(DEPTH):
        z = _inproj(xs, mod, w_in_bf16, l)
        mix_ctx, st, kn, vn = _mixer(z, tabs, l, latent=False)
        (mix_lat,) = _mixer(z, tabs, l, latent=True, extra=extra)
        x1 = _outproj([mix_ctx, mix_lat], xs, mod, w_out_bf16, ln8, l)
        xs = _ffn(x1, mod, up_chunks, conv8, down_bf16, ln8, l, split_out=(l == DEPTH - 1))
        new_k.append(kn.reshape(BATCH, SEQ, 2, HEAD_DIM))
        new_v.append(vn.reshape(BATCH, SEQ, 2, HEAD_DIM))
        new_s.append(st)

    y_prompt = xs[0].reshape(BATCH, SEQ, D_MODEL)
    y_sample = xs[1].reshape(DEC_BATCH, DEC_SEQ, D_MODEL)
    return (y_prompt, y_sample, jnp.stack(new_k, axis=1), jnp.stack(new_v, axis=1),
            jnp.stack(new_s, axis=1))
```

```python
import functools

import numpy as np
import jax
import jax.numpy as jnp
from jax import lax
from jax.experimental import pallas as pl
from jax.experimental.pallas import tpu as pltpu

F32 = jnp.float32
BF16 = jnp.bfloat16

D_MODEL = 1024
BATCH = 16
SEQ = 256
DEPTH = 2
DEC_BATCH = 2
DEC_SEQ = 2048
PAST_LEN = 256
GRID_W = 64
CHUNK = 128
HEAD_DIM = 64
GW = D_MODEL // 4
N_HEADS = 4
POOL_WINDOWS = (2, 4, 8, 16)
POOL_HALO = 8
D_FF = 2816
ROPE_BASE = 10000.0
LN_EPS = 1e-5
NEG_INF = -1e30
IN_WIDTH = 10 * GW
ALPHA = (2.0 * DEPTH) ** 0.25

N_CTX = BATCH * SEQ
N_LAT = DEC_BATCH * DEC_SEQ
ROWS = N_CTX + N_LAT

TM = 1024
CTX_TILES = N_CTX // TM
LAT_TILES_PER_SEQ = DEC_SEQ // TM
NB_IN = 512
FC = 256
NB_ADA = 1536
VMEM_LIMIT = 56 * 1024 * 1024


def _cond_of_tile(i, tm=TM):
    ctx_tiles = N_CTX // tm
    return jnp.where(i < ctx_tiles, 0, 1 + (i - ctx_tiles) // (DEC_SEQ // tm))


def _tile_specs(n_src, tm, grid_rank):
    ctx_tiles = N_CTX // tm
    if n_src == 1:
        rows = [lambda i: i]
    else:
        rows = [lambda i: jnp.minimum(i, ctx_tiles - 1), lambda i: jnp.maximum(i - ctx_tiles, 0)]
    if grid_rank == 1:
        return [pl.BlockSpec((tm, D_MODEL), lambda i, f=f: (f(i), 0)) for f in rows]
    return [pl.BlockSpec((tm, D_MODEL), lambda i, j, f=f: (f(i), 0)) for f in rows]


def _per_half(i, tm, n_max, fn):
    if n_max == 1:
        fn(0)
        return
    ctx_tiles = N_CTX // tm

    @pl.when(i < ctx_tiles)
    def _():
        fn(0)

    @pl.when(i >= ctx_tiles)
    def _():
        fn(1)


def _layer_norm(x, w, b):
    mu = jnp.mean(x, axis=-1, keepdims=True)
    d = x - mu
    var = jnp.mean(d * d, axis=-1, keepdims=True)
    return d * lax.rsqrt(var + LN_EPS) * w + b


def _silu(x):
    return x * jax.nn.sigmoid(x)


def _bdot(a, b):
    return jnp.dot(a.astype(BF16), b.astype(BF16), preferred_element_type=F32)


def _mod_kernel(c_ref, w_ref, b_ref, o_ref):
    o_ref[...] = _bdot(_silu(c_ref[...]), w_ref[...]) + b_ref[...]


def _modulation(cond8, w_ada, b_ada):
    return pl.pallas_call(
        _mod_kernel,
        grid=(DEPTH, 6 * D_MODEL // NB_ADA),
        in_specs=[
            pl.BlockSpec((8, D_MODEL), lambda l, j: (0, 0)),
            pl.BlockSpec((None, D_MODEL, NB_ADA), lambda l, j: (l, 0, j)),
            pl.BlockSpec((None, 1, NB_ADA), lambda l, j: (l, 0, j)),
        ],
        out_specs=pl.BlockSpec((None, 8, NB_ADA), lambda l, j: (l, 0, j)),
        out_shape=jax.ShapeDtypeStruct((DEPTH, 8, 6 * D_MODEL), F32),
        compiler_params=pltpu.CompilerParams(
            dimension_semantics=("arbitrary", "arbitrary"), vmem_limit_bytes=VMEM_LIMIT),
        name="modulation",
    )(cond8, w_ada, b_ada.reshape(DEPTH, 1, 6 * D_MODEL))


def _cast_kernel(w_ref, o_ref):
    o_ref[...] = w_ref[...].astype(BF16)


def _to_bf16(w, block_rows):
    depth, rows, cols = w.shape
    return pl.pallas_call(
        _cast_kernel,
        grid=(depth, rows // block_rows),
        in_specs=[pl.BlockSpec((None, block_rows, cols), lambda l, r: (l, r, 0))],
        out_specs=pl.BlockSpec((None, block_rows, cols), lambda l, r: (l, r, 0)),
        out_shape=jax.ShapeDtypeStruct(w.shape, BF16),
        compiler_params=pltpu.CompilerParams(dimension_semantics=("arbitrary", "arbitrary")),
        name="cast_bf16",
    )(w)


def _to_bf16_chunks(w, chunk):
    depth, rows, cols = w.shape
    return pl.pallas_call(
        _cast_kernel,
        grid=(depth, cols // chunk),
        in_specs=[pl.BlockSpec((None, rows, chunk), lambda l, c: (l, 0, c))],
        out_specs=pl.BlockSpec((None, None, rows, chunk), lambda l, c: (l, c, 0, 0)),
        out_shape=jax.ShapeDtypeStruct((depth, cols // chunk, rows, chunk), BF16),
        compiler_params=pltpu.CompilerParams(dimension_semantics=("arbitrary", "arbitrary")),
        name="cast_bf16_chunks",
    )(w)


def _inproj_kernel(*refs, n_x):
    x_refs = refs[:n_x]
    mod_ref, w_ref, z_ref, h_scr = refs[n_x:]

    m = mod_ref[...]

    def build(side):
        h_scr[...] = (x_refs[side][...] * (1.0 + m[1:2]) + m[0:1]).astype(BF16)

    _per_half(pl.program_id(0), TM, n_x, build)
    h = h_scr[...]
    for jb in range(IN_WIDTH // NB_IN):
        cols = slice(jb * NB_IN, (jb + 1) * NB_IN)
        z_ref[:, cols] = jnp.dot(h, w_ref[:, cols], preferred_element_type=F32)


def _inproj(xs, mod, w_in_bf16, l):
    return pl.pallas_call(
        functools.partial(_inproj_kernel, n_x=len(xs)),
        grid=(ROWS // TM,),
        in_specs=_tile_specs(len(xs), TM, 1) + [
            pl.BlockSpec((None, None, 6, D_MODEL), lambda i: (l, _cond_of_tile(i), 0, 0)),
            pl.BlockSpec((None, D_MODEL, IN_WIDTH), lambda i: (l, 0, 0), pipeline_mode=pl.Buffered(1)),
        ],
        out_specs=pl.BlockSpec((TM, IN_WIDTH), lambda i: (i, 0)),
        out_shape=jax.ShapeDtypeStruct((ROWS, IN_WIDTH), F32),
        scratch_shapes=[pltpu.VMEM((TM, D_MODEL), BF16)],
        compiler_params=pltpu.CompilerParams(
            dimension_semantics=("arbitrary",), vmem_limit_bytes=VMEM_LIMIT),
        name="inproj",
    )(*xs, mod, w_in_bf16)


TM_OUT = 512


def _outproj_kernel(*refs, n_x):
    mix_refs = refs[:2]
    x_refs = refs[2:2 + n_x]
    mod_ref, w_ref, ln_ref, o_ref = refs[2 + n_x:]
    m = mod_ref[...]
    ln = ln_ref[...]

    def body(side):
        y = jnp.dot(mix_refs[side][...].astype(BF16), w_ref[...], preferred_element_type=F32)
        x = x_refs[min(side, n_x - 1)][...]
        o_ref[...] = _layer_norm(ALPHA * x + m[2:3] * y, ln[0:1], ln[2:3])

    _per_half(pl.program_id(0), TM_OUT, 2, body)


def _outproj(mixes, xs, mod, w_out, ln8, l):
    return pl.pallas_call(
        functools.partial(_outproj_kernel, n_x=len(xs)),
        grid=(ROWS // TM_OUT,),
        in_specs=_tile_specs(2, TM_OUT, 1) + _tile_specs(len(xs), TM_OUT, 1) + [
            pl.BlockSpec((None, None, 6, D_MODEL), lambda i: (l, _cond_of_tile(i, TM_OUT), 0, 0)),
            pl.BlockSpec((None, D_MODEL, D_MODEL), lambda i: (l, 0, 0)),
            pl.BlockSpec((None, 8, D_MODEL), lambda i: (l, 0, 0)),
        ],
        out_specs=pl.BlockSpec((TM_OUT, D_MODEL), lambda i: (i, 0)),
        out_shape=jax.ShapeDtypeStruct((ROWS, D_MODEL), F32),
        compiler_params=pltpu.CompilerParams(
            dimension_semantics=("arbitrary",), vmem_limit_bytes=VMEM_LIMIT),
        name="outproj",
    )(*mixes, *xs, mod, w_out, ln8)


SEG = TM // 8
HALO_ROWS = 16
RB = 256
RBU = 512
GB = 64
NFC = D_FF // FC
LANE_BLOCKS = D_MODEL // 128
SEG_PITCH = SEG + 8


def _seg_rows(xc_ref, k):
    return jnp.concatenate([xc_ref[cb, pl.ds(k, 8, stride=SEG_PITCH), :] for cb in range(LANE_BLOCKS)], axis=1)


def _ffn_kernel(*refs, n_out):
    (x_ref, xp_ref, xn_ref, mod_ref, upa_ref, upg_ref, cva_ref, cvg_ref, dn_ref, ln_ref) = refs[:10]
    o_refs = refs[10:10 + n_out]
    h_scr, act_scr, xc_scr, u0_scr, u1_scr, wbf_scr = refs[10 + n_out:]
    u_scrs = (u0_scr, u1_scr)
    i = pl.program_id(0)
    j = pl.program_id(1)
    is_ctx = i < CTX_TILES
    lat_pos = (i - CTX_TILES) % LAT_TILES_PER_SEQ

    def build_h():
        m = mod_ref[...]
        scale = 1.0 + m[4:5]
        shift = m[3:4]
        for cb in range(LANE_BLOCKS):
            for s in range(8):
                xc_scr[cb, s * SEG_PITCH:s * SEG_PITCH + SEG, :] = x_ref[s * SEG:(s + 1) * SEG,
                                                                         cb * 128:(cb + 1) * 128]
        for k in range(0, SEG, 2):
            rows = jnp.concatenate([_seg_rows(xc_scr, k), _seg_rows(xc_scr, k + 1)], axis=0)
            h_scr[8 * k:8 * k + 16, :] = (rows * scale + shift).astype(BF16)
        sub = lax.broadcasted_iota(jnp.int32, (HALO_ROWS, D_MODEL), 0)
        prev_ok = jnp.logical_not(is_ctx) & (lat_pos > 0)
        next_ok = jnp.logical_not(is_ctx) & (lat_pos < LAT_TILES_PER_SEQ - 1)
        halo_x = jnp.where(sub == 0, xp_ref[POOL_HALO - 1:POOL_HALO, :], xn_ref[0:1, :])
        keep = ((sub == 0) & prev_ok) | ((sub == 1) & next_ok)
        h_scr[TM:TM + HALO_ROWS, :] = jnp.where(keep, halo_x * scale + shift, 0.0).astype(BF16)

    n_ub = TM // RBU

    def cast_up_weights():
        wbf_scr[:, 0:FC] = upa_ref[...].astype(BF16)
        wbf_scr[:, FC:2 * FC] = upg_ref[...].astype(BF16)

    def up_proj(slot, ub):
        rows = slice(ub * RBU, (ub + 1) * RBU + (HALO_ROWS if ub == n_ub - 1 else 0))
        u_scrs[slot][rows, :] = jnp.dot(h_scr[rows, :], wbf_scr[...], preferred_element_type=F32)

    def conv(u_ref, lanes, cv, r0):
        lo = max(r0 - 8, 0)
        hi = min(r0 + GB + 8, TM)
        ue = u_ref[lo:hi, lanes]
        u = ue[r0 - lo:r0 - lo + GB]
        if r0 == 0 or r0 == TM - GB:
            sub = lax.broadcasted_iota(jnp.int32, (8, FC), 0)
            seg_per_seq = SEQ // SEG
        if r0 == 0:
            ctx_first = is_ctx & (sub % seg_per_seq == 0)
            b_first = jnp.where(sub == 0, u_ref[TM:TM + 1, lanes],
                                pltpu.roll(u_ref[TM - 8:TM, lanes], 1, axis=0))
            um1 = jnp.concatenate([jnp.where(ctx_first, 0.0, b_first), u[0:GB - 8]], axis=0)
        else:
            um1 = ue[0:GB]
        if r0 == TM - GB:
            ctx_last = is_ctx & (sub % seg_per_seq == seg_per_seq - 1)
            b_last = jnp.where(sub == 7, u_ref[TM + 1:TM + 2, lanes],
                               pltpu.roll(u_ref[0:8, lanes], 7, axis=0))
            up1 = jnp.concatenate([u[8:GB], jnp.where(ctx_last, 0.0, b_last)], axis=0)
        else:
            up1 = ue[r0 - lo + 8:r0 - lo + GB + 8]
        return um1 * cv[0:1] + u * cv[1:2] + up1 * cv[2:3] + cv[3:4]

    def gate(slot, chunk, ub):
        for r0 in range(ub * RBU, (ub + 1) * RBU, GB):
            a = conv(u_scrs[slot], slice(0, FC), cva_ref[...], r0)
            g = conv(u_scrs[slot], slice(FC, 2 * FC), cvg_ref[...], r0)
            act_scr[chunk, r0:r0 + GB, :] = (_silu(a) * g).astype(BF16)

    @pl.when(j == 0)
    def _():
        build_h()
        cast_up_weights()
        for ub in range(n_ub):
            up_proj(0, ub)

    for slot in range(2):
        @pl.when((j >= 1) & (j < NFC) & (j % 2 == slot))
        def _(slot=slot):
            cast_up_weights()
            for ub in range(n_ub):
                up_proj(slot, ub)
                gate(1 - slot, j - 1, ub)

    @pl.when(j == NFC)
    def _():
        for ub in range(n_ub):
            gate((NFC - 1) % 2, NFC - 1, ub)
        m = mod_ref[...]
        ln = ln_ref[...]
        for rb in range(TM // RB):
            vrows = range(rb * RB // 8, (rb + 1) * RB // 8)
            lhs = jnp.concatenate([act_scr[jj, rb * RB:(rb + 1) * RB, :] for jj in range(NFC)], axis=1)
            y = jnp.dot(lhs, dn_ref[...], preferred_element_type=F32)
            xr = jnp.concatenate([_seg_rows(xc_scr, k) for k in vrows], axis=0)
            out = _layer_norm(ALPHA * xr + m[5:6] * y, ln[1:2], ln[3:4])
            for kk, k in enumerate(vrows):
                for cb in range(LANE_BLOCKS):
                    xc_scr[cb, pl.ds(k, 8, stride=SEG_PITCH), :] = out[8 * kk:8 * kk + 8,
                                                                       cb * 128:(cb + 1) * 128]
        def write_out(side):
            for cb in range(LANE_BLOCKS):
                for s in range(8):
                    o_refs[side][s * SEG:(s + 1) * SEG, cb * 128:(cb + 1) * 128] = xc_scr[
                        cb, s * SEG_PITCH:s * SEG_PITCH + SEG, :]

        _per_half(i, TM, n_out, write_out)


def _ffn(x, mod, up, conv8, down, ln8, l, split_out):
    halo_blocks = TM // POOL_HALO
    last_halo = ROWS // POOL_HALO - 1

    def up_chunk(j):
        return jnp.minimum(j, NFC - 1)

    def gate_chunk(j):
        return jnp.maximum(j - 1, 0)

    if split_out:
        out_specs = _tile_specs(2, TM, 2)
        out_shape = [jax.ShapeDtypeStruct((N_CTX, D_MODEL), F32), jax.ShapeDtypeStruct((N_LAT, D_MODEL), F32)]
    else:
        out_specs = _tile_specs(1, TM, 2)
        out_shape = [jax.ShapeDtypeStruct((ROWS, D_MODEL), F32)]
    return pl.pallas_call(
        functools.partial(_ffn_kernel, n_out=len(out_shape)),
        grid=(ROWS // TM, NFC + 1),
        in_specs=[
            pl.BlockSpec((TM, D_MODEL), lambda i, j: (i, 0)),
            pl.BlockSpec((POOL_HALO, D_MODEL), lambda i, j: (jnp.maximum(i * halo_blocks - 1, 0), 0)),
            pl.BlockSpec((POOL_HALO, D_MODEL),
                         lambda i, j: (jnp.minimum((i + 1) * halo_blocks, last_halo), 0)),
            pl.BlockSpec((None, None, 6, D_MODEL), lambda i, j: (l, _cond_of_tile(i), 0, 0)),
            pl.BlockSpec((None, D_MODEL, FC), lambda i, j: (l, 0, up_chunk(j))),
            pl.BlockSpec((None, D_MODEL, FC), lambda i, j: (l, 0, NFC + up_chunk(j))),
            pl.BlockSpec((None, 8, FC), lambda i, j: (l, 0, gate_chunk(j))),
            pl.BlockSpec((None, 8, FC), lambda i, j: (l, 0, NFC + gate_chunk(j))),
            pl.BlockSpec((None, D_FF, D_MODEL), lambda i, j: (l, 0, 0), pipeline_mode=pl.Buffered(1)),
            pl.BlockSpec((None, 8, D_MODEL), lambda i, j: (l, 0, 0)),
        ],
        out_specs=out_specs,
        out_shape=out_shape,
        scratch_shapes=[pltpu.VMEM((TM + HALO_ROWS, D_MODEL), BF16),
                        pltpu.VMEM((NFC, TM, FC), BF16),
                        pltpu.VMEM((LANE_BLOCKS, 8 * SEG_PITCH, 128), F32),
                        pltpu.VMEM((TM + HALO_ROWS, 2 * FC), F32),
                        pltpu.VMEM((TM + HALO_ROWS, 2 * FC), F32),
                        pltpu.VMEM((D_MODEL, 2 * FC), BF16)],
        compiler_params=pltpu.CompilerParams(
            dimension_semantics=("arbitrary", "arbitrary"), vmem_limit_bytes=VMEM_LIMIT),
        name="convffn",
    )(x, x, x, mod, up, up, conv8, conv8, down, ln8)


def _head_masks(width):
    lane = lax.broadcasted_iota(jnp.int32, (1, width), 1)
    return [(lane >= h * HEAD_DIM) & (lane < (h + 1) * HEAD_DIM) for h in range(width // HEAD_DIM)]


def _stack_heads(x, masks):
    return jnp.concatenate([jnp.where(m, x, jnp.zeros_like(x)) for m in masks], axis=0)


def _rope(x, cos, sin):
    lane = lax.broadcasted_iota(jnp.int32, (1, 128), 1)
    lower = (lane & 31) < 16
    outs = []
    for k in range(x.shape[1] // 128):
        xb = x[:, k * 128:(k + 1) * 128]
        partner = jnp.where(lower, pltpu.roll(xb, 112, axis=1), pltpu.roll(xb, 16, axis=1))
        outs.append(xb * cos + partner * sin)
    return outs[0] if len(outs) == 1 else jnp.concatenate(outs, axis=1)


def _retention_chunk(q, k, v, dmat, qdec, kdec, cdec, state_scr, masks, bd_mask):
    qb = q.astype(BF16)
    kb = k.astype(BF16)
    vb = v.astype(BF16)
    s = lax.dot_general(_stack_heads(qb, masks), kb, (((1,), (1,)), ((), ())),
                        preferred_element_type=F32)
    p = (s * dmat).astype(BF16)
    p_cat = jnp.concatenate([p[h * CHUNK:(h + 1) * CHUNK] for h in range(N_HEADS)], axis=1)
    state = state_scr[...]
    o = (jnp.dot(p_cat, _stack_heads(vb, masks), preferred_element_type=F32)
         + _bdot(q * qdec, state))
    upd = lax.dot_general((k * kdec).astype(BF16), vb, (((0,), (0,)), ((), ())),
                          preferred_element_type=F32)
    state_scr[...] = state * cdec + jnp.where(bd_mask, upd, 0.0)
    return o


def _load_state(state_scr, blocks_ref, d):
    state_scr[...] = jnp.zeros((GW, GW), F32)
    for h in range(N_HEADS):
        sl = slice(h * HEAD_DIM, (h + 1) * HEAD_DIM)
        state_scr[sl, sl] = blocks_ref[d, h]


def _store_state(blocks_ref, d, state_scr):
    for h in range(N_HEADS):
        sl = slice(h * HEAD_DIM, (h + 1) * HEAD_DIM)
        blocks_ref[d, h] = state_scr[sl, sl]


def _group_norm(o, gmat):
    mu = jnp.dot(o.astype(BF16), gmat, preferred_element_type=F32)
    d = o - mu
    var = jnp.dot((d * d).astype(BF16), gmat, preferred_element_type=F32)
    return d * lax.rsqrt(var + LN_EPS)


def _mixer_kernel(*refs, latent, nc):
    if latent:
        (qa_ref, ka_ref, va_ref, kx_ref, vx_ref, ub_ref, vb_ref, qc_ref, kc_ref, vc_ref, gf_ref,
         gb_ref, pd_ref, pdp_ref, pdn_ref, cos_ref, sin_ref, s0_ref, dmat_ref, dec_ref, sink_ref,
         ws_ref, bias_ref, vec_ref, gmat_ref, cnt_ref, wpool_ref,
         mix_ref, sf_scr, sb_scr, ob_scr, pext_scr, k_scr, v_scr) = refs
    else:
        (qa_ref, kx_ref, vx_ref, ub_ref, vb_ref, qc_ref, kc_ref, vc_ref, gf_ref,
         gb_ref, pd_ref, pdp_ref, pdn_ref, dmat_ref, dec_ref, sink_ref,
         ws_ref, bias_ref, vec_ref, gmat_ref, cnt_ref, wpool_ref,
         mix_ref, st_ref, kn_ref, vn_ref, sf_scr, sb_scr, ob_scr, pext_scr) = refs

    p = pl.program_id(1)
    c = pl.program_id(2)
    masks = _head_masks(GW)
    row = lax.broadcasted_iota(jnp.int32, (GW, GW), 0)
    col = lax.broadcasted_iota(jnp.int32, (GW, GW), 1)
    bd_mask = (row // HEAD_DIM) == (col // HEAD_DIM)
    vec = vec_ref[...]

    @pl.when(p == 0)
    def _():
        rc = nc - 1 - c

        @pl.when(c == 0)
        def _():
            if latent:
                _load_state(sb_scr, s0_ref, 1)
                zero_blk = jnp.zeros((CHUNK, 2 * HEAD_DIM), BF16)
                k_scr[0:CHUNK, :] = zero_blk
                v_scr[0:CHUNK, :] = zero_blk
                k_scr[(nc + 1) * CHUNK:(nc + 2) * CHUNK, :] = zero_blk
                v_scr[(nc + 1) * CHUNK:(nc + 2) * CHUNK, :] = zero_blk
            else:
                sb_scr[...] = jnp.zeros((GW, GW), F32)

        if latent:
            dst = pl.ds(pl.multiple_of((rc + 1) * CHUNK, CHUNK), CHUNK)
            k_scr[dst, :] = _rope(ka_ref[...], cos_ref[...], sin_ref[...]).astype(BF16)
            v_scr[dst, :] = va_ref[...].astype(BF16)

        dec = dec_ref[...]
        o_b = _retention_chunk(qc_ref[...], kc_ref[...], vc_ref[...], dmat_ref[1],
                               dec[1], dec[3], vec[6:7], sb_scr, masks, bd_mask)
        ob_scr[pl.ds(pl.multiple_of(rc * CHUNK, CHUNK), CHUNK), :] = o_b

        if not latent:
            @pl.when(c == nc - 1)
            def _():
                _store_state(st_ref, 1, sb_scr)

    @pl.when(p == 1)
    def _():
        @pl.when(c == 0)
        def _():
            if latent:
                _load_state(sf_scr, s0_ref, 0)
            else:
                sf_scr[...] = jnp.zeros((GW, GW), F32)
                kn_ref[...] = kx_ref[...]
                vn_ref[...] = vx_ref[...]

        q = qa_ref[...]
        if latent:
            q = _rope(q, cos_ref[...], sin_ref[...])
        q = q * (HEAD_DIM ** -0.5)
        lane = lax.broadcasted_iota(jnp.int32, (1, 2 * HEAD_DIM), 1)
        lo = lane < HEAD_DIM
        q0, q1 = q[:, :128], q[:, 128:]
        zero = jnp.zeros_like(q0)
        q_st = jnp.concatenate([
            jnp.where(lo, q0, zero),
            jnp.where(lo, pltpu.roll(q0, HEAD_DIM, axis=1), zero),
            jnp.where(lo, zero, pltpu.roll(q1, HEAD_DIM, axis=1)),
            jnp.where(lo, zero, q1)], axis=0).astype(BF16)
        if latent:
            band = pl.ds(pl.multiple_of(c * CHUNK, CHUNK), 3 * CHUNK)
            k_all = jnp.concatenate([k_scr[band, :], kx_ref[...].astype(BF16)], axis=0)
            v_all = jnp.concatenate([v_scr[band, :], vx_ref[...].astype(BF16)], axis=0)
        else:
            k_all = kx_ref[...].astype(BF16)
            v_all = vx_ref[...].astype(BF16)
        s = lax.dot_general(q_st, k_all, (((1,), (1,)), ((), ())), preferred_element_type=F32)
        if latent:
            nk = 3 * CHUNK + PAST_LEN
            qi = lax.broadcasted_iota(jnp.int32, (N_HEADS * CHUNK, nk), 0) & (CHUNK - 1)
            kj = lax.broadcasted_iota(jnp.int32, (N_HEADS * CHUNK, nk), 1)
            kpos = kj + (c - 1) * CHUNK
            valid = (kj >= 3 * CHUNK) | ((kj >= qi) & (kj <= qi + 2 * CHUNK)
                                         & (kpos >= 0) & (kpos < nc * CHUNK))
            s = jnp.where(valid, s, NEG_INF)
        sink = sink_ref[...][:, 0:1]
        mx = jnp.maximum(jnp.max(s, axis=-1, keepdims=True), sink)
        e = jnp.exp(s - mx)
        den = jnp.sum(e, axis=-1, keepdims=True) + jnp.exp(sink - mx)
        o = jnp.dot(e.astype(BF16), v_all, preferred_element_type=F32) / den
        mix_ref[:, 0:128] = jnp.where(lo, o[0:CHUNK], pltpu.roll(o[CHUNK:2 * CHUNK], HEAD_DIM, axis=1))
        mix_ref[:, 128:256] = jnp.where(lo, pltpu.roll(o[2 * CHUNK:3 * CHUNK], HEAD_DIM, axis=1),
                                        o[3 * CHUNK:4 * CHUNK])

        vn = _layer_norm(vb_ref[...], vec[0:1], vec[1:2]).astype(BF16)
        sg = jnp.dot(ws_ref[...], _stack_heads(vn, masks), preferred_element_type=F32) + bias_ref[...]
        mix_ref[:, GW:2 * GW] = ub_ref[...] * sg

        dec = dec_ref[...]
        o_f = _retention_chunk(qc_ref[...], kc_ref[...], vc_ref[...], dmat_ref[0],
                               dec[0], dec[2], vec[5:6], sf_scr, masks, bd_mask)
        o_b = ob_scr[pl.ds(pl.multiple_of(c * CHUNK, CHUNK), CHUNK), :]
        gmat = gmat_ref[...]
        mix_ref[:, 2 * GW:3 * GW] = (_silu(gf_ref[...]) * (_group_norm(o_f, gmat) * vec[3:4])
                                     + _silu(gb_ref[...]) * (_group_norm(o_b, gmat) * vec[4:5]))
        if not latent:
            @pl.when(c == nc - 1)
            def _():
                _store_state(st_ref, 0, sf_scr)

        pd = pd_ref[...]
        pext_scr[0:POOL_HALO, :] = jnp.where(c > 0, pdp_ref[...], 0.0)
        pext_scr[POOL_HALO:POOL_HALO + CHUNK, :] = pd
        pext_scr[POOL_HALO + CHUNK:2 * POOL_HALO + CHUNK, :] = jnp.where(c < nc - 1, pdn_ref[...], 0.0)

        def win(d, half):
            return pext_scr[pl.ds(POOL_HALO + d, CHUNK), half * 128:(half + 1) * 128]

        a2 = win(-1, 0) + win(0, 0)
        a4 = a2 + win(-2, 0) + win(1, 0)
        a8 = win(-4, 1)
        for d in range(-3, 4):
            a8 = a8 + win(d, 1)
        a16 = a8
        for d in list(range(-8, -4)) + list(range(4, 8)):
            a16 = a16 + win(d, 1)
        sums = jnp.concatenate([jnp.where(lo, a2, a4), jnp.where(lo, a8, a16)], axis=1)
        yd = sums * cnt_ref[...] - pd
        mix_ref[:, 3 * GW:4 * GW] = _bdot(yd, wpool_ref[...]) * vec[2:3]


def _mixer(z, tabs, l, latent, extra=None):
    nb = DEC_BATCH if latent else BATCH
    nc = (DEC_SEQ if latent else SEQ) // CHUNK
    base = (N_CTX // CHUNK) if latent else 0
    last_halo = ROWS // POOL_HALO - 1
    per8 = CHUNK // POOL_HALO

    def fwd(b, p, c):
        return base + b * nc + c * p

    def both(b, p, c):
        return base + b * nc + jnp.where(p == 0, nc - 1 - c, c)

    def col(width, idx, rowmap):
        return pl.BlockSpec((CHUNK, width), lambda b, p, c: (rowmap(b, p, c), idx))

    def const(shape):
        return pl.BlockSpec(shape, lambda b, p, c: (0,) * len(shape))

    def layer(shape):
        return pl.BlockSpec((None,) + shape, lambda b, p, c: (l,) + (0,) * len(shape))

    specs, args = [], []

    def add(spec, arr):
        specs.append(spec)
        args.append(arr)

    add(col(GW, 0, fwd), z)
    if latent:
        add(pl.BlockSpec((CHUNK, 128), lambda b, p, c: (base + b * nc + (nc - 1 - c) * (1 - p), 2)), z)
        add(pl.BlockSpec((CHUNK, 128), lambda b, p, c: (base + b * nc + (nc - 1 - c) * (1 - p), 3)), z)
        add(pl.BlockSpec((None, None, PAST_LEN, 128), lambda b, p, c: (b, l, 0, 0)), extra["ck"])
        add(pl.BlockSpec((None, None, PAST_LEN, 128), lambda b, p, c: (b, l, 0, 0)), extra["cv"])
    else:
        add(pl.BlockSpec((SEQ, 128), lambda b, p, c: (b, 2)), z)
        add(pl.BlockSpec((SEQ, 128), lambda b, p, c: (b, 3)), z)
    add(col(GW, 2, fwd), z)
    add(col(GW, 3, fwd), z)
    add(col(GW, 4, both), z)
    add(col(GW, 5, both), z)
    add(col(GW, 6, both), z)
    add(col(GW, 7, fwd), z)
    add(col(GW, 8, fwd), z)
    add(col(GW, 9, fwd), z)
    add(pl.BlockSpec((POOL_HALO, GW),
                     lambda b, p, c: (jnp.maximum(fwd(b, p, c) * per8 - 1, 0), 9)), z)
    add(pl.BlockSpec((POOL_HALO, GW),
                     lambda b, p, c: (jnp.minimum((fwd(b, p, c) + 1) * per8, last_halo), 9)), z)
    if latent:
        rope_map = lambda b, p, c: (jnp.where(p == 0, nc - 1 - c, c), 0)
        add(pl.BlockSpec((CHUNK, 128), rope_map), extra["cos"])
        add(pl.BlockSpec((CHUNK, 128), rope_map), extra["sin"])
        add(pl.BlockSpec((None, None, 2, N_HEADS, HEAD_DIM, HEAD_DIM),
                         lambda b, p, c: (b, l, 0, 0, 0, 0)), extra["s0"])
    add(layer((2, N_HEADS * CHUNK, CHUNK)), tabs["dmat"])
    add(layer((4, CHUNK, GW)), tabs["dec"])
    add(layer((N_HEADS * CHUNK, 128)), tabs["sink"])
    add(layer((CHUNK, N_HEADS * CHUNK)), tabs["ws"])
    add(layer((CHUNK, GW)), tabs["bias"])
    add(layer((8, GW)), tabs["vec"])
    add(const((GW, GW)), tabs["gmat"])
    add(pl.BlockSpec((CHUNK, GW), lambda b, p, c: (c * p, 0)), tabs["cnt_lat"] if latent else tabs["cnt_ctx"])
    add(layer((GW, GW)), tabs["wpool"])

    out_shape = [jax.ShapeDtypeStruct((nb * nc * CHUNK, D_MODEL), F32)]
    out_specs = [pl.BlockSpec((CHUNK, D_MODEL), lambda b, p, c: (b * nc + c * p, 0))]
    scratch = [pltpu.VMEM((GW, GW), F32), pltpu.VMEM((GW, GW), F32),
               pltpu.VMEM((nc * CHUNK, GW), F32),
               pltpu.VMEM((CHUNK + 2 * POOL_HALO, GW), F32)]
    if latent:
        scratch += [pltpu.VMEM(((nc + 2) * CHUNK, 128), BF16), pltpu.VMEM(((nc + 2) * CHUNK, 128), BF16)]
    else:
        out_shape.append(jax.ShapeDtypeStruct((nb, 2, N_HEADS, HEAD_DIM, HEAD_DIM), F32))
        out_specs.append(pl.BlockSpec((None, 2, N_HEADS, HEAD_DIM, HEAD_DIM), lambda b, p, c: (b, 0, 0, 0, 0)))
        for _ in range(2):
            out_shape.append(jax.ShapeDtypeStruct((nb, SEQ, 128), F32))
            out_specs.append(pl.BlockSpec((None, SEQ, 128), lambda b, p, c: (b, 0, 0)))

    return pl.pallas_call(
        functools.partial(_mixer_kernel, latent=latent, nc=nc),
        grid=(nb, 2, nc),
        in_specs=specs,
        out_specs=out_specs,
        out_shape=out_shape,
        scratch_shapes=scratch,
        compiler_params=pltpu.CompilerParams(
            dimension_semantics=("arbitrary", "arbitrary", "arbitrary"), vmem_limit_bytes=VMEM_LIMIT),
        name="mixer_latent" if latent else "mixer_context",
    )(*args)


def _pad_rows(rows, n=8):
    a = jnp.stack(rows)
    return jnp.concatenate([a, jnp.zeros((n - a.shape[0],) + a.shape[1:], a.dtype)], axis=0)


def _block_diag(blocks):
    g, n, _ = blocks.shape
    eye = jnp.eye(g, dtype=blocks.dtype)
    return (eye[:, None, :, None] * blocks[:, :, None, :]).reshape(g * n, g * n)


def _inv_count(n):
    t = np.arange(n)
    cols = []
    for w in POOL_WINDOWS:
        cnt = np.clip(t + w // 2, 0, n) - np.clip(t - w // 2, 0, n)
        cols.append(np.repeat((1.0 / cnt)[:, None], HEAD_DIM, axis=1))
    return jnp.asarray(np.concatenate(cols, axis=1), F32)


def _rope_tables():
    rows = DEC_SEQ // GRID_W
    r, cc = jnp.meshgrid(jnp.arange(rows), jnp.arange(GRID_W), indexing="ij")
    half = HEAD_DIM // 2
    freqs = ROPE_BASE ** (-jnp.arange(0, half, 2, dtype=F32) / half)

    def tables(pos):
        ang = pos.reshape(-1).astype(F32)[:, None] * freqs[None, :]
        cos, sin = jnp.cos(ang), jnp.sin(ang)
        return jnp.concatenate([cos, cos], axis=1), jnp.concatenate([-sin, sin], axis=1)

    cr, sr = tables(r)
    ccol, scol = tables(cc)
    cos = jnp.concatenate([cr, ccol], axis=1)
    sin = jnp.concatenate([sr, scol], axis=1)
    return jnp.tile(cos, (1, 2)), jnp.tile(sin, (1, 2))


def _layer_tables(attn_sink, sgu_norm_w, sgu_norm_b, sgu_ws, sgu_bs, ret_decay, ret_gn_w, pool_w, pool_scale):
    log_g = jax.nn.log_sigmoid(ret_decay.astype(F32))
    i = jnp.arange(CHUNK, dtype=F32)
    rel = i[:, None] - i[None, :]
    kscale = HEAD_DIM ** -0.5
    d_f = jnp.where(rel >= 0, jnp.exp(jnp.maximum(rel, 0.0)[None] * log_g[0][:, None, None]), 0.0)
    d_b = jnp.where(rel <= 0, jnp.exp(jnp.maximum(-rel, 0.0)[None] * log_g[1][:, None, None]), 0.0)
    dmat = jnp.stack([d_f.reshape(N_HEADS * CHUNK, CHUNK), d_b.reshape(N_HEADS * CHUNK, CHUNK)]) * kscale

    def lanes(per_head):
        return jnp.repeat(per_head, HEAD_DIM, axis=1)

    qdec_f = lanes(jnp.exp((i + 1.0)[:, None] * log_g[0][None, :]))
    qdec_b = lanes(jnp.exp((CHUNK - i)[:, None] * log_g[1][None, :]))
    kdec_f = lanes(jnp.exp((CHUNK - 1.0 - i)[:, None] * log_g[0][None, :])) * kscale
    kdec_b = lanes(jnp.exp(i[:, None] * log_g[1][None, :])) * kscale
    cdec = jnp.repeat(jnp.exp(CHUNK * log_g), HEAD_DIM, axis=1)
    vec = _pad_rows([sgu_norm_w, sgu_norm_b, pool_scale, ret_gn_w[0], ret_gn_w[1], cdec[0], cdec[1]])
    return {
        "dmat": dmat,
        "dec": jnp.stack([qdec_f, qdec_b, kdec_f, kdec_b]),
        "sink": jnp.broadcast_to(jnp.repeat(attn_sink, CHUNK)[:, None], (N_HEADS * CHUNK, 128)),
        "ws": jnp.concatenate([sgu_ws[h] for h in range(N_HEADS)], axis=1).astype(BF16),
        "bias": jnp.repeat(sgu_bs.T, HEAD_DIM, axis=1),
        "vec": vec,
        "wpool": _block_diag(pool_w).astype(BF16),
    }


def kernel(x_prompt, x_sample, cache_attn_k, cache_attn_v, state_ret, c, c_ctx, w_ada, b_ada, w_in,
           w_out, attn_sink, sgu_norm_w, sgu_norm_b, sgu_ws, sgu_bs, ret_decay, ret_gn_w, pool_w,
           pool_scale, ffn_up, ffn_conv_w, ffn_conv_b, ffn_down, ln_w, ln_b):
    cond8 = jnp.concatenate([c_ctx[None], c, jnp.zeros((8 - 1 - DEC_BATCH, D_MODEL), F32)], axis=0)
    mod = _modulation(cond8, w_ada, b_ada).reshape(DEPTH, 8, 6, D_MODEL)

    tabs = jax.vmap(_layer_tables)(attn_sink, sgu_norm_w, sgu_norm_b, sgu_ws, sgu_bs, ret_decay, ret_gn_w,
                                   pool_w, pool_scale)
    tabs["gmat"] = _block_diag(jnp.full((N_HEADS, HEAD_DIM, HEAD_DIM), 1.0 / HEAD_DIM, F32)).astype(BF16)
    tabs["cnt_ctx"] = _inv_count(SEQ)
    tabs["cnt_lat"] = _inv_count(DEC_SEQ)
    cos, sin = _rope_tables()
    extra = {"ck": cache_attn_k.reshape(DEC_BATCH, DEPTH, PAST_LEN, 128),
             "cv": cache_attn_v.reshape(DEC_BATCH, DEPTH, PAST_LEN, 128),
             "cos": cos, "sin": sin, "s0": state_ret}
    ln8 = jnp.concatenate([ln_w, ln_b, jnp.zeros((DEPTH, 4, D_MODEL), F32)], axis=1)
    conv8 = jnp.concatenate([ffn_conv_w, ffn_conv_b[:, None], jnp.zeros((DEPTH, 4, 2 * D_FF), F32)], axis=1)

    down_bf16 = _to_bf16(ffn_down, D_FF // 4)
    w_in_bf16 = _to_bf16(w_in, D_MODEL // 2)
    w_out_bf16 = _to_bf16(w_out, D_MODEL)

    xs = [x_prompt.reshape(N_CTX, D_MODEL), x_sample.reshape(N_LAT, D_MODEL)]
    new_k, new_v, new_s = [], [], []
    for l in range(DEPTH):
        z = _inproj(xs, mod, w_in_bf16, l)
        mix_ctx, st, kn, vn = _mixer(z, tabs, l, latent=False)
        (mix_lat,) = _mixer(z, tabs, l, latent=True, extra=extra)
        x1 = _outproj([mix_ctx, mix_lat], xs, mod, w_out_bf16, ln8, l)
        xs = _ffn(x1, mod, ffn_up, conv8, down_bf16, ln8, l, split_out=(l == DEPTH - 1))
        new_k.append(kn.reshape(BATCH, SEQ, 2, HEAD_DIM))
        new_v.append(vn.reshape(BATCH, SEQ, 2, HEAD_DIM))
        new_s.append(st)

    y_prompt = xs[0].reshape(BATCH, SEQ, D_MODEL)
    y_sample = xs[1].reshape(DEC_BATCH, DEC_SEQ, D_MODEL)
    return (y_prompt, y_sample, jnp.stack(new_k, axis=1), jnp.stack(new_v, axis=1),
            jnp.stack(new_s, axis=1))
```

```python
import functools

import numpy as np
import jax
import jax.numpy as jnp
from jax import lax
from jax.experimental import pallas as pl
from jax.experimental.pallas import tpu as pltpu

F32 = jnp.float32
BF16 = jnp.bfloat16

D_MODEL = 1024
BATCH = 16
SEQ = 256
DEPTH = 2
DEC_BATCH = 2
DEC_SEQ = 2048
PAST_LEN = 256
GRID_W = 64
CHUNK = 128
HEAD_DIM = 64
GW = D_MODEL // 4
N_HEADS = 4
POOL_WINDOWS = (2, 4, 8, 16)
POOL_HALO = 8
D_FF = 2816
ROPE_BASE = 10000.0
LN_EPS = 1e-5
NEG_INF = -1e30
IN_WIDTH = 10 * GW
ALPHA = (2.0 * DEPTH) ** 0.25

N_CTX = BATCH * SEQ
N_LAT = DEC_BATCH * DEC_SEQ
ROWS = N_CTX + N_LAT

TM = 1024
CTX_TILES = N_CTX // TM
LAT_TILES_PER_SEQ = DEC_SEQ // TM
NB_IN = 512
FC = 256
NB_ADA = 1536
VMEM_LIMIT = 56 * 1024 * 1024


def _cond_of_tile(i, tm=TM):
    ctx_tiles = N_CTX // tm
    return jnp.where(i < ctx_tiles, 0, 1 + (i - ctx_tiles) // (DEC_SEQ // tm))


def _tile_specs(n_src, tm, grid_rank):
    ctx_tiles = N_CTX // tm
    if n_src == 1:
        rows = [lambda i: i]
    else:
        rows = [lambda i: jnp.minimum(i, ctx_tiles - 1), lambda i: jnp.maximum(i - ctx_tiles, 0)]
    if grid_rank == 1:
        return [pl.BlockSpec((tm, D_MODEL), lambda i, f=f: (f(i), 0)) for f in rows]
    return [pl.BlockSpec((tm, D_MODEL), lambda i, j, f=f: (f(i), 0)) for f in rows]


def _per_half(i, tm, n_max, fn):
    if n_max == 1:
        fn(0)
        return
    ctx_tiles = N_CTX // tm

    @pl.when(i < ctx_tiles)
    def _():
        fn(0)

    @pl.when(i >= ctx_tiles)
    def _():
        fn(1)


def _layer_norm(x, w, b):
    mu = jnp.mean(x, axis=-1, keepdims=True)
    d = x - mu
    var = jnp.mean(d * d, axis=-1, keepdims=True)
    return d * lax.rsqrt(var + LN_EPS) * w + b


def _silu(x):
    return x * jax.nn.sigmoid(x)


def _bdot(a, b):
    return jnp.dot(a.astype(BF16), b.astype(BF16), preferred_element_type=F32)


def _mod_kernel(c_ref, w_ref, b_ref, o_ref):
    o_ref[...] = _bdot(_silu(c_ref[...]), w_ref[...]) + b_ref[...]


def _modulation(cond8, w_ada, b_ada):
    return pl.pallas_call(
        _mod_kernel,
        grid=(DEPTH, 6 * D_MODEL // NB_ADA),
        in_specs=[
            pl.BlockSpec((8, D_MODEL), lambda l, j: (0, 0)),
            pl.BlockSpec((None, D_MODEL, NB_ADA), lambda l, j: (l, 0, j)),
            pl.BlockSpec((None, 1, NB_ADA), lambda l, j: (l, 0, j)),
        ],
        out_specs=pl.BlockSpec((None, 8, NB_ADA), lambda l, j: (l, 0, j)),
        out_shape=jax.ShapeDtypeStruct((DEPTH, 8, 6 * D_MODEL), F32),
        compiler_params=pltpu.CompilerParams(
            dimension_semantics=("arbitrary", "arbitrary"), vmem_limit_bytes=VMEM_LIMIT),
        name="modulation",
    )(cond8, w_ada, b_ada.reshape(DEPTH, 1, 6 * D_MODEL))


def _cast_kernel(w_ref, o_ref):
    o_ref[...] = w_ref[...].astype(BF16)


def _to_bf16(w, block_rows):
    depth, rows, cols = w.shape
    return pl.pallas_call(
        _cast_kernel,
        grid=(depth, rows // block_rows),
        in_specs=[pl.BlockSpec((None, block_rows, cols), lambda l, r: (l, r, 0))],
        out_specs=pl.BlockSpec((None, block_rows, cols), lambda l, r: (l, r, 0)),
        out_shape=jax.ShapeDtypeStruct(w.shape, BF16),
        compiler_params=pltpu.CompilerParams(dimension_semantics=("arbitrary", "arbitrary")),
        name="cast_bf16",
    )(w)


def _to_bf16_chunks(w, chunk):
    depth, rows, cols = w.shape
    return pl.pallas_call(
        _cast_kernel,
        grid=(depth, cols // chunk),
        in_specs=[pl.BlockSpec((None, rows, chunk), lambda l, c: (l, 0, c))],
        out_specs=pl.BlockSpec((None, None, rows, chunk), lambda l, c: (l, c, 0, 0)),
        out_shape=jax.ShapeDtypeStruct((depth, cols // chunk, rows, chunk), BF16),
        compiler_params=pltpu.CompilerParams(dimension_semantics=("arbitrary", "arbitrary")),
        name="cast_bf16_chunks",
    )(w)


def _inproj_kernel(*refs, n_x):
    x_refs = refs[:n_x]
    mod_ref, w_ref, z_ref, h_scr = refs[n_x:]

    m = mod_ref[...]

    def build(side):
        h_scr[...] = (x_refs[side][...] * (1.0 + m[1:2]) + m[0:1]).astype(BF16)

    _per_half(pl.program_id(0), TM, n_x, build)
    h = h_scr[...]
    for jb in range(IN_WIDTH // NB_IN):
        cols = slice(jb * NB_IN, (jb + 1) * NB_IN)
        z_ref[:, cols] = jnp.dot(h, w_ref[:, cols], preferred_element_type=F32)


def _inproj(xs, mod, w_in_bf16, l):
    return pl.pallas_call(
        functools.partial(_inproj_kernel, n_x=len(xs)),
        grid=(ROWS // TM,),
        in_specs=_tile_specs(len(xs), TM, 1) + [
            pl.BlockSpec((None, None, 6, D_MODEL), lambda i: (l, _cond_of_tile(i), 0, 0)),
            pl.BlockSpec((None, D_MODEL, IN_WIDTH), lambda i: (l, 0, 0), pipeline_mode=pl.Buffered(1)),
        ],
        out_specs=pl.BlockSpec((TM, IN_WIDTH), lambda i: (i, 0)),
        out_shape=jax.ShapeDtypeStruct((ROWS, IN_WIDTH), F32),
        scratch_shapes=[pltpu.VMEM((TM, D_MODEL), BF16)],
        compiler_params=pltpu.CompilerParams(
            dimension_semantics=("arbitrary",), vmem_limit_bytes=VMEM_LIMIT),
        name="inproj",
    )(*xs, mod, w_in_bf16)


TM_OUT = 512


def _outproj_kernel(*refs, n_x):
    mix_refs = refs[:2]
    x_refs = refs[2:2 + n_x]
    mod_ref, w_ref, ln_ref, o_ref = refs[2 + n_x:]
    m = mod_ref[...]
    ln = ln_ref[...]

    def body(side):
        y = jnp.dot(mix_refs[side][...].astype(BF16), w_ref[...], preferred_element_type=F32)
        x = x_refs[min(side, n_x - 1)][...]
        o_ref[...] = _layer_norm(ALPHA * x + m[2:3] * y, ln[0:1], ln[2:3])

    _per_half(pl.program_id(0), TM_OUT, 2, body)


def _outproj(mixes, xs, mod, w_out, ln8, l):
    return pl.pallas_call(
        functools.partial(_outproj_kernel, n_x=len(xs)),
        grid=(ROWS // TM_OUT,),
        in_specs=_tile_specs(2, TM_OUT, 1) + _tile_specs(len(xs), TM_OUT, 1) + [
            pl.BlockSpec((None, None, 6, D_MODEL), lambda i: (l, _cond_of_tile(i, TM_OUT), 0, 0)),
            pl.BlockSpec((None, D_MODEL, D_MODEL), lambda i: (l, 0, 0)),
            pl.BlockSpec((None, 8, D_MODEL), lambda i: (l, 0, 0)),
        ],
        out_specs=pl.BlockSpec((TM_OUT, D_MODEL), lambda i: (i, 0)),
        out_shape=jax.ShapeDtypeStruct((ROWS, D_MODEL), F32),
        compiler_params=pltpu.CompilerParams(
            dimension_semantics=("arbitrary",), vmem_limit_bytes=VMEM_LIMIT),
        name="outproj",
    )(*mixes, *xs, mod, w_out, ln8)


SEG = TM // 8
HALO_ROWS = 16
RB = 256
RBU = 512
GB = 64
NFC = D_FF // FC
LANE_BLOCKS = D_MODEL // 128
SEG_PITCH = SEG + 8


def _seg_rows(xc_ref, k):
    return jnp.concatenate([xc_ref[cb, pl.ds(k, 8, stride=SEG_PITCH), :] for cb in range(LANE_BLOCKS)], axis=1)


def _ffn_kernel(*refs, n_out):
    (x_ref, xp_ref, xn_ref, mod_ref, upa_ref, upg_ref, cva_ref, cvg_ref, dn_ref, ln_ref) = refs[:10]
    o_refs = refs[10:10 + n_out]
    h_scr, act_scr, xc_scr, u0_scr, u1_scr, wbf_scr = refs[10 + n_out:]
    u_scrs = (u0_scr, u1_scr)
    i = pl.program_id(0)
    j = pl.program_id(1)
    is_ctx = i < CTX_TILES
    lat_pos = (i - CTX_TILES) % LAT_TILES_PER_SEQ

    def build_h():
        m = mod_ref[...]
        scale = 1.0 + m[4:5]
        shift = m[3:4]
        for cb in range(LANE_BLOCKS):
            for s in range(8):
                xc_scr[cb, s * SEG_PITCH:s * SEG_PITCH + SEG, :] = x_ref[s * SEG:(s + 1) * SEG,
                                                                         cb * 128:(cb + 1) * 128]
        for k in range(0, SEG, 2):
            rows = jnp.concatenate([_seg_rows(xc_scr, k), _seg_rows(xc_scr, k + 1)], axis=0)
            h_scr[8 * k:8 * k + 16, :] = (rows * scale + shift).astype(BF16)
        sub = lax.broadcasted_iota(jnp.int32, (HALO_ROWS, D_MODEL), 0)
        prev_ok = jnp.logical_not(is_ctx) & (lat_pos > 0)
        next_ok = jnp.logical_not(is_ctx) & (lat_pos < LAT_TILES_PER_SEQ - 1)
        halo_x = jnp.where(sub == 0, xp_ref[POOL_HALO - 1:POOL_HALO, :], xn_ref[0:1, :])
        keep = ((sub == 0) & prev_ok) | ((sub == 1) & next_ok)
        h_scr[TM:TM + HALO_ROWS, :] = jnp.where(keep, halo_x * scale + shift, 0.0).astype(BF16)

    n_ub = TM // RBU

    def cast_up_weights():
        wbf_scr[:, 0:FC] = upa_ref[...].astype(BF16)
        wbf_scr[:, FC:2 * FC] = upg_ref[...].astype(BF16)

    def up_proj(slot, ub):
        rows = slice(ub * RBU, (ub + 1) * RBU + (HALO_ROWS if ub == n_ub - 1 else 0))
        u_scrs[slot][rows, :] = jnp.dot(h_scr[rows, :], wbf_scr[...], preferred_element_type=F32)

    def conv(u_ref, lanes, cv, r0):
        lo = max(r0 - 8, 0)
        hi = min(r0 + GB + 8, TM)
        ue = u_ref[lo:hi, lanes]
        u = ue[r0 - lo:r0 - lo + GB]
        if r0 == 0 or r0 == TM - GB:
            sub = lax.broadcasted_iota(jnp.int32, (8, FC), 0)
            seg_per_seq = SEQ // SEG
        if r0 == 0:
            ctx_first = is_ctx & (sub % seg_per_seq == 0)
            b_first = jnp.where(sub == 0, u_ref[TM:TM + 1, lanes],
                                pltpu.roll(u_ref[TM - 8:TM, lanes], 1, axis=0))
            um1 = jnp.concatenate([jnp.where(ctx_first, 0.0, b_first), u[0:GB - 8]], axis=0)
        else:
            um1 = ue[0:GB]
        if r0 == TM - GB:
            ctx_last = is_ctx & (sub % seg_per_seq == seg_per_seq - 1)
            b_last = jnp.where(sub == 7, u_ref[TM + 1:TM + 2, lanes],
                               pltpu.roll(u_ref[0:8, lanes], 7, axis=0))
            up1 = jnp.concatenate([u[8:GB], jnp.where(ctx_last, 0.0, b_last)], axis=0)
        else:
            up1 = ue[r0 - lo + 8:r0 - lo + GB + 8]
        return um1 * cv[0:1] + u * cv[1:2] + up1 * cv[2:3] + cv[3:4]

    def gate(slot, chunk, ub):
        for r0 in range(ub * RBU, (ub + 1) * RBU, GB):
            a = conv(u_scrs[slot], slice(0, FC), cva_ref[...], r0)
            g = conv(u_scrs[slot], slice(FC, 2 * FC), cvg_ref[...], r0)
            act_scr[chunk, r0:r0 + GB, :] = (_silu(a) * g).astype(BF16)

    @pl.when(j == 0)
    def _():
        build_h()
        cast_up_weights()
        for ub in range(n_ub):
            up_proj(0, ub)

    for slot in range(2):
        @pl.when((j >= 1) & (j < NFC) & (j % 2 == slot))
        def _(slot=slot):
            cast_up_weights()
            for ub in range(n_ub):
                up_proj(slot, ub)
                gate(1 - slot, j - 1, ub)

    @pl.when(j == NFC)
    def _():
        for ub in range(n_ub):
            gate((NFC - 1) % 2, NFC - 1, ub)
        m = mod_ref[...]
        ln = ln_ref[...]
        for rb in range(TM // RB):
            vrows = range(rb * RB // 8, (rb + 1) * RB // 8)
            lhs = jnp.concatenate([act_scr[jj, rb * RB:(rb + 1) * RB, :] for jj in range(NFC)], axis=1)
            y = jnp.dot(lhs, dn_ref[...], preferred_element_type=F32)
            xr = jnp.concatenate([_seg_rows(xc_scr, k) for k in vrows], axis=0)
            out = _layer_norm(ALPHA * xr + m[5:6] * y, ln[1:2], ln[3:4])
            for kk, k in enumerate(vrows):
                for cb in range(LANE_BLOCKS):
                    xc_scr[cb, pl.ds(k, 8, stride=SEG_PITCH), :] = out[8 * kk:8 * kk + 8,
                                                                       cb * 128:(cb + 1) * 128]
        def write_out(side):
            for cb in range(LANE_BLOCKS):
                for s in range(8):
                    o_refs[side][s * SEG:(s + 1) * SEG, cb * 128:(cb + 1) * 128] = xc_scr[
                        cb, s * SEG_PITCH:s * SEG_PITCH + SEG, :]

        _per_half(i, TM, n_out, write_out)


def _ffn(x, mod, up, conv8, down, ln8, l, split_out):
    halo_blocks = TM // POOL_HALO
    last_halo = ROWS // POOL_HALO - 1

    def up_chunk(j):
        return jnp.minimum(j, NFC - 1)

    def gate_chunk(j):
        return jnp.maximum(j - 1, 0)

    if split_out:
        out_specs = _tile_specs(2, TM, 2)
        out_shape = [jax.ShapeDtypeStruct((N_CTX, D_MODEL), F32), jax.ShapeDtypeStruct((N_LAT, D_MODEL), F32)]
    else:
        out_specs = _tile_specs(1, TM, 2)
        out_shape = [jax.ShapeDtypeStruct((ROWS, D_MODEL), F32)]
    return pl.pallas_call(
        functools.partial(_ffn_kernel, n_out=len(out_shape)),
        grid=(ROWS // TM, NFC + 1),
        in_specs=[
            pl.BlockSpec((TM, D_MODEL), lambda i, j: (i, 0)),
            pl.BlockSpec((POOL_HALO, D_MODEL), lambda i, j: (jnp.maximum(i * halo_blocks - 1, 0), 0)),
            pl.BlockSpec((POOL_HALO, D_MODEL),
                         lambda i, j: (jnp.minimum((i + 1) * halo_blocks, last_halo), 0)),
            pl.BlockSpec((None, None, 6, D_MODEL), lambda i, j: (l, _cond_of_tile(i), 0, 0)),
            pl.BlockSpec((None, D_MODEL, FC), lambda i, j: (l, 0, up_chunk(j))),
            pl.BlockSpec((None, D_MODEL, FC), lambda i, j: (l, 0, NFC + up_chunk(j))),
            pl.BlockSpec((None, 8, FC), lambda i, j: (l, 0, gate_chunk(j))),
            pl.BlockSpec((None, 8, FC), lambda i, j: (l, 0, NFC + gate_chunk(j))),
            pl.BlockSpec((None, D_FF, D_MODEL), lambda i, j: (l, 0, 0), pipeline_mode=pl.Buffered(1)),
            pl.BlockSpec((None, 8, D_MODEL), lambda i, j: (l, 0, 0)),
        ],
        out_specs=out_specs,
        out_shape=out_shape,
        scratch_shapes=[pltpu.VMEM((TM + HALO_ROWS, D_MODEL), BF16),
                        pltpu.VMEM((NFC, TM, FC), BF16),
                        pltpu.VMEM((LANE_BLOCKS, 8 * SEG_PITCH, 128), F32),
                        pltpu.VMEM((TM + HALO_ROWS, 2 * FC), F32),
                        pltpu.VMEM((TM + HALO_ROWS, 2 * FC), F32),
                        pltpu.VMEM((D_MODEL, 2 * FC), BF16)],
        compiler_params=pltpu.CompilerParams(
            dimension_semantics=("arbitrary", "arbitrary"), vmem_limit_bytes=VMEM_LIMIT),
        name="convffn",
    )(x, x, x, mod, up, up, conv8, conv8, down, ln8)


def _head_masks(width):
    lane = lax.broadcasted_iota(jnp.int32, (1, width), 1)
    return [(lane >= h * HEAD_DIM) & (lane < (h + 1) * HEAD_DIM) for h in range(width // HEAD_DIM)]


def _stack_heads(x, masks):
    return jnp.concatenate([jnp.where(m, x, jnp.zeros_like(x)) for m in masks], axis=0)


def _rope(x, cos, sin):
    lane = lax.broadcasted_iota(jnp.int32, (1, 128), 1)
    lower = (lane & 31) < 16
    outs = []
    for k in range(x.shape[1] // 128):
        xb = x[:, k * 128:(k + 1) * 128]
        partner = jnp.where(lower, pltpu.roll(xb, 112, axis=1), pltpu.roll(xb, 16, axis=1))
        outs.append(xb * cos + partner * sin)
    return outs[0] if len(outs) == 1 else jnp.concatenate(outs, axis=1)


def _retention_chunk(q, k, v, dmat, qdec, kdec, cdec, state_scr, masks, bd_mask):
    qb = q.astype(BF16)
    kb = k.astype(BF16)
    vb = v.astype(BF16)
    s = lax.dot_general(_stack_heads(qb, masks), kb, (((1,), (1,)), ((), ())),
                        preferred_element_type=F32)
    p = (s * dmat).astype(BF16)
    p_cat = jnp.concatenate([p[h * CHUNK:(h + 1) * CHUNK] for h in range(N_HEADS)], axis=1)
    state = state_scr[...]
    o = (jnp.dot(p_cat, _stack_heads(vb, masks), preferred_element_type=F32)
         + _bdot(q * qdec, state))
    upd = lax.dot_general((k * kdec).astype(BF16), vb, (((0,), (0,)), ((), ())),
                          preferred_element_type=F32)
    state_scr[...] = state * cdec + jnp.where(bd_mask, upd, 0.0)
    return o


def _load_state(state_scr, blocks_ref, d):
    state_scr[...] = jnp.zeros((GW, GW), F32)
    for h in range(N_HEADS):
        sl = slice(h * HEAD_DIM, (h + 1) * HEAD_DIM)
        state_scr[sl, sl] = blocks_ref[d, h]


def _store_state(blocks_ref, d, state_scr):
    for h in range(N_HEADS):
        sl = slice(h * HEAD_DIM, (h + 1) * HEAD_DIM)
        blocks_ref[d, h] = state_scr[sl, sl]


def _group_norm(o, gmat):
    mu = jnp.dot(o.astype(BF16), gmat, preferred_element_type=F32)
    d = o - mu
    var = jnp.dot((d * d).astype(BF16), gmat, preferred_element_type=F32)
    return d * lax.rsqrt(var + LN_EPS)


def _mixer_kernel_old(*refs, latent, nc):
    if latent:
        (qa_ref, ka_ref, va_ref, kx_ref, vx_ref, ub_ref, vb_ref, qc_ref, kc_ref, vc_ref, gf_ref,
         gb_ref, pd_ref, pdp_ref, pdn_ref, cos_ref, sin_ref, s0_ref, dmat_ref, dec_ref, sink_ref,
         ws_ref, bias_ref, vec_ref, gmat_ref, cnt_ref, wpool_ref,
         mix_ref, sf_scr, sb_scr, ob_scr, pext_scr, k_scr, v_scr) = refs
    else:
        (qa_ref, kx_ref, vx_ref, ub_ref, vb_ref, qc_ref, kc_ref, vc_ref, gf_ref,
         gb_ref, pd_ref, pdp_ref, pdn_ref, dmat_ref, dec_ref, sink_ref,
         ws_ref, bias_ref, vec_ref, gmat_ref, cnt_ref, wpool_ref,
         mix_ref, st_ref, kn_ref, vn_ref, sf_scr, sb_scr, ob_scr, pext_scr) = refs

    p = pl.program_id(1)
    c = pl.program_id(2)
    masks = _head_masks(GW)
    row = lax.broadcasted_iota(jnp.int32, (GW, GW), 0)
    col = lax.broadcasted_iota(jnp.int32, (GW, GW), 1)
    bd_mask = (row // HEAD_DIM) == (col // HEAD_DIM)
    vec = vec_ref[...]

    @pl.when(p == 0)
    def _():
        rc = nc - 1 - c

        @pl.when(c == 0)
        def _():
            if latent:
                _load_state(sb_scr, s0_ref, 1)
                zero_blk = jnp.zeros((CHUNK, 2 * HEAD_DIM), BF16)
                k_scr[0:CHUNK, :] = zero_blk
                v_scr[0:CHUNK, :] = zero_blk
                k_scr[(nc + 1) * CHUNK:(nc + 2) * CHUNK, :] = zero_blk
                v_scr[(nc + 1) * CHUNK:(nc + 2) * CHUNK, :] = zero_blk
            else:
                sb_scr[...] = jnp.zeros((GW, GW), F32)

        if latent:
            dst = pl.ds(pl.multiple_of((rc + 1) * CHUNK, CHUNK), CHUNK)
            k_scr[dst, :] = _rope(ka_ref[...], cos_ref[...], sin_ref[...]).astype(BF16)
            v_scr[dst, :] = va_ref[...].astype(BF16)

        dec = dec_ref[...]
        o_b = _retention_chunk(qc_ref[...], kc_ref[...], vc_ref[...], dmat_ref[1],
                               dec[1], dec[3], vec[6:7], sb_scr, masks, bd_mask)
        ob_scr[pl.ds(pl.multiple_of(rc * CHUNK, CHUNK), CHUNK), :] = o_b

        if not latent:
            @pl.when(c == nc - 1)
            def _():
                _store_state(st_ref, 1, sb_scr)

    @pl.when(p == 1)
    def _():
        @pl.when(c == 0)
        def _():
            if latent:
                _load_state(sf_scr, s0_ref, 0)
            else:
                sf_scr[...] = jnp.zeros((GW, GW), F32)
                kn_ref[...] = kx_ref[...]
                vn_ref[...] = vx_ref[...]

        q = qa_ref[...]
        if latent:
            q = _rope(q, cos_ref[...], sin_ref[...])
        q = q * (HEAD_DIM ** -0.5)
        lane = lax.broadcasted_iota(jnp.int32, (1, 2 * HEAD_DIM), 1)
        lo = lane < HEAD_DIM
        q0, q1 = q[:, :128], q[:, 128:]
        zero = jnp.zeros_like(q0)
        q_st = jnp.concatenate([
            jnp.where(lo, q0, zero),
            jnp.where(lo, pltpu.roll(q0, HEAD_DIM, axis=1), zero),
            jnp.where(lo, zero, pltpu.roll(q1, HEAD_DIM, axis=1)),
            jnp.where(lo, zero, q1)], axis=0).astype(BF16)
        if latent:
            band = pl.ds(pl.multiple_of(c * CHUNK, CHUNK), 3 * CHUNK)
            k_all = jnp.concatenate([k_scr[band, :], kx_ref[...].astype(BF16)], axis=0)
            v_all = jnp.concatenate([v_scr[band, :], vx_ref[...].astype(BF16)], axis=0)
        else:
            k_all = kx_ref[...].astype(BF16)
            v_all = vx_ref[...].astype(BF16)
        s = lax.dot_general(q_st, k_all, (((1,), (1,)), ((), ())), preferred_element_type=F32)
        if latent:
            nk = 3 * CHUNK + PAST_LEN
            qi = lax.broadcasted_iota(jnp.int32, (N_HEADS * CHUNK, nk), 0) & (CHUNK - 1)
            kj = lax.broadcasted_iota(jnp.int32, (N_HEADS * CHUNK, nk), 1)
            kpos = kj + (c - 1) * CHUNK
            valid = (kj >= 3 * CHUNK) | ((kj >= qi) & (kj <= qi + 2 * CHUNK)
                                         & (kpos >= 0) & (kpos < nc * CHUNK))
            s = jnp.where(valid, s, NEG_INF)
        sink = sink_ref[...][:, 0:1]
        mx = jnp.maximum(jnp.max(s, axis=-1, keepdims=True), sink)
        e = jnp.exp(s - mx)
        den = jnp.sum(e, axis=-1, keepdims=True) + jnp.exp(sink - mx)
        o = jnp.dot(e.astype(BF16), v_all, preferred_element_type=F32) / den
        mix_ref[:, 0:128] = jnp.where(lo, o[0:CHUNK], pltpu.roll(o[CHUNK:2 * CHUNK], HEAD_DIM, axis=1))
        mix_ref[:, 128:256] = jnp.where(lo, pltpu.roll(o[2 * CHUNK:3 * CHUNK], HEAD_DIM, axis=1),
                                        o[3 * CHUNK:4 * CHUNK])

        vn = _layer_norm(vb_ref[...], vec[0:1], vec[1:2]).astype(BF16)
        sg = jnp.dot(ws_ref[...], _stack_heads(vn, masks), preferred_element_type=F32) + bias_ref[...]
        mix_ref[:, GW:2 * GW] = ub_ref[...] * sg

        dec = dec_ref[...]
        o_f = _retention_chunk(qc_ref[...], kc_ref[...], vc_ref[...], dmat_ref[0],
                               dec[0], dec[2], vec[5:6], sf_scr, masks, bd_mask)
        o_b = ob_scr[pl.ds(pl.multiple_of(c * CHUNK, CHUNK), CHUNK), :]
        gmat = gmat_ref[...]
        mix_ref[:, 2 * GW:3 * GW] = (_silu(gf_ref[...]) * (_group_norm(o_f, gmat) * vec[3:4])
                                     + _silu(gb_ref[...]) * (_group_norm(o_b, gmat) * vec[4:5]))
        if not latent:
            @pl.when(c == nc - 1)
            def _():
                _store_state(st_ref, 0, sf_scr)

        pd = pd_ref[...]
        pext_scr[0:POOL_HALO, :] = jnp.where(c > 0, pdp_ref[...], 0.0)
        pext_scr[POOL_HALO:POOL_HALO + CHUNK, :] = pd
        pext_scr[POOL_HALO + CHUNK:2 * POOL_HALO + CHUNK, :] = jnp.where(c < nc - 1, pdn_ref[...], 0.0)

        def win(d, half):
            return pext_scr[pl.ds(POOL_HALO + d, CHUNK), half * 128:(half + 1) * 128]

        a2 = win(-1, 0) + win(0, 0)
        a4 = a2 + win(-2, 0) + win(1, 0)
        a8 = win(-4, 1)
        for d in range(-3, 4):
            a8 = a8 + win(d, 1)
        a16 = a8
        for d in list(range(-8, -4)) + list(range(4, 8)):
            a16 = a16 + win(d, 1)
        sums = jnp.concatenate([jnp.where(lo, a2, a4), jnp.where(lo, a8, a16)], axis=1)
        yd = sums * cnt_ref[...] - pd
        mix_ref[:, 3 * GW:4 * GW] = _bdot(yd, wpool_ref[...]) * vec[2:3]


def _mixer_old(z, tabs, l, latent, extra=None):
    nb = DEC_BATCH if latent else BATCH
    nc = (DEC_SEQ if latent else SEQ) // CHUNK
    base = (N_CTX // CHUNK) if latent else 0
    last_halo = ROWS // POOL_HALO - 1
    per8 = CHUNK // POOL_HALO

    def fwd(b, p, c):
        return base + b * nc + c * p

    def both(b, p, c):
        return base + b * nc + jnp.where(p == 0, nc - 1 - c, c)

    def col(width, idx, rowmap):
        return pl.BlockSpec((CHUNK, width), lambda b, p, c: (rowmap(b, p, c), idx))

    def const(shape):
        return pl.BlockSpec(shape, lambda b, p, c: (0,) * len(shape))

    def layer(shape):
        return pl.BlockSpec((None,) + shape, lambda b, p, c: (l,) + (0,) * len(shape))

    specs, args = [], []

    def add(spec, arr):
        specs.append(spec)
        args.append(arr)

    add(col(GW, 0, fwd), z)
    if latent:
        add(pl.BlockSpec((CHUNK, 128), lambda b, p, c: (base + b * nc + (nc - 1 - c) * (1 - p), 2)), z)
        add(pl.BlockSpec((CHUNK, 128), lambda b, p, c: (base + b * nc + (nc - 1 - c) * (1 - p), 3)), z)
        add(pl.BlockSpec((None, None, PAST_LEN, 128), lambda b, p, c: (b, l, 0, 0)), extra["ck"])
        add(pl.BlockSpec((None, None, PAST_LEN, 128), lambda b, p, c: (b, l, 0, 0)), extra["cv"])
    else:
        add(pl.BlockSpec((SEQ, 128), lambda b, p, c: (b, 2)), z)
        add(pl.BlockSpec((SEQ, 128), lambda b, p, c: (b, 3)), z)
    add(col(GW, 2, fwd), z)
    add(col(GW, 3, fwd), z)
    add(col(GW, 4, both), z)
    add(col(GW, 5, both), z)
    add(col(GW, 6, both), z)
    add(col(GW, 7, fwd), z)
    add(col(GW, 8, fwd), z)
    add(col(GW, 9, fwd), z)
    add(pl.BlockSpec((POOL_HALO, GW),
                     lambda b, p, c: (jnp.maximum(fwd(b, p, c) * per8 - 1, 0), 9)), z)
    add(pl.BlockSpec((POOL_HALO, GW),
                     lambda b, p, c: (jnp.minimum((fwd(b, p, c) + 1) * per8, last_halo), 9)), z)
    if latent:
        rope_map = lambda b, p, c: (jnp.where(p == 0, nc - 1 - c, c), 0)
        add(pl.BlockSpec((CHUNK, 128), rope_map), extra["cos"])
        add(pl.BlockSpec((CHUNK, 128), rope_map), extra["sin"])
        add(pl.BlockSpec((None, None, 2, N_HEADS, HEAD_DIM, HEAD_DIM),
                         lambda b, p, c: (b, l, 0, 0, 0, 0)), extra["s0"])
    add(layer((2, N_HEADS * CHUNK, CHUNK)), tabs["dmat"])
    add(layer((4, CHUNK, GW)), tabs["dec"])
    add(layer((N_HEADS * CHUNK, 128)), tabs["sink"])
    add(layer((CHUNK, N_HEADS * CHUNK)), tabs["ws"])
    add(layer((CHUNK, GW)), tabs["bias"])
    add(layer((8, GW)), tabs["vec"])
    add(const((GW, GW)), tabs["gmat"])
    add(pl.BlockSpec((CHUNK, GW), lambda b, p, c: (c * p, 0)), tabs["cnt_lat"] if latent else tabs["cnt_ctx"])
    add(layer((GW, GW)), tabs["wpool"])

    out_shape = [jax.ShapeDtypeStruct((nb * nc * CHUNK, D_MODEL), F32)]
    out_specs = [pl.BlockSpec((CHUNK, D_MODEL), lambda b, p, c: (b * nc + c * p, 0))]
    scratch = [pltpu.VMEM((GW, GW), F32), pltpu.VMEM((GW, GW), F32),
               pltpu.VMEM((nc * CHUNK, GW), F32),
               pltpu.VMEM((CHUNK + 2 * POOL_HALO, GW), F32)]
    if latent:
        scratch += [pltpu.VMEM(((nc + 2) * CHUNK, 128), BF16), pltpu.VMEM(((nc + 2) * CHUNK, 128), BF16)]
    else:
        out_shape.append(jax.ShapeDtypeStruct((nb, 2, N_HEADS, HEAD_DIM, HEAD_DIM), F32))
        out_specs.append(pl.BlockSpec((None, 2, N_HEADS, HEAD_DIM, HEAD_DIM), lambda b, p, c: (b, 0, 0, 0, 0)))
        for _ in range(2):
            out_shape.append(jax.ShapeDtypeStruct((nb, SEQ, 128), F32))
            out_specs.append(pl.BlockSpec((None, SEQ, 128), lambda b, p, c: (b, 0, 0)))

    return pl.pallas_call(
        functools.partial(_mixer_kernel, latent=latent, nc=nc),
        grid=(nb, 2, nc),
        in_specs=specs,
        out_specs=out_specs,
        out_shape=out_shape,
        scratch_shapes=scratch,
        compiler_params=pltpu.CompilerParams(
            dimension_semantics=("arbitrary", "arbitrary", "arbitrary"), vmem_limit_bytes=VMEM_LIMIT),
        name="mixer_latent" if latent else "mixer_context",
    )(*args)


LAT_GROUP = 4


def _chunk_off(c):
    return c * CHUNK if isinstance(c, int) else pl.multiple_of(c * CHUNK, CHUNK)


def _mixer_kernel(*refs, latent, group, ng):
    if latent:
        (qa_ref, ka_ref, va_ref, kx_ref, vx_ref, ub_ref, vb_ref, qc_ref, kc_ref, vc_ref, gf_ref, gb_ref,
         pd_ref, pdp_ref, pdn_ref, cos_ref, sin_ref, s0_ref, dmat_ref, dec_ref, sink_ref, ws_ref, bias_ref,
         vec_ref, gmat_ref, cnt_ref, wpool_ref,
         mix_ref, sf_scr, sb_scr, ob_scr, pext_scr, k_scr, v_scr) = refs
    else:
        (qa_ref, ka_ref, va_ref, ub_ref, vb_ref, qc_ref, kc_ref, vc_ref, gf_ref, gb_ref, pd_ref,
         dmat_ref, dec_ref, sink_ref, ws_ref, bias_ref, vec_ref, gmat_ref, cnt_ref, wpool_ref,
         mix_ref, st_ref, kn_ref, vn_ref, sf_scr, sb_scr, ob_scr, pext_scr) = refs

    nc = group * ng
    masks = _head_masks(GW)
    row = lax.broadcasted_iota(jnp.int32, (GW, GW), 0)
    col = lax.broadcasted_iota(jnp.int32, (GW, GW), 1)
    bd_mask = (row // HEAD_DIM) == (col // HEAD_DIM)
    vec = vec_ref[...]
    lane = lax.broadcasted_iota(jnp.int32, (1, 2 * HEAD_DIM), 1)
    lo = lane < HEAD_DIM

    def rows(k):
        return slice(k * CHUNK, (k + 1) * CHUNK)

    def backward_chunk(k, c):
        if latent:
            dst = pl.ds(_chunk_off(c + 1), CHUNK)
            k_scr[dst, :] = _rope(ka_ref[rows(k), :], cos_ref[rows(k), :], sin_ref[rows(k), :]).astype(BF16)
            v_scr[dst, :] = va_ref[rows(k), :].astype(BF16)
        dec = dec_ref[...]
        o_b = _retention_chunk(qc_ref[rows(k), :], kc_ref[rows(k), :], vc_ref[rows(k), :], dmat_ref[1],
                               dec[1], dec[3], vec[6:7], sb_scr, masks, bd_mask)
        ob_scr[pl.ds(_chunk_off(c), CHUNK), :] = o_b

    def forward_chunk(k, c):
        q = qa_ref[rows(k), :]
        if latent:
            q = _rope(q, cos_ref[rows(k), :], sin_ref[rows(k), :])
        q = q * (HEAD_DIM ** -0.5)
        q0, q1 = q[:, :128], q[:, 128:]
        zero = jnp.zeros_like(q0)
        q_st = jnp.concatenate([
            jnp.where(lo, q0, zero),
            jnp.where(lo, pltpu.roll(q0, HEAD_DIM, axis=1), zero),
            jnp.where(lo, zero, pltpu.roll(q1, HEAD_DIM, axis=1)),
            jnp.where(lo, zero, q1)], axis=0).astype(BF16)
        if latent:
            band = pl.ds(_chunk_off(c), 3 * CHUNK)
            k_all = jnp.concatenate([k_scr[band, :], kx_ref[...].astype(BF16)], axis=0)
            v_all = jnp.concatenate([v_scr[band, :], vx_ref[...].astype(BF16)], axis=0)
        else:
            k_all = ka_ref[...].astype(BF16)
            v_all = va_ref[...].astype(BF16)
        s = lax.dot_general(q_st, k_all, (((1,), (1,)), ((), ())), preferred_element_type=F32)
        if latent:
            nk = 3 * CHUNK + PAST_LEN
            qi = lax.broadcasted_iota(jnp.int32, (N_HEADS * CHUNK, nk), 0) & (CHUNK - 1)
            kj = lax.broadcasted_iota(jnp.int32, (N_HEADS * CHUNK, nk), 1)
            kpos = kj + (c - 1) * CHUNK
            valid = (kj >= 3 * CHUNK) | ((kj >= qi) & (kj <= qi + 2 * CHUNK)
                                         & (kpos >= 0) & (kpos < nc * CHUNK))
            s = jnp.where(valid, s, NEG_INF)
        sink = sink_ref[...][:, 0:1]
        mx = jnp.maximum(jnp.max(s, axis=-1, keepdims=True), sink)
        e = jnp.exp(s - mx)
        den = jnp.sum(e, axis=-1, keepdims=True) + jnp.exp(sink - mx)
        o = jnp.dot(e.astype(BF16), v_all, preferred_element_type=F32) / den
        mix_ref[rows(k), 0:128] = jnp.where(lo, o[0:CHUNK], pltpu.roll(o[CHUNK:2 * CHUNK], HEAD_DIM, axis=1))
        mix_ref[rows(k), 128:256] = jnp.where(lo, pltpu.roll(o[2 * CHUNK:3 * CHUNK], HEAD_DIM, axis=1),
                                              o[3 * CHUNK:4 * CHUNK])

        vn = _layer_norm(vb_ref[rows(k), :], vec[0:1], vec[1:2]).astype(BF16)
        sg = jnp.dot(ws_ref[...], _stack_heads(vn, masks), preferred_element_type=F32) + bias_ref[...]
        mix_ref[rows(k), GW:2 * GW] = ub_ref[rows(k), :] * sg

        dec = dec_ref[...]
        o_f = _retention_chunk(qc_ref[rows(k), :], kc_ref[rows(k), :], vc_ref[rows(k), :], dmat_ref[0],
                               dec[0], dec[2], vec[5:6], sf_scr, masks, bd_mask)
        o_b = ob_scr[pl.ds(_chunk_off(c), CHUNK), :]
        gmat = gmat_ref[...]
        mix_ref[rows(k), 2 * GW:3 * GW] = (
            _silu(gf_ref[rows(k), :]) * (_group_norm(o_f, gmat) * vec[3:4])
            + _silu(gb_ref[rows(k), :]) * (_group_norm(o_b, gmat) * vec[4:5]))

        pd = pd_ref[rows(k), :]
        pext = pext_scr.at[k]
        zeros8 = jnp.zeros((POOL_HALO, GW), F32)
        if k > 0:
            prev8 = pd_ref[k * CHUNK - POOL_HALO:k * CHUNK, :]
        else:
            prev8 = jnp.where(c > 0, pdp_ref[...], 0.0) if latent else zeros8
        if k < group - 1:
            next8 = pd_ref[(k + 1) * CHUNK:(k + 1) * CHUNK + POOL_HALO, :]
        else:
            next8 = jnp.where(c < nc - 1, pdn_ref[...], 0.0) if latent else zeros8
        pext[0:POOL_HALO, :] = prev8
        pext[POOL_HALO:POOL_HALO + CHUNK, :] = pd
        pext[POOL_HALO + CHUNK:2 * POOL_HALO + CHUNK, :] = next8

        def win(d, half):
            return pext[pl.ds(POOL_HALO + d, CHUNK), half * 128:(half + 1) * 128]

        a2 = win(-1, 0) + win(0, 0)
        a4 = a2 + win(-2, 0) + win(1, 0)
        a8 = win(-4, 1)
        for d in range(-3, 4):
            a8 = a8 + win(d, 1)
        a16 = a8
        for d in list(range(-8, -4)) + list(range(4, 8)):
            a16 = a16 + win(d, 1)
        sums = jnp.concatenate([jnp.where(lo, a2, a4), jnp.where(lo, a8, a16)], axis=1)
        yd = sums * cnt_ref[rows(k), :] - pd
        mix_ref[rows(k), 3 * GW:4 * GW] = _bdot(yd, wpool_ref[...]) * vec[2:3]

    if not latent:
        sb_scr[...] = jnp.zeros((GW, GW), F32)
        for k in reversed(range(group)):
            backward_chunk(k, k)
        _store_state(st_ref, 1, sb_scr)
        sf_scr[...] = jnp.zeros((GW, GW), F32)
        kn_ref[...] = ka_ref[...]
        vn_ref[...] = va_ref[...]
        for k in range(group):
            forward_chunk(k, k)
        _store_state(st_ref, 0, sf_scr)
        return

    p = pl.program_id(1)
    g = pl.program_id(2)

    @pl.when(p == 0)
    def _():
        @pl.when(g == 0)
        def _():
            _load_state(sb_scr, s0_ref, 1)
            zero_blk = jnp.zeros((CHUNK, 2 * HEAD_DIM), BF16)
            for scr in (k_scr, v_scr):
                scr[0:CHUNK, :] = zero_blk
                scr[(nc + 1) * CHUNK:(nc + 2) * CHUNK, :] = zero_blk

        for k in reversed(range(group)):
            backward_chunk(k, (ng - 1 - g) * group + k)

    @pl.when(p == 1)
    def _():
        @pl.when(g == 0)
        def _():
            _load_state(sf_scr, s0_ref, 0)

        for k in range(group):
            forward_chunk(k, g * group + k)


def _mixer(z, tabs, l, latent, extra=None):
    nb = DEC_BATCH if latent else BATCH
    nc = (DEC_SEQ if latent else SEQ) // CHUNK
    group = LAT_GROUP if latent else nc
    ng = nc // group
    blk = group * CHUNK
    base = (N_CTX // blk) if latent else 0
    per8 = blk // POOL_HALO
    last_halo = ROWS // POOL_HALO - 1

    def on_grid(f):
        return (lambda b, p, g: f(b, p, g)) if latent else (lambda b: f(b, 1, 0))

    def fwd(b, p, g):
        return base + b * ng + g * p

    def both(b, p, g):
        return base + b * ng + jnp.where(p == 0, ng - 1 - g, g)

    def bwd_only(b, p, g):
        return base + b * ng + (ng - 1 - g) * (1 - p)

    def col(width, idx, rowmap):
        return pl.BlockSpec((blk, width), on_grid(lambda b, p, g: (rowmap(b, p, g), idx)))

    def const(shape):
        return pl.BlockSpec(shape, on_grid(lambda b, p, g: (0,) * len(shape)))

    def layer(shape):
        return pl.BlockSpec((None,) + shape, on_grid(lambda b, p, g: (l,) + (0,) * len(shape)))

    specs, args = [], []

    def add(spec, arr):
        specs.append(spec)
        args.append(arr)

    add(col(GW, 0, fwd), z)
    add(col(128, 2, bwd_only if latent else fwd), z)
    add(col(128, 3, bwd_only if latent else fwd), z)
    if latent:
        add(pl.BlockSpec((None, None, PAST_LEN, 128), lambda b, p, g: (b, l, 0, 0)), extra["ck"])
        add(pl.BlockSpec((None, None, PAST_LEN, 128), lambda b, p, g: (b, l, 0, 0)), extra["cv"])
    add(col(GW, 2, fwd), z)
    add(col(GW, 3, fwd), z)
    add(col(GW, 4, both), z)
    add(col(GW, 5, both), z)
    add(col(GW, 6, both), z)
    add(col(GW, 7, fwd), z)
    add(col(GW, 8, fwd), z)
    add(col(GW, 9, fwd), z)
    if latent:
        add(pl.BlockSpec((POOL_HALO, GW),
                         lambda b, p, g: (jnp.maximum(fwd(b, p, g) * per8 - 1, 0), 9)), z)
        add(pl.BlockSpec((POOL_HALO, GW),
                         lambda b, p, g: (jnp.minimum((fwd(b, p, g) + 1) * per8, last_halo), 9)), z)
        rope_map = lambda b, p, g: (jnp.where(p == 0, ng - 1 - g, g), 0)
        add(pl.BlockSpec((blk, 128), rope_map), extra["cos"])
        add(pl.BlockSpec((blk, 128), rope_map), extra["sin"])
        add(pl.BlockSpec((None, None, 2, N_HEADS, HEAD_DIM, HEAD_DIM),
                         lambda b, p, g: (b, l, 0, 0, 0, 0)), extra["s0"])
    add(layer((2, N_HEADS * CHUNK, CHUNK)), tabs["dmat"])
    add(layer((4, CHUNK, GW)), tabs["dec"])
    add(layer((N_HEADS * CHUNK, 128)), tabs["sink"])
    add(layer((CHUNK, N_HEADS * CHUNK)), tabs["ws"])
    add(layer((CHUNK, GW)), tabs["bias"])
    add(layer((8, GW)), tabs["vec"])
    add(const((GW, GW)), tabs["gmat"])
    add(pl.BlockSpec((blk, GW), on_grid(lambda b, p, g: (g * p, 0))),
        tabs["cnt_lat"] if latent else tabs["cnt_ctx"])
    add(layer((GW, GW)), tabs["wpool"])

    out_shape = [jax.ShapeDtypeStruct((nb * nc * CHUNK, D_MODEL), F32)]
    out_specs = [pl.BlockSpec((blk, D_MODEL), on_grid(lambda b, p, g: (b * ng + g * p, 0)))]
    scratch = [pltpu.VMEM((GW, GW), F32), pltpu.VMEM((GW, GW), F32),
               pltpu.VMEM((nc * CHUNK, GW), F32),
               pltpu.VMEM((group, CHUNK + 2 * POOL_HALO, GW), F32)]
    if latent:
        scratch += [pltpu.VMEM(((nc + 2) * CHUNK, 128), BF16), pltpu.VMEM(((nc + 2) * CHUNK, 128), BF16)]
    else:
        out_shape.append(jax.ShapeDtypeStruct((nb, 2, N_HEADS, HEAD_DIM, HEAD_DIM), F32))
        out_specs.append(pl.BlockSpec((None, 2, N_HEADS, HEAD_DIM, HEAD_DIM), lambda b: (b, 0, 0, 0, 0)))
        for _ in range(2):
            out_shape.append(jax.ShapeDtypeStruct((nb, SEQ, 128), F32))
            out_specs.append(pl.BlockSpec((None, SEQ, 128), lambda b: (b, 0, 0)))

    return pl.pallas_call(
        functools.partial(_mixer_kernel, latent=latent, group=group, ng=ng),
        grid=(nb, 2, ng) if latent else (nb,),
        in_specs=specs,
        out_specs=out_specs,
        out_shape=out_shape,
        scratch_shapes=scratch,
        compiler_params=pltpu.CompilerParams(
            dimension_semantics=("arbitrary",) * (3 if latent else 1), vmem_limit_bytes=VMEM_LIMIT),
        name="mixer_latent" if latent else "mixer_context",
    )(*args)


def _pad_rows(rows, n=8):
    a = jnp.stack(rows)
    return jnp.concatenate([a, jnp.zeros((n - a.shape[0],) + a.shape[1:], a.dtype)], axis=0)


def _block_diag(blocks):
    g, n, _ = blocks.shape
    eye = jnp.eye(g, dtype=blocks.dtype)
    return (eye[:, None, :, None] * blocks[:, :, None, :]).reshape(g * n, g * n)


def _inv_count(n):
    t = np.arange(n)
    cols = []
    for w in POOL_WINDOWS:
        cnt = np.clip(t + w // 2, 0, n) - np.clip(t - w // 2, 0, n)
        cols.append(np.repeat((1.0 / cnt)[:, None], HEAD_DIM, axis=1))
    return jnp.asarray(np.concatenate(cols, axis=1), F32)


def _rope_tables():
    rows = DEC_SEQ // GRID_W
    r, cc = jnp.meshgrid(jnp.arange(rows), jnp.arange(GRID_W), indexing="ij")
    half = HEAD_DIM // 2
    freqs = ROPE_BASE ** (-jnp.arange(0, half, 2, dtype=F32) / half)

    def tables(pos):
        ang = pos.reshape(-1).astype(F32)[:, None] * freqs[None, :]
        cos, sin = jnp.cos(ang), jnp.sin(ang)
        return jnp.concatenate([cos, cos], axis=1), jnp.concatenate([-sin, sin], axis=1)

    cr, sr = tables(r)
    ccol, scol = tables(cc)
    cos = jnp.concatenate([cr, ccol], axis=1)
    sin = jnp.concatenate([sr, scol], axis=1)
    return jnp.tile(cos, (1, 2)), jnp.tile(sin, (1, 2))


def _layer_tables(attn_sink, sgu_norm_w, sgu_norm_b, sgu_ws, sgu_bs, ret_decay, ret_gn_w, pool_w, pool_scale):
    log_g = jax.nn.log_sigmoid(ret_decay.astype(F32))
    i = jnp.arange(CHUNK, dtype=F32)
    rel = i[:, None] - i[None, :]
    kscale = HEAD_DIM ** -0.5
    d_f = jnp.where(rel >= 0, jnp.exp(jnp.maximum(rel, 0.0)[None] * log_g[0][:, None, None]), 0.0)
    d_b = jnp.where(rel <= 0, jnp.exp(jnp.maximum(-rel, 0.0)[None] * log_g[1][:, None, None]), 0.0)
    dmat = jnp.stack([d_f.reshape(N_HEADS * CHUNK, CHUNK), d_b.reshape(N_HEADS * CHUNK, CHUNK)]) * kscale

    def lanes(per_head):
        return jnp.repeat(per_head, HEAD_DIM, axis=1)

    qdec_f = lanes(jnp.exp((i + 1.0)[:, None] * log_g[0][None, :]))
    qdec_b = lanes(jnp.exp((CHUNK - i)[:, None] * log_g[1][None, :]))
    kdec_f = lanes(jnp.exp((CHUNK - 1.0 - i)[:, None] * log_g[0][None, :])) * kscale
    kdec_b = lanes(jnp.exp(i[:, None] * log_g[1][None, :])) * kscale
    cdec = jnp.repeat(jnp.exp(CHUNK * log_g), HEAD_DIM, axis=1)
    vec = _pad_rows([sgu_norm_w, sgu_norm_b, pool_scale, ret_gn_w[0], ret_gn_w[1], cdec[0], cdec[1]])
    return {
        "dmat": dmat,
        "dec": jnp.stack([qdec_f, qdec_b, kdec_f, kdec_b]),
        "sink": jnp.broadcast_to(jnp.repeat(attn_sink, CHUNK)[:, None], (N_HEADS * CHUNK, 128)),
        "ws": jnp.concatenate([sgu_ws[h] for h in range(N_HEADS)], axis=1).astype(BF16),
        "bias": jnp.repeat(sgu_bs.T, HEAD_DIM, axis=1),
        "vec": vec,
        "wpool": _block_diag(pool_w).astype(BF16),
    }


def kernel(x_prompt, x_sample, cache_attn_k, cache_attn_v, state_ret, c, c_ctx, w_ada, b_ada, w_in,
           w_out, attn_sink, sgu_norm_w, sgu_norm_b, sgu_ws, sgu_bs, ret_decay, ret_gn_w, pool_w,
           pool_scale, ffn_up, ffn_conv_w, ffn_conv_b, ffn_down, ln_w, ln_b):
    cond8 = jnp.concatenate([c_ctx[None], c, jnp.zeros((8 - 1 - DEC_BATCH, D_MODEL), F32)], axis=0)
    mod = _modulation(cond8, w_ada, b_ada).reshape(DEPTH, 8, 6, D_MODEL)

    tabs = jax.vmap(_layer_tables)(attn_sink, sgu_norm_w, sgu_norm_b, sgu_ws, sgu_bs, ret_decay, ret_gn_w,
                                   pool_w, pool_scale)
    tabs["gmat"] = _block_diag(jnp.full((N_HEADS, HEAD_DIM, HEAD_DIM), 1.0 / HEAD_DIM, F32)).astype(BF16)
    tabs["cnt_ctx"] = _inv_count(SEQ)
    tabs["cnt_lat"] = _inv_count(DEC_SEQ)
    cos, sin = _rope_tables()
    extra = {"ck": cache_attn_k.reshape(DEC_BATCH, DEPTH, PAST_LEN, 128),
             "cv": cache_attn_v.reshape(DEC_BATCH, DEPTH, PAST_LEN, 128),
             "cos": cos, "sin": sin, "s0": state_ret}
    ln8 = jnp.concatenate([ln_w, ln_b, jnp.zeros((DEPTH, 4, D_MODEL), F32)], axis=1)
    conv8 = jnp.concatenate([ffn_conv_w, ffn_conv_b[:, None], jnp.zeros((DEPTH, 4, 2 * D_FF), F32)], axis=1)

    down_bf16 = _to_bf16(ffn_down, D_FF // 4)
    w_in_bf16 = _to_bf16(w_in, D_MODEL // 2)
    w_out_bf16 = _to_bf16(w_out, D_MODEL)

    xs = [x_prompt.reshape(N_CTX, D_MODEL), x_sample.reshape(N_LAT, D_MODEL)]
    new_k, new_v, new_s = [], [], []
    for l in range(DEPTH):
        z = _inproj(xs, mod, w_in_bf16, l)
        mix_ctx, st, kn, vn = _mixer(z, tabs, l, latent=False)
        (mix_lat,) = _mixer(z, tabs, l, latent=True, extra=extra)
        x1 = _outproj([mix_ctx, mix_lat], xs, mod, w_out_bf16, ln8, l)
        xs = _ffn(x1, mod, ffn_up, conv8, down_bf16, ln8, l, split_out=(l == DEPTH - 1))
        new_k.append(kn.reshape(BATCH, SEQ, 2, HEAD_DIM))
        new_v.append(vn.reshape(BATCH, SEQ, 2, HEAD_DIM))
        new_s.append(st)

    y_prompt = xs[0].reshape(BATCH, SEQ, D_MODEL)
    y_sample = xs[1].reshape(DEC_BATCH, DEC_SEQ, D_MODEL)
    return (y_prompt, y_sample, jnp.stack(new_k, axis=1), jnp.stack(new_v, axis=1),
            jnp.stack(new_s, axis=1))
```

```python
import functools

import numpy as np
import jax
import jax.numpy as jnp
from jax import lax
from jax.experimental import pallas as pl
from jax.experimental.pallas import tpu as pltpu

F32 = jnp.float32
BF16 = jnp.bfloat16

D_MODEL = 1024
BATCH = 16
SEQ = 256
DEPTH = 2
DEC_BATCH = 2
DEC_SEQ = 2048
PAST_LEN = 256
GRID_W = 64
CHUNK = 128
HEAD_DIM = 64
GW = D_MODEL // 4
N_HEADS = 4
POOL_WINDOWS = (2, 4, 8, 16)
POOL_HALO = 8
D_FF = 2816
ROPE_BASE = 10000.0
LN_EPS = 1e-5
NEG_INF = -1e30
IN_WIDTH = 10 * GW
ALPHA = (2.0 * DEPTH) ** 0.25

N_CTX = BATCH * SEQ
N_LAT = DEC_BATCH * DEC_SEQ
ROWS = N_CTX + N_LAT

TM = 1024
CTX_TILES = N_CTX // TM
LAT_TILES_PER_SEQ = DEC_SEQ // TM
NB_IN = 512
FC = 256
NB_ADA = 1536
VMEM_LIMIT = 56 * 1024 * 1024


def _cond_of_tile(i, tm=TM):
    ctx_tiles = N_CTX // tm
    return jnp.where(i < ctx_tiles, 0, 1 + (i - ctx_tiles) // (DEC_SEQ // tm))


def _tile_specs(n_src, tm, grid_rank):
    ctx_tiles = N_CTX // tm
    if n_src == 1:
        rows = [lambda i: i]
    else:
        rows = [lambda i: jnp.minimum(i, ctx_tiles - 1), lambda i: jnp.maximum(i - ctx_tiles, 0)]
    if grid_rank == 1:
        return [pl.BlockSpec((tm, D_MODEL), lambda i, f=f: (f(i), 0)) for f in rows]
    return [pl.BlockSpec((tm, D_MODEL), lambda i, j, f=f: (f(i), 0)) for f in rows]


def _per_half(i, tm, n_max, fn):
    if n_max == 1:
        fn(0)
        return
    ctx_tiles = N_CTX // tm

    @pl.when(i < ctx_tiles)
    def _():
        fn(0)

    @pl.when(i >= ctx_tiles)
    def _():
        fn(1)


def _layer_norm(x, w, b):
    mu = jnp.mean(x, axis=-1, keepdims=True)
    d = x - mu
    var = jnp.mean(d * d, axis=-1, keepdims=True)
    return d * lax.rsqrt(var + LN_EPS) * w + b


def _silu(x):
    return x * jax.nn.sigmoid(x)


def _bdot(a, b):
    return jnp.dot(a.astype(BF16), b.astype(BF16), preferred_element_type=F32)


def _mod_kernel(c_ref, w_ref, b_ref, o_ref):
    o_ref[...] = _bdot(_silu(c_ref[...]), w_ref[...]) + b_ref[...]


def _modulation(cond8, w_ada, b_ada):
    return pl.pallas_call(
        _mod_kernel,
        grid=(DEPTH, 6 * D_MODEL // NB_ADA),
        in_specs=[
            pl.BlockSpec((8, D_MODEL), lambda l, j: (0, 0)),
            pl.BlockSpec((None, D_MODEL, NB_ADA), lambda l, j: (l, 0, j)),
            pl.BlockSpec((None, 1, NB_ADA), lambda l, j: (l, 0, j)),
        ],
        out_specs=pl.BlockSpec((None, 8, NB_ADA), lambda l, j: (l, 0, j)),
        out_shape=jax.ShapeDtypeStruct((DEPTH, 8, 6 * D_MODEL), F32),
        compiler_params=pltpu.CompilerParams(
            dimension_semantics=("arbitrary", "arbitrary"), vmem_limit_bytes=VMEM_LIMIT),
        name="modulation",
    )(cond8, w_ada, b_ada.reshape(DEPTH, 1, 6 * D_MODEL))


def _cast_kernel(w_ref, o_ref):
    o_ref[...] = w_ref[...].astype(BF16)


def _to_bf16(w, block_rows):
    depth, rows, cols = w.shape
    return pl.pallas_call(
        _cast_kernel,
        grid=(depth, rows // block_rows),
        in_specs=[pl.BlockSpec((None, block_rows, cols), lambda l, r: (l, r, 0))],
        out_specs=pl.BlockSpec((None, block_rows, cols), lambda l, r: (l, r, 0)),
        out_shape=jax.ShapeDtypeStruct(w.shape, BF16),
        compiler_params=pltpu.CompilerParams(dimension_semantics=("arbitrary", "arbitrary")),
        name="cast_bf16",
    )(w)


def _to_bf16_chunks(w, chunk):
    depth, rows, cols = w.shape
    return pl.pallas_call(
        _cast_kernel,
        grid=(depth, cols // chunk),
        in_specs=[pl.BlockSpec((None, rows, chunk), lambda l, c: (l, 0, c))],
        out_specs=pl.BlockSpec((None, None, rows, chunk), lambda l, c: (l, c, 0, 0)),
        out_shape=jax.ShapeDtypeStruct((depth, cols // chunk, rows, chunk), BF16),
        compiler_params=pltpu.CompilerParams(dimension_semantics=("arbitrary", "arbitrary")),
        name="cast_bf16_chunks",
    )(w)


def _inproj_kernel(*refs, n_x):
    x_refs = refs[:n_x]
    mod_ref, w_ref, z_ref, h_scr = refs[n_x:]

    m = mod_ref[...]

    def build(side):
        h_scr[...] = (x_refs[side][...] * (1.0 + m[1:2]) + m[0:1]).astype(BF16)

    _per_half(pl.program_id(0), TM, n_x, build)
    h = h_scr[...]
    for jb in range(IN_WIDTH // NB_IN):
        cols = slice(jb * NB_IN, (jb + 1) * NB_IN)
        z_ref[:, cols] = jnp.dot(h, w_ref[:, cols], preferred_element_type=F32)


def _inproj(xs, mod, w_in_bf16, l):
    return pl.pallas_call(
        functools.partial(_inproj_kernel, n_x=len(xs)),
        grid=(ROWS // TM,),
        in_specs=_tile_specs(len(xs), TM, 1) + [
            pl.BlockSpec((None, None, 6, D_MODEL), lambda i: (l, _cond_of_tile(i), 0, 0)),
            pl.BlockSpec((None, D_MODEL, IN_WIDTH), lambda i: (l, 0, 0), pipeline_mode=pl.Buffered(1)),
        ],
        out_specs=pl.BlockSpec((TM, IN_WIDTH), lambda i: (i, 0)),
        out_shape=jax.ShapeDtypeStruct((ROWS, IN_WIDTH), F32),
        scratch_shapes=[pltpu.VMEM((TM, D_MODEL), BF16)],
        compiler_params=pltpu.CompilerParams(
            dimension_semantics=("arbitrary",), vmem_limit_bytes=VMEM_LIMIT),
        name="inproj",
    )(*xs, mod, w_in_bf16)


TM_OUT = 512


def _outproj_kernel(*refs, n_x):
    mix_refs = refs[:2]
    x_refs = refs[2:2 + n_x]
    mod_ref, w_ref, ln_ref, o_ref = refs[2 + n_x:]
    m = mod_ref[...]
    ln = ln_ref[...]

    def body(side):
        y = jnp.dot(mix_refs[side][...].astype(BF16), w_ref[...], preferred_element_type=F32)
        x = x_refs[min(side, n_x - 1)][...]
        o_ref[...] = _layer_norm(ALPHA * x + m[2:3] * y, ln[0:1], ln[2:3])

    _per_half(pl.program_id(0), TM_OUT, 2, body)


def _outproj(mixes, xs, mod, w_out, ln8, l):
    return pl.pallas_call(
        functools.partial(_outproj_kernel, n_x=len(xs)),
        grid=(ROWS // TM_OUT,),
        in_specs=_tile_specs(2, TM_OUT, 1) + _tile_specs(len(xs), TM_OUT, 1) + [
            pl.BlockSpec((None, None, 6, D_MODEL), lambda i: (l, _cond_of_tile(i, TM_OUT), 0, 0)),
            pl.BlockSpec((None, D_MODEL, D_MODEL), lambda i: (l, 0, 0)),
            pl.BlockSpec((None, 8, D_MODEL), lambda i: (l, 0, 0)),
        ],
        out_specs=pl.BlockSpec((TM_OUT, D_MODEL), lambda i: (i, 0)),
        out_shape=jax.ShapeDtypeStruct((ROWS, D_MODEL), F32),
        compiler_params=pltpu.CompilerParams(
            dimension_semantics=("arbitrary",), vmem_limit_bytes=VMEM_LIMIT),
        name="outproj",
    )(*mixes, *xs, mod, w_out, ln8)


SEG = TM // 8
HALO_ROWS = 16
RB = 256
RBU = 512
GB = 64
NFC = D_FF // FC
LANE_BLOCKS = D_MODEL // 128
SEG_PITCH = SEG + 8


def _seg_rows(xc_ref, k):
    return jnp.concatenate([xc_ref[cb, pl.ds(k, 8, stride=SEG_PITCH), :] for cb in range(LANE_BLOCKS)], axis=1)


def _ffn_kernel_old(*refs, n_out):
    (x_ref, xp_ref, xn_ref, mod_ref, upa_ref, upg_ref, cva_ref, cvg_ref, dn_ref, ln_ref) = refs[:10]
    o_refs = refs[10:10 + n_out]
    h_scr, act_scr, xc_scr, u0_scr, u1_scr, wbf_scr = refs[10 + n_out:]
    u_scrs = (u0_scr, u1_scr)
    i = pl.program_id(0)
    j = pl.program_id(1)
    is_ctx = i < CTX_TILES
    lat_pos = (i - CTX_TILES) % LAT_TILES_PER_SEQ

    def build_h():
        m = mod_ref[...]
        scale = 1.0 + m[4:5]
        shift = m[3:4]
        for cb in range(LANE_BLOCKS):
            for s in range(8):
                xc_scr[cb, s * SEG_PITCH:s * SEG_PITCH + SEG, :] = x_ref[s * SEG:(s + 1) * SEG,
                                                                         cb * 128:(cb + 1) * 128]
        for k in range(0, SEG, 2):
            rows = jnp.concatenate([_seg_rows(xc_scr, k), _seg_rows(xc_scr, k + 1)], axis=0)
            h_scr[8 * k:8 * k + 16, :] = (rows * scale + shift).astype(BF16)
        sub = lax.broadcasted_iota(jnp.int32, (HALO_ROWS, D_MODEL), 0)
        prev_ok = jnp.logical_not(is_ctx) & (lat_pos > 0)
        next_ok = jnp.logical_not(is_ctx) & (lat_pos < LAT_TILES_PER_SEQ - 1)
        halo_x = jnp.where(sub == 0, xp_ref[POOL_HALO - 1:POOL_HALO, :], xn_ref[0:1, :])
        keep = ((sub == 0) & prev_ok) | ((sub == 1) & next_ok)
        h_scr[TM:TM + HALO_ROWS, :] = jnp.where(keep, halo_x * scale + shift, 0.0).astype(BF16)

    n_ub = TM // RBU

    def cast_up_weights():
        wbf_scr[:, 0:FC] = upa_ref[...].astype(BF16)
        wbf_scr[:, FC:2 * FC] = upg_ref[...].astype(BF16)

    def up_proj(slot, ub):
        rows = slice(ub * RBU, (ub + 1) * RBU + (HALO_ROWS if ub == n_ub - 1 else 0))
        u_scrs[slot][rows, :] = jnp.dot(h_scr[rows, :], wbf_scr[...], preferred_element_type=F32)

    def conv(u_ref, lanes, cv, r0):
        lo = max(r0 - 8, 0)
        hi = min(r0 + GB + 8, TM)
        ue = u_ref[lo:hi, lanes]
        u = ue[r0 - lo:r0 - lo + GB]
        if r0 == 0 or r0 == TM - GB:
            sub = lax.broadcasted_iota(jnp.int32, (8, FC), 0)
            seg_per_seq = SEQ // SEG
        if r0 == 0:
            ctx_first = is_ctx & (sub % seg_per_seq == 0)
            b_first = jnp.where(sub == 0, u_ref[TM:TM + 1, lanes],
                                pltpu.roll(u_ref[TM - 8:TM, lanes], 1, axis=0))
            um1 = jnp.concatenate([jnp.where(ctx_first, 0.0, b_first), u[0:GB - 8]], axis=0)
        else:
            um1 = ue[0:GB]
        if r0 == TM - GB:
            ctx_last = is_ctx & (sub % seg_per_seq == seg_per_seq - 1)
            b_last = jnp.where(sub == 7, u_ref[TM + 1:TM + 2, lanes],
                               pltpu.roll(u_ref[0:8, lanes], 7, axis=0))
            up1 = jnp.concatenate([u[8:GB], jnp.where(ctx_last, 0.0, b_last)], axis=0)
        else:
            up1 = ue[r0 - lo + 8:r0 - lo + GB + 8]
        return um1 * cv[0:1] + u * cv[1:2] + up1 * cv[2:3] + cv[3:4]

    def gate(slot, chunk, ub):
        for r0 in range(ub * RBU, (ub + 1) * RBU, GB):
            a = conv(u_scrs[slot], slice(0, FC), cva_ref[...], r0)
            g = conv(u_scrs[slot], slice(FC, 2 * FC), cvg_ref[...], r0)
            act_scr[chunk, r0:r0 + GB, :] = (_silu(a) * g).astype(BF16)

    @pl.when(j == 0)
    def _():
        build_h()
        cast_up_weights()
        for ub in range(n_ub):
            up_proj(0, ub)

    for slot in range(2):
        @pl.when((j >= 1) & (j < NFC) & (j % 2 == slot))
        def _(slot=slot):
            cast_up_weights()
            for ub in range(n_ub):
                up_proj(slot, ub)
                gate(1 - slot, j - 1, ub)

    @pl.when(j == NFC)
    def _():
        for ub in range(n_ub):
            gate((NFC - 1) % 2, NFC - 1, ub)
        m = mod_ref[...]
        ln = ln_ref[...]
        for rb in range(TM // RB):
            vrows = range(rb * RB // 8, (rb + 1) * RB // 8)
            lhs = jnp.concatenate([act_scr[jj, rb * RB:(rb + 1) * RB, :] for jj in range(NFC)], axis=1)
            y = jnp.dot(lhs, dn_ref[...], preferred_element_type=F32)
            xr = jnp.concatenate([_seg_rows(xc_scr, k) for k in vrows], axis=0)
            out = _layer_norm(ALPHA * xr + m[5:6] * y, ln[1:2], ln[3:4])
            for kk, k in enumerate(vrows):
                for cb in range(LANE_BLOCKS):
                    xc_scr[cb, pl.ds(k, 8, stride=SEG_PITCH), :] = out[8 * kk:8 * kk + 8,
                                                                       cb * 128:(cb + 1) * 128]
        def write_out(side):
            for cb in range(LANE_BLOCKS):
                for s in range(8):
                    o_refs[side][s * SEG:(s + 1) * SEG, cb * 128:(cb + 1) * 128] = xc_scr[
                        cb, s * SEG_PITCH:s * SEG_PITCH + SEG, :]

        _per_half(i, TM, n_out, write_out)


def _ffn_old(x, mod, up, conv8, down, ln8, l, split_out):
    halo_blocks = TM // POOL_HALO
    last_halo = ROWS // POOL_HALO - 1

    def up_chunk(j):
        return jnp.minimum(j, NFC - 1)

    def gate_chunk(j):
        return jnp.maximum(j - 1, 0)

    if split_out:
        out_specs = _tile_specs(2, TM, 2)
        out_shape = [jax.ShapeDtypeStruct((N_CTX, D_MODEL), F32), jax.ShapeDtypeStruct((N_LAT, D_MODEL), F32)]
    else:
        out_specs = _tile_specs(1, TM, 2)
        out_shape = [jax.ShapeDtypeStruct((ROWS, D_MODEL), F32)]
    return pl.pallas_call(
        functools.partial(_ffn_kernel, n_out=len(out_shape)),
        grid=(ROWS // TM, NFC + 1),
        in_specs=[
            pl.BlockSpec((TM, D_MODEL), lambda i, j: (i, 0)),
            pl.BlockSpec((POOL_HALO, D_MODEL), lambda i, j: (jnp.maximum(i * halo_blocks - 1, 0), 0)),
            pl.BlockSpec((POOL_HALO, D_MODEL),
                         lambda i, j: (jnp.minimum((i + 1) * halo_blocks, last_halo), 0)),
            pl.BlockSpec((None, None, 6, D_MODEL), lambda i, j: (l, _cond_of_tile(i), 0, 0)),
            pl.BlockSpec((None, D_MODEL, FC), lambda i, j: (l, 0, up_chunk(j))),
            pl.BlockSpec((None, D_MODEL, FC), lambda i, j: (l, 0, NFC + up_chunk(j))),
            pl.BlockSpec((None, 8, FC), lambda i, j: (l, 0, gate_chunk(j))),
            pl.BlockSpec((None, 8, FC), lambda i, j: (l, 0, NFC + gate_chunk(j))),
            pl.BlockSpec((None, D_FF, D_MODEL), lambda i, j: (l, 0, 0), pipeline_mode=pl.Buffered(1)),
            pl.BlockSpec((None, 8, D_MODEL), lambda i, j: (l, 0, 0)),
        ],
        out_specs=out_specs,
        out_shape=out_shape,
        scratch_shapes=[pltpu.VMEM((TM + HALO_ROWS, D_MODEL), BF16),
                        pltpu.VMEM((NFC, TM, FC), BF16),
                        pltpu.VMEM((LANE_BLOCKS, 8 * SEG_PITCH, 128), F32),
                        pltpu.VMEM((TM + HALO_ROWS, 2 * FC), F32),
                        pltpu.VMEM((TM + HALO_ROWS, 2 * FC), F32),
                        pltpu.VMEM((D_MODEL, 2 * FC), BF16)],
        compiler_params=pltpu.CompilerParams(
            dimension_semantics=("arbitrary", "arbitrary"), vmem_limit_bytes=VMEM_LIMIT),
        name="convffn",
    )(x, x, x, mod, up, up, conv8, conv8, down, ln8)


TMF = 512
SEGF = TMF // 8
PITCHF = SEGF + 8
CTXF_TILES = N_CTX // TMF
LATF_PER_SEQ = DEC_SEQ // TMF
RBUF = 256


def _up_weight_kernel(a_ref, g_ref, o_ref):
    for c in range(NFC):
        o_ref[c, :, 0:FC] = a_ref[:, c * FC:(c + 1) * FC].astype(BF16)
        o_ref[c, :, FC:2 * FC] = g_ref[:, c * FC:(c + 1) * FC].astype(BF16)


def _up_weight_chunks(ffn_up):
    half = D_MODEL // 2
    return pl.pallas_call(
        _up_weight_kernel,
        grid=(DEPTH, 2),
        in_specs=[pl.BlockSpec((None, half, D_FF), lambda l, r: (l, r, 0)),
                  pl.BlockSpec((None, half, D_FF), lambda l, r: (l, r, 1))],
        out_specs=pl.BlockSpec((None, NFC, half, 2 * FC), lambda l, r: (l, 0, r, 0)),
        out_shape=jax.ShapeDtypeStruct((DEPTH, NFC, D_MODEL, 2 * FC), BF16),
        compiler_params=pltpu.CompilerParams(
            dimension_semantics=("arbitrary", "arbitrary"), vmem_limit_bytes=VMEM_LIMIT),
        name="cast_up_chunks",
    )(ffn_up, ffn_up)


def _seg_rows_f(xc_ref, k):
    return jnp.concatenate([xc_ref[cb, pl.ds(k, 8, stride=PITCHF), :] for cb in range(LANE_BLOCKS)], axis=1)


def _ffn_kernel(*refs, n_out):
    (x_ref, xp_ref, xn_ref, mod_ref, up_ref, cv_ref, dn_ref, ln_ref) = refs[:8]
    o_refs = refs[8:8 + n_out]
    h_scr, act_scr, xc_scr, u0_scr, u1_scr = refs[8 + n_out:]
    u_scrs = (u0_scr, u1_scr)
    i = pl.program_id(0)
    is_ctx = i < CTXF_TILES
    lat_pos = (i - CTXF_TILES) % LATF_PER_SEQ
    m = mod_ref[...]
    ln = ln_ref[...]

    scale = 1.0 + m[4:5]
    shift = m[3:4]
    for cb in range(LANE_BLOCKS):
        for s in range(8):
            xc_scr[cb, s * PITCHF:s * PITCHF + SEGF, :] = x_ref[s * SEGF:(s + 1) * SEGF, cb * 128:(cb + 1) * 128]
    for k in range(0, SEGF, 2):
        rows = jnp.concatenate([_seg_rows_f(xc_scr, k), _seg_rows_f(xc_scr, k + 1)], axis=0)
        h_scr[8 * k:8 * k + 16, :] = (rows * scale + shift).astype(BF16)
    sub16 = lax.broadcasted_iota(jnp.int32, (HALO_ROWS, D_MODEL), 0)
    prev_ok = jnp.logical_not(is_ctx) & (lat_pos > 0)
    next_ok = jnp.logical_not(is_ctx) & (lat_pos < LATF_PER_SEQ - 1)
    halo_x = jnp.where(sub16 == 0, xp_ref[POOL_HALO - 1:POOL_HALO, :], xn_ref[0:1, :])
    keep = ((sub16 == 0) & prev_ok) | ((sub16 == 1) & next_ok)
    h_scr[TMF:TMF + HALO_ROWS, :] = jnp.where(keep, halo_x * scale + shift, 0.0).astype(BF16)

    n_ub = TMF // RBUF
    sub = lax.broadcasted_iota(jnp.int32, (8, FC), 0)
    seg_per_seq = SEQ // SEGF
    ctx_first = is_ctx & (sub % seg_per_seq == 0)
    ctx_last = is_ctx & (sub % seg_per_seq == seg_per_seq - 1)

    def up_proj(slot, c, ub):
        rows = slice(ub * RBUF, (ub + 1) * RBUF + (HALO_ROWS if ub == n_ub - 1 else 0))
        u_scrs[slot][rows, :] = jnp.dot(h_scr[rows, :], up_ref[c], preferred_element_type=F32)

    def conv(u_ref, lanes, cvs, r0):
        lo = max(r0 - 8, 0)
        hi = min(r0 + GB + 8, TMF)
        ue = u_ref[lo:hi, lanes]
        u = ue[r0 - lo:r0 - lo + GB]
        if r0 == 0:
            b_first = jnp.where(sub == 0, u_ref[TMF:TMF + 1, lanes],
                                pltpu.roll(u_ref[TMF - 8:TMF, lanes], 1, axis=0))
            um1 = jnp.concatenate([jnp.where(ctx_first, 0.0, b_first), u[0:GB - 8]], axis=0)
        else:
            um1 = ue[0:GB]
        if r0 == TMF - GB:
            b_last = jnp.where(sub == 7, u_ref[TMF + 1:TMF + 2, lanes],
                               pltpu.roll(u_ref[0:8, lanes], 7, axis=0))
            up1 = jnp.concatenate([u[8:GB], jnp.where(ctx_last, 0.0, b_last)], axis=0)
        else:
            up1 = ue[r0 - lo + 8:r0 - lo + GB + 8]
        return um1 * cvs[0:1] + u * cvs[1:2] + up1 * cvs[2:3] + cvs[3:4]

    def gate(slot, c, ub):
        cva = cv_ref[0:4, c * FC:(c + 1) * FC]
        cvg = cv_ref[0:4, D_FF + c * FC:D_FF + (c + 1) * FC]
        for r0 in range(ub * RBUF, (ub + 1) * RBUF, GB):
            a = conv(u_scrs[slot], slice(0, FC), cva, r0)
            g = conv(u_scrs[slot], slice(FC, 2 * FC), cvg, r0)
            act_scr[c, r0:r0 + GB, :] = (_silu(a) * g).astype(BF16)

    for c in range(NFC + 1):
        for ub in range(n_ub):
            if c < NFC:
                up_proj(c % 2, c, ub)
            if c >= 1:
                gate((c - 1) % 2, c - 1, ub)

    for rb in range(TMF // RB):
        vrows = range(rb * RB // 8, (rb + 1) * RB // 8)
        lhs = jnp.concatenate([act_scr[c, rb * RB:(rb + 1) * RB, :] for c in range(NFC)], axis=1)
        y = jnp.dot(lhs, dn_ref[...], preferred_element_type=F32)
        xr = jnp.concatenate([_seg_rows_f(xc_scr, k) for k in vrows], axis=0)
        out = _layer_norm(ALPHA * xr + m[5:6] * y, ln[1:2], ln[3:4])
        for kk, k in enumerate(vrows):
            for cb in range(LANE_BLOCKS):
                xc_scr[cb, pl.ds(k, 8, stride=PITCHF), :] = out[8 * kk:8 * kk + 8, cb * 128:(cb + 1) * 128]

    def write_out(side):
        for cb in range(LANE_BLOCKS):
            for s in range(8):
                o_refs[side][s * SEGF:(s + 1) * SEGF, cb * 128:(cb + 1) * 128] = xc_scr[
                    cb, s * PITCHF:s * PITCHF + SEGF, :]

    _per_half(i, TMF, n_out, write_out)


def _ffn(x, mod, up_chunks, conv8, down_bf16, ln8, l, split_out):
    halo_blocks = TMF // POOL_HALO
    last_halo = ROWS // POOL_HALO - 1
    n_out = 2 if split_out else 1
    if split_out:
        out_shape = [jax.ShapeDtypeStruct((N_CTX, D_MODEL), F32), jax.ShapeDtypeStruct((N_LAT, D_MODEL), F32)]
    else:
        out_shape = [jax.ShapeDtypeStruct((ROWS, D_MODEL), F32)]
    return pl.pallas_call(
        functools.partial(_ffn_kernel, n_out=n_out),
        grid=(ROWS // TMF,),
        in_specs=[
            pl.BlockSpec((TMF, D_MODEL), lambda i: (i, 0)),
            pl.BlockSpec((POOL_HALO, D_MODEL), lambda i: (jnp.maximum(i * halo_blocks - 1, 0), 0)),
            pl.BlockSpec((POOL_HALO, D_MODEL), lambda i: (jnp.minimum((i + 1) * halo_blocks, last_halo), 0)),
            pl.BlockSpec((None, None, 6, D_MODEL), lambda i: (l, _cond_of_tile(i, TMF), 0, 0)),
            pl.BlockSpec((None, NFC, D_MODEL, 2 * FC), lambda i: (l, 0, 0, 0), pipeline_mode=pl.Buffered(1)),
            pl.BlockSpec((None, 8, 2 * D_FF), lambda i: (l, 0, 0)),
            pl.BlockSpec((None, D_FF, D_MODEL), lambda i: (l, 0, 0), pipeline_mode=pl.Buffered(1)),
            pl.BlockSpec((None, 8, D_MODEL), lambda i: (l, 0, 0)),
        ],
        out_specs=_tile_specs(n_out, TMF, 1),
        out_shape=out_shape,
        scratch_shapes=[pltpu.VMEM((TMF + HALO_ROWS, D_MODEL), BF16),
                        pltpu.VMEM((NFC, TMF, FC), BF16),
                        pltpu.VMEM((LANE_BLOCKS, 8 * PITCHF, 128), F32),
                        pltpu.VMEM((TMF + HALO_ROWS, 2 * FC), F32),
                        pltpu.VMEM((TMF + HALO_ROWS, 2 * FC), F32)],
        compiler_params=pltpu.CompilerParams(
            dimension_semantics=("arbitrary",), vmem_limit_bytes=VMEM_LIMIT),
        name="convffn",
    )(x, x, x, mod, up_chunks, conv8, down_bf16, ln8)


def _head_masks(width):
    lane = lax.broadcasted_iota(jnp.int32, (1, width), 1)
    return [(lane >= h * HEAD_DIM) & (lane < (h + 1) * HEAD_DIM) for h in range(width // HEAD_DIM)]


def _stack_heads(x, masks):
    return jnp.concatenate([jnp.where(m, x, jnp.zeros_like(x)) for m in masks], axis=0)


def _rope(x, cos, sin):
    lane = lax.broadcasted_iota(jnp.int32, (1, 128), 1)
    lower = (lane & 31) < 16
    outs = []
    for k in range(x.shape[1] // 128):
        xb = x[:, k * 128:(k + 1) * 128]
        partner = jnp.where(lower, pltpu.roll(xb, 112, axis=1), pltpu.roll(xb, 16, axis=1))
        outs.append(xb * cos + partner * sin)
    return outs[0] if len(outs) == 1 else jnp.concatenate(outs, axis=1)


def _retention_chunk(q, k, v, dmat, qdec, kdec, cdec, state_scr, masks, bd_mask):
    qb = q.astype(BF16)
    kb = k.astype(BF16)
    vb = v.astype(BF16)
    s = lax.dot_general(_stack_heads(qb, masks), kb, (((1,), (1,)), ((), ())),
                        preferred_element_type=F32)
    p = (s * dmat).astype(BF16)
    p_cat = jnp.concatenate([p[h * CHUNK:(h + 1) * CHUNK] for h in range(N_HEADS)], axis=1)
    state = state_scr[...]
    o = (jnp.dot(p_cat, _stack_heads(vb, masks), preferred_element_type=F32)
         + _bdot(q * qdec, state))
    upd = lax.dot_general((k * kdec).astype(BF16), vb, (((0,), (0,)), ((), ())),
                          preferred_element_type=F32)
    state_scr[...] = state * cdec + jnp.where(bd_mask, upd, 0.0)
    return o


def _load_state(state_scr, blocks_ref, d):
    state_scr[...] = jnp.zeros((GW, GW), F32)
    for h in range(N_HEADS):
        sl = slice(h * HEAD_DIM, (h + 1) * HEAD_DIM)
        state_scr[sl, sl] = blocks_ref[d, h]


def _store_state(blocks_ref, d, state_scr):
    for h in range(N_HEADS):
        sl = slice(h * HEAD_DIM, (h + 1) * HEAD_DIM)
        blocks_ref[d, h] = state_scr[sl, sl]


def _group_norm(o, gmat):
    mu = jnp.dot(o.astype(BF16), gmat, preferred_element_type=F32)
    d = o - mu
    var = jnp.dot((d * d).astype(BF16), gmat, preferred_element_type=F32)
    return d * lax.rsqrt(var + LN_EPS)


def _mixer_kernel_old(*refs, latent, nc):
    if latent:
        (qa_ref, ka_ref, va_ref, kx_ref, vx_ref, ub_ref, vb_ref, qc_ref, kc_ref, vc_ref, gf_ref,
         gb_ref, pd_ref, pdp_ref, pdn_ref, cos_ref, sin_ref, s0_ref, dmat_ref, dec_ref, sink_ref,
         ws_ref, bias_ref, vec_ref, gmat_ref, cnt_ref, wpool_ref,
         mix_ref, sf_scr, sb_scr, ob_scr, pext_scr, k_scr, v_scr) = refs
    else:
        (qa_ref, kx_ref, vx_ref, ub_ref, vb_ref, qc_ref, kc_ref, vc_ref, gf_ref,
         gb_ref, pd_ref, pdp_ref, pdn_ref, dmat_ref, dec_ref, sink_ref,
         ws_ref, bias_ref, vec_ref, gmat_ref, cnt_ref, wpool_ref,
         mix_ref, st_ref, kn_ref, vn_ref, sf_scr, sb_scr, ob_scr, pext_scr) = refs

    p = pl.program_id(1)
    c = pl.program_id(2)
    masks = _head_masks(GW)
    row = lax.broadcasted_iota(jnp.int32, (GW, GW), 0)
    col = lax.broadcasted_iota(jnp.int32, (GW, GW), 1)
    bd_mask = (row // HEAD_DIM) == (col // HEAD_DIM)
    vec = vec_ref[...]

    @pl.when(p == 0)
    def _():
        rc = nc - 1 - c

        @pl.when(c == 0)
        def _():
            if latent:
                _load_state(sb_scr, s0_ref, 1)
                zero_blk = jnp.zeros((CHUNK, 2 * HEAD_DIM), BF16)
                k_scr[0:CHUNK, :] = zero_blk
                v_scr[0:CHUNK, :] = zero_blk
                k_scr[(nc + 1) * CHUNK:(nc + 2) * CHUNK, :] = zero_blk
                v_scr[(nc + 1) * CHUNK:(nc + 2) * CHUNK, :] = zero_blk
            else:
                sb_scr[...] = jnp.zeros((GW, GW), F32)

        if latent:
            dst = pl.ds(pl.multiple_of((rc + 1) * CHUNK, CHUNK), CHUNK)
            k_scr[dst, :] = _rope(ka_ref[...], cos_ref[...], sin_ref[...]).astype(BF16)
            v_scr[dst, :] = va_ref[...].astype(BF16)

        dec = dec_ref[...]
        o_b = _retention_chunk(qc_ref[...], kc_ref[...], vc_ref[...], dmat_ref[1],
                               dec[1], dec[3], vec[6:7], sb_scr, masks, bd_mask)
        ob_scr[pl.ds(pl.multiple_of(rc * CHUNK, CHUNK), CHUNK), :] = o_b

        if not latent:
            @pl.when(c == nc - 1)
            def _():
                _store_state(st_ref, 1, sb_scr)

    @pl.when(p == 1)
    def _():
        @pl.when(c == 0)
        def _():
            if latent:
                _load_state(sf_scr, s0_ref, 0)
            else:
                sf_scr[...] = jnp.zeros((GW, GW), F32)
                kn_ref[...] = kx_ref[...]
                vn_ref[...] = vx_ref[...]

        q = qa_ref[...]
        if latent:
            q = _rope(q, cos_ref[...], sin_ref[...])
        q = q * (HEAD_DIM ** -0.5)
        lane = lax.broadcasted_iota(jnp.int32, (1, 2 * HEAD_DIM), 1)
        lo = lane < HEAD_DIM
        q0, q1 = q[:, :128], q[:, 128:]
        zero = jnp.zeros_like(q0)
        q_st = jnp.concatenate([
            jnp.where(lo, q0, zero),
            jnp.where(lo, pltpu.roll(q0, HEAD_DIM, axis=1), zero),
            jnp.where(lo, zero, pltpu.roll(q1, HEAD_DIM, axis=1)),
            jnp.where(lo, zero, q1)], axis=0).astype(BF16)
        if latent:
            band = pl.ds(pl.multiple_of(c * CHUNK, CHUNK), 3 * CHUNK)
            k_all = jnp.concatenate([k_scr[band, :], kx_ref[...].astype(BF16)], axis=0)
            v_all = jnp.concatenate([v_scr[band, :], vx_ref[...].astype(BF16)], axis=0)
        else:
            k_all = kx_ref[...].astype(BF16)
            v_all = vx_ref[...].astype(BF16)
        s = lax.dot_general(q_st, k_all, (((1,), (1,)), ((), ())), preferred_element_type=F32)
        if latent:
            nk = 3 * CHUNK + PAST_LEN
            qi = lax.broadcasted_iota(jnp.int32, (N_HEADS * CHUNK, nk), 0) & (CHUNK - 1)
            kj = lax.broadcasted_iota(jnp.int32, (N_HEADS * CHUNK, nk), 1)
            kpos = kj + (c - 1) * CHUNK
            valid = (kj >= 3 * CHUNK) | ((kj >= qi) & (kj <= qi + 2 * CHUNK)
                                         & (kpos >= 0) & (kpos < nc * CHUNK))
            s = jnp.where(valid, s, NEG_INF)
        sink = sink_ref[...][:, 0:1]
        mx = jnp.maximum(jnp.max(s, axis=-1, keepdims=True), sink)
        e = jnp.exp(s - mx)
        den = jnp.sum(e, axis=-1, keepdims=True) + jnp.exp(sink - mx)
        o = jnp.dot(e.astype(BF16), v_all, preferred_element_type=F32) / den
        mix_ref[:, 0:128] = jnp.where(lo, o[0:CHUNK], pltpu.roll(o[CHUNK:2 * CHUNK], HEAD_DIM, axis=1))
        mix_ref[:, 128:256] = jnp.where(lo, pltpu.roll(o[2 * CHUNK:3 * CHUNK], HEAD_DIM, axis=1),
                                        o[3 * CHUNK:4 * CHUNK])

        vn = _layer_norm(vb_ref[...], vec[0:1], vec[1:2]).astype(BF16)
        sg = jnp.dot(ws_ref[...], _stack_heads(vn, masks), preferred_element_type=F32) + bias_ref[...]
        mix_ref[:, GW:2 * GW] = ub_ref[...] * sg

        dec = dec_ref[...]
        o_f = _retention_chunk(qc_ref[...], kc_ref[...], vc_ref[...], dmat_ref[0],
                               dec[0], dec[2], vec[5:6], sf_scr, masks, bd_mask)
        o_b = ob_scr[pl.ds(pl.multiple_of(c * CHUNK, CHUNK), CHUNK), :]
        gmat = gmat_ref[...]
        mix_ref[:, 2 * GW:3 * GW] = (_silu(gf_ref[...]) * (_group_norm(o_f, gmat) * vec[3:4])
                                     + _silu(gb_ref[...]) * (_group_norm(o_b, gmat) * vec[4:5]))
        if not latent:
            @pl.when(c == nc - 1)
            def _():
                _store_state(st_ref, 0, sf_scr)

        pd = pd_ref[...]
        pext_scr[0:POOL_HALO, :] = jnp.where(c > 0, pdp_ref[...], 0.0)
        pext_scr[POOL_HALO:POOL_HALO + CHUNK, :] = pd
        pext_scr[POOL_HALO + CHUNK:2 * POOL_HALO + CHUNK, :] = jnp.where(c < nc - 1, pdn_ref[...], 0.0)

        def win(d, half):
            return pext_scr[pl.ds(POOL_HALO + d, CHUNK), half * 128:(half + 1) * 128]

        a2 = win(-1, 0) + win(0, 0)
        a4 = a2 + win(-2, 0) + win(1, 0)
        a8 = win(-4, 1)
        for d in range(-3, 4):
            a8 = a8 + win(d, 1)
        a16 = a8
        for d in list(range(-8, -4)) + list(range(4, 8)):
            a16 = a16 + win(d, 1)
        sums = jnp.concatenate([jnp.where(lo, a2, a4), jnp.where(lo, a8, a16)], axis=1)
        yd = sums * cnt_ref[...] - pd
        mix_ref[:, 3 * GW:4 * GW] = _bdot(yd, wpool_ref[...]) * vec[2:3]


def _mixer_old(z, tabs, l, latent, extra=None):
    nb = DEC_BATCH if latent else BATCH
    nc = (DEC_SEQ if latent else SEQ) // CHUNK
    base = (N_CTX // CHUNK) if latent else 0
    last_halo = ROWS // POOL_HALO - 1
    per8 = CHUNK // POOL_HALO

    def fwd(b, p, c):
        return base + b * nc + c * p

    def both(b, p, c):
        return base + b * nc + jnp.where(p == 0, nc - 1 - c, c)

    def col(width, idx, rowmap):
        return pl.BlockSpec((CHUNK, width), lambda b, p, c: (rowmap(b, p, c), idx))

    def const(shape):
        return pl.BlockSpec(shape, lambda b, p, c: (0,) * len(shape))

    def layer(shape):
        return pl.BlockSpec((None,) + shape, lambda b, p, c: (l,) + (0,) * len(shape))

    specs, args = [], []

    def add(spec, arr):
        specs.append(spec)
        args.append(arr)

    add(col(GW, 0, fwd), z)
    if latent:
        add(pl.BlockSpec((CHUNK, 128), lambda b, p, c: (base + b * nc + (nc - 1 - c) * (1 - p), 2)), z)
        add(pl.BlockSpec((CHUNK, 128), lambda b, p, c: (base + b * nc + (nc - 1 - c) * (1 - p), 3)), z)
        add(pl.BlockSpec((None, None, PAST_LEN, 128), lambda b, p, c: (b, l, 0, 0)), extra["ck"])
        add(pl.BlockSpec((None, None, PAST_LEN, 128), lambda b, p, c: (b, l, 0, 0)), extra["cv"])
    else:
        add(pl.BlockSpec((SEQ, 128), lambda b, p, c: (b, 2)), z)
        add(pl.BlockSpec((SEQ, 128), lambda b, p, c: (b, 3)), z)
    add(col(GW, 2, fwd), z)
    add(col(GW, 3, fwd), z)
    add(col(GW, 4, both), z)
    add(col(GW, 5, both), z)
    add(col(GW, 6, both), z)
    add(col(GW, 7, fwd), z)
    add(col(GW, 8, fwd), z)
    add(col(GW, 9, fwd), z)
    add(pl.BlockSpec((POOL_HALO, GW),
                     lambda b, p, c: (jnp.maximum(fwd(b, p, c) * per8 - 1, 0), 9)), z)
    add(pl.BlockSpec((POOL_HALO, GW),
                     lambda b, p, c: (jnp.minimum((fwd(b, p, c) + 1) * per8, last_halo), 9)), z)
    if latent:
        rope_map = lambda b, p, c: (jnp.where(p == 0, nc - 1 - c, c), 0)
        add(pl.BlockSpec((CHUNK, 128), rope_map), extra["cos"])
        add(pl.BlockSpec((CHUNK, 128), rope_map), extra["sin"])
        add(pl.BlockSpec((None, None, 2, N_HEADS, HEAD_DIM, HEAD_DIM),
                         lambda b, p, c: (b, l, 0, 0, 0, 0)), extra["s0"])
    add(layer((2, N_HEADS * CHUNK, CHUNK)), tabs["dmat"])
    add(layer((4, CHUNK, GW)), tabs["dec"])
    add(layer((N_HEADS * CHUNK, 128)), tabs["sink"])
    add(layer((CHUNK, N_HEADS * CHUNK)), tabs["ws"])
    add(layer((CHUNK, GW)), tabs["bias"])
    add(layer((8, GW)), tabs["vec"])
    add(const((GW, GW)), tabs["gmat"])
    add(pl.BlockSpec((CHUNK, GW), lambda b, p, c: (c * p, 0)), tabs["cnt_lat"] if latent else tabs["cnt_ctx"])
    add(layer((GW, GW)), tabs["wpool"])

    out_shape = [jax.ShapeDtypeStruct((nb * nc * CHUNK, D_MODEL), F32)]
    out_specs = [pl.BlockSpec((CHUNK, D_MODEL), lambda b, p, c: (b * nc + c * p, 0))]
    scratch = [pltpu.VMEM((GW, GW), F32), pltpu.VMEM((GW, GW), F32),
               pltpu.VMEM((nc * CHUNK, GW), F32),
               pltpu.VMEM((CHUNK + 2 * POOL_HALO, GW), F32)]
    if latent:
        scratch += [pltpu.VMEM(((nc + 2) * CHUNK, 128), BF16), pltpu.VMEM(((nc + 2) * CHUNK, 128), BF16)]
    else:
        out_shape.append(jax.ShapeDtypeStruct((nb, 2, N_HEADS, HEAD_DIM, HEAD_DIM), F32))
        out_specs.append(pl.BlockSpec((None, 2, N_HEADS, HEAD_DIM, HEAD_DIM), lambda b, p, c: (b, 0, 0, 0, 0)))
        for _ in range(2):
            out_shape.append(jax.ShapeDtypeStruct((nb, SEQ, 128), F32))
            out_specs.append(pl.BlockSpec((None, SEQ, 128), lambda b, p, c: (b, 0, 0)))

    return pl.pallas_call(
        functools.partial(_mixer_kernel, latent=latent, nc=nc),
        grid=(nb, 2, nc),
        in_specs=specs,
        out_specs=out_specs,
        out_shape=out_shape,
        scratch_shapes=scratch,
        compiler_params=pltpu.CompilerParams(
            dimension_semantics=("arbitrary", "arbitrary", "arbitrary"), vmem_limit_bytes=VMEM_LIMIT),
        name="mixer_latent" if latent else "mixer_context",
    )(*args)


LAT_GROUP = 4


def _chunk_off(c):
    return c * CHUNK if isinstance(c, int) else pl.multiple_of(c * CHUNK, CHUNK)


def _mixer_kernel(*refs, latent, group, ng):
    if latent:
        (qa_ref, ka_ref, va_ref, kx_ref, vx_ref, ub_ref, vb_ref, qc_ref, kc_ref, vc_ref, gf_ref, gb_ref,
         pd_ref, pdp_ref, pdn_ref, cos_ref, sin_ref, s0_ref, dmat_ref, dec_ref, sink_ref, ws_ref, bias_ref,
         vec_ref, gmat_ref, cnt_ref, wpool_ref,
         mix_ref, sf_scr, sb_scr, ob_scr, pext_scr, k_scr, v_scr) = refs
    else:
        (qa_ref, ka_ref, va_ref, ub_ref, vb_ref, qc_ref, kc_ref, vc_ref, gf_ref, gb_ref, pd_ref,
         dmat_ref, dec_ref, sink_ref, ws_ref, bias_ref, vec_ref, gmat_ref, cnt_ref, wpool_ref,
         mix_ref, st_ref, kn_ref, vn_ref, sf_scr, sb_scr, ob_scr, pext_scr) = refs

    nc = group * ng
    masks = _head_masks(GW)
    row = lax.broadcasted_iota(jnp.int32, (GW, GW), 0)
    col = lax.broadcasted_iota(jnp.int32, (GW, GW), 1)
    bd_mask = (row // HEAD_DIM) == (col // HEAD_DIM)
    vec = vec_ref[...]
    lane = lax.broadcasted_iota(jnp.int32, (1, 2 * HEAD_DIM), 1)
    lo = lane < HEAD_DIM

    def rows(k):
        return slice(k * CHUNK, (k + 1) * CHUNK)

    def backward_chunk(k, c):
        if latent:
            dst = pl.ds(_chunk_off(c + 1), CHUNK)
            k_scr[dst, :] = _rope(ka_ref[rows(k), :], cos_ref[rows(k), :], sin_ref[rows(k), :]).astype(BF16)
            v_scr[dst, :] = va_ref[rows(k), :].astype(BF16)
        dec = dec_ref[...]
        o_b = _retention_chunk(qc_ref[rows(k), :], kc_ref[rows(k), :], vc_ref[rows(k), :], dmat_ref[1],
                               dec[1], dec[3], vec[6:7], sb_scr, masks, bd_mask)
        ob_scr[pl.ds(_chunk_off(c), CHUNK), :] = o_b

    def forward_chunk(k, c):
        q = qa_ref[rows(k), :]
        if latent:
            q = _rope(q, cos_ref[rows(k), :], sin_ref[rows(k), :])
        q = q * (HEAD_DIM ** -0.5)
        q0, q1 = q[:, :128], q[:, 128:]
        zero = jnp.zeros_like(q0)
        q_st = jnp.concatenate([
            jnp.where(lo, q0, zero),
            jnp.where(lo, pltpu.roll(q0, HEAD_DIM, axis=1), zero),
            jnp.where(lo, zero, pltpu.roll(q1, HEAD_DIM, axis=1)),
            jnp.where(lo, zero, q1)], axis=0).astype(BF16)
        if latent:
            band = pl.ds(_chunk_off(c), 3 * CHUNK)
            k_all = jnp.concatenate([k_scr[band, :], kx_ref[...].astype(BF16)], axis=0)
            v_all = jnp.concatenate([v_scr[band, :], vx_ref[...].astype(BF16)], axis=0)
        else:
            k_all = ka_ref[...].astype(BF16)
            v_all = va_ref[...].astype(BF16)
        s = lax.dot_general(q_st, k_all, (((1,), (1,)), ((), ())), preferred_element_type=F32)
        if latent:
            nk = 3 * CHUNK + PAST_LEN
            qi = lax.broadcasted_iota(jnp.int32, (N_HEADS * CHUNK, nk), 0) & (CHUNK - 1)
            kj = lax.broadcasted_iota(jnp.int32, (N_HEADS * CHUNK, nk), 1)
            kpos = kj + (c - 1) * CHUNK
            valid = (kj >= 3 * CHUNK) | ((kj >= qi) & (kj <= qi + 2 * CHUNK)
                                         & (kpos >= 0) & (kpos < nc * CHUNK))
            s = jnp.where(valid, s, NEG_INF)
        sink = sink_ref[...][:, 0:1]
        mx = jnp.maximum(jnp.max(s, axis=-1, keepdims=True), sink)
        e = jnp.exp(s - mx)
        den = jnp.sum(e, axis=-1, keepdims=True) + jnp.exp(sink - mx)
        o = jnp.dot(e.astype(BF16), v_all, preferred_element_type=F32) / den
        mix_ref[rows(k), 0:128] = jnp.where(lo, o[0:CHUNK], pltpu.roll(o[CHUNK:2 * CHUNK], HEAD_DIM, axis=1))
        mix_ref[rows(k), 128:256] = jnp.where(lo, pltpu.roll(o[2 * CHUNK:3 * CHUNK], HEAD_DIM, axis=1),
                                              o[3 * CHUNK:4 * CHUNK])

        vn = _layer_norm(vb_ref[rows(k), :], vec[0:1], vec[1:2]).astype(BF16)
        sg = jnp.dot(ws_ref[...], _stack_heads(vn, masks), preferred_element_type=F32) + bias_ref[...]
        mix_ref[rows(k), GW:2 * GW] = ub_ref[rows(k), :] * sg

        dec = dec_ref[...]
        o_f = _retention_chunk(qc_ref[rows(k), :], kc_ref[rows(k), :], vc_ref[rows(k), :], dmat_ref[0],
                               dec[0], dec[2], vec[5:6], sf_scr, masks, bd_mask)
        o_b = ob_scr[pl.ds(_chunk_off(c), CHUNK), :]
        gmat = gmat_ref[...]
        mix_ref[rows(k), 2 * GW:3 * GW] = (
            _silu(gf_ref[rows(k), :]) * (_group_norm(o_f, gmat) * vec[3:4])
            + _silu(gb_ref[rows(k), :]) * (_group_norm(o_b, gmat) * vec[4:5]))

        pd = pd_ref[rows(k), :]
        pext = pext_scr.at[k]
        zeros8 = jnp.zeros((POOL_HALO, GW), F32)
        if k > 0:
            prev8 = pd_ref[k * CHUNK - POOL_HALO:k * CHUNK, :]
        else:
            prev8 = jnp.where(c > 0, pdp_ref[...], 0.0) if latent else zeros8
        if k < group - 1:
            next8 = pd_ref[(k + 1) * CHUNK:(k + 1) * CHUNK + POOL_HALO, :]
        else:
            next8 = jnp.where(c < nc - 1, pdn_ref[...], 0.0) if latent else zeros8
        pext[0:POOL_HALO, :] = prev8
        pext[POOL_HALO:POOL_HALO + CHUNK, :] = pd
        pext[POOL_HALO + CHUNK:2 * POOL_HALO + CHUNK, :] = next8

        def win(d, half):
            return pext[pl.ds(POOL_HALO + d, CHUNK), half * 128:(half + 1) * 128]

        a2 = win(-1, 0) + win(0, 0)
        a4 = a2 + win(-2, 0) + win(1, 0)
        a8 = win(-4, 1)
        for d in range(-3, 4):
            a8 = a8 + win(d, 1)
        a16 = a8
        for d in list(range(-8, -4)) + list(range(4, 8)):
            a16 = a16 + win(d, 1)
        sums = jnp.concatenate([jnp.where(lo, a2, a4), jnp.where(lo, a8, a16)], axis=1)
        yd = sums * cnt_ref[rows(k), :] - pd
        mix_ref[rows(k), 3 * GW:4 * GW] = _bdot(yd, wpool_ref[...]) * vec[2:3]

    if not latent:
        sb_scr[...] = jnp.zeros((GW, GW), F32)
        for k in reversed(range(group)):
            backward_chunk(k, k)
        _store_state(st_ref, 1, sb_scr)
        sf_scr[...] = jnp.zeros((GW, GW), F32)
        kn_ref[...] = ka_ref[...]
        vn_ref[...] = va_ref[...]
        for k in range(group):
            forward_chunk(k, k)
        _store_state(st_ref, 0, sf_scr)
        return

    p = pl.program_id(1)
    g = pl.program_id(2)

    @pl.when(p == 0)
    def _():
        @pl.when(g == 0)
        def _():
            _load_state(sb_scr, s0_ref, 1)
            zero_blk = jnp.zeros((CHUNK, 2 * HEAD_DIM), BF16)
            for scr in (k_scr, v_scr):
                scr[0:CHUNK, :] = zero_blk
                scr[(nc + 1) * CHUNK:(nc + 2) * CHUNK, :] = zero_blk

        for k in reversed(range(group)):
            backward_chunk(k, (ng - 1 - g) * group + k)

    @pl.when(p == 1)
    def _():
        @pl.when(g == 0)
        def _():
            _load_state(sf_scr, s0_ref, 0)

        for k in range(group):
            forward_chunk(k, g * group + k)


def _mixer(z, tabs, l, latent, extra=None):
    nb = DEC_BATCH if latent else BATCH
    nc = (DEC_SEQ if latent else SEQ) // CHUNK
    group = LAT_GROUP if latent else nc
    ng = nc // group
    blk = group * CHUNK
    base = (N_CTX // blk) if latent else 0
    per8 = blk // POOL_HALO
    last_halo = ROWS // POOL_HALO - 1

    def on_grid(f):
        return (lambda b, p, g: f(b, p, g)) if latent else (lambda b: f(b, 1, 0))

    def fwd(b, p, g):
        return base + b * ng + g * p

    def both(b, p, g):
        return base + b * ng + jnp.where(p == 0, ng - 1 - g, g)

    def bwd_only(b, p, g):
        return base + b * ng + (ng - 1 - g) * (1 - p)

    def col(width, idx, rowmap):
        return pl.BlockSpec((blk, width), on_grid(lambda b, p, g: (rowmap(b, p, g), idx)))

    def const(shape):
        return pl.BlockSpec(shape, on_grid(lambda b, p, g: (0,) * len(shape)))

    def layer(shape):
        return pl.BlockSpec((None,) + shape, on_grid(lambda b, p, g: (l,) + (0,) * len(shape)))

    specs, args = [], []

    def add(spec, arr):
        specs.append(spec)
        args.append(arr)

    add(col(GW, 0, fwd), z)
    add(col(128, 2, bwd_only if latent else fwd), z)
    add(col(128, 3, bwd_only if latent else fwd), z)
    if latent:
        add(pl.BlockSpec((None, None, PAST_LEN, 128), lambda b, p, g: (b, l, 0, 0)), extra["ck"])
        add(pl.BlockSpec((None, None, PAST_LEN, 128), lambda b, p, g: (b, l, 0, 0)), extra["cv"])
    add(col(GW, 2, fwd), z)
    add(col(GW, 3, fwd), z)
    add(col(GW, 4, both), z)
    add(col(GW, 5, both), z)
    add(col(GW, 6, both), z)
    add(col(GW, 7, fwd), z)
    add(col(GW, 8, fwd), z)
    add(col(GW, 9, fwd), z)
    if latent:
        add(pl.BlockSpec((POOL_HALO, GW),
                         lambda b, p, g: (jnp.maximum(fwd(b, p, g) * per8 - 1, 0), 9)), z)
        add(pl.BlockSpec((POOL_HALO, GW),
                         lambda b, p, g: (jnp.minimum((fwd(b, p, g) + 1) * per8, last_halo), 9)), z)
        rope_map = lambda b, p, g: (jnp.where(p == 0, ng - 1 - g, g), 0)
        add(pl.BlockSpec((blk, 128), rope_map), extra["cos"])
        add(pl.BlockSpec((blk, 128), rope_map), extra["sin"])
        add(pl.BlockSpec((None, None, 2, N_HEADS, HEAD_DIM, HEAD_DIM),
                         lambda b, p, g: (b, l, 0, 0, 0, 0)), extra["s0"])
    add(layer((2, N_HEADS * CHUNK, CHUNK)), tabs["dmat"])
    add(layer((4, CHUNK, GW)), tabs["dec"])
    add(layer((N_HEADS * CHUNK, 128)), tabs["sink"])
    add(layer((CHUNK, N_HEADS * CHUNK)), tabs["ws"])
    add(layer((CHUNK, GW)), tabs["bias"])
    add(layer((8, GW)), tabs["vec"])
    add(const((GW, GW)), tabs["gmat"])
    add(pl.BlockSpec((blk, GW), on_grid(lambda b, p, g: (g * p, 0))),
        tabs["cnt_lat"] if latent else tabs["cnt_ctx"])
    add(layer((GW, GW)), tabs["wpool"])

    out_shape = [jax.ShapeDtypeStruct((nb * nc * CHUNK, D_MODEL), F32)]
    out_specs = [pl.BlockSpec((blk, D_MODEL), on_grid(lambda b, p, g: (b * ng + g * p, 0)))]
    scratch = [pltpu.VMEM((GW, GW), F32), pltpu.VMEM((GW, GW), F32),
               pltpu.VMEM((nc * CHUNK, GW), F32),
               pltpu.VMEM((group, CHUNK + 2 * POOL_HALO, GW), F32)]
    if latent:
        scratch += [pltpu.VMEM(((nc + 2) * CHUNK, 128), BF16), pltpu.VMEM(((nc + 2) * CHUNK, 128), BF16)]
    else:
        out_shape.append(jax.ShapeDtypeStruct((nb, 2, N_HEADS, HEAD_DIM, HEAD_DIM), F32))
        out_specs.append(pl.BlockSpec((None, 2, N_HEADS, HEAD_DIM, HEAD_DIM), lambda b: (b, 0, 0, 0, 0)))
        for _ in range(2):
            out_shape.append(jax.ShapeDtypeStruct((nb, SEQ, 128), F32))
            out_specs.append(pl.BlockSpec((None, SEQ, 128), lambda b: (b, 0, 0)))

    return pl.pallas_call(
        functools.partial(_mixer_kernel, latent=latent, group=group, ng=ng),
        grid=(nb, 2, ng) if latent else (nb,),
        in_specs=specs,
        out_specs=out_specs,
        out_shape=out_shape,
        scratch_shapes=scratch,
        compiler_params=pltpu.CompilerParams(
            dimension_semantics=("arbitrary",) * (3 if latent else 1), vmem_limit_bytes=VMEM_LIMIT),
        name="mixer_latent" if latent else "mixer_context",
    )(*args)


def _pad_rows(rows, n=8):
    a = jnp.stack(rows)
    return jnp.concatenate([a, jnp.zeros((n - a.shape[0],) + a.shape[1:], a.dtype)], axis=0)


def _block_diag(blocks):
    g, n, _ = blocks.shape
    eye = jnp.eye(g, dtype=blocks.dtype)
    return (eye[:, None, :, None] * blocks[:, :, None, :]).reshape(g * n, g * n)


def _inv_count(n):
    t = np.arange(n)
    cols = []
    for w in POOL_WINDOWS:
        cnt = np.clip(t + w // 2, 0, n) - np.clip(t - w // 2, 0, n)
        cols.append(np.repeat((1.0 / cnt)[:, None], HEAD_DIM, axis=1))
    return jnp.asarray(np.concatenate(cols, axis=1), F32)


def _rope_tables():
    rows = DEC_SEQ // GRID_W
    r, cc = jnp.meshgrid(jnp.arange(rows), jnp.arange(GRID_W), indexing="ij")
    half = HEAD_DIM // 2
    freqs = ROPE_BASE ** (-jnp.arange(0, half, 2, dtype=F32) / half)

    def tables(pos):
        ang = pos.reshape(-1).astype(F32)[:, None] * freqs[None, :]
        cos, sin = jnp.cos(ang), jnp.sin(ang)
        return jnp.concatenate([cos, cos], axis=1), jnp.concatenate([-sin, sin], axis=1)

    cr, sr = tables(r)
    ccol, scol = tables(cc)
    cos = jnp.concatenate([cr, ccol], axis=1)
    sin = jnp.concatenate([sr, scol], axis=1)
    return jnp.tile(cos, (1, 2)), jnp.tile(sin, (1, 2))


def _layer_tables(attn_sink, sgu_norm_w, sgu_norm_b, sgu_ws, sgu_bs, ret_decay, ret_gn_w, pool_w, pool_scale):
    log_g = jax.nn.log_sigmoid(ret_decay.astype(F32))
    i = jnp.arange(CHUNK, dtype=F32)
    rel = i[:, None] - i[None, :]
    kscale = HEAD_DIM ** -0.5
    d_f = jnp.where(rel >= 0, jnp.exp(jnp.maximum(rel, 0.0)[None] * log_g[0][:, None, None]), 0.0)
    d_b = jnp.where(rel <= 0, jnp.exp(jnp.maximum(-rel, 0.0)[None] * log_g[1][:, None, None]), 0.0)
    dmat = jnp.stack([d_f.reshape(N_HEADS * CHUNK, CHUNK), d_b.reshape(N_HEADS * CHUNK, CHUNK)]) * kscale

    def lanes(per_head):
        return jnp.repeat(per_head, HEAD_DIM, axis=1)

    qdec_f = lanes(jnp.exp((i + 1.0)[:, None] * log_g[0][None, :]))
    qdec_b = lanes(jnp.exp((CHUNK - i)[:, None] * log_g[1][None, :]))
    kdec_f = lanes(jnp.exp((CHUNK - 1.0 - i)[:, None] * log_g[0][None, :])) * kscale
    kdec_b = lanes(jnp.exp(i[:, None] * log_g[1][None, :])) * kscale
    cdec = jnp.repeat(jnp.exp(CHUNK * log_g), HEAD_DIM, axis=1)
    vec = _pad_rows([sgu_norm_w, sgu_norm_b, pool_scale, ret_gn_w[0], ret_gn_w[1], cdec[0], cdec[1]])
    return {
        "dmat": dmat,
        "dec": jnp.stack([qdec_f, qdec_b, kdec_f, kdec_b]),
        "sink": jnp.broadcast_to(jnp.repeat(attn_sink, CHUNK)[:, None], (N_HEADS * CHUNK, 128)),
        "ws": jnp.concatenate([sgu_ws[h] for h in range(N_HEADS)], axis=1).astype(BF16),
        "bias": jnp.repeat(sgu_bs.T, HEAD_DIM, axis=1),
        "vec": vec,
        "wpool": _block_diag(pool_w).astype(BF16),
    }


def kernel(x_prompt, x_sample, cache_attn_k, cache_attn_v, state_ret, c, c_ctx, w_ada, b_ada, w_in,
           w_out, attn_sink, sgu_norm_w, sgu_norm_b, sgu_ws, sgu_bs, ret_decay, ret_gn_w, pool_w,
           pool_scale, ffn_up, ffn_conv_w, ffn_conv_b, ffn_down, ln_w, ln_b):
    cond8 = jnp.concatenate([c_ctx[None], c, jnp.zeros((8 - 1 - DEC_BATCH, D_MODEL), F32)], axis=0)
    mod = _modulation(cond8, w_ada, b_ada).reshape(DEPTH, 8, 6, D_MODEL)

    tabs = jax.vmap(_layer_tables)(attn_sink, sgu_norm_w, sgu_norm_b, sgu_ws, sgu_bs, ret_decay, ret_gn_w,
                                   pool_w, pool_scale)
    tabs["gmat"] = _block_diag(jnp.full((N_HEADS, HEAD_DIM, HEAD_DIM), 1.0 / HEAD_DIM, F32)).astype(BF16)
    tabs["cnt_ctx"] = _inv_count(SEQ)
    tabs["cnt_lat"] = _inv_count(DEC_SEQ)
    cos, sin = _rope_tables()
    extra = {"ck": cache_attn_k.reshape(DEC_BATCH, DEPTH, PAST_LEN, 128),
             "cv": cache_attn_v.reshape(DEC_BATCH, DEPTH, PAST_LEN, 128),
             "cos": cos, "sin": sin, "s0": state_ret}
    ln8 = jnp.concatenate([ln_w, ln_b, jnp.zeros((DEPTH, 4, D_MODEL), F32)], axis=1)
    conv8 = jnp.concatenate([ffn_conv_w, ffn_conv_b[:, None], jnp.zeros((DEPTH, 4, 2 * D_FF), F32)], axis=1)

    down_bf16 = _to_bf16(ffn_down, D_FF // 4)
    w_in_bf16 = _to_bf16(w_in, D_MODEL // 2)
    w_out_bf16 = _to_bf16(w_out, D_MODEL)
    up_chunks = _up_weight_chunks(ffn_up)

    xs = [x_prompt.reshape(N_CTX, D_MODEL), x_sample.reshape(N_LAT, D_MODEL)]
    new_k, new_v, new_s = [], [], []
    for l in range(DEPTH):
        z = _inproj(xs, mod, w_in_bf16, l)
        mix_ctx, st, kn, vn = _mixer(z, tabs, l, latent=False)
        (mix_lat,) = _mixer(z, tabs, l, latent=True, extra=extra)
        x1 = _outproj([mix_ctx, mix_lat], xs, mod, w_out_bf16, ln8, l)
        xs = _ffn(x1, mod, up_chunks, conv8, down_bf16, ln8, l, split_out=(l == DEPTH - 1))
        new_k.append(kn.reshape(BATCH, SEQ, 2, HEAD_DIM))
        new_v.append(vn.reshape(BATCH, SEQ, 2, HEAD_DIM))
        new_s.append(st)

    y_prompt = xs[0].reshape(BATCH, SEQ, D_MODEL)
    y_sample = xs[1].reshape(DEC_BATCH, DEC_SEQ, D_MODEL)
    return (y_prompt, y_sample, jnp.stack(new_k, axis=1), jnp.stack(new_v, axis=1),
            jnp.stack(new_s, axis=1))
```

```python
import functools

import numpy as np
import jax
import jax.numpy as jnp
from jax import lax
from jax.experimental import pallas as pl
from jax.experimental.pallas import tpu as pltpu

F32 = jnp.float32
BF16 = jnp.bfloat16

D_MODEL = 1024
BATCH = 16
SEQ = 256
DEPTH = 2
DEC_BATCH = 2
DEC_SEQ = 2048
PAST_LEN = 256
GRID_W = 64
CHUNK = 128
HEAD_DIM = 64
GW = D_MODEL // 4
N_HEADS = 4
POOL_WINDOWS = (2, 4, 8, 16)
POOL_HALO = 8
D_FF = 2816
ROPE_BASE = 10000.0
LN_EPS = 1e-5
NEG_INF = -1e30
IN_WIDTH = 10 * GW
ALPHA = (2.0 * DEPTH) ** 0.25

N_CTX = BATCH * SEQ
N_LAT = DEC_BATCH * DEC_SEQ
ROWS = N_CTX + N_LAT

TM = 1024
CTX_TILES = N_CTX // TM
LAT_TILES_PER_SEQ = DEC_SEQ // TM
NB_IN = 512
FC = 256
NB_ADA = 1536
VMEM_LIMIT = 56 * 1024 * 1024


def _cond_of_tile(i, tm=TM):
    ctx_tiles = N_CTX // tm
    return jnp.where(i < ctx_tiles, 0, 1 + (i - ctx_tiles) // (DEC_SEQ // tm))


def _tile_specs(n_src, tm, grid_rank):
    ctx_tiles = N_CTX // tm
    if n_src == 1:
        rows = [lambda i: i]
    else:
        rows = [lambda i: jnp.minimum(i, ctx_tiles - 1), lambda i: jnp.maximum(i - ctx_tiles, 0)]
    if grid_rank == 1:
        return [pl.BlockSpec((tm, D_MODEL), lambda i, f=f: (f(i), 0)) for f in rows]
    return [pl.BlockSpec((tm, D_MODEL), lambda i, j, f=f: (f(i), 0)) for f in rows]


def _per_half(i, tm, n_max, fn):
    if n_max == 1:
        fn(0)
        return
    ctx_tiles = N_CTX // tm

    @pl.when(i < ctx_tiles)
    def _():
        fn(0)

    @pl.when(i >= ctx_tiles)
    def _():
        fn(1)


def _layer_norm(x, w, b):
    mu = jnp.mean(x, axis=-1, keepdims=True)
    d = x - mu
    var = jnp.mean(d * d, axis=-1, keepdims=True)
    return d * lax.rsqrt(var + LN_EPS) * w + b


def _silu(x):
    return x * jax.nn.sigmoid(x)


def _bdot(a, b):
    return jnp.dot(a.astype(BF16), b.astype(BF16), preferred_element_type=F32)


def _mod_kernel(c_ref, w_ref, b_ref, o_ref):
    o_ref[...] = _bdot(_silu(c_ref[...]), w_ref[...]) + b_ref[...]


def _modulation(cond8, w_ada, b_ada):
    return pl.pallas_call(
        _mod_kernel,
        grid=(DEPTH, 6 * D_MODEL // NB_ADA),
        in_specs=[
            pl.BlockSpec((8, D_MODEL), lambda l, j: (0, 0)),
            pl.BlockSpec((None, D_MODEL, NB_ADA), lambda l, j: (l, 0, j)),
            pl.BlockSpec((None, 1, NB_ADA), lambda l, j: (l, 0, j)),
        ],
        out_specs=pl.BlockSpec((None, 8, NB_ADA), lambda l, j: (l, 0, j)),
        out_shape=jax.ShapeDtypeStruct((DEPTH, 8, 6 * D_MODEL), F32),
        compiler_params=pltpu.CompilerParams(
            dimension_semantics=("arbitrary", "arbitrary"), vmem_limit_bytes=VMEM_LIMIT),
        name="modulation",
    )(cond8, w_ada, b_ada.reshape(DEPTH, 1, 6 * D_MODEL))


def _cast_kernel(w_ref, o_ref):
    o_ref[...] = w_ref[...].astype(BF16)


def _to_bf16(w, block_rows):
    depth, rows, cols = w.shape
    return pl.pallas_call(
        _cast_kernel,
        grid=(depth, rows // block_rows),
        in_specs=[pl.BlockSpec((None, block_rows, cols), lambda l, r: (l, r, 0))],
        out_specs=pl.BlockSpec((None, block_rows, cols), lambda l, r: (l, r, 0)),
        out_shape=jax.ShapeDtypeStruct(w.shape, BF16),
        compiler_params=pltpu.CompilerParams(dimension_semantics=("arbitrary", "arbitrary")),
        name="cast_bf16",
    )(w)


def _to_bf16_chunks(w, chunk):
    depth, rows, cols = w.shape
    return pl.pallas_call(
        _cast_kernel,
        grid=(depth, cols // chunk),
        in_specs=[pl.BlockSpec((None, rows, chunk), lambda l, c: (l, 0, c))],
        out_specs=pl.BlockSpec((None, None, rows, chunk), lambda l, c: (l, c, 0, 0)),
        out_shape=jax.ShapeDtypeStruct((depth, cols // chunk, rows, chunk), BF16),
        compiler_params=pltpu.CompilerParams(dimension_semantics=("arbitrary", "arbitrary")),
        name="cast_bf16_chunks",
    )(w)


def _inproj_kernel(*refs, n_x):
    x_refs = refs[:n_x]
    mod_ref, w_ref, z_ref, h_scr = refs[n_x:]

    m = mod_ref[...]

    def build(side):
        h_scr[...] = (x_refs[side][...] * (1.0 + m[1:2]) + m[0:1]).astype(BF16)

    _per_half(pl.program_id(0), TM, n_x, build)
    h = h_scr[...]
    for jb in range(IN_WIDTH // NB_IN):
        cols = slice(jb * NB_IN, (jb + 1) * NB_IN)
        z_ref[:, cols] = jnp.dot(h, w_ref[:, cols], preferred_element_type=F32)


def _inproj(xs, mod, w_in_bf16, l):
    return pl.pallas_call(
        functools.partial(_inproj_kernel, n_x=len(xs)),
        grid=(ROWS // TM,),
        in_specs=_tile_specs(len(xs), TM, 1) + [
            pl.BlockSpec((None, None, 6, D_MODEL), lambda i: (l, _cond_of_tile(i), 0, 0)),
            pl.BlockSpec((None, D_MODEL, IN_WIDTH), lambda i: (l, 0, 0), pipeline_mode=pl.Buffered(1)),
        ],
        out_specs=pl.BlockSpec((TM, IN_WIDTH), lambda i: (i, 0)),
        out_shape=jax.ShapeDtypeStruct((ROWS, IN_WIDTH), F32),
        scratch_shapes=[pltpu.VMEM((TM, D_MODEL), BF16)],
        compiler_params=pltpu.CompilerParams(
            dimension_semantics=("arbitrary",), vmem_limit_bytes=VMEM_LIMIT),
        name="inproj",
    )(*xs, mod, w_in_bf16)


TM_OUT = 512


def _outproj_kernel(*refs, n_x):
    mix_refs = refs[:2]
    x_refs = refs[2:2 + n_x]
    mod_ref, w_ref, ln_ref, o_ref = refs[2 + n_x:]
    m = mod_ref[...]
    ln = ln_ref[...]

    def body(side):
        y = jnp.dot(mix_refs[side][...].astype(BF16), w_ref[...], preferred_element_type=F32)
        x = x_refs[min(side, n_x - 1)][...]
        o_ref[...] = _layer_norm(ALPHA * x + m[2:3] * y, ln[0:1], ln[2:3])

    _per_half(pl.program_id(0), TM_OUT, 2, body)


def _outproj(mixes, xs, mod, w_out, ln8, l):
    return pl.pallas_call(
        functools.partial(_outproj_kernel, n_x=len(xs)),
        grid=(ROWS // TM_OUT,),
        in_specs=_tile_specs(2, TM_OUT, 1) + _tile_specs(len(xs), TM_OUT, 1) + [
            pl.BlockSpec((None, None, 6, D_MODEL), lambda i: (l, _cond_of_tile(i, TM_OUT), 0, 0)),
            pl.BlockSpec((None, D_MODEL, D_MODEL), lambda i: (l, 0, 0)),
            pl.BlockSpec((None, 8, D_MODEL), lambda i: (l, 0, 0)),
        ],
        out_specs=pl.BlockSpec((TM_OUT, D_MODEL), lambda i: (i, 0)),
        out_shape=jax.ShapeDtypeStruct((ROWS, D_MODEL), F32),
        compiler_params=pltpu.CompilerParams(
            dimension_semantics=("arbitrary",), vmem_limit_bytes=VMEM_LIMIT),
        name="outproj",
    )(*mixes, *xs, mod, w_out, ln8)


SEG = TM // 8
HALO_ROWS = 16
RB = 256
RBU = 512
GB = 64
NFC = D_FF // FC
LANE_BLOCKS = D_MODEL // 128
SEG_PITCH = SEG + 8


def _seg_rows(xc_ref, k):
    return jnp.concatenate([xc_ref[cb, pl.ds(k, 8, stride=SEG_PITCH), :] for cb in range(LANE_BLOCKS)], axis=1)


def _ffn_kernel_old(*refs, n_out):
    (x_ref, xp_ref, xn_ref, mod_ref, upa_ref, upg_ref, cva_ref, cvg_ref, dn_ref, ln_ref) = refs[:10]
    o_refs = refs[10:10 + n_out]
    h_scr, act_scr, xc_scr, u0_scr, u1_scr, wbf_scr = refs[10 + n_out:]
    u_scrs = (u0_scr, u1_scr)
    i = pl.program_id(0)
    j = pl.program_id(1)
    is_ctx = i < CTX_TILES
    lat_pos = (i - CTX_TILES) % LAT_TILES_PER_SEQ

    def build_h():
        m = mod_ref[...]
        scale = 1.0 + m[4:5]
        shift = m[3:4]
        for cb in range(LANE_BLOCKS):
            for s in range(8):
                xc_scr[cb, s * SEG_PITCH:s * SEG_PITCH + SEG, :] = x_ref[s * SEG:(s + 1) * SEG,
                                                                         cb * 128:(cb + 1) * 128]
        for k in range(0, SEG, 2):
            rows = jnp.concatenate([_seg_rows(xc_scr, k), _seg_rows(xc_scr, k + 1)], axis=0)
            h_scr[8 * k:8 * k + 16, :] = (rows * scale + shift).astype(BF16)
        sub = lax.broadcasted_iota(jnp.int32, (HALO_ROWS, D_MODEL), 0)
        prev_ok = jnp.logical_not(is_ctx) & (lat_pos > 0)
        next_ok = jnp.logical_not(is_ctx) & (lat_pos < LAT_TILES_PER_SEQ - 1)
        halo_x = jnp.where(sub == 0, xp_ref[POOL_HALO - 1:POOL_HALO, :], xn_ref[0:1, :])
        keep = ((sub == 0) & prev_ok) | ((sub == 1) & next_ok)
        h_scr[TM:TM + HALO_ROWS, :] = jnp.where(keep, halo_x * scale + shift, 0.0).astype(BF16)

    n_ub = TM // RBU

    def cast_up_weights():
        wbf_scr[:, 0:FC] = upa_ref[...].astype(BF16)
        wbf_scr[:, FC:2 * FC] = upg_ref[...].astype(BF16)

    def up_proj(slot, ub):
        rows = slice(ub * RBU, (ub + 1) * RBU + (HALO_ROWS if ub == n_ub - 1 else 0))
        u_scrs[slot][rows, :] = jnp.dot(h_scr[rows, :], wbf_scr[...], preferred_element_type=F32)

    def conv(u_ref, lanes, cv, r0):
        lo = max(r0 - 8, 0)
        hi = min(r0 + GB + 8, TM)
        ue = u_ref[lo:hi, lanes]
        u = ue[r0 - lo:r0 - lo + GB]
        if r0 == 0 or r0 == TM - GB:
            sub = lax.broadcasted_iota(jnp.int32, (8, FC), 0)
            seg_per_seq = SEQ // SEG
        if r0 == 0:
            ctx_first = is_ctx & (sub % seg_per_seq == 0)
            b_first = jnp.where(sub == 0, u_ref[TM:TM + 1, lanes],
                                pltpu.roll(u_ref[TM - 8:TM, lanes], 1, axis=0))
            um1 = jnp.concatenate([jnp.where(ctx_first, 0.0, b_first), u[0:GB - 8]], axis=0)
        else:
            um1 = ue[0:GB]
        if r0 == TM - GB:
            ctx_last = is_ctx & (sub % seg_per_seq == seg_per_seq - 1)
            b_last = jnp.where(sub == 7, u_ref[TM + 1:TM + 2, lanes],
                               pltpu.roll(u_ref[0:8, lanes], 7, axis=0))
            up1 = jnp.concatenate([u[8:GB], jnp.where(ctx_last, 0.0, b_last)], axis=0)
        else:
            up1 = ue[r0 - lo + 8:r0 - lo + GB + 8]
        return um1 * cv[0:1] + u * cv[1:2] + up1 * cv[2:3] + cv[3:4]

    def gate(slot, chunk, ub):
        for r0 in range(ub * RBU, (ub + 1) * RBU, GB):
            a = conv(u_scrs[slot], slice(0, FC), cva_ref[...], r0)
            g = conv(u_scrs[slot], slice(FC, 2 * FC), cvg_ref[...], r0)
            act_scr[chunk, r0:r0 + GB, :] = (_silu(a) * g).astype(BF16)

    @pl.when(j == 0)
    def _():
        build_h()
        cast_up_weights()
        for ub in range(n_ub):
            up_proj(0, ub)

    for slot in range(2):
        @pl.when((j >= 1) & (j < NFC) & (j % 2 == slot))
        def _(slot=slot):
            cast_up_weights()
            for ub in range(n_ub):
                up_proj(slot, ub)
                gate(1 - slot, j - 1, ub)

    @pl.when(j == NFC)
    def _():
        for ub in range(n_ub):
            gate((NFC - 1) % 2, NFC - 1, ub)
        m = mod_ref[...]
        ln = ln_ref[...]
        for rb in range(TM // RB):
            vrows = range(rb * RB // 8, (rb + 1) * RB // 8)
            lhs = jnp.concatenate([act_scr[jj, rb * RB:(rb + 1) * RB, :] for jj in range(NFC)], axis=1)
            y = jnp.dot(lhs, dn_ref[...], preferred_element_type=F32)
            xr = jnp.concatenate([_seg_rows(xc_scr, k) for k in vrows], axis=0)
            out = _layer_norm(ALPHA * xr + m[5:6] * y, ln[1:2], ln[3:4])
            for kk, k in enumerate(vrows):
                for cb in range(LANE_BLOCKS):
                    xc_scr[cb, pl.ds(k, 8, stride=SEG_PITCH), :] = out[8 * kk:8 * kk + 8,
                                                                       cb * 128:(cb + 1) * 128]
        def write_out(side):
            for cb in range(LANE_BLOCKS):
                for s in range(8):
                    o_refs[side][s * SEG:(s + 1) * SEG, cb * 128:(cb + 1) * 128] = xc_scr[
                        cb, s * SEG_PITCH:s * SEG_PITCH + SEG, :]

        _per_half(i, TM, n_out, write_out)


def _ffn_old(x, mod, up, conv8, down, ln8, l, split_out):
    halo_blocks = TM // POOL_HALO
    last_halo = ROWS // POOL_HALO - 1

    def up_chunk(j):
        return jnp.minimum(j, NFC - 1)

    def gate_chunk(j):
        return jnp.maximum(j - 1, 0)

    if split_out:
        out_specs = _tile_specs(2, TM, 2)
        out_shape = [jax.ShapeDtypeStruct((N_CTX, D_MODEL), F32), jax.ShapeDtypeStruct((N_LAT, D_MODEL), F32)]
    else:
        out_specs = _tile_specs(1, TM, 2)
        out_shape = [jax.ShapeDtypeStruct((ROWS, D_MODEL), F32)]
    return pl.pallas_call(
        functools.partial(_ffn_kernel, n_out=len(out_shape)),
        grid=(ROWS // TM, NFC + 1),
        in_specs=[
            pl.BlockSpec((TM, D_MODEL), lambda i, j: (i, 0)),
            pl.BlockSpec((POOL_HALO, D_MODEL), lambda i, j: (jnp.maximum(i * halo_blocks - 1, 0), 0)),
            pl.BlockSpec((POOL_HALO, D_MODEL),
                         lambda i, j: (jnp.minimum((i + 1) * halo_blocks, last_halo), 0)),
            pl.BlockSpec((None, None, 6, D_MODEL), lambda i, j: (l, _cond_of_tile(i), 0, 0)),
            pl.BlockSpec((None, D_MODEL, FC), lambda i, j: (l, 0, up_chunk(j))),
            pl.BlockSpec((None, D_MODEL, FC), lambda i, j: (l, 0, NFC + up_chunk(j))),
            pl.BlockSpec((None, 8, FC), lambda i, j: (l, 0, gate_chunk(j))),
            pl.BlockSpec((None, 8, FC), lambda i, j: (l, 0, NFC + gate_chunk(j))),
            pl.BlockSpec((None, D_FF, D_MODEL), lambda i, j: (l, 0, 0), pipeline_mode=pl.Buffered(1)),
            pl.BlockSpec((None, 8, D_MODEL), lambda i, j: (l, 0, 0)),
        ],
        out_specs=out_specs,
        out_shape=out_shape,
        scratch_shapes=[pltpu.VMEM((TM + HALO_ROWS, D_MODEL), BF16),
                        pltpu.VMEM((NFC, TM, FC), BF16),
                        pltpu.VMEM((LANE_BLOCKS, 8 * SEG_PITCH, 128), F32),
                        pltpu.VMEM((TM + HALO_ROWS, 2 * FC), F32),
                        pltpu.VMEM((TM + HALO_ROWS, 2 * FC), F32),
                        pltpu.VMEM((D_MODEL, 2 * FC), BF16)],
        compiler_params=pltpu.CompilerParams(
            dimension_semantics=("arbitrary", "arbitrary"), vmem_limit_bytes=VMEM_LIMIT),
        name="convffn",
    )(x, x, x, mod, up, up, conv8, conv8, down, ln8)


TMF = 512
SEGF = TMF // 8
PITCHF = SEGF + 8
CTXF_TILES = N_CTX // TMF
LATF_PER_SEQ = DEC_SEQ // TMF
RBUF = 256


def _up_weight_kernel(a_ref, g_ref, o_ref):
    for c in range(NFC):
        o_ref[c, :, 0:FC] = a_ref[:, c * FC:(c + 1) * FC].astype(BF16)
        o_ref[c, :, FC:2 * FC] = g_ref[:, c * FC:(c + 1) * FC].astype(BF16)


def _up_weight_chunks(ffn_up):
    half = D_MODEL // 2
    return pl.pallas_call(
        _up_weight_kernel,
        grid=(DEPTH, 2),
        in_specs=[pl.BlockSpec((None, half, D_FF), lambda l, r: (l, r, 0)),
                  pl.BlockSpec((None, half, D_FF), lambda l, r: (l, r, 1))],
        out_specs=pl.BlockSpec((None, NFC, half, 2 * FC), lambda l, r: (l, 0, r, 0)),
        out_shape=jax.ShapeDtypeStruct((DEPTH, NFC, D_MODEL, 2 * FC), BF16),
        compiler_params=pltpu.CompilerParams(
            dimension_semantics=("arbitrary", "arbitrary"), vmem_limit_bytes=VMEM_LIMIT),
        name="cast_up_chunks",
    )(ffn_up, ffn_up)


def _seg_rows_f(xc_ref, k):
    return jnp.concatenate([xc_ref[cb, pl.ds(k, 8, stride=PITCHF), :] for cb in range(LANE_BLOCKS)], axis=1)


def _ffn_kernel(*refs, n_out):
    x_refs = refs[:2]
    (xp_ref, xn_ref, mod_ref, up_ref, cv_ref, dn_ref, ln_ref) = refs[2:9]
    o_refs = refs[9:9 + n_out]
    h_scr, act_scr, xc_scr, u0_scr, u1_scr = refs[9 + n_out:]
    u_scrs = (u0_scr, u1_scr)
    i = pl.program_id(0)
    is_ctx = i < CTXF_TILES
    lat_pos = (i - CTXF_TILES) % LATF_PER_SEQ
    m = mod_ref[...]
    ln = ln_ref[...]

    scale = 1.0 + m[4:5]
    shift = m[3:4]
    def stage(side):
        for cb in range(LANE_BLOCKS):
            for s in range(8):
                xc_scr[cb, s * PITCHF:s * PITCHF + SEGF, :] = x_refs[side][s * SEGF:(s + 1) * SEGF,
                                                                            cb * 128:(cb + 1) * 128]

    _per_half(i, TMF, 2, stage)
    for k in range(0, SEGF, 2):
        rows = jnp.concatenate([_seg_rows_f(xc_scr, k), _seg_rows_f(xc_scr, k + 1)], axis=0)
        h_scr[8 * k:8 * k + 16, :] = (rows * scale + shift).astype(BF16)
    sub16 = lax.broadcasted_iota(jnp.int32, (HALO_ROWS, D_MODEL), 0)
    prev_ok = jnp.logical_not(is_ctx) & (lat_pos > 0)
    next_ok = jnp.logical_not(is_ctx) & (lat_pos < LATF_PER_SEQ - 1)
    halo_x = jnp.where(sub16 == 0, xp_ref[POOL_HALO - 1:POOL_HALO, :], xn_ref[0:1, :])
    keep = ((sub16 == 0) & prev_ok) | ((sub16 == 1) & next_ok)
    h_scr[TMF:TMF + HALO_ROWS, :] = jnp.where(keep, halo_x * scale + shift, 0.0).astype(BF16)

    n_ub = TMF // RBUF
    sub = lax.broadcasted_iota(jnp.int32, (8, FC), 0)
    seg_per_seq = SEQ // SEGF
    ctx_first = is_ctx & (sub % seg_per_seq == 0)
    ctx_last = is_ctx & (sub % seg_per_seq == seg_per_seq - 1)

    def up_proj(slot, c, ub):
        rows = slice(ub * RBUF, (ub + 1) * RBUF + (HALO_ROWS if ub == n_ub - 1 else 0))
        u_scrs[slot][rows, :] = jnp.dot(h_scr[rows, :], up_ref[c], preferred_element_type=F32)

    def conv(u_ref, lanes, cvs, r0):
        lo = max(r0 - 8, 0)
        hi = min(r0 + GB + 8, TMF)
        ue = u_ref[lo:hi, lanes]
        u = ue[r0 - lo:r0 - lo + GB]
        if r0 == 0:
            b_first = jnp.where(sub == 0, u_ref[TMF:TMF + 1, lanes],
                                pltpu.roll(u_ref[TMF - 8:TMF, lanes], 1, axis=0))
            um1 = jnp.concatenate([jnp.where(ctx_first, 0.0, b_first), u[0:GB - 8]], axis=0)
        else:
            um1 = ue[0:GB]
        if r0 == TMF - GB:
            b_last = jnp.where(sub == 7, u_ref[TMF + 1:TMF + 2, lanes],
                               pltpu.roll(u_ref[0:8, lanes], 7, axis=0))
            up1 = jnp.concatenate([u[8:GB], jnp.where(ctx_last, 0.0, b_last)], axis=0)
        else:
            up1 = ue[r0 - lo + 8:r0 - lo + GB + 8]
        return um1 * cvs[0:1] + u * cvs[1:2] + up1 * cvs[2:3] + cvs[3:4]

    def gate(slot, c, ub):
        cva = cv_ref[0:4, c * FC:(c + 1) * FC]
        cvg = cv_ref[0:4, D_FF + c * FC:D_FF + (c + 1) * FC]
        for r0 in range(ub * RBUF, (ub + 1) * RBUF, GB):
            a = conv(u_scrs[slot], slice(0, FC), cva, r0)
            g = conv(u_scrs[slot], slice(FC, 2 * FC), cvg, r0)
            act_scr[c, r0:r0 + GB, :] = (_silu(a) * g).astype(BF16)

    for c in range(NFC + 1):
        for ub in range(n_ub):
            if c < NFC:
                up_proj(c % 2, c, ub)
            if c >= 1:
                gate((c - 1) % 2, c - 1, ub)

    for rb in range(TMF // RB):
        vrows = range(rb * RB // 8, (rb + 1) * RB // 8)
        lhs = jnp.concatenate([act_scr[c, rb * RB:(rb + 1) * RB, :] for c in range(NFC)], axis=1)
        y = jnp.dot(lhs, dn_ref[...], preferred_element_type=F32)
        xr = jnp.concatenate([_seg_rows_f(xc_scr, k) for k in vrows], axis=0)
        out = _layer_norm(ALPHA * xr + m[5:6] * y, ln[1:2], ln[3:4])
        for kk, k in enumerate(vrows):
            for cb in range(LANE_BLOCKS):
                xc_scr[cb, pl.ds(k, 8, stride=PITCHF), :] = out[8 * kk:8 * kk + 8, cb * 128:(cb + 1) * 128]

    def write_out(side):
        for cb in range(LANE_BLOCKS):
            for s in range(8):
                o_refs[side][s * SEGF:(s + 1) * SEGF, cb * 128:(cb + 1) * 128] = xc_scr[
                    cb, s * PITCHF:s * PITCHF + SEGF, :]

    _per_half(i, TMF, n_out, write_out)


def _ffn(xs, mod, up_chunks, conv8, down_bf16, ln8, l, split_out):
    halo_blocks = TMF // POOL_HALO
    last_halo = N_LAT // POOL_HALO - 1
    n_out = 2 if split_out else 1
    if split_out:
        out_shape = [jax.ShapeDtypeStruct((N_CTX, D_MODEL), F32), jax.ShapeDtypeStruct((N_LAT, D_MODEL), F32)]
    else:
        out_shape = [jax.ShapeDtypeStruct((ROWS, D_MODEL), F32)]
    return pl.pallas_call(
        functools.partial(_ffn_kernel, n_out=n_out),
        grid=(ROWS // TMF,),
        in_specs=[
            *_tile_specs(2, TMF, 1),
            pl.BlockSpec((POOL_HALO, D_MODEL),
                         lambda i: (jnp.maximum((i - CTXF_TILES) * halo_blocks - 1, 0), 0)),
            pl.BlockSpec((POOL_HALO, D_MODEL),
                         lambda i: (jnp.clip((i - CTXF_TILES + 1) * halo_blocks, 0, last_halo), 0)),
            pl.BlockSpec((None, None, 6, D_MODEL), lambda i: (l, _cond_of_tile(i, TMF), 0, 0)),
            pl.BlockSpec((None, NFC, D_MODEL, 2 * FC), lambda i: (l, 0, 0, 0), pipeline_mode=pl.Buffered(1)),
            pl.BlockSpec((None, 8, 2 * D_FF), lambda i: (l, 0, 0)),
            pl.BlockSpec((None, D_FF, D_MODEL), lambda i: (l, 0, 0), pipeline_mode=pl.Buffered(1)),
            pl.BlockSpec((None, 8, D_MODEL), lambda i: (l, 0, 0)),
        ],
        out_specs=_tile_specs(n_out, TMF, 1),
        out_shape=out_shape,
        scratch_shapes=[pltpu.VMEM((TMF + HALO_ROWS, D_MODEL), BF16),
                        pltpu.VMEM((NFC, TMF, FC), BF16),
                        pltpu.VMEM((LANE_BLOCKS, 8 * PITCHF, 128), F32),
                        pltpu.VMEM((TMF + HALO_ROWS, 2 * FC), F32),
                        pltpu.VMEM((TMF + HALO_ROWS, 2 * FC), F32)],
        compiler_params=pltpu.CompilerParams(
            dimension_semantics=("arbitrary",), vmem_limit_bytes=VMEM_LIMIT),
        name="convffn",
    )(xs[0], xs[1], xs[1], xs[1], mod, up_chunks, conv8, down_bf16, ln8)


def _head_masks(width):
    lane = lax.broadcasted_iota(jnp.int32, (1, width), 1)
    return [(lane >= h * HEAD_DIM) & (lane < (h + 1) * HEAD_DIM) for h in range(width // HEAD_DIM)]


def _stack_heads(x, masks):
    return jnp.concatenate([jnp.where(m, x, jnp.zeros_like(x)) for m in masks], axis=0)


def _rope(x, cos, sin):
    lane = lax.broadcasted_iota(jnp.int32, (1, 128), 1)
    lower = (lane & 31) < 16
    outs = []
    for k in range(x.shape[1] // 128):
        xb = x[:, k * 128:(k + 1) * 128]
        partner = jnp.where(lower, pltpu.roll(xb, 112, axis=1), pltpu.roll(xb, 16, axis=1))
        outs.append(xb * cos + partner * sin)
    return outs[0] if len(outs) == 1 else jnp.concatenate(outs, axis=1)


def _retention_chunk(q, k, v, dmat, qdec, kdec, cdec, state_scr, masks, bd_mask):
    qb = q.astype(BF16)
    kb = k.astype(BF16)
    vb = v.astype(BF16)
    s = lax.dot_general(_stack_heads(qb, masks), kb, (((1,), (1,)), ((), ())),
                        preferred_element_type=F32)
    p = (s * dmat).astype(BF16)
    p_cat = jnp.concatenate([p[h * CHUNK:(h + 1) * CHUNK] for h in range(N_HEADS)], axis=1)
    state = state_scr[...]
    o = (jnp.dot(p_cat, _stack_heads(vb, masks), preferred_element_type=F32)
         + _bdot(q * qdec, state))
    upd = lax.dot_general((k * kdec).astype(BF16), vb, (((0,), (0,)), ((), ())),
                          preferred_element_type=F32)
    state_scr[...] = state * cdec + jnp.where(bd_mask, upd, 0.0)
    return o


def _load_state(state_scr, blocks_ref, d):
    state_scr[...] = jnp.zeros((GW, GW), F32)
    for h in range(N_HEADS):
        sl = slice(h * HEAD_DIM, (h + 1) * HEAD_DIM)
        state_scr[sl, sl] = blocks_ref[d, h]


def _store_state(blocks_ref, d, state_scr):
    for h in range(N_HEADS):
        sl = slice(h * HEAD_DIM, (h + 1) * HEAD_DIM)
        blocks_ref[d, h] = state_scr[sl, sl]


def _group_norm(o, gmat):
    mu = jnp.dot(o.astype(BF16), gmat, preferred_element_type=F32)
    d = o - mu
    var = jnp.dot((d * d).astype(BF16), gmat, preferred_element_type=F32)
    return d * lax.rsqrt(var + LN_EPS)


def _mixer_kernel_old(*refs, latent, nc):
    if latent:
        (qa_ref, ka_ref, va_ref, kx_ref, vx_ref, ub_ref, vb_ref, qc_ref, kc_ref, vc_ref, gf_ref,
         gb_ref, pd_ref, pdp_ref, pdn_ref, cos_ref, sin_ref, s0_ref, dmat_ref, dec_ref, sink_ref,
         ws_ref, bias_ref, vec_ref, gmat_ref, cnt_ref, wpool_ref,
         mix_ref, sf_scr, sb_scr, ob_scr, pext_scr, k_scr, v_scr) = refs
    else:
        (qa_ref, kx_ref, vx_ref, ub_ref, vb_ref, qc_ref, kc_ref, vc_ref, gf_ref,
         gb_ref, pd_ref, pdp_ref, pdn_ref, dmat_ref, dec_ref, sink_ref,
         ws_ref, bias_ref, vec_ref, gmat_ref, cnt_ref, wpool_ref,
         mix_ref, st_ref, kn_ref, vn_ref, sf_scr, sb_scr, ob_scr, pext_scr) = refs

    p = pl.program_id(1)
    c = pl.program_id(2)
    masks = _head_masks(GW)
    row = lax.broadcasted_iota(jnp.int32, (GW, GW), 0)
    col = lax.broadcasted_iota(jnp.int32, (GW, GW), 1)
    bd_mask = (row // HEAD_DIM) == (col // HEAD_DIM)
    vec = vec_ref[...]

    @pl.when(p == 0)
    def _():
        rc = nc - 1 - c

        @pl.when(c == 0)
        def _():
            if latent:
                _load_state(sb_scr, s0_ref, 1)
                zero_blk = jnp.zeros((CHUNK, 2 * HEAD_DIM), BF16)
                k_scr[0:CHUNK, :] = zero_blk
                v_scr[0:CHUNK, :] = zero_blk
                k_scr[(nc + 1) * CHUNK:(nc + 2) * CHUNK, :] = zero_blk
                v_scr[(nc + 1) * CHUNK:(nc + 2) * CHUNK, :] = zero_blk
            else:
                sb_scr[...] = jnp.zeros((GW, GW), F32)

        if latent:
            dst = pl.ds(pl.multiple_of((rc + 1) * CHUNK, CHUNK), CHUNK)
            k_scr[dst, :] = _rope(ka_ref[...], cos_ref[...], sin_ref[...]).astype(BF16)
            v_scr[dst, :] = va_ref[...].astype(BF16)

        dec = dec_ref[...]
        o_b = _retention_chunk(qc_ref[...], kc_ref[...], vc_ref[...], dmat_ref[1],
                               dec[1], dec[3], vec[6:7], sb_scr, masks, bd_mask)
        ob_scr[pl.ds(pl.multiple_of(rc * CHUNK, CHUNK), CHUNK), :] = o_b

        if not latent:
            @pl.when(c == nc - 1)
            def _():
                _store_state(st_ref, 1, sb_scr)

    @pl.when(p == 1)
    def _():
        @pl.when(c == 0)
        def _():
            if latent:
                _load_state(sf_scr, s0_ref, 0)
            else:
                sf_scr[...] = jnp.zeros((GW, GW), F32)
                kn_ref[...] = kx_ref[...]
                vn_ref[...] = vx_ref[...]

        q = qa_ref[...]
        if latent:
            q = _rope(q, cos_ref[...], sin_ref[...])
        q = q * (HEAD_DIM ** -0.5)
        lane = lax.broadcasted_iota(jnp.int32, (1, 2 * HEAD_DIM), 1)
        lo = lane < HEAD_DIM
        q0, q1 = q[:, :128], q[:, 128:]
        zero = jnp.zeros_like(q0)
        q_st = jnp.concatenate([
            jnp.where(lo, q0, zero),
            jnp.where(lo, pltpu.roll(q0, HEAD_DIM, axis=1), zero),
            jnp.where(lo, zero, pltpu.roll(q1, HEAD_DIM, axis=1)),
            jnp.where(lo, zero, q1)], axis=0).astype(BF16)
        if latent:
            band = pl.ds(pl.multiple_of(c * CHUNK, CHUNK), 3 * CHUNK)
            k_all = jnp.concatenate([k_scr[band, :], kx_ref[...].astype(BF16)], axis=0)
            v_all = jnp.concatenate([v_scr[band, :], vx_ref[...].astype(BF16)], axis=0)
        else:
            k_all = kx_ref[...].astype(BF16)
            v_all = vx_ref[...].astype(BF16)
        s = lax.dot_general(q_st, k_all, (((1,), (1,)), ((), ())), preferred_element_type=F32)
        if latent:
            nk = 3 * CHUNK + PAST_LEN
            qi = lax.broadcasted_iota(jnp.int32, (N_HEADS * CHUNK, nk), 0) & (CHUNK - 1)
            kj = lax.broadcasted_iota(jnp.int32, (N_HEADS * CHUNK, nk), 1)
            kpos = kj + (c - 1) * CHUNK
            valid = (kj >= 3 * CHUNK) | ((kj >= qi) & (kj <= qi + 2 * CHUNK)
                                         & (kpos >= 0) & (kpos < nc * CHUNK))
            s = jnp.where(valid, s, NEG_INF)
        sink = sink_ref[...][:, 0:1]
        mx = jnp.maximum(jnp.max(s, axis=-1, keepdims=True), sink)
        e = jnp.exp(s - mx)
        den = jnp.sum(e, axis=-1, keepdims=True) + jnp.exp(sink - mx)
        o = jnp.dot(e.astype(BF16), v_all, preferred_element_type=F32) / den
        mix_ref[:, 0:128] = jnp.where(lo, o[0:CHUNK], pltpu.roll(o[CHUNK:2 * CHUNK], HEAD_DIM, axis=1))
        mix_ref[:, 128:256] = jnp.where(lo, pltpu.roll(o[2 * CHUNK:3 * CHUNK], HEAD_DIM, axis=1),
                                        o[3 * CHUNK:4 * CHUNK])

        vn = _layer_norm(vb_ref[...], vec[0:1], vec[1:2]).astype(BF16)
        sg = jnp.dot(ws_ref[...], _stack_heads(vn, masks), preferred_element_type=F32) + bias_ref[...]
        mix_ref[:, GW:2 * GW] = ub_ref[...] * sg

        dec = dec_ref[...]
        o_f = _retention_chunk(qc_ref[...], kc_ref[...], vc_ref[...], dmat_ref[0],
                               dec[0], dec[2], vec[5:6], sf_scr, masks, bd_mask)
        o_b = ob_scr[pl.ds(pl.multiple_of(c * CHUNK, CHUNK), CHUNK), :]
        gmat = gmat_ref[...]
        mix_ref[:, 2 * GW:3 * GW] = (_silu(gf_ref[...]) * (_group_norm(o_f, gmat) * vec[3:4])
                                     + _silu(gb_ref[...]) * (_group_norm(o_b, gmat) * vec[4:5]))
        if not latent:
            @pl.when(c == nc - 1)
            def _():
                _store_state(st_ref, 0, sf_scr)

        pd = pd_ref[...]
        pext_scr[0:POOL_HALO, :] = jnp.where(c > 0, pdp_ref[...], 0.0)
        pext_scr[POOL_HALO:POOL_HALO + CHUNK, :] = pd
        pext_scr[POOL_HALO + CHUNK:2 * POOL_HALO + CHUNK, :] = jnp.where(c < nc - 1, pdn_ref[...], 0.0)

        def win(d, half):
            return pext_scr[pl.ds(POOL_HALO + d, CHUNK), half * 128:(half + 1) * 128]

        a2 = win(-1, 0) + win(0, 0)
        a4 = a2 + win(-2, 0) + win(1, 0)
        a8 = win(-4, 1)
        for d in range(-3, 4):
            a8 = a8 + win(d, 1)
        a16 = a8
        for d in list(range(-8, -4)) + list(range(4, 8)):
            a16 = a16 + win(d, 1)
        sums = jnp.concatenate([jnp.where(lo, a2, a4), jnp.where(lo, a8, a16)], axis=1)
        yd = sums * cnt_ref[...] - pd
        mix_ref[:, 3 * GW:4 * GW] = _bdot(yd, wpool_ref[...]) * vec[2:3]


def _mixer_old(z, tabs, l, latent, extra=None):
    nb = DEC_BATCH if latent else BATCH
    nc = (DEC_SEQ if latent else SEQ) // CHUNK
    base = (N_CTX // CHUNK) if latent else 0
    last_halo = ROWS // POOL_HALO - 1
    per8 = CHUNK // POOL_HALO

    def fwd(b, p, c):
        return base + b * nc + c * p

    def both(b, p, c):
        return base + b * nc + jnp.where(p == 0, nc - 1 - c, c)

    def col(width, idx, rowmap):
        return pl.BlockSpec((CHUNK, width), lambda b, p, c: (rowmap(b, p, c), idx))

    def const(shape):
        return pl.BlockSpec(shape, lambda b, p, c: (0,) * len(shape))

    def layer(shape):
        return pl.BlockSpec((None,) + shape, lambda b, p, c: (l,) + (0,) * len(shape))

    specs, args = [], []

    def add(spec, arr):
        specs.append(spec)
        args.append(arr)

    add(col(GW, 0, fwd), z)
    if latent:
        add(pl.BlockSpec((CHUNK, 128), lambda b, p, c: (base + b * nc + (nc - 1 - c) * (1 - p), 2)), z)
        add(pl.BlockSpec((CHUNK, 128), lambda b, p, c: (base + b * nc + (nc - 1 - c) * (1 - p), 3)), z)
        add(pl.BlockSpec((None, None, PAST_LEN, 128), lambda b, p, c: (b, l, 0, 0)), extra["ck"])
        add(pl.BlockSpec((None, None, PAST_LEN, 128), lambda b, p, c: (b, l, 0, 0)), extra["cv"])
    else:
        add(pl.BlockSpec((SEQ, 128), lambda b, p, c: (b, 2)), z)
        add(pl.BlockSpec((SEQ, 128), lambda b, p, c: (b, 3)), z)
    add(col(GW, 2, fwd), z)
    add(col(GW, 3, fwd), z)
    add(col(GW, 4, both), z)
    add(col(GW, 5, both), z)
    add(col(GW, 6, both), z)
    add(col(GW, 7, fwd), z)
    add(col(GW, 8, fwd), z)
    add(col(GW, 9, fwd), z)
    add(pl.BlockSpec((POOL_HALO, GW),
                     lambda b, p, c: (jnp.maximum(fwd(b, p, c) * per8 - 1, 0), 9)), z)
    add(pl.BlockSpec((POOL_HALO, GW),
                     lambda b, p, c: (jnp.minimum((fwd(b, p, c) + 1) * per8, last_halo), 9)), z)
    if latent:
        rope_map = lambda b, p, c: (jnp.where(p == 0, nc - 1 - c, c), 0)
        add(pl.BlockSpec((CHUNK, 128), rope_map), extra["cos"])
        add(pl.BlockSpec((CHUNK, 128), rope_map), extra["sin"])
        add(pl.BlockSpec((None, None, 2, N_HEADS, HEAD_DIM, HEAD_DIM),
                         lambda b, p, c: (b, l, 0, 0, 0, 0)), extra["s0"])
    add(layer((2, N_HEADS * CHUNK, CHUNK)), tabs["dmat"])
    add(layer((4, CHUNK, GW)), tabs["dec"])
    add(layer((N_HEADS * CHUNK, 128)), tabs["sink"])
    add(layer((CHUNK, N_HEADS * CHUNK)), tabs["ws"])
    add(layer((CHUNK, GW)), tabs["bias"])
    add(layer((8, GW)), tabs["vec"])
    add(const((GW, GW)), tabs["gmat"])
    add(pl.BlockSpec((CHUNK, GW), lambda b, p, c: (c * p, 0)), tabs["cnt_lat"] if latent else tabs["cnt_ctx"])
    add(layer((GW, GW)), tabs["wpool"])

    out_shape = [jax.ShapeDtypeStruct((nb * nc * CHUNK, D_MODEL), F32)]
    out_specs = [pl.BlockSpec((CHUNK, D_MODEL), lambda b, p, c: (b * nc + c * p, 0))]
    scratch = [pltpu.VMEM((GW, GW), F32), pltpu.VMEM((GW, GW), F32),
               pltpu.VMEM((nc * CHUNK, GW), F32),
               pltpu.VMEM((CHUNK + 2 * POOL_HALO, GW), F32)]
    if latent:
        scratch += [pltpu.VMEM(((nc + 2) * CHUNK, 128), BF16), pltpu.VMEM(((nc + 2) * CHUNK, 128), BF16)]
    else:
        out_shape.append(jax.ShapeDtypeStruct((nb, 2, N_HEADS, HEAD_DIM, HEAD_DIM), F32))
        out_specs.append(pl.BlockSpec((None, 2, N_HEADS, HEAD_DIM, HEAD_DIM), lambda b, p, c: (b, 0, 0, 0, 0)))
        for _ in range(2):
            out_shape.append(jax.ShapeDtypeStruct((nb, SEQ, 128), F32))
            out_specs.append(pl.BlockSpec((None, SEQ, 128), lambda b, p, c: (b, 0, 0)))

    return pl.pallas_call(
        functools.partial(_mixer_kernel, latent=latent, nc=nc),
        grid=(nb, 2, nc),
        in_specs=specs,
        out_specs=out_specs,
        out_shape=out_shape,
        scratch_shapes=scratch,
        compiler_params=pltpu.CompilerParams(
            dimension_semantics=("arbitrary", "arbitrary", "arbitrary"), vmem_limit_bytes=VMEM_LIMIT),
        name="mixer_latent" if latent else "mixer_context",
    )(*args)


LAT_GROUP = 4


def _chunk_off(c):
    return c * CHUNK if isinstance(c, int) else pl.multiple_of(c * CHUNK, CHUNK)


def _mixer_kernel(*refs, latent, group, ng):
    if latent:
        (qa_ref, ka_ref, va_ref, kx_ref, vx_ref, ub_ref, vb_ref, qc_ref, kc_ref, vc_ref, gf_ref, gb_ref,
         pd_ref, pdp_ref, pdn_ref, cos_ref, sin_ref, s0_ref, dmat_ref, dec_ref, sink_ref, ws_ref, bias_ref,
         vec_ref, gmat_ref, cnt_ref, wpool_ref, xres_ref, mod_ref, wout_ref, ln_ref,
         x1_ref, sf_scr, sb_scr, ob_scr, pext_scr, mix_ref, k_scr, v_scr) = refs
    else:
        (qa_ref, ka_ref, va_ref, ub_ref, vb_ref, qc_ref, kc_ref, vc_ref, gf_ref, gb_ref, pd_ref,
         dmat_ref, dec_ref, sink_ref, ws_ref, bias_ref, vec_ref, gmat_ref, cnt_ref, wpool_ref,
         xres_ref, mod_ref, wout_ref, ln_ref,
         x1_ref, st_ref, kn_ref, vn_ref, sf_scr, sb_scr, ob_scr, pext_scr, mix_ref) = refs

    nc = group * ng
    g_a = mod_ref[...][2:3]
    ln = ln_ref[...]
    masks = _head_masks(GW)
    row = lax.broadcasted_iota(jnp.int32, (GW, GW), 0)
    col = lax.broadcasted_iota(jnp.int32, (GW, GW), 1)
    bd_mask = (row // HEAD_DIM) == (col // HEAD_DIM)
    vec = vec_ref[...]
    lane = lax.broadcasted_iota(jnp.int32, (1, 2 * HEAD_DIM), 1)
    lo = lane < HEAD_DIM

    def rows(k):
        return slice(k * CHUNK, (k + 1) * CHUNK)

    def backward_chunk(k, c):
        if latent:
            dst = pl.ds(_chunk_off(c + 1), CHUNK)
            k_scr[dst, :] = _rope(ka_ref[rows(k), :], cos_ref[rows(k), :], sin_ref[rows(k), :]).astype(BF16)
            v_scr[dst, :] = va_ref[rows(k), :].astype(BF16)
        dec = dec_ref[...]
        o_b = _retention_chunk(qc_ref[rows(k), :], kc_ref[rows(k), :], vc_ref[rows(k), :], dmat_ref[1],
                               dec[1], dec[3], vec[6:7], sb_scr, masks, bd_mask)
        ob_scr[pl.ds(_chunk_off(c), CHUNK), :] = o_b

    def forward_chunk(k, c):
        q = qa_ref[rows(k), :]
        if latent:
            q = _rope(q, cos_ref[rows(k), :], sin_ref[rows(k), :])
        q = q * (HEAD_DIM ** -0.5)
        q0, q1 = q[:, :128], q[:, 128:]
        zero = jnp.zeros_like(q0)
        q_st = jnp.concatenate([
            jnp.where(lo, q0, zero),
            jnp.where(lo, pltpu.roll(q0, HEAD_DIM, axis=1), zero),
            jnp.where(lo, zero, pltpu.roll(q1, HEAD_DIM, axis=1)),
            jnp.where(lo, zero, q1)], axis=0).astype(BF16)
        if latent:
            band = pl.ds(_chunk_off(c), 3 * CHUNK)
            k_all = jnp.concatenate([k_scr[band, :], kx_ref[...].astype(BF16)], axis=0)
            v_all = jnp.concatenate([v_scr[band, :], vx_ref[...].astype(BF16)], axis=0)
        else:
            k_all = ka_ref[...].astype(BF16)
            v_all = va_ref[...].astype(BF16)
        s = lax.dot_general(q_st, k_all, (((1,), (1,)), ((), ())), preferred_element_type=F32)
        if latent:
            nk = 3 * CHUNK + PAST_LEN
            qi = lax.broadcasted_iota(jnp.int32, (N_HEADS * CHUNK, nk), 0) & (CHUNK - 1)
            kj = lax.broadcasted_iota(jnp.int32, (N_HEADS * CHUNK, nk), 1)
            kpos = kj + (c - 1) * CHUNK
            valid = (kj >= 3 * CHUNK) | ((kj >= qi) & (kj <= qi + 2 * CHUNK)
                                         & (kpos >= 0) & (kpos < nc * CHUNK))
            s = jnp.where(valid, s, NEG_INF)
        sink = sink_ref[...][:, 0:1]
        mx = jnp.maximum(jnp.max(s, axis=-1, keepdims=True), sink)
        e = jnp.exp(s - mx)
        den = jnp.sum(e, axis=-1, keepdims=True) + jnp.exp(sink - mx)
        o = jnp.dot(e.astype(BF16), v_all, preferred_element_type=F32) / den
        mix_ref[rows(k), 0:128] = jnp.where(lo, o[0:CHUNK], pltpu.roll(o[CHUNK:2 * CHUNK], HEAD_DIM, axis=1))
        mix_ref[rows(k), 128:256] = jnp.where(lo, pltpu.roll(o[2 * CHUNK:3 * CHUNK], HEAD_DIM, axis=1),
                                              o[3 * CHUNK:4 * CHUNK])

        vn = _layer_norm(vb_ref[rows(k), :], vec[0:1], vec[1:2]).astype(BF16)
        sg = jnp.dot(ws_ref[...], _stack_heads(vn, masks), preferred_element_type=F32) + bias_ref[...]
        mix_ref[rows(k), GW:2 * GW] = ub_ref[rows(k), :] * sg

        dec = dec_ref[...]
        o_f = _retention_chunk(qc_ref[rows(k), :], kc_ref[rows(k), :], vc_ref[rows(k), :], dmat_ref[0],
                               dec[0], dec[2], vec[5:6], sf_scr, masks, bd_mask)
        o_b = ob_scr[pl.ds(_chunk_off(c), CHUNK), :]
        gmat = gmat_ref[...]
        mix_ref[rows(k), 2 * GW:3 * GW] = (
            _silu(gf_ref[rows(k), :]) * (_group_norm(o_f, gmat) * vec[3:4])
            + _silu(gb_ref[rows(k), :]) * (_group_norm(o_b, gmat) * vec[4:5]))

        pd = pd_ref[rows(k), :]
        pext = pext_scr.at[k]
        zeros8 = jnp.zeros((POOL_HALO, GW), F32)
        if k > 0:
            prev8 = pd_ref[k * CHUNK - POOL_HALO:k * CHUNK, :]
        else:
            prev8 = jnp.where(c > 0, pdp_ref[...], 0.0) if latent else zeros8
        if k < group - 1:
            next8 = pd_ref[(k + 1) * CHUNK:(k + 1) * CHUNK + POOL_HALO, :]
        else:
            next8 = jnp.where(c < nc - 1, pdn_ref[...], 0.0) if latent else zeros8
        pext[0:POOL_HALO, :] = prev8
        pext[POOL_HALO:POOL_HALO + CHUNK, :] = pd
        pext[POOL_HALO + CHUNK:2 * POOL_HALO + CHUNK, :] = next8

        def win(d, half):
            return pext[pl.ds(POOL_HALO + d, CHUNK), half * 128:(half + 1) * 128]

        a2 = win(-1, 0) + win(0, 0)
        a4 = a2 + win(-2, 0) + win(1, 0)
        a8 = win(-4, 1)
        for d in range(-3, 4):
            a8 = a8 + win(d, 1)
        a16 = a8
        for d in list(range(-8, -4)) + list(range(4, 8)):
            a16 = a16 + win(d, 1)
        sums = jnp.concatenate([jnp.where(lo, a2, a4), jnp.where(lo, a8, a16)], axis=1)
        yd = sums * cnt_ref[rows(k), :] - pd
        mix_ref[rows(k), 3 * GW:4 * GW] = _bdot(yd, wpool_ref[...]) * vec[2:3]

        y = jnp.dot(mix_ref[rows(k), :].astype(BF16), wout_ref[...], preferred_element_type=F32)
        x1_ref[rows(k), :] = _layer_norm(ALPHA * xres_ref[rows(k), :] + g_a * y, ln[0:1], ln[2:3])

    if not latent:
        sb_scr[...] = jnp.zeros((GW, GW), F32)
        for k in reversed(range(group)):
            backward_chunk(k, k)
        _store_state(st_ref, 1, sb_scr)
        sf_scr[...] = jnp.zeros((GW, GW), F32)
        kn_ref[...] = ka_ref[...]
        vn_ref[...] = va_ref[...]
        for k in range(group):
            forward_chunk(k, k)
        _store_state(st_ref, 0, sf_scr)
        return

    p = pl.program_id(1)
    g = pl.program_id(2)

    @pl.when(p == 0)
    def _():
        @pl.when(g == 0)
        def _():
            _load_state(sb_scr, s0_ref, 1)
            zero_blk = jnp.zeros((CHUNK, 2 * HEAD_DIM), BF16)
            for scr in (k_scr, v_scr):
                scr[0:CHUNK, :] = zero_blk
                scr[(nc + 1) * CHUNK:(nc + 2) * CHUNK, :] = zero_blk

        for k in reversed(range(group)):
            backward_chunk(k, (ng - 1 - g) * group + k)

    @pl.when(p == 1)
    def _():
        @pl.when(g == 0)
        def _():
            _load_state(sf_scr, s0_ref, 0)

        for k in range(group):
            forward_chunk(k, g * group + k)


def _mixer(z, x_res, mod, w_out_bf16, ln8, tabs, l, latent, extra=None):
    nb = DEC_BATCH if latent else BATCH
    nc = (DEC_SEQ if latent else SEQ) // CHUNK
    group = LAT_GROUP if latent else nc
    ng = nc // group
    blk = group * CHUNK
    base = (N_CTX // blk) if latent else 0
    per8 = blk // POOL_HALO
    last_halo = ROWS // POOL_HALO - 1

    def on_grid(f):
        return (lambda b, p, g: f(b, p, g)) if latent else (lambda b: f(b, 1, 0))

    def fwd(b, p, g):
        return base + b * ng + g * p

    def both(b, p, g):
        return base + b * ng + jnp.where(p == 0, ng - 1 - g, g)

    def bwd_only(b, p, g):
        return base + b * ng + (ng - 1 - g) * (1 - p)

    def col(width, idx, rowmap):
        return pl.BlockSpec((blk, width), on_grid(lambda b, p, g: (rowmap(b, p, g), idx)))

    def const(shape):
        return pl.BlockSpec(shape, on_grid(lambda b, p, g: (0,) * len(shape)))

    def layer(shape):
        return pl.BlockSpec((None,) + shape, on_grid(lambda b, p, g: (l,) + (0,) * len(shape)))

    specs, args = [], []

    def add(spec, arr):
        specs.append(spec)
        args.append(arr)

    add(col(GW, 0, fwd), z)
    add(col(128, 2, bwd_only if latent else fwd), z)
    add(col(128, 3, bwd_only if latent else fwd), z)
    if latent:
        add(pl.BlockSpec((None, None, PAST_LEN, 128), lambda b, p, g: (b, l, 0, 0)), extra["ck"])
        add(pl.BlockSpec((None, None, PAST_LEN, 128), lambda b, p, g: (b, l, 0, 0)), extra["cv"])
    add(col(GW, 2, fwd), z)
    add(col(GW, 3, fwd), z)
    add(col(GW, 4, both), z)
    add(col(GW, 5, both), z)
    add(col(GW, 6, both), z)
    add(col(GW, 7, fwd), z)
    add(col(GW, 8, fwd), z)
    add(col(GW, 9, fwd), z)
    if latent:
        add(pl.BlockSpec((POOL_HALO, GW),
                         lambda b, p, g: (jnp.maximum(fwd(b, p, g) * per8 - 1, 0), 9)), z)
        add(pl.BlockSpec((POOL_HALO, GW),
                         lambda b, p, g: (jnp.minimum((fwd(b, p, g) + 1) * per8, last_halo), 9)), z)
        rope_map = lambda b, p, g: (jnp.where(p == 0, ng - 1 - g, g), 0)
        add(pl.BlockSpec((blk, 128), rope_map), extra["cos"])
        add(pl.BlockSpec((blk, 128), rope_map), extra["sin"])
        add(pl.BlockSpec((None, None, 2, N_HEADS, HEAD_DIM, HEAD_DIM),
                         lambda b, p, g: (b, l, 0, 0, 0, 0)), extra["s0"])
    add(layer((2, N_HEADS * CHUNK, CHUNK)), tabs["dmat"])
    add(layer((4, CHUNK, GW)), tabs["dec"])
    add(layer((N_HEADS * CHUNK, 128)), tabs["sink"])
    add(layer((CHUNK, N_HEADS * CHUNK)), tabs["ws"])
    add(layer((CHUNK, GW)), tabs["bias"])
    add(layer((8, GW)), tabs["vec"])
    add(const((GW, GW)), tabs["gmat"])
    add(pl.BlockSpec((blk, GW), on_grid(lambda b, p, g: (g * p, 0))),
        tabs["cnt_lat"] if latent else tabs["cnt_ctx"])
    add(layer((GW, GW)), tabs["wpool"])
    local = lambda b, p, g: (b * ng + g * p, 0)
    add(pl.BlockSpec((blk, D_MODEL), on_grid(local)), x_res)
    add(pl.BlockSpec((None, None, 6, D_MODEL),
                     on_grid(lambda b, p, g: (l, (1 + b) if latent else 0, 0, 0))), mod)
    add(layer((D_MODEL, D_MODEL)), w_out_bf16)
    add(layer((8, D_MODEL)), ln8)

    out_shape = [jax.ShapeDtypeStruct((nb * nc * CHUNK, D_MODEL), F32)]
    out_specs = [pl.BlockSpec((blk, D_MODEL), on_grid(local))]
    scratch = [pltpu.VMEM((GW, GW), F32), pltpu.VMEM((GW, GW), F32),
               pltpu.VMEM((nc * CHUNK, GW), F32),
               pltpu.VMEM((group, CHUNK + 2 * POOL_HALO, GW), F32),
               pltpu.VMEM((blk, D_MODEL), F32)]
    if latent:
        scratch += [pltpu.VMEM(((nc + 2) * CHUNK, 128), BF16), pltpu.VMEM(((nc + 2) * CHUNK, 128), BF16)]
    else:
        out_shape.append(jax.ShapeDtypeStruct((nb, 2, N_HEADS, HEAD_DIM, HEAD_DIM), F32))
        out_specs.append(pl.BlockSpec((None, 2, N_HEADS, HEAD_DIM, HEAD_DIM), lambda b: (b, 0, 0, 0, 0)))
        for _ in range(2):
            out_shape.append(jax.ShapeDtypeStruct((nb, SEQ, 128), F32))
            out_specs.append(pl.BlockSpec((None, SEQ, 128), lambda b: (b, 0, 0)))

    return pl.pallas_call(
        functools.partial(_mixer_kernel, latent=latent, group=group, ng=ng),
        grid=(nb, 2, ng) if latent else (nb,),
        in_specs=specs,
        out_specs=out_specs,
        out_shape=out_shape,
        scratch_shapes=scratch,
        compiler_params=pltpu.CompilerParams(
            dimension_semantics=("arbitrary",) * (3 if latent else 1), vmem_limit_bytes=VMEM_LIMIT),
        name="mixer_latent" if latent else "mixer_context",
    )(*args)


def _pad_rows(rows, n=8):
    a = jnp.stack(rows)
    return jnp.concatenate([a, jnp.zeros((n - a.shape[0],) + a.shape[1:], a.dtype)], axis=0)


def _block_diag(blocks):
    g, n, _ = blocks.shape
    eye = jnp.eye(g, dtype=blocks.dtype)
    return (eye[:, None, :, None] * blocks[:, :, None, :]).reshape(g * n, g * n)


def _inv_count(n):
    t = np.arange(n)
    cols = []
    for w in POOL_WINDOWS:
        cnt = np.clip(t + w // 2, 0, n) - np.clip(t - w // 2, 0, n)
        cols.append(np.repeat((1.0 / cnt)[:, None], HEAD_DIM, axis=1))
    return jnp.asarray(np.concatenate(cols, axis=1), F32)


def _rope_tables():
    rows = DEC_SEQ // GRID_W
    r, cc = jnp.meshgrid(jnp.arange(rows), jnp.arange(GRID_W), indexing="ij")
    half = HEAD_DIM // 2
    freqs = ROPE_BASE ** (-jnp.arange(0, half, 2, dtype=F32) / half)

    def tables(pos):
        ang = pos.reshape(-1).astype(F32)[:, None] * freqs[None, :]
        cos, sin = jnp.cos(ang), jnp.sin(ang)
        return jnp.concatenate([cos, cos], axis=1), jnp.concatenate([-sin, sin], axis=1)

    cr, sr = tables(r)
    ccol, scol = tables(cc)
    cos = jnp.concatenate([cr, ccol], axis=1)
    sin = jnp.concatenate([sr, scol], axis=1)
    return jnp.tile(cos, (1, 2)), jnp.tile(sin, (1, 2))


def _layer_tables(attn_sink, sgu_norm_w, sgu_norm_b, sgu_ws, sgu_bs, ret_decay, ret_gn_w, pool_w, pool_scale):
    log_g = jax.nn.log_sigmoid(ret_decay.astype(F32))
    i = jnp.arange(CHUNK, dtype=F32)
    rel = i[:, None] - i[None, :]
    kscale = HEAD_DIM ** -0.5
    d_f = jnp.where(rel >= 0, jnp.exp(jnp.maximum(rel, 0.0)[None] * log_g[0][:, None, None]), 0.0)
    d_b = jnp.where(rel <= 0, jnp.exp(jnp.maximum(-rel, 0.0)[None] * log_g[1][:, None, None]), 0.0)
    dmat = jnp.stack([d_f.reshape(N_HEADS * CHUNK, CHUNK), d_b.reshape(N_HEADS * CHUNK, CHUNK)]) * kscale

    def lanes(per_head):
        return jnp.repeat(per_head, HEAD_DIM, axis=1)

    qdec_f = lanes(jnp.exp((i + 1.0)[:, None] * log_g[0][None, :]))
    qdec_b = lanes(jnp.exp((CHUNK - i)[:, None] * log_g[1][None, :]))
    kdec_f = lanes(jnp.exp((CHUNK - 1.0 - i)[:, None] * log_g[0][None, :])) * kscale
    kdec_b = lanes(jnp.exp(i[:, None] * log_g[1][None, :])) * kscale
    cdec = jnp.repeat(jnp.exp(CHUNK * log_g), HEAD_DIM, axis=1)
    vec = _pad_rows([sgu_norm_w, sgu_norm_b, pool_scale, ret_gn_w[0], ret_gn_w[1], cdec[0], cdec[1]])
    return {
        "dmat": dmat,
        "dec": jnp.stack([qdec_f, qdec_b, kdec_f, kdec_b]),
        "sink": jnp.broadcast_to(jnp.repeat(attn_sink, CHUNK)[:, None], (N_HEADS * CHUNK, 128)),
        "ws": jnp.concatenate([sgu_ws[h] for h in range(N_HEADS)], axis=1).astype(BF16),
        "bias": jnp.repeat(sgu_bs.T, HEAD_DIM, axis=1),
        "vec": vec,
        "wpool": _block_diag(pool_w).astype(BF16),
    }


def kernel(x_prompt, x_sample, cache_attn_k, cache_attn_v, state_ret, c, c_ctx, w_ada, b_ada, w_in,
           w_out, attn_sink, sgu_norm_w, sgu_norm_b, sgu_ws, sgu_bs, ret_decay, ret_gn_w, pool_w,
           pool_scale, ffn_up, ffn_conv_w, ffn_conv_b, ffn_down, ln_w, ln_b):
    cond8 = jnp.concatenate([c_ctx[None], c, jnp.zeros((8 - 1 - DEC_BATCH, D_MODEL), F32)], axis=0)
    mod = _modulation(cond8, w_ada, b_ada).reshape(DEPTH, 8, 6, D_MODEL)

    tabs = jax.vmap(_layer_tables)(attn_sink, sgu_norm_w, sgu_norm_b, sgu_ws, sgu_bs, ret_decay, ret_gn_w,
                                   pool_w, pool_scale)
    tabs["gmat"] = _block_diag(jnp.full((N_HEADS, HEAD_DIM, HEAD_DIM), 1.0 / HEAD_DIM, F32)).astype(BF16)
    tabs["cnt_ctx"] = _inv_count(SEQ)
    tabs["cnt_lat"] = _inv_count(DEC_SEQ)
    cos, sin = _rope_tables()
    extra = {"ck": cache_attn_k.reshape(DEC_BATCH, DEPTH, PAST_LEN, 128),
             "cv": cache_attn_v.reshape(DEC_BATCH, DEPTH, PAST_LEN, 128),
             "cos": cos, "sin": sin, "s0": state_ret}
    ln8 = jnp.concatenate([ln_w, ln_b, jnp.zeros((DEPTH, 4, D_MODEL), F32)], axis=1)
    conv8 = jnp.concatenate([ffn_conv_w, ffn_conv_b[:, None], jnp.zeros((DEPTH, 4, 2 * D_FF), F32)], axis=1)

    down_bf16 = _to_bf16(ffn_down, D_FF // 4)
    w_in_bf16 = _to_bf16(w_in, D_MODEL // 2)
    w_out_bf16 = _to_bf16(w_out, D_MODEL)
    up_chunks = _up_weight_chunks(ffn_up)

    xs = [x_prompt.reshape(N_CTX, D_MODEL), x_sample.reshape(N_LAT, D_MODEL)]
    new_k, new_v, new_s = [], [], []
    for l in range(DEPTH):
        z = _inproj(xs, mod, w_in_bf16, l)
        x1_ctx, st, kn, vn = _mixer(z, xs[0], mod, w_out_bf16, ln8, tabs, l, latent=False)
        (x1_lat,) = _mixer(z, xs[1], mod, w_out_bf16, ln8, tabs, l, latent=True, extra=extra)
        xs = _ffn([x1_ctx, x1_lat], mod, up_chunks, conv8, down_bf16, ln8, l, split_out=True)
        new_k.append(kn.reshape(BATCH, SEQ, 2, HEAD_DIM))
        new_v.append(vn.reshape(BATCH, SEQ, 2, HEAD_DIM))
        new_s.append(st)

    y_prompt = xs[0].reshape(BATCH, SEQ, D_MODEL)
    y_sample = xs[1].reshape(DEC_BATCH, DEC_SEQ, D_MODEL)
    return (y_prompt, y_sample, jnp.stack(new_k, axis=1), jnp.stack(new_v, axis=1),
            jnp.stack(new_s, axis=1))
```

```python
import functools

import numpy as np
import jax
import jax.numpy as jnp
from jax import lax
from jax.experimental import pallas as pl
from jax.experimental.pallas import tpu as pltpu

F32 = jnp.float32
BF16 = jnp.bfloat16

D_MODEL = 1024
BATCH = 16
SEQ = 256
DEPTH = 2
DEC_BATCH = 2
DEC_SEQ = 2048
PAST_LEN = 256
GRID_W = 64
CHUNK = 128
HEAD_DIM = 64
GW = D_MODEL // 4
N_HEADS = 4
POOL_WINDOWS = (2, 4, 8, 16)
POOL_HALO = 8
D_FF = 2816
ROPE_BASE = 10000.0
LN_EPS = 1e-5
NEG_INF = -1e30
IN_WIDTH = 10 * GW
ALPHA = (2.0 * DEPTH) ** 0.25

N_CTX = BATCH * SEQ
N_LAT = DEC_BATCH * DEC_SEQ
ROWS = N_CTX + N_LAT

TM = 1024
CTX_TILES = N_CTX // TM
LAT_TILES_PER_SEQ = DEC_SEQ // TM
NB_IN = 512
FC = 256
NB_ADA = 1536
VMEM_LIMIT = 56 * 1024 * 1024


def _cond_of_tile(i, tm=TM):
    ctx_tiles = N_CTX // tm
    return jnp.where(i < ctx_tiles, 0, 1 + (i - ctx_tiles) // (DEC_SEQ // tm))


def _tile_specs(n_src, tm, grid_rank):
    ctx_tiles = N_CTX // tm
    if n_src == 1:
        rows = [lambda i: i]
    else:
        rows = [lambda i: jnp.minimum(i, ctx_tiles - 1), lambda i: jnp.maximum(i - ctx_tiles, 0)]
    if grid_rank == 1:
        return [pl.BlockSpec((tm, D_MODEL), lambda i, f=f: (f(i), 0)) for f in rows]
    return [pl.BlockSpec((tm, D_MODEL), lambda i, j, f=f: (f(i), 0)) for f in rows]


def _per_half(i, tm, n_max, fn):
    if n_max == 1:
        fn(0)
        return
    ctx_tiles = N_CTX // tm

    @pl.when(i < ctx_tiles)
    def _():
        fn(0)

    @pl.when(i >= ctx_tiles)
    def _():
        fn(1)


def _layer_norm(x, w, b):
    mu = jnp.mean(x, axis=-1, keepdims=True)
    d = x - mu
    var = jnp.mean(d * d, axis=-1, keepdims=True)
    return d * lax.rsqrt(var + LN_EPS) * w + b


def _silu(x):
    return x * jax.nn.sigmoid(x)


def _bdot(a, b):
    return jnp.dot(a.astype(BF16), b.astype(BF16), preferred_element_type=F32)


def _mod_kernel(c_ref, w_ref, b_ref, o_ref):
    o_ref[...] = _bdot(_silu(c_ref[...]), w_ref[...]) + b_ref[...]


def _modulation(cond8, w_ada, b_ada):
    return pl.pallas_call(
        _mod_kernel,
        grid=(DEPTH, 6 * D_MODEL // NB_ADA),
        in_specs=[
            pl.BlockSpec((8, D_MODEL), lambda l, j: (0, 0)),
            pl.BlockSpec((None, D_MODEL, NB_ADA), lambda l, j: (l, 0, j)),
            pl.BlockSpec((None, 1, NB_ADA), lambda l, j: (l, 0, j)),
        ],
        out_specs=pl.BlockSpec((None, 8, NB_ADA), lambda l, j: (l, 0, j)),
        out_shape=jax.ShapeDtypeStruct((DEPTH, 8, 6 * D_MODEL), F32),
        compiler_params=pltpu.CompilerParams(
            dimension_semantics=("arbitrary", "arbitrary"), vmem_limit_bytes=VMEM_LIMIT),
        name="modulation",
    )(cond8, w_ada, b_ada.reshape(DEPTH, 1, 6 * D_MODEL))


def _cast_kernel(w_ref, o_ref):
    o_ref[...] = w_ref[...].astype(BF16)


def _to_bf16(w, block_rows):
    depth, rows, cols = w.shape
    return pl.pallas_call(
        _cast_kernel,
        grid=(depth, rows // block_rows),
        in_specs=[pl.BlockSpec((None, block_rows, cols), lambda l, r: (l, r, 0))],
        out_specs=pl.BlockSpec((None, block_rows, cols), lambda l, r: (l, r, 0)),
        out_shape=jax.ShapeDtypeStruct(w.shape, BF16),
        compiler_params=pltpu.CompilerParams(dimension_semantics=("arbitrary", "arbitrary")),
        name="cast_bf16",
    )(w)


def _to_bf16_chunks(w, chunk):
    depth, rows, cols = w.shape
    return pl.pallas_call(
        _cast_kernel,
        grid=(depth, cols // chunk),
        in_specs=[pl.BlockSpec((None, rows, chunk), lambda l, c: (l, 0, c))],
        out_specs=pl.BlockSpec((None, None, rows, chunk), lambda l, c: (l, c, 0, 0)),
        out_shape=jax.ShapeDtypeStruct((depth, cols // chunk, rows, chunk), BF16),
        compiler_params=pltpu.CompilerParams(dimension_semantics=("arbitrary", "arbitrary")),
        name="cast_bf16_chunks",
    )(w)


def _inproj_kernel(*refs, n_x):
    x_refs = refs[:n_x]
    mod_ref, w_ref, z_ref, h_scr = refs[n_x:]

    m = mod_ref[...]

    def build(side):
        h_scr[...] = (x_refs[side][...] * (1.0 + m[1:2]) + m[0:1]).astype(BF16)

    _per_half(pl.program_id(0), TM, n_x, build)
    h = h_scr[...]
    for jb in range(IN_WIDTH // NB_IN):
        cols = slice(jb * NB_IN, (jb + 1) * NB_IN)
        z_ref[:, cols] = jnp.dot(h, w_ref[:, cols], preferred_element_type=F32)


def _inproj(xs, mod, w_in_bf16, l):
    return pl.pallas_call(
        functools.partial(_inproj_kernel, n_x=len(xs)),
        grid=(ROWS // TM,),
        in_specs=_tile_specs(len(xs), TM, 1) + [
            pl.BlockSpec((None, None, 6, D_MODEL), lambda i: (l, _cond_of_tile(i), 0, 0)),
            pl.BlockSpec((None, D_MODEL, IN_WIDTH), lambda i: (l, 0, 0), pipeline_mode=pl.Buffered(1)),
        ],
        out_specs=pl.BlockSpec((TM, IN_WIDTH), lambda i: (i, 0)),
        out_shape=jax.ShapeDtypeStruct((ROWS, IN_WIDTH), F32),
        scratch_shapes=[pltpu.VMEM((TM, D_MODEL), BF16)],
        compiler_params=pltpu.CompilerParams(
            dimension_semantics=("arbitrary",), vmem_limit_bytes=VMEM_LIMIT),
        name="inproj",
    )(*xs, mod, w_in_bf16)


TM_OUT = 512


def _outproj_kernel(*refs, n_x):
    mix_refs = refs[:2]
    x_refs = refs[2:2 + n_x]
    mod_ref, w_ref, ln_ref, o_ref = refs[2 + n_x:]
    m = mod_ref[...]
    ln = ln_ref[...]

    def body(side):
        y = jnp.dot(mix_refs[side][...].astype(BF16), w_ref[...], preferred_element_type=F32)
        x = x_refs[min(side, n_x - 1)][...]
        o_ref[...] = _layer_norm(ALPHA * x + m[2:3] * y, ln[0:1], ln[2:3])

    _per_half(pl.program_id(0), TM_OUT, 2, body)


def _outproj(mixes, xs, mod, w_out, ln8, l):
    return pl.pallas_call(
        functools.partial(_outproj_kernel, n_x=len(xs)),
        grid=(ROWS // TM_OUT,),
        in_specs=_tile_specs(2, TM_OUT, 1) + _tile_specs(len(xs), TM_OUT, 1) + [
            pl.BlockSpec((None, None, 6, D_MODEL), lambda i: (l, _cond_of_tile(i, TM_OUT), 0, 0)),
            pl.BlockSpec((None, D_MODEL, D_MODEL), lambda i: (l, 0, 0)),
            pl.BlockSpec((None, 8, D_MODEL), lambda i: (l, 0, 0)),
        ],
        out_specs=pl.BlockSpec((TM_OUT, D_MODEL), lambda i: (i, 0)),
        out_shape=jax.ShapeDtypeStruct((ROWS, D_MODEL), F32),
        compiler_params=pltpu.CompilerParams(
            dimension_semantics=("arbitrary",), vmem_limit_bytes=VMEM_LIMIT),
        name="outproj",
    )(*mixes, *xs, mod, w_out, ln8)


SEG = TM // 8
HALO_ROWS = 16
RB = 256
RBU = 512
GB = 64
NFC = D_FF // FC
LANE_BLOCKS = D_MODEL // 128
SEG_PITCH = SEG + 8


def _seg_rows(xc_ref, k):
    return jnp.concatenate([xc_ref[cb, pl.ds(k, 8, stride=SEG_PITCH), :] for cb in range(LANE_BLOCKS)], axis=1)


def _ffn_kernel_old(*refs, n_out):
    (x_ref, xp_ref, xn_ref, mod_ref, upa_ref, upg_ref, cva_ref, cvg_ref, dn_ref, ln_ref) = refs[:10]
    o_refs = refs[10:10 + n_out]
    h_scr, act_scr, xc_scr, u0_scr, u1_scr, wbf_scr = refs[10 + n_out:]
    u_scrs = (u0_scr, u1_scr)
    i = pl.program_id(0)
    j = pl.program_id(1)
    is_ctx = i < CTX_TILES
    lat_pos = (i - CTX_TILES) % LAT_TILES_PER_SEQ

    def build_h():
        m = mod_ref[...]
        scale = 1.0 + m[4:5]
        shift = m[3:4]
        for cb in range(LANE_BLOCKS):
            for s in range(8):
                xc_scr[cb, s * SEG_PITCH:s * SEG_PITCH + SEG, :] = x_ref[s * SEG:(s + 1) * SEG,
                                                                         cb * 128:(cb + 1) * 128]
        for k in range(0, SEG, 2):
            rows = jnp.concatenate([_seg_rows(xc_scr, k), _seg_rows(xc_scr, k + 1)], axis=0)
            h_scr[8 * k:8 * k + 16, :] = (rows * scale + shift).astype(BF16)
        sub = lax.broadcasted_iota(jnp.int32, (HALO_ROWS, D_MODEL), 0)
        prev_ok = jnp.logical_not(is_ctx) & (lat_pos > 0)
        next_ok = jnp.logical_not(is_ctx) & (lat_pos < LAT_TILES_PER_SEQ - 1)
        halo_x = jnp.where(sub == 0, xp_ref[POOL_HALO - 1:POOL_HALO, :], xn_ref[0:1, :])
        keep = ((sub == 0) & prev_ok) | ((sub == 1) & next_ok)
        h_scr[TM:TM + HALO_ROWS, :] = jnp.where(keep, halo_x * scale + shift, 0.0).astype(BF16)

    n_ub = TM // RBU

    def cast_up_weights():
        wbf_scr[:, 0:FC] = upa_ref[...].astype(BF16)
        wbf_scr[:, FC:2 * FC] = upg_ref[...].astype(BF16)

    def up_proj(slot, ub):
        rows = slice(ub * RBU, (ub + 1) * RBU + (HALO_ROWS if ub == n_ub - 1 else 0))
        u_scrs[slot][rows, :] = jnp.dot(h_scr[rows, :], wbf_scr[...], preferred_element_type=F32)

    def conv(u_ref, lanes, cv, r0):
        lo = max(r0 - 8, 0)
        hi = min(r0 + GB + 8, TM)
        ue = u_ref[lo:hi, lanes]
        u = ue[r0 - lo:r0 - lo + GB]
        if r0 == 0 or r0 == TM - GB:
            sub = lax.broadcasted_iota(jnp.int32, (8, FC), 0)
            seg_per_seq = SEQ // SEG
        if r0 == 0:
            ctx_first = is_ctx & (sub % seg_per_seq == 0)
            b_first = jnp.where(sub == 0, u_ref[TM:TM + 1, lanes],
                                pltpu.roll(u_ref[TM - 8:TM, lanes], 1, axis=0))
            um1 = jnp.concatenate([jnp.where(ctx_first, 0.0, b_first), u[0:GB - 8]], axis=0)
        else:
            um1 = ue[0:GB]
        if r0 == TM - GB:
            ctx_last = is_ctx & (sub % seg_per_seq == seg_per_seq - 1)
            b_last = jnp.where(sub == 7, u_ref[TM + 1:TM + 2, lanes],
                               pltpu.roll(u_ref[0:8, lanes], 7, axis=0))
            up1 = jnp.concatenate([u[8:GB], jnp.where(ctx_last, 0.0, b_last)], axis=0)
        else:
            up1 = ue[r0 - lo + 8:r0 - lo + GB + 8]
        return um1 * cv[0:1] + u * cv[1:2] + up1 * cv[2:3] + cv[3:4]

    def gate(slot, chunk, ub):
        for r0 in range(ub * RBU, (ub + 1) * RBU, GB):
            a = conv(u_scrs[slot], slice(0, FC), cva_ref[...], r0)
            g = conv(u_scrs[slot], slice(FC, 2 * FC), cvg_ref[...], r0)
            act_scr[chunk, r0:r0 + GB, :] = (_silu(a) * g).astype(BF16)

    @pl.when(j == 0)
    def _():
        build_h()
        cast_up_weights()
        for ub in range(n_ub):
            up_proj(0, ub)

    for slot in range(2):
        @pl.when((j >= 1) & (j < NFC) & (j % 2 == slot))
        def _(slot=slot):
            cast_up_weights()
            for ub in range(n_ub):
                up_proj(slot, ub)
                gate(1 - slot, j - 1, ub)

    @pl.when(j == NFC)
    def _():
        for ub in range(n_ub):
            gate((NFC - 1) % 2, NFC - 1, ub)
        m = mod_ref[...]
        ln = ln_ref[...]
        for rb in range(TM // RB):
            vrows = range(rb * RB // 8, (rb + 1) * RB // 8)
            lhs = jnp.concatenate([act_scr[jj, rb * RB:(rb + 1) * RB, :] for jj in range(NFC)], axis=1)
            y = jnp.dot(lhs, dn_ref[...], preferred_element_type=F32)
            xr = jnp.concatenate([_seg_rows(xc_scr, k) for k in vrows], axis=0)
            out = _layer_norm(ALPHA * xr + m[5:6] * y, ln[1:2], ln[3:4])
            for kk, k in enumerate(vrows):
                for cb in range(LANE_BLOCKS):
                    xc_scr[cb, pl.ds(k, 8, stride=SEG_PITCH), :] = out[8 * kk:8 * kk + 8,
                                                                       cb * 128:(cb + 1) * 128]
        def write_out(side):
            for cb in range(LANE_BLOCKS):
                for s in range(8):
                    o_refs[side][s * SEG:(s + 1) * SEG, cb * 128:(cb + 1) * 128] = xc_scr[
                        cb, s * SEG_PITCH:s * SEG_PITCH + SEG, :]

        _per_half(i, TM, n_out, write_out)


def _ffn_old(x, mod, up, conv8, down, ln8, l, split_out):
    halo_blocks = TM // POOL_HALO
    last_halo = ROWS // POOL_HALO - 1

    def up_chunk(j):
        return jnp.minimum(j, NFC - 1)

    def gate_chunk(j):
        return jnp.maximum(j - 1, 0)

    if split_out:
        out_specs = _tile_specs(2, TM, 2)
        out_shape = [jax.ShapeDtypeStruct((N_CTX, D_MODEL), F32), jax.ShapeDtypeStruct((N_LAT, D_MODEL), F32)]
    else:
        out_specs = _tile_specs(1, TM, 2)
        out_shape = [jax.ShapeDtypeStruct((ROWS, D_MODEL), F32)]
    return pl.pallas_call(
        functools.partial(_ffn_kernel, n_out=len(out_shape)),
        grid=(ROWS // TM, NFC + 1),
        in_specs=[
            pl.BlockSpec((TM, D_MODEL), lambda i, j: (i, 0)),
            pl.BlockSpec((POOL_HALO, D_MODEL), lambda i, j: (jnp.maximum(i * halo_blocks - 1, 0), 0)),
            pl.BlockSpec((POOL_HALO, D_MODEL),
                         lambda i, j: (jnp.minimum((i + 1) * halo_blocks, last_halo), 0)),
            pl.BlockSpec((None, None, 6, D_MODEL), lambda i, j: (l, _cond_of_tile(i), 0, 0)),
            pl.BlockSpec((None, D_MODEL, FC), lambda i, j: (l, 0, up_chunk(j))),
            pl.BlockSpec((None, D_MODEL, FC), lambda i, j: (l, 0, NFC + up_chunk(j))),
            pl.BlockSpec((None, 8, FC), lambda i, j: (l, 0, gate_chunk(j))),
            pl.BlockSpec((None, 8, FC), lambda i, j: (l, 0, NFC + gate_chunk(j))),
            pl.BlockSpec((None, D_FF, D_MODEL), lambda i, j: (l, 0, 0), pipeline_mode=pl.Buffered(1)),
            pl.BlockSpec((None, 8, D_MODEL), lambda i, j: (l, 0, 0)),
        ],
        out_specs=out_specs,
        out_shape=out_shape,
        scratch_shapes=[pltpu.VMEM((TM + HALO_ROWS, D_MODEL), BF16),
                        pltpu.VMEM((NFC, TM, FC), BF16),
                        pltpu.VMEM((LANE_BLOCKS, 8 * SEG_PITCH, 128), F32),
                        pltpu.VMEM((TM + HALO_ROWS, 2 * FC), F32),
                        pltpu.VMEM((TM + HALO_ROWS, 2 * FC), F32),
                        pltpu.VMEM((D_MODEL, 2 * FC), BF16)],
        compiler_params=pltpu.CompilerParams(
            dimension_semantics=("arbitrary", "arbitrary"), vmem_limit_bytes=VMEM_LIMIT),
        name="convffn",
    )(x, x, x, mod, up, up, conv8, conv8, down, ln8)


TMF = 512
SEGF = TMF // 8
PITCHF = SEGF + 8
CTXF_TILES = N_CTX // TMF
LATF_PER_SEQ = DEC_SEQ // TMF
RBUF = 256


def _up_weight_kernel(a_ref, g_ref, o_ref):
    for c in range(NFC):
        o_ref[c, :, 0:FC] = a_ref[:, c * FC:(c + 1) * FC].astype(BF16)
        o_ref[c, :, FC:2 * FC] = g_ref[:, c * FC:(c + 1) * FC].astype(BF16)


def _up_weight_chunks(ffn_up):
    half = D_MODEL // 2
    return pl.pallas_call(
        _up_weight_kernel,
        grid=(DEPTH, 2),
        in_specs=[pl.BlockSpec((None, half, D_FF), lambda l, r: (l, r, 0)),
                  pl.BlockSpec((None, half, D_FF), lambda l, r: (l, r, 1))],
        out_specs=pl.BlockSpec((None, NFC, half, 2 * FC), lambda l, r: (l, 0, r, 0)),
        out_shape=jax.ShapeDtypeStruct((DEPTH, NFC, D_MODEL, 2 * FC), BF16),
        compiler_params=pltpu.CompilerParams(
            dimension_semantics=("arbitrary", "arbitrary"), vmem_limit_bytes=VMEM_LIMIT),
        name="cast_up_chunks",
    )(ffn_up, ffn_up)


def _seg_rows_f(xc_ref, k):
    return jnp.concatenate([xc_ref[cb, pl.ds(k, 8, stride=PITCHF), :] for cb in range(LANE_BLOCKS)], axis=1)


def _ffn_kernel(*refs, n_out):
    x_refs = refs[:2]
    (xp_ref, xn_ref, mod_ref, up_ref, cv_ref, dn_ref, ln_ref) = refs[2:9]
    o_refs = refs[9:9 + n_out]
    h_scr, act_scr, xc_scr, u0_scr, u1_scr = refs[9 + n_out:]
    u_scrs = (u0_scr, u1_scr)
    i = pl.program_id(0)
    is_ctx = i < CTXF_TILES
    lat_pos = (i - CTXF_TILES) % LATF_PER_SEQ
    m = mod_ref[...]
    ln = ln_ref[...]

    scale = 1.0 + m[4:5]
    shift = m[3:4]
    def stage(side):
        for cb in range(LANE_BLOCKS):
            for s in range(8):
                xc_scr[cb, s * PITCHF:s * PITCHF + SEGF, :] = x_refs[side][s * SEGF:(s + 1) * SEGF,
                                                                            cb * 128:(cb + 1) * 128]

    _per_half(i, TMF, 2, stage)
    for k in range(0, SEGF, 2):
        rows = jnp.concatenate([_seg_rows_f(xc_scr, k), _seg_rows_f(xc_scr, k + 1)], axis=0)
        h_scr[8 * k:8 * k + 16, :] = (rows * scale + shift).astype(BF16)
    sub16 = lax.broadcasted_iota(jnp.int32, (HALO_ROWS, D_MODEL), 0)
    prev_ok = jnp.logical_not(is_ctx) & (lat_pos > 0)
    next_ok = jnp.logical_not(is_ctx) & (lat_pos < LATF_PER_SEQ - 1)
    halo_x = jnp.where(sub16 == 0, xp_ref[POOL_HALO - 1:POOL_HALO, :], xn_ref[0:1, :])
    keep = ((sub16 == 0) & prev_ok) | ((sub16 == 1) & next_ok)
    h_scr[TMF:TMF + HALO_ROWS, :] = jnp.where(keep, halo_x * scale + shift, 0.0).astype(BF16)

    n_ub = TMF // RBUF
    sub = lax.broadcasted_iota(jnp.int32, (8, FC), 0)
    seg_per_seq = SEQ // SEGF
    ctx_first = is_ctx & (sub % seg_per_seq == 0)
    ctx_last = is_ctx & (sub % seg_per_seq == seg_per_seq - 1)

    def up_proj(slot, c, ub):
        rows = slice(ub * RBUF, (ub + 1) * RBUF + (HALO_ROWS if ub == n_ub - 1 else 0))
        u_scrs[slot][rows, :] = jnp.dot(h_scr[rows, :], up_ref[c], preferred_element_type=F32)

    def conv(u_ref, lanes, cvs, r0):
        lo = max(r0 - 8, 0)
        hi = min(r0 + GB + 8, TMF)
        ue = u_ref[lo:hi, lanes]
        u = ue[r0 - lo:r0 - lo + GB]
        if r0 == 0:
            b_first = jnp.where(sub == 0, u_ref[TMF:TMF + 1, lanes],
                                pltpu.roll(u_ref[TMF - 8:TMF, lanes], 1, axis=0))
            um1 = jnp.concatenate([jnp.where(ctx_first, 0.0, b_first), u[0:GB - 8]], axis=0)
        else:
            um1 = ue[0:GB]
        if r0 == TMF - GB:
            b_last = jnp.where(sub == 7, u_ref[TMF + 1:TMF + 2, lanes],
                               pltpu.roll(u_ref[0:8, lanes], 7, axis=0))
            up1 = jnp.concatenate([u[8:GB], jnp.where(ctx_last, 0.0, b_last)], axis=0)
        else:
            up1 = ue[r0 - lo + 8:r0 - lo + GB + 8]
        return um1 * cvs[0:1] + u * cvs[1:2] + up1 * cvs[2:3] + cvs[3:4]

    def gate(slot, c, ub):
        cva = cv_ref[0:4, c * FC:(c + 1) * FC]
        cvg = cv_ref[0:4, D_FF + c * FC:D_FF + (c + 1) * FC]
        for r0 in range(ub * RBUF, (ub + 1) * RBUF, GB):
            a = conv(u_scrs[slot], slice(0, FC), cva, r0)
            g = conv(u_scrs[slot], slice(FC, 2 * FC), cvg, r0)
            act_scr[c, r0:r0 + GB, :] = (_silu(a) * g).astype(BF16)

    for c in range(NFC + 1):
        for ub in range(n_ub):
            if c < NFC:
                up_proj(c % 2, c, ub)
            if c >= 1:
                gate((c - 1) % 2, c - 1, ub)

    for rb in range(TMF // RB):
        vrows = range(rb * RB // 8, (rb + 1) * RB // 8)
        lhs = jnp.concatenate([act_scr[c, rb * RB:(rb + 1) * RB, :] for c in range(NFC)], axis=1)
        y = jnp.dot(lhs, dn_ref[...], preferred_element_type=F32)
        xr = jnp.concatenate([_seg_rows_f(xc_scr, k) for k in vrows], axis=0)
        out = _layer_norm(ALPHA * xr + m[5:6] * y, ln[1:2], ln[3:4])
        for kk, k in enumerate(vrows):
            for cb in range(LANE_BLOCKS):
                xc_scr[cb, pl.ds(k, 8, stride=PITCHF), :] = out[8 * kk:8 * kk + 8, cb * 128:(cb + 1) * 128]

    def write_out(side):
        for cb in range(LANE_BLOCKS):
            for s in range(8):
                o_refs[side][s * SEGF:(s + 1) * SEGF, cb * 128:(cb + 1) * 128] = xc_scr[
                    cb, s * PITCHF:s * PITCHF + SEGF, :]

    _per_half(i, TMF, n_out, write_out)


def _ffn(xs, mod, up_chunks, conv8, down_bf16, ln8, l, split_out):
    halo_blocks = TMF // POOL_HALO
    last_halo = N_LAT // POOL_HALO - 1
    n_out = 2 if split_out else 1
    if split_out:
        out_shape = [jax.ShapeDtypeStruct((N_CTX, D_MODEL), F32), jax.ShapeDtypeStruct((N_LAT, D_MODEL), F32)]
    else:
        out_shape = [jax.ShapeDtypeStruct((ROWS, D_MODEL), F32)]
    return pl.pallas_call(
        functools.partial(_ffn_kernel, n_out=n_out),
        grid=(ROWS // TMF,),
        in_specs=[
            *_tile_specs(2, TMF, 1),
            pl.BlockSpec((POOL_HALO, D_MODEL),
                         lambda i: (jnp.maximum((i - CTXF_TILES) * halo_blocks - 1, 0), 0)),
            pl.BlockSpec((POOL_HALO, D_MODEL),
                         lambda i: (jnp.clip((i - CTXF_TILES + 1) * halo_blocks, 0, last_halo), 0)),
            pl.BlockSpec((None, None, 6, D_MODEL), lambda i: (l, _cond_of_tile(i, TMF), 0, 0)),
            pl.BlockSpec((None, NFC, D_MODEL, 2 * FC), lambda i: (l, 0, 0, 0), pipeline_mode=pl.Buffered(1)),
            pl.BlockSpec((None, 8, 2 * D_FF), lambda i: (l, 0, 0)),
            pl.BlockSpec((None, D_FF, D_MODEL), lambda i: (l, 0, 0), pipeline_mode=pl.Buffered(1)),
            pl.BlockSpec((None, 8, D_MODEL), lambda i: (l, 0, 0)),
        ],
        out_specs=_tile_specs(n_out, TMF, 1),
        out_shape=out_shape,
        scratch_shapes=[pltpu.VMEM((TMF + HALO_ROWS, D_MODEL), BF16),
                        pltpu.VMEM((NFC, TMF, FC), BF16),
                        pltpu.VMEM((LANE_BLOCKS, 8 * PITCHF, 128), F32),
                        pltpu.VMEM((TMF + HALO_ROWS, 2 * FC), F32),
                        pltpu.VMEM((TMF + HALO_ROWS, 2 * FC), F32)],
        compiler_params=pltpu.CompilerParams(
            dimension_semantics=("arbitrary",), vmem_limit_bytes=VMEM_LIMIT),
        name="convffn",
    )(xs[0], xs[1], xs[1], xs[1], mod, up_chunks, conv8, down_bf16, ln8)


def _head_masks(width):
    lane = lax.broadcasted_iota(jnp.int32, (1, width), 1)
    return [(lane >= h * HEAD_DIM) & (lane < (h + 1) * HEAD_DIM) for h in range(width // HEAD_DIM)]


def _stack_heads(x, masks):
    return jnp.concatenate([jnp.where(m, x, jnp.zeros_like(x)) for m in masks], axis=0)


def _rope(x, cos, sin):
    lane = lax.broadcasted_iota(jnp.int32, (1, 128), 1)
    lower = (lane & 31) < 16
    outs = []
    for k in range(x.shape[1] // 128):
        xb = x[:, k * 128:(k + 1) * 128]
        partner = jnp.where(lower, pltpu.roll(xb, 112, axis=1), pltpu.roll(xb, 16, axis=1))
        outs.append(xb * cos + partner * sin)
    return outs[0] if len(outs) == 1 else jnp.concatenate(outs, axis=1)


def _retention_chunk(q, k, v, dmat, qdec, kdec, cdec, state_scr, masks, bd_mask):
    qb = q.astype(BF16)
    kb = k.astype(BF16)
    vb = v.astype(BF16)
    s = lax.dot_general(_stack_heads(qb, masks), kb, (((1,), (1,)), ((), ())),
                        preferred_element_type=F32)
    p = (s * dmat).astype(BF16)
    p_cat = jnp.concatenate([p[h * CHUNK:(h + 1) * CHUNK] for h in range(N_HEADS)], axis=1)
    state = state_scr[...]
    o = (jnp.dot(p_cat, _stack_heads(vb, masks), preferred_element_type=F32)
         + _bdot(q * qdec, state))
    upd = lax.dot_general((k * kdec).astype(BF16), vb, (((0,), (0,)), ((), ())),
                          preferred_element_type=F32)
    state_scr[...] = state * cdec + jnp.where(bd_mask, upd, 0.0)
    return o


def _load_state(state_scr, blocks_ref, d):
    state_scr[...] = jnp.zeros((GW, GW), F32)
    for h in range(N_HEADS):
        sl = slice(h * HEAD_DIM, (h + 1) * HEAD_DIM)
        state_scr[sl, sl] = blocks_ref[d, h]


def _store_state(blocks_ref, d, state_scr):
    for h in range(N_HEADS):
        sl = slice(h * HEAD_DIM, (h + 1) * HEAD_DIM)
        blocks_ref[d, h] = state_scr[sl, sl]


def _group_norm(o, gmat):
    mu = jnp.dot(o.astype(BF16), gmat, preferred_element_type=F32)
    d = o - mu
    var = jnp.dot((d * d).astype(BF16), gmat, preferred_element_type=F32)
    return d * lax.rsqrt(var + LN_EPS)


def _mixer_kernel_old(*refs, latent, nc):
    if latent:
        (qa_ref, ka_ref, va_ref, kx_ref, vx_ref, ub_ref, vb_ref, qc_ref, kc_ref, vc_ref, gf_ref,
         gb_ref, pd_ref, pdp_ref, pdn_ref, cos_ref, sin_ref, s0_ref, dmat_ref, dec_ref, sink_ref,
         ws_ref, bias_ref, vec_ref, gmat_ref, cnt_ref, wpool_ref,
         mix_ref, sf_scr, sb_scr, ob_scr, pext_scr, k_scr, v_scr) = refs
    else:
        (qa_ref, kx_ref, vx_ref, ub_ref, vb_ref, qc_ref, kc_ref, vc_ref, gf_ref,
         gb_ref, pd_ref, pdp_ref, pdn_ref, dmat_ref, dec_ref, sink_ref,
         ws_ref, bias_ref, vec_ref, gmat_ref, cnt_ref, wpool_ref,
         mix_ref, st_ref, kn_ref, vn_ref, sf_scr, sb_scr, ob_scr, pext_scr) = refs

    p = pl.program_id(1)
    c = pl.program_id(2)
    masks = _head_masks(GW)
    row = lax.broadcasted_iota(jnp.int32, (GW, GW), 0)
    col = lax.broadcasted_iota(jnp.int32, (GW, GW), 1)
    bd_mask = (row // HEAD_DIM) == (col // HEAD_DIM)
    vec = vec_ref[...]

    @pl.when(p == 0)
    def _():
        rc = nc - 1 - c

        @pl.when(c == 0)
        def _():
            if latent:
                _load_state(sb_scr, s0_ref, 1)
                zero_blk = jnp.zeros((CHUNK, 2 * HEAD_DIM), BF16)
                k_scr[0:CHUNK, :] = zero_blk
                v_scr[0:CHUNK, :] = zero_blk
                k_scr[(nc + 1) * CHUNK:(nc + 2) * CHUNK, :] = zero_blk
                v_scr[(nc + 1) * CHUNK:(nc + 2) * CHUNK, :] = zero_blk
            else:
                sb_scr[...] = jnp.zeros((GW, GW), F32)

        if latent:
            dst = pl.ds(pl.multiple_of((rc + 1) * CHUNK, CHUNK), CHUNK)
            k_scr[dst, :] = _rope(ka_ref[...], cos_ref[...], sin_ref[...]).astype(BF16)
            v_scr[dst, :] = va_ref[...].astype(BF16)

        dec = dec_ref[...]
        o_b = _retention_chunk(qc_ref[...], kc_ref[...], vc_ref[...], dmat_ref[1],
                               dec[1], dec[3], vec[6:7], sb_scr, masks, bd_mask)
        ob_scr[pl.ds(pl.multiple_of(rc * CHUNK, CHUNK), CHUNK), :] = o_b

        if not latent:
            @pl.when(c == nc - 1)
            def _():
                _store_state(st_ref, 1, sb_scr)

    @pl.when(p == 1)
    def _():
        @pl.when(c == 0)
        def _():
            if latent:
                _load_state(sf_scr, s0_ref, 0)
            else:
                sf_scr[...] = jnp.zeros((GW, GW), F32)
                kn_ref[...] = kx_ref[...]
                vn_ref[...] = vx_ref[...]

        q = qa_ref[...]
        if latent:
            q = _rope(q, cos_ref[...], sin_ref[...])
        q = q * (HEAD_DIM ** -0.5)
        lane = lax.broadcasted_iota(jnp.int32, (1, 2 * HEAD_DIM), 1)
        lo = lane < HEAD_DIM
        q0, q1 = q[:, :128], q[:, 128:]
        zero = jnp.zeros_like(q0)
        q_st = jnp.concatenate([
            jnp.where(lo, q0, zero),
            jnp.where(lo, pltpu.roll(q0, HEAD_DIM, axis=1), zero),
            jnp.where(lo, zero, pltpu.roll(q1, HEAD_DIM, axis=1)),
            jnp.where(lo, zero, q1)], axis=0).astype(BF16)
        if latent:
            band = pl.ds(pl.multiple_of(c * CHUNK, CHUNK), 3 * CHUNK)
            k_all = jnp.concatenate([k_scr[band, :], kx_ref[...].astype(BF16)], axis=0)
            v_all = jnp.concatenate([v_scr[band, :], vx_ref[...].astype(BF16)], axis=0)
        else:
            k_all = kx_ref[...].astype(BF16)
            v_all = vx_ref[...].astype(BF16)
        s = lax.dot_general(q_st, k_all, (((1,), (1,)), ((), ())), preferred_element_type=F32)
        if latent:
            nk = 3 * CHUNK + PAST_LEN
            qi = lax.broadcasted_iota(jnp.int32, (N_HEADS * CHUNK, nk), 0) & (CHUNK - 1)
            kj = lax.broadcasted_iota(jnp.int32, (N_HEADS * CHUNK, nk), 1)
            kpos = kj + (c - 1) * CHUNK
            valid = (kj >= 3 * CHUNK) | ((kj >= qi) & (kj <= qi + 2 * CHUNK)
                                         & (kpos >= 0) & (kpos < nc * CHUNK))
            s = jnp.where(valid, s, NEG_INF)
        sink = sink_ref[...][:, 0:1]
        mx = jnp.maximum(jnp.max(s, axis=-1, keepdims=True), sink)
        e = jnp.exp(s - mx)
        den = jnp.sum(e, axis=-1, keepdims=True) + jnp.exp(sink - mx)
        o = jnp.dot(e.astype(BF16), v_all, preferred_element_type=F32) / den
        mix_ref[:, 0:128] = jnp.where(lo, o[0:CHUNK], pltpu.roll(o[CHUNK:2 * CHUNK], HEAD_DIM, axis=1))
        mix_ref[:, 128:256] = jnp.where(lo, pltpu.roll(o[2 * CHUNK:3 * CHUNK], HEAD_DIM, axis=1),
                                        o[3 * CHUNK:4 * CHUNK])

        vn = _layer_norm(vb_ref[...], vec[0:1], vec[1:2]).astype(BF16)
        sg = jnp.dot(ws_ref[...], _stack_heads(vn, masks), preferred_element_type=F32) + bias_ref[...]
        mix_ref[:, GW:2 * GW] = ub_ref[...] * sg

        dec = dec_ref[...]
        o_f = _retention_chunk(qc_ref[...], kc_ref[...], vc_ref[...], dmat_ref[0],
                               dec[0], dec[2], vec[5:6], sf_scr, masks, bd_mask)
        o_b = ob_scr[pl.ds(pl.multiple_of(c * CHUNK, CHUNK), CHUNK), :]
        gmat = gmat_ref[...]
        mix_ref[:, 2 * GW:3 * GW] = (_silu(gf_ref[...]) * (_group_norm(o_f, gmat) * vec[3:4])
                                     + _silu(gb_ref[...]) * (_group_norm(o_b, gmat) * vec[4:5]))
        if not latent:
            @pl.when(c == nc - 1)
            def _():
                _store_state(st_ref, 0, sf_scr)

        pd = pd_ref[...]
        pext_scr[0:POOL_HALO, :] = jnp.where(c > 0, pdp_ref[...], 0.0)
        pext_scr[POOL_HALO:POOL_HALO + CHUNK, :] = pd
        pext_scr[POOL_HALO + CHUNK:2 * POOL_HALO + CHUNK, :] = jnp.where(c < nc - 1, pdn_ref[...], 0.0)

        def win(d, half):
            return pext_scr[pl.ds(POOL_HALO + d, CHUNK), half * 128:(half + 1) * 128]

        a2 = win(-1, 0) + win(0, 0)
        a4 = a2 + win(-2, 0) + win(1, 0)
        a8 = win(-4, 1)
        for d in range(-3, 4):
            a8 = a8 + win(d, 1)
        a16 = a8
        for d in list(range(-8, -4)) + list(range(4, 8)):
            a16 = a16 + win(d, 1)
        sums = jnp.concatenate([jnp.where(lo, a2, a4), jnp.where(lo, a8, a16)], axis=1)
        yd = sums * cnt_ref[...] - pd
        mix_ref[:, 3 * GW:4 * GW] = _bdot(yd, wpool_ref[...]) * vec[2:3]


def _mixer_old(z, tabs, l, latent, extra=None):
    nb = DEC_BATCH if latent else BATCH
    nc = (DEC_SEQ if latent else SEQ) // CHUNK
    base = (N_CTX // CHUNK) if latent else 0
    last_halo = ROWS // POOL_HALO - 1
    per8 = CHUNK // POOL_HALO

    def fwd(b, p, c):
        return base + b * nc + c * p

    def both(b, p, c):
        return base + b * nc + jnp.where(p == 0, nc - 1 - c, c)

    def col(width, idx, rowmap):
        return pl.BlockSpec((CHUNK, width), lambda b, p, c: (rowmap(b, p, c), idx))

    def const(shape):
        return pl.BlockSpec(shape, lambda b, p, c: (0,) * len(shape))

    def layer(shape):
        return pl.BlockSpec((None,) + shape, lambda b, p, c: (l,) + (0,) * len(shape))

    specs, args = [], []

    def add(spec, arr):
        specs.append(spec)
        args.append(arr)

    add(col(GW, 0, fwd), z)
    if latent:
        add(pl.BlockSpec((CHUNK, 128), lambda b, p, c: (base + b * nc + (nc - 1 - c) * (1 - p), 2)), z)
        add(pl.BlockSpec((CHUNK, 128), lambda b, p, c: (base + b * nc + (nc - 1 - c) * (1 - p), 3)), z)
        add(pl.BlockSpec((None, None, PAST_LEN, 128), lambda b, p, c: (b, l, 0, 0)), extra["ck"])
        add(pl.BlockSpec((None, None, PAST_LEN, 128), lambda b, p, c: (b, l, 0, 0)), extra["cv"])
    else:
        add(pl.BlockSpec((SEQ, 128), lambda b, p, c: (b, 2)), z)
        add(pl.BlockSpec((SEQ, 128), lambda b, p, c: (b, 3)), z)
    add(col(GW, 2, fwd), z)
    add(col(GW, 3, fwd), z)
    add(col(GW, 4, both), z)
    add(col(GW, 5, both), z)
    add(col(GW, 6, both), z)
    add(col(GW, 7, fwd), z)
    add(col(GW, 8, fwd), z)
    add(col(GW, 9, fwd), z)
    add(pl.BlockSpec((POOL_HALO, GW),
                     lambda b, p, c: (jnp.maximum(fwd(b, p, c) * per8 - 1, 0), 9)), z)
    add(pl.BlockSpec((POOL_HALO, GW),
                     lambda b, p, c: (jnp.minimum((fwd(b, p, c) + 1) * per8, last_halo), 9)), z)
    if latent:
        rope_map = lambda b, p, c: (jnp.where(p == 0, nc - 1 - c, c), 0)
        add(pl.BlockSpec((CHUNK, 128), rope_map), extra["cos"])
        add(pl.BlockSpec((CHUNK, 128), rope_map), extra["sin"])
        add(pl.BlockSpec((None, None, 2, N_HEADS, HEAD_DIM, HEAD_DIM),
                         lambda b, p, c: (b, l, 0, 0, 0, 0)), extra["s0"])
    add(layer((2, N_HEADS * CHUNK, CHUNK)), tabs["dmat"])
    add(layer((4, CHUNK, GW)), tabs["dec"])
    add(layer((N_HEADS * CHUNK, 128)), tabs["sink"])
    add(layer((CHUNK, N_HEADS * CHUNK)), tabs["ws"])
    add(layer((CHUNK, GW)), tabs["bias"])
    add(layer((8, GW)), tabs["vec"])
    add(const((GW, GW)), tabs["gmat"])
    add(pl.BlockSpec((CHUNK, GW), lambda b, p, c: (c * p, 0)), tabs["cnt_lat"] if latent else tabs["cnt_ctx"])
    add(layer((GW, GW)), tabs["wpool"])

    out_shape = [jax.ShapeDtypeStruct((nb * nc * CHUNK, D_MODEL), F32)]
    out_specs = [pl.BlockSpec((CHUNK, D_MODEL), lambda b, p, c: (b * nc + c * p, 0))]
    scratch = [pltpu.VMEM((GW, GW), F32), pltpu.VMEM((GW, GW), F32),
               pltpu.VMEM((nc * CHUNK, GW), F32),
               pltpu.VMEM((CHUNK + 2 * POOL_HALO, GW), F32)]
    if latent:
        scratch += [pltpu.VMEM(((nc + 2) * CHUNK, 128), BF16), pltpu.VMEM(((nc + 2) * CHUNK, 128), BF16)]
    else:
        out_shape.append(jax.ShapeDtypeStruct((nb, 2, N_HEADS, HEAD_DIM, HEAD_DIM), F32))
        out_specs.append(pl.BlockSpec((None, 2, N_HEADS, HEAD_DIM, HEAD_DIM), lambda b, p, c: (b, 0, 0, 0, 0)))
        for _ in range(2):
            out_shape.append(jax.ShapeDtypeStruct((nb, SEQ, 128), F32))
            out_specs.append(pl.BlockSpec((None, SEQ, 128), lambda b, p, c: (b, 0, 0)))

    return pl.pallas_call(
        functools.partial(_mixer_kernel, latent=latent, nc=nc),
        grid=(nb, 2, nc),
        in_specs=specs,
        out_specs=out_specs,
        out_shape=out_shape,
        scratch_shapes=scratch,
        compiler_params=pltpu.CompilerParams(
            dimension_semantics=("arbitrary", "arbitrary", "arbitrary"), vmem_limit_bytes=VMEM_LIMIT),
        name="mixer_latent" if latent else "mixer_context",
    )(*args)


LAT_GROUP = 4


def _chunk_off(c):
    return c * CHUNK if isinstance(c, int) else pl.multiple_of(c * CHUNK, CHUNK)


def _mixer_kernel_v9(*refs, latent, group, ng):
    if latent:
        (qa_ref, ka_ref, va_ref, kx_ref, vx_ref, ub_ref, vb_ref, qc_ref, kc_ref, vc_ref, gf_ref, gb_ref,
         pd_ref, pdp_ref, pdn_ref, cos_ref, sin_ref, s0_ref, dmat_ref, dec_ref, sink_ref, ws_ref, bias_ref,
         vec_ref, gmat_ref, cnt_ref, wpool_ref, xres_ref, mod_ref, wout_ref, ln_ref,
         x1_ref, sf_scr, sb_scr, ob_scr, pext_scr, mix_ref, k_scr, v_scr) = refs
    else:
        (qa_ref, ka_ref, va_ref, ub_ref, vb_ref, qc_ref, kc_ref, vc_ref, gf_ref, gb_ref, pd_ref,
         dmat_ref, dec_ref, sink_ref, ws_ref, bias_ref, vec_ref, gmat_ref, cnt_ref, wpool_ref,
         xres_ref, mod_ref, wout_ref, ln_ref,
         x1_ref, st_ref, kn_ref, vn_ref, sf_scr, sb_scr, ob_scr, pext_scr, mix_ref) = refs

    nc = group * ng
    g_a = mod_ref[...][2:3]
    ln = ln_ref[...]
    masks = _head_masks(GW)
    row = lax.broadcasted_iota(jnp.int32, (GW, GW), 0)
    col = lax.broadcasted_iota(jnp.int32, (GW, GW), 1)
    bd_mask = (row // HEAD_DIM) == (col // HEAD_DIM)
    vec = vec_ref[...]
    lane = lax.broadcasted_iota(jnp.int32, (1, 2 * HEAD_DIM), 1)
    lo = lane < HEAD_DIM

    def rows(k):
        return slice(k * CHUNK, (k + 1) * CHUNK)

    def backward_chunk(k, c):
        if latent:
            dst = pl.ds(_chunk_off(c + 1), CHUNK)
            k_scr[dst, :] = _rope(ka_ref[rows(k), :], cos_ref[rows(k), :], sin_ref[rows(k), :]).astype(BF16)
            v_scr[dst, :] = va_ref[rows(k), :].astype(BF16)
        dec = dec_ref[...]
        o_b = _retention_chunk(qc_ref[rows(k), :], kc_ref[rows(k), :], vc_ref[rows(k), :], dmat_ref[1],
                               dec[1], dec[3], vec[6:7], sb_scr, masks, bd_mask)
        ob_scr[pl.ds(_chunk_off(c), CHUNK), :] = o_b

    def forward_chunk(k, c):
        q = qa_ref[rows(k), :]
        if latent:
            q = _rope(q, cos_ref[rows(k), :], sin_ref[rows(k), :])
        q = q * (HEAD_DIM ** -0.5)
        q0, q1 = q[:, :128], q[:, 128:]
        zero = jnp.zeros_like(q0)
        q_st = jnp.concatenate([
            jnp.where(lo, q0, zero),
            jnp.where(lo, pltpu.roll(q0, HEAD_DIM, axis=1), zero),
            jnp.where(lo, zero, pltpu.roll(q1, HEAD_DIM, axis=1)),
            jnp.where(lo, zero, q1)], axis=0).astype(BF16)
        if latent:
            band = pl.ds(_chunk_off(c), 3 * CHUNK)
            k_all = jnp.concatenate([k_scr[band, :], kx_ref[...].astype(BF16)], axis=0)
            v_all = jnp.concatenate([v_scr[band, :], vx_ref[...].astype(BF16)], axis=0)
        else:
            k_all = ka_ref[...].astype(BF16)
            v_all = va_ref[...].astype(BF16)
        s = lax.dot_general(q_st, k_all, (((1,), (1,)), ((), ())), preferred_element_type=F32)
        if latent:
            nk = 3 * CHUNK + PAST_LEN
            qi = lax.broadcasted_iota(jnp.int32, (N_HEADS * CHUNK, nk), 0) & (CHUNK - 1)
            kj = lax.broadcasted_iota(jnp.int32, (N_HEADS * CHUNK, nk), 1)
            kpos = kj + (c - 1) * CHUNK
            valid = (kj >= 3 * CHUNK) | ((kj >= qi) & (kj <= qi + 2 * CHUNK)
                                         & (kpos >= 0) & (kpos < nc * CHUNK))
            s = jnp.where(valid, s, NEG_INF)
        sink = sink_ref[...][:, 0:1]
        mx = jnp.maximum(jnp.max(s, axis=-1, keepdims=True), sink)
        e = jnp.exp(s - mx)
        den = jnp.sum(e, axis=-1, keepdims=True) + jnp.exp(sink - mx)
        o = jnp.dot(e.astype(BF16), v_all, preferred_element_type=F32) / den
        mix_ref[rows(k), 0:128] = jnp.where(lo, o[0:CHUNK], pltpu.roll(o[CHUNK:2 * CHUNK], HEAD_DIM, axis=1))
        mix_ref[rows(k), 128:256] = jnp.where(lo, pltpu.roll(o[2 * CHUNK:3 * CHUNK], HEAD_DIM, axis=1),
                                              o[3 * CHUNK:4 * CHUNK])

        vn = _layer_norm(vb_ref[rows(k), :], vec[0:1], vec[1:2]).astype(BF16)
        sg = jnp.dot(ws_ref[...], _stack_heads(vn, masks), preferred_element_type=F32) + bias_ref[...]
        mix_ref[rows(k), GW:2 * GW] = ub_ref[rows(k), :] * sg

        dec = dec_ref[...]
        o_f = _retention_chunk(qc_ref[rows(k), :], kc_ref[rows(k), :], vc_ref[rows(k), :], dmat_ref[0],
                               dec[0], dec[2], vec[5:6], sf_scr, masks, bd_mask)
        o_b = ob_scr[pl.ds(_chunk_off(c), CHUNK), :]
        gmat = gmat_ref[...]
        mix_ref[rows(k), 2 * GW:3 * GW] = (
            _silu(gf_ref[rows(k), :]) * (_group_norm(o_f, gmat) * vec[3:4])
            + _silu(gb_ref[rows(k), :]) * (_group_norm(o_b, gmat) * vec[4:5]))

        pd = pd_ref[rows(k), :]
        pext = pext_scr.at[k]
        zeros8 = jnp.zeros((POOL_HALO, GW), F32)
        if k > 0:
            prev8 = pd_ref[k * CHUNK - POOL_HALO:k * CHUNK, :]
        else:
            prev8 = jnp.where(c > 0, pdp_ref[...], 0.0) if latent else zeros8
        if k < group - 1:
            next8 = pd_ref[(k + 1) * CHUNK:(k + 1) * CHUNK + POOL_HALO, :]
        else:
            next8 = jnp.where(c < nc - 1, pdn_ref[...], 0.0) if latent else zeros8
        pext[0:POOL_HALO, :] = prev8
        pext[POOL_HALO:POOL_HALO + CHUNK, :] = pd
        pext[POOL_HALO + CHUNK:2 * POOL_HALO + CHUNK, :] = next8

        def win(d, half):
            return pext[pl.ds(POOL_HALO + d, CHUNK), half * 128:(half + 1) * 128]

        a2 = win(-1, 0) + win(0, 0)
        a4 = a2 + win(-2, 0) + win(1, 0)
        a8 = win(-4, 1)
        for d in range(-3, 4):
            a8 = a8 + win(d, 1)
        a16 = a8
        for d in list(range(-8, -4)) + list(range(4, 8)):
            a16 = a16 + win(d, 1)
        sums = jnp.concatenate([jnp.where(lo, a2, a4), jnp.where(lo, a8, a16)], axis=1)
        yd = sums * cnt_ref[rows(k), :] - pd
        mix_ref[rows(k), 3 * GW:4 * GW] = _bdot(yd, wpool_ref[...]) * vec[2:3]

        y = jnp.dot(mix_ref[rows(k), :].astype(BF16), wout_ref[...], preferred_element_type=F32)
        x1_ref[rows(k), :] = _layer_norm(ALPHA * xres_ref[rows(k), :] + g_a * y, ln[0:1], ln[2:3])

    if not latent:
        sb_scr[...] = jnp.zeros((GW, GW), F32)
        for k in reversed(range(group)):
            backward_chunk(k, k)
        _store_state(st_ref, 1, sb_scr)
        sf_scr[...] = jnp.zeros((GW, GW), F32)
        kn_ref[...] = ka_ref[...]
        vn_ref[...] = va_ref[...]
        for k in range(group):
            forward_chunk(k, k)
        _store_state(st_ref, 0, sf_scr)
        return

    p = pl.program_id(1)
    g = pl.program_id(2)

    @pl.when(p == 0)
    def _():
        @pl.when(g == 0)
        def _():
            _load_state(sb_scr, s0_ref, 1)
            zero_blk = jnp.zeros((CHUNK, 2 * HEAD_DIM), BF16)
            for scr in (k_scr, v_scr):
                scr[0:CHUNK, :] = zero_blk
                scr[(nc + 1) * CHUNK:(nc + 2) * CHUNK, :] = zero_blk

        for k in reversed(range(group)):
            backward_chunk(k, (ng - 1 - g) * group + k)

    @pl.when(p == 1)
    def _():
        @pl.when(g == 0)
        def _():
            _load_state(sf_scr, s0_ref, 0)

        for k in range(group):
            forward_chunk(k, g * group + k)


def _interleave(chains):
    chains = list(chains)
    while chains:
        for ch in list(chains):
            try:
                next(ch)
            except StopIteration:
                chains.remove(ch)


def _mixer_kernel(*refs, latent, group, ng):
    if latent:
        (qa_ref, ka_ref, va_ref, kx_ref, vx_ref, ub_ref, vb_ref, qc_ref, kc_ref, vc_ref, gf_ref, gb_ref,
         pd_ref, pdp_ref, pdn_ref, cos_ref, sin_ref, s0_ref, dmat_ref, dec_ref, sink_ref, ws_ref, bias_ref,
         vec_ref, gmat_ref, cnt_ref, wpool_ref, xres_ref, mod_ref, wout_ref, ln_ref,
         x1_ref, sf_scr, sb_scr, ob_scr, pext_scr, mix_ref, k_scr, v_scr) = refs
    else:
        (qa_ref, ka_ref, va_ref, ub_ref, vb_ref, qc_ref, kc_ref, vc_ref, gf_ref, gb_ref, pd_ref,
         dmat_ref, dec_ref, sink_ref, ws_ref, bias_ref, vec_ref, gmat_ref, cnt_ref, wpool_ref,
         xres_ref, mod_ref, wout_ref, ln_ref,
         x1_ref, st_ref, kn_ref, vn_ref, sf_scr, sb_scr, ob_scr, pext_scr, mix_ref) = refs

    nc = group * ng
    g_a = mod_ref[...][2:3]
    ln = ln_ref[...]
    masks = _head_masks(GW)
    row = lax.broadcasted_iota(jnp.int32, (GW, GW), 0)
    col = lax.broadcasted_iota(jnp.int32, (GW, GW), 1)
    bd_mask = (row // HEAD_DIM) == (col // HEAD_DIM)
    vec = vec_ref[...]
    lane = lax.broadcasted_iota(jnp.int32, (1, 2 * HEAD_DIM), 1)
    lo = lane < HEAD_DIM

    def rows(k):
        return slice(k * CHUNK, (k + 1) * CHUNK)

    def retention(order, d, state_scr, out):
        dmat = dmat_ref[d]
        qdec, kdec = dec_ref[d], dec_ref[2 + d]
        cdec = vec[5 + d:6 + d]
        state = state_scr[...]
        for k in order:
            q, kk, v = qc_ref[rows(k), :], kc_ref[rows(k), :], vc_ref[rows(k), :]
            qb, kb, vb = q.astype(BF16), kk.astype(BF16), v.astype(BF16)
            s = lax.dot_general(_stack_heads(qb, masks), kb, (((1,), (1,)), ((), ())),
                                preferred_element_type=F32)
            yield
            p = (s * dmat).astype(BF16)
            p_cat = jnp.concatenate([p[h * CHUNK:(h + 1) * CHUNK] for h in range(N_HEADS)], axis=1)
            o = (jnp.dot(p_cat, _stack_heads(vb, masks), preferred_element_type=F32)
                 + _bdot(q * qdec, state))
            yield
            upd = lax.dot_general((kk * kdec).astype(BF16), vb, (((0,), (0,)), ((), ())),
                                  preferred_element_type=F32)
            state = state * cdec + jnp.where(bd_mask, upd, 0.0)
            out[k] = o
            yield
        state_scr[...] = state

    def stage_kv(k, c):
        dst = pl.ds(_chunk_off(c + 1), CHUNK)
        k_scr[dst, :] = _rope(ka_ref[rows(k), :], cos_ref[rows(k), :], sin_ref[rows(k), :]).astype(BF16)
        yield
        v_scr[dst, :] = va_ref[rows(k), :].astype(BF16)
        yield

    def attention(k, c, done):
        q = qa_ref[rows(k), :]
        if latent:
            q = _rope(q, cos_ref[rows(k), :], sin_ref[rows(k), :])
        q = q * (HEAD_DIM ** -0.5)
        q0, q1 = q[:, :128], q[:, 128:]
        zero = jnp.zeros_like(q0)
        q_st = jnp.concatenate([
            jnp.where(lo, q0, zero),
            jnp.where(lo, pltpu.roll(q0, HEAD_DIM, axis=1), zero),
            jnp.where(lo, zero, pltpu.roll(q1, HEAD_DIM, axis=1)),
            jnp.where(lo, zero, q1)], axis=0).astype(BF16)
        if latent:
            band = pl.ds(_chunk_off(c), 3 * CHUNK)
            k_all = jnp.concatenate([k_scr[band, :], kx_ref[...].astype(BF16)], axis=0)
            v_all = jnp.concatenate([v_scr[band, :], vx_ref[...].astype(BF16)], axis=0)
        else:
            k_all = ka_ref[...].astype(BF16)
            v_all = va_ref[...].astype(BF16)
        s = lax.dot_general(q_st, k_all, (((1,), (1,)), ((), ())), preferred_element_type=F32)
        yield
        if latent:
            nk = 3 * CHUNK + PAST_LEN
            qi = lax.broadcasted_iota(jnp.int32, (N_HEADS * CHUNK, nk), 0) & (CHUNK - 1)
            kj = lax.broadcasted_iota(jnp.int32, (N_HEADS * CHUNK, nk), 1)
            kpos = kj + (c - 1) * CHUNK
            valid = (kj >= 3 * CHUNK) | ((kj >= qi) & (kj <= qi + 2 * CHUNK)
                                         & (kpos >= 0) & (kpos < nc * CHUNK))
            s = jnp.where(valid, s, NEG_INF)
        sink = sink_ref[...][:, 0:1]
        mx = jnp.maximum(jnp.max(s, axis=-1, keepdims=True), sink)
        yield
        e = jnp.exp(s - mx)
        den = jnp.sum(e, axis=-1, keepdims=True) + jnp.exp(sink - mx)
        yield
        o = jnp.dot(e.astype(BF16), v_all, preferred_element_type=F32) / den
        yield
        mix_ref[rows(k), 0:128] = jnp.where(lo, o[0:CHUNK], pltpu.roll(o[CHUNK:2 * CHUNK], HEAD_DIM, axis=1))
        mix_ref[rows(k), 128:256] = jnp.where(lo, pltpu.roll(o[2 * CHUNK:3 * CHUNK], HEAD_DIM, axis=1),
                                              o[3 * CHUNK:4 * CHUNK])
        done.add(("A", k))
        yield

    def gating_unit(k, done):
        vn = _layer_norm(vb_ref[rows(k), :], vec[0:1], vec[1:2]).astype(BF16)
        yield
        sg = jnp.dot(ws_ref[...], _stack_heads(vn, masks), preferred_element_type=F32) + bias_ref[...]
        yield
        mix_ref[rows(k), GW:2 * GW] = ub_ref[rows(k), :] * sg
        done.add(("B", k))
        yield

    def retention_mix(k, o_fwd, o_bwd, done):
        while k not in o_fwd or k not in o_bwd:
            yield
        gmat = gmat_ref[...]
        normed = []
        for o in (o_fwd[k], o_bwd[k]):
            mu = jnp.dot(o.astype(BF16), gmat, preferred_element_type=F32)
            yield
            dlt = o - mu
            var = jnp.dot((dlt * dlt).astype(BF16), gmat, preferred_element_type=F32)
            yield
            normed.append(dlt * lax.rsqrt(var + LN_EPS))
        mix_ref[rows(k), 2 * GW:3 * GW] = (_silu(gf_ref[rows(k), :]) * (normed[0] * vec[3:4])
                                           + _silu(gb_ref[rows(k), :]) * (normed[1] * vec[4:5]))
        done.add(("C", k))
        yield

    def pooling(k, c, done):
        pd = pd_ref[rows(k), :]
        pext = pext_scr.at[k]
        zeros8 = jnp.zeros((POOL_HALO, GW), F32)
        if k > 0:
            prev8 = pd_ref[k * CHUNK - POOL_HALO:k * CHUNK, :]
        else:
            prev8 = jnp.where(c > 0, pdp_ref[...], 0.0) if latent else zeros8
        if k < group - 1:
            next8 = pd_ref[(k + 1) * CHUNK:(k + 1) * CHUNK + POOL_HALO, :]
        else:
            next8 = jnp.where(c < nc - 1, pdn_ref[...], 0.0) if latent else zeros8
        pext[0:POOL_HALO, :] = prev8
        pext[POOL_HALO:POOL_HALO + CHUNK, :] = pd
        pext[POOL_HALO + CHUNK:2 * POOL_HALO + CHUNK, :] = next8
        yield

        def win(d, half):
            return pext[pl.ds(POOL_HALO + d, CHUNK), half * 128:(half + 1) * 128]

        a2 = win(-1, 0) + win(0, 0)
        a4 = a2 + win(-2, 0) + win(1, 0)
        yield
        a8 = win(-4, 1)
        for d in range(-3, 4):
            a8 = a8 + win(d, 1)
        yield
        a16 = a8
        for d in list(range(-8, -4)) + list(range(4, 8)):
            a16 = a16 + win(d, 1)
        yield
        sums = jnp.concatenate([jnp.where(lo, a2, a4), jnp.where(lo, a8, a16)], axis=1)
        yd = sums * cnt_ref[rows(k), :] - pd
        mix_ref[rows(k), 3 * GW:4 * GW] = _bdot(yd, wpool_ref[...]) * vec[2:3]
        done.add(("D", k))
        yield

    def out_projection(k, done):
        while not all((m, k) in done for m in "ABCD"):
            yield
        y = jnp.dot(mix_ref[rows(k), :].astype(BF16), wout_ref[...], preferred_element_type=F32)
        yield
        x1_ref[rows(k), :] = _layer_norm(ALPHA * xres_ref[rows(k), :] + g_a * y, ln[0:1], ln[2:3])
        yield

    def forward_chains(chunk_of, o_bwd):
        done, o_fwd = set(), {}
        chains = [retention(range(group), 0, sf_scr, o_fwd)]
        for k in range(group):
            chains += [attention(k, chunk_of(k), done), gating_unit(k, done), pooling(k, chunk_of(k), done),
                       retention_mix(k, o_fwd, o_bwd, done), out_projection(k, done)]
        return chains

    if not latent:
        sb_scr[...] = jnp.zeros((GW, GW), F32)
        sf_scr[...] = jnp.zeros((GW, GW), F32)
        kn_ref[...] = ka_ref[...]
        vn_ref[...] = va_ref[...]
        o_bwd = {}
        _interleave([retention(reversed(range(group)), 1, sb_scr, o_bwd)] + forward_chains(lambda k: k, o_bwd))
        _store_state(st_ref, 1, sb_scr)
        _store_state(st_ref, 0, sf_scr)
        return

    p = pl.program_id(1)
    g = pl.program_id(2)

    @pl.when(p == 0)
    def _():
        @pl.when(g == 0)
        def _():
            _load_state(sb_scr, s0_ref, 1)
            zero_blk = jnp.zeros((CHUNK, 2 * HEAD_DIM), BF16)
            for scr in (k_scr, v_scr):
                scr[0:CHUNK, :] = zero_blk
                scr[(nc + 1) * CHUNK:(nc + 2) * CHUNK, :] = zero_blk

        first = (ng - 1 - g) * group
        o_bwd = {}
        _interleave([retention(reversed(range(group)), 1, sb_scr, o_bwd)]
                    + [stage_kv(k, first + k) for k in range(group)])
        for k in range(group):
            ob_scr[pl.ds(_chunk_off(first + k), CHUNK), :] = o_bwd[k]

    @pl.when(p == 1)
    def _():
        @pl.when(g == 0)
        def _():
            _load_state(sf_scr, s0_ref, 0)

        first = g * group
        o_bwd = {k: ob_scr[pl.ds(_chunk_off(first + k), CHUNK), :] for k in range(group)}
        _interleave(forward_chains(lambda k: first + k, o_bwd))


def _mixer(z, x_res, mod, w_out_bf16, ln8, tabs, l, latent, extra=None):
    nb = DEC_BATCH if latent else BATCH
    nc = (DEC_SEQ if latent else SEQ) // CHUNK
    group = LAT_GROUP if latent else nc
    ng = nc // group
    blk = group * CHUNK
    base = (N_CTX // blk) if latent else 0
    per8 = blk // POOL_HALO
    last_halo = ROWS // POOL_HALO - 1

    def on_grid(f):
        return (lambda b, p, g: f(b, p, g)) if latent else (lambda b: f(b, 1, 0))

    def fwd(b, p, g):
        return base + b * ng + g * p

    def both(b, p, g):
        return base + b * ng + jnp.where(p == 0, ng - 1 - g, g)

    def bwd_only(b, p, g):
        return base + b * ng + (ng - 1 - g) * (1 - p)

    def col(width, idx, rowmap):
        return pl.BlockSpec((blk, width), on_grid(lambda b, p, g: (rowmap(b, p, g), idx)))

    def const(shape):
        return pl.BlockSpec(shape, on_grid(lambda b, p, g: (0,) * len(shape)))

    def layer(shape):
        return pl.BlockSpec((None,) + shape, on_grid(lambda b, p, g: (l,) + (0,) * len(shape)))

    specs, args = [], []

    def add(spec, arr):
        specs.append(spec)
        args.append(arr)

    add(col(GW, 0, fwd), z)
    add(col(128, 2, bwd_only if latent else fwd), z)
    add(col(128, 3, bwd_only if latent else fwd), z)
    if latent:
        add(pl.BlockSpec((None, None, PAST_LEN, 128), lambda b, p, g: (b, l, 0, 0)), extra["ck"])
        add(pl.BlockSpec((None, None, PAST_LEN, 128), lambda b, p, g: (b, l, 0, 0)), extra["cv"])
    add(col(GW, 2, fwd), z)
    add(col(GW, 3, fwd), z)
    add(col(GW, 4, both), z)
    add(col(GW, 5, both), z)
    add(col(GW, 6, both), z)
    add(col(GW, 7, fwd), z)
    add(col(GW, 8, fwd), z)
    add(col(GW, 9, fwd), z)
    if latent:
        add(pl.BlockSpec((POOL_HALO, GW),
                         lambda b, p, g: (jnp.maximum(fwd(b, p, g) * per8 - 1, 0), 9)), z)
        add(pl.BlockSpec((POOL_HALO, GW),
                         lambda b, p, g: (jnp.minimum((fwd(b, p, g) + 1) * per8, last_halo), 9)), z)
        rope_map = lambda b, p, g: (jnp.where(p == 0, ng - 1 - g, g), 0)
        add(pl.BlockSpec((blk, 128), rope_map), extra["cos"])
        add(pl.BlockSpec((blk, 128), rope_map), extra["sin"])
        add(pl.BlockSpec((None, None, 2, N_HEADS, HEAD_DIM, HEAD_DIM),
                         lambda b, p, g: (b, l, 0, 0, 0, 0)), extra["s0"])
    add(layer((2, N_HEADS * CHUNK, CHUNK)), tabs["dmat"])
    add(layer((4, CHUNK, GW)), tabs["dec"])
    add(layer((N_HEADS * CHUNK, 128)), tabs["sink"])
    add(layer((CHUNK, N_HEADS * CHUNK)), tabs["ws"])
    add(layer((CHUNK, GW)), tabs["bias"])
    add(layer((8, GW)), tabs["vec"])
    add(const((GW, GW)), tabs["gmat"])
    add(pl.BlockSpec((blk, GW), on_grid(lambda b, p, g: (g * p, 0))),
        tabs["cnt_lat"] if latent else tabs["cnt_ctx"])
    add(layer((GW, GW)), tabs["wpool"])
    local = lambda b, p, g: (b * ng + g * p, 0)
    add(pl.BlockSpec((blk, D_MODEL), on_grid(local)), x_res)
    add(pl.BlockSpec((None, None, 6, D_MODEL),
                     on_grid(lambda b, p, g: (l, (1 + b) if latent else 0, 0, 0))), mod)
    add(layer((D_MODEL, D_MODEL)), w_out_bf16)
    add(layer((8, D_MODEL)), ln8)

    out_shape = [jax.ShapeDtypeStruct((nb * nc * CHUNK, D_MODEL), F32)]
    out_specs = [pl.BlockSpec((blk, D_MODEL), on_grid(local))]
    scratch = [pltpu.VMEM((GW, GW), F32), pltpu.VMEM((GW, GW), F32),
               pltpu.VMEM((nc * CHUNK, GW), F32),
               pltpu.VMEM((group, CHUNK + 2 * POOL_HALO, GW), F32),
               pltpu.VMEM((blk, D_MODEL), F32)]
    if latent:
        scratch += [pltpu.VMEM(((nc + 2) * CHUNK, 128), BF16), pltpu.VMEM(((nc + 2) * CHUNK, 128), BF16)]
    else:
        out_shape.append(jax.ShapeDtypeStruct((nb, 2, N_HEADS, HEAD_DIM, HEAD_DIM), F32))
        out_specs.append(pl.BlockSpec((None, 2, N_HEADS, HEAD_DIM, HEAD_DIM), lambda b: (b, 0, 0, 0, 0)))
        for _ in range(2):
            out_shape.append(jax.ShapeDtypeStruct((nb, SEQ, 128), F32))
            out_specs.append(pl.BlockSpec((None, SEQ, 128), lambda b: (b, 0, 0)))

    return pl.pallas_call(
        functools.partial(_mixer_kernel, latent=latent, group=group, ng=ng),
        grid=(nb, 2, ng) if latent else (nb,),
        in_specs=specs,
        out_specs=out_specs,
        out_shape=out_shape,
        scratch_shapes=scratch,
        compiler_params=pltpu.CompilerParams(
            dimension_semantics=("arbitrary",) * (3 if latent else 1), vmem_limit_bytes=VMEM_LIMIT),
        name="mixer_latent" if latent else "mixer_context",
    )(*args)


def _pad_rows(rows, n=8):
    a = jnp.stack(rows)
    return jnp.concatenate([a, jnp.zeros((n - a.shape[0],) + a.shape[1:], a.dtype)], axis=0)


def _block_diag(blocks):
    g, n, _ = blocks.shape
    eye = jnp.eye(g, dtype=blocks.dtype)
    return (eye[:, None, :, None] * blocks[:, :, None, :]).reshape(g * n, g * n)


def _inv_count(n):
    t = np.arange(n)
    cols = []
    for w in POOL_WINDOWS:
        cnt = np.clip(t + w // 2, 0, n) - np.clip(t - w // 2, 0, n)
        cols.append(np.repeat((1.0 / cnt)[:, None], HEAD_DIM, axis=1))
    return jnp.asarray(np.concatenate(cols, axis=1), F32)


def _rope_tables():
    rows = DEC_SEQ // GRID_W
    r, cc = jnp.meshgrid(jnp.arange(rows), jnp.arange(GRID_W), indexing="ij")
    half = HEAD_DIM // 2
    freqs = ROPE_BASE ** (-jnp.arange(0, half, 2, dtype=F32) / half)

    def tables(pos):
        ang = pos.reshape(-1).astype(F32)[:, None] * freqs[None, :]
        cos, sin = jnp.cos(ang), jnp.sin(ang)
        return jnp.concatenate([cos, cos], axis=1), jnp.concatenate([-sin, sin], axis=1)

    cr, sr = tables(r)
    ccol, scol = tables(cc)
    cos = jnp.concatenate([cr, ccol], axis=1)
    sin = jnp.concatenate([sr, scol], axis=1)
    return jnp.tile(cos, (1, 2)), jnp.tile(sin, (1, 2))


def _layer_tables(attn_sink, sgu_norm_w, sgu_norm_b, sgu_ws, sgu_bs, ret_decay, ret_gn_w, pool_w, pool_scale):
    log_g = jax.nn.log_sigmoid(ret_decay.astype(F32))
    i = jnp.arange(CHUNK, dtype=F32)
    rel = i[:, None] - i[None, :]
    kscale = HEAD_DIM ** -0.5
    d_f = jnp.where(rel >= 0, jnp.exp(jnp.maximum(rel, 0.0)[None] * log_g[0][:, None, None]), 0.0)
    d_b = jnp.where(rel <= 0, jnp.exp(jnp.maximum(-rel, 0.0)[None] * log_g[1][:, None, None]), 0.0)
    dmat = jnp.stack([d_f.reshape(N_HEADS * CHUNK, CHUNK), d_b.reshape(N_HEADS * CHUNK, CHUNK)]) * kscale

    def lanes(per_head):
        return jnp.repeat(per_head, HEAD_DIM, axis=1)

    qdec_f = lanes(jnp.exp((i + 1.0)[:, None] * log_g[0][None, :]))
    qdec_b = lanes(jnp.exp((CHUNK - i)[:, None] * log_g[1][None, :]))
    kdec_f = lanes(jnp.exp((CHUNK - 1.0 - i)[:, None] * log_g[0][None, :])) * kscale
    kdec_b = lanes(jnp.exp(i[:, None] * log_g[1][None, :])) * kscale
    cdec = jnp.repeat(jnp.exp(CHUNK * log_g), HEAD_DIM, axis=1)
    vec = _pad_rows([sgu_norm_w, sgu_norm_b, pool_scale, ret_gn_w[0], ret_gn_w[1], cdec[0], cdec[1]])
    return {
        "dmat": dmat,
        "dec": jnp.stack([qdec_f, qdec_b, kdec_f, kdec_b]),
        "sink": jnp.broadcast_to(jnp.repeat(attn_sink, CHUNK)[:, None], (N_HEADS * CHUNK, 128)),
        "ws": jnp.concatenate([sgu_ws[h] for h in range(N_HEADS)], axis=1).astype(BF16),
        "bias": jnp.repeat(sgu_bs.T, HEAD_DIM, axis=1),
        "vec": vec,
        "wpool": _block_diag(pool_w).astype(BF16),
    }


def kernel(x_prompt, x_sample, cache_attn_k, cache_attn_v, state_ret, c, c_ctx, w_ada, b_ada, w_in,
           w_out, attn_sink, sgu_norm_w, sgu_norm_b, sgu_ws, sgu_bs, ret_decay, ret_gn_w, pool_w,
           pool_scale, ffn_up, ffn_conv_w, ffn_conv_b, ffn_down, ln_w, ln_b):
    cond8 = jnp.concatenate([c_ctx[None], c, jnp.zeros((8 - 1 - DEC_BATCH, D_MODEL), F32)], axis=0)
    mod = _modulation(cond8, w_ada, b_ada).reshape(DEPTH, 8, 6, D_MODEL)

    tabs = jax.vmap(_layer_tables)(attn_sink, sgu_norm_w, sgu_norm_b, sgu_ws, sgu_bs, ret_decay, ret_gn_w,
                                   pool_w, pool_scale)
    tabs["gmat"] = _block_diag(jnp.full((N_HEADS, HEAD_DIM, HEAD_DIM), 1.0 / HEAD_DIM, F32)).astype(BF16)
    tabs["cnt_ctx"] = _inv_count(SEQ)
    tabs["cnt_lat"] = _inv_count(DEC_SEQ)
    cos, sin = _rope_tables()
    extra = {"ck": cache_attn_k.reshape(DEC_BATCH, DEPTH, PAST_LEN, 128),
             "cv": cache_attn_v.reshape(DEC_BATCH, DEPTH, PAST_LEN, 128),
             "cos": cos, "sin": sin, "s0": state_ret}
    ln8 = jnp.concatenate([ln_w, ln_b, jnp.zeros((DEPTH, 4, D_MODEL), F32)], axis=1)
    conv8 = jnp.concatenate([ffn_conv_w, ffn_conv_b[:, None], jnp.zeros((DEPTH, 4, 2 * D_FF), F32)], axis=1)

    down_bf16 = _to_bf16(ffn_down, D_FF // 4)
    w_in_bf16 = _to_bf16(w_in, D_MODEL // 2)
    w_out_bf16 = _to_bf16(w_out, D_MODEL)
    up_chunks = _up_weight_chunks(ffn_up)

    xs = [x_prompt.reshape(N_CTX, D_MODEL), x_sample.reshape(N_LAT, D_MODEL)]
    new_k, new_v, new_s = [], [], []
    for l in range(DEPTH):
        z = _inproj(xs, mod, w_in_bf16, l)
        x1_ctx, st, kn, vn = _mixer(z, xs[0], mod, w_out_bf16, ln8, tabs, l, latent=False)
        (x1_lat,) = _mixer(z, xs[1], mod, w_out_bf16, ln8, tabs, l, latent=True, extra=extra)
        xs = _ffn([x1_ctx, x1_lat], mod, up_chunks, conv8, down_bf16, ln8, l, split_out=True)
        new_k.append(kn.reshape(BATCH, SEQ, 2, HEAD_DIM))
        new_v.append(vn.reshape(BATCH, SEQ, 2, HEAD_DIM))
        new_s.append(st)

    y_prompt = xs[0].reshape(BATCH, SEQ, D_MODEL)
    y_sample = xs[1].reshape(DEC_BATCH, DEC_SEQ, D_MODEL)
    return (y_prompt, y_sample, jnp.stack(new_k, axis=1), jnp.stack(new_v, axis=1),
            jnp.stack(new_s, axis=1))
```

```python
import functools

import numpy as np
import jax
import jax.numpy as jnp
from jax import lax
from jax.experimental import pallas as pl
from jax.experimental.pallas import tpu as pltpu

F32 = jnp.float32
BF16 = jnp.bfloat16

D_MODEL = 1024
BATCH = 16
SEQ = 256
DEPTH = 2
DEC_BATCH = 2
DEC_SEQ = 2048
PAST_LEN = 256
GRID_W = 64
CHUNK = 128
HEAD_DIM = 64
GW = D_MODEL // 4
N_HEADS = 4
POOL_WINDOWS = (2, 4, 8, 16)
POOL_HALO = 8
D_FF = 2816
ROPE_BASE = 10000.0
LN_EPS = 1e-5
NEG_INF = -1e30
IN_WIDTH = 10 * GW
ALPHA = (2.0 * DEPTH) ** 0.25

N_CTX = BATCH * SEQ
N_LAT = DEC_BATCH * DEC_SEQ
ROWS = N_CTX + N_LAT

TM = 1024
NB_IN = 512
FC = 256
NB_ADA = 1536
VMEM_LIMIT = 56 * 1024 * 1024


def _cond_of_tile(i, tm=TM):
    ctx_tiles = N_CTX // tm
    return jnp.where(i < ctx_tiles, 0, 1 + (i - ctx_tiles) // (DEC_SEQ // tm))


def _tile_specs(n_src, tm, grid_rank):
    ctx_tiles = N_CTX // tm
    if n_src == 1:
        rows = [lambda i: i]
    else:
        rows = [lambda i: jnp.minimum(i, ctx_tiles - 1), lambda i: jnp.maximum(i - ctx_tiles, 0)]
    if grid_rank == 1:
        return [pl.BlockSpec((tm, D_MODEL), lambda i, f=f: (f(i), 0)) for f in rows]
    return [pl.BlockSpec((tm, D_MODEL), lambda i, j, f=f: (f(i), 0)) for f in rows]


def _per_half(i, tm, n_max, fn):
    if n_max == 1:
        fn(0)
        return
    ctx_tiles = N_CTX // tm

    @pl.when(i < ctx_tiles)
    def _():
        fn(0)

    @pl.when(i >= ctx_tiles)
    def _():
        fn(1)


def _layer_norm(x, w, b):
    mu = jnp.mean(x, axis=-1, keepdims=True)
    d = x - mu
    var = jnp.mean(d * d, axis=-1, keepdims=True)
    return d * lax.rsqrt(var + LN_EPS) * w + b


def _silu(x):
    return x * jax.nn.sigmoid(x)


def _bdot(a, b):
    return jnp.dot(a.astype(BF16), b.astype(BF16), preferred_element_type=F32)


def _mod_kernel(c_ref, w_ref, b_ref, o_ref):
    o_ref[...] = _bdot(_silu(c_ref[...]), w_ref[...]) + b_ref[...]


def _modulation(cond8, w_ada, b_ada):
    return pl.pallas_call(
        _mod_kernel,
        grid=(DEPTH, 6 * D_MODEL // NB_ADA),
        in_specs=[
            pl.BlockSpec((8, D_MODEL), lambda l, j: (0, 0)),
            pl.BlockSpec((None, D_MODEL, NB_ADA), lambda l, j: (l, 0, j)),
            pl.BlockSpec((None, 1, NB_ADA), lambda l, j: (l, 0, j)),
        ],
        out_specs=pl.BlockSpec((None, 8, NB_ADA), lambda l, j: (l, 0, j)),
        out_shape=jax.ShapeDtypeStruct((DEPTH, 8, 6 * D_MODEL), F32),
        compiler_params=pltpu.CompilerParams(
            dimension_semantics=("arbitrary", "arbitrary"), vmem_limit_bytes=VMEM_LIMIT),
        name="modulation",
    )(cond8, w_ada, b_ada.reshape(DEPTH, 1, 6 * D_MODEL))


def _cast_kernel(w_ref, o_ref):
    o_ref[...] = w_ref[...].astype(BF16)


def _to_bf16(w, block_rows):
    depth, rows, cols = w.shape
    return pl.pallas_call(
        _cast_kernel,
        grid=(depth, rows // block_rows),
        in_specs=[pl.BlockSpec((None, block_rows, cols), lambda l, r: (l, r, 0))],
        out_specs=pl.BlockSpec((None, block_rows, cols), lambda l, r: (l, r, 0)),
        out_shape=jax.ShapeDtypeStruct(w.shape, BF16),
        compiler_params=pltpu.CompilerParams(dimension_semantics=("arbitrary", "arbitrary")),
        name="cast_bf16",
    )(w)


def _inproj_kernel(*refs, n_x):
    x_refs = refs[:n_x]
    mod_ref, w_ref, z_ref, h_scr = refs[n_x:]

    m = mod_ref[...]

    def build(side):
        h_scr[...] = (x_refs[side][...] * (1.0 + m[1:2]) + m[0:1]).astype(BF16)

    _per_half(pl.program_id(0), TM, n_x, build)
    h = h_scr[...]
    for jb in range(IN_WIDTH // NB_IN):
        cols = slice(jb * NB_IN, (jb + 1) * NB_IN)
        z_ref[:, cols] = jnp.dot(h, w_ref[:, cols], preferred_element_type=F32)


def _inproj(xs, mod, w_in_bf16, l):
    return pl.pallas_call(
        functools.partial(_inproj_kernel, n_x=len(xs)),
        grid=(ROWS // TM,),
        in_specs=_tile_specs(len(xs), TM, 1) + [
            pl.BlockSpec((None, None, 6, D_MODEL), lambda i: (l, _cond_of_tile(i), 0, 0)),
            pl.BlockSpec((None, D_MODEL, IN_WIDTH), lambda i: (l, 0, 0), pipeline_mode=pl.Buffered(1)),
        ],
        out_specs=pl.BlockSpec((TM, IN_WIDTH), lambda i: (i, 0)),
        out_shape=jax.ShapeDtypeStruct((ROWS, IN_WIDTH), F32),
        scratch_shapes=[pltpu.VMEM((TM, D_MODEL), BF16)],
        compiler_params=pltpu.CompilerParams(
            dimension_semantics=("arbitrary",), vmem_limit_bytes=VMEM_LIMIT),
        name="inproj",
    )(*xs, mod, w_in_bf16)


TMF = 512
SEGF = TMF // 8
PITCHF = SEGF + 8
CTXF_TILES = N_CTX // TMF
LATF_PER_SEQ = DEC_SEQ // TMF
HALO_ROWS = 16
RBUF = 256
RB = 256
GB = 64
NFC = D_FF // FC
LANE_BLOCKS = D_MODEL // 128
assert RB == RBUF


def _up_weight_kernel(a_ref, g_ref, o_ref):
    for c in range(NFC):
        o_ref[c, :, 0:FC] = a_ref[:, c * FC:(c + 1) * FC].astype(BF16)
        o_ref[c, :, FC:2 * FC] = g_ref[:, c * FC:(c + 1) * FC].astype(BF16)


def _up_weight_chunks(ffn_up):
    half = D_MODEL // 2
    return pl.pallas_call(
        _up_weight_kernel,
        grid=(DEPTH, 2),
        in_specs=[pl.BlockSpec((None, half, D_FF), lambda l, r: (l, r, 0)),
                  pl.BlockSpec((None, half, D_FF), lambda l, r: (l, r, 1))],
        out_specs=pl.BlockSpec((None, NFC, half, 2 * FC), lambda l, r: (l, 0, r, 0)),
        out_shape=jax.ShapeDtypeStruct((DEPTH, NFC, D_MODEL, 2 * FC), BF16),
        compiler_params=pltpu.CompilerParams(
            dimension_semantics=("arbitrary", "arbitrary"), vmem_limit_bytes=VMEM_LIMIT),
        name="cast_up_chunks",
    )(ffn_up, ffn_up)


def _seg_rows_f(xc_ref, k):
    return jnp.concatenate([xc_ref[cb, pl.ds(k, 8, stride=PITCHF), :] for cb in range(LANE_BLOCKS)], axis=1)


def _ffn_kernel(*refs, n_out):
    x_refs = refs[:2]
    (xp_ref, xn_ref, mod_ref, up_ref, cv_ref, dn_ref, ln_ref) = refs[2:9]
    o_refs = refs[9:9 + n_out]
    h_scr, act_scr, xc_scr, u0_scr, u1_scr = refs[9 + n_out:]
    u_scrs = (u0_scr, u1_scr)
    i = pl.program_id(0)
    is_ctx = i < CTXF_TILES
    lat_pos = (i - CTXF_TILES) % LATF_PER_SEQ
    m = mod_ref[...]
    ln = ln_ref[...]

    scale = 1.0 + m[4:5]
    shift = m[3:4]

    def stage(side):
        for cb in range(LANE_BLOCKS):
            for s in range(8):
                xc_scr[cb, s * PITCHF:s * PITCHF + SEGF, :] = x_refs[side][s * SEGF:(s + 1) * SEGF,
                                                                            cb * 128:(cb + 1) * 128]

    _per_half(i, TMF, 2, stage)

    def build_h(ub):
        for k in range(ub * RBUF // 8, (ub + 1) * RBUF // 8, 2):
            rows = jnp.concatenate([_seg_rows_f(xc_scr, k), _seg_rows_f(xc_scr, k + 1)], axis=0)
            h_scr[8 * k:8 * k + 16, :] = (rows * scale + shift).astype(BF16)

    sub16 = lax.broadcasted_iota(jnp.int32, (HALO_ROWS, D_MODEL), 0)
    prev_ok = jnp.logical_not(is_ctx) & (lat_pos > 0)
    next_ok = jnp.logical_not(is_ctx) & (lat_pos < LATF_PER_SEQ - 1)
    halo_x = jnp.where(sub16 == 0, xp_ref[POOL_HALO - 1:POOL_HALO, :], xn_ref[0:1, :])
    keep = ((sub16 == 0) & prev_ok) | ((sub16 == 1) & next_ok)
    h_scr[TMF:TMF + HALO_ROWS, :] = jnp.where(keep, halo_x * scale + shift, 0.0).astype(BF16)

    n_ub = TMF // RBUF
    sub = lax.broadcasted_iota(jnp.int32, (8, FC), 0)
    seg_per_seq = SEQ // SEGF
    ctx_first = is_ctx & (sub % seg_per_seq == 0)
    ctx_last = is_ctx & (sub % seg_per_seq == seg_per_seq - 1)

    def up_proj(slot, c, ub):
        rows = slice(ub * RBUF, (ub + 1) * RBUF + (HALO_ROWS if ub == n_ub - 1 else 0))
        u_scrs[slot][rows, :] = jnp.dot(h_scr[rows, :], up_ref[c], preferred_element_type=F32)

    def conv(u_ref, lanes, cvs, r0):
        lo = max(r0 - 8, 0)
        hi = min(r0 + GB + 8, TMF)
        ue = u_ref[lo:hi, lanes]
        u = ue[r0 - lo:r0 - lo + GB]
        if r0 == 0:
            b_first = jnp.where(sub == 0, u_ref[TMF:TMF + 1, lanes],
                                pltpu.roll(u_ref[TMF - 8:TMF, lanes], 1, axis=0))
            um1 = jnp.concatenate([jnp.where(ctx_first, 0.0, b_first), u[0:GB - 8]], axis=0)
        else:
            um1 = ue[0:GB]
        if r0 == TMF - GB:
            b_last = jnp.where(sub == 7, u_ref[TMF + 1:TMF + 2, lanes],
                               pltpu.roll(u_ref[0:8, lanes], 7, axis=0))
            up1 = jnp.concatenate([u[8:GB], jnp.where(ctx_last, 0.0, b_last)], axis=0)
        else:
            up1 = ue[r0 - lo + 8:r0 - lo + GB + 8]
        return um1 * cvs[0:1] + u * cvs[1:2] + up1 * cvs[2:3] + cvs[3:4]

    def gate(slot, c, ub):
        cva = cv_ref[0:4, c * FC:(c + 1) * FC]
        cvg = cv_ref[0:4, D_FF + c * FC:D_FF + (c + 1) * FC]
        for r0 in range(ub * RBUF, (ub + 1) * RBUF, GB):
            a = conv(u_scrs[slot], slice(0, FC), cva, r0)
            g = conv(u_scrs[slot], slice(FC, 2 * FC), cvg, r0)
            act_scr[c, r0:r0 + GB, :] = (_silu(a) * g).astype(BF16)

    def finish(rb):
        vrows = range(rb * RB // 8, (rb + 1) * RB // 8)
        lhs = jnp.concatenate([act_scr[c, rb * RB:(rb + 1) * RB, :] for c in range(NFC)], axis=1)
        y = jnp.dot(lhs, dn_ref[...], preferred_element_type=F32)
        xr = jnp.concatenate([_seg_rows_f(xc_scr, k) for k in vrows], axis=0)
        out = _layer_norm(ALPHA * xr + m[5:6] * y, ln[1:2], ln[3:4])
        for kk, k in enumerate(vrows):
            for cb in range(LANE_BLOCKS):
                xc_scr[cb, pl.ds(k, 8, stride=PITCHF), :] = out[8 * kk:8 * kk + 8, cb * 128:(cb + 1) * 128]

    def write_out(side):
        for cb in range(LANE_BLOCKS):
            for s in range(8):
                o_refs[side][s * SEGF:(s + 1) * SEGF, cb * 128:(cb + 1) * 128] = xc_scr[
                    cb, s * PITCHF:s * PITCHF + SEGF, :]

    build_h(0)
    for c in range(NFC + 1):
        for ub in range(n_ub):
            if c == 0 and ub > 0:
                build_h(ub)
            if c < NFC:
                up_proj(c % 2, c, ub)
            if c >= 1:
                gate((c - 1) % 2, c - 1, ub)
            if c == NFC:
                finish(ub)

    _per_half(i, TMF, n_out, write_out)


def _ffn(xs, mod, up_chunks, conv8, down_bf16, ln8, l, split_out):
    halo_blocks = TMF // POOL_HALO
    last_halo = N_LAT // POOL_HALO - 1
    n_out = 2 if split_out else 1
    if split_out:
        out_shape = [jax.ShapeDtypeStruct((N_CTX, D_MODEL), F32), jax.ShapeDtypeStruct((N_LAT, D_MODEL), F32)]
    else:
        out_shape = [jax.ShapeDtypeStruct((ROWS, D_MODEL), F32)]
    return pl.pallas_call(
        functools.partial(_ffn_kernel, n_out=n_out),
        grid=(ROWS // TMF,),
        in_specs=[
            *_tile_specs(2, TMF, 1),
            pl.BlockSpec((POOL_HALO, D_MODEL),
                         lambda i: (jnp.maximum((i - CTXF_TILES) * halo_blocks - 1, 0), 0)),
            pl.BlockSpec((POOL_HALO, D_MODEL),
                         lambda i: (jnp.clip((i - CTXF_TILES + 1) * halo_blocks, 0, last_halo), 0)),
            pl.BlockSpec((None, None, 6, D_MODEL), lambda i: (l, _cond_of_tile(i, TMF), 0, 0)),
            pl.BlockSpec((None, NFC, D_MODEL, 2 * FC), lambda i: (l, 0, 0, 0), pipeline_mode=pl.Buffered(1)),
            pl.BlockSpec((None, 8, 2 * D_FF), lambda i: (l, 0, 0)),
            pl.BlockSpec((None, D_FF, D_MODEL), lambda i: (l, 0, 0), pipeline_mode=pl.Buffered(1)),
            pl.BlockSpec((None, 8, D_MODEL), lambda i: (l, 0, 0)),
        ],
        out_specs=_tile_specs(n_out, TMF, 1),
        out_shape=out_shape,
        scratch_shapes=[pltpu.VMEM((TMF + HALO_ROWS, D_MODEL), BF16),
                        pltpu.VMEM((NFC, TMF, FC), BF16),
                        pltpu.VMEM((LANE_BLOCKS, 8 * PITCHF, 128), F32),
                        pltpu.VMEM((TMF + HALO_ROWS, 2 * FC), F32),
                        pltpu.VMEM((TMF + HALO_ROWS, 2 * FC), F32)],
        compiler_params=pltpu.CompilerParams(
            dimension_semantics=("arbitrary",), vmem_limit_bytes=VMEM_LIMIT),
        name="convffn",
    )(xs[0], xs[1], xs[1], xs[1], mod, up_chunks, conv8, down_bf16, ln8)


def _head_masks(width):
    lane = lax.broadcasted_iota(jnp.int32, (1, width), 1)
    return [(lane >= h * HEAD_DIM) & (lane < (h + 1) * HEAD_DIM) for h in range(width // HEAD_DIM)]


def _stack_heads(x, masks):
    return jnp.concatenate([jnp.where(m, x, jnp.zeros_like(x)) for m in masks], axis=0)


def _rope(x, cos, sin):
    lane = lax.broadcasted_iota(jnp.int32, (1, 128), 1)
    lower = (lane & 31) < 16
    outs = []
    for k in range(x.shape[1] // 128):
        xb = x[:, k * 128:(k + 1) * 128]
        partner = jnp.where(lower, pltpu.roll(xb, 112, axis=1), pltpu.roll(xb, 16, axis=1))
        outs.append(xb * cos + partner * sin)
    return outs[0] if len(outs) == 1 else jnp.concatenate(outs, axis=1)


def _load_state(state_scr, blocks_ref, d):
    state_scr[...] = jnp.zeros((GW, GW), F32)
    for h in range(N_HEADS):
        sl = slice(h * HEAD_DIM, (h + 1) * HEAD_DIM)
        state_scr[sl, sl] = blocks_ref[d, h]


def _store_state(blocks_ref, d, state_scr):
    for h in range(N_HEADS):
        sl = slice(h * HEAD_DIM, (h + 1) * HEAD_DIM)
        blocks_ref[d, h] = state_scr[sl, sl]


LAT_GROUP = 4


def _chunk_off(c):
    return c * CHUNK if isinstance(c, int) else pl.multiple_of(c * CHUNK, CHUNK)


def _interleave(chains):
    chains = list(chains)
    while chains:
        for ch in list(chains):
            try:
                next(ch)
            except StopIteration:
                chains.remove(ch)


def _mixer_kernel(*refs, latent, group, ng):
    if latent:
        (qa_ref, ka_ref, va_ref, kx_ref, vx_ref, ub_ref, vb_ref, qc_ref, kc_ref, vc_ref, gf_ref, gb_ref,
         pd_ref, pdp_ref, pdn_ref, cos_ref, sin_ref, s0_ref, dmat_ref, dec_ref, sink_ref, ws_ref, bias_ref,
         vec_ref, gmat_ref, cnt_ref, wpool_ref, xres_ref, mod_ref, wout_ref, ln_ref,
         x1_ref, sf_scr, sb_scr, ob_scr, pext_scr, mix_ref, k_scr, v_scr) = refs
    else:
        (qa_ref, ka_ref, va_ref, ub_ref, vb_ref, qc_ref, kc_ref, vc_ref, gf_ref, gb_ref, pd_ref,
         dmat_ref, dec_ref, sink_ref, ws_ref, bias_ref, vec_ref, gmat_ref, cnt_ref, wpool_ref,
         xres_ref, mod_ref, wout_ref, ln_ref,
         x1_ref, st_ref, kn_ref, vn_ref, sf_scr, sb_scr, ob_scr, pext_scr, mix_ref) = refs

    nc = group * ng
    g_a = mod_ref[...][2:3]
    ln = ln_ref[...]
    masks = _head_masks(GW)
    row = lax.broadcasted_iota(jnp.int32, (GW, GW), 0)
    col = lax.broadcasted_iota(jnp.int32, (GW, GW), 1)
    bd_mask = (row // HEAD_DIM) == (col // HEAD_DIM)
    vec = vec_ref[...]
    lane = lax.broadcasted_iota(jnp.int32, (1, 2 * HEAD_DIM), 1)
    lo = lane < HEAD_DIM

    def rows(k):
        return slice(k * CHUNK, (k + 1) * CHUNK)

    def retention(order, d, state_scr, out):
        dmat = dmat_ref[d]
        qdec, kdec = dec_ref[d], dec_ref[2 + d]
        cdec = vec[5 + d:6 + d]
        state = state_scr[...]
        for k in order:
            q, kk, v = qc_ref[rows(k), :], kc_ref[rows(k), :], vc_ref[rows(k), :]
            qb, kb, vb = q.astype(BF16), kk.astype(BF16), v.astype(BF16)
            s = lax.dot_general(_stack_heads(qb, masks), kb, (((1,), (1,)), ((), ())),
                                preferred_element_type=F32)
            yield
            p = (s * dmat).astype(BF16)
            p_cat = jnp.concatenate([p[h * CHUNK:(h + 1) * CHUNK] for h in range(N_HEADS)], axis=1)
            o = (jnp.dot(p_cat, _stack_heads(vb, masks), preferred_element_type=F32)
                 + _bdot(q * qdec, state))
            yield
            upd = lax.dot_general((kk * kdec).astype(BF16), vb, (((0,), (0,)), ((), ())),
                                  preferred_element_type=F32)
            state = state * cdec + jnp.where(bd_mask, upd, 0.0)
            out[k] = o
            yield
        state_scr[...] = state

    def stage_kv(k, c):
        dst = pl.ds(_chunk_off(c + 1), CHUNK)
        k_scr[dst, :] = _rope(ka_ref[rows(k), :], cos_ref[rows(k), :], sin_ref[rows(k), :]).astype(BF16)
        yield
        v_scr[dst, :] = va_ref[rows(k), :].astype(BF16)
        yield

    def attention(k, c, done):
        q = qa_ref[rows(k), :]
        if latent:
            q = _rope(q, cos_ref[rows(k), :], sin_ref[rows(k), :])
        q = q * (HEAD_DIM ** -0.5)
        q0, q1 = q[:, :128], q[:, 128:]
        zero = jnp.zeros_like(q0)
        q_st = jnp.concatenate([
            jnp.where(lo, q0, zero),
            jnp.where(lo, pltpu.roll(q0, HEAD_DIM, axis=1), zero),
            jnp.where(lo, zero, pltpu.roll(q1, HEAD_DIM, axis=1)),
            jnp.where(lo, zero, q1)], axis=0).astype(BF16)
        if latent:
            band = pl.ds(_chunk_off(c), 3 * CHUNK)
            k_all = jnp.concatenate([k_scr[band, :], kx_ref[...].astype(BF16)], axis=0)
            v_all = jnp.concatenate([v_scr[band, :], vx_ref[...].astype(BF16)], axis=0)
        else:
            k_all = ka_ref[...].astype(BF16)
            v_all = va_ref[...].astype(BF16)
        s = lax.dot_general(q_st, k_all, (((1,), (1,)), ((), ())), preferred_element_type=F32)
        yield
        if latent:
            nk = 3 * CHUNK + PAST_LEN
            qi = lax.broadcasted_iota(jnp.int32, (N_HEADS * CHUNK, nk), 0) & (CHUNK - 1)
            kj = lax.broadcasted_iota(jnp.int32, (N_HEADS * CHUNK, nk), 1)
            kpos = kj + (c - 1) * CHUNK
            valid = (kj >= 3 * CHUNK) | ((kj >= qi) & (kj <= qi + 2 * CHUNK)
                                         & (kpos >= 0) & (kpos < nc * CHUNK))
            s = jnp.where(valid, s, NEG_INF)
        sink = sink_ref[...][:, 0:1]
        mx = jnp.maximum(jnp.max(s, axis=-1, keepdims=True), sink)
        yield
        e = jnp.exp(s - mx)
        den = jnp.sum(e, axis=-1, keepdims=True) + jnp.exp(sink - mx)
        yield
        o = jnp.dot(e.astype(BF16), v_all, preferred_element_type=F32) / den
        yield
        mix_ref[rows(k), 0:128] = jnp.where(lo, o[0:CHUNK], pltpu.roll(o[CHUNK:2 * CHUNK], HEAD_DIM, axis=1))
        mix_ref[rows(k), 128:256] = jnp.where(lo, pltpu.roll(o[2 * CHUNK:3 * CHUNK], HEAD_DIM, axis=1),
                                              o[3 * CHUNK:4 * CHUNK])
        done.add(("A", k))
        yield

    def gating_unit(k, done):
        vn = _layer_norm(vb_ref[rows(k), :], vec[0:1], vec[1:2]).astype(BF16)
        yield
        sg = jnp.dot(ws_ref[...], _stack_heads(vn, masks), preferred_element_type=F32) + bias_ref[...]
        yield
        mix_ref[rows(k), GW:2 * GW] = ub_ref[rows(k), :] * sg
        done.add(("B", k))
        yield

    def retention_mix(k, o_fwd, o_bwd, done):
        while k not in o_fwd or k not in o_bwd:
            yield
        gmat = gmat_ref[...]
        normed = []
        for o in (o_fwd[k], o_bwd[k]):
            mu = jnp.dot(o.astype(BF16), gmat, preferred_element_type=F32)
            yield
            dlt = o - mu
            var = jnp.dot((dlt * dlt).astype(BF16), gmat, preferred_element_type=F32)
            yield
            normed.append(dlt * lax.rsqrt(var + LN_EPS))
        mix_ref[rows(k), 2 * GW:3 * GW] = (_silu(gf_ref[rows(k), :]) * (normed[0] * vec[3:4])
                                           + _silu(gb_ref[rows(k), :]) * (normed[1] * vec[4:5]))
        done.add(("C", k))
        yield

    def pooling(k, c, done):
        pd = pd_ref[rows(k), :]
        pext = pext_scr.at[k]
        zeros8 = jnp.zeros((POOL_HALO, GW), F32)
        if k > 0:
            prev8 = pd_ref[k * CHUNK - POOL_HALO:k * CHUNK, :]
        else:
            prev8 = jnp.where(c > 0, pdp_ref[...], 0.0) if latent else zeros8
        if k < group - 1:
            next8 = pd_ref[(k + 1) * CHUNK:(k + 1) * CHUNK + POOL_HALO, :]
        else:
            next8 = jnp.where(c < nc - 1, pdn_ref[...], 0.0) if latent else zeros8
        pext[0:POOL_HALO, :] = prev8
        pext[POOL_HALO:POOL_HALO + CHUNK, :] = pd
        pext[POOL_HALO + CHUNK:2 * POOL_HALO + CHUNK, :] = next8
        yield

        def win(d, half):
            return pext[pl.ds(POOL_HALO + d, CHUNK), half * 128:(half + 1) * 128]

        a2 = win(-1, 0) + win(0, 0)
        a4 = a2 + win(-2, 0) + win(1, 0)
        yield
        a8 = win(-4, 1)
        for d in range(-3, 4):
            a8 = a8 + win(d, 1)
        yield
        a16 = a8
        for d in list(range(-8, -4)) + list(range(4, 8)):
            a16 = a16 + win(d, 1)
        yield
        sums = jnp.concatenate([jnp.where(lo, a2, a4), jnp.where(lo, a8, a16)], axis=1)
        yd = sums * cnt_ref[rows(k), :] - pd
        mix_ref[rows(k), 3 * GW:4 * GW] = _bdot(yd, wpool_ref[...]) * vec[2:3]
        done.add(("D", k))
        yield

    def out_projection(k, done):
        while not all((m, k) in done for m in "ABCD"):
            yield
        y = jnp.dot(mix_ref[rows(k), :].astype(BF16), wout_ref[...], preferred_element_type=F32)
        yield
        x1_ref[rows(k), :] = _layer_norm(ALPHA * xres_ref[rows(k), :] + g_a * y, ln[0:1], ln[2:3])
        yield

    def forward_chains(chunk_of, o_bwd):
        done, o_fwd = set(), {}
        chains = [retention(range(group), 0, sf_scr, o_fwd)]
        for k in range(group):
            chains += [attention(k, chunk_of(k), done), gating_unit(k, done), pooling(k, chunk_of(k), done),
                       retention_mix(k, o_fwd, o_bwd, done), out_projection(k, done)]
        return chains

    if not latent:
        sb_scr[...] = jnp.zeros((GW, GW), F32)
        sf_scr[...] = jnp.zeros((GW, GW), F32)
        kn_ref[...] = ka_ref[...]
        vn_ref[...] = va_ref[...]
        o_bwd = {}
        _interleave([retention(reversed(range(group)), 1, sb_scr, o_bwd)] + forward_chains(lambda k: k, o_bwd))
        _store_state(st_ref, 1, sb_scr)
        _store_state(st_ref, 0, sf_scr)
        return

    p = pl.program_id(1)
    g = pl.program_id(2)

    @pl.when(p == 0)
    def _():
        @pl.when(g == 0)
        def _():
            _load_state(sb_scr, s0_ref, 1)
            zero_blk = jnp.zeros((CHUNK, 2 * HEAD_DIM), BF16)
            for scr in (k_scr, v_scr):
                scr[0:CHUNK, :] = zero_blk
                scr[(nc + 1) * CHUNK:(nc + 2) * CHUNK, :] = zero_blk

        first = (ng - 1 - g) * group
        o_bwd = {}
        _interleave([retention(reversed(range(group)), 1, sb_scr, o_bwd)]
                    + [stage_kv(k, first + k) for k in range(group)])
        for k in range(group):
            ob_scr[pl.ds(_chunk_off(first + k), CHUNK), :] = o_bwd[k]

    @pl.when(p == 1)
    def _():
        @pl.when(g == 0)
        def _():
            _load_state(sf_scr, s0_ref, 0)

        first = g * group
        o_bwd = {k: ob_scr[pl.ds(_chunk_off(first + k), CHUNK), :] for k in range(group)}
        _interleave(forward_chains(lambda k: first + k, o_bwd))


def _mixer(z, x_res, mod, w_out_bf16, ln8, tabs, l, latent, extra=None):
    nb = DEC_BATCH if latent else BATCH
    nc = (DEC_SEQ if latent else SEQ) // CHUNK
    group = LAT_GROUP if latent else nc
    ng = nc // group
    blk = group * CHUNK
    base = (N_CTX // blk) if latent else 0
    per8 = blk // POOL_HALO
    last_halo = ROWS // POOL_HALO - 1

    def on_grid(f):
        return (lambda b, p, g: f(b, p, g)) if latent else (lambda b: f(b, 1, 0))

    def fwd(b, p, g):
        return base + b * ng + g * p

    def both(b, p, g):
        return base + b * ng + jnp.where(p == 0, ng - 1 - g, g)

    def bwd_only(b, p, g):
        return base + b * ng + (ng - 1 - g) * (1 - p)

    def col(width, idx, rowmap):
        return pl.BlockSpec((blk, width), on_grid(lambda b, p, g: (rowmap(b, p, g), idx)))

    def const(shape):
        return pl.BlockSpec(shape, on_grid(lambda b, p, g: (0,) * len(shape)))

    def layer(shape):
        return pl.BlockSpec((None,) + shape, on_grid(lambda b, p, g: (l,) + (0,) * len(shape)))

    specs, args = [], []

    def add(spec, arr):
        specs.append(spec)
        args.append(arr)

    add(col(GW, 0, fwd), z)
    add(col(128, 2, bwd_only if latent else fwd), z)
    add(col(128, 3, bwd_only if latent else fwd), z)
    if latent:
        add(pl.BlockSpec((None, None, PAST_LEN, 128), lambda b, p, g: (b, l, 0, 0)), extra["ck"])
        add(pl.BlockSpec((None, None, PAST_LEN, 128), lambda b, p, g: (b, l, 0, 0)), extra["cv"])
    add(col(GW, 2, fwd), z)
    add(col(GW, 3, fwd), z)
    add(col(GW, 4, both), z)
    add(col(GW, 5, both), z)
    add(col(GW, 6, both), z)
    add(col(GW, 7, fwd), z)
    add(col(GW, 8, fwd), z)
    add(col(GW, 9, fwd), z)
    if latent:
        add(pl.BlockSpec((POOL_HALO, GW),
                         lambda b, p, g: (jnp.maximum(fwd(b, p, g) * per8 - 1, 0), 9)), z)
        add(pl.BlockSpec((POOL_HALO, GW),
                         lambda b, p, g: (jnp.minimum((fwd(b, p, g) + 1) * per8, last_halo), 9)), z)
        rope_map = lambda b, p, g: (jnp.where(p == 0, ng - 1 - g, g), 0)
        add(pl.BlockSpec((blk, 128), rope_map), extra["cos"])
        add(pl.BlockSpec((blk, 128), rope_map), extra["sin"])
        add(pl.BlockSpec((None, None, 2, N_HEADS, HEAD_DIM, HEAD_DIM),
                         lambda b, p, g: (b, l, 0, 0, 0, 0)), extra["s0"])
    add(layer((2, N_HEADS * CHUNK, CHUNK)), tabs["dmat"])
    add(layer((4, CHUNK, GW)), tabs["dec"])
    add(layer((N_HEADS * CHUNK, 128)), tabs["sink"])
    add(layer((CHUNK, N_HEADS * CHUNK)), tabs["ws"])
    add(layer((CHUNK, GW)), tabs["bias"])
    add(layer((8, GW)), tabs["vec"])
    add(const((GW, GW)), tabs["gmat"])
    add(pl.BlockSpec((blk, GW), on_grid(lambda b, p, g: (g * p, 0))),
        tabs["cnt_lat"] if latent else tabs["cnt_ctx"])
    add(layer((GW, GW)), tabs["wpool"])
    local = lambda b, p, g: (b * ng + g * p, 0)
    add(pl.BlockSpec((blk, D_MODEL), on_grid(local)), x_res)
    add(pl.BlockSpec((None, None, 6, D_MODEL),
                     on_grid(lambda b, p, g: (l, (1 + b) if latent else 0, 0, 0))), mod)
    add(layer((D_MODEL, D_MODEL)), w_out_bf16)
    add(layer((8, D_MODEL)), ln8)

    out_shape = [jax.ShapeDtypeStruct((nb * nc * CHUNK, D_MODEL), F32)]
    out_specs = [pl.BlockSpec((blk, D_MODEL), on_grid(local))]
    scratch = [pltpu.VMEM((GW, GW), F32), pltpu.VMEM((GW, GW), F32),
               pltpu.VMEM((nc * CHUNK, GW), F32),
               pltpu.VMEM((group, CHUNK + 2 * POOL_HALO, GW), F32),
               pltpu.VMEM((blk, D_MODEL), F32)]
    if latent:
        scratch += [pltpu.VMEM(((nc + 2) * CHUNK, 128), BF16), pltpu.VMEM(((nc + 2) * CHUNK, 128), BF16)]
    else:
        out_shape.append(jax.ShapeDtypeStruct((nb, 2, N_HEADS, HEAD_DIM, HEAD_DIM), F32))
        out_specs.append(pl.BlockSpec((None, 2, N_HEADS, HEAD_DIM, HEAD_DIM), lambda b: (b, 0, 0, 0, 0)))
        for _ in range(2):
            out_shape.append(jax.ShapeDtypeStruct((nb, SEQ, 128), F32))
            out_specs.append(pl.BlockSpec((None, SEQ, 128), lambda b: (b, 0, 0)))

    return pl.pallas_call(
        functools.partial(_mixer_kernel, latent=latent, group=group, ng=ng),
        grid=(nb, 2, ng) if latent else (nb,),
        in_specs=specs,
        out_specs=out_specs,
        out_shape=out_shape,
        scratch_shapes=scratch,
        compiler_params=pltpu.CompilerParams(
            dimension_semantics=("arbitrary",) * (3 if latent else 1), vmem_limit_bytes=VMEM_LIMIT),
        name="mixer_latent" if latent else "mixer_context",
    )(*args)


def _pad_rows(rows, n=8):
    a = jnp.stack(rows)
    return jnp.concatenate([a, jnp.zeros((n - a.shape[0],) + a.shape[1:], a.dtype)], axis=0)


def _block_diag(blocks):
    g, n, _ = blocks.shape
    eye = jnp.eye(g, dtype=blocks.dtype)
    return (eye[:, None, :, None] * blocks[:, :, None, :]).reshape(g * n, g * n)


def _inv_count(n):
    t = np.arange(n)
    cols = []
    for w in POOL_WINDOWS:
        cnt = np.clip(t + w // 2, 0, n) - np.clip(t - w // 2, 0, n)
        cols.append(np.repeat((1.0 / cnt)[:, None], HEAD_DIM, axis=1))
    return jnp.asarray(np.concatenate(cols, axis=1), F32)


def _rope_tables():
    rows = DEC_SEQ // GRID_W
    r, cc = jnp.meshgrid(jnp.arange(rows), jnp.arange(GRID_W), indexing="ij")
    half = HEAD_DIM // 2
    freqs = ROPE_BASE ** (-jnp.arange(0, half, 2, dtype=F32) / half)

    def tables(pos):
        ang = pos.reshape(-1).astype(F32)[:, None] * freqs[None, :]
        cos, sin = jnp.cos(ang), jnp.sin(ang)
        return jnp.concatenate([cos, cos], axis=1), jnp.concatenate([-sin, sin], axis=1)

    cr, sr = tables(r)
    ccol, scol = tables(cc)
    cos = jnp.concatenate([cr, ccol], axis=1)
    sin = jnp.concatenate([sr, scol], axis=1)
    return jnp.tile(cos, (1, 2)), jnp.tile(sin, (1, 2))


def _layer_tables(attn_sink, sgu_norm_w, sgu_norm_b, sgu_ws, sgu_bs, ret_decay, ret_gn_w, pool_w, pool_scale):
    log_g = jax.nn.log_sigmoid(ret_decay.astype(F32))
    i = jnp.arange(CHUNK, dtype=F32)
    rel = i[:, None] - i[None, :]
    kscale = HEAD_DIM ** -0.5
    d_f = jnp.where(rel >= 0, jnp.exp(jnp.maximum(rel, 0.0)[None] * log_g[0][:, None, None]), 0.0)
    d_b = jnp.where(rel <= 0, jnp.exp(jnp.maximum(-rel, 0.0)[None] * log_g[1][:, None, None]), 0.0)
    dmat = jnp.stack([d_f.reshape(N_HEADS * CHUNK, CHUNK), d_b.reshape(N_HEADS * CHUNK, CHUNK)]) * kscale

    def lanes(per_head):
        return jnp.repeat(per_head, HEAD_DIM, axis=1)

    qdec_f = lanes(jnp.exp((i + 1.0)[:, None] * log_g[0][None, :]))
    qdec_b = lanes(jnp.exp((CHUNK - i)[:, None] * log_g[1][None, :]))
    kdec_f = lanes(jnp.exp((CHUNK - 1.0 - i)[:, None] * log_g[0][None, :])) * kscale
    kdec_b = lanes(jnp.exp(i[:, None] * log_g[1][None, :])) * kscale
    cdec = jnp.repeat(jnp.exp(CHUNK * log_g), HEAD_DIM, axis=1)
    vec = _pad_rows([sgu_norm_w, sgu_norm_b, pool_scale, ret_gn_w[0], ret_gn_w[1], cdec[0], cdec[1]])
    return {
        "dmat": dmat,
        "dec": jnp.stack([qdec_f, qdec_b, kdec_f, kdec_b]),
        "sink": jnp.broadcast_to(jnp.repeat(attn_sink, CHUNK)[:, None], (N_HEADS * CHUNK, 128)),
        "ws": jnp.concatenate([sgu_ws[h] for h in range(N_HEADS)], axis=1).astype(BF16),
        "bias": jnp.repeat(sgu_bs.T, HEAD_DIM, axis=1),
        "vec": vec,
        "wpool": _block_diag(pool_w).astype(BF16),
    }


def kernel(x_prompt, x_sample, cache_attn_k, cache_attn_v, state_ret, c, c_ctx, w_ada, b_ada, w_in,
           w_out, attn_sink, sgu_norm_w, sgu_norm_b, sgu_ws, sgu_bs, ret_decay, ret_gn_w, pool_w,
           pool_scale, ffn_up, ffn_conv_w, ffn_conv_b, ffn_down, ln_w, ln_b):
    cond8 = jnp.concatenate([c_ctx[None], c, jnp.zeros((8 - 1 - DEC_BATCH, D_MODEL), F32)], axis=0)
    mod = _modulation(cond8, w_ada, b_ada).reshape(DEPTH, 8, 6, D_MODEL)

    tabs = jax.vmap(_layer_tables)(attn_sink, sgu_norm_w, sgu_norm_b, sgu_ws, sgu_bs, ret_decay, ret_gn_w,
                                   pool_w, pool_scale)
    tabs["gmat"] = _block_diag(jnp.full((N_HEADS, HEAD_DIM, HEAD_DIM), 1.0 / HEAD_DIM, F32)).astype(BF16)
    tabs["cnt_ctx"] = _inv_count(SEQ)
    tabs["cnt_lat"] = _inv_count(DEC_SEQ)
    cos, sin = _rope_tables()
    extra = {"ck": cache_attn_k.reshape(DEC_BATCH, DEPTH, PAST_LEN, 128),
             "cv": cache_attn_v.reshape(DEC_BATCH, DEPTH, PAST_LEN, 128),
             "cos": cos, "sin": sin, "s0": state_ret}
    ln8 = jnp.concatenate([ln_w, ln_b, jnp.zeros((DEPTH, 4, D_MODEL), F32)], axis=1)
    conv8 = jnp.concatenate([ffn_conv_w, ffn_conv_b[:, None], jnp.zeros((DEPTH, 4, 2 * D_FF), F32)], axis=1)

    down_bf16 = _to_bf16(ffn_down, D_FF // 4)
    w_in_bf16 = _to_bf16(w_in, D_MODEL // 2)
    w_out_bf16 = _to_bf16(w_out, D_MODEL)
    up_chunks = _up_weight_chunks(ffn_up)

    xs = [x_prompt.reshape(N_CTX, D_MODEL), x_sample.reshape(N_LAT, D_MODEL)]
    new_k, new_v, new_s = [], [], []
    for l in range(DEPTH):
        z = _inproj(xs, mod, w_in_bf16, l)
        x1_ctx, st, kn, vn = _mixer(z, xs[0], mod, w_out_bf16, ln8, tabs, l, latent=False)
        (x1_lat,) = _mixer(z, xs[1], mod, w_out_bf16, ln8, tabs, l, latent=True, extra=extra)
        xs = _ffn([x1_ctx, x1_lat], mod, up_chunks, conv8, down_bf16, ln8, l, split_out=True)
        new_k.append(kn.reshape(BATCH, SEQ, 2, HEAD_DIM))
        new_v.append(vn.reshape(BATCH, SEQ, 2, HEAD_DIM))
        new_s.append(st)

    y_prompt = xs[0].reshape(BATCH, SEQ, D_MODEL)
    y_sample = xs[1].reshape(DEC_BATCH, DEC_SEQ, D_MODEL)
    return (y_prompt, y_sample, jnp.stack(new_k, axis=1), jnp.stack(new_v, axis=1),
            jnp.stack(new_s, axis=1))
```

```python
import functools

import numpy as np
import jax
import jax.numpy as jnp
from jax import lax
from jax.experimental import pallas as pl
from jax.experimental.pallas import tpu as pltpu

F32 = jnp.float32
BF16 = jnp.bfloat16

D_MODEL = 1024
BATCH = 16
SEQ = 256
DEPTH = 2
DEC_BATCH = 2
DEC_SEQ = 2048
PAST_LEN = 256
GRID_W = 64
CHUNK = 128
HEAD_DIM = 64
GW = D_MODEL // 4
N_HEADS = 4
POOL_WINDOWS = (2, 4, 8, 16)
POOL_HALO = 8
D_FF = 2816
ROPE_BASE = 10000.0
LN_EPS = 1e-5
NEG_INF = -1e30
IN_WIDTH = 10 * GW
ALPHA = (2.0 * DEPTH) ** 0.25

N_CTX = BATCH * SEQ
N_LAT = DEC_BATCH * DEC_SEQ
ROWS = N_CTX + N_LAT

TM = 1024
NB_IN = 512
FC = 256
NB_ADA = 1536
VMEM_LIMIT = 56 * 1024 * 1024


def _cond_of_tile(i, tm=TM):
    ctx_tiles = N_CTX // tm
    return jnp.where(i < ctx_tiles, 0, 1 + (i - ctx_tiles) // (DEC_SEQ // tm))


def _tile_specs(n_src, tm, grid_rank):
    ctx_tiles = N_CTX // tm
    if n_src == 1:
        rows = [lambda i: i]
    else:
        rows = [lambda i: jnp.minimum(i, ctx_tiles - 1), lambda i: jnp.maximum(i - ctx_tiles, 0)]
    if grid_rank == 1:
        return [pl.BlockSpec((tm, D_MODEL), lambda i, f=f: (f(i), 0)) for f in rows]
    return [pl.BlockSpec((tm, D_MODEL), lambda i, j, f=f: (f(i), 0)) for f in rows]


def _per_half(i, tm, n_max, fn):
    if n_max == 1:
        fn(0)
        return
    ctx_tiles = N_CTX // tm

    @pl.when(i < ctx_tiles)
    def _():
        fn(0)

    @pl.when(i >= ctx_tiles)
    def _():
        fn(1)


def _layer_norm(x, w, b):
    mu = jnp.mean(x, axis=-1, keepdims=True)
    d = x - mu
    var = jnp.mean(d * d, axis=-1, keepdims=True)
    return d * lax.rsqrt(var + LN_EPS) * w + b


def _silu(x):
    return x * jax.nn.sigmoid(x)


def _bdot(a, b):
    return jnp.dot(a.astype(BF16), b.astype(BF16), preferred_element_type=F32)


def _mod_kernel(c_ref, w_ref, b_ref, o_ref):
    o_ref[...] = _bdot(_silu(c_ref[...]), w_ref[...]) + b_ref[...]


def _modulation(cond8, w_ada, b_ada):
    return pl.pallas_call(
        _mod_kernel,
        grid=(DEPTH, 6 * D_MODEL // NB_ADA),
        in_specs=[
            pl.BlockSpec((8, D_MODEL), lambda l, j: (0, 0)),
            pl.BlockSpec((None, D_MODEL, NB_ADA), lambda l, j: (l, 0, j)),
            pl.BlockSpec((None, 1, NB_ADA), lambda l, j: (l, 0, j)),
        ],
        out_specs=pl.BlockSpec((None, 8, NB_ADA), lambda l, j: (l, 0, j)),
        out_shape=jax.ShapeDtypeStruct((DEPTH, 8, 6 * D_MODEL), F32),
        compiler_params=pltpu.CompilerParams(
            dimension_semantics=("arbitrary", "arbitrary"), vmem_limit_bytes=VMEM_LIMIT),
        name="modulation",
    )(cond8, w_ada, b_ada.reshape(DEPTH, 1, 6 * D_MODEL))


def _cast_kernel(w_ref, o_ref):
    o_ref[...] = w_ref[...].astype(BF16)


def _to_bf16(w, block_rows):
    depth, rows, cols = w.shape
    return pl.pallas_call(
        _cast_kernel,
        grid=(depth, rows // block_rows),
        in_specs=[pl.BlockSpec((None, block_rows, cols), lambda l, r: (l, r, 0))],
        out_specs=pl.BlockSpec((None, block_rows, cols), lambda l, r: (l, r, 0)),
        out_shape=jax.ShapeDtypeStruct(w.shape, BF16),
        compiler_params=pltpu.CompilerParams(dimension_semantics=("arbitrary", "arbitrary")),
        name="cast_bf16",
    )(w)


def _inproj_kernel(*refs, n_x):
    x_refs = refs[:n_x]
    mod_ref, w_ref, z_ref, h_scr = refs[n_x:]

    m = mod_ref[...]

    def build(side):
        h_scr[...] = (x_refs[side][...] * (1.0 + m[1:2]) + m[0:1]).astype(BF16)

    _per_half(pl.program_id(0), TM, n_x, build)
    h = h_scr[...]
    for jb in range(IN_WIDTH // NB_IN):
        cols = slice(jb * NB_IN, (jb + 1) * NB_IN)
        z_ref[:, cols] = jnp.dot(h, w_ref[:, cols].astype(BF16), preferred_element_type=F32)


def _inproj(xs, mod, w_in, l):
    return pl.pallas_call(
        functools.partial(_inproj_kernel, n_x=len(xs)),
        grid=(ROWS // TM,),
        in_specs=_tile_specs(len(xs), TM, 1) + [
            pl.BlockSpec((None, None, 6, D_MODEL), lambda i: (l, _cond_of_tile(i), 0, 0)),
            pl.BlockSpec((None, D_MODEL, IN_WIDTH), lambda i: (l, 0, 0), pipeline_mode=pl.Buffered(1)),
        ],
        out_specs=pl.BlockSpec((TM, IN_WIDTH), lambda i: (i, 0)),
        out_shape=jax.ShapeDtypeStruct((ROWS, IN_WIDTH), F32),
        scratch_shapes=[pltpu.VMEM((TM, D_MODEL), BF16)],
        compiler_params=pltpu.CompilerParams(
            dimension_semantics=("arbitrary",), vmem_limit_bytes=VMEM_LIMIT),
        name="inproj",
    )(*xs, mod, w_in)


TMF = 512
SEGF = TMF // 8
PITCHF = SEGF + 8
CTXF_TILES = N_CTX // TMF
LATF_PER_SEQ = DEC_SEQ // TMF
HALO_ROWS = 16
RBUF = 256
RB = 256
GB = 64
NFC = D_FF // FC
LANE_BLOCKS = D_MODEL // 128
assert RB == RBUF


def _up_weight_kernel(a_ref, g_ref, o_ref):
    for c in range(NFC):
        o_ref[c, :, 0:FC] = a_ref[:, c * FC:(c + 1) * FC].astype(BF16)
        o_ref[c, :, FC:2 * FC] = g_ref[:, c * FC:(c + 1) * FC].astype(BF16)


def _up_weight_chunks(ffn_up):
    half = D_MODEL // 2
    return pl.pallas_call(
        _up_weight_kernel,
        grid=(DEPTH, 2),
        in_specs=[pl.BlockSpec((None, half, D_FF), lambda l, r: (l, r, 0)),
                  pl.BlockSpec((None, half, D_FF), lambda l, r: (l, r, 1))],
        out_specs=pl.BlockSpec((None, NFC, half, 2 * FC), lambda l, r: (l, 0, r, 0)),
        out_shape=jax.ShapeDtypeStruct((DEPTH, NFC, D_MODEL, 2 * FC), BF16),
        compiler_params=pltpu.CompilerParams(
            dimension_semantics=("arbitrary", "arbitrary"), vmem_limit_bytes=VMEM_LIMIT),
        name="cast_up_chunks",
    )(ffn_up, ffn_up)


def _seg_rows_f(xc_ref, k):
    return jnp.concatenate([xc_ref[cb, pl.ds(k, 8, stride=PITCHF), :] for cb in range(LANE_BLOCKS)], axis=1)


def _ffn_kernel(*refs, n_out):
    x_refs = refs[:2]
    (xp_ref, xn_ref, mod_ref, up_ref, cv_ref, dn_ref, ln_ref) = refs[2:9]
    o_refs = refs[9:9 + n_out]
    h_scr, act_scr, xc_scr, u0_scr, u1_scr = refs[9 + n_out:]
    u_scrs = (u0_scr, u1_scr)
    i = pl.program_id(0)
    is_ctx = i < CTXF_TILES
    lat_pos = (i - CTXF_TILES) % LATF_PER_SEQ
    m = mod_ref[...]
    ln = ln_ref[...]

    scale = 1.0 + m[4:5]
    shift = m[3:4]

    def stage(side):
        for cb in range(LANE_BLOCKS):
            for s in range(8):
                xc_scr[cb, s * PITCHF:s * PITCHF + SEGF, :] = x_refs[side][s * SEGF:(s + 1) * SEGF,
                                                                            cb * 128:(cb + 1) * 128]

    _per_half(i, TMF, 2, stage)

    def build_h(ub):
        for k in range(ub * RBUF // 8, (ub + 1) * RBUF // 8, 2):
            rows = jnp.concatenate([_seg_rows_f(xc_scr, k), _seg_rows_f(xc_scr, k + 1)], axis=0)
            h_scr[8 * k:8 * k + 16, :] = (rows * scale + shift).astype(BF16)

    sub16 = lax.broadcasted_iota(jnp.int32, (HALO_ROWS, D_MODEL), 0)
    prev_ok = jnp.logical_not(is_ctx) & (lat_pos > 0)
    next_ok = jnp.logical_not(is_ctx) & (lat_pos < LATF_PER_SEQ - 1)
    halo_x = jnp.where(sub16 == 0, xp_ref[POOL_HALO - 1:POOL_HALO, :], xn_ref[0:1, :])
    keep = ((sub16 == 0) & prev_ok) | ((sub16 == 1) & next_ok)
    h_scr[TMF:TMF + HALO_ROWS, :] = jnp.where(keep, halo_x * scale + shift, 0.0).astype(BF16)

    n_ub = TMF // RBUF
    sub = lax.broadcasted_iota(jnp.int32, (8, FC), 0)
    seg_per_seq = SEQ // SEGF
    ctx_first = is_ctx & (sub % seg_per_seq == 0)
    ctx_last = is_ctx & (sub % seg_per_seq == seg_per_seq - 1)

    def up_proj(slot, c, ub):
        rows = slice(ub * RBUF, (ub + 1) * RBUF + (HALO_ROWS if ub == n_ub - 1 else 0))
        u_scrs[slot][rows, :] = jnp.dot(h_scr[rows, :], up_ref[c], preferred_element_type=F32)

    def conv(u_ref, lanes, cvs, r0):
        lo = max(r0 - 8, 0)
        hi = min(r0 + GB + 8, TMF)
        ue = u_ref[lo:hi, lanes]
        u = ue[r0 - lo:r0 - lo + GB]
        if r0 == 0:
            b_first = jnp.where(sub == 0, u_ref[TMF:TMF + 1, lanes],
                                pltpu.roll(u_ref[TMF - 8:TMF, lanes], 1, axis=0))
            um1 = jnp.concatenate([jnp.where(ctx_first, 0.0, b_first), u[0:GB - 8]], axis=0)
        else:
            um1 = ue[0:GB]
        if r0 == TMF - GB:
            b_last = jnp.where(sub == 7, u_ref[TMF + 1:TMF + 2, lanes],
                               pltpu.roll(u_ref[0:8, lanes], 7, axis=0))
            up1 = jnp.concatenate([u[8:GB], jnp.where(ctx_last, 0.0, b_last)], axis=0)
        else:
            up1 = ue[r0 - lo + 8:r0 - lo + GB + 8]
        return um1 * cvs[0:1] + u * cvs[1:2] + up1 * cvs[2:3] + cvs[3:4]

    def gate(slot, c, ub):
        cva = cv_ref[0:4, c * FC:(c + 1) * FC]
        cvg = cv_ref[0:4, D_FF + c * FC:D_FF + (c + 1) * FC]
        for r0 in range(ub * RBUF, (ub + 1) * RBUF, GB):
            a = conv(u_scrs[slot], slice(0, FC), cva, r0)
            g = conv(u_scrs[slot], slice(FC, 2 * FC), cvg, r0)
            act_scr[c, r0:r0 + GB, :] = (_silu(a) * g).astype(BF16)

    def finish(rb):
        vrows = range(rb * RB // 8, (rb + 1) * RB // 8)
        lhs = jnp.concatenate([act_scr[c, rb * RB:(rb + 1) * RB, :] for c in range(NFC)], axis=1)
        y = jnp.dot(lhs, dn_ref[...].astype(BF16), preferred_element_type=F32)
        xr = jnp.concatenate([_seg_rows_f(xc_scr, k) for k in vrows], axis=0)
        out = _layer_norm(ALPHA * xr + m[5:6] * y, ln[1:2], ln[3:4])
        for kk, k in enumerate(vrows):
            for cb in range(LANE_BLOCKS):
                xc_scr[cb, pl.ds(k, 8, stride=PITCHF), :] = out[8 * kk:8 * kk + 8, cb * 128:(cb + 1) * 128]

    def write_out(side):
        for cb in range(LANE_BLOCKS):
            for s in range(8):
                o_refs[side][s * SEGF:(s + 1) * SEGF, cb * 128:(cb + 1) * 128] = xc_scr[
                    cb, s * PITCHF:s * PITCHF + SEGF, :]

    build_h(0)
    for c in range(NFC + 1):
        for ub in range(n_ub):
            if c == 0 and ub > 0:
                build_h(ub)
            if c < NFC:
                up_proj(c % 2, c, ub)
            if c >= 1:
                gate((c - 1) % 2, c - 1, ub)
            if c == NFC:
                finish(ub)

    _per_half(i, TMF, n_out, write_out)


def _ffn(xs, mod, up_chunks, conv8, down_bf16, ln8, l, split_out):
    halo_blocks = TMF // POOL_HALO
    last_halo = N_LAT // POOL_HALO - 1
    n_out = 2 if split_out else 1
    if split_out:
        out_shape = [jax.ShapeDtypeStruct((N_CTX, D_MODEL), F32), jax.ShapeDtypeStruct((N_LAT, D_MODEL), F32)]
    else:
        out_shape = [jax.ShapeDtypeStruct((ROWS, D_MODEL), F32)]
    return pl.pallas_call(
        functools.partial(_ffn_kernel, n_out=n_out),
        grid=(ROWS // TMF,),
        in_specs=[
            *_tile_specs(2, TMF, 1),
            pl.BlockSpec((POOL_HALO, D_MODEL),
                         lambda i: (jnp.maximum((i - CTXF_TILES) * halo_blocks - 1, 0), 0)),
            pl.BlockSpec((POOL_HALO, D_MODEL),
                         lambda i: (jnp.clip((i - CTXF_TILES + 1) * halo_blocks, 0, last_halo), 0)),
            pl.BlockSpec((None, None, 6, D_MODEL), lambda i: (l, _cond_of_tile(i, TMF), 0, 0)),
            pl.BlockSpec((None, NFC, D_MODEL, 2 * FC), lambda i: (l, 0, 0, 0), pipeline_mode=pl.Buffered(1)),
            pl.BlockSpec((None, 8, 2 * D_FF), lambda i: (l, 0, 0)),
            pl.BlockSpec((None, D_FF, D_MODEL), lambda i: (l, 0, 0), pipeline_mode=pl.Buffered(1)),
            pl.BlockSpec((None, 8, D_MODEL), lambda i: (l, 0, 0)),
        ],
        out_specs=_tile_specs(n_out, TMF, 1),
        out_shape=out_shape,
        scratch_shapes=[pltpu.VMEM((TMF + HALO_ROWS, D_MODEL), BF16),
                        pltpu.VMEM((NFC, TMF, FC), BF16),
                        pltpu.VMEM((LANE_BLOCKS, 8 * PITCHF, 128), F32),
                        pltpu.VMEM((TMF + HALO_ROWS, 2 * FC), F32),
                        pltpu.VMEM((TMF + HALO_ROWS, 2 * FC), F32)],
        compiler_params=pltpu.CompilerParams(
            dimension_semantics=("arbitrary",), vmem_limit_bytes=VMEM_LIMIT),
        name="convffn",
    )(xs[0], xs[1], xs[1], xs[1], mod, up_chunks, conv8, down_bf16, ln8)


def _head_masks(width):
    lane = lax.broadcasted_iota(jnp.int32, (1, width), 1)
    return [(lane >= h * HEAD_DIM) & (lane < (h + 1) * HEAD_DIM) for h in range(width // HEAD_DIM)]


def _stack_heads(x, masks):
    return jnp.concatenate([jnp.where(m, x, jnp.zeros_like(x)) for m in masks], axis=0)


def _rope(x, cos, sin):
    lane = lax.broadcasted_iota(jnp.int32, (1, 128), 1)
    lower = (lane & 31) < 16
    outs = []
    for k in range(x.shape[1] // 128):
        xb = x[:, k * 128:(k + 1) * 128]
        partner = jnp.where(lower, pltpu.roll(xb, 112, axis=1), pltpu.roll(xb, 16, axis=1))
        outs.append(xb * cos + partner * sin)
    return outs[0] if len(outs) == 1 else jnp.concatenate(outs, axis=1)


def _load_state(state_scr, blocks_ref, d):
    state_scr[...] = jnp.zeros((GW, GW), F32)
    for h in range(N_HEADS):
        sl = slice(h * HEAD_DIM, (h + 1) * HEAD_DIM)
        state_scr[sl, sl] = blocks_ref[d, h]


def _store_state(blocks_ref, d, state_scr):
    for h in range(N_HEADS):
        sl = slice(h * HEAD_DIM, (h + 1) * HEAD_DIM)
        blocks_ref[d, h] = state_scr[sl, sl]


LAT_GROUP = 4


def _chunk_off(c):
    return c * CHUNK if isinstance(c, int) else pl.multiple_of(c * CHUNK, CHUNK)


def _interleave(chains):
    chains = list(chains)
    while chains:
        for ch in list(chains):
            try:
                next(ch)
            except StopIteration:
                chains.remove(ch)


def _mixer_kernel(*refs, latent, group, ng):
    if latent:
        (qa_ref, ka_ref, va_ref, kx_ref, vx_ref, ub_ref, vb_ref, qc_ref, kc_ref, vc_ref, gf_ref, gb_ref,
         pd_ref, pdp_ref, pdn_ref, cos_ref, sin_ref, s0_ref, dmat_ref, dec_ref, sink_ref, ws_ref, bias_ref,
         vec_ref, gmat_ref, cnt_ref, wpool_ref, xres_ref, mod_ref, wout_ref, ln_ref,
         x1_ref, sf_scr, sb_scr, ob_scr, pext_scr, mix_ref, k_scr, v_scr) = refs
    else:
        (qa_ref, ka_ref, va_ref, ub_ref, vb_ref, qc_ref, kc_ref, vc_ref, gf_ref, gb_ref, pd_ref,
         dmat_ref, dec_ref, sink_ref, ws_ref, bias_ref, vec_ref, gmat_ref, cnt_ref, wpool_ref,
         xres_ref, mod_ref, wout_ref, ln_ref,
         x1_ref, st_ref, kn_ref, vn_ref, sf_scr, sb_scr, ob_scr, pext_scr, mix_ref) = refs

    nc = group * ng
    g_a = mod_ref[...][2:3]
    ln = ln_ref[...]
    masks = _head_masks(GW)
    row = lax.broadcasted_iota(jnp.int32, (GW, GW), 0)
    col = lax.broadcasted_iota(jnp.int32, (GW, GW), 1)
    bd_mask = (row // HEAD_DIM) == (col // HEAD_DIM)
    vec = vec_ref[...]
    lane = lax.broadcasted_iota(jnp.int32, (1, 2 * HEAD_DIM), 1)
    lo = lane < HEAD_DIM

    def rows(k):
        return slice(k * CHUNK, (k + 1) * CHUNK)

    def retention(order, d, state_scr, out):
        dmat = dmat_ref[d]
        qdec, kdec = dec_ref[d], dec_ref[2 + d]
        cdec = vec[5 + d:6 + d]
        state = state_scr[...]
        for k in order:
            q, kk, v = qc_ref[rows(k), :], kc_ref[rows(k), :], vc_ref[rows(k), :]
            qb, kb, vb = q.astype(BF16), kk.astype(BF16), v.astype(BF16)
            s = lax.dot_general(_stack_heads(qb, masks), kb, (((1,), (1,)), ((), ())),
                                preferred_element_type=F32)
            yield
            p = (s * dmat).astype(BF16)
            p_cat = jnp.concatenate([p[h * CHUNK:(h + 1) * CHUNK] for h in range(N_HEADS)], axis=1)
            o = (jnp.dot(p_cat, _stack_heads(vb, masks), preferred_element_type=F32)
                 + _bdot(q * qdec, state))
            yield
            upd = lax.dot_general((kk * kdec).astype(BF16), vb, (((0,), (0,)), ((), ())),
                                  preferred_element_type=F32)
            state = state * cdec + jnp.where(bd_mask, upd, 0.0)
            out[k] = o
            yield
        state_scr[...] = state

    def stage_kv(k, c):
        dst = pl.ds(_chunk_off(c + 1), CHUNK)
        k_scr[dst, :] = _rope(ka_ref[rows(k), :], cos_ref[rows(k), :], sin_ref[rows(k), :]).astype(BF16)
        yield
        v_scr[dst, :] = va_ref[rows(k), :].astype(BF16)
        yield

    def attention(k, c, done):
        q = qa_ref[rows(k), :]
        if latent:
            q = _rope(q, cos_ref[rows(k), :], sin_ref[rows(k), :])
        q = q * (HEAD_DIM ** -0.5)
        q0, q1 = q[:, :128], q[:, 128:]
        zero = jnp.zeros_like(q0)
        q_st = jnp.concatenate([
            jnp.where(lo, q0, zero),
            jnp.where(lo, pltpu.roll(q0, HEAD_DIM, axis=1), zero),
            jnp.where(lo, zero, pltpu.roll(q1, HEAD_DIM, axis=1)),
            jnp.where(lo, zero, q1)], axis=0).astype(BF16)
        if latent:
            band = pl.ds(_chunk_off(c), 3 * CHUNK)
            k_all = jnp.concatenate([k_scr[band, :], kx_ref[...].astype(BF16)], axis=0)
            v_all = jnp.concatenate([v_scr[band, :], vx_ref[...].astype(BF16)], axis=0)
        else:
            k_all = ka_ref[...].astype(BF16)
            v_all = va_ref[...].astype(BF16)
        s = lax.dot_general(q_st, k_all, (((1,), (1,)), ((), ())), preferred_element_type=F32)
        yield
        if latent:
            nk = 3 * CHUNK + PAST_LEN
            qi = lax.broadcasted_iota(jnp.int32, (N_HEADS * CHUNK, nk), 0) & (CHUNK - 1)
            kj = lax.broadcasted_iota(jnp.int32, (N_HEADS * CHUNK, nk), 1)
            kpos = kj + (c - 1) * CHUNK
            valid = (kj >= 3 * CHUNK) | ((kj >= qi) & (kj <= qi + 2 * CHUNK)
                                         & (kpos >= 0) & (kpos < nc * CHUNK))
            s = jnp.where(valid, s, NEG_INF)
        sink = sink_ref[...][:, 0:1]
        mx = jnp.maximum(jnp.max(s, axis=-1, keepdims=True), sink)
        yield
        e = jnp.exp(s - mx)
        den = jnp.sum(e, axis=-1, keepdims=True) + jnp.exp(sink - mx)
        yield
        o = jnp.dot(e.astype(BF16), v_all, preferred_element_type=F32) / den
        yield
        mix_ref[rows(k), 0:128] = jnp.where(lo, o[0:CHUNK], pltpu.roll(o[CHUNK:2 * CHUNK], HEAD_DIM, axis=1))
        mix_ref[rows(k), 128:256] = jnp.where(lo, pltpu.roll(o[2 * CHUNK:3 * CHUNK], HEAD_DIM, axis=1),
                                              o[3 * CHUNK:4 * CHUNK])
        done.add(("A", k))
        yield

    def gating_unit(k, done):
        vn = _layer_norm(vb_ref[rows(k), :], vec[0:1], vec[1:2]).astype(BF16)
        yield
        sg = jnp.dot(ws_ref[...], _stack_heads(vn, masks), preferred_element_type=F32) + bias_ref[...]
        yield
        mix_ref[rows(k), GW:2 * GW] = ub_ref[rows(k), :] * sg
        done.add(("B", k))
        yield

    def retention_mix(k, o_fwd, o_bwd, done):
        while k not in o_fwd or k not in o_bwd:
            yield
        gmat = gmat_ref[...]
        normed = []
        for o in (o_fwd[k], o_bwd[k]):
            mu = jnp.dot(o.astype(BF16), gmat, preferred_element_type=F32)
            yield
            dlt = o - mu
            var = jnp.dot((dlt * dlt).astype(BF16), gmat, preferred_element_type=F32)
            yield
            normed.append(dlt * lax.rsqrt(var + LN_EPS))
        mix_ref[rows(k), 2 * GW:3 * GW] = (_silu(gf_ref[rows(k), :]) * (normed[0] * vec[3:4])
                                           + _silu(gb_ref[rows(k), :]) * (normed[1] * vec[4:5]))
        done.add(("C", k))
        yield

    def pooling(k, c, done):
        pd = pd_ref[rows(k), :]
        pext = pext_scr.at[k]
        zeros8 = jnp.zeros((POOL_HALO, GW), F32)
        if k > 0:
            prev8 = pd_ref[k * CHUNK - POOL_HALO:k * CHUNK, :]
        else:
            prev8 = jnp.where(c > 0, pdp_ref[...], 0.0) if latent else zeros8
        if k < group - 1:
            next8 = pd_ref[(k + 1) * CHUNK:(k + 1) * CHUNK + POOL_HALO, :]
        else:
            next8 = jnp.where(c < nc - 1, pdn_ref[...], 0.0) if latent else zeros8
        pext[0:POOL_HALO, :] = prev8
        pext[POOL_HALO:POOL_HALO + CHUNK, :] = pd
        pext[POOL_HALO + CHUNK:2 * POOL_HALO + CHUNK, :] = next8
        yield

        def win(d, half):
            return pext[pl.ds(POOL_HALO + d, CHUNK), half * 128:(half + 1) * 128]

        a2 = win(-1, 0) + win(0, 0)
        a4 = a2 + win(-2, 0) + win(1, 0)
        yield
        a8 = win(-4, 1)
        for d in range(-3, 4):
            a8 = a8 + win(d, 1)
        yield
        a16 = a8
        for d in list(range(-8, -4)) + list(range(4, 8)):
            a16 = a16 + win(d, 1)
        yield
        sums = jnp.concatenate([jnp.where(lo, a2, a4), jnp.where(lo, a8, a16)], axis=1)
        yd = sums * cnt_ref[rows(k), :] - pd
        mix_ref[rows(k), 3 * GW:4 * GW] = _bdot(yd, wpool_ref[...]) * vec[2:3]
        done.add(("D", k))
        yield

    def out_projection(k, done):
        while not all((m, k) in done for m in "ABCD"):
            yield
        y = jnp.dot(mix_ref[rows(k), :].astype(BF16), wout_ref[...], preferred_element_type=F32)
        yield
        x1_ref[rows(k), :] = _layer_norm(ALPHA * xres_ref[rows(k), :] + g_a * y, ln[0:1], ln[2:3])
        yield

    def forward_chains(chunk_of, o_bwd):
        done, o_fwd = set(), {}
        chains = [retention(range(group), 0, sf_scr, o_fwd)]
        for k in range(group):
            chains += [attention(k, chunk_of(k), done), gating_unit(k, done), pooling(k, chunk_of(k), done),
                       retention_mix(k, o_fwd, o_bwd, done), out_projection(k, done)]
        return chains

    if not latent:
        sb_scr[...] = jnp.zeros((GW, GW), F32)
        sf_scr[...] = jnp.zeros((GW, GW), F32)
        kn_ref[...] = ka_ref[...]
        vn_ref[...] = va_ref[...]
        o_bwd = {}
        _interleave([retention(reversed(range(group)), 1, sb_scr, o_bwd)] + forward_chains(lambda k: k, o_bwd))
        _store_state(st_ref, 1, sb_scr)
        _store_state(st_ref, 0, sf_scr)
        return

    p = pl.program_id(1)
    g = pl.program_id(2)

    @pl.when(p == 0)
    def _():
        @pl.when(g == 0)
        def _():
            _load_state(sb_scr, s0_ref, 1)
            zero_blk = jnp.zeros((CHUNK, 2 * HEAD_DIM), BF16)
            for scr in (k_scr, v_scr):
                scr[0:CHUNK, :] = zero_blk
                scr[(nc + 1) * CHUNK:(nc + 2) * CHUNK, :] = zero_blk

        first = (ng - 1 - g) * group
        o_bwd = {}
        _interleave([retention(reversed(range(group)), 1, sb_scr, o_bwd)]
                    + [stage_kv(k, first + k) for k in range(group)])
        for k in range(group):
            ob_scr[pl.ds(_chunk_off(first + k), CHUNK), :] = o_bwd[k]

    @pl.when(p == 1)
    def _():
        @pl.when(g == 0)
        def _():
            _load_state(sf_scr, s0_ref, 0)

        first = g * group
        o_bwd = {k: ob_scr[pl.ds(_chunk_off(first + k), CHUNK), :] for k in range(group)}
        _interleave(forward_chains(lambda k: first + k, o_bwd))


def _mixer(z, x_res, mod, w_out_bf16, ln8, tabs, l, latent, extra=None):
    nb = DEC_BATCH if latent else BATCH
    nc = (DEC_SEQ if latent else SEQ) // CHUNK
    group = LAT_GROUP if latent else nc
    ng = nc // group
    blk = group * CHUNK
    base = (N_CTX // blk) if latent else 0
    per8 = blk // POOL_HALO
    last_halo = ROWS // POOL_HALO - 1

    def on_grid(f):
        return (lambda b, p, g: f(b, p, g)) if latent else (lambda b: f(b, 1, 0))

    def fwd(b, p, g):
        return base + b * ng + g * p

    def both(b, p, g):
        return base + b * ng + jnp.where(p == 0, ng - 1 - g, g)

    def bwd_only(b, p, g):
        return base + b * ng + (ng - 1 - g) * (1 - p)

    def col(width, idx, rowmap):
        return pl.BlockSpec((blk, width), on_grid(lambda b, p, g: (rowmap(b, p, g), idx)))

    def const(shape):
        return pl.BlockSpec(shape, on_grid(lambda b, p, g: (0,) * len(shape)))

    def layer(shape):
        return pl.BlockSpec((None,) + shape, on_grid(lambda b, p, g: (l,) + (0,) * len(shape)))

    specs, args = [], []

    def add(spec, arr):
        specs.append(spec)
        args.append(arr)

    add(col(GW, 0, fwd), z)
    add(col(128, 2, bwd_only if latent else fwd), z)
    add(col(128, 3, bwd_only if latent else fwd), z)
    if latent:
        add(pl.BlockSpec((None, None, PAST_LEN, 128), lambda b, p, g: (b, l, 0, 0)), extra["ck"])
        add(pl.BlockSpec((None, None, PAST_LEN, 128), lambda b, p, g: (b, l, 0, 0)), extra["cv"])
    add(col(GW, 2, fwd), z)
    add(col(GW, 3, fwd), z)
    add(col(GW, 4, both), z)
    add(col(GW, 5, both), z)
    add(col(GW, 6, both), z)
    add(col(GW, 7, fwd), z)
    add(col(GW, 8, fwd), z)
    add(col(GW, 9, fwd), z)
    if latent:
        add(pl.BlockSpec((POOL_HALO, GW),
                         lambda b, p, g: (jnp.maximum(fwd(b, p, g) * per8 - 1, 0), 9)), z)
        add(pl.BlockSpec((POOL_HALO, GW),
                         lambda b, p, g: (jnp.minimum((fwd(b, p, g) + 1) * per8, last_halo), 9)), z)
        rope_map = lambda b, p, g: (jnp.where(p == 0, ng - 1 - g, g), 0)
        add(pl.BlockSpec((blk, 128), rope_map), extra["cos"])
        add(pl.BlockSpec((blk, 128), rope_map), extra["sin"])
        add(pl.BlockSpec((None, None, 2, N_HEADS, HEAD_DIM, HEAD_DIM),
                         lambda b, p, g: (b, l, 0, 0, 0, 0)), extra["s0"])
    add(layer((2, N_HEADS * CHUNK, CHUNK)), tabs["dmat"])
    add(layer((4, CHUNK, GW)), tabs["dec"])
    add(layer((N_HEADS * CHUNK, 128)), tabs["sink"])
    add(layer((CHUNK, N_HEADS * CHUNK)), tabs["ws"])
    add(layer((CHUNK, GW)), tabs["bias"])
    add(layer((8, GW)), tabs["vec"])
    add(const((GW, GW)), tabs["gmat"])
    add(pl.BlockSpec((blk, GW), on_grid(lambda b, p, g: (g * p, 0))),
        tabs["cnt_lat"] if latent else tabs["cnt_ctx"])
    add(layer((GW, GW)), tabs["wpool"])
    local = lambda b, p, g: (b * ng + g * p, 0)
    add(pl.BlockSpec((blk, D_MODEL), on_grid(local)), x_res)
    add(pl.BlockSpec((None, None, 6, D_MODEL),
                     on_grid(lambda b, p, g: (l, (1 + b) if latent else 0, 0, 0))), mod)
    add(layer((D_MODEL, D_MODEL)), w_out_bf16)
    add(layer((8, D_MODEL)), ln8)

    out_shape = [jax.ShapeDtypeStruct((nb * nc * CHUNK, D_MODEL), F32)]
    out_specs = [pl.BlockSpec((blk, D_MODEL), on_grid(local))]
    scratch = [pltpu.VMEM((GW, GW), F32), pltpu.VMEM((GW, GW), F32),
               pltpu.VMEM((nc * CHUNK, GW), F32),
               pltpu.VMEM((group, CHUNK + 2 * POOL_HALO, GW), F32),
               pltpu.VMEM((blk, D_MODEL), F32)]
    if latent:
        scratch += [pltpu.VMEM(((nc + 2) * CHUNK, 128), BF16), pltpu.VMEM(((nc + 2) * CHUNK, 128), BF16)]
    else:
        out_shape.append(jax.ShapeDtypeStruct((nb, 2, N_HEADS, HEAD_DIM, HEAD_DIM), F32))
        out_specs.append(pl.BlockSpec((None, 2, N_HEADS, HEAD_DIM, HEAD_DIM), lambda b: (b, 0, 0, 0, 0)))
        for _ in range(2):
            out_shape.append(jax.ShapeDtypeStruct((nb, SEQ, 128), F32))
            out_specs.append(pl.BlockSpec((None, SEQ, 128), lambda b: (b, 0, 0)))

    return pl.pallas_call(
        functools.partial(_mixer_kernel, latent=latent, group=group, ng=ng),
        grid=(nb, 2, ng) if latent else (nb,),
        in_specs=specs,
        out_specs=out_specs,
        out_shape=out_shape,
        scratch_shapes=scratch,
        compiler_params=pltpu.CompilerParams(
            dimension_semantics=("arbitrary",) * (3 if latent else 1), vmem_limit_bytes=VMEM_LIMIT),
        name="mixer_latent" if latent else "mixer_context",
    )(*args)


def _pad_rows(rows, n=8):
    a = jnp.stack(rows)
    return jnp.concatenate([a, jnp.zeros((n - a.shape[0],) + a.shape[1:], a.dtype)], axis=0)


def _block_diag(blocks):
    g, n, _ = blocks.shape
    eye = jnp.eye(g, dtype=blocks.dtype)
    return (eye[:, None, :, None] * blocks[:, :, None, :]).reshape(g * n, g * n)


def _inv_count(n):
    t = np.arange(n)
    cols = []
    for w in POOL_WINDOWS:
        cnt = np.clip(t + w // 2, 0, n) - np.clip(t - w // 2, 0, n)
        cols.append(np.repeat((1.0 / cnt)[:, None], HEAD_DIM, axis=1))
    return jnp.asarray(np.concatenate(cols, axis=1), F32)


def _rope_tables():
    rows = DEC_SEQ // GRID_W
    r, cc = jnp.meshgrid(jnp.arange(rows), jnp.arange(GRID_W), indexing="ij")
    half = HEAD_DIM // 2
    freqs = ROPE_BASE ** (-jnp.arange(0, half, 2, dtype=F32) / half)

    def tables(pos):
        ang = pos.reshape(-1).astype(F32)[:, None] * freqs[None, :]
        cos, sin = jnp.cos(ang), jnp.sin(ang)
        return jnp.concatenate([cos, cos], axis=1), jnp.concatenate([-sin, sin], axis=1)

    cr, sr = tables(r)
    ccol, scol = tables(cc)
    cos = jnp.concatenate([cr, ccol], axis=1)
    sin = jnp.concatenate([sr, scol], axis=1)
    return jnp.tile(cos, (1, 2)), jnp.tile(sin, (1, 2))


def _layer_tables(attn_sink, sgu_norm_w, sgu_norm_b, sgu_ws, sgu_bs, ret_decay, ret_gn_w, pool_w, pool_scale):
    log_g = jax.nn.log_sigmoid(ret_decay.astype(F32))
    i = jnp.arange(CHUNK, dtype=F32)
    rel = i[:, None] - i[None, :]
    kscale = HEAD_DIM ** -0.5
    d_f = jnp.where(rel >= 0, jnp.exp(jnp.maximum(rel, 0.0)[None] * log_g[0][:, None, None]), 0.0)
    d_b = jnp.where(rel <= 0, jnp.exp(jnp.maximum(-rel, 0.0)[None] * log_g[1][:, None, None]), 0.0)
    dmat = jnp.stack([d_f.reshape(N_HEADS * CHUNK, CHUNK), d_b.reshape(N_HEADS * CHUNK, CHUNK)]) * kscale

    def lanes(per_head):
        return jnp.repeat(per_head, HEAD_DIM, axis=1)

    qdec_f = lanes(jnp.exp((i + 1.0)[:, None] * log_g[0][None, :]))
    qdec_b = lanes(jnp.exp((CHUNK - i)[:, None] * log_g[1][None, :]))
    kdec_f = lanes(jnp.exp((CHUNK - 1.0 - i)[:, None] * log_g[0][None, :])) * kscale
    kdec_b = lanes(jnp.exp(i[:, None] * log_g[1][None, :])) * kscale
    cdec = jnp.repeat(jnp.exp(CHUNK * log_g), HEAD_DIM, axis=1)
    vec = _pad_rows([sgu_norm_w, sgu_norm_b, pool_scale, ret_gn_w[0], ret_gn_w[1], cdec[0], cdec[1]])
    return {
        "dmat": dmat,
        "dec": jnp.stack([qdec_f, qdec_b, kdec_f, kdec_b]),
        "sink": jnp.broadcast_to(jnp.repeat(attn_sink, CHUNK)[:, None], (N_HEADS * CHUNK, 128)),
        "ws": jnp.concatenate([sgu_ws[h] for h in range(N_HEADS)], axis=1).astype(BF16),
        "bias": jnp.repeat(sgu_bs.T, HEAD_DIM, axis=1),
        "vec": vec,
        "wpool": _block_diag(pool_w).astype(BF16),
    }


def kernel(x_prompt, x_sample, cache_attn_k, cache_attn_v, state_ret, c, c_ctx, w_ada, b_ada, w_in,
           w_out, attn_sink, sgu_norm_w, sgu_norm_b, sgu_ws, sgu_bs, ret_decay, ret_gn_w, pool_w,
           pool_scale, ffn_up, ffn_conv_w, ffn_conv_b, ffn_down, ln_w, ln_b):
    cond8 = jnp.concatenate([c_ctx[None], c, jnp.zeros((8 - 1 - DEC_BATCH, D_MODEL), F32)], axis=0)
    mod = _modulation(cond8, w_ada, b_ada).reshape(DEPTH, 8, 6, D_MODEL)

    tabs = jax.vmap(_layer_tables)(attn_sink, sgu_norm_w, sgu_norm_b, sgu_ws, sgu_bs, ret_decay, ret_gn_w,
                                   pool_w, pool_scale)
    tabs["gmat"] = _block_diag(jnp.full((N_HEADS, HEAD_DIM, HEAD_DIM), 1.0 / HEAD_DIM, F32)).astype(BF16)
    tabs["cnt_ctx"] = _inv_count(SEQ)
    tabs["cnt_lat"] = _inv_count(DEC_SEQ)
    cos, sin = _rope_tables()
    extra = {"ck": cache_attn_k.reshape(DEC_BATCH, DEPTH, PAST_LEN, 128),
             "cv": cache_attn_v.reshape(DEC_BATCH, DEPTH, PAST_LEN, 128),
             "cos": cos, "sin": sin, "s0": state_ret}
    ln8 = jnp.concatenate([ln_w, ln_b, jnp.zeros((DEPTH, 4, D_MODEL), F32)], axis=1)
    conv8 = jnp.concatenate([ffn_conv_w, ffn_conv_b[:, None], jnp.zeros((DEPTH, 4, 2 * D_FF), F32)], axis=1)

    down_bf16 = ffn_down
    w_out_bf16 = _to_bf16(w_out, D_MODEL)
    up_chunks = _up_weight_chunks(ffn_up)

    xs = [x_prompt.reshape(N_CTX, D_MODEL), x_sample.reshape(N_LAT, D_MODEL)]
    new_k, new_v, new_s = [], [], []
    for l in range(DEPTH):
        z = _inproj(xs, mod, w_in, l)
        x1_ctx, st, kn, vn = _mixer(z, xs[0], mod, w_out_bf16, ln8, tabs, l, latent=False)
        (x1_lat,) = _mixer(z, xs[1], mod, w_out_bf16, ln8, tabs, l, latent=True, extra=extra)
        xs = _ffn([x1_ctx, x1_lat], mod, up_chunks, conv8, down_bf16, ln8, l, split_out=True)
        new_k.append(kn.reshape(BATCH, SEQ, 2, HEAD_DIM))
        new_v.append(vn.reshape(BATCH, SEQ, 2, HEAD_DIM))
        new_s.append(st)

    y_prompt = xs[0].reshape(BATCH, SEQ, D_MODEL)
    y_sample = xs[1].reshape(DEC_BATCH, DEC_SEQ, D_MODEL)
    return (y_prompt, y_sample, jnp.stack(new_k, axis=1), jnp.stack(new_v, axis=1),
            jnp.stack(new_s, axis=1))
```

```python
import functools

import numpy as np
import jax
import jax.numpy as jnp
from jax import lax
from jax.experimental import pallas as pl
from jax.experimental.pallas import tpu as pltpu

F32 = jnp.float32
BF16 = jnp.bfloat16

D_MODEL = 1024
BATCH = 16
SEQ = 256
DEPTH = 2
DEC_BATCH = 2
DEC_SEQ = 2048
PAST_LEN = 256
GRID_W = 64
CHUNK = 128
HEAD_DIM = 64
GW = D_MODEL // 4
N_HEADS = 4
POOL_WINDOWS = (2, 4, 8, 16)
POOL_HALO = 8
D_FF = 2816
ROPE_BASE = 10000.0
LN_EPS = 1e-5
NEG_INF = -1e30
IN_WIDTH = 10 * GW
ALPHA = (2.0 * DEPTH) ** 0.25

N_CTX = BATCH * SEQ
N_LAT = DEC_BATCH * DEC_SEQ
ROWS = N_CTX + N_LAT

TM = 1024
NB_IN = 512
FC = 256
NB_ADA = 1536
VMEM_LIMIT = 56 * 1024 * 1024


def _cond_of_tile(i, tm=TM):
    ctx_tiles = N_CTX // tm
    return jnp.where(i < ctx_tiles, 0, 1 + (i - ctx_tiles) // (DEC_SEQ // tm))


def _tile_specs(n_src, tm, grid_rank):
    ctx_tiles = N_CTX // tm
    if n_src == 1:
        rows = [lambda i: i]
    else:
        rows = [lambda i: jnp.minimum(i, ctx_tiles - 1), lambda i: jnp.maximum(i - ctx_tiles, 0)]
    if grid_rank == 1:
        return [pl.BlockSpec((tm, D_MODEL), lambda i, f=f: (f(i), 0)) for f in rows]
    return [pl.BlockSpec((tm, D_MODEL), lambda i, j, f=f: (f(i), 0)) for f in rows]


def _per_half(i, tm, n_max, fn):
    if n_max == 1:
        fn(0)
        return
    ctx_tiles = N_CTX // tm

    @pl.when(i < ctx_tiles)
    def _():
        fn(0)

    @pl.when(i >= ctx_tiles)
    def _():
        fn(1)


def _layer_norm(x, w, b):
    mu = jnp.mean(x, axis=-1, keepdims=True)
    d = x - mu
    var = jnp.mean(d * d, axis=-1, keepdims=True)
    return d * lax.rsqrt(var + LN_EPS) * w + b


def _silu(x):
    return x * jax.nn.sigmoid(x)


def _bdot(a, b):
    return jnp.dot(a.astype(BF16), b.astype(BF16), preferred_element_type=F32)


def _mod_kernel(c_ref, w_ref, b_ref, o_ref):
    o_ref[...] = _bdot(_silu(c_ref[...]), w_ref[...]) + b_ref[...]


def _modulation(cond8, w_ada, b_ada):
    return pl.pallas_call(
        _mod_kernel,
        grid=(DEPTH, 6 * D_MODEL // NB_ADA),
        in_specs=[
            pl.BlockSpec((8, D_MODEL), lambda l, j: (0, 0)),
            pl.BlockSpec((None, D_MODEL, NB_ADA), lambda l, j: (l, 0, j)),
            pl.BlockSpec((None, 1, NB_ADA), lambda l, j: (l, 0, j)),
        ],
        out_specs=pl.BlockSpec((None, 8, NB_ADA), lambda l, j: (l, 0, j)),
        out_shape=jax.ShapeDtypeStruct((DEPTH, 8, 6 * D_MODEL), F32),
        compiler_params=pltpu.CompilerParams(
            dimension_semantics=("arbitrary", "arbitrary"), vmem_limit_bytes=VMEM_LIMIT),
        name="modulation",
    )(cond8, w_ada, b_ada.reshape(DEPTH, 1, 6 * D_MODEL))


def _cast_kernel(w_ref, o_ref):
    o_ref[...] = w_ref[...].astype(BF16)


def _to_bf16(w, block_rows):
    depth, rows, cols = w.shape
    return pl.pallas_call(
        _cast_kernel,
        grid=(depth, rows // block_rows),
        in_specs=[pl.BlockSpec((None, block_rows, cols), lambda l, r: (l, r, 0))],
        out_specs=pl.BlockSpec((None, block_rows, cols), lambda l, r: (l, r, 0)),
        out_shape=jax.ShapeDtypeStruct(w.shape, BF16),
        compiler_params=pltpu.CompilerParams(dimension_semantics=("arbitrary", "arbitrary")),
        name="cast_bf16",
    )(w)


def _inproj_kernel(*refs, n_x):
    x_refs = refs[:n_x]
    mod_ref, w_ref, z_ref, h_scr = refs[n_x:]

    m = mod_ref[...]

    def build(side):
        h_scr[...] = (x_refs[side][...] * (1.0 + m[1:2]) + m[0:1]).astype(BF16)

    _per_half(pl.program_id(0), TM, n_x, build)
    h = h_scr[...]
    for jb in range(IN_WIDTH // NB_IN):
        cols = slice(jb * NB_IN, (jb + 1) * NB_IN)
        z_ref[:, cols] = jnp.dot(h, w_ref[:, cols].astype(BF16), preferred_element_type=F32)


def _inproj(xs, mod, w_in, l):
    return pl.pallas_call(
        functools.partial(_inproj_kernel, n_x=len(xs)),
        grid=(ROWS // TM,),
        in_specs=_tile_specs(len(xs), TM, 1) + [
            pl.BlockSpec((None, None, 6, D_MODEL), lambda i: (l, _cond_of_tile(i), 0, 0)),
            pl.BlockSpec((None, D_MODEL, IN_WIDTH), lambda i: (l, 0, 0), pipeline_mode=pl.Buffered(1)),
        ],
        out_specs=pl.BlockSpec((TM, IN_WIDTH), lambda i: (i, 0)),
        out_shape=jax.ShapeDtypeStruct((ROWS, IN_WIDTH), F32),
        scratch_shapes=[pltpu.VMEM((TM, D_MODEL), BF16)],
        compiler_params=pltpu.CompilerParams(
            dimension_semantics=("arbitrary",), vmem_limit_bytes=VMEM_LIMIT),
        name="inproj",
    )(*xs, mod, w_in)


TMF = 512
SEGF = TMF // 8
PITCHF = SEGF + 8
CTXF_TILES = N_CTX // TMF
LATF_PER_SEQ = DEC_SEQ // TMF
HALO_ROWS = 16
RBUF = 256
RB = 256
GB = 64
NFC = D_FF // FC
LANE_BLOCKS = D_MODEL // 128
assert RB == RBUF


def _up_weight_kernel(a_ref, g_ref, o_ref):
    for c in range(NFC):
        o_ref[c, :, 0:FC] = a_ref[:, c * FC:(c + 1) * FC].astype(BF16)
        o_ref[c, :, FC:2 * FC] = g_ref[:, c * FC:(c + 1) * FC].astype(BF16)


def _up_weight_chunks(ffn_up):
    half = D_MODEL // 2
    return pl.pallas_call(
        _up_weight_kernel,
        grid=(DEPTH, 2),
        in_specs=[pl.BlockSpec((None, half, D_FF), lambda l, r: (l, r, 0)),
                  pl.BlockSpec((None, half, D_FF), lambda l, r: (l, r, 1))],
        out_specs=pl.BlockSpec((None, NFC, half, 2 * FC), lambda l, r: (l, 0, r, 0)),
        out_shape=jax.ShapeDtypeStruct((DEPTH, NFC, D_MODEL, 2 * FC), BF16),
        compiler_params=pltpu.CompilerParams(
            dimension_semantics=("arbitrary", "arbitrary"), vmem_limit_bytes=VMEM_LIMIT),
        name="cast_up_chunks",
    )(ffn_up, ffn_up)


def _seg_rows_f(xc_ref, k):
    return jnp.concatenate([xc_ref[cb, pl.ds(k, 8, stride=PITCHF), :] for cb in range(LANE_BLOCKS)], axis=1)


def _ffn_kernel(*refs, n_out):
    x_refs = refs[:2]
    (xp_ref, xn_ref, mod_ref, up_ref, cv_ref, dn_ref, ln_ref) = refs[2:9]
    o_refs = refs[9:9 + n_out]
    h_scr, act_scr, xc_scr, u0_scr, u1_scr = refs[9 + n_out:]
    u_scrs = (u0_scr, u1_scr)
    i = pl.program_id(0)
    is_ctx = i < CTXF_TILES
    lat_pos = (i - CTXF_TILES) % LATF_PER_SEQ
    m = mod_ref[...]
    ln = ln_ref[...]

    scale = 1.0 + m[4:5]
    shift = m[3:4]

    def stage(side):
        for cb in range(LANE_BLOCKS):
            for s in range(8):
                xc_scr[cb, s * PITCHF:s * PITCHF + SEGF, :] = x_refs[side][s * SEGF:(s + 1) * SEGF,
                                                                            cb * 128:(cb + 1) * 128]

    _per_half(i, TMF, 2, stage)

    def build_h(ub):
        for k in range(ub * RBUF // 8, (ub + 1) * RBUF // 8, 2):
            rows = jnp.concatenate([_seg_rows_f(xc_scr, k), _seg_rows_f(xc_scr, k + 1)], axis=0)
            h_scr[8 * k:8 * k + 16, :] = (rows * scale + shift).astype(BF16)

    sub16 = lax.broadcasted_iota(jnp.int32, (HALO_ROWS, D_MODEL), 0)
    prev_ok = jnp.logical_not(is_ctx) & (lat_pos > 0)
    next_ok = jnp.logical_not(is_ctx) & (lat_pos < LATF_PER_SEQ - 1)
    halo_x = jnp.where(sub16 == 0, xp_ref[POOL_HALO - 1:POOL_HALO, :], xn_ref[0:1, :])
    keep = ((sub16 == 0) & prev_ok) | ((sub16 == 1) & next_ok)
    h_scr[TMF:TMF + HALO_ROWS, :] = jnp.where(keep, halo_x * scale + shift, 0.0).astype(BF16)

    n_ub = TMF // RBUF
    sub = lax.broadcasted_iota(jnp.int32, (8, FC), 0)
    seg_per_seq = SEQ // SEGF
    ctx_first = is_ctx & (sub % seg_per_seq == 0)
    ctx_last = is_ctx & (sub % seg_per_seq == seg_per_seq - 1)

    def up_proj(slot, c, ub):
        rows = slice(ub * RBUF, (ub + 1) * RBUF + (HALO_ROWS if ub == n_ub - 1 else 0))
        u_scrs[slot][rows, :] = jnp.dot(h_scr[rows, :], up_ref[c], preferred_element_type=F32)

    def conv(u_ref, lanes, cvs, r0):
        lo = max(r0 - 8, 0)
        hi = min(r0 + GB + 8, TMF)
        ue = u_ref[lo:hi, lanes]
        u = ue[r0 - lo:r0 - lo + GB]
        if r0 == 0:
            b_first = jnp.where(sub == 0, u_ref[TMF:TMF + 1, lanes],
                                pltpu.roll(u_ref[TMF - 8:TMF, lanes], 1, axis=0))
            um1 = jnp.concatenate([jnp.where(ctx_first, 0.0, b_first), u[0:GB - 8]], axis=0)
        else:
            um1 = ue[0:GB]
        if r0 == TMF - GB:
            b_last = jnp.where(sub == 7, u_ref[TMF + 1:TMF + 2, lanes],
                               pltpu.roll(u_ref[0:8, lanes], 7, axis=0))
            up1 = jnp.concatenate([u[8:GB], jnp.where(ctx_last, 0.0, b_last)], axis=0)
        else:
            up1 = ue[r0 - lo + 8:r0 - lo + GB + 8]
        return um1 * cvs[0:1] + u * cvs[1:2] + up1 * cvs[2:3] + cvs[3:4]

    def gate(slot, c, ub):
        cva = cv_ref[0:4, c * FC:(c + 1) * FC]
        cvg = cv_ref[0:4, D_FF + c * FC:D_FF + (c + 1) * FC]
        for r0 in range(ub * RBUF, (ub + 1) * RBUF, GB):
            a = conv(u_scrs[slot], slice(0, FC), cva, r0)
            g = conv(u_scrs[slot], slice(FC, 2 * FC), cvg, r0)
            act_scr[c, r0:r0 + GB, :] = (_silu(a) * g).astype(BF16)

    def finish(rb):
        vrows = range(rb * RB // 8, (rb + 1) * RB // 8)
        lhs = jnp.concatenate([act_scr[c, rb * RB:(rb + 1) * RB, :] for c in range(NFC)], axis=1)
        y = jnp.dot(lhs, dn_ref[...].astype(BF16), preferred_element_type=F32)
        xr = jnp.concatenate([_seg_rows_f(xc_scr, k) for k in vrows], axis=0)
        out = _layer_norm(ALPHA * xr + m[5:6] * y, ln[1:2], ln[3:4])
        for kk, k in enumerate(vrows):
            for cb in range(LANE_BLOCKS):
                xc_scr[cb, pl.ds(k, 8, stride=PITCHF), :] = out[8 * kk:8 * kk + 8, cb * 128:(cb + 1) * 128]

    def write_out(side):
        for cb in range(LANE_BLOCKS):
            for s in range(8):
                o_refs[side][s * SEGF:(s + 1) * SEGF, cb * 128:(cb + 1) * 128] = xc_scr[
                    cb, s * PITCHF:s * PITCHF + SEGF, :]

    build_h(0)
    for c in range(NFC + 1):
        for ub in range(n_ub):
            if c == 0 and ub > 0:
                build_h(ub)
            if c < NFC:
                up_proj(c % 2, c, ub)
            if c >= 1:
                gate((c - 1) % 2, c - 1, ub)
            if c == NFC:
                finish(ub)

    _per_half(i, TMF, n_out, write_out)


def _ffn(xs, mod, up_chunks, conv8, down_bf16, ln8, l, split_out):
    halo_blocks = TMF // POOL_HALO
    last_halo = N_LAT // POOL_HALO - 1
    n_out = 2 if split_out else 1
    if split_out:
        out_shape = [jax.ShapeDtypeStruct((N_CTX, D_MODEL), F32), jax.ShapeDtypeStruct((N_LAT, D_MODEL), F32)]
    else:
        out_shape = [jax.ShapeDtypeStruct((ROWS, D_MODEL), F32)]
    return pl.pallas_call(
        functools.partial(_ffn_kernel, n_out=n_out),
        grid=(ROWS // TMF,),
        in_specs=[
            *_tile_specs(2, TMF, 1),
            pl.BlockSpec((POOL_HALO, D_MODEL),
                         lambda i: (jnp.maximum((i - CTXF_TILES) * halo_blocks - 1, 0), 0)),
            pl.BlockSpec((POOL_HALO, D_MODEL),
                         lambda i: (jnp.clip((i - CTXF_TILES + 1) * halo_blocks, 0, last_halo), 0)),
            pl.BlockSpec((None, None, 6, D_MODEL), lambda i: (l, _cond_of_tile(i, TMF), 0, 0)),
            pl.BlockSpec((None, NFC, D_MODEL, 2 * FC), lambda i: (l, 0, 0, 0), pipeline_mode=pl.Buffered(1)),
            pl.BlockSpec((None, 8, 2 * D_FF), lambda i: (l, 0, 0)),
            pl.BlockSpec((None, D_FF, D_MODEL), lambda i: (l, 0, 0), pipeline_mode=pl.Buffered(1)),
            pl.BlockSpec((None, 8, D_MODEL), lambda i: (l, 0, 0)),
        ],
        out_specs=_tile_specs(n_out, TMF, 1),
        out_shape=out_shape,
        scratch_shapes=[pltpu.VMEM((TMF + HALO_ROWS, D_MODEL), BF16),
                        pltpu.VMEM((NFC, TMF, FC), BF16),
                        pltpu.VMEM((LANE_BLOCKS, 8 * PITCHF, 128), F32),
                        pltpu.VMEM((TMF + HALO_ROWS, 2 * FC), F32),
                        pltpu.VMEM((TMF + HALO_ROWS, 2 * FC), F32)],
        compiler_params=pltpu.CompilerParams(
            dimension_semantics=("arbitrary",), vmem_limit_bytes=VMEM_LIMIT),
        name="convffn",
    )(xs[0], xs[1], xs[1], xs[1], mod, up_chunks, conv8, down_bf16, ln8)


def _head_masks(width):
    lane = lax.broadcasted_iota(jnp.int32, (1, width), 1)
    return [(lane >= h * HEAD_DIM) & (lane < (h + 1) * HEAD_DIM) for h in range(width // HEAD_DIM)]


def _stack_heads(x, masks):
    return jnp.concatenate([jnp.where(m, x, jnp.zeros_like(x)) for m in masks], axis=0)


def _rope(x, cos, sin):
    lane = lax.broadcasted_iota(jnp.int32, (1, 128), 1)
    lower = (lane & 31) < 16
    outs = []
    for k in range(x.shape[1] // 128):
        xb = x[:, k * 128:(k + 1) * 128]
        partner = jnp.where(lower, pltpu.roll(xb, 112, axis=1), pltpu.roll(xb, 16, axis=1))
        outs.append(xb * cos + partner * sin)
    return outs[0] if len(outs) == 1 else jnp.concatenate(outs, axis=1)


def _load_state(state_scr, blocks_ref, d):
    state_scr[...] = jnp.zeros((GW, GW), F32)
    for h in range(N_HEADS):
        sl = slice(h * HEAD_DIM, (h + 1) * HEAD_DIM)
        state_scr[sl, sl] = blocks_ref[d, h]


def _store_state(blocks_ref, d, state_scr):
    for h in range(N_HEADS):
        sl = slice(h * HEAD_DIM, (h + 1) * HEAD_DIM)
        blocks_ref[d, h] = state_scr[sl, sl]


LAT_GROUP = 4


def _chunk_off(c):
    return c * CHUNK if isinstance(c, int) else pl.multiple_of(c * CHUNK, CHUNK)


def _interleave(chains):
    chains = list(chains)
    while chains:
        for ch in list(chains):
            try:
                next(ch)
            except StopIteration:
                chains.remove(ch)


def _mixer_kernel(*refs, latent, group, ng, cast_up=False):
    if latent:
        (qa_ref, ka_ref, va_ref, kx_ref, vx_ref, ub_ref, vb_ref, qc_ref, kc_ref, vc_ref, gf_ref, gb_ref,
         pd_ref, pdp_ref, pdn_ref, cos_ref, sin_ref, s0_ref, dmat_ref, dec_ref, sink_ref, ws_ref, bias_ref,
         vec_ref, gmat_ref, cnt_ref, wpool_ref, xres_ref, mod_ref, wout_ref, ln_ref,
         x1_ref, sf_scr, sb_scr, ob_scr, pext_scr, mix_ref, k_scr, v_scr) = refs
    else:
        (qa_ref, ka_ref, va_ref, ub_ref, vb_ref, qc_ref, kc_ref, vc_ref, gf_ref, gb_ref, pd_ref,
         dmat_ref, dec_ref, sink_ref, ws_ref, bias_ref, vec_ref, gmat_ref, cnt_ref, wpool_ref,
         xres_ref, mod_ref, wout_ref, ln_ref) = refs[:24]
        if cast_up:
            upa_ref, upg_ref, x1_ref, st_ref, kn_ref, vn_ref, upo_ref = refs[24:31]
        else:
            x1_ref, st_ref, kn_ref, vn_ref = refs[24:28]
        sf_scr, sb_scr, ob_scr, pext_scr, mix_ref = refs[-5:]

    nc = group * ng
    g_a = mod_ref[...][2:3]
    ln = ln_ref[...]
    masks = _head_masks(GW)
    row = lax.broadcasted_iota(jnp.int32, (GW, GW), 0)
    col = lax.broadcasted_iota(jnp.int32, (GW, GW), 1)
    bd_mask = (row // HEAD_DIM) == (col // HEAD_DIM)
    vec = vec_ref[...]
    lane = lax.broadcasted_iota(jnp.int32, (1, 2 * HEAD_DIM), 1)
    lo = lane < HEAD_DIM

    def rows(k):
        return slice(k * CHUNK, (k + 1) * CHUNK)

    def retention(order, d, state_scr, out):
        dmat = dmat_ref[d]
        qdec, kdec = dec_ref[d], dec_ref[2 + d]
        cdec = vec[5 + d:6 + d]
        state = state_scr[...]
        for k in order:
            q, kk, v = qc_ref[rows(k), :], kc_ref[rows(k), :], vc_ref[rows(k), :]
            qb, kb, vb = q.astype(BF16), kk.astype(BF16), v.astype(BF16)
            s = lax.dot_general(_stack_heads(qb, masks), kb, (((1,), (1,)), ((), ())),
                                preferred_element_type=F32)
            yield
            p = (s * dmat).astype(BF16)
            p_cat = jnp.concatenate([p[h * CHUNK:(h + 1) * CHUNK] for h in range(N_HEADS)], axis=1)
            o = (jnp.dot(p_cat, _stack_heads(vb, masks), preferred_element_type=F32)
                 + _bdot(q * qdec, state))
            yield
            upd = lax.dot_general((kk * kdec).astype(BF16), vb, (((0,), (0,)), ((), ())),
                                  preferred_element_type=F32)
            state = state * cdec + jnp.where(bd_mask, upd, 0.0)
            out[k] = o
            yield
        state_scr[...] = state

    def stage_kv(k, c):
        dst = pl.ds(_chunk_off(c + 1), CHUNK)
        k_scr[dst, :] = _rope(ka_ref[rows(k), :], cos_ref[rows(k), :], sin_ref[rows(k), :]).astype(BF16)
        yield
        v_scr[dst, :] = va_ref[rows(k), :].astype(BF16)
        yield

    def attention(k, c, done):
        q = qa_ref[rows(k), :]
        if latent:
            q = _rope(q, cos_ref[rows(k), :], sin_ref[rows(k), :])
        q = q * (HEAD_DIM ** -0.5)
        q0, q1 = q[:, :128], q[:, 128:]
        zero = jnp.zeros_like(q0)
        q_st = jnp.concatenate([
            jnp.where(lo, q0, zero),
            jnp.where(lo, pltpu.roll(q0, HEAD_DIM, axis=1), zero),
            jnp.where(lo, zero, pltpu.roll(q1, HEAD_DIM, axis=1)),
            jnp.where(lo, zero, q1)], axis=0).astype(BF16)
        if latent:
            band = pl.ds(_chunk_off(c), 3 * CHUNK)
            k_all = jnp.concatenate([k_scr[band, :], kx_ref[...].astype(BF16)], axis=0)
            v_all = jnp.concatenate([v_scr[band, :], vx_ref[...].astype(BF16)], axis=0)
        else:
            k_all = ka_ref[...].astype(BF16)
            v_all = va_ref[...].astype(BF16)
        s = lax.dot_general(q_st, k_all, (((1,), (1,)), ((), ())), preferred_element_type=F32)
        yield
        if latent:
            nk = 3 * CHUNK + PAST_LEN
            qi = lax.broadcasted_iota(jnp.int32, (N_HEADS * CHUNK, nk), 0) & (CHUNK - 1)
            kj = lax.broadcasted_iota(jnp.int32, (N_HEADS * CHUNK, nk), 1)
            kpos = kj + (c - 1) * CHUNK
            valid = (kj >= 3 * CHUNK) | ((kj >= qi) & (kj <= qi + 2 * CHUNK)
                                         & (kpos >= 0) & (kpos < nc * CHUNK))
            s = jnp.where(valid, s, NEG_INF)
        sink = sink_ref[...][:, 0:1]
        mx = jnp.maximum(jnp.max(s, axis=-1, keepdims=True), sink)
        yield
        e = jnp.exp(s - mx)
        den = jnp.sum(e, axis=-1, keepdims=True) + jnp.exp(sink - mx)
        yield
        o = jnp.dot(e.astype(BF16), v_all, preferred_element_type=F32) / den
        yield
        mix_ref[rows(k), 0:128] = jnp.where(lo, o[0:CHUNK], pltpu.roll(o[CHUNK:2 * CHUNK], HEAD_DIM, axis=1))
        mix_ref[rows(k), 128:256] = jnp.where(lo, pltpu.roll(o[2 * CHUNK:3 * CHUNK], HEAD_DIM, axis=1),
                                              o[3 * CHUNK:4 * CHUNK])
        done.add(("A", k))
        yield

    def gating_unit(k, done):
        vn = _layer_norm(vb_ref[rows(k), :], vec[0:1], vec[1:2]).astype(BF16)
        yield
        sg = jnp.dot(ws_ref[...], _stack_heads(vn, masks), preferred_element_type=F32) + bias_ref[...]
        yield
        mix_ref[rows(k), GW:2 * GW] = ub_ref[rows(k), :] * sg
        done.add(("B", k))
        yield

    def retention_mix(k, o_fwd, o_bwd, done):
        while k not in o_fwd or k not in o_bwd:
            yield
        gmat = gmat_ref[...]
        normed = []
        for o in (o_fwd[k], o_bwd[k]):
            mu = jnp.dot(o.astype(BF16), gmat, preferred_element_type=F32)
            yield
            dlt = o - mu
            var = jnp.dot((dlt * dlt).astype(BF16), gmat, preferred_element_type=F32)
            yield
            normed.append(dlt * lax.rsqrt(var + LN_EPS))
        mix_ref[rows(k), 2 * GW:3 * GW] = (_silu(gf_ref[rows(k), :]) * (normed[0] * vec[3:4])
                                           + _silu(gb_ref[rows(k), :]) * (normed[1] * vec[4:5]))
        done.add(("C", k))
        yield

    def pooling(k, c, done):
        pd = pd_ref[rows(k), :]
        pext = pext_scr.at[k]
        zeros8 = jnp.zeros((POOL_HALO, GW), F32)
        if k > 0:
            prev8 = pd_ref[k * CHUNK - POOL_HALO:k * CHUNK, :]
        else:
            prev8 = jnp.where(c > 0, pdp_ref[...], 0.0) if latent else zeros8
        if k < group - 1:
            next8 = pd_ref[(k + 1) * CHUNK:(k + 1) * CHUNK + POOL_HALO, :]
        else:
            next8 = jnp.where(c < nc - 1, pdn_ref[...], 0.0) if latent else zeros8
        pext[0:POOL_HALO, :] = prev8
        pext[POOL_HALO:POOL_HALO + CHUNK, :] = pd
        pext[POOL_HALO + CHUNK:2 * POOL_HALO + CHUNK, :] = next8
        yield

        def win(d, half):
            return pext[pl.ds(POOL_HALO + d, CHUNK), half * 128:(half + 1) * 128]

        a2 = win(-1, 0) + win(0, 0)
        a4 = a2 + win(-2, 0) + win(1, 0)
        yield
        a8 = win(-4, 1)
        for d in range(-3, 4):
            a8 = a8 + win(d, 1)
        yield
        a16 = a8
        for d in list(range(-8, -4)) + list(range(4, 8)):
            a16 = a16 + win(d, 1)
        yield
        sums = jnp.concatenate([jnp.where(lo, a2, a4), jnp.where(lo, a8, a16)], axis=1)
        yd = sums * cnt_ref[rows(k), :] - pd
        mix_ref[rows(k), 3 * GW:4 * GW] = _bdot(yd, wpool_ref[...]) * vec[2:3]
        done.add(("D", k))
        yield

    def out_projection(k, done):
        while not all((m, k) in done for m in "ABCD"):
            yield
        y = jnp.dot(mix_ref[rows(k), :].astype(BF16), wout_ref[...], preferred_element_type=F32)
        yield
        x1_ref[rows(k), :] = _layer_norm(ALPHA * xres_ref[rows(k), :] + g_a * y, ln[0:1], ln[2:3])
        yield

    def up_weight_rows():
        for lyr in range(DEPTH):
            for cc in range(NFC):
                upo_ref[lyr, cc, :, 0:FC] = upa_ref[lyr, :, cc * FC:(cc + 1) * FC].astype(BF16)
                upo_ref[lyr, cc, :, FC:2 * FC] = upg_ref[lyr, :, cc * FC:(cc + 1) * FC].astype(BF16)
                yield

    def forward_chains(chunk_of, o_bwd):
        done, o_fwd = set(), {}
        chains = [retention(range(group), 0, sf_scr, o_fwd)]
        for k in range(group):
            chains += [attention(k, chunk_of(k), done), gating_unit(k, done), pooling(k, chunk_of(k), done),
                       retention_mix(k, o_fwd, o_bwd, done), out_projection(k, done)]
        return chains

    if not latent:
        sb_scr[...] = jnp.zeros((GW, GW), F32)
        sf_scr[...] = jnp.zeros((GW, GW), F32)
        kn_ref[...] = ka_ref[...]
        vn_ref[...] = va_ref[...]
        o_bwd = {}
        side_chains = [retention(reversed(range(group)), 1, sb_scr, o_bwd)]
        if cast_up:
            side_chains.append(up_weight_rows())
        _interleave(side_chains + forward_chains(lambda k: k, o_bwd))
        _store_state(st_ref, 1, sb_scr)
        _store_state(st_ref, 0, sf_scr)
        return

    p = pl.program_id(1)
    g = pl.program_id(2)

    @pl.when(p == 0)
    def _():
        @pl.when(g == 0)
        def _():
            _load_state(sb_scr, s0_ref, 1)
            zero_blk = jnp.zeros((CHUNK, 2 * HEAD_DIM), BF16)
            for scr in (k_scr, v_scr):
                scr[0:CHUNK, :] = zero_blk
                scr[(nc + 1) * CHUNK:(nc + 2) * CHUNK, :] = zero_blk

        first = (ng - 1 - g) * group
        o_bwd = {}
        _interleave([retention(reversed(range(group)), 1, sb_scr, o_bwd)]
                    + [stage_kv(k, first + k) for k in range(group)])
        for k in range(group):
            ob_scr[pl.ds(_chunk_off(first + k), CHUNK), :] = o_bwd[k]

    @pl.when(p == 1)
    def _():
        @pl.when(g == 0)
        def _():
            _load_state(sf_scr, s0_ref, 0)

        first = g * group
        o_bwd = {k: ob_scr[pl.ds(_chunk_off(first + k), CHUNK), :] for k in range(group)}
        _interleave(forward_chains(lambda k: first + k, o_bwd))


def _mixer(z, x_res, mod, w_out_bf16, ln8, tabs, l, latent, extra=None, ffn_up=None):
    nb = DEC_BATCH if latent else BATCH
    nc = (DEC_SEQ if latent else SEQ) // CHUNK
    group = LAT_GROUP if latent else nc
    ng = nc // group
    blk = group * CHUNK
    base = (N_CTX // blk) if latent else 0
    per8 = blk // POOL_HALO
    last_halo = ROWS // POOL_HALO - 1

    def on_grid(f):
        return (lambda b, p, g: f(b, p, g)) if latent else (lambda b: f(b, 1, 0))

    def fwd(b, p, g):
        return base + b * ng + g * p

    def both(b, p, g):
        return base + b * ng + jnp.where(p == 0, ng - 1 - g, g)

    def bwd_only(b, p, g):
        return base + b * ng + (ng - 1 - g) * (1 - p)

    def col(width, idx, rowmap):
        return pl.BlockSpec((blk, width), on_grid(lambda b, p, g: (rowmap(b, p, g), idx)))

    def const(shape):
        return pl.BlockSpec(shape, on_grid(lambda b, p, g: (0,) * len(shape)))

    def layer(shape):
        return pl.BlockSpec((None,) + shape, on_grid(lambda b, p, g: (l,) + (0,) * len(shape)))

    specs, args = [], []

    def add(spec, arr):
        specs.append(spec)
        args.append(arr)

    add(col(GW, 0, fwd), z)
    add(col(128, 2, bwd_only if latent else fwd), z)
    add(col(128, 3, bwd_only if latent else fwd), z)
    if latent:
        add(pl.BlockSpec((None, None, PAST_LEN, 128), lambda b, p, g: (b, l, 0, 0)), extra["ck"])
        add(pl.BlockSpec((None, None, PAST_LEN, 128), lambda b, p, g: (b, l, 0, 0)), extra["cv"])
    add(col(GW, 2, fwd), z)
    add(col(GW, 3, fwd), z)
    add(col(GW, 4, both), z)
    add(col(GW, 5, both), z)
    add(col(GW, 6, both), z)
    add(col(GW, 7, fwd), z)
    add(col(GW, 8, fwd), z)
    add(col(GW, 9, fwd), z)
    if latent:
        add(pl.BlockSpec((POOL_HALO, GW),
                         lambda b, p, g: (jnp.maximum(fwd(b, p, g) * per8 - 1, 0), 9)), z)
        add(pl.BlockSpec((POOL_HALO, GW),
                         lambda b, p, g: (jnp.minimum((fwd(b, p, g) + 1) * per8, last_halo), 9)), z)
        rope_map = lambda b, p, g: (jnp.where(p == 0, ng - 1 - g, g), 0)
        add(pl.BlockSpec((blk, 128), rope_map), extra["cos"])
        add(pl.BlockSpec((blk, 128), rope_map), extra["sin"])
        add(pl.BlockSpec((None, None, 2, N_HEADS, HEAD_DIM, HEAD_DIM),
                         lambda b, p, g: (b, l, 0, 0, 0, 0)), extra["s0"])
    add(layer((2, N_HEADS * CHUNK, CHUNK)), tabs["dmat"])
    add(layer((4, CHUNK, GW)), tabs["dec"])
    add(layer((N_HEADS * CHUNK, 128)), tabs["sink"])
    add(layer((CHUNK, N_HEADS * CHUNK)), tabs["ws"])
    add(layer((CHUNK, GW)), tabs["bias"])
    add(layer((8, GW)), tabs["vec"])
    add(const((GW, GW)), tabs["gmat"])
    add(pl.BlockSpec((blk, GW), on_grid(lambda b, p, g: (g * p, 0))),
        tabs["cnt_lat"] if latent else tabs["cnt_ctx"])
    add(layer((GW, GW)), tabs["wpool"])
    local = lambda b, p, g: (b * ng + g * p, 0)
    add(pl.BlockSpec((blk, D_MODEL), on_grid(local)), x_res)
    add(pl.BlockSpec((None, None, 6, D_MODEL),
                     on_grid(lambda b, p, g: (l, (1 + b) if latent else 0, 0, 0))), mod)
    add(layer((D_MODEL, D_MODEL)), w_out_bf16)
    add(layer((8, D_MODEL)), ln8)
    up_rows = D_MODEL // nb
    if ffn_up is not None:
        add(pl.BlockSpec((DEPTH, up_rows, D_FF), lambda b: (0, b, 0)), ffn_up)
        add(pl.BlockSpec((DEPTH, up_rows, D_FF), lambda b: (0, b, 1)), ffn_up)

    out_shape = [jax.ShapeDtypeStruct((nb * nc * CHUNK, D_MODEL), F32)]
    out_specs = [pl.BlockSpec((blk, D_MODEL), on_grid(local))]
    scratch = [pltpu.VMEM((GW, GW), F32), pltpu.VMEM((GW, GW), F32),
               pltpu.VMEM((nc * CHUNK, GW), F32),
               pltpu.VMEM((group, CHUNK + 2 * POOL_HALO, GW), F32),
               pltpu.VMEM((blk, D_MODEL), F32)]
    if latent:
        scratch += [pltpu.VMEM(((nc + 2) * CHUNK, 128), BF16), pltpu.VMEM(((nc + 2) * CHUNK, 128), BF16)]
    else:
        out_shape.append(jax.ShapeDtypeStruct((nb, 2, N_HEADS, HEAD_DIM, HEAD_DIM), F32))
        out_specs.append(pl.BlockSpec((None, 2, N_HEADS, HEAD_DIM, HEAD_DIM), lambda b: (b, 0, 0, 0, 0)))
        for _ in range(2):
            out_shape.append(jax.ShapeDtypeStruct((nb, SEQ, 128), F32))
            out_specs.append(pl.BlockSpec((None, SEQ, 128), lambda b: (b, 0, 0)))
        if ffn_up is not None:
            out_shape.append(jax.ShapeDtypeStruct((DEPTH, NFC, D_MODEL, 2 * FC), BF16))
            out_specs.append(pl.BlockSpec((DEPTH, NFC, up_rows, 2 * FC), lambda b: (0, 0, b, 0)))

    return pl.pallas_call(
        functools.partial(_mixer_kernel, latent=latent, group=group, ng=ng, cast_up=ffn_up is not None),
        grid=(nb, 2, ng) if latent else (nb,),
        in_specs=specs,
        out_specs=out_specs,
        out_shape=out_shape,
        scratch_shapes=scratch,
        compiler_params=pltpu.CompilerParams(
            dimension_semantics=("arbitrary",) * (3 if latent else 1), vmem_limit_bytes=VMEM_LIMIT),
        name="mixer_latent" if latent else "mixer_context",
    )(*args)


def _pad_rows(rows, n=8):
    a = jnp.stack(rows)
    return jnp.concatenate([a, jnp.zeros((n - a.shape[0],) + a.shape[1:], a.dtype)], axis=0)


def _block_diag(blocks):
    g, n, _ = blocks.shape
    eye = jnp.eye(g, dtype=blocks.dtype)
    return (eye[:, None, :, None] * blocks[:, :, None, :]).reshape(g * n, g * n)


def _inv_count(n):
    t = np.arange(n)
    cols = []
    for w in POOL_WINDOWS:
        cnt = np.clip(t + w // 2, 0, n) - np.clip(t - w // 2, 0, n)
        cols.append(np.repeat((1.0 / cnt)[:, None], HEAD_DIM, axis=1))
    return jnp.asarray(np.concatenate(cols, axis=1), F32)


def _rope_tables():
    rows = DEC_SEQ // GRID_W
    r, cc = jnp.meshgrid(jnp.arange(rows), jnp.arange(GRID_W), indexing="ij")
    half = HEAD_DIM // 2
    freqs = ROPE_BASE ** (-jnp.arange(0, half, 2, dtype=F32) / half)

    def tables(pos):
        ang = pos.reshape(-1).astype(F32)[:, None] * freqs[None, :]
        cos, sin = jnp.cos(ang), jnp.sin(ang)
        return jnp.concatenate([cos, cos], axis=1), jnp.concatenate([-sin, sin], axis=1)

    cr, sr = tables(r)
    ccol, scol = tables(cc)
    cos = jnp.concatenate([cr, ccol], axis=1)
    sin = jnp.concatenate([sr, scol], axis=1)
    return jnp.tile(cos, (1, 2)), jnp.tile(sin, (1, 2))


def _layer_tables(attn_sink, sgu_norm_w, sgu_norm_b, sgu_ws, sgu_bs, ret_decay, ret_gn_w, pool_w, pool_scale):
    log_g = jax.nn.log_sigmoid(ret_decay.astype(F32))
    i = jnp.arange(CHUNK, dtype=F32)
    rel = i[:, None] - i[None, :]
    kscale = HEAD_DIM ** -0.5
    d_f = jnp.where(rel >= 0, jnp.exp(jnp.maximum(rel, 0.0)[None] * log_g[0][:, None, None]), 0.0)
    d_b = jnp.where(rel <= 0, jnp.exp(jnp.maximum(-rel, 0.0)[None] * log_g[1][:, None, None]), 0.0)
    dmat = jnp.stack([d_f.reshape(N_HEADS * CHUNK, CHUNK), d_b.reshape(N_HEADS * CHUNK, CHUNK)]) * kscale

    def lanes(per_head):
        return jnp.repeat(per_head, HEAD_DIM, axis=1)

    qdec_f = lanes(jnp.exp((i + 1.0)[:, None] * log_g[0][None, :]))
    qdec_b = lanes(jnp.exp((CHUNK - i)[:, None] * log_g[1][None, :]))
    kdec_f = lanes(jnp.exp((CHUNK - 1.0 - i)[:, None] * log_g[0][None, :])) * kscale
    kdec_b = lanes(jnp.exp(i[:, None] * log_g[1][None, :])) * kscale
    cdec = jnp.repeat(jnp.exp(CHUNK * log_g), HEAD_DIM, axis=1)
    vec = _pad_rows([sgu_norm_w, sgu_norm_b, pool_scale, ret_gn_w[0], ret_gn_w[1], cdec[0], cdec[1]])
    return {
        "dmat": dmat,
        "dec": jnp.stack([qdec_f, qdec_b, kdec_f, kdec_b]),
        "sink": jnp.broadcast_to(jnp.repeat(attn_sink, CHUNK)[:, None], (N_HEADS * CHUNK, 128)),
        "ws": jnp.concatenate([sgu_ws[h] for h in range(N_HEADS)], axis=1).astype(BF16),
        "bias": jnp.repeat(sgu_bs.T, HEAD_DIM, axis=1),
        "vec": vec,
        "wpool": _block_diag(pool_w).astype(BF16),
    }


def kernel(x_prompt, x_sample, cache_attn_k, cache_attn_v, state_ret, c, c_ctx, w_ada, b_ada, w_in,
           w_out, attn_sink, sgu_norm_w, sgu_norm_b, sgu_ws, sgu_bs, ret_decay, ret_gn_w, pool_w,
           pool_scale, ffn_up, ffn_conv_w, ffn_conv_b, ffn_down, ln_w, ln_b):
    cond8 = jnp.concatenate([c_ctx[None], c, jnp.zeros((8 - 1 - DEC_BATCH, D_MODEL), F32)], axis=0)
    mod = _modulation(cond8, w_ada, b_ada).reshape(DEPTH, 8, 6, D_MODEL)

    tabs = jax.vmap(_layer_tables)(attn_sink, sgu_norm_w, sgu_norm_b, sgu_ws, sgu_bs, ret_decay, ret_gn_w,
                                   pool_w, pool_scale)
    tabs["gmat"] = _block_diag(jnp.full((N_HEADS, HEAD_DIM, HEAD_DIM), 1.0 / HEAD_DIM, F32)).astype(BF16)
    tabs["cnt_ctx"] = _inv_count(SEQ)
    tabs["cnt_lat"] = _inv_count(DEC_SEQ)
    cos, sin = _rope_tables()
    extra = {"ck": cache_attn_k.reshape(DEC_BATCH, DEPTH, PAST_LEN, 128),
             "cv": cache_attn_v.reshape(DEC_BATCH, DEPTH, PAST_LEN, 128),
             "cos": cos, "sin": sin, "s0": state_ret}
    ln8 = jnp.concatenate([ln_w, ln_b, jnp.zeros((DEPTH, 4, D_MODEL), F32)], axis=1)
    conv8 = jnp.concatenate([ffn_conv_w, ffn_conv_b[:, None], jnp.zeros((DEPTH, 4, 2 * D_FF), F32)], axis=1)

    down_bf16 = ffn_down
    w_out_bf16 = _to_bf16(w_out, D_MODEL)

    xs = [x_prompt.reshape(N_CTX, D_MODEL), x_sample.reshape(N_LAT, D_MODEL)]
    new_k, new_v, new_s = [], [], []
    for l in range(DEPTH):
        z = _inproj(xs, mod, w_in, l)
        if l == 0:
            x1_ctx, st, kn, vn, up_chunks = _mixer(z, xs[0], mod, w_out_bf16, ln8, tabs, l, latent=False,
                                                   ffn_up=ffn_up)
        else:
            x1_ctx, st, kn, vn = _mixer(z, xs[0], mod, w_out_bf16, ln8, tabs, l, latent=False)
        (x1_lat,) = _mixer(z, xs[1], mod, w_out_bf16, ln8, tabs, l, latent=True, extra=extra)
        xs = _ffn([x1_ctx, x1_lat], mod, up_chunks, conv8, down_bf16, ln8, l, split_out=True)
        new_k.append(kn.reshape(BATCH, SEQ, 2, HEAD_DIM))
        new_v.append(vn.reshape(BATCH, SEQ, 2, HEAD_DIM))
        new_s.append(st)

    y_prompt = xs[0].reshape(BATCH, SEQ, D_MODEL)
    y_sample = xs[1].reshape(DEC_BATCH, DEC_SEQ, D_MODEL)
    return (y_prompt, y_sample, jnp.stack(new_k, axis=1), jnp.stack(new_v, axis=1),
            jnp.stack(new_s, axis=1))
```

```python
import functools

import numpy as np
import jax
import jax.numpy as jnp
from jax import lax
from jax.experimental import pallas as pl
from jax.experimental.pallas import tpu as pltpu

F32 = jnp.float32
BF16 = jnp.bfloat16

D_MODEL = 1024
BATCH = 16
SEQ = 256
DEPTH = 2
DEC_BATCH = 2
DEC_SEQ = 2048
PAST_LEN = 256
GRID_W = 64
CHUNK = 128
HEAD_DIM = 64
GW = D_MODEL // 4
N_HEADS = 4
POOL_WINDOWS = (2, 4, 8, 16)
POOL_HALO = 8
D_FF = 2816
ROPE_BASE = 10000.0
LN_EPS = 1e-5
NEG_INF = -1e30
IN_WIDTH = 10 * GW
ALPHA = (2.0 * DEPTH) ** 0.25

N_CTX = BATCH * SEQ
N_LAT = DEC_BATCH * DEC_SEQ
ROWS = N_CTX + N_LAT

TM = 1024
NB_IN = 512
FC = 256
NB_ADA = 1536
VMEM_LIMIT = 56 * 1024 * 1024


def _cond_of_tile(i, tm=TM):
    ctx_tiles = N_CTX // tm
    return jnp.where(i < ctx_tiles, 0, 1 + (i - ctx_tiles) // (DEC_SEQ // tm))


def _tile_specs(n_src, tm, grid_rank):
    ctx_tiles = N_CTX // tm
    if n_src == 1:
        rows = [lambda i: i]
    else:
        rows = [lambda i: jnp.minimum(i, ctx_tiles - 1), lambda i: jnp.maximum(i - ctx_tiles, 0)]
    if grid_rank == 1:
        return [pl.BlockSpec((tm, D_MODEL), lambda i, f=f: (f(i), 0)) for f in rows]
    return [pl.BlockSpec((tm, D_MODEL), lambda i, j, f=f: (f(i), 0)) for f in rows]


def _per_half(i, tm, n_max, fn):
    if n_max == 1:
        fn(0)
        return
    ctx_tiles = N_CTX // tm

    @pl.when(i < ctx_tiles)
    def _():
        fn(0)

    @pl.when(i >= ctx_tiles)
    def _():
        fn(1)


def _layer_norm(x, w, b):
    mu = jnp.mean(x, axis=-1, keepdims=True)
    d = x - mu
    var = jnp.mean(d * d, axis=-1, keepdims=True)
    return d * lax.rsqrt(var + LN_EPS) * w + b


def _silu(x):
    return x * jax.nn.sigmoid(x)


def _bdot(a, b):
    return jnp.dot(a.astype(BF16), b.astype(BF16), preferred_element_type=F32)


def _mod_kernel(c_ref, w_ref, b_ref, o_ref):
    o_ref[...] = _bdot(_silu(c_ref[...]), w_ref[...]) + b_ref[...]


def _modulation(cond8, w_ada, b_ada):
    return pl.pallas_call(
        _mod_kernel,
        grid=(DEPTH, 6 * D_MODEL // NB_ADA),
        in_specs=[
            pl.BlockSpec((8, D_MODEL), lambda l, j: (0, 0)),
            pl.BlockSpec((None, D_MODEL, NB_ADA), lambda l, j: (l, 0, j)),
            pl.BlockSpec((None, 1, NB_ADA), lambda l, j: (l, 0, j)),
        ],
        out_specs=pl.BlockSpec((None, 8, NB_ADA), lambda l, j: (l, 0, j)),
        out_shape=jax.ShapeDtypeStruct((DEPTH, 8, 6 * D_MODEL), F32),
        compiler_params=pltpu.CompilerParams(
            dimension_semantics=("arbitrary", "arbitrary"), vmem_limit_bytes=VMEM_LIMIT),
        name="modulation",
    )(cond8, w_ada, b_ada.reshape(DEPTH, 1, 6 * D_MODEL))


def _cast_kernel(w_ref, o_ref):
    o_ref[...] = w_ref[...].astype(BF16)


def _to_bf16(w, block_rows):
    depth, rows, cols = w.shape
    return pl.pallas_call(
        _cast_kernel,
        grid=(depth, rows // block_rows),
        in_specs=[pl.BlockSpec((None, block_rows, cols), lambda l, r: (l, r, 0))],
        out_specs=pl.BlockSpec((None, block_rows, cols), lambda l, r: (l, r, 0)),
        out_shape=jax.ShapeDtypeStruct(w.shape, BF16),
        compiler_params=pltpu.CompilerParams(dimension_semantics=("arbitrary", "arbitrary")),
        name="cast_bf16",
    )(w)


def _inproj_kernel(*refs, n_x):
    x_refs = refs[:n_x]
    mod_ref, w_ref, z_ref, h_scr = refs[n_x:]

    m = mod_ref[...]

    def build(side):
        h_scr[...] = (x_refs[side][...] * (1.0 + m[1:2]) + m[0:1]).astype(BF16)

    _per_half(pl.program_id(0), TM, n_x, build)
    h = h_scr[...]
    for jb in range(IN_WIDTH // NB_IN):
        cols = slice(jb * NB_IN, (jb + 1) * NB_IN)
        z_ref[:, cols] = jnp.dot(h, w_ref[:, cols].astype(BF16), preferred_element_type=F32)


def _inproj(xs, mod, w_in, l):
    return pl.pallas_call(
        functools.partial(_inproj_kernel, n_x=len(xs)),
        grid=(ROWS // TM,),
        in_specs=_tile_specs(len(xs), TM, 1) + [
            pl.BlockSpec((None, None, 6, D_MODEL), lambda i: (l, _cond_of_tile(i), 0, 0)),
            pl.BlockSpec((None, D_MODEL, IN_WIDTH), lambda i: (l, 0, 0), pipeline_mode=pl.Buffered(1)),
        ],
        out_specs=pl.BlockSpec((TM, IN_WIDTH), lambda i: (i, 0)),
        out_shape=jax.ShapeDtypeStruct((ROWS, IN_WIDTH), F32),
        scratch_shapes=[pltpu.VMEM((TM, D_MODEL), BF16)],
        compiler_params=pltpu.CompilerParams(
            dimension_semantics=("arbitrary",), vmem_limit_bytes=VMEM_LIMIT),
        name="inproj",
    )(*xs, mod, w_in)


TMF = 512
SEGF = TMF // 8
PITCHF = SEGF + 8
CTXF_TILES = N_CTX // TMF
LATF_PER_SEQ = DEC_SEQ // TMF
HALO_ROWS = 16
RBUF = 256
RB = 256
GB = 64
NFC = D_FF // FC
LANE_BLOCKS = D_MODEL // 128
assert RB == RBUF


def _seg_rows_f(xc_ref, k):
    return jnp.concatenate([xc_ref[cb, pl.ds(k, 8, stride=PITCHF), :] for cb in range(LANE_BLOCKS)], axis=1)


def _ffn_kernel(*refs, n_out):
    x_refs = refs[:2]
    (xp_ref, xn_ref, mod_ref, up_ref, cv_ref, dn_ref, ln_ref) = refs[2:9]
    o_refs = refs[9:9 + n_out]
    h_scr, act_scr, xc_scr, u0_scr, u1_scr = refs[9 + n_out:]
    u_scrs = (u0_scr, u1_scr)
    i = pl.program_id(0)
    is_ctx = i < CTXF_TILES
    lat_pos = (i - CTXF_TILES) % LATF_PER_SEQ
    m = mod_ref[...]
    ln = ln_ref[...]

    scale = 1.0 + m[4:5]
    shift = m[3:4]

    def stage(side):
        for cb in range(LANE_BLOCKS):
            for s in range(8):
                xc_scr[cb, s * PITCHF:s * PITCHF + SEGF, :] = x_refs[side][s * SEGF:(s + 1) * SEGF,
                                                                            cb * 128:(cb + 1) * 128]

    _per_half(i, TMF, 2, stage)

    def build_h(ub):
        for k in range(ub * RBUF // 8, (ub + 1) * RBUF // 8, 2):
            rows = jnp.concatenate([_seg_rows_f(xc_scr, k), _seg_rows_f(xc_scr, k + 1)], axis=0)
            h_scr[8 * k:8 * k + 16, :] = (rows * scale + shift).astype(BF16)

    sub16 = lax.broadcasted_iota(jnp.int32, (HALO_ROWS, D_MODEL), 0)
    prev_ok = jnp.logical_not(is_ctx) & (lat_pos > 0)
    next_ok = jnp.logical_not(is_ctx) & (lat_pos < LATF_PER_SEQ - 1)
    halo_x = jnp.where(sub16 == 0, xp_ref[POOL_HALO - 1:POOL_HALO, :], xn_ref[0:1, :])
    keep = ((sub16 == 0) & prev_ok) | ((sub16 == 1) & next_ok)
    h_scr[TMF:TMF + HALO_ROWS, :] = jnp.where(keep, halo_x * scale + shift, 0.0).astype(BF16)

    n_ub = TMF // RBUF
    sub = lax.broadcasted_iota(jnp.int32, (8, FC), 0)
    seg_per_seq = SEQ // SEGF
    ctx_first = is_ctx & (sub % seg_per_seq == 0)
    ctx_last = is_ctx & (sub % seg_per_seq == seg_per_seq - 1)

    def up_proj(slot, c, ub):
        rows = slice(ub * RBUF, (ub + 1) * RBUF + (HALO_ROWS if ub == n_ub - 1 else 0))
        u_scrs[slot][rows, :] = jnp.dot(h_scr[rows, :], up_ref[c], preferred_element_type=F32)

    def conv(u_ref, lanes, cvs, r0):
        lo = max(r0 - 8, 0)
        hi = min(r0 + GB + 8, TMF)
        ue = u_ref[lo:hi, lanes]
        u = ue[r0 - lo:r0 - lo + GB]
        if r0 == 0:
            b_first = jnp.where(sub == 0, u_ref[TMF:TMF + 1, lanes],
                                pltpu.roll(u_ref[TMF - 8:TMF, lanes], 1, axis=0))
            um1 = jnp.concatenate([jnp.where(ctx_first, 0.0, b_first), u[0:GB - 8]], axis=0)
        else:
            um1 = ue[0:GB]
        if r0 == TMF - GB:
            b_last = jnp.where(sub == 7, u_ref[TMF + 1:TMF + 2, lanes],
                               pltpu.roll(u_ref[0:8, lanes], 7, axis=0))
            up1 = jnp.concatenate([u[8:GB], jnp.where(ctx_last, 0.0, b_last)], axis=0)
        else:
            up1 = ue[r0 - lo + 8:r0 - lo + GB + 8]
        return um1 * cvs[0:1] + u * cvs[1:2] + up1 * cvs[2:3] + cvs[3:4]

    def gate(slot, c, ub):
        cva = cv_ref[0:4, c * FC:(c + 1) * FC]
        cvg = cv_ref[0:4, D_FF + c * FC:D_FF + (c + 1) * FC]
        for r0 in range(ub * RBUF, (ub + 1) * RBUF, GB):
            a = conv(u_scrs[slot], slice(0, FC), cva, r0)
            g = conv(u_scrs[slot], slice(FC, 2 * FC), cvg, r0)
            act_scr[c, r0:r0 + GB, :] = (_silu(a) * g).astype(BF16)

    def finish(rb):
        vrows = range(rb * RB // 8, (rb + 1) * RB // 8)
        lhs = jnp.concatenate([act_scr[c, rb * RB:(rb + 1) * RB, :] for c in range(NFC)], axis=1)
        y = jnp.dot(lhs, dn_ref[...], preferred_element_type=F32)
        xr = jnp.concatenate([_seg_rows_f(xc_scr, k) for k in vrows], axis=0)
        out = _layer_norm(ALPHA * xr + m[5:6] * y, ln[1:2], ln[3:4])
        for kk, k in enumerate(vrows):
            for cb in range(LANE_BLOCKS):
                xc_scr[cb, pl.ds(k, 8, stride=PITCHF), :] = out[8 * kk:8 * kk + 8, cb * 128:(cb + 1) * 128]

    def write_out(side):
        for cb in range(LANE_BLOCKS):
            for s in range(8):
                o_refs[side][s * SEGF:(s + 1) * SEGF, cb * 128:(cb + 1) * 128] = xc_scr[
                    cb, s * PITCHF:s * PITCHF + SEGF, :]

    build_h(0)
    for c in range(NFC + 1):
        for ub in range(n_ub):
            if c == 0 and ub > 0:
                build_h(ub)
            if c < NFC:
                up_proj(c % 2, c, ub)
            if c >= 1:
                gate((c - 1) % 2, c - 1, ub)
            if c == NFC:
                finish(ub)

    _per_half(i, TMF, n_out, write_out)


def _ffn(xs, mod, up_chunks, conv8, down_bf16, ln8, l, split_out):
    halo_blocks = TMF // POOL_HALO
    last_halo = N_LAT // POOL_HALO - 1
    n_out = 2 if split_out else 1
    if split_out:
        out_shape = [jax.ShapeDtypeStruct((N_CTX, D_MODEL), F32), jax.ShapeDtypeStruct((N_LAT, D_MODEL), F32)]
    else:
        out_shape = [jax.ShapeDtypeStruct((ROWS, D_MODEL), F32)]
    return pl.pallas_call(
        functools.partial(_ffn_kernel, n_out=n_out),
        grid=(ROWS // TMF,),
        in_specs=[
            *_tile_specs(2, TMF, 1),
            pl.BlockSpec((POOL_HALO, D_MODEL),
                         lambda i: (jnp.maximum((i - CTXF_TILES) * halo_blocks - 1, 0), 0)),
            pl.BlockSpec((POOL_HALO, D_MODEL),
                         lambda i: (jnp.clip((i - CTXF_TILES + 1) * halo_blocks, 0, last_halo), 0)),
            pl.BlockSpec((None, None, 6, D_MODEL), lambda i: (l, _cond_of_tile(i, TMF), 0, 0)),
            pl.BlockSpec((None, NFC, D_MODEL, 2 * FC), lambda i: (l, 0, 0, 0), pipeline_mode=pl.Buffered(1)),
            pl.BlockSpec((None, 8, 2 * D_FF), lambda i: (l, 0, 0)),
            pl.BlockSpec((None, D_FF, D_MODEL), lambda i: (l, 0, 0), pipeline_mode=pl.Buffered(1)),
            pl.BlockSpec((None, 8, D_MODEL), lambda i: (l, 0, 0)),
        ],
        out_specs=_tile_specs(n_out, TMF, 1),
        out_shape=out_shape,
        scratch_shapes=[pltpu.VMEM((TMF + HALO_ROWS, D_MODEL), BF16),
                        pltpu.VMEM((NFC, TMF, FC), BF16),
                        pltpu.VMEM((LANE_BLOCKS, 8 * PITCHF, 128), F32),
                        pltpu.VMEM((TMF + HALO_ROWS, 2 * FC), F32),
                        pltpu.VMEM((TMF + HALO_ROWS, 2 * FC), F32)],
        compiler_params=pltpu.CompilerParams(
            dimension_semantics=("arbitrary",), vmem_limit_bytes=VMEM_LIMIT),
        name="convffn",
    )(xs[0], xs[1], xs[1], xs[1], mod, up_chunks, conv8, down_bf16, ln8)


def _head_masks(width):
    lane = lax.broadcasted_iota(jnp.int32, (1, width), 1)
    return [(lane >= h * HEAD_DIM) & (lane < (h + 1) * HEAD_DIM) for h in range(width // HEAD_DIM)]


def _stack_heads(x, masks):
    return jnp.concatenate([jnp.where(m, x, jnp.zeros_like(x)) for m in masks], axis=0)


def _rope(x, cos, sin):
    lane = lax.broadcasted_iota(jnp.int32, (1, 128), 1)
    lower = (lane & 31) < 16
    outs = []
    for k in range(x.shape[1] // 128):
        xb = x[:, k * 128:(k + 1) * 128]
        partner = jnp.where(lower, pltpu.roll(xb, 112, axis=1), pltpu.roll(xb, 16, axis=1))
        outs.append(xb * cos + partner * sin)
    return outs[0] if len(outs) == 1 else jnp.concatenate(outs, axis=1)


def _load_state(state_scr, blocks_ref, d):
    state_scr[...] = jnp.zeros((GW, GW), F32)
    for h in range(N_HEADS):
        sl = slice(h * HEAD_DIM, (h + 1) * HEAD_DIM)
        state_scr[sl, sl] = blocks_ref[d, h]


def _store_state(blocks_ref, d, state_scr):
    for h in range(N_HEADS):
        sl = slice(h * HEAD_DIM, (h + 1) * HEAD_DIM)
        blocks_ref[d, h] = state_scr[sl, sl]


LAT_GROUP = 4


def _chunk_off(c):
    return c * CHUNK if isinstance(c, int) else pl.multiple_of(c * CHUNK, CHUNK)


def _interleave(chains):
    chains = list(chains)
    while chains:
        for ch in list(chains):
            try:
                next(ch)
            except StopIteration:
                chains.remove(ch)


def _mixer_kernel(*refs, latent, group, ng, cast_up=False):
    if latent:
        (qa_ref, ka_ref, va_ref, kx_ref, vx_ref, ub_ref, vb_ref, qc_ref, kc_ref, vc_ref, gf_ref, gb_ref,
         pd_ref, pdp_ref, pdn_ref, cos_ref, sin_ref, s0_ref, dmat_ref, dec_ref, sink_ref, ws_ref, bias_ref,
         vec_ref, gmat_ref, cnt_ref, wpool_ref, xres_ref, mod_ref, wout_ref, ln_ref,
         x1_ref, sf_scr, sb_scr, ob_scr, pext_scr, mix_ref, k_scr, v_scr) = refs
    else:
        (qa_ref, ka_ref, va_ref, ub_ref, vb_ref, qc_ref, kc_ref, vc_ref, gf_ref, gb_ref, pd_ref,
         dmat_ref, dec_ref, sink_ref, ws_ref, bias_ref, vec_ref, gmat_ref, cnt_ref, wpool_ref,
         xres_ref, mod_ref, wout_ref, ln_ref) = refs[:24]
        if cast_up:
            upa_ref, upg_ref, dnw_ref, x1_ref, st_ref, kn_ref, vn_ref, upo_ref, dno_ref = refs[24:33]
        else:
            x1_ref, st_ref, kn_ref, vn_ref = refs[24:28]
        sf_scr, sb_scr, ob_scr, pext_scr, mix_ref = refs[-5:]

    nc = group * ng
    g_a = mod_ref[...][2:3]
    ln = ln_ref[...]
    masks = _head_masks(GW)
    row = lax.broadcasted_iota(jnp.int32, (GW, GW), 0)
    col = lax.broadcasted_iota(jnp.int32, (GW, GW), 1)
    bd_mask = (row // HEAD_DIM) == (col // HEAD_DIM)
    vec = vec_ref[...]
    lane = lax.broadcasted_iota(jnp.int32, (1, 2 * HEAD_DIM), 1)
    lo = lane < HEAD_DIM

    def rows(k):
        return slice(k * CHUNK, (k + 1) * CHUNK)

    def retention(order, d, state_scr, out):
        dmat = dmat_ref[d]
        qdec, kdec = dec_ref[d], dec_ref[2 + d]
        cdec = vec[5 + d:6 + d]
        state = state_scr[...]
        for k in order:
            q, kk, v = qc_ref[rows(k), :], kc_ref[rows(k), :], vc_ref[rows(k), :]
            qb, kb, vb = q.astype(BF16), kk.astype(BF16), v.astype(BF16)
            s = lax.dot_general(_stack_heads(qb, masks), kb, (((1,), (1,)), ((), ())),
                                preferred_element_type=F32)
            yield
            p = (s * dmat).astype(BF16)
            p_cat = jnp.concatenate([p[h * CHUNK:(h + 1) * CHUNK] for h in range(N_HEADS)], axis=1)
            o = (jnp.dot(p_cat, _stack_heads(vb, masks), preferred_element_type=F32)
                 + _bdot(q * qdec, state))
            yield
            upd = lax.dot_general((kk * kdec).astype(BF16), vb, (((0,), (0,)), ((), ())),
                                  preferred_element_type=F32)
            state = state * cdec + jnp.where(bd_mask, upd, 0.0)
            out[k] = o
            yield
        state_scr[...] = state

    def stage_kv(k, c):
        dst = pl.ds(_chunk_off(c + 1), CHUNK)
        k_scr[dst, :] = _rope(ka_ref[rows(k), :], cos_ref[rows(k), :], sin_ref[rows(k), :]).astype(BF16)
        yield
        v_scr[dst, :] = va_ref[rows(k), :].astype(BF16)
        yield

    def attention(k, c, done):
        q = qa_ref[rows(k), :]
        if latent:
            q = _rope(q, cos_ref[rows(k), :], sin_ref[rows(k), :])
        q = q * (HEAD_DIM ** -0.5)
        q0, q1 = q[:, :128], q[:, 128:]
        zero = jnp.zeros_like(q0)
        q_st = jnp.concatenate([
            jnp.where(lo, q0, zero),
            jnp.where(lo, pltpu.roll(q0, HEAD_DIM, axis=1), zero),
            jnp.where(lo, zero, pltpu.roll(q1, HEAD_DIM, axis=1)),
            jnp.where(lo, zero, q1)], axis=0).astype(BF16)
        if latent:
            band = pl.ds(_chunk_off(c), 3 * CHUNK)
            k_all = jnp.concatenate([k_scr[band, :], kx_ref[...].astype(BF16)], axis=0)
            v_all = jnp.concatenate([v_scr[band, :], vx_ref[...].astype(BF16)], axis=0)
        else:
            k_all = ka_ref[...].astype(BF16)
            v_all = va_ref[...].astype(BF16)
        s = lax.dot_general(q_st, k_all, (((1,), (1,)), ((), ())), preferred_element_type=F32)
        yield
        if latent:
            nk = 3 * CHUNK + PAST_LEN
            qi = lax.broadcasted_iota(jnp.int32, (N_HEADS * CHUNK, nk), 0) & (CHUNK - 1)
            kj = lax.broadcasted_iota(jnp.int32, (N_HEADS * CHUNK, nk), 1)
            kpos = kj + (c - 1) * CHUNK
            valid = (kj >= 3 * CHUNK) | ((kj >= qi) & (kj <= qi + 2 * CHUNK)
                                         & (kpos >= 0) & (kpos < nc * CHUNK))
            s = jnp.where(valid, s, NEG_INF)
        sink = sink_ref[...][:, 0:1]
        mx = jnp.maximum(jnp.max(s, axis=-1, keepdims=True), sink)
        yield
        e = jnp.exp(s - mx)
        den = jnp.sum(e, axis=-1, keepdims=True) + jnp.exp(sink - mx)
        yield
        o = jnp.dot(e.astype(BF16), v_all, preferred_element_type=F32) / den
        yield
        mix_ref[rows(k), 0:128] = jnp.where(lo, o[0:CHUNK], pltpu.roll(o[CHUNK:2 * CHUNK], HEAD_DIM, axis=1))
        mix_ref[rows(k), 128:256] = jnp.where(lo, pltpu.roll(o[2 * CHUNK:3 * CHUNK], HEAD_DIM, axis=1),
                                              o[3 * CHUNK:4 * CHUNK])
        done.add(("A", k))
        yield

    def gating_unit(k, done):
        vn = _layer_norm(vb_ref[rows(k), :], vec[0:1], vec[1:2]).astype(BF16)
        yield
        sg = jnp.dot(ws_ref[...], _stack_heads(vn, masks), preferred_element_type=F32) + bias_ref[...]
        yield
        mix_ref[rows(k), GW:2 * GW] = ub_ref[rows(k), :] * sg
        done.add(("B", k))
        yield

    def retention_mix(k, o_fwd, o_bwd, done):
        while k not in o_fwd or k not in o_bwd:
            yield
        gmat = gmat_ref[...]
        normed = []
        for o in (o_fwd[k], o_bwd[k]):
            mu = jnp.dot(o.astype(BF16), gmat, preferred_element_type=F32)
            yield
            dlt = o - mu
            var = jnp.dot((dlt * dlt).astype(BF16), gmat, preferred_element_type=F32)
            yield
            normed.append(dlt * lax.rsqrt(var + LN_EPS))
        mix_ref[rows(k), 2 * GW:3 * GW] = (_silu(gf_ref[rows(k), :]) * (normed[0] * vec[3:4])
                                           + _silu(gb_ref[rows(k), :]) * (normed[1] * vec[4:5]))
        done.add(("C", k))
        yield

    def pooling(k, c, done):
        pd = pd_ref[rows(k), :]
        pext = pext_scr.at[k]
        zeros8 = jnp.zeros((POOL_HALO, GW), F32)
        if k > 0:
            prev8 = pd_ref[k * CHUNK - POOL_HALO:k * CHUNK, :]
        else:
            prev8 = jnp.where(c > 0, pdp_ref[...], 0.0) if latent else zeros8
        if k < group - 1:
            next8 = pd_ref[(k + 1) * CHUNK:(k + 1) * CHUNK + POOL_HALO, :]
        else:
            next8 = jnp.where(c < nc - 1, pdn_ref[...], 0.0) if latent else zeros8
        pext[0:POOL_HALO, :] = prev8
        pext[POOL_HALO:POOL_HALO + CHUNK, :] = pd
        pext[POOL_HALO + CHUNK:2 * POOL_HALO + CHUNK, :] = next8
        yield

        def win(d, half):
            return pext[pl.ds(POOL_HALO + d, CHUNK), half * 128:(half + 1) * 128]

        a2 = win(-1, 0) + win(0, 0)
        a4 = a2 + win(-2, 0) + win(1, 0)
        yield
        a8 = win(-4, 1)
        for d in range(-3, 4):
            a8 = a8 + win(d, 1)
        yield
        a16 = a8
        for d in list(range(-8, -4)) + list(range(4, 8)):
            a16 = a16 + win(d, 1)
        yield
        sums = jnp.concatenate([jnp.where(lo, a2, a4), jnp.where(lo, a8, a16)], axis=1)
        yd = sums * cnt_ref[rows(k), :] - pd
        mix_ref[rows(k), 3 * GW:4 * GW] = _bdot(yd, wpool_ref[...]) * vec[2:3]
        done.add(("D", k))
        yield

    def out_projection(k, done):
        while not all((m, k) in done for m in "ABCD"):
            yield
        y = jnp.dot(mix_ref[rows(k), :].astype(BF16), wout_ref[...], preferred_element_type=F32)
        yield
        x1_ref[rows(k), :] = _layer_norm(ALPHA * xres_ref[rows(k), :] + g_a * y, ln[0:1], ln[2:3])
        yield

    def up_weight_rows():
        for lyr in range(DEPTH):
            for cc in range(NFC):
                upo_ref[lyr, cc, :, 0:FC] = upa_ref[lyr, :, cc * FC:(cc + 1) * FC].astype(BF16)
                upo_ref[lyr, cc, :, FC:2 * FC] = upg_ref[lyr, :, cc * FC:(cc + 1) * FC].astype(BF16)
                yield
            dno_ref[lyr] = dnw_ref[lyr].astype(BF16)
            yield

    def forward_chains(chunk_of, o_bwd):
        done, o_fwd = set(), {}
        chains = [retention(range(group), 0, sf_scr, o_fwd)]
        for k in range(group):
            chains += [attention(k, chunk_of(k), done), gating_unit(k, done), pooling(k, chunk_of(k), done),
                       retention_mix(k, o_fwd, o_bwd, done), out_projection(k, done)]
        return chains

    if not latent:
        sb_scr[...] = jnp.zeros((GW, GW), F32)
        sf_scr[...] = jnp.zeros((GW, GW), F32)
        kn_ref[...] = ka_ref[...]
        vn_ref[...] = va_ref[...]
        o_bwd = {}
        side_chains = [retention(reversed(range(group)), 1, sb_scr, o_bwd)]
        if cast_up:
            side_chains.append(up_weight_rows())
        _interleave(side_chains + forward_chains(lambda k: k, o_bwd))
        _store_state(st_ref, 1, sb_scr)
        _store_state(st_ref, 0, sf_scr)
        return

    p = pl.program_id(1)
    g = pl.program_id(2)

    @pl.when(p == 0)
    def _():
        @pl.when(g == 0)
        def _():
            _load_state(sb_scr, s0_ref, 1)
            zero_blk = jnp.zeros((CHUNK, 2 * HEAD_DIM), BF16)
            for scr in (k_scr, v_scr):
                scr[0:CHUNK, :] = zero_blk
                scr[(nc + 1) * CHUNK:(nc + 2) * CHUNK, :] = zero_blk

        first = (ng - 1 - g) * group
        o_bwd = {}
        _interleave([retention(reversed(range(group)), 1, sb_scr, o_bwd)]
                    + [stage_kv(k, first + k) for k in range(group)])
        for k in range(group):
            ob_scr[pl.ds(_chunk_off(first + k), CHUNK), :] = o_bwd[k]

    @pl.when(p == 1)
    def _():
        @pl.when(g == 0)
        def _():
            _load_state(sf_scr, s0_ref, 0)

        first = g * group
        o_bwd = {k: ob_scr[pl.ds(_chunk_off(first + k), CHUNK), :] for k in range(group)}
        _interleave(forward_chains(lambda k: first + k, o_bwd))


def _mixer(z, x_res, mod, w_out_bf16, ln8, tabs, l, latent, extra=None, ffn_up=None, ffn_down=None):
    nb = DEC_BATCH if latent else BATCH
    nc = (DEC_SEQ if latent else SEQ) // CHUNK
    group = LAT_GROUP if latent else nc
    ng = nc // group
    blk = group * CHUNK
    base = (N_CTX // blk) if latent else 0
    per8 = blk // POOL_HALO
    last_halo = ROWS // POOL_HALO - 1

    def on_grid(f):
        return (lambda b, p, g: f(b, p, g)) if latent else (lambda b: f(b, 1, 0))

    def fwd(b, p, g):
        return base + b * ng + g * p

    def both(b, p, g):
        return base + b * ng + jnp.where(p == 0, ng - 1 - g, g)

    def bwd_only(b, p, g):
        return base + b * ng + (ng - 1 - g) * (1 - p)

    def col(width, idx, rowmap):
        return pl.BlockSpec((blk, width), on_grid(lambda b, p, g: (rowmap(b, p, g), idx)))

    def const(shape):
        return pl.BlockSpec(shape, on_grid(lambda b, p, g: (0,) * len(shape)))

    def layer(shape):
        return pl.BlockSpec((None,) + shape, on_grid(lambda b, p, g: (l,) + (0,) * len(shape)))

    specs, args = [], []

    def add(spec, arr):
        specs.append(spec)
        args.append(arr)

    add(col(GW, 0, fwd), z)
    add(col(128, 2, bwd_only if latent else fwd), z)
    add(col(128, 3, bwd_only if latent else fwd), z)
    if latent:
        add(pl.BlockSpec((None, None, PAST_LEN, 128), lambda b, p, g: (b, l, 0, 0)), extra["ck"])
        add(pl.BlockSpec((None, None, PAST_LEN, 128), lambda b, p, g: (b, l, 0, 0)), extra["cv"])
    add(col(GW, 2, fwd), z)
    add(col(GW, 3, fwd), z)
    add(col(GW, 4, both), z)
    add(col(GW, 5, both), z)
    add(col(GW, 6, both), z)
    add(col(GW, 7, fwd), z)
    add(col(GW, 8, fwd), z)
    add(col(GW, 9, fwd), z)
    if latent:
        add(pl.BlockSpec((POOL_HALO, GW),
                         lambda b, p, g: (jnp.maximum(fwd(b, p, g) * per8 - 1, 0), 9)), z)
        add(pl.BlockSpec((POOL_HALO, GW),
                         lambda b, p, g: (jnp.minimum((fwd(b, p, g) + 1) * per8, last_halo), 9)), z)
        rope_map = lambda b, p, g: (jnp.where(p == 0, ng - 1 - g, g), 0)
        add(pl.BlockSpec((blk, 128), rope_map), extra["cos"])
        add(pl.BlockSpec((blk, 128), rope_map), extra["sin"])
        add(pl.BlockSpec((None, None, 2, N_HEADS, HEAD_DIM, HEAD_DIM),
                         lambda b, p, g: (b, l, 0, 0, 0, 0)), extra["s0"])
    add(layer((2, N_HEADS * CHUNK, CHUNK)), tabs["dmat"])
    add(layer((4, CHUNK, GW)), tabs["dec"])
    add(layer((N_HEADS * CHUNK, 128)), tabs["sink"])
    add(layer((CHUNK, N_HEADS * CHUNK)), tabs["ws"])
    add(layer((CHUNK, GW)), tabs["bias"])
    add(layer((8, GW)), tabs["vec"])
    add(const((GW, GW)), tabs["gmat"])
    add(pl.BlockSpec((blk, GW), on_grid(lambda b, p, g: (g * p, 0))),
        tabs["cnt_lat"] if latent else tabs["cnt_ctx"])
    add(layer((GW, GW)), tabs["wpool"])
    local = lambda b, p, g: (b * ng + g * p, 0)
    add(pl.BlockSpec((blk, D_MODEL), on_grid(local)), x_res)
    add(pl.BlockSpec((None, None, 6, D_MODEL),
                     on_grid(lambda b, p, g: (l, (1 + b) if latent else 0, 0, 0))), mod)
    add(layer((D_MODEL, D_MODEL)), w_out_bf16)
    add(layer((8, D_MODEL)), ln8)
    up_rows = D_MODEL // nb
    if ffn_up is not None:
        add(pl.BlockSpec((DEPTH, up_rows, D_FF), lambda b: (0, b, 0)), ffn_up)
        add(pl.BlockSpec((DEPTH, up_rows, D_FF), lambda b: (0, b, 1)), ffn_up)
        add(pl.BlockSpec((DEPTH, D_FF // nb, D_MODEL), lambda b: (0, b, 0)), ffn_down)

    out_shape = [jax.ShapeDtypeStruct((nb * nc * CHUNK, D_MODEL), F32)]
    out_specs = [pl.BlockSpec((blk, D_MODEL), on_grid(local))]
    scratch = [pltpu.VMEM((GW, GW), F32), pltpu.VMEM((GW, GW), F32),
               pltpu.VMEM((nc * CHUNK, GW), F32),
               pltpu.VMEM((group, CHUNK + 2 * POOL_HALO, GW), F32),
               pltpu.VMEM((blk, D_MODEL), F32)]
    if latent:
        scratch += [pltpu.VMEM(((nc + 2) * CHUNK, 128), BF16), pltpu.VMEM(((nc + 2) * CHUNK, 128), BF16)]
    else:
        out_shape.append(jax.ShapeDtypeStruct((nb, 2, N_HEADS, HEAD_DIM, HEAD_DIM), F32))
        out_specs.append(pl.BlockSpec((None, 2, N_HEADS, HEAD_DIM, HEAD_DIM), lambda b: (b, 0, 0, 0, 0)))
        for _ in range(2):
            out_shape.append(jax.ShapeDtypeStruct((nb, SEQ, 128), F32))
            out_specs.append(pl.BlockSpec((None, SEQ, 128), lambda b: (b, 0, 0)))
        if ffn_up is not None:
            out_shape.append(jax.ShapeDtypeStruct((DEPTH, NFC, D_MODEL, 2 * FC), BF16))
            out_specs.append(pl.BlockSpec((DEPTH, NFC, up_rows, 2 * FC), lambda b: (0, 0, b, 0)))
            out_shape.append(jax.ShapeDtypeStruct((DEPTH, D_FF, D_MODEL), BF16))
            out_specs.append(pl.BlockSpec((DEPTH, D_FF // nb, D_MODEL), lambda b: (0, b, 0)))

    return pl.pallas_call(
        functools.partial(_mixer_kernel, latent=latent, group=group, ng=ng, cast_up=ffn_up is not None),
        grid=(nb, 2, ng) if latent else (nb,),
        in_specs=specs,
        out_specs=out_specs,
        out_shape=out_shape,
        scratch_shapes=scratch,
        compiler_params=pltpu.CompilerParams(
            dimension_semantics=("arbitrary",) * (3 if latent else 1), vmem_limit_bytes=VMEM_LIMIT),
        name="mixer_latent" if latent else "mixer_context",
    )(*args)


def _pad_rows(rows, n=8):
    a = jnp.stack(rows)
    return jnp.concatenate([a, jnp.zeros((n - a.shape[0],) + a.shape[1:], a.dtype)], axis=0)


def _block_diag(blocks):
    g, n, _ = blocks.shape
    eye = jnp.eye(g, dtype=blocks.dtype)
    return (eye[:, None, :, None] * blocks[:, :, None, :]).reshape(g * n, g * n)


def _inv_count(n):
    t = np.arange(n)
    cols = []
    for w in POOL_WINDOWS:
        cnt = np.clip(t + w // 2, 0, n) - np.clip(t - w // 2, 0, n)
        cols.append(np.repeat((1.0 / cnt)[:, None], HEAD_DIM, axis=1))
    return jnp.asarray(np.concatenate(cols, axis=1), F32)


def _rope_tables():
    rows = DEC_SEQ // GRID_W
    r, cc = jnp.meshgrid(jnp.arange(rows), jnp.arange(GRID_W), indexing="ij")
    half = HEAD_DIM // 2
    freqs = ROPE_BASE ** (-jnp.arange(0, half, 2, dtype=F32) / half)

    def tables(pos):
        ang = pos.reshape(-1).astype(F32)[:, None] * freqs[None, :]
        cos, sin = jnp.cos(ang), jnp.sin(ang)
        return jnp.concatenate([cos, cos], axis=1), jnp.concatenate([-sin, sin], axis=1)

    cr, sr = tables(r)
    ccol, scol = tables(cc)
    cos = jnp.concatenate([cr, ccol], axis=1)
    sin = jnp.concatenate([sr, scol], axis=1)
    return jnp.tile(cos, (1, 2)), jnp.tile(sin, (1, 2))


def _layer_tables(attn_sink, sgu_norm_w, sgu_norm_b, sgu_ws, sgu_bs, ret_decay, ret_gn_w, pool_w, pool_scale):
    log_g = jax.nn.log_sigmoid(ret_decay.astype(F32))
    i = jnp.arange(CHUNK, dtype=F32)
    rel = i[:, None] - i[None, :]
    kscale = HEAD_DIM ** -0.5
    d_f = jnp.where(rel >= 0, jnp.exp(jnp.maximum(rel, 0.0)[None] * log_g[0][:, None, None]), 0.0)
    d_b = jnp.where(rel <= 0, jnp.exp(jnp.maximum(-rel, 0.0)[None] * log_g[1][:, None, None]), 0.0)
    dmat = jnp.stack([d_f.reshape(N_HEADS * CHUNK, CHUNK), d_b.reshape(N_HEADS * CHUNK, CHUNK)]) * kscale

    def lanes(per_head):
        return jnp.repeat(per_head, HEAD_DIM, axis=1)

    qdec_f = lanes(jnp.exp((i + 1.0)[:, None] * log_g[0][None, :]))
    qdec_b = lanes(jnp.exp((CHUNK - i)[:, None] * log_g[1][None, :]))
    kdec_f = lanes(jnp.exp((CHUNK - 1.0 - i)[:, None] * log_g[0][None, :])) * kscale
    kdec_b = lanes(jnp.exp(i[:, None] * log_g[1][None, :])) * kscale
    cdec = jnp.repeat(jnp.exp(CHUNK * log_g), HEAD_DIM, axis=1)
    vec = _pad_rows([sgu_norm_w, sgu_norm_b, pool_scale, ret_gn_w[0], ret_gn_w[1], cdec[0], cdec[1]])
    return {
        "dmat": dmat,
        "dec": jnp.stack([qdec_f, qdec_b, kdec_f, kdec_b]),
        "sink": jnp.broadcast_to(jnp.repeat(attn_sink, CHUNK)[:, None], (N_HEADS * CHUNK, 128)),
        "ws": jnp.concatenate([sgu_ws[h] for h in range(N_HEADS)], axis=1).astype(BF16),
        "bias": jnp.repeat(sgu_bs.T, HEAD_DIM, axis=1),
        "vec": vec,
        "wpool": _block_diag(pool_w).astype(BF16),
    }


def kernel(x_prompt, x_sample, cache_attn_k, cache_attn_v, state_ret, c, c_ctx, w_ada, b_ada, w_in,
           w_out, attn_sink, sgu_norm_w, sgu_norm_b, sgu_ws, sgu_bs, ret_decay, ret_gn_w, pool_w,
           pool_scale, ffn_up, ffn_conv_w, ffn_conv_b, ffn_down, ln_w, ln_b):
    cond8 = jnp.concatenate([c_ctx[None], c, jnp.zeros((8 - 1 - DEC_BATCH, D_MODEL), F32)], axis=0)
    mod = _modulation(cond8, w_ada, b_ada).reshape(DEPTH, 8, 6, D_MODEL)

    tabs = jax.vmap(_layer_tables)(attn_sink, sgu_norm_w, sgu_norm_b, sgu_ws, sgu_bs, ret_decay, ret_gn_w,
                                   pool_w, pool_scale)
    tabs["gmat"] = _block_diag(jnp.full((N_HEADS, HEAD_DIM, HEAD_DIM), 1.0 / HEAD_DIM, F32)).astype(BF16)
    tabs["cnt_ctx"] = _inv_count(SEQ)
    tabs["cnt_lat"] = _inv_count(DEC_SEQ)
    cos, sin = _rope_tables()
    extra = {"ck": cache_attn_k.reshape(DEC_BATCH, DEPTH, PAST_LEN, 128),
             "cv": cache_attn_v.reshape(DEC_BATCH, DEPTH, PAST_LEN, 128),
             "cos": cos, "sin": sin, "s0": state_ret}
    ln8 = jnp.concatenate([ln_w, ln_b, jnp.zeros((DEPTH, 4, D_MODEL), F32)], axis=1)
    conv8 = jnp.concatenate([ffn_conv_w, ffn_conv_b[:, None], jnp.zeros((DEPTH, 4, 2 * D_FF), F32)], axis=1)

    w_out_bf16 = _to_bf16(w_out, D_MODEL)

    xs = [x_prompt.reshape(N_CTX, D_MODEL), x_sample.reshape(N_LAT, D_MODEL)]
    new_k, new_v, new_s = [], [], []
    for l in range(DEPTH):
        z = _inproj(xs, mod, w_in, l)
        if l == 0:
            x1_ctx, st, kn, vn, up_chunks, down_bf16 = _mixer(z, xs[0], mod, w_out_bf16, ln8, tabs, l,
                                                              latent=False, ffn_up=ffn_up, ffn_down=ffn_down)
        else:
            x1_ctx, st, kn, vn = _mixer(z, xs[0], mod, w_out_bf16, ln8, tabs, l, latent=False)
        (x1_lat,) = _mixer(z, xs[1], mod, w_out_bf16, ln8, tabs, l, latent=True, extra=extra)
        xs = _ffn([x1_ctx, x1_lat], mod, up_chunks, conv8, down_bf16, ln8, l, split_out=True)
        new_k.append(kn.reshape(BATCH, SEQ, 2, HEAD_DIM))
        new_v.append(vn.reshape(BATCH, SEQ, 2, HEAD_DIM))
        new_s.append(st)

    y_prompt = xs[0].reshape(BATCH, SEQ, D_MODEL)
    y_sample = xs[1].reshape(DEC_BATCH, DEC_SEQ, D_MODEL)
    return (y_prompt, y_sample, jnp.stack(new_k, axis=1), jnp.stack(new_v, axis=1),
            jnp.stack(new_s, axis=1))
```

```python
import functools

import numpy as np
import jax
import jax.numpy as jnp
from jax import lax
from jax.experimental import pallas as pl
from jax.experimental.pallas import tpu as pltpu

F32 = jnp.float32
BF16 = jnp.bfloat16

D_MODEL = 1024
BATCH = 16
SEQ = 256
DEPTH = 2
DEC_BATCH = 2
DEC_SEQ = 2048
PAST_LEN = 256
GRID_W = 64
CHUNK = 128
HEAD_DIM = 64
GW = D_MODEL // 4
N_HEADS = 4
POOL_WINDOWS = (2, 4, 8, 16)
POOL_HALO = 8
D_FF = 2816
ROPE_BASE = 10000.0
LN_EPS = 1e-5
NEG_INF = -1e30
IN_WIDTH = 10 * GW
ALPHA = (2.0 * DEPTH) ** 0.25

N_CTX = BATCH * SEQ
N_LAT = DEC_BATCH * DEC_SEQ
ROWS = N_CTX + N_LAT

TM = 1024
NB_IN = 512
FC = 256
NB_ADA = 1536
VMEM_LIMIT = 56 * 1024 * 1024


def _cond_of_tile(i, tm=TM):
    ctx_tiles = N_CTX // tm
    return jnp.where(i < ctx_tiles, 0, 1 + (i - ctx_tiles) // (DEC_SEQ // tm))


def _tile_specs(n_src, tm, grid_rank):
    ctx_tiles = N_CTX // tm
    if n_src == 1:
        rows = [lambda i: i]
    else:
        rows = [lambda i: jnp.minimum(i, ctx_tiles - 1), lambda i: jnp.maximum(i - ctx_tiles, 0)]
    if grid_rank == 1:
        return [pl.BlockSpec((tm, D_MODEL), lambda i, f=f: (f(i), 0)) for f in rows]
    return [pl.BlockSpec((tm, D_MODEL), lambda i, j, f=f: (f(i), 0)) for f in rows]


def _per_half(i, tm, n_max, fn):
    if n_max == 1:
        fn(0)
        return
    ctx_tiles = N_CTX // tm

    @pl.when(i < ctx_tiles)
    def _():
        fn(0)

    @pl.when(i >= ctx_tiles)
    def _():
        fn(1)


def _layer_norm(x, w, b):
    mu = jnp.mean(x, axis=-1, keepdims=True)
    d = x - mu
    var = jnp.mean(d * d, axis=-1, keepdims=True)
    return d * lax.rsqrt(var + LN_EPS) * w + b


def _silu(x):
    return x * jax.nn.sigmoid(x)


def _bdot(a, b):
    return jnp.dot(a.astype(BF16), b.astype(BF16), preferred_element_type=F32)


def _mod_kernel(c_ref, w_ref, b_ref, o_ref):
    o_ref[...] = _bdot(_silu(c_ref[...]), w_ref[...]) + b_ref[...]


def _modulation(cond8, w_ada, b_ada):
    return pl.pallas_call(
        _mod_kernel,
        grid=(DEPTH, 6 * D_MODEL // NB_ADA),
        in_specs=[
            pl.BlockSpec((8, D_MODEL), lambda l, j: (0, 0)),
            pl.BlockSpec((None, D_MODEL, NB_ADA), lambda l, j: (l, 0, j)),
            pl.BlockSpec((None, 1, NB_ADA), lambda l, j: (l, 0, j)),
        ],
        out_specs=pl.BlockSpec((None, 8, NB_ADA), lambda l, j: (l, 0, j)),
        out_shape=jax.ShapeDtypeStruct((DEPTH, 8, 6 * D_MODEL), F32),
        compiler_params=pltpu.CompilerParams(
            dimension_semantics=("arbitrary", "arbitrary"), vmem_limit_bytes=VMEM_LIMIT),
        name="modulation",
    )(cond8, w_ada, b_ada.reshape(DEPTH, 1, 6 * D_MODEL))


def _cast_kernel(w_ref, o_ref):
    o_ref[...] = w_ref[...].astype(BF16)


def _to_bf16(w, block_rows):
    depth, rows, cols = w.shape
    return pl.pallas_call(
        _cast_kernel,
        grid=(depth, rows // block_rows),
        in_specs=[pl.BlockSpec((None, block_rows, cols), lambda l, r: (l, r, 0))],
        out_specs=pl.BlockSpec((None, block_rows, cols), lambda l, r: (l, r, 0)),
        out_shape=jax.ShapeDtypeStruct(w.shape, BF16),
        compiler_params=pltpu.CompilerParams(dimension_semantics=("arbitrary", "arbitrary")),
        name="cast_bf16",
    )(w)


def _inproj_kernel(*refs, n_x):
    x_refs = refs[:n_x]
    mod_ref, w_ref, z_ref, h_scr = refs[n_x:]

    m = mod_ref[...]

    def build(side):
        h_scr[...] = (x_refs[side][...] * (1.0 + m[1:2]) + m[0:1]).astype(BF16)

    _per_half(pl.program_id(0), TM, n_x, build)
    h = h_scr[...]
    for jb in range(IN_WIDTH // NB_IN):
        cols = slice(jb * NB_IN, (jb + 1) * NB_IN)
        z_ref[:, cols] = jnp.dot(h, w_ref[:, cols].astype(BF16), preferred_element_type=F32)


def _inproj(xs, mod, w_in, l):
    return pl.pallas_call(
        functools.partial(_inproj_kernel, n_x=len(xs)),
        grid=(ROWS // TM,),
        in_specs=_tile_specs(len(xs), TM, 1) + [
            pl.BlockSpec((None, None, 6, D_MODEL), lambda i: (l, _cond_of_tile(i), 0, 0)),
            pl.BlockSpec((None, D_MODEL, IN_WIDTH), lambda i: (l, 0, 0), pipeline_mode=pl.Buffered(1)),
        ],
        out_specs=pl.BlockSpec((TM, IN_WIDTH), lambda i: (i, 0)),
        out_shape=jax.ShapeDtypeStruct((ROWS, IN_WIDTH), F32),
        scratch_shapes=[pltpu.VMEM((TM, D_MODEL), BF16)],
        compiler_params=pltpu.CompilerParams(
            dimension_semantics=("arbitrary",), vmem_limit_bytes=VMEM_LIMIT),
        name="inproj",
    )(*xs, mod, w_in)


TMF = 512
SEGF = TMF // 8
PITCHF = SEGF + 8
CTXF_TILES = N_CTX // TMF
LATF_PER_SEQ = DEC_SEQ // TMF
HALO_ROWS = 16
RBUF = 256
RB = 256
GB = 64
NFC = D_FF // FC
LANE_BLOCKS = D_MODEL // 128
assert RB == RBUF


def _seg_rows_f(xc_ref, k):
    return jnp.concatenate([xc_ref[cb, pl.ds(k, 8, stride=PITCHF), :] for cb in range(LANE_BLOCKS)], axis=1)


def _ffn_kernel(*refs, n_out):
    x_refs = refs[:2]
    (xp_ref, xn_ref, mod_ref, up_ref, cv_ref, dn_ref, ln_ref) = refs[2:9]
    o_refs = refs[9:9 + n_out]
    h_scr, act_scr, xc_scr, u0_scr, u1_scr = refs[9 + n_out:]
    u_scrs = (u0_scr, u1_scr)
    i = pl.program_id(0)
    is_ctx = i < CTXF_TILES
    lat_pos = (i - CTXF_TILES) % LATF_PER_SEQ
    m = mod_ref[...]
    ln = ln_ref[...]

    scale = 1.0 + m[4:5]
    shift = m[3:4]

    def stage(side):
        for cb in range(LANE_BLOCKS):
            for s in range(8):
                xc_scr[cb, s * PITCHF:s * PITCHF + SEGF, :] = x_refs[side][s * SEGF:(s + 1) * SEGF,
                                                                            cb * 128:(cb + 1) * 128]

    _per_half(i, TMF, 2, stage)

    def build_h(ub):
        for k in range(ub * RBUF // 8, (ub + 1) * RBUF // 8, 2):
            rows = jnp.concatenate([_seg_rows_f(xc_scr, k), _seg_rows_f(xc_scr, k + 1)], axis=0)
            h_scr[8 * k:8 * k + 16, :] = (rows * scale + shift).astype(BF16)

    sub16 = lax.broadcasted_iota(jnp.int32, (HALO_ROWS, D_MODEL), 0)
    prev_ok = jnp.logical_not(is_ctx) & (lat_pos > 0)
    next_ok = jnp.logical_not(is_ctx) & (lat_pos < LATF_PER_SEQ - 1)
    halo_x = jnp.where(sub16 == 0, xp_ref[POOL_HALO - 1:POOL_HALO, :], xn_ref[0:1, :])
    keep = ((sub16 == 0) & prev_ok) | ((sub16 == 1) & next_ok)
    h_scr[TMF:TMF + HALO_ROWS, :] = jnp.where(keep, halo_x * scale + shift, 0.0).astype(BF16)

    n_ub = TMF // RBUF
    sub = lax.broadcasted_iota(jnp.int32, (8, FC), 0)
    seg_per_seq = SEQ // SEGF
    ctx_first = is_ctx & (sub % seg_per_seq == 0)
    ctx_last = is_ctx & (sub % seg_per_seq == seg_per_seq - 1)

    def up_proj(slot, c, ub):
        rows = slice(ub * RBUF, (ub + 1) * RBUF + (HALO_ROWS if ub == n_ub - 1 else 0))
        u_scrs[slot][rows, :] = jnp.dot(h_scr[rows, :], up_ref[c], preferred_element_type=F32)

    def conv(u_ref, lanes, cvs, r0):
        lo = max(r0 - 8, 0)
        hi = min(r0 + GB + 8, TMF)
        ue = u_ref[lo:hi, lanes]
        u = ue[r0 - lo:r0 - lo + GB]
        if r0 == 0:
            b_first = jnp.where(sub == 0, u_ref[TMF:TMF + 1, lanes],
                                pltpu.roll(u_ref[TMF - 8:TMF, lanes], 1, axis=0))
            um1 = jnp.concatenate([jnp.where(ctx_first, 0.0, b_first), u[0:GB - 8]], axis=0)
        else:
            um1 = ue[0:GB]
        if r0 == TMF - GB:
            b_last = jnp.where(sub == 7, u_ref[TMF + 1:TMF + 2, lanes],
                               pltpu.roll(u_ref[0:8, lanes], 7, axis=0))
            up1 = jnp.concatenate([u[8:GB], jnp.where(ctx_last, 0.0, b_last)], axis=0)
        else:
            up1 = ue[r0 - lo + 8:r0 - lo + GB + 8]
        return um1 * cvs[0:1] + u * cvs[1:2] + up1 * cvs[2:3] + cvs[3:4]

    def gate(slot, c, ub):
        cva = cv_ref[0:4, c * FC:(c + 1) * FC]
        cvg = cv_ref[0:4, D_FF + c * FC:D_FF + (c + 1) * FC]
        for r0 in range(ub * RBUF, (ub + 1) * RBUF, GB):
            a = conv(u_scrs[slot], slice(0, FC), cva, r0)
            g = conv(u_scrs[slot], slice(FC, 2 * FC), cvg, r0)
            act_scr[c, r0:r0 + GB, :] = (_silu(a) * g).astype(BF16)

    def finish(rb):
        vrows = range(rb * RB // 8, (rb + 1) * RB // 8)
        lhs = jnp.concatenate([act_scr[c, rb * RB:(rb + 1) * RB, :] for c in range(NFC)], axis=1)
        y = jnp.dot(lhs, dn_ref[...].astype(BF16), preferred_element_type=F32)
        xr = jnp.concatenate([_seg_rows_f(xc_scr, k) for k in vrows], axis=0)
        out = _layer_norm(ALPHA * xr + m[5:6] * y, ln[1:2], ln[3:4])
        for kk, k in enumerate(vrows):
            for cb in range(LANE_BLOCKS):
                xc_scr[cb, pl.ds(k, 8, stride=PITCHF), :] = out[8 * kk:8 * kk + 8, cb * 128:(cb + 1) * 128]

    def write_out(side):
        for cb in range(LANE_BLOCKS):
            for s in range(8):
                o_refs[side][s * SEGF:(s + 1) * SEGF, cb * 128:(cb + 1) * 128] = xc_scr[
                    cb, s * PITCHF:s * PITCHF + SEGF, :]

    build_h(0)
    for c in range(NFC + 1):
        for ub in range(n_ub):
            if c == 0 and ub > 0:
                build_h(ub)
            if c < NFC:
                up_proj(c % 2, c, ub)
            if c >= 1:
                gate((c - 1) % 2, c - 1, ub)
            if c == NFC:
                finish(ub)

    _per_half(i, TMF, n_out, write_out)


def _ffn(xs, mod, up_chunks, conv8, down, ln8, l, split_out):
    halo_blocks = TMF // POOL_HALO
    last_halo = N_LAT // POOL_HALO - 1
    n_out = 2 if split_out else 1
    if split_out:
        out_shape = [jax.ShapeDtypeStruct((N_CTX, D_MODEL), F32), jax.ShapeDtypeStruct((N_LAT, D_MODEL), F32)]
    else:
        out_shape = [jax.ShapeDtypeStruct((ROWS, D_MODEL), F32)]
    return pl.pallas_call(
        functools.partial(_ffn_kernel, n_out=n_out),
        grid=(ROWS // TMF,),
        in_specs=[
            *_tile_specs(2, TMF, 1),
            pl.BlockSpec((POOL_HALO, D_MODEL),
                         lambda i: (jnp.maximum((i - CTXF_TILES) * halo_blocks - 1, 0), 0)),
            pl.BlockSpec((POOL_HALO, D_MODEL),
                         lambda i: (jnp.clip((i - CTXF_TILES + 1) * halo_blocks, 0, last_halo), 0)),
            pl.BlockSpec((None, None, 6, D_MODEL), lambda i: (l, _cond_of_tile(i, TMF), 0, 0)),
            pl.BlockSpec((None, NFC, D_MODEL, 2 * FC), lambda i: (l, 0, 0, 0), pipeline_mode=pl.Buffered(1)),
            pl.BlockSpec((None, 8, 2 * D_FF), lambda i: (l, 0, 0)),
            pl.BlockSpec((None, D_FF, D_MODEL), lambda i: (l, 0, 0), pipeline_mode=pl.Buffered(1)),
            pl.BlockSpec((None, 8, D_MODEL), lambda i: (l, 0, 0)),
        ],
        out_specs=_tile_specs(n_out, TMF, 1),
        out_shape=out_shape,
        scratch_shapes=[pltpu.VMEM((TMF + HALO_ROWS, D_MODEL), BF16),
                        pltpu.VMEM((NFC, TMF, FC), BF16),
                        pltpu.VMEM((LANE_BLOCKS, 8 * PITCHF, 128), F32),
                        pltpu.VMEM((TMF + HALO_ROWS, 2 * FC), F32),
                        pltpu.VMEM((TMF + HALO_ROWS, 2 * FC), F32)],
        compiler_params=pltpu.CompilerParams(
            dimension_semantics=("arbitrary",), vmem_limit_bytes=VMEM_LIMIT),
        name="convffn",
    )(xs[0], xs[1], xs[1], xs[1], mod, up_chunks, conv8, down, ln8)


def _head_masks(width):
    lane = lax.broadcasted_iota(jnp.int32, (1, width), 1)
    return [(lane >= h * HEAD_DIM) & (lane < (h + 1) * HEAD_DIM) for h in range(width // HEAD_DIM)]


def _stack_heads(x, masks):
    return jnp.concatenate([jnp.where(m, x, jnp.zeros_like(x)) for m in masks], axis=0)


def _rope(x, cos, sin):
    lane = lax.broadcasted_iota(jnp.int32, (1, 128), 1)
    lower = (lane & 31) < 16
    outs = []
    for k in range(x.shape[1] // 128):
        xb = x[:, k * 128:(k + 1) * 128]
        partner = jnp.where(lower, pltpu.roll(xb, 112, axis=1), pltpu.roll(xb, 16, axis=1))
        outs.append(xb * cos + partner * sin)
    return outs[0] if len(outs) == 1 else jnp.concatenate(outs, axis=1)


def _load_state(state_scr, blocks_ref, d):
    state_scr[...] = jnp.zeros((GW, GW), F32)
    for h in range(N_HEADS):
        sl = slice(h * HEAD_DIM, (h + 1) * HEAD_DIM)
        state_scr[sl, sl] = blocks_ref[d, h]


def _store_state(blocks_ref, d, state_scr):
    for h in range(N_HEADS):
        sl = slice(h * HEAD_DIM, (h + 1) * HEAD_DIM)
        blocks_ref[d, h] = state_scr[sl, sl]


LAT_GROUP = 4


def _chunk_off(c):
    return c * CHUNK if isinstance(c, int) else pl.multiple_of(c * CHUNK, CHUNK)


def _interleave(chains):
    chains = list(chains)
    while chains:
        for ch in list(chains):
            try:
                next(ch)
            except StopIteration:
                chains.remove(ch)


def _mixer_kernel(*refs, latent, group, ng, cast_up=False):
    if latent:
        (qa_ref, ka_ref, va_ref, kx_ref, vx_ref, ub_ref, vb_ref, qc_ref, kc_ref, vc_ref, gf_ref, gb_ref,
         pd_ref, pdp_ref, pdn_ref, cos_ref, sin_ref, s0_ref, dmat_ref, dec_ref, sink_ref, ws_ref, bias_ref,
         vec_ref, gmat_ref, cnt_ref, wpool_ref, xres_ref, mod_ref, wout_ref, ln_ref,
         x1_ref, sf_scr, sb_scr, ob_scr, pext_scr, mix_ref, k_scr, v_scr) = refs
    else:
        (qa_ref, ka_ref, va_ref, ub_ref, vb_ref, qc_ref, kc_ref, vc_ref, gf_ref, gb_ref, pd_ref,
         dmat_ref, dec_ref, sink_ref, ws_ref, bias_ref, vec_ref, gmat_ref, cnt_ref, wpool_ref,
         xres_ref, mod_ref, wout_ref, ln_ref) = refs[:24]
        if cast_up:
            upa_ref, upg_ref, x1_ref, st_ref, kn_ref, vn_ref, upo_ref = refs[24:31]
        else:
            x1_ref, st_ref, kn_ref, vn_ref = refs[24:28]
        sf_scr, sb_scr, ob_scr, pext_scr, mix_ref = refs[-5:]

    nc = group * ng
    g_a = mod_ref[...][2:3]
    ln = ln_ref[...]
    masks = _head_masks(GW)
    row = lax.broadcasted_iota(jnp.int32, (GW, GW), 0)
    col = lax.broadcasted_iota(jnp.int32, (GW, GW), 1)
    bd_mask = (row // HEAD_DIM) == (col // HEAD_DIM)
    vec = vec_ref[...]
    lane = lax.broadcasted_iota(jnp.int32, (1, 2 * HEAD_DIM), 1)
    lo = lane < HEAD_DIM

    def rows(k):
        return slice(k * CHUNK, (k + 1) * CHUNK)

    def retention(order, d, state_scr, out):
        dmat = dmat_ref[d]
        qdec, kdec = dec_ref[d], dec_ref[2 + d]
        cdec = vec[5 + d:6 + d]
        state = state_scr[...]
        for k in order:
            q, kk, v = qc_ref[rows(k), :], kc_ref[rows(k), :], vc_ref[rows(k), :]
            qb, kb, vb = q.astype(BF16), kk.astype(BF16), v.astype(BF16)
            s = lax.dot_general(_stack_heads(qb, masks), kb, (((1,), (1,)), ((), ())),
                                preferred_element_type=F32)
            yield
            p = (s * dmat).astype(BF16)
            p_cat = jnp.concatenate([p[h * CHUNK:(h + 1) * CHUNK] for h in range(N_HEADS)], axis=1)
            o = (jnp.dot(p_cat, _stack_heads(vb, masks), preferred_element_type=F32)
                 + _bdot(q * qdec, state))
            yield
            upd = lax.dot_general((kk * kdec).astype(BF16), vb, (((0,), (0,)), ((), ())),
                                  preferred_element_type=F32)
            state = state * cdec + jnp.where(bd_mask, upd, 0.0)
            out[k] = o
            yield
        state_scr[...] = state

    def stage_kv(k, c):
        dst = pl.ds(_chunk_off(c + 1), CHUNK)
        k_scr[dst, :] = _rope(ka_ref[rows(k), :], cos_ref[rows(k), :], sin_ref[rows(k), :]).astype(BF16)
        yield
        v_scr[dst, :] = va_ref[rows(k), :].astype(BF16)
        yield

    def attention(k, c, done):
        q = qa_ref[rows(k), :]
        if latent:
            q = _rope(q, cos_ref[rows(k), :], sin_ref[rows(k), :])
        q = q * (HEAD_DIM ** -0.5)
        q0, q1 = q[:, :128], q[:, 128:]
        zero = jnp.zeros_like(q0)
        q_st = jnp.concatenate([
            jnp.where(lo, q0, zero),
            jnp.where(lo, pltpu.roll(q0, HEAD_DIM, axis=1), zero),
            jnp.where(lo, zero, pltpu.roll(q1, HEAD_DIM, axis=1)),
            jnp.where(lo, zero, q1)], axis=0).astype(BF16)
        if latent:
            band = pl.ds(_chunk_off(c), 3 * CHUNK)
            k_all = jnp.concatenate([k_scr[band, :], kx_ref[...].astype(BF16)], axis=0)
            v_all = jnp.concatenate([v_scr[band, :], vx_ref[...].astype(BF16)], axis=0)
        else:
            k_all = ka_ref[...].astype(BF16)
            v_all = va_ref[...].astype(BF16)
        s = lax.dot_general(q_st, k_all, (((1,), (1,)), ((), ())), preferred_element_type=F32)
        yield
        if latent:
            nk = 3 * CHUNK + PAST_LEN
            qi = lax.broadcasted_iota(jnp.int32, (N_HEADS * CHUNK, nk), 0) & (CHUNK - 1)
            kj = lax.broadcasted_iota(jnp.int32, (N_HEADS * CHUNK, nk), 1)
            kpos = kj + (c - 1) * CHUNK
            valid = (kj >= 3 * CHUNK) | ((kj >= qi) & (kj <= qi + 2 * CHUNK)
                                         & (kpos >= 0) & (kpos < nc * CHUNK))
            s = jnp.where(valid, s, NEG_INF)
        sink = sink_ref[...][:, 0:1]
        mx = jnp.maximum(jnp.max(s, axis=-1, keepdims=True), sink)
        yield
        e = jnp.exp(s - mx)
        den = jnp.sum(e, axis=-1, keepdims=True) + jnp.exp(sink - mx)
        yield
        o = jnp.dot(e.astype(BF16), v_all, preferred_element_type=F32) / den
        yield
        mix_ref[rows(k), 0:128] = jnp.where(lo, o[0:CHUNK], pltpu.roll(o[CHUNK:2 * CHUNK], HEAD_DIM, axis=1))
        mix_ref[rows(k), 128:256] = jnp.where(lo, pltpu.roll(o[2 * CHUNK:3 * CHUNK], HEAD_DIM, axis=1),
                                              o[3 * CHUNK:4 * CHUNK])
        done.add(("A", k))
        yield

    def gating_unit(k, done):
        vn = _layer_norm(vb_ref[rows(k), :], vec[0:1], vec[1:2]).astype(BF16)
        yield
        sg = jnp.dot(ws_ref[...], _stack_heads(vn, masks), preferred_element_type=F32) + bias_ref[...]
        yield
        mix_ref[rows(k), GW:2 * GW] = ub_ref[rows(k), :] * sg
        done.add(("B", k))
        yield

    def retention_mix(k, o_fwd, o_bwd, done):
        while k not in o_fwd or k not in o_bwd:
            yield
        gmat = gmat_ref[...]
        normed = []
        for o in (o_fwd[k], o_bwd[k]):
            mu = jnp.dot(o.astype(BF16), gmat, preferred_element_type=F32)
            yield
            dlt = o - mu
            var = jnp.dot((dlt * dlt).astype(BF16), gmat, preferred_element_type=F32)
            yield
            normed.append(dlt * lax.rsqrt(var + LN_EPS))
        mix_ref[rows(k), 2 * GW:3 * GW] = (_silu(gf_ref[rows(k), :]) * (normed[0] * vec[3:4])
                                           + _silu(gb_ref[rows(k), :]) * (normed[1] * vec[4:5]))
        done.add(("C", k))
        yield

    def pooling(k, c, done):
        pd = pd_ref[rows(k), :]
        pext = pext_scr.at[k]
        zeros8 = jnp.zeros((POOL_HALO, GW), F32)
        if k > 0:
            prev8 = pd_ref[k * CHUNK - POOL_HALO:k * CHUNK, :]
        else:
            prev8 = jnp.where(c > 0, pdp_ref[...], 0.0) if latent else zeros8
        if k < group - 1:
            next8 = pd_ref[(k + 1) * CHUNK:(k + 1) * CHUNK + POOL_HALO, :]
        else:
            next8 = jnp.where(c < nc - 1, pdn_ref[...], 0.0) if latent else zeros8
        pext[0:POOL_HALO, :] = prev8
        pext[POOL_HALO:POOL_HALO + CHUNK, :] = pd
        pext[POOL_HALO + CHUNK:2 * POOL_HALO + CHUNK, :] = next8
        yield

        def win(d, half):
            return pext[pl.ds(POOL_HALO + d, CHUNK), half * 128:(half + 1) * 128]

        a2 = win(-1, 0) + win(0, 0)
        a4 = a2 + win(-2, 0) + win(1, 0)
        yield
        a8 = win(-4, 1)
        for d in range(-3, 4):
            a8 = a8 + win(d, 1)
        yield
        a16 = a8
        for d in list(range(-8, -4)) + list(range(4, 8)):
            a16 = a16 + win(d, 1)
        yield
        sums = jnp.concatenate([jnp.where(lo, a2, a4), jnp.where(lo, a8, a16)], axis=1)
        yd = sums * cnt_ref[rows(k), :] - pd
        mix_ref[rows(k), 3 * GW:4 * GW] = _bdot(yd, wpool_ref[...]) * vec[2:3]
        done.add(("D", k))
        yield

    def out_projection(k, done):
        while not all((m, k) in done for m in "ABCD"):
            yield
        y = jnp.dot(mix_ref[rows(k), :].astype(BF16), wout_ref[...], preferred_element_type=F32)
        yield
        x1_ref[rows(k), :] = _layer_norm(ALPHA * xres_ref[rows(k), :] + g_a * y, ln[0:1], ln[2:3])
        yield

    def up_weight_rows():
        for lyr in range(DEPTH):
            for cc in range(NFC):
                upo_ref[lyr, cc, :, 0:FC] = upa_ref[lyr, :, cc * FC:(cc + 1) * FC].astype(BF16)
                upo_ref[lyr, cc, :, FC:2 * FC] = upg_ref[lyr, :, cc * FC:(cc + 1) * FC].astype(BF16)
                yield

    def forward_chains(chunk_of, o_bwd):
        done, o_fwd = set(), {}
        chains = [retention(range(group), 0, sf_scr, o_fwd)]
        for k in range(group):
            chains += [attention(k, chunk_of(k), done), gating_unit(k, done), pooling(k, chunk_of(k), done),
                       retention_mix(k, o_fwd, o_bwd, done), out_projection(k, done)]
        return chains

    if not latent:
        sb_scr[...] = jnp.zeros((GW, GW), F32)
        sf_scr[...] = jnp.zeros((GW, GW), F32)
        kn_ref[...] = ka_ref[...]
        vn_ref[...] = va_ref[...]
        o_bwd = {}
        side_chains = [retention(reversed(range(group)), 1, sb_scr, o_bwd)]
        if cast_up:
            side_chains.append(up_weight_rows())
        _interleave(side_chains + forward_chains(lambda k: k, o_bwd))
        _store_state(st_ref, 1, sb_scr)
        _store_state(st_ref, 0, sf_scr)
        return

    p = pl.program_id(1)
    g = pl.program_id(2)

    @pl.when(p == 0)
    def _():
        @pl.when(g == 0)
        def _():
            _load_state(sb_scr, s0_ref, 1)
            zero_blk = jnp.zeros((CHUNK, 2 * HEAD_DIM), BF16)
            for scr in (k_scr, v_scr):
                scr[0:CHUNK, :] = zero_blk
                scr[(nc + 1) * CHUNK:(nc + 2) * CHUNK, :] = zero_blk

        first = (ng - 1 - g) * group
        o_bwd = {}
        _interleave([retention(reversed(range(group)), 1, sb_scr, o_bwd)]
                    + [stage_kv(k, first + k) for k in range(group)])
        for k in range(group):
            ob_scr[pl.ds(_chunk_off(first + k), CHUNK), :] = o_bwd[k]

    @pl.when(p == 1)
    def _():
        @pl.when(g == 0)
        def _():
            _load_state(sf_scr, s0_ref, 0)

        first = g * group
        o_bwd = {k: ob_scr[pl.ds(_chunk_off(first + k), CHUNK), :] for k in range(group)}
        _interleave(forward_chains(lambda k: first + k, o_bwd))


def _mixer(z, x_res, mod, w_out_bf16, ln8, tabs, l, latent, extra=None, ffn_up=None):
    nb = DEC_BATCH if latent else BATCH
    nc = (DEC_SEQ if latent else SEQ) // CHUNK
    group = LAT_GROUP if latent else nc
    ng = nc // group
    blk = group * CHUNK
    base = (N_CTX // blk) if latent else 0
    per8 = blk // POOL_HALO
    last_halo = ROWS // POOL_HALO - 1

    def on_grid(f):
        return (lambda b, p, g: f(b, p, g)) if latent else (lambda b: f(b, 1, 0))

    def fwd(b, p, g):
        return base + b * ng + g * p

    def both(b, p, g):
        return base + b * ng + jnp.where(p == 0, ng - 1 - g, g)

    def bwd_only(b, p, g):
        return base + b * ng + (ng - 1 - g) * (1 - p)

    def col(width, idx, rowmap):
        return pl.BlockSpec((blk, width), on_grid(lambda b, p, g: (rowmap(b, p, g), idx)))

    def const(shape):
        return pl.BlockSpec(shape, on_grid(lambda b, p, g: (0,) * len(shape)))

    def layer(shape):
        return pl.BlockSpec((None,) + shape, on_grid(lambda b, p, g: (l,) + (0,) * len(shape)))

    specs, args = [], []

    def add(spec, arr):
        specs.append(spec)
        args.append(arr)

    add(col(GW, 0, fwd), z)
    add(col(128, 2, bwd_only if latent else fwd), z)
    add(col(128, 3, bwd_only if latent else fwd), z)
    if latent:
        add(pl.BlockSpec((None, None, PAST_LEN, 128), lambda b, p, g: (b, l, 0, 0)), extra["ck"])
        add(pl.BlockSpec((None, None, PAST_LEN, 128), lambda b, p, g: (b, l, 0, 0)), extra["cv"])
    add(col(GW, 2, fwd), z)
    add(col(GW, 3, fwd), z)
    add(col(GW, 4, both), z)
    add(col(GW, 5, both), z)
    add(col(GW, 6, both), z)
    add(col(GW, 7, fwd), z)
    add(col(GW, 8, fwd), z)
    add(col(GW, 9, fwd), z)
    if latent:
        add(pl.BlockSpec((POOL_HALO, GW),
                         lambda b, p, g: (jnp.maximum(fwd(b, p, g) * per8 - 1, 0), 9)), z)
        add(pl.BlockSpec((POOL_HALO, GW),
                         lambda b, p, g: (jnp.minimum((fwd(b, p, g) + 1) * per8, last_halo), 9)), z)
        rope_map = lambda b, p, g: (jnp.where(p == 0, ng - 1 - g, g), 0)
        add(pl.BlockSpec((blk, 128), rope_map), extra["cos"])
        add(pl.BlockSpec((blk, 128), rope_map), extra["sin"])
        add(pl.BlockSpec((None, None, 2, N_HEADS, HEAD_DIM, HEAD_DIM),
                         lambda b, p, g: (b, l, 0, 0, 0, 0)), extra["s0"])
    add(layer((2, N_HEADS * CHUNK, CHUNK)), tabs["dmat"])
    add(layer((4, CHUNK, GW)), tabs["dec"])
    add(layer((N_HEADS * CHUNK, 128)), tabs["sink"])
    add(layer((CHUNK, N_HEADS * CHUNK)), tabs["ws"])
    add(layer((CHUNK, GW)), tabs["bias"])
    add(layer((8, GW)), tabs["vec"])
    add(const((GW, GW)), tabs["gmat"])
    add(pl.BlockSpec((blk, GW), on_grid(lambda b, p, g: (g * p, 0))),
        tabs["cnt_lat"] if latent else tabs["cnt_ctx"])
    add(layer((GW, GW)), tabs["wpool"])
    local = lambda b, p, g: (b * ng + g * p, 0)
    add(pl.BlockSpec((blk, D_MODEL), on_grid(local)), x_res)
    add(pl.BlockSpec((None, None, 6, D_MODEL),
                     on_grid(lambda b, p, g: (l, (1 + b) if latent else 0, 0, 0))), mod)
    add(layer((D_MODEL, D_MODEL)), w_out_bf16)
    add(layer((8, D_MODEL)), ln8)
    up_rows = D_MODEL // nb
    if ffn_up is not None:
        add(pl.BlockSpec((DEPTH, up_rows, D_FF), lambda b: (0, b, 0)), ffn_up)
        add(pl.BlockSpec((DEPTH, up_rows, D_FF), lambda b: (0, b, 1)), ffn_up)

    out_shape = [jax.ShapeDtypeStruct((nb * nc * CHUNK, D_MODEL), F32)]
    out_specs = [pl.BlockSpec((blk, D_MODEL), on_grid(local))]
    scratch = [pltpu.VMEM((GW, GW), F32), pltpu.VMEM((GW, GW), F32),
               pltpu.VMEM((nc * CHUNK, GW), F32),
               pltpu.VMEM((group, CHUNK + 2 * POOL_HALO, GW), F32),
               pltpu.VMEM((blk, D_MODEL), F32)]
    if latent:
        scratch += [pltpu.VMEM(((nc + 2) * CHUNK, 128), BF16), pltpu.VMEM(((nc + 2) * CHUNK, 128), BF16)]
    else:
        out_shape.append(jax.ShapeDtypeStruct((nb, 2, N_HEADS, HEAD_DIM, HEAD_DIM), F32))
        out_specs.append(pl.BlockSpec((None, 2, N_HEADS, HEAD_DIM, HEAD_DIM), lambda b: (b, 0, 0, 0, 0)))
        for _ in range(2):
            out_shape.append(jax.ShapeDtypeStruct((nb, SEQ, 128), F32))
            out_specs.append(pl.BlockSpec((None, SEQ, 128), lambda b: (b, 0, 0)))
        if ffn_up is not None:
            out_shape.append(jax.ShapeDtypeStruct((DEPTH, NFC, D_MODEL, 2 * FC), BF16))
            out_specs.append(pl.BlockSpec((DEPTH, NFC, up_rows, 2 * FC), lambda b: (0, 0, b, 0)))

    return pl.pallas_call(
        functools.partial(_mixer_kernel, latent=latent, group=group, ng=ng, cast_up=ffn_up is not None),
        grid=(nb, 2, ng) if latent else (nb,),
        in_specs=specs,
        out_specs=out_specs,
        out_shape=out_shape,
        scratch_shapes=scratch,
        compiler_params=pltpu.CompilerParams(
            dimension_semantics=("arbitrary",) * (3 if latent else 1), vmem_limit_bytes=VMEM_LIMIT),
        name="mixer_latent" if latent else "mixer_context",
    )(*args)


def _pad_rows(rows, n=8):
    a = jnp.stack(rows)
    return jnp.concatenate([a, jnp.zeros((n - a.shape[0],) + a.shape[1:], a.dtype)], axis=0)


def _block_diag(blocks):
    g, n, _ = blocks.shape
    eye = jnp.eye(g, dtype=blocks.dtype)
    return (eye[:, None, :, None] * blocks[:, :, None, :]).reshape(g * n, g * n)


def _inv_count(n):
    t = np.arange(n)
    cols = []
    for w in POOL_WINDOWS:
        cnt = np.clip(t + w // 2, 0, n) - np.clip(t - w // 2, 0, n)
        cols.append(np.repeat((1.0 / cnt)[:, None], HEAD_DIM, axis=1))
    return jnp.asarray(np.concatenate(cols, axis=1), F32)


def _rope_tables():
    rows = DEC_SEQ // GRID_W
    r, cc = jnp.meshgrid(jnp.arange(rows), jnp.arange(GRID_W), indexing="ij")
    half = HEAD_DIM // 2
    freqs = ROPE_BASE ** (-jnp.arange(0, half, 2, dtype=F32) / half)

    def tables(pos):
        ang = pos.reshape(-1).astype(F32)[:, None] * freqs[None, :]
        cos, sin = jnp.cos(ang), jnp.sin(ang)
        return jnp.concatenate([cos, cos], axis=1), jnp.concatenate([-sin, sin], axis=1)

    cr, sr = tables(r)
    ccol, scol = tables(cc)
    cos = jnp.concatenate([cr, ccol], axis=1)
    sin = jnp.concatenate([sr, scol], axis=1)
    return jnp.tile(cos, (1, 2)), jnp.tile(sin, (1, 2))


def _layer_tables(attn_sink, sgu_norm_w, sgu_norm_b, sgu_ws, sgu_bs, ret_decay, ret_gn_w, pool_w, pool_scale):
    log_g = jax.nn.log_sigmoid(ret_decay.astype(F32))
    i = jnp.arange(CHUNK, dtype=F32)
    rel = i[:, None] - i[None, :]
    kscale = HEAD_DIM ** -0.5
    d_f = jnp.where(rel >= 0, jnp.exp(jnp.maximum(rel, 0.0)[None] * log_g[0][:, None, None]), 0.0)
    d_b = jnp.where(rel <= 0, jnp.exp(jnp.maximum(-rel, 0.0)[None] * log_g[1][:, None, None]), 0.0)
    dmat = jnp.stack([d_f.reshape(N_HEADS * CHUNK, CHUNK), d_b.reshape(N_HEADS * CHUNK, CHUNK)]) * kscale

    def lanes(per_head):
        return jnp.repeat(per_head, HEAD_DIM, axis=1)

    qdec_f = lanes(jnp.exp((i + 1.0)[:, None] * log_g[0][None, :]))
    qdec_b = lanes(jnp.exp((CHUNK - i)[:, None] * log_g[1][None, :]))
    kdec_f = lanes(jnp.exp((CHUNK - 1.0 - i)[:, None] * log_g[0][None, :])) * kscale
    kdec_b = lanes(jnp.exp(i[:, None] * log_g[1][None, :])) * kscale
    cdec = jnp.repeat(jnp.exp(CHUNK * log_g), HEAD_DIM, axis=1)
    vec = _pad_rows([sgu_norm_w, sgu_norm_b, pool_scale, ret_gn_w[0], ret_gn_w[1], cdec[0], cdec[1]])
    return {
        "dmat": dmat,
        "dec": jnp.stack([qdec_f, qdec_b, kdec_f, kdec_b]),
        "sink": jnp.broadcast_to(jnp.repeat(attn_sink, CHUNK)[:, None], (N_HEADS * CHUNK, 128)),
        "ws": jnp.concatenate([sgu_ws[h] for h in range(N_HEADS)], axis=1).astype(BF16),
        "bias": jnp.repeat(sgu_bs.T, HEAD_DIM, axis=1),
        "vec": vec,
        "wpool": _block_diag(pool_w).astype(BF16),
    }


def kernel(x_prompt, x_sample, cache_attn_k, cache_attn_v, state_ret, c, c_ctx, w_ada, b_ada, w_in,
           w_out, attn_sink, sgu_norm_w, sgu_norm_b, sgu_ws, sgu_bs, ret_decay, ret_gn_w, pool_w,
           pool_scale, ffn_up, ffn_conv_w, ffn_conv_b, ffn_down, ln_w, ln_b):
    cond8 = jnp.concatenate([c_ctx[None], c, jnp.zeros((8 - 1 - DEC_BATCH, D_MODEL), F32)], axis=0)
    mod = _modulation(cond8, w_ada, b_ada).reshape(DEPTH, 8, 6, D_MODEL)

    tabs = jax.vmap(_layer_tables)(attn_sink, sgu_norm_w, sgu_norm_b, sgu_ws, sgu_bs, ret_decay, ret_gn_w,
                                   pool_w, pool_scale)
    tabs["gmat"] = _block_diag(jnp.full((N_HEADS, HEAD_DIM, HEAD_DIM), 1.0 / HEAD_DIM, F32)).astype(BF16)
    tabs["cnt_ctx"] = _inv_count(SEQ)
    tabs["cnt_lat"] = _inv_count(DEC_SEQ)
    cos, sin = _rope_tables()
    extra = {"ck": cache_attn_k.reshape(DEC_BATCH, DEPTH, PAST_LEN, 128),
             "cv": cache_attn_v.reshape(DEC_BATCH, DEPTH, PAST_LEN, 128),
             "cos": cos, "sin": sin, "s0": state_ret}
    ln8 = jnp.concatenate([ln_w, ln_b, jnp.zeros((DEPTH, 4, D_MODEL), F32)], axis=1)
    conv8 = jnp.concatenate([ffn_conv_w, ffn_conv_b[:, None], jnp.zeros((DEPTH, 4, 2 * D_FF), F32)], axis=1)

    w_out_bf16 = _to_bf16(w_out, D_MODEL)

    xs = [x_prompt.reshape(N_CTX, D_MODEL), x_sample.reshape(N_LAT, D_MODEL)]
    new_k, new_v, new_s = [], [], []
    for l in range(DEPTH):
        z = _inproj(xs, mod, w_in, l)
        if l == 0:
            x1_ctx, st, kn, vn, up_chunks = _mixer(z, xs[0], mod, w_out_bf16, ln8, tabs, l, latent=False,
                                                   ffn_up=ffn_up)
        else:
            x1_ctx, st, kn, vn = _mixer(z, xs[0], mod, w_out_bf16, ln8, tabs, l, latent=False)
        (x1_lat,) = _mixer(z, xs[1], mod, w_out_bf16, ln8, tabs, l, latent=True, extra=extra)
        xs = _ffn([x1_ctx, x1_lat], mod, up_chunks, conv8, ffn_down, ln8, l, split_out=True)
        new_k.append(kn.reshape(BATCH, SEQ, 2, HEAD_DIM))
        new_v.append(vn.reshape(BATCH, SEQ, 2, HEAD_DIM))
        new_s.append(st)

    y_prompt = xs[0].reshape(BATCH, SEQ, D_MODEL)
    y_sample = xs[1].reshape(DEC_BATCH, DEC_SEQ, D_MODEL)
    return (y_prompt, y_sample, jnp.stack(new_k, axis=1), jnp.stack(new_v, axis=1),
            jnp.stack(new_s, axis=1))
```

```python
import functools

import numpy as np
import jax
import jax.numpy as jnp
from jax import lax
from jax.experimental import pallas as pl
from jax.experimental.pallas import tpu as pltpu

F32 = jnp.float32
BF16 = jnp.bfloat16

D_MODEL = 1024
BATCH = 16
SEQ = 256
DEPTH = 2
DEC_BATCH = 2
DEC_SEQ = 2048
PAST_LEN = 256
GRID_W = 64
CHUNK = 128
HEAD_DIM = 64
GW = D_MODEL // 4
N_HEADS = 4
POOL_WINDOWS = (2, 4, 8, 16)
POOL_HALO = 8
D_FF = 2816
ROPE_BASE = 10000.0
LN_EPS = 1e-5
NEG_INF = -1e30
IN_WIDTH = 10 * GW
ALPHA = (2.0 * DEPTH) ** 0.25

N_CTX = BATCH * SEQ
N_LAT = DEC_BATCH * DEC_SEQ
ROWS = N_CTX + N_LAT

TM = 1024
NB_IN = 512
FC = 256
NB_ADA = 1536
VMEM_LIMIT = 56 * 1024 * 1024


def _cond_of_tile(i, tm=TM):
    ctx_tiles = N_CTX // tm
    return jnp.where(i < ctx_tiles, 0, 1 + (i - ctx_tiles) // (DEC_SEQ // tm))


def _tile_specs(n_src, tm, grid_rank):
    ctx_tiles = N_CTX // tm
    if n_src == 1:
        rows = [lambda i: i]
    else:
        rows = [lambda i: jnp.minimum(i, ctx_tiles - 1), lambda i: jnp.maximum(i - ctx_tiles, 0)]
    if grid_rank == 1:
        return [pl.BlockSpec((tm, D_MODEL), lambda i, f=f: (f(i), 0)) for f in rows]
    return [pl.BlockSpec((tm, D_MODEL), lambda i, j, f=f: (f(i), 0)) for f in rows]


def _per_half(i, tm, n_max, fn):
    if n_max == 1:
        fn(0)
        return
    ctx_tiles = N_CTX // tm

    @pl.when(i < ctx_tiles)
    def _():
        fn(0)

    @pl.when(i >= ctx_tiles)
    def _():
        fn(1)


def _layer_norm(x, w, b):
    mu = jnp.mean(x, axis=-1, keepdims=True)
    d = x - mu
    var = jnp.mean(d * d, axis=-1, keepdims=True)
    return d * lax.rsqrt(var + LN_EPS) * w + b


def _silu(x):
    return x * jax.nn.sigmoid(x)


def _bdot(a, b):
    return jnp.dot(a.astype(BF16), b.astype(BF16), preferred_element_type=F32)


def _mod_kernel(c_ref, w_ref, b_ref, o_ref):
    o_ref[...] = _bdot(_silu(c_ref[...]), w_ref[...]) + b_ref[...]


def _modulation(cond8, w_ada, b_ada):
    return pl.pallas_call(
        _mod_kernel,
        grid=(DEPTH, 6 * D_MODEL // NB_ADA),
        in_specs=[
            pl.BlockSpec((8, D_MODEL), lambda l, j: (0, 0)),
            pl.BlockSpec((None, D_MODEL, NB_ADA), lambda l, j: (l, 0, j)),
            pl.BlockSpec((None, 1, NB_ADA), lambda l, j: (l, 0, j)),
        ],
        out_specs=pl.BlockSpec((None, 8, NB_ADA), lambda l, j: (l, 0, j)),
        out_shape=jax.ShapeDtypeStruct((DEPTH, 8, 6 * D_MODEL), F32),
        compiler_params=pltpu.CompilerParams(
            dimension_semantics=("arbitrary", "arbitrary"), vmem_limit_bytes=VMEM_LIMIT),
        name="modulation",
    )(cond8, w_ada, b_ada.reshape(DEPTH, 1, 6 * D_MODEL))


def _cast_kernel(w_ref, o_ref):
    o_ref[...] = w_ref[...].astype(BF16)


def _to_bf16(w, block_rows):
    depth, rows, cols = w.shape
    return pl.pallas_call(
        _cast_kernel,
        grid=(depth, rows // block_rows),
        in_specs=[pl.BlockSpec((None, block_rows, cols), lambda l, r: (l, r, 0))],
        out_specs=pl.BlockSpec((None, block_rows, cols), lambda l, r: (l, r, 0)),
        out_shape=jax.ShapeDtypeStruct(w.shape, BF16),
        compiler_params=pltpu.CompilerParams(dimension_semantics=("arbitrary", "arbitrary")),
        name="cast_bf16",
    )(w)


def _inproj_kernel(*refs, n_x):
    x_refs = refs[:n_x]
    mod_ref, w_ref, z_ref, h_scr = refs[n_x:]

    m = mod_ref[...]

    def project(side):
        for rh in range(2):
            rows = slice(rh * TM // 2, (rh + 1) * TM // 2)
            h_scr[rows, :] = (x_refs[side][rows, :] * (1.0 + m[1:2]) + m[0:1]).astype(BF16)
            for jb in range(IN_WIDTH // NB_IN):
                cols = slice(jb * NB_IN, (jb + 1) * NB_IN)
                z_ref[rows, cols] = jnp.dot(h_scr[rows, :], w_ref[:, cols].astype(BF16),
                                            preferred_element_type=F32)

    _per_half(pl.program_id(0), TM, n_x, project)


def _inproj(xs, mod, w_in, l):
    return pl.pallas_call(
        functools.partial(_inproj_kernel, n_x=len(xs)),
        grid=(ROWS // TM,),
        in_specs=_tile_specs(len(xs), TM, 1) + [
            pl.BlockSpec((None, None, 6, D_MODEL), lambda i: (l, _cond_of_tile(i), 0, 0)),
            pl.BlockSpec((None, D_MODEL, IN_WIDTH), lambda i: (l, 0, 0), pipeline_mode=pl.Buffered(1)),
        ],
        out_specs=pl.BlockSpec((TM, IN_WIDTH), lambda i: (i, 0)),
        out_shape=jax.ShapeDtypeStruct((ROWS, IN_WIDTH), F32),
        scratch_shapes=[pltpu.VMEM((TM, D_MODEL), BF16)],
        compiler_params=pltpu.CompilerParams(
            dimension_semantics=("arbitrary",), vmem_limit_bytes=VMEM_LIMIT),
        name="inproj",
    )(*xs, mod, w_in)


TMF = 512
SEGF = TMF // 8
PITCHF = SEGF + 8
CTXF_TILES = N_CTX // TMF
LATF_PER_SEQ = DEC_SEQ // TMF
HALO_ROWS = 16
RBUF = 256
RB = 256
GB = 64
NFC = D_FF // FC
LANE_BLOCKS = D_MODEL // 128
assert RB == RBUF


def _seg_rows_f(xc_ref, k):
    return jnp.concatenate([xc_ref[cb, pl.ds(k, 8, stride=PITCHF), :] for cb in range(LANE_BLOCKS)], axis=1)


def _ffn_kernel(*refs, n_out):
    x_refs = refs[:2]
    (xp_ref, xn_ref, mod_ref, up_ref, cv_ref, dn_ref, ln_ref) = refs[2:9]
    o_refs = refs[9:9 + n_out]
    h_scr, act_scr, xc_scr, u0_scr, u1_scr = refs[9 + n_out:]
    u_scrs = (u0_scr, u1_scr)
    i = pl.program_id(0)
    is_ctx = i < CTXF_TILES
    lat_pos = (i - CTXF_TILES) % LATF_PER_SEQ
    m = mod_ref[...]
    ln = ln_ref[...]

    scale = 1.0 + m[4:5]
    shift = m[3:4]

    def stage(side):
        for cb in range(LANE_BLOCKS):
            for s in range(8):
                xc_scr[cb, s * PITCHF:s * PITCHF + SEGF, :] = x_refs[side][s * SEGF:(s + 1) * SEGF,
                                                                            cb * 128:(cb + 1) * 128]

    _per_half(i, TMF, 2, stage)

    def build_h(ub):
        for k in range(ub * RBUF // 8, (ub + 1) * RBUF // 8, 2):
            rows = jnp.concatenate([_seg_rows_f(xc_scr, k), _seg_rows_f(xc_scr, k + 1)], axis=0)
            h_scr[8 * k:8 * k + 16, :] = (rows * scale + shift).astype(BF16)

    sub16 = lax.broadcasted_iota(jnp.int32, (HALO_ROWS, D_MODEL), 0)
    prev_ok = jnp.logical_not(is_ctx) & (lat_pos > 0)
    next_ok = jnp.logical_not(is_ctx) & (lat_pos < LATF_PER_SEQ - 1)
    halo_x = jnp.where(sub16 == 0, xp_ref[POOL_HALO - 1:POOL_HALO, :], xn_ref[0:1, :])
    keep = ((sub16 == 0) & prev_ok) | ((sub16 == 1) & next_ok)
    h_scr[TMF:TMF + HALO_ROWS, :] = jnp.where(keep, halo_x * scale + shift, 0.0).astype(BF16)

    n_ub = TMF // RBUF
    sub = lax.broadcasted_iota(jnp.int32, (8, FC), 0)
    seg_per_seq = SEQ // SEGF
    ctx_first = is_ctx & (sub % seg_per_seq == 0)
    ctx_last = is_ctx & (sub % seg_per_seq == seg_per_seq - 1)

    def up_proj(slot, c, ub):
        rows = slice(ub * RBUF, (ub + 1) * RBUF + (HALO_ROWS if ub == n_ub - 1 else 0))
        u_scrs[slot][rows, :] = jnp.dot(h_scr[rows, :], up_ref[c], preferred_element_type=F32)

    def conv(u_ref, lanes, cvs, r0):
        lo = max(r0 - 8, 0)
        hi = min(r0 + GB + 8, TMF)
        ue = u_ref[lo:hi, lanes]
        u = ue[r0 - lo:r0 - lo + GB]
        if r0 == 0:
            b_first = jnp.where(sub == 0, u_ref[TMF:TMF + 1, lanes],
                                pltpu.roll(u_ref[TMF - 8:TMF, lanes], 1, axis=0))
            um1 = jnp.concatenate([jnp.where(ctx_first, 0.0, b_first), u[0:GB - 8]], axis=0)
        else:
            um1 = ue[0:GB]
        if r0 == TMF - GB:
            b_last = jnp.where(sub == 7, u_ref[TMF + 1:TMF + 2, lanes],
                               pltpu.roll(u_ref[0:8, lanes], 7, axis=0))
            up1 = jnp.concatenate([u[8:GB], jnp.where(ctx_last, 0.0, b_last)], axis=0)
        else:
            up1 = ue[r0 - lo + 8:r0 - lo + GB + 8]
        return um1 * cvs[0:1] + u * cvs[1:2] + up1 * cvs[2:3] + cvs[3:4]

    def gate(slot, c, ub):
        cva = cv_ref[0:4, c * FC:(c + 1) * FC]
        cvg = cv_ref[0:4, D_FF + c * FC:D_FF + (c + 1) * FC]
        for r0 in range(ub * RBUF, (ub + 1) * RBUF, GB):
            a = conv(u_scrs[slot], slice(0, FC), cva, r0)
            g = conv(u_scrs[slot], slice(FC, 2 * FC), cvg, r0)
            act_scr[c, r0:r0 + GB, :] = (_silu(a) * g).astype(BF16)

    def finish(rb):
        vrows = range(rb * RB // 8, (rb + 1) * RB // 8)
        lhs = jnp.concatenate([act_scr[c, rb * RB:(rb + 1) * RB, :] for c in range(NFC)], axis=1)
        y = jnp.dot(lhs, dn_ref[...].astype(BF16), preferred_element_type=F32)
        xr = jnp.concatenate([_seg_rows_f(xc_scr, k) for k in vrows], axis=0)
        out = _layer_norm(ALPHA * xr + m[5:6] * y, ln[1:2], ln[3:4])
        for kk, k in enumerate(vrows):
            for cb in range(LANE_BLOCKS):
                xc_scr[cb, pl.ds(k, 8, stride=PITCHF), :] = out[8 * kk:8 * kk + 8, cb * 128:(cb + 1) * 128]

    def write_out(side):
        for cb in range(LANE_BLOCKS):
            for s in range(8):
                o_refs[side][s * SEGF:(s + 1) * SEGF, cb * 128:(cb + 1) * 128] = xc_scr[
                    cb, s * PITCHF:s * PITCHF + SEGF, :]

    build_h(0)
    for c in range(NFC + 1):
        for ub in range(n_ub):
            if c == 0 and ub > 0:
                build_h(ub)
            if c < NFC:
                up_proj(c % 2, c, ub)
            if c >= 1:
                gate((c - 1) % 2, c - 1, ub)
            if c == NFC:
                finish(ub)

    _per_half(i, TMF, n_out, write_out)


def _ffn(xs, mod, up_chunks, conv8, down, ln8, l, split_out):
    halo_blocks = TMF // POOL_HALO
    last_halo = N_LAT // POOL_HALO - 1
    n_out = 2 if split_out else 1
    if split_out:
        out_shape = [jax.ShapeDtypeStruct((N_CTX, D_MODEL), F32), jax.ShapeDtypeStruct((N_LAT, D_MODEL), F32)]
    else:
        out_shape = [jax.ShapeDtypeStruct((ROWS, D_MODEL), F32)]
    return pl.pallas_call(
        functools.partial(_ffn_kernel, n_out=n_out),
        grid=(ROWS // TMF,),
        in_specs=[
            *_tile_specs(2, TMF, 1),
            pl.BlockSpec((POOL_HALO, D_MODEL),
                         lambda i: (jnp.maximum((i - CTXF_TILES) * halo_blocks - 1, 0), 0)),
            pl.BlockSpec((POOL_HALO, D_MODEL),
                         lambda i: (jnp.clip((i - CTXF_TILES + 1) * halo_blocks, 0, last_halo), 0)),
            pl.BlockSpec((None, None, 6, D_MODEL), lambda i: (l, _cond_of_tile(i, TMF), 0, 0)),
            pl.BlockSpec((None, NFC, D_MODEL, 2 * FC), lambda i: (l, 0, 0, 0), pipeline_mode=pl.Buffered(1)),
            pl.BlockSpec((None, 8, 2 * D_FF), lambda i: (l, 0, 0)),
            pl.BlockSpec((None, D_FF, D_MODEL), lambda i: (l, 0, 0), pipeline_mode=pl.Buffered(1)),
            pl.BlockSpec((None, 8, D_MODEL), lambda i: (l, 0, 0)),
        ],
        out_specs=_tile_specs(n_out, TMF, 1),
        out_shape=out_shape,
        scratch_shapes=[pltpu.VMEM((TMF + HALO_ROWS, D_MODEL), BF16),
                        pltpu.VMEM((NFC, TMF, FC), BF16),
                        pltpu.VMEM((LANE_BLOCKS, 8 * PITCHF, 128), F32),
                        pltpu.VMEM((TMF + HALO_ROWS, 2 * FC), F32),
                        pltpu.VMEM((TMF + HALO_ROWS, 2 * FC), F32)],
        compiler_params=pltpu.CompilerParams(
            dimension_semantics=("arbitrary",), vmem_limit_bytes=VMEM_LIMIT),
        name="convffn",
    )(xs[0], xs[1], xs[1], xs[1], mod, up_chunks, conv8, down, ln8)


def _head_masks(width):
    lane = lax.broadcasted_iota(jnp.int32, (1, width), 1)
    return [(lane >= h * HEAD_DIM) & (lane < (h + 1) * HEAD_DIM) for h in range(width // HEAD_DIM)]


def _stack_heads(x, masks):
    return jnp.concatenate([jnp.where(m, x, jnp.zeros_like(x)) for m in masks], axis=0)


def _rope(x, cos, sin):
    lane = lax.broadcasted_iota(jnp.int32, (1, 128), 1)
    lower = (lane & 31) < 16
    outs = []
    for k in range(x.shape[1] // 128):
        xb = x[:, k * 128:(k + 1) * 128]
        partner = jnp.where(lower, pltpu.roll(xb, 112, axis=1), pltpu.roll(xb, 16, axis=1))
        outs.append(xb * cos + partner * sin)
    return outs[0] if len(outs) == 1 else jnp.concatenate(outs, axis=1)


def _load_state(state_scr, blocks_ref, d):
    state_scr[...] = jnp.zeros((GW, GW), F32)
    for h in range(N_HEADS):
        sl = slice(h * HEAD_DIM, (h + 1) * HEAD_DIM)
        state_scr[sl, sl] = blocks_ref[d, h]


def _store_state(blocks_ref, d, state_scr):
    for h in range(N_HEADS):
        sl = slice(h * HEAD_DIM, (h + 1) * HEAD_DIM)
        blocks_ref[d, h] = state_scr[sl, sl]


LAT_GROUP = 4


def _chunk_off(c):
    return c * CHUNK if isinstance(c, int) else pl.multiple_of(c * CHUNK, CHUNK)


def _interleave(chains):
    chains = list(chains)
    while chains:
        for ch in list(chains):
            try:
                next(ch)
            except StopIteration:
                chains.remove(ch)


def _mixer_kernel(*refs, latent, group, ng, cast_up=False):
    if latent:
        (qa_ref, ka_ref, va_ref, kx_ref, vx_ref, ub_ref, vb_ref, qc_ref, kc_ref, vc_ref, gf_ref, gb_ref,
         pd_ref, pdp_ref, pdn_ref, cos_ref, sin_ref, s0_ref, dmat_ref, dec_ref, sink_ref, ws_ref, bias_ref,
         vec_ref, gmat_ref, cnt_ref, wpool_ref, xres_ref, mod_ref, wout_ref, ln_ref,
         x1_ref, sf_scr, sb_scr, ob_scr, pext_scr, mix_ref, k_scr, v_scr) = refs
    else:
        (qa_ref, ka_ref, va_ref, ub_ref, vb_ref, qc_ref, kc_ref, vc_ref, gf_ref, gb_ref, pd_ref,
         dmat_ref, dec_ref, sink_ref, ws_ref, bias_ref, vec_ref, gmat_ref, cnt_ref, wpool_ref,
         xres_ref, mod_ref, wout_ref, ln_ref) = refs[:24]
        if cast_up:
            upa_ref, upg_ref, x1_ref, st_ref, kn_ref, vn_ref, upo_ref = refs[24:31]
        else:
            x1_ref, st_ref, kn_ref, vn_ref = refs[24:28]
        sf_scr, sb_scr, ob_scr, pext_scr, mix_ref = refs[-5:]

    nc = group * ng
    g_a = mod_ref[...][2:3]
    ln = ln_ref[...]
    masks = _head_masks(GW)
    row = lax.broadcasted_iota(jnp.int32, (GW, GW), 0)
    col = lax.broadcasted_iota(jnp.int32, (GW, GW), 1)
    bd_mask = (row // HEAD_DIM) == (col // HEAD_DIM)
    vec = vec_ref[...]
    lane = lax.broadcasted_iota(jnp.int32, (1, 2 * HEAD_DIM), 1)
    lo = lane < HEAD_DIM

    def rows(k):
        return slice(k * CHUNK, (k + 1) * CHUNK)

    def retention(order, d, state_scr, out):
        dmat = dmat_ref[d]
        qdec, kdec = dec_ref[d], dec_ref[2 + d]
        cdec = vec[5 + d:6 + d]
        state = state_scr[...]
        for k in order:
            q, kk, v = qc_ref[rows(k), :], kc_ref[rows(k), :], vc_ref[rows(k), :]
            qb, kb, vb = q.astype(BF16), kk.astype(BF16), v.astype(BF16)
            s = lax.dot_general(_stack_heads(qb, masks), kb, (((1,), (1,)), ((), ())),
                                preferred_element_type=F32)
            yield
            p = (s * dmat).astype(BF16)
            p_cat = jnp.concatenate([p[h * CHUNK:(h + 1) * CHUNK] for h in range(N_HEADS)], axis=1)
            o = (jnp.dot(p_cat, _stack_heads(vb, masks), preferred_element_type=F32)
                 + _bdot(q * qdec, state))
            yield
            upd = lax.dot_general((kk * kdec).astype(BF16), vb, (((0,), (0,)), ((), ())),
                                  preferred_element_type=F32)
            state = state * cdec + jnp.where(bd_mask, upd, 0.0)
            out[k] = o
            yield
        state_scr[...] = state

    def stage_kv(k, c):
        dst = pl.ds(_chunk_off(c + 1), CHUNK)
        k_scr[dst, :] = _rope(ka_ref[rows(k), :], cos_ref[rows(k), :], sin_ref[rows(k), :]).astype(BF16)
        yield
        v_scr[dst, :] = va_ref[rows(k), :].astype(BF16)
        yield

    def attention(k, c, done):
        q = qa_ref[rows(k), :]
        if latent:
            q = _rope(q, cos_ref[rows(k), :], sin_ref[rows(k), :])
        q = q * (HEAD_DIM ** -0.5)
        q0, q1 = q[:, :128], q[:, 128:]
        zero = jnp.zeros_like(q0)
        q_st = jnp.concatenate([
            jnp.where(lo, q0, zero),
            jnp.where(lo, pltpu.roll(q0, HEAD_DIM, axis=1), zero),
            jnp.where(lo, zero, pltpu.roll(q1, HEAD_DIM, axis=1)),
            jnp.where(lo, zero, q1)], axis=0).astype(BF16)
        if latent:
            band = pl.ds(_chunk_off(c), 3 * CHUNK)
            k_all = jnp.concatenate([k_scr[band, :], kx_ref[...].astype(BF16)], axis=0)
            v_all = jnp.concatenate([v_scr[band, :], vx_ref[...].astype(BF16)], axis=0)
        else:
            k_all = ka_ref[...].astype(BF16)
            v_all = va_ref[...].astype(BF16)
        s = lax.dot_general(q_st, k_all, (((1,), (1,)), ((), ())), preferred_element_type=F32)
        yield
        if latent:
            nk = 3 * CHUNK + PAST_LEN
            qi = lax.broadcasted_iota(jnp.int32, (N_HEADS * CHUNK, nk), 0) & (CHUNK - 1)
            kj = lax.broadcasted_iota(jnp.int32, (N_HEADS * CHUNK, nk), 1)
            kpos = kj + (c - 1) * CHUNK
            valid = (kj >= 3 * CHUNK) | ((kj >= qi) & (kj <= qi + 2 * CHUNK)
                                         & (kpos >= 0) & (kpos < nc * CHUNK))
            s = jnp.where(valid, s, NEG_INF)
        sink = sink_ref[...][:, 0:1]
        mx = jnp.maximum(jnp.max(s, axis=-1, keepdims=True), sink)
        yield
        e = jnp.exp(s - mx)
        den = jnp.sum(e, axis=-1, keepdims=True) + jnp.exp(sink - mx)
        yield
        o = jnp.dot(e.astype(BF16), v_all, preferred_element_type=F32) / den
        yield
        mix_ref[rows(k), 0:128] = jnp.where(lo, o[0:CHUNK], pltpu.roll(o[CHUNK:2 * CHUNK], HEAD_DIM, axis=1))
        mix_ref[rows(k), 128:256] = jnp.where(lo, pltpu.roll(o[2 * CHUNK:3 * CHUNK], HEAD_DIM, axis=1),
                                              o[3 * CHUNK:4 * CHUNK])
        done.add(("A", k))
        yield

    def gating_unit(k, done):
        vn = _layer_norm(vb_ref[rows(k), :], vec[0:1], vec[1:2]).astype(BF16)
        yield
        sg = jnp.dot(ws_ref[...], _stack_heads(vn, masks), preferred_element_type=F32) + bias_ref[...]
        yield
        mix_ref[rows(k), GW:2 * GW] = ub_ref[rows(k), :] * sg
        done.add(("B", k))
        yield

    def retention_mix(k, o_fwd, o_bwd, done):
        while k not in o_fwd or k not in o_bwd:
            yield
        gmat = gmat_ref[...]
        normed = []
        for o in (o_fwd[k], o_bwd[k]):
            mu = jnp.dot(o.astype(BF16), gmat, preferred_element_type=F32)
            yield
            dlt = o - mu
            var = jnp.dot((dlt * dlt).astype(BF16), gmat, preferred_element_type=F32)
            yield
            normed.append(dlt * lax.rsqrt(var + LN_EPS))
        mix_ref[rows(k), 2 * GW:3 * GW] = (_silu(gf_ref[rows(k), :]) * (normed[0] * vec[3:4])
                                           + _silu(gb_ref[rows(k), :]) * (normed[1] * vec[4:5]))
        done.add(("C", k))
        yield

    def pooling(k, c, done):
        pd = pd_ref[rows(k), :]
        pext = pext_scr.at[k]
        zeros8 = jnp.zeros((POOL_HALO, GW), F32)
        if k > 0:
            prev8 = pd_ref[k * CHUNK - POOL_HALO:k * CHUNK, :]
        else:
            prev8 = jnp.where(c > 0, pdp_ref[...], 0.0) if latent else zeros8
        if k < group - 1:
            next8 = pd_ref[(k + 1) * CHUNK:(k + 1) * CHUNK + POOL_HALO, :]
        else:
            next8 = jnp.where(c < nc - 1, pdn_ref[...], 0.0) if latent else zeros8
        pext[0:POOL_HALO, :] = prev8
        pext[POOL_HALO:POOL_HALO + CHUNK, :] = pd
        pext[POOL_HALO + CHUNK:2 * POOL_HALO + CHUNK, :] = next8
        yield

        def win(d, half):
            return pext[pl.ds(POOL_HALO + d, CHUNK), half * 128:(half + 1) * 128]

        a2 = win(-1, 0) + win(0, 0)
        a4 = a2 + win(-2, 0) + win(1, 0)
        yield
        a8 = win(-4, 1)
        for d in range(-3, 4):
            a8 = a8 + win(d, 1)
        yield
        a16 = a8
        for d in list(range(-8, -4)) + list(range(4, 8)):
            a16 = a16 + win(d, 1)
        yield
        sums = jnp.concatenate([jnp.where(lo, a2, a4), jnp.where(lo, a8, a16)], axis=1)
        yd = sums * cnt_ref[rows(k), :] - pd
        mix_ref[rows(k), 3 * GW:4 * GW] = _bdot(yd, wpool_ref[...]) * vec[2:3]
        done.add(("D", k))
        yield

    def out_projection(k, done):
        while not all((m, k) in done for m in "ABCD"):
            yield
        y = jnp.dot(mix_ref[rows(k), :].astype(BF16), wout_ref[...], preferred_element_type=F32)
        yield
        x1_ref[rows(k), :] = _layer_norm(ALPHA * xres_ref[rows(k), :] + g_a * y, ln[0:1], ln[2:3])
        yield

    def up_weight_rows():
        for lyr in range(DEPTH):
            for cc in range(NFC):
                upo_ref[lyr, cc, :, 0:FC] = upa_ref[lyr, :, cc * FC:(cc + 1) * FC].astype(BF16)
                upo_ref[lyr, cc, :, FC:2 * FC] = upg_ref[lyr, :, cc * FC:(cc + 1) * FC].astype(BF16)
                yield

    def forward_chains(chunk_of, o_bwd):
        done, o_fwd = set(), {}
        chains = [retention(range(group), 0, sf_scr, o_fwd)]
        for k in range(group):
            chains += [attention(k, chunk_of(k), done), gating_unit(k, done), pooling(k, chunk_of(k), done),
                       retention_mix(k, o_fwd, o_bwd, done), out_projection(k, done)]
        return chains

    if not latent:
        sb_scr[...] = jnp.zeros((GW, GW), F32)
        sf_scr[...] = jnp.zeros((GW, GW), F32)
        kn_ref[...] = ka_ref[...]
        vn_ref[...] = va_ref[...]
        o_bwd = {}
        side_chains = [retention(reversed(range(group)), 1, sb_scr, o_bwd)]
        if cast_up:
            side_chains.append(up_weight_rows())
        _interleave(side_chains + forward_chains(lambda k: k, o_bwd))
        _store_state(st_ref, 1, sb_scr)
        _store_state(st_ref, 0, sf_scr)
        return

    p = pl.program_id(1)
    g = pl.program_id(2)

    @pl.when(p == 0)
    def _():
        @pl.when(g == 0)
        def _():
            _load_state(sb_scr, s0_ref, 1)
            zero_blk = jnp.zeros((CHUNK, 2 * HEAD_DIM), BF16)
            for scr in (k_scr, v_scr):
                scr[0:CHUNK, :] = zero_blk
                scr[(nc + 1) * CHUNK:(nc + 2) * CHUNK, :] = zero_blk

        first = (ng - 1 - g) * group
        o_bwd = {}
        _interleave([retention(reversed(range(group)), 1, sb_scr, o_bwd)]
                    + [stage_kv(k, first + k) for k in range(group)])
        for k in range(group):
            ob_scr[pl.ds(_chunk_off(first + k), CHUNK), :] = o_bwd[k]

    @pl.when(p == 1)
    def _():
        @pl.when(g == 0)
        def _():
            _load_state(sf_scr, s0_ref, 0)

        first = g * group
        o_bwd = {k: ob_scr[pl.ds(_chunk_off(first + k), CHUNK), :] for k in range(group)}
        _interleave(forward_chains(lambda k: first + k, o_bwd))


def _mixer(z, x_res, mod, w_out_bf16, ln8, tabs, l, latent, extra=None, ffn_up=None):
    nb = DEC_BATCH if latent else BATCH
    nc = (DEC_SEQ if latent else SEQ) // CHUNK
    group = LAT_GROUP if latent else nc
    ng = nc // group
    blk = group * CHUNK
    base = (N_CTX // blk) if latent else 0
    per8 = blk // POOL_HALO
    last_halo = ROWS // POOL_HALO - 1

    def on_grid(f):
        return (lambda b, p, g: f(b, p, g)) if latent else (lambda b: f(b, 1, 0))

    def fwd(b, p, g):
        return base + b * ng + g * p

    def both(b, p, g):
        return base + b * ng + jnp.where(p == 0, ng - 1 - g, g)

    def bwd_only(b, p, g):
        return base + b * ng + (ng - 1 - g) * (1 - p)

    def col(width, idx, rowmap):
        return pl.BlockSpec((blk, width), on_grid(lambda b, p, g: (rowmap(b, p, g), idx)))

    def const(shape):
        return pl.BlockSpec(shape, on_grid(lambda b, p, g: (0,) * len(shape)))

    def layer(shape):
        return pl.BlockSpec((None,) + shape, on_grid(lambda b, p, g: (l,) + (0,) * len(shape)))

    specs, args = [], []

    def add(spec, arr):
        specs.append(spec)
        args.append(arr)

    add(col(GW, 0, fwd), z)
    add(col(128, 2, bwd_only if latent else fwd), z)
    add(col(128, 3, bwd_only if latent else fwd), z)
    if latent:
        add(pl.BlockSpec((None, None, PAST_LEN, 128), lambda b, p, g: (b, l, 0, 0)), extra["ck"])
        add(pl.BlockSpec((None, None, PAST_LEN, 128), lambda b, p, g: (b, l, 0, 0)), extra["cv"])
    add(col(GW, 2, fwd), z)
    add(col(GW, 3, fwd), z)
    add(col(GW, 4, both), z)
    add(col(GW, 5, both), z)
    add(col(GW, 6, both), z)
    add(col(GW, 7, fwd), z)
    add(col(GW, 8, fwd), z)
    add(col(GW, 9, fwd), z)
    if latent:
        add(pl.BlockSpec((POOL_HALO, GW),
                         lambda b, p, g: (jnp.maximum(fwd(b, p, g) * per8 - 1, 0), 9)), z)
        add(pl.BlockSpec((POOL_HALO, GW),
                         lambda b, p, g: (jnp.minimum((fwd(b, p, g) + 1) * per8, last_halo), 9)), z)
        rope_map = lambda b, p, g: (jnp.where(p == 0, ng - 1 - g, g), 0)
        add(pl.BlockSpec((blk, 128), rope_map), extra["cos"])
        add(pl.BlockSpec((blk, 128), rope_map), extra["sin"])
        add(pl.BlockSpec((None, None, 2, N_HEADS, HEAD_DIM, HEAD_DIM),
                         lambda b, p, g: (b, l, 0, 0, 0, 0)), extra["s0"])
    add(layer((2, N_HEADS * CHUNK, CHUNK)), tabs["dmat"])
    add(layer((4, CHUNK, GW)), tabs["dec"])
    add(layer((N_HEADS * CHUNK, 128)), tabs["sink"])
    add(layer((CHUNK, N_HEADS * CHUNK)), tabs["ws"])
    add(layer((CHUNK, GW)), tabs["bias"])
    add(layer((8, GW)), tabs["vec"])
    add(const((GW, GW)), tabs["gmat"])
    add(pl.BlockSpec((blk, GW), on_grid(lambda b, p, g: (g * p, 0))),
        tabs["cnt_lat"] if latent else tabs["cnt_ctx"])
    add(layer((GW, GW)), tabs["wpool"])
    local = lambda b, p, g: (b * ng + g * p, 0)
    add(pl.BlockSpec((blk, D_MODEL), on_grid(local)), x_res)
    add(pl.BlockSpec((None, None, 6, D_MODEL),
                     on_grid(lambda b, p, g: (l, (1 + b) if latent else 0, 0, 0))), mod)
    add(layer((D_MODEL, D_MODEL)), w_out_bf16)
    add(layer((8, D_MODEL)), ln8)
    up_rows = D_MODEL // nb
    if ffn_up is not None:
        add(pl.BlockSpec((DEPTH, up_rows, D_FF), lambda b: (0, b, 0)), ffn_up)
        add(pl.BlockSpec((DEPTH, up_rows, D_FF), lambda b: (0, b, 1)), ffn_up)

    out_shape = [jax.ShapeDtypeStruct((nb * nc * CHUNK, D_MODEL), F32)]
    out_specs = [pl.BlockSpec((blk, D_MODEL), on_grid(local))]
    scratch = [pltpu.VMEM((GW, GW), F32), pltpu.VMEM((GW, GW), F32),
               pltpu.VMEM((nc * CHUNK, GW), F32),
               pltpu.VMEM((group, CHUNK + 2 * POOL_HALO, GW), F32),
               pltpu.VMEM((blk, D_MODEL), F32)]
    if latent:
        scratch += [pltpu.VMEM(((nc + 2) * CHUNK, 128), BF16), pltpu.VMEM(((nc + 2) * CHUNK, 128), BF16)]
    else:
        out_shape.append(jax.ShapeDtypeStruct((nb, 2, N_HEADS, HEAD_DIM, HEAD_DIM), F32))
        out_specs.append(pl.BlockSpec((None, 2, N_HEADS, HEAD_DIM, HEAD_DIM), lambda b: (b, 0, 0, 0, 0)))
        for _ in range(2):
            out_shape.append(jax.ShapeDtypeStruct((nb, SEQ, 128), F32))
            out_specs.append(pl.BlockSpec((None, SEQ, 128), lambda b: (b, 0, 0)))
        if ffn_up is not None:
            out_shape.append(jax.ShapeDtypeStruct((DEPTH, NFC, D_MODEL, 2 * FC), BF16))
            out_specs.append(pl.BlockSpec((DEPTH, NFC, up_rows, 2 * FC), lambda b: (0, 0, b, 0)))

    return pl.pallas_call(
        functools.partial(_mixer_kernel, latent=latent, group=group, ng=ng, cast_up=ffn_up is not None),
        grid=(nb, 2, ng) if latent else (nb,),
        in_specs=specs,
        out_specs=out_specs,
        out_shape=out_shape,
        scratch_shapes=scratch,
        compiler_params=pltpu.CompilerParams(
            dimension_semantics=("arbitrary",) * (3 if latent else 1), vmem_limit_bytes=VMEM_LIMIT),
        name="mixer_latent" if latent else "mixer_context",
    )(*args)


def _pad_rows(rows, n=8):
    a = jnp.stack(rows)
    return jnp.concatenate([a, jnp.zeros((n - a.shape[0],) + a.shape[1:], a.dtype)], axis=0)


def _block_diag(blocks):
    g, n, _ = blocks.shape
    eye = jnp.eye(g, dtype=blocks.dtype)
    return (eye[:, None, :, None] * blocks[:, :, None, :]).reshape(g * n, g * n)


def _inv_count(n):
    t = np.arange(n)
    cols = []
    for w in POOL_WINDOWS:
        cnt = np.clip(t + w // 2, 0, n) - np.clip(t - w // 2, 0, n)
        cols.append(np.repeat((1.0 / cnt)[:, None], HEAD_DIM, axis=1))
    return jnp.asarray(np.concatenate(cols, axis=1), F32)


def _rope_tables():
    rows = DEC_SEQ // GRID_W
    r, cc = jnp.meshgrid(jnp.arange(rows), jnp.arange(GRID_W), indexing="ij")
    half = HEAD_DIM // 2
    freqs = ROPE_BASE ** (-jnp.arange(0, half, 2, dtype=F32) / half)

    def tables(pos):
        ang = pos.reshape(-1).astype(F32)[:, None] * freqs[None, :]
        cos, sin = jnp.cos(ang), jnp.sin(ang)
        return jnp.concatenate([cos, cos], axis=1), jnp.concatenate([-sin, sin], axis=1)

    cr, sr = tables(r)
    ccol, scol = tables(cc)
    cos = jnp.concatenate([cr, ccol], axis=1)
    sin = jnp.concatenate([sr, scol], axis=1)
    return jnp.tile(cos, (1, 2)), jnp.tile(sin, (1, 2))


def _layer_tables(attn_sink, sgu_norm_w, sgu_norm_b, sgu_ws, sgu_bs, ret_decay, ret_gn_w, pool_w, pool_scale):
    log_g = jax.nn.log_sigmoid(ret_decay.astype(F32))
    i = jnp.arange(CHUNK, dtype=F32)
    rel = i[:, None] - i[None, :]
    kscale = HEAD_DIM ** -0.5
    d_f = jnp.where(rel >= 0, jnp.exp(jnp.maximum(rel, 0.0)[None] * log_g[0][:, None, None]), 0.0)
    d_b = jnp.where(rel <= 0, jnp.exp(jnp.maximum(-rel, 0.0)[None] * log_g[1][:, None, None]), 0.0)
    dmat = jnp.stack([d_f.reshape(N_HEADS * CHUNK, CHUNK), d_b.reshape(N_HEADS * CHUNK, CHUNK)]) * kscale

    def lanes(per_head):
        return jnp.repeat(per_head, HEAD_DIM, axis=1)

    qdec_f = lanes(jnp.exp((i + 1.0)[:, None] * log_g[0][None, :]))
    qdec_b = lanes(jnp.exp((CHUNK - i)[:, None] * log_g[1][None, :]))
    kdec_f = lanes(jnp.exp((CHUNK - 1.0 - i)[:, None] * log_g[0][None, :])) * kscale
    kdec_b = lanes(jnp.exp(i[:, None] * log_g[1][None, :])) * kscale
    cdec = jnp.repeat(jnp.exp(CHUNK * log_g), HEAD_DIM, axis=1)
    vec = _pad_rows([sgu_norm_w, sgu_norm_b, pool_scale, ret_gn_w[0], ret_gn_w[1], cdec[0], cdec[1]])
    return {
        "dmat": dmat,
        "dec": jnp.stack([qdec_f, qdec_b, kdec_f, kdec_b]),
        "sink": jnp.broadcast_to(jnp.repeat(attn_sink, CHUNK)[:, None], (N_HEADS * CHUNK, 128)),
        "ws": jnp.concatenate([sgu_ws[h] for h in range(N_HEADS)], axis=1).astype(BF16),
        "bias": jnp.repeat(sgu_bs.T, HEAD_DIM, axis=1),
        "vec": vec,
        "wpool": _block_diag(pool_w).astype(BF16),
    }


def kernel(x_prompt, x_sample, cache_attn_k, cache_attn_v, state_ret, c, c_ctx, w_ada, b_ada, w_in,
           w_out, attn_sink, sgu_norm_w, sgu_norm_b, sgu_ws, sgu_bs, ret_decay, ret_gn_w, pool_w,
           pool_scale, ffn_up, ffn_conv_w, ffn_conv_b, ffn_down, ln_w, ln_b):
    cond8 = jnp.concatenate([c_ctx[None], c, jnp.zeros((8 - 1 - DEC_BATCH, D_MODEL), F32)], axis=0)
    mod = _modulation(cond8, w_ada, b_ada).reshape(DEPTH, 8, 6, D_MODEL)

    tabs = jax.vmap(_layer_tables)(attn_sink, sgu_norm_w, sgu_norm_b, sgu_ws, sgu_bs, ret_decay, ret_gn_w,
                                   pool_w, pool_scale)
    tabs["gmat"] = _block_diag(jnp.full((N_HEADS, HEAD_DIM, HEAD_DIM), 1.0 / HEAD_DIM, F32)).astype(BF16)
    tabs["cnt_ctx"] = _inv_count(SEQ)
    tabs["cnt_lat"] = _inv_count(DEC_SEQ)
    cos, sin = _rope_tables()
    extra = {"ck": cache_attn_k.reshape(DEC_BATCH, DEPTH, PAST_LEN, 128),
             "cv": cache_attn_v.reshape(DEC_BATCH, DEPTH, PAST_LEN, 128),
             "cos": cos, "sin": sin, "s0": state_ret}
    ln8 = jnp.concatenate([ln_w, ln_b, jnp.zeros((DEPTH, 4, D_MODEL), F32)], axis=1)
    conv8 = jnp.concatenate([ffn_conv_w, ffn_conv_b[:, None], jnp.zeros((DEPTH, 4, 2 * D_FF), F32)], axis=1)

    w_out_bf16 = _to_bf16(w_out, D_MODEL)

    xs = [x_prompt.reshape(N_CTX, D_MODEL), x_sample.reshape(N_LAT, D_MODEL)]
    new_k, new_v, new_s = [], [], []
    for l in range(DEPTH):
        z = _inproj(xs, mod, w_in, l)
        if l == 0:
            x1_ctx, st, kn, vn, up_chunks = _mixer(z, xs[0], mod, w_out_bf16, ln8, tabs, l, latent=False,
                                                   ffn_up=ffn_up)
        else:
            x1_ctx, st, kn, vn = _mixer(z, xs[0], mod, w_out_bf16, ln8, tabs, l, latent=False)
        (x1_lat,) = _mixer(z, xs[1], mod, w_out_bf16, ln8, tabs, l, latent=True, extra=extra)
        xs = _ffn([x1_ctx, x1_lat], mod, up_chunks, conv8, ffn_down, ln8, l, split_out=True)
        new_k.append(kn.reshape(BATCH, SEQ, 2, HEAD_DIM))
        new_v.append(vn.reshape(BATCH, SEQ, 2, HEAD_DIM))
        new_s.append(st)

    y_prompt = xs[0].reshape(BATCH, SEQ, D_MODEL)
    y_sample = xs[1].reshape(DEC_BATCH, DEC_SEQ, D_MODEL)
    return (y_prompt, y_sample, jnp.stack(new_k, axis=1), jnp.stack(new_v, axis=1),
            jnp.stack(new_s, axis=1))
```

```python
import functools

import numpy as np
import jax
import jax.numpy as jnp
from jax import lax
from jax.experimental import pallas as pl
from jax.experimental.pallas import tpu as pltpu

F32 = jnp.float32
BF16 = jnp.bfloat16

D_MODEL = 1024
BATCH = 16
SEQ = 256
DEPTH = 2
DEC_BATCH = 2
DEC_SEQ = 2048
PAST_LEN = 256
GRID_W = 64
CHUNK = 128
HEAD_DIM = 64
GW = D_MODEL // 4
N_HEADS = 4
POOL_WINDOWS = (2, 4, 8, 16)
POOL_HALO = 8
D_FF = 2816
ROPE_BASE = 10000.0
LN_EPS = 1e-5
NEG_INF = -1e30
IN_WIDTH = 10 * GW
ALPHA = (2.0 * DEPTH) ** 0.25

N_CTX = BATCH * SEQ
N_LAT = DEC_BATCH * DEC_SEQ
ROWS = N_CTX + N_LAT

TM = 1024
NB_IN = 512
FC = 256
NB_ADA = 1536
VMEM_LIMIT = 56 * 1024 * 1024


def _cond_of_tile(i, tm=TM):
    ctx_tiles = N_CTX // tm
    return jnp.where(i < ctx_tiles, 0, 1 + (i - ctx_tiles) // (DEC_SEQ // tm))


def _tile_specs(n_src, tm, grid_rank):
    ctx_tiles = N_CTX // tm
    if n_src == 1:
        rows = [lambda i: i]
    else:
        rows = [lambda i: jnp.minimum(i, ctx_tiles - 1), lambda i: jnp.maximum(i - ctx_tiles, 0)]
    if grid_rank == 1:
        return [pl.BlockSpec((tm, D_MODEL), lambda i, f=f: (f(i), 0)) for f in rows]
    return [pl.BlockSpec((tm, D_MODEL), lambda i, j, f=f: (f(i), 0)) for f in rows]


def _per_half(i, tm, n_max, fn):
    if n_max == 1:
        fn(0)
        return
    ctx_tiles = N_CTX // tm

    @pl.when(i < ctx_tiles)
    def _():
        fn(0)

    @pl.when(i >= ctx_tiles)
    def _():
        fn(1)


def _layer_norm(x, w, b):
    mu = jnp.mean(x, axis=-1, keepdims=True)
    d = x - mu
    var = jnp.mean(d * d, axis=-1, keepdims=True)
    return d * lax.rsqrt(var + LN_EPS) * w + b


def _silu(x):
    return x * jax.nn.sigmoid(x)


def _bdot(a, b):
    return jnp.dot(a.astype(BF16), b.astype(BF16), preferred_element_type=F32)


def _mod_kernel(c_ref, w_ref, b_ref, o_ref):
    o_ref[...] = _bdot(_silu(c_ref[...]), w_ref[...]) + b_ref[...]


def _modulation(cond8, w_ada, b_ada):
    return pl.pallas_call(
        _mod_kernel,
        grid=(DEPTH, 6 * D_MODEL // NB_ADA),
        in_specs=[
            pl.BlockSpec((8, D_MODEL), lambda l, j: (0, 0)),
            pl.BlockSpec((None, D_MODEL, NB_ADA), lambda l, j: (l, 0, j)),
            pl.BlockSpec((None, 1, NB_ADA), lambda l, j: (l, 0, j)),
        ],
        out_specs=pl.BlockSpec((None, 8, NB_ADA), lambda l, j: (l, 0, j)),
        out_shape=jax.ShapeDtypeStruct((DEPTH, 8, 6 * D_MODEL), F32),
        compiler_params=pltpu.CompilerParams(
            dimension_semantics=("arbitrary", "arbitrary"), vmem_limit_bytes=VMEM_LIMIT),
        name="modulation",
    )(cond8, w_ada, b_ada.reshape(DEPTH, 1, 6 * D_MODEL))


def _cast_kernel(w_ref, o_ref):
    o_ref[...] = w_ref[...].astype(BF16)


def _to_bf16(w, block_rows):
    depth, rows, cols = w.shape
    return pl.pallas_call(
        _cast_kernel,
        grid=(depth, rows // block_rows),
        in_specs=[pl.BlockSpec((None, block_rows, cols), lambda l, r: (l, r, 0))],
        out_specs=pl.BlockSpec((None, block_rows, cols), lambda l, r: (l, r, 0)),
        out_shape=jax.ShapeDtypeStruct(w.shape, BF16),
        compiler_params=pltpu.CompilerParams(dimension_semantics=("arbitrary", "arbitrary")),
        name="cast_bf16",
    )(w)


def _inproj_kernel(*refs, n_x):
    x_refs = refs[:n_x]
    mod_ref, w_ref, z_ref, h_scr = refs[n_x:]

    m = mod_ref[...]

    def project(side):
        for rh in range(2):
            rows = slice(rh * TM // 2, (rh + 1) * TM // 2)
            h_scr[rows, :] = (x_refs[side][rows, :] * (1.0 + m[1:2]) + m[0:1]).astype(BF16)
            for jb in range(IN_WIDTH // NB_IN):
                cols = slice(jb * NB_IN, (jb + 1) * NB_IN)
                z_ref[rows, cols] = jnp.dot(h_scr[rows, :], w_ref[:, cols].astype(BF16),
                                            preferred_element_type=F32)

    _per_half(pl.program_id(0), TM, n_x, project)


def _inproj(xs, mod, w_in, l):
    return pl.pallas_call(
        functools.partial(_inproj_kernel, n_x=len(xs)),
        grid=(ROWS // TM,),
        in_specs=_tile_specs(len(xs), TM, 1) + [
            pl.BlockSpec((None, None, 6, D_MODEL), lambda i: (l, _cond_of_tile(i), 0, 0)),
            pl.BlockSpec((None, D_MODEL, IN_WIDTH), lambda i: (l, 0, 0), pipeline_mode=pl.Buffered(1)),
        ],
        out_specs=pl.BlockSpec((TM, IN_WIDTH), lambda i: (i, 0)),
        out_shape=jax.ShapeDtypeStruct((ROWS, IN_WIDTH), F32),
        scratch_shapes=[pltpu.VMEM((TM, D_MODEL), BF16)],
        compiler_params=pltpu.CompilerParams(
            dimension_semantics=("arbitrary",), vmem_limit_bytes=VMEM_LIMIT),
        name="inproj",
    )(*xs, mod, w_in)


TMF = 512
SEGF = TMF // 8
PITCHF = SEGF + 8
CTXF_TILES = N_CTX // TMF
LATF_PER_SEQ = DEC_SEQ // TMF
HALO_ROWS = 16
RBUF = 256
RB = 256
GB = 64
NFC = D_FF // FC
LANE_BLOCKS = D_MODEL // 128
assert RB == RBUF


def _seg_rows_f(xc_ref, k):
    return jnp.concatenate([xc_ref[cb, pl.ds(k, 8, stride=PITCHF), :] for cb in range(LANE_BLOCKS)], axis=1)


def _ffn_kernel(*refs, n_out):
    x_refs = refs[:2]
    (xp_ref, xn_ref, mod_ref, up_ref, cv_ref, dn_ref, ln_ref) = refs[2:9]
    o_refs = refs[9:9 + n_out]
    h_scr, act_scr, xc_scr, u0_scr, u1_scr = refs[9 + n_out:]
    u_scrs = (u0_scr, u1_scr)
    i = pl.program_id(0)
    is_ctx = i < CTXF_TILES
    lat_pos = (i - CTXF_TILES) % LATF_PER_SEQ
    m = mod_ref[...]
    ln = ln_ref[...]

    scale = 1.0 + m[4:5]
    shift = m[3:4]

    def stage(side):
        for cb in range(LANE_BLOCKS):
            for s in range(8):
                xc_scr[cb, s * PITCHF:s * PITCHF + SEGF, :] = x_refs[side][s * SEGF:(s + 1) * SEGF,
                                                                            cb * 128:(cb + 1) * 128]

    _per_half(i, TMF, 2, stage)

    def build_h(ub):
        for k in range(ub * RBUF // 8, (ub + 1) * RBUF // 8, 2):
            rows = jnp.concatenate([_seg_rows_f(xc_scr, k), _seg_rows_f(xc_scr, k + 1)], axis=0)
            h_scr[8 * k:8 * k + 16, :] = (rows * scale + shift).astype(BF16)

    sub16 = lax.broadcasted_iota(jnp.int32, (HALO_ROWS, D_MODEL), 0)
    prev_ok = jnp.logical_not(is_ctx) & (lat_pos > 0)
    next_ok = jnp.logical_not(is_ctx) & (lat_pos < LATF_PER_SEQ - 1)
    halo_x = jnp.where(sub16 == 0, xp_ref[POOL_HALO - 1:POOL_HALO, :], xn_ref[0:1, :])
    keep = ((sub16 == 0) & prev_ok) | ((sub16 == 1) & next_ok)
    h_scr[TMF:TMF + HALO_ROWS, :] = jnp.where(keep, halo_x * scale + shift, 0.0).astype(BF16)

    n_ub = TMF // RBUF
    sub = lax.broadcasted_iota(jnp.int32, (8, FC), 0)
    seg_per_seq = SEQ // SEGF
    ctx_first = is_ctx & (sub % seg_per_seq == 0)
    ctx_last = is_ctx & (sub % seg_per_seq == seg_per_seq - 1)

    def up_proj(slot, c, ub):
        rows = slice(ub * RBUF, (ub + 1) * RBUF + (HALO_ROWS if ub == n_ub - 1 else 0))
        u_scrs[slot][rows, :] = jnp.dot(h_scr[rows, :], up_ref[c], preferred_element_type=F32)

    def conv(u_ref, lanes, cvs, r0):
        lo = max(r0 - 8, 0)
        hi = min(r0 + GB + 8, TMF)
        ue = u_ref[lo:hi, lanes]
        u = ue[r0 - lo:r0 - lo + GB]
        if r0 == 0:
            b_first = jnp.where(sub == 0, u_ref[TMF:TMF + 1, lanes],
                                pltpu.roll(u_ref[TMF - 8:TMF, lanes], 1, axis=0))
            um1 = jnp.concatenate([jnp.where(ctx_first, 0.0, b_first), u[0:GB - 8]], axis=0)
        else:
            um1 = ue[0:GB]
        if r0 == TMF - GB:
            b_last = jnp.where(sub == 7, u_ref[TMF + 1:TMF + 2, lanes],
                               pltpu.roll(u_ref[0:8, lanes], 7, axis=0))
            up1 = jnp.concatenate([u[8:GB], jnp.where(ctx_last, 0.0, b_last)], axis=0)
        else:
            up1 = ue[r0 - lo + 8:r0 - lo + GB + 8]
        return um1 * cvs[0:1] + u * cvs[1:2] + up1 * cvs[2:3] + cvs[3:4]

    def gate(slot, c, ub):
        cva = cv_ref[0:4, c * FC:(c + 1) * FC]
        cvg = cv_ref[0:4, D_FF + c * FC:D_FF + (c + 1) * FC]
        for r0 in range(ub * RBUF, (ub + 1) * RBUF, GB):
            a = conv(u_scrs[slot], slice(0, FC), cva, r0)
            g = conv(u_scrs[slot], slice(FC, 2 * FC), cvg, r0)
            act_scr[c, r0:r0 + GB, :] = (_silu(a) * g).astype(BF16)

    def finish(rb):
        vrows = range(rb * RB // 8, (rb + 1) * RB // 8)
        lhs = jnp.concatenate([act_scr[c, rb * RB:(rb + 1) * RB, :] for c in range(NFC)], axis=1)
        y = jnp.dot(lhs, dn_ref[...].astype(BF16), preferred_element_type=F32)
        xr = jnp.concatenate([_seg_rows_f(xc_scr, k) for k in vrows], axis=0)
        out = _layer_norm(ALPHA * xr + m[5:6] * y, ln[1:2], ln[3:4])
        for kk, k in enumerate(vrows):
            for cb in range(LANE_BLOCKS):
                xc_scr[cb, pl.ds(k, 8, stride=PITCHF), :] = out[8 * kk:8 * kk + 8, cb * 128:(cb + 1) * 128]

    def write_out(side):
        for cb in range(LANE_BLOCKS):
            for s in range(8):
                o_refs[side][s * SEGF:(s + 1) * SEGF, cb * 128:(cb + 1) * 128] = xc_scr[
                    cb, s * PITCHF:s * PITCHF + SEGF, :]

    build_h(0)
    for c in range(NFC + 1):
        for ub in range(n_ub):
            if c == 0 and ub > 0:
                build_h(ub)
            if c < NFC:
                up_proj(c % 2, c, ub)
            if c >= 1:
                gate((c - 1) % 2, c - 1, ub)
            if c == NFC:
                finish(ub)

    _per_half(i, TMF, n_out, write_out)


def _ffn(xs, mod, up_chunks, conv8, down, ln8, l, split_out):
    halo_blocks = TMF // POOL_HALO
    last_halo = N_LAT // POOL_HALO - 1
    n_out = 2 if split_out else 1
    if split_out:
        out_shape = [jax.ShapeDtypeStruct((N_CTX, D_MODEL), F32), jax.ShapeDtypeStruct((N_LAT, D_MODEL), F32)]
    else:
        out_shape = [jax.ShapeDtypeStruct((ROWS, D_MODEL), F32)]
    return pl.pallas_call(
        functools.partial(_ffn_kernel, n_out=n_out),
        grid=(ROWS // TMF,),
        in_specs=[
            *_tile_specs(2, TMF, 1),
            pl.BlockSpec((POOL_HALO, D_MODEL),
                         lambda i: (jnp.maximum((i - CTXF_TILES) * halo_blocks - 1, 0), 0)),
            pl.BlockSpec((POOL_HALO, D_MODEL),
                         lambda i: (jnp.clip((i - CTXF_TILES + 1) * halo_blocks, 0, last_halo), 0)),
            pl.BlockSpec((None, None, 6, D_MODEL), lambda i: (l, _cond_of_tile(i, TMF), 0, 0)),
            pl.BlockSpec((None, NFC, D_MODEL, 2 * FC), lambda i: (l, 0, 0, 0), pipeline_mode=pl.Buffered(1)),
            pl.BlockSpec((None, 8, 2 * D_FF), lambda i: (l, 0, 0)),
            pl.BlockSpec((None, D_FF, D_MODEL), lambda i: (l, 0, 0), pipeline_mode=pl.Buffered(1)),
            pl.BlockSpec((None, 8, D_MODEL), lambda i: (l, 0, 0)),
        ],
        out_specs=_tile_specs(n_out, TMF, 1),
        out_shape=out_shape,
        scratch_shapes=[pltpu.VMEM((TMF + HALO_ROWS, D_MODEL), BF16),
                        pltpu.VMEM((NFC, TMF, FC), BF16),
                        pltpu.VMEM((LANE_BLOCKS, 8 * PITCHF, 128), F32),
                        pltpu.VMEM((TMF + HALO_ROWS, 2 * FC), F32),
                        pltpu.VMEM((TMF + HALO_ROWS, 2 * FC), F32)],
        compiler_params=pltpu.CompilerParams(
            dimension_semantics=("arbitrary",), vmem_limit_bytes=VMEM_LIMIT),
        name="convffn",
    )(xs[0], xs[1], xs[1], xs[1], mod, up_chunks, conv8, down, ln8)


def _head_masks(width):
    lane = lax.broadcasted_iota(jnp.int32, (1, width), 1)
    return [(lane >= h * HEAD_DIM) & (lane < (h + 1) * HEAD_DIM) for h in range(width // HEAD_DIM)]


def _stack_heads(x, masks):
    return jnp.concatenate([jnp.where(m, x, jnp.zeros_like(x)) for m in masks], axis=0)


def _rope(x, cos, sin):
    lane = lax.broadcasted_iota(jnp.int32, (1, 128), 1)
    lower = (lane & 31) < 16
    outs = []
    for k in range(x.shape[1] // 128):
        xb = x[:, k * 128:(k + 1) * 128]
        partner = jnp.where(lower, pltpu.roll(xb, 112, axis=1), pltpu.roll(xb, 16, axis=1))
        outs.append(xb * cos + partner * sin)
    return outs[0] if len(outs) == 1 else jnp.concatenate(outs, axis=1)


def _load_state(state_scr, blocks_ref, d):
    state_scr[...] = jnp.zeros((GW, GW), F32)
    for h in range(N_HEADS):
        sl = slice(h * HEAD_DIM, (h + 1) * HEAD_DIM)
        state_scr[sl, sl] = blocks_ref[d, h]


def _store_state(blocks_ref, d, state_scr):
    for h in range(N_HEADS):
        sl = slice(h * HEAD_DIM, (h + 1) * HEAD_DIM)
        blocks_ref[d, h] = state_scr[sl, sl]


LAT_GROUP = 4
CTX_SEQS = 2


def _chunk_off(c):
    return c * CHUNK if isinstance(c, int) else pl.multiple_of(c * CHUNK, CHUNK)


def _interleave(chains):
    chains = list(chains)
    while chains:
        for ch in list(chains):
            try:
                next(ch)
            except StopIteration:
                chains.remove(ch)


def _mixer_kernel(*refs, latent, group, ng, cast_up=False):
    if latent:
        (qa_ref, ka_ref, va_ref, kx_ref, vx_ref, ub_ref, vb_ref, qc_ref, kc_ref, vc_ref, gf_ref, gb_ref,
         pd_ref, pdp_ref, pdn_ref, cos_ref, sin_ref, s0_ref, dmat_ref, dec_ref, sink_ref, ws_ref, bias_ref,
         vec_ref, gmat_ref, cnt_ref, wpool_ref, xres_ref, mod_ref, wout_ref, ln_ref,
         x1_ref, sf_scr, sb_scr, ob_scr, pext_scr, mix_ref, k_scr, v_scr) = refs
    else:
        (qa_ref, ka_ref, va_ref, ub_ref, vb_ref, qc_ref, kc_ref, vc_ref, gf_ref, gb_ref, pd_ref,
         dmat_ref, dec_ref, sink_ref, ws_ref, bias_ref, vec_ref, gmat_ref, cnt_ref, wpool_ref,
         xres_ref, mod_ref, wout_ref, ln_ref) = refs[:24]
        if cast_up:
            upa_ref, upg_ref, x1_ref, st_ref, kn_ref, vn_ref, upo_ref = refs[24:31]
        else:
            x1_ref, st_ref, kn_ref, vn_ref = refs[24:28]
        sf_scr, sb_scr, ob_scr, pext_scr, mix_ref = refs[-5:]

    nc = group * ng
    cps = group if latent else SEQ // CHUNK
    g_a = mod_ref[...][2:3]
    ln = ln_ref[...]
    masks = _head_masks(GW)
    row = lax.broadcasted_iota(jnp.int32, (GW, GW), 0)
    col = lax.broadcasted_iota(jnp.int32, (GW, GW), 1)
    bd_mask = (row // HEAD_DIM) == (col // HEAD_DIM)
    vec = vec_ref[...]
    lane = lax.broadcasted_iota(jnp.int32, (1, 2 * HEAD_DIM), 1)
    lo = lane < HEAD_DIM

    def rows(k):
        return slice(k * CHUNK, (k + 1) * CHUNK)

    def retention(order, d, state_scr, out):
        dmat = dmat_ref[d]
        qdec, kdec = dec_ref[d], dec_ref[2 + d]
        cdec = vec[5 + d:6 + d]
        state = state_scr[...]
        for k in order:
            q, kk, v = qc_ref[rows(k), :], kc_ref[rows(k), :], vc_ref[rows(k), :]
            qb, kb, vb = q.astype(BF16), kk.astype(BF16), v.astype(BF16)
            s = lax.dot_general(_stack_heads(qb, masks), kb, (((1,), (1,)), ((), ())),
                                preferred_element_type=F32)
            yield
            p = (s * dmat).astype(BF16)
            p_cat = jnp.concatenate([p[h * CHUNK:(h + 1) * CHUNK] for h in range(N_HEADS)], axis=1)
            o = (jnp.dot(p_cat, _stack_heads(vb, masks), preferred_element_type=F32)
                 + _bdot(q * qdec, state))
            yield
            upd = lax.dot_general((kk * kdec).astype(BF16), vb, (((0,), (0,)), ((), ())),
                                  preferred_element_type=F32)
            state = state * cdec + jnp.where(bd_mask, upd, 0.0)
            out[k] = o
            yield
        state_scr[...] = state

    def stage_kv(k, c):
        dst = pl.ds(_chunk_off(c + 1), CHUNK)
        k_scr[dst, :] = _rope(ka_ref[rows(k), :], cos_ref[rows(k), :], sin_ref[rows(k), :]).astype(BF16)
        yield
        v_scr[dst, :] = va_ref[rows(k), :].astype(BF16)
        yield

    def attention(k, c, done):
        q = qa_ref[rows(k), :]
        if latent:
            q = _rope(q, cos_ref[rows(k), :], sin_ref[rows(k), :])
        q = q * (HEAD_DIM ** -0.5)
        q0, q1 = q[:, :128], q[:, 128:]
        zero = jnp.zeros_like(q0)
        q_st = jnp.concatenate([
            jnp.where(lo, q0, zero),
            jnp.where(lo, pltpu.roll(q0, HEAD_DIM, axis=1), zero),
            jnp.where(lo, zero, pltpu.roll(q1, HEAD_DIM, axis=1)),
            jnp.where(lo, zero, q1)], axis=0).astype(BF16)
        if latent:
            band = pl.ds(_chunk_off(c), 3 * CHUNK)
            k_all = jnp.concatenate([k_scr[band, :], kx_ref[...].astype(BF16)], axis=0)
            v_all = jnp.concatenate([v_scr[band, :], vx_ref[...].astype(BF16)], axis=0)
        else:
            own = slice(k // cps * SEQ, (k // cps + 1) * SEQ)
            k_all = ka_ref[own, :].astype(BF16)
            v_all = va_ref[own, :].astype(BF16)
        s = lax.dot_general(q_st, k_all, (((1,), (1,)), ((), ())), preferred_element_type=F32)
        yield
        if latent:
            nk = 3 * CHUNK + PAST_LEN
            qi = lax.broadcasted_iota(jnp.int32, (N_HEADS * CHUNK, nk), 0) & (CHUNK - 1)
            kj = lax.broadcasted_iota(jnp.int32, (N_HEADS * CHUNK, nk), 1)
            kpos = kj + (c - 1) * CHUNK
            valid = (kj >= 3 * CHUNK) | ((kj >= qi) & (kj <= qi + 2 * CHUNK)
                                         & (kpos >= 0) & (kpos < nc * CHUNK))
            s = jnp.where(valid, s, NEG_INF)
        sink = sink_ref[...][:, 0:1]
        mx = jnp.maximum(jnp.max(s, axis=-1, keepdims=True), sink)
        yield
        e = jnp.exp(s - mx)
        den = jnp.sum(e, axis=-1, keepdims=True) + jnp.exp(sink - mx)
        yield
        o = jnp.dot(e.astype(BF16), v_all, preferred_element_type=F32) / den
        yield
        mix_ref[rows(k), 0:128] = jnp.where(lo, o[0:CHUNK], pltpu.roll(o[CHUNK:2 * CHUNK], HEAD_DIM, axis=1))
        mix_ref[rows(k), 128:256] = jnp.where(lo, pltpu.roll(o[2 * CHUNK:3 * CHUNK], HEAD_DIM, axis=1),
                                              o[3 * CHUNK:4 * CHUNK])
        done.add(("A", k))
        yield

    def gating_unit(k, done):
        vn = _layer_norm(vb_ref[rows(k), :], vec[0:1], vec[1:2]).astype(BF16)
        yield
        sg = jnp.dot(ws_ref[...], _stack_heads(vn, masks), preferred_element_type=F32) + bias_ref[...]
        yield
        mix_ref[rows(k), GW:2 * GW] = ub_ref[rows(k), :] * sg
        done.add(("B", k))
        yield

    def retention_mix(k, o_fwd, o_bwd, done):
        while k not in o_fwd or k not in o_bwd:
            yield
        gmat = gmat_ref[...]
        normed = []
        for o in (o_fwd[k], o_bwd[k]):
            mu = jnp.dot(o.astype(BF16), gmat, preferred_element_type=F32)
            yield
            dlt = o - mu
            var = jnp.dot((dlt * dlt).astype(BF16), gmat, preferred_element_type=F32)
            yield
            normed.append(dlt * lax.rsqrt(var + LN_EPS))
        mix_ref[rows(k), 2 * GW:3 * GW] = (_silu(gf_ref[rows(k), :]) * (normed[0] * vec[3:4])
                                           + _silu(gb_ref[rows(k), :]) * (normed[1] * vec[4:5]))
        done.add(("C", k))
        yield

    def pooling(k, c, done):
        pd = pd_ref[rows(k), :]
        pext = pext_scr.at[k]
        zeros8 = jnp.zeros((POOL_HALO, GW), F32)
        if k % cps > 0:
            prev8 = pd_ref[k * CHUNK - POOL_HALO:k * CHUNK, :]
        else:
            prev8 = jnp.where(c > 0, pdp_ref[...], 0.0) if latent else zeros8
        if k % cps < cps - 1:
            next8 = pd_ref[(k + 1) * CHUNK:(k + 1) * CHUNK + POOL_HALO, :]
        else:
            next8 = jnp.where(c < nc - 1, pdn_ref[...], 0.0) if latent else zeros8
        pext[0:POOL_HALO, :] = prev8
        pext[POOL_HALO:POOL_HALO + CHUNK, :] = pd
        pext[POOL_HALO + CHUNK:2 * POOL_HALO + CHUNK, :] = next8
        yield

        def win(d, half):
            return pext[pl.ds(POOL_HALO + d, CHUNK), half * 128:(half + 1) * 128]

        a2 = win(-1, 0) + win(0, 0)
        a4 = a2 + win(-2, 0) + win(1, 0)
        yield
        a8 = win(-4, 1)
        for d in range(-3, 4):
            a8 = a8 + win(d, 1)
        yield
        a16 = a8
        for d in list(range(-8, -4)) + list(range(4, 8)):
            a16 = a16 + win(d, 1)
        yield
        sums = jnp.concatenate([jnp.where(lo, a2, a4), jnp.where(lo, a8, a16)], axis=1)
        yd = sums * cnt_ref[rows(k), :] - pd
        mix_ref[rows(k), 3 * GW:4 * GW] = _bdot(yd, wpool_ref[...]) * vec[2:3]
        done.add(("D", k))
        yield

    def out_projection(k, done):
        while not all((m, k) in done for m in "ABCD"):
            yield
        y = jnp.dot(mix_ref[rows(k), :].astype(BF16), wout_ref[...], preferred_element_type=F32)
        yield
        x1_ref[rows(k), :] = _layer_norm(ALPHA * xres_ref[rows(k), :] + g_a * y, ln[0:1], ln[2:3])
        yield

    def up_weight_rows():
        for lyr in range(DEPTH):
            for cc in range(NFC):
                upo_ref[lyr, cc, :, 0:FC] = upa_ref[lyr, :, cc * FC:(cc + 1) * FC].astype(BF16)
                upo_ref[lyr, cc, :, FC:2 * FC] = upg_ref[lyr, :, cc * FC:(cc + 1) * FC].astype(BF16)
                yield

    def forward_chains(chunk_of, o_bwd, fwd_states):
        done, o_fwd = set(), {}
        chains = [retention(ks, 0, state, o_fwd) for ks, state in fwd_states]
        for k in range(group):
            chains += [attention(k, chunk_of(k), done), gating_unit(k, done), pooling(k, chunk_of(k), done),
                       retention_mix(k, o_fwd, o_bwd, done), out_projection(k, done)]
        return chains

    if not latent:
        sb_scr[...] = jnp.zeros((CTX_SEQS, GW, GW), F32)
        sf_scr[...] = jnp.zeros((CTX_SEQS, GW, GW), F32)
        seq_chunks = [range(sq * cps, (sq + 1) * cps) for sq in range(CTX_SEQS)]
        o_bwd = {}
        side_chains = [retention(reversed(ks), 1, sb_scr.at[sq], o_bwd) for sq, ks in enumerate(seq_chunks)]
        if cast_up:
            side_chains.append(up_weight_rows())
        _interleave(side_chains + forward_chains(lambda k: k % cps, o_bwd,
                                                 [(ks, sf_scr.at[sq]) for sq, ks in enumerate(seq_chunks)]))
        for sq in range(CTX_SEQS):
            kn_ref[sq] = ka_ref[sq * SEQ:(sq + 1) * SEQ, :]
            vn_ref[sq] = va_ref[sq * SEQ:(sq + 1) * SEQ, :]
            _store_state(st_ref.at[sq], 1, sb_scr.at[sq])
            _store_state(st_ref.at[sq], 0, sf_scr.at[sq])
        return

    p = pl.program_id(1)
    g = pl.program_id(2)

    @pl.when(p == 0)
    def _():
        @pl.when(g == 0)
        def _():
            _load_state(sb_scr, s0_ref, 1)
            zero_blk = jnp.zeros((CHUNK, 2 * HEAD_DIM), BF16)
            for scr in (k_scr, v_scr):
                scr[0:CHUNK, :] = zero_blk
                scr[(nc + 1) * CHUNK:(nc + 2) * CHUNK, :] = zero_blk

        first = (ng - 1 - g) * group
        o_bwd = {}
        _interleave([retention(reversed(range(group)), 1, sb_scr, o_bwd)]
                    + [stage_kv(k, first + k) for k in range(group)])
        for k in range(group):
            ob_scr[pl.ds(_chunk_off(first + k), CHUNK), :] = o_bwd[k]

    @pl.when(p == 1)
    def _():
        @pl.when(g == 0)
        def _():
            _load_state(sf_scr, s0_ref, 0)

        first = g * group
        o_bwd = {k: ob_scr[pl.ds(_chunk_off(first + k), CHUNK), :] for k in range(group)}
        _interleave(forward_chains(lambda k: first + k, o_bwd, [(range(group), sf_scr)]))


def _mixer(z, x_res, mod, w_out_bf16, ln8, tabs, l, latent, extra=None, ffn_up=None):
    nb = DEC_BATCH if latent else BATCH // CTX_SEQS
    nc = (DEC_SEQ if latent else CTX_SEQS * SEQ) // CHUNK
    group = LAT_GROUP if latent else nc
    ng = nc // group
    blk = group * CHUNK
    base = (N_CTX // blk) if latent else 0
    per8 = blk // POOL_HALO
    last_halo = ROWS // POOL_HALO - 1

    def on_grid(f):
        return (lambda b, p, g: f(b, p, g)) if latent else (lambda b: f(b, 1, 0))

    def fwd(b, p, g):
        return base + b * ng + g * p

    def both(b, p, g):
        return base + b * ng + jnp.where(p == 0, ng - 1 - g, g)

    def bwd_only(b, p, g):
        return base + b * ng + (ng - 1 - g) * (1 - p)

    def col(width, idx, rowmap):
        return pl.BlockSpec((blk, width), on_grid(lambda b, p, g: (rowmap(b, p, g), idx)))

    def const(shape):
        return pl.BlockSpec(shape, on_grid(lambda b, p, g: (0,) * len(shape)))

    def layer(shape):
        return pl.BlockSpec((None,) + shape, on_grid(lambda b, p, g: (l,) + (0,) * len(shape)))

    specs, args = [], []

    def add(spec, arr):
        specs.append(spec)
        args.append(arr)

    add(col(GW, 0, fwd), z)
    add(col(128, 2, bwd_only if latent else fwd), z)
    add(col(128, 3, bwd_only if latent else fwd), z)
    if latent:
        add(pl.BlockSpec((None, None, PAST_LEN, 128), lambda b, p, g: (b, l, 0, 0)), extra["ck"])
        add(pl.BlockSpec((None, None, PAST_LEN, 128), lambda b, p, g: (b, l, 0, 0)), extra["cv"])
    add(col(GW, 2, fwd), z)
    add(col(GW, 3, fwd), z)
    add(col(GW, 4, both), z)
    add(col(GW, 5, both), z)
    add(col(GW, 6, both), z)
    add(col(GW, 7, fwd), z)
    add(col(GW, 8, fwd), z)
    add(col(GW, 9, fwd), z)
    if latent:
        add(pl.BlockSpec((POOL_HALO, GW),
                         lambda b, p, g: (jnp.maximum(fwd(b, p, g) * per8 - 1, 0), 9)), z)
        add(pl.BlockSpec((POOL_HALO, GW),
                         lambda b, p, g: (jnp.minimum((fwd(b, p, g) + 1) * per8, last_halo), 9)), z)
        rope_map = lambda b, p, g: (jnp.where(p == 0, ng - 1 - g, g), 0)
        add(pl.BlockSpec((blk, 128), rope_map), extra["cos"])
        add(pl.BlockSpec((blk, 128), rope_map), extra["sin"])
        add(pl.BlockSpec((None, None, 2, N_HEADS, HEAD_DIM, HEAD_DIM),
                         lambda b, p, g: (b, l, 0, 0, 0, 0)), extra["s0"])
    add(layer((2, N_HEADS * CHUNK, CHUNK)), tabs["dmat"])
    add(layer((4, CHUNK, GW)), tabs["dec"])
    add(layer((N_HEADS * CHUNK, 128)), tabs["sink"])
    add(layer((CHUNK, N_HEADS * CHUNK)), tabs["ws"])
    add(layer((CHUNK, GW)), tabs["bias"])
    add(layer((8, GW)), tabs["vec"])
    add(const((GW, GW)), tabs["gmat"])
    add(pl.BlockSpec((blk, GW), on_grid(lambda b, p, g: (g * p, 0))),
        tabs["cnt_lat"] if latent else tabs["cnt_ctx"])
    add(layer((GW, GW)), tabs["wpool"])
    local = lambda b, p, g: (b * ng + g * p, 0)
    add(pl.BlockSpec((blk, D_MODEL), on_grid(local)), x_res)
    add(pl.BlockSpec((None, None, 6, D_MODEL),
                     on_grid(lambda b, p, g: (l, (1 + b) if latent else 0, 0, 0))), mod)
    add(layer((D_MODEL, D_MODEL)), w_out_bf16)
    add(layer((8, D_MODEL)), ln8)
    up_rows = D_MODEL // nb
    if ffn_up is not None:
        add(pl.BlockSpec((DEPTH, up_rows, D_FF), lambda b: (0, b, 0)), ffn_up)
        add(pl.BlockSpec((DEPTH, up_rows, D_FF), lambda b: (0, b, 1)), ffn_up)

    out_shape = [jax.ShapeDtypeStruct((nb * nc * CHUNK, D_MODEL), F32)]
    out_specs = [pl.BlockSpec((blk, D_MODEL), on_grid(local))]
    state_shape = (GW, GW) if latent else (CTX_SEQS, GW, GW)
    scratch = [pltpu.VMEM(state_shape, F32), pltpu.VMEM(state_shape, F32),
               pltpu.VMEM((nc * CHUNK, GW), F32),
               pltpu.VMEM((group, CHUNK + 2 * POOL_HALO, GW), F32),
               pltpu.VMEM((blk, D_MODEL), F32)]
    if latent:
        scratch += [pltpu.VMEM(((nc + 2) * CHUNK, 128), BF16), pltpu.VMEM(((nc + 2) * CHUNK, 128), BF16)]
    else:
        out_shape.append(jax.ShapeDtypeStruct((BATCH, 2, N_HEADS, HEAD_DIM, HEAD_DIM), F32))
        out_specs.append(pl.BlockSpec((CTX_SEQS, 2, N_HEADS, HEAD_DIM, HEAD_DIM), lambda b: (b, 0, 0, 0, 0)))
        for _ in range(2):
            out_shape.append(jax.ShapeDtypeStruct((BATCH, SEQ, 128), F32))
            out_specs.append(pl.BlockSpec((CTX_SEQS, SEQ, 128), lambda b: (b, 0, 0)))
        if ffn_up is not None:
            out_shape.append(jax.ShapeDtypeStruct((DEPTH, NFC, D_MODEL, 2 * FC), BF16))
            out_specs.append(pl.BlockSpec((DEPTH, NFC, up_rows, 2 * FC), lambda b: (0, 0, b, 0)))

    return pl.pallas_call(
        functools.partial(_mixer_kernel, latent=latent, group=group, ng=ng, cast_up=ffn_up is not None),
        grid=(nb, 2, ng) if latent else (nb,),
        in_specs=specs,
        out_specs=out_specs,
        out_shape=out_shape,
        scratch_shapes=scratch,
        compiler_params=pltpu.CompilerParams(
            dimension_semantics=("arbitrary",) * (3 if latent else 1), vmem_limit_bytes=VMEM_LIMIT),
        name="mixer_latent" if latent else "mixer_context",
    )(*args)


def _pad_rows(rows, n=8):
    a = jnp.stack(rows)
    return jnp.concatenate([a, jnp.zeros((n - a.shape[0],) + a.shape[1:], a.dtype)], axis=0)


def _block_diag(blocks):
    g, n, _ = blocks.shape
    eye = jnp.eye(g, dtype=blocks.dtype)
    return (eye[:, None, :, None] * blocks[:, :, None, :]).reshape(g * n, g * n)


def _inv_count(n, seqs=1):
    t = np.arange(n)
    cols = []
    for w in POOL_WINDOWS:
        cnt = np.clip(t + w // 2, 0, n) - np.clip(t - w // 2, 0, n)
        cols.append(np.repeat((1.0 / cnt)[:, None], HEAD_DIM, axis=1))
    return jnp.asarray(np.tile(np.concatenate(cols, axis=1), (seqs, 1)), F32)


def _rope_tables():
    rows = DEC_SEQ // GRID_W
    r, cc = jnp.meshgrid(jnp.arange(rows), jnp.arange(GRID_W), indexing="ij")
    half = HEAD_DIM // 2
    freqs = ROPE_BASE ** (-jnp.arange(0, half, 2, dtype=F32) / half)

    def tables(pos):
        ang = pos.reshape(-1).astype(F32)[:, None] * freqs[None, :]
        cos, sin = jnp.cos(ang), jnp.sin(ang)
        return jnp.concatenate([cos, cos], axis=1), jnp.concatenate([-sin, sin], axis=1)

    cr, sr = tables(r)
    ccol, scol = tables(cc)
    cos = jnp.concatenate([cr, ccol], axis=1)
    sin = jnp.concatenate([sr, scol], axis=1)
    return jnp.tile(cos, (1, 2)), jnp.tile(sin, (1, 2))


def _layer_tables(attn_sink, sgu_norm_w, sgu_norm_b, sgu_ws, sgu_bs, ret_decay, ret_gn_w, pool_w, pool_scale):
    log_g = jax.nn.log_sigmoid(ret_decay.astype(F32))
    i = jnp.arange(CHUNK, dtype=F32)
    rel = i[:, None] - i[None, :]
    kscale = HEAD_DIM ** -0.5
    d_f = jnp.where(rel >= 0, jnp.exp(jnp.maximum(rel, 0.0)[None] * log_g[0][:, None, None]), 0.0)
    d_b = jnp.where(rel <= 0, jnp.exp(jnp.maximum(-rel, 0.0)[None] * log_g[1][:, None, None]), 0.0)
    dmat = jnp.stack([d_f.reshape(N_HEADS * CHUNK, CHUNK), d_b.reshape(N_HEADS * CHUNK, CHUNK)]) * kscale

    def lanes(per_head):
        return jnp.repeat(per_head, HEAD_DIM, axis=1)

    qdec_f = lanes(jnp.exp((i + 1.0)[:, None] * log_g[0][None, :]))
    qdec_b = lanes(jnp.exp((CHUNK - i)[:, None] * log_g[1][None, :]))
    kdec_f = lanes(jnp.exp((CHUNK - 1.0 - i)[:, None] * log_g[0][None, :])) * kscale
    kdec_b = lanes(jnp.exp(i[:, None] * log_g[1][None, :])) * kscale
    cdec = jnp.repeat(jnp.exp(CHUNK * log_g), HEAD_DIM, axis=1)
    vec = _pad_rows([sgu_norm_w, sgu_norm_b, pool_scale, ret_gn_w[0], ret_gn_w[1], cdec[0], cdec[1]])
    return {
        "dmat": dmat,
        "dec": jnp.stack([qdec_f, qdec_b, kdec_f, kdec_b]),
        "sink": jnp.broadcast_to(jnp.repeat(attn_sink, CHUNK)[:, None], (N_HEADS * CHUNK, 128)),
        "ws": jnp.concatenate([sgu_ws[h] for h in range(N_HEADS)], axis=1).astype(BF16),
        "bias": jnp.repeat(sgu_bs.T, HEAD_DIM, axis=1),
        "vec": vec,
        "wpool": _block_diag(pool_w).astype(BF16),
    }


def kernel(x_prompt, x_sample, cache_attn_k, cache_attn_v, state_ret, c, c_ctx, w_ada, b_ada, w_in,
           w_out, attn_sink, sgu_norm_w, sgu_norm_b, sgu_ws, sgu_bs, ret_decay, ret_gn_w, pool_w,
           pool_scale, ffn_up, ffn_conv_w, ffn_conv_b, ffn_down, ln_w, ln_b):
    cond8 = jnp.concatenate([c_ctx[None], c, jnp.zeros((8 - 1 - DEC_BATCH, D_MODEL), F32)], axis=0)
    mod = _modulation(cond8, w_ada, b_ada).reshape(DEPTH, 8, 6, D_MODEL)

    tabs = jax.vmap(_layer_tables)(attn_sink, sgu_norm_w, sgu_norm_b, sgu_ws, sgu_bs, ret_decay, ret_gn_w,
                                   pool_w, pool_scale)
    tabs["gmat"] = _block_diag(jnp.full((N_HEADS, HEAD_DIM, HEAD_DIM), 1.0 / HEAD_DIM, F32)).astype(BF16)
    tabs["cnt_ctx"] = _inv_count(SEQ, CTX_SEQS)
    tabs["cnt_lat"] = _inv_count(DEC_SEQ)
    cos, sin = _rope_tables()
    extra = {"ck": cache_attn_k.reshape(DEC_BATCH, DEPTH, PAST_LEN, 128),
             "cv": cache_attn_v.reshape(DEC_BATCH, DEPTH, PAST_LEN, 128),
             "cos": cos, "sin": sin, "s0": state_ret}
    ln8 = jnp.concatenate([ln_w, ln_b, jnp.zeros((DEPTH, 4, D_MODEL), F32)], axis=1)
    conv8 = jnp.concatenate([ffn_conv_w, ffn_conv_b[:, None], jnp.zeros((DEPTH, 4, 2 * D_FF), F32)], axis=1)

    w_out_bf16 = _to_bf16(w_out, D_MODEL)

    xs = [x_prompt.reshape(N_CTX, D_MODEL), x_sample.reshape(N_LAT, D_MODEL)]
    new_k, new_v, new_s = [], [], []
    for l in range(DEPTH):
        z = _inproj(xs, mod, w_in, l)
        if l == 0:
            x1_ctx, st, kn, vn, up_chunks = _mixer(z, xs[0], mod, w_out_bf16, ln8, tabs, l, latent=False,
                                                   ffn_up=ffn_up)
        else:
            x1_ctx, st, kn, vn = _mixer(z, xs[0], mod, w_out_bf16, ln8, tabs, l, latent=False)
        (x1_lat,) = _mixer(z, xs[1], mod, w_out_bf16, ln8, tabs, l, latent=True, extra=extra)
        xs = _ffn([x1_ctx, x1_lat], mod, up_chunks, conv8, ffn_down, ln8, l, split_out=True)
        new_k.append(kn.reshape(BATCH, SEQ, 2, HEAD_DIM))
        new_v.append(vn.reshape(BATCH, SEQ, 2, HEAD_DIM))
        new_s.append(st)

    y_prompt = xs[0].reshape(BATCH, SEQ, D_MODEL)
    y_sample = xs[1].reshape(DEC_BATCH, DEC_SEQ, D_MODEL)
    return (y_prompt, y_sample, jnp.stack(new_k, axis=1), jnp.stack(new_v, axis=1),
            jnp.stack(new_s, axis=1))
```

```python
import functools

import numpy as np
import jax
import jax.numpy as jnp
from jax import lax
from jax.experimental import pallas as pl
from jax.experimental.pallas import tpu as pltpu

F32 = jnp.float32
BF16 = jnp.bfloat16

D_MODEL = 1024
BATCH = 16
SEQ = 256
DEPTH = 2
DEC_BATCH = 2
DEC_SEQ = 2048
PAST_LEN = 256
GRID_W = 64
CHUNK = 128
HEAD_DIM = 64
GW = D_MODEL // 4
N_HEADS = 4
POOL_WINDOWS = (2, 4, 8, 16)
POOL_HALO = 8
D_FF = 2816
ROPE_BASE = 10000.0
LN_EPS = 1e-5
NEG_INF = -1e30
IN_WIDTH = 10 * GW
ALPHA = (2.0 * DEPTH) ** 0.25

N_CTX = BATCH * SEQ
N_LAT = DEC_BATCH * DEC_SEQ
ROWS = N_CTX + N_LAT

TM = 1024
NB_IN = 512
FC = 256
NB_ADA = 1536
VMEM_LIMIT = 56 * 1024 * 1024


def _cond_of_tile(i, tm=TM):
    ctx_tiles = N_CTX // tm
    return jnp.where(i < ctx_tiles, 0, 1 + (i - ctx_tiles) // (DEC_SEQ // tm))


def _tile_specs(tm):
    ctx_tiles = N_CTX // tm
    return [pl.BlockSpec((tm, D_MODEL), lambda i: (jnp.minimum(i, ctx_tiles - 1), 0)),
            pl.BlockSpec((tm, D_MODEL), lambda i: (jnp.maximum(i - ctx_tiles, 0), 0))]


def _per_half(i, tm, fn):
    ctx_tiles = N_CTX // tm

    @pl.when(i < ctx_tiles)
    def _():
        fn(0)

    @pl.when(i >= ctx_tiles)
    def _():
        fn(1)


def _layer_norm(x, w, b):
    mu = jnp.mean(x, axis=-1, keepdims=True)
    d = x - mu
    var = jnp.mean(d * d, axis=-1, keepdims=True)
    return d * lax.rsqrt(var + LN_EPS) * w + b


def _silu(x):
    return x * jax.nn.sigmoid(x)


def _bdot(a, b):
    return jnp.dot(a.astype(BF16), b.astype(BF16), preferred_element_type=F32)


def _mod_kernel(c_ref, w_ref, b_ref, o_ref):
    o_ref[...] = _bdot(_silu(c_ref[...]), w_ref[...]) + b_ref[...]


def _modulation(cond8, w_ada, b_ada):
    return pl.pallas_call(
        _mod_kernel,
        grid=(DEPTH, 6 * D_MODEL // NB_ADA),
        in_specs=[
            pl.BlockSpec((8, D_MODEL), lambda l, j: (0, 0)),
            pl.BlockSpec((None, D_MODEL, NB_ADA), lambda l, j: (l, 0, j)),
            pl.BlockSpec((None, 1, NB_ADA), lambda l, j: (l, 0, j)),
        ],
        out_specs=pl.BlockSpec((None, 8, NB_ADA), lambda l, j: (l, 0, j)),
        out_shape=jax.ShapeDtypeStruct((DEPTH, 8, 6 * D_MODEL), F32),
        compiler_params=pltpu.CompilerParams(
            dimension_semantics=("arbitrary", "arbitrary"), vmem_limit_bytes=VMEM_LIMIT),
        name="modulation",
    )(cond8, w_ada, b_ada.reshape(DEPTH, 1, 6 * D_MODEL))


def _cast_kernel(w_ref, o_ref):
    o_ref[...] = w_ref[...].astype(BF16)


def _to_bf16(w, block_rows):
    depth, rows, cols = w.shape
    return pl.pallas_call(
        _cast_kernel,
        grid=(depth, rows // block_rows),
        in_specs=[pl.BlockSpec((None, block_rows, cols), lambda l, r: (l, r, 0))],
        out_specs=pl.BlockSpec((None, block_rows, cols), lambda l, r: (l, r, 0)),
        out_shape=jax.ShapeDtypeStruct(w.shape, BF16),
        compiler_params=pltpu.CompilerParams(dimension_semantics=("arbitrary", "arbitrary")),
        name="cast_bf16",
    )(w)


def _inproj_kernel(xc_ref, xl_ref, mod_ref, w_ref, z_ref, h_scr):
    x_refs = (xc_ref, xl_ref)
    m = mod_ref[...]

    def project(side):
        for rh in range(2):
            rows = slice(rh * TM // 2, (rh + 1) * TM // 2)
            h_scr[rows, :] = (x_refs[side][rows, :] * (1.0 + m[1:2]) + m[0:1]).astype(BF16)
            for jb in range(IN_WIDTH // NB_IN):
                cols = slice(jb * NB_IN, (jb + 1) * NB_IN)
                z_ref[rows, cols] = jnp.dot(h_scr[rows, :], w_ref[:, cols].astype(BF16),
                                            preferred_element_type=F32)

    _per_half(pl.program_id(0), TM, project)


def _inproj(xs, mod, w_in, l):
    return pl.pallas_call(
        _inproj_kernel,
        grid=(ROWS // TM,),
        in_specs=_tile_specs(TM) + [
            pl.BlockSpec((None, None, 6, D_MODEL), lambda i: (l, _cond_of_tile(i), 0, 0)),
            pl.BlockSpec((None, D_MODEL, IN_WIDTH), lambda i: (l, 0, 0), pipeline_mode=pl.Buffered(1)),
        ],
        out_specs=pl.BlockSpec((TM, IN_WIDTH), lambda i: (i, 0)),
        out_shape=jax.ShapeDtypeStruct((ROWS, IN_WIDTH), F32),
        scratch_shapes=[pltpu.VMEM((TM, D_MODEL), BF16)],
        compiler_params=pltpu.CompilerParams(
            dimension_semantics=("arbitrary",), vmem_limit_bytes=VMEM_LIMIT),
        name="inproj",
    )(*xs, mod, w_in)


TMF = 512
SEGF = TMF // 8
PITCHF = SEGF + 8
CTXF_TILES = N_CTX // TMF
LATF_PER_SEQ = DEC_SEQ // TMF
HALO_ROWS = 16
RBUF = 256
RB = 256
GB = 64
NFC = D_FF // FC
LANE_BLOCKS = D_MODEL // 128
assert RB == RBUF


def _seg_rows_f(xc_ref, k):
    return jnp.concatenate([xc_ref[cb, pl.ds(k, 8, stride=PITCHF), :] for cb in range(LANE_BLOCKS)], axis=1)


def _ffn_kernel(xc_ref, xl_ref, xp_ref, xn_ref, mod_ref, up_ref, cv_ref, dn_ref, ln_ref, oc_ref, ol_ref,
                h_scr, act_scr, xc_scr, u0_scr, u1_scr):
    x_refs = (xc_ref, xl_ref)
    o_refs = (oc_ref, ol_ref)
    u_scrs = (u0_scr, u1_scr)
    i = pl.program_id(0)
    is_ctx = i < CTXF_TILES
    lat_pos = (i - CTXF_TILES) % LATF_PER_SEQ
    m = mod_ref[...]
    ln = ln_ref[...]

    scale = 1.0 + m[4:5]
    shift = m[3:4]

    def stage(side):
        for cb in range(LANE_BLOCKS):
            for s in range(8):
                xc_scr[cb, s * PITCHF:s * PITCHF + SEGF, :] = x_refs[side][s * SEGF:(s + 1) * SEGF,
                                                                            cb * 128:(cb + 1) * 128]

    _per_half(i, TMF, stage)

    def build_h(ub):
        for k in range(ub * RBUF // 8, (ub + 1) * RBUF // 8, 2):
            rows = jnp.concatenate([_seg_rows_f(xc_scr, k), _seg_rows_f(xc_scr, k + 1)], axis=0)
            h_scr[8 * k:8 * k + 16, :] = (rows * scale + shift).astype(BF16)

    sub16 = lax.broadcasted_iota(jnp.int32, (HALO_ROWS, D_MODEL), 0)
    prev_ok = jnp.logical_not(is_ctx) & (lat_pos > 0)
    next_ok = jnp.logical_not(is_ctx) & (lat_pos < LATF_PER_SEQ - 1)
    halo_x = jnp.where(sub16 == 0, xp_ref[POOL_HALO - 1:POOL_HALO, :], xn_ref[0:1, :])
    keep = ((sub16 == 0) & prev_ok) | ((sub16 == 1) & next_ok)
    h_scr[TMF:TMF + HALO_ROWS, :] = jnp.where(keep, halo_x * scale + shift, 0.0).astype(BF16)

    n_ub = TMF // RBUF
    sub = lax.broadcasted_iota(jnp.int32, (8, FC), 0)
    seg_per_seq = SEQ // SEGF
    ctx_first = is_ctx & (sub % seg_per_seq == 0)
    ctx_last = is_ctx & (sub % seg_per_seq == seg_per_seq - 1)

    def up_proj(slot, c, ub):
        rows = slice(ub * RBUF, (ub + 1) * RBUF + (HALO_ROWS if ub == n_ub - 1 else 0))
        u_scrs[slot][rows, :] = jnp.dot(h_scr[rows, :], up_ref[c], preferred_element_type=F32)

    def conv(u_ref, lanes, cvs, r0):
        lo = max(r0 - 8, 0)
        hi = min(r0 + GB + 8, TMF)
        ue = u_ref[lo:hi, lanes]
        u = ue[r0 - lo:r0 - lo + GB]
        if r0 == 0:
            b_first = jnp.where(sub == 0, u_ref[TMF:TMF + 1, lanes],
                                pltpu.roll(u_ref[TMF - 8:TMF, lanes], 1, axis=0))
            um1 = jnp.concatenate([jnp.where(ctx_first, 0.0, b_first), u[0:GB - 8]], axis=0)
        else:
            um1 = ue[0:GB]
        if r0 == TMF - GB:
            b_last = jnp.where(sub == 7, u_ref[TMF + 1:TMF + 2, lanes],
                               pltpu.roll(u_ref[0:8, lanes], 7, axis=0))
            up1 = jnp.concatenate([u[8:GB], jnp.where(ctx_last, 0.0, b_last)], axis=0)
        else:
            up1 = ue[r0 - lo + 8:r0 - lo + GB + 8]
        return um1 * cvs[0:1] + u * cvs[1:2] + up1 * cvs[2:3] + cvs[3:4]

    def gate(slot, c, ub):
        cva = cv_ref[0:4, c * FC:(c + 1) * FC]
        cvg = cv_ref[0:4, D_FF + c * FC:D_FF + (c + 1) * FC]
        for r0 in range(ub * RBUF, (ub + 1) * RBUF, GB):
            a = conv(u_scrs[slot], slice(0, FC), cva, r0)
            g = conv(u_scrs[slot], slice(FC, 2 * FC), cvg, r0)
            act_scr[c, r0:r0 + GB, :] = (_silu(a) * g).astype(BF16)

    def finish(rb):
        vrows = range(rb * RB // 8, (rb + 1) * RB // 8)
        lhs = jnp.concatenate([act_scr[c, rb * RB:(rb + 1) * RB, :] for c in range(NFC)], axis=1)
        y = jnp.dot(lhs, dn_ref[...].astype(BF16), preferred_element_type=F32)
        xr = jnp.concatenate([_seg_rows_f(xc_scr, k) for k in vrows], axis=0)
        out = _layer_norm(ALPHA * xr + m[5:6] * y, ln[1:2], ln[3:4])
        for kk, k in enumerate(vrows):
            for cb in range(LANE_BLOCKS):
                xc_scr[cb, pl.ds(k, 8, stride=PITCHF), :] = out[8 * kk:8 * kk + 8, cb * 128:(cb + 1) * 128]

    def write_out(side):
        for cb in range(LANE_BLOCKS):
            for s in range(8):
                o_refs[side][s * SEGF:(s + 1) * SEGF, cb * 128:(cb + 1) * 128] = xc_scr[
                    cb, s * PITCHF:s * PITCHF + SEGF, :]

    build_h(0)
    for c in range(NFC + 1):
        for ub in range(n_ub):
            if c == 0 and ub > 0:
                build_h(ub)
            if c < NFC:
                up_proj(c % 2, c, ub)
            if c >= 1:
                gate((c - 1) % 2, c - 1, ub)
            if c == NFC:
                finish(ub)

    _per_half(i, TMF, write_out)


def _ffn(xs, mod, up_chunks, conv8, down, ln8, l):
    halo_blocks = TMF // POOL_HALO
    last_halo = N_LAT // POOL_HALO - 1
    return pl.pallas_call(
        _ffn_kernel,
        grid=(ROWS // TMF,),
        in_specs=[
            *_tile_specs(TMF),
            pl.BlockSpec((POOL_HALO, D_MODEL),
                         lambda i: (jnp.maximum((i - CTXF_TILES) * halo_blocks - 1, 0), 0)),
            pl.BlockSpec((POOL_HALO, D_MODEL),
                         lambda i: (jnp.clip((i - CTXF_TILES + 1) * halo_blocks, 0, last_halo), 0)),
            pl.BlockSpec((None, None, 6, D_MODEL), lambda i: (l, _cond_of_tile(i, TMF), 0, 0)),
            pl.BlockSpec((None, NFC, D_MODEL, 2 * FC), lambda i: (l, 0, 0, 0), pipeline_mode=pl.Buffered(1)),
            pl.BlockSpec((None, 8, 2 * D_FF), lambda i: (l, 0, 0)),
            pl.BlockSpec((None, D_FF, D_MODEL), lambda i: (l, 0, 0), pipeline_mode=pl.Buffered(1)),
            pl.BlockSpec((None, 8, D_MODEL), lambda i: (l, 0, 0)),
        ],
        out_specs=_tile_specs(TMF),
        out_shape=[jax.ShapeDtypeStruct((N_CTX, D_MODEL), F32), jax.ShapeDtypeStruct((N_LAT, D_MODEL), F32)],
        scratch_shapes=[pltpu.VMEM((TMF + HALO_ROWS, D_MODEL), BF16),
                        pltpu.VMEM((NFC, TMF, FC), BF16),
                        pltpu.VMEM((LANE_BLOCKS, 8 * PITCHF, 128), F32),
                        pltpu.VMEM((TMF + HALO_ROWS, 2 * FC), F32),
                        pltpu.VMEM((TMF + HALO_ROWS, 2 * FC), F32)],
        compiler_params=pltpu.CompilerParams(
            dimension_semantics=("arbitrary",), vmem_limit_bytes=VMEM_LIMIT),
        name="convffn",
    )(xs[0], xs[1], xs[1], xs[1], mod, up_chunks, conv8, down, ln8)


def _head_masks(width):
    lane = lax.broadcasted_iota(jnp.int32, (1, width), 1)
    return [(lane >= h * HEAD_DIM) & (lane < (h + 1) * HEAD_DIM) for h in range(width // HEAD_DIM)]


def _stack_heads(x, masks):
    return jnp.concatenate([jnp.where(m, x, jnp.zeros_like(x)) for m in masks], axis=0)


def _rope(x, cos, sin):
    lane = lax.broadcasted_iota(jnp.int32, (1, 128), 1)
    lower = (lane & 31) < 16
    outs = []
    for k in range(x.shape[1] // 128):
        xb = x[:, k * 128:(k + 1) * 128]
        partner = jnp.where(lower, pltpu.roll(xb, 112, axis=1), pltpu.roll(xb, 16, axis=1))
        outs.append(xb * cos + partner * sin)
    return outs[0] if len(outs) == 1 else jnp.concatenate(outs, axis=1)


def _load_state(state_scr, blocks_ref, d):
    state_scr[...] = jnp.zeros((GW, GW), F32)
    for h in range(N_HEADS):
        sl = slice(h * HEAD_DIM, (h + 1) * HEAD_DIM)
        state_scr[sl, sl] = blocks_ref[d, h]


def _store_state(blocks_ref, d, state_scr):
    for h in range(N_HEADS):
        sl = slice(h * HEAD_DIM, (h + 1) * HEAD_DIM)
        blocks_ref[d, h] = state_scr[sl, sl]


LAT_GROUP = 4
CTX_SEQS = 2


def _chunk_off(c):
    return c * CHUNK if isinstance(c, int) else pl.multiple_of(c * CHUNK, CHUNK)


def _interleave(chains):
    chains = list(chains)
    while chains:
        for ch in list(chains):
            try:
                next(ch)
            except StopIteration:
                chains.remove(ch)


def _mixer_kernel(*refs, latent, group, ng, cast_up=False):
    if latent:
        (qa_ref, ka_ref, va_ref, kx_ref, vx_ref, ub_ref, vb_ref, qc_ref, kc_ref, vc_ref, gf_ref, gb_ref,
         pd_ref, pdp_ref, pdn_ref, cos_ref, sin_ref, s0_ref, dmat_ref, dec_ref, sink_ref, ws_ref, bias_ref,
         vec_ref, gmat_ref, cnt_ref, wpool_ref, xres_ref, mod_ref, wout_ref, ln_ref,
         x1_ref, sf_scr, sb_scr, ob_scr, pext_scr, mix_ref, k_scr, v_scr) = refs
    else:
        (qa_ref, ka_ref, va_ref, ub_ref, vb_ref, qc_ref, kc_ref, vc_ref, gf_ref, gb_ref, pd_ref,
         dmat_ref, dec_ref, sink_ref, ws_ref, bias_ref, vec_ref, gmat_ref, cnt_ref, wpool_ref,
         xres_ref, mod_ref, wout_ref, ln_ref) = refs[:24]
        if cast_up:
            upa_ref, upg_ref, x1_ref, st_ref, kn_ref, vn_ref, upo_ref = refs[24:31]
        else:
            x1_ref, st_ref, kn_ref, vn_ref = refs[24:28]
        sf_scr, sb_scr, ob_scr, pext_scr, mix_ref = refs[-5:]

    nc = group * ng
    cps = group if latent else SEQ // CHUNK
    g_a = mod_ref[...][2:3]
    ln = ln_ref[...]
    masks = _head_masks(GW)
    row = lax.broadcasted_iota(jnp.int32, (GW, GW), 0)
    col = lax.broadcasted_iota(jnp.int32, (GW, GW), 1)
    bd_mask = (row // HEAD_DIM) == (col // HEAD_DIM)
    vec = vec_ref[...]
    lane = lax.broadcasted_iota(jnp.int32, (1, 2 * HEAD_DIM), 1)
    lo = lane < HEAD_DIM

    def rows(k):
        return slice(k * CHUNK, (k + 1) * CHUNK)

    def retention(order, d, state_scr, out):
        dmat = dmat_ref[d]
        qdec, kdec = dec_ref[d], dec_ref[2 + d]
        cdec = vec[5 + d:6 + d]
        state = state_scr[...]
        for k in order:
            q, kk, v = qc_ref[rows(k), :], kc_ref[rows(k), :], vc_ref[rows(k), :]
            qb, kb, vb = q.astype(BF16), kk.astype(BF16), v.astype(BF16)
            s = lax.dot_general(_stack_heads(qb, masks), kb, (((1,), (1,)), ((), ())),
                                preferred_element_type=F32)
            yield
            p = (s * dmat).astype(BF16)
            p_cat = jnp.concatenate([p[h * CHUNK:(h + 1) * CHUNK] for h in range(N_HEADS)], axis=1)
            o = (jnp.dot(p_cat, _stack_heads(vb, masks), preferred_element_type=F32)
                 + _bdot(q * qdec, state))
            yield
            upd = lax.dot_general((kk * kdec).astype(BF16), vb, (((0,), (0,)), ((), ())),
                                  preferred_element_type=F32)
            state = state * cdec + jnp.where(bd_mask, upd, 0.0)
            out[k] = o
            yield
        state_scr[...] = state

    def stage_kv(k, c):
        dst = pl.ds(_chunk_off(c + 1), CHUNK)
        k_scr[dst, :] = _rope(ka_ref[rows(k), :], cos_ref[rows(k), :], sin_ref[rows(k), :]).astype(BF16)
        yield
        v_scr[dst, :] = va_ref[rows(k), :].astype(BF16)
        yield

    def attention(k, c, done):
        q = qa_ref[rows(k), :]
        if latent:
            q = _rope(q, cos_ref[rows(k), :], sin_ref[rows(k), :])
        q = q * (HEAD_DIM ** -0.5)
        q0, q1 = q[:, :128], q[:, 128:]
        zero = jnp.zeros_like(q0)
        q_st = jnp.concatenate([
            jnp.where(lo, q0, zero),
            jnp.where(lo, pltpu.roll(q0, HEAD_DIM, axis=1), zero),
            jnp.where(lo, zero, pltpu.roll(q1, HEAD_DIM, axis=1)),
            jnp.where(lo, zero, q1)], axis=0).astype(BF16)
        if latent:
            band = pl.ds(_chunk_off(c), 3 * CHUNK)
            k_all = jnp.concatenate([k_scr[band, :], kx_ref[...].astype(BF16)], axis=0)
            v_all = jnp.concatenate([v_scr[band, :], vx_ref[...].astype(BF16)], axis=0)
        else:
            own = slice(k // cps * SEQ, (k // cps + 1) * SEQ)
            k_all = ka_ref[own, :].astype(BF16)
            v_all = va_ref[own, :].astype(BF16)
        s = lax.dot_general(q_st, k_all, (((1,), (1,)), ((), ())), preferred_element_type=F32)
        yield
        if latent:
            nk = 3 * CHUNK + PAST_LEN
            qi = lax.broadcasted_iota(jnp.int32, (N_HEADS * CHUNK, nk), 0) & (CHUNK - 1)
            kj = lax.broadcasted_iota(jnp.int32, (N_HEADS * CHUNK, nk), 1)
            kpos = kj + (c - 1) * CHUNK
            valid = (kj >= 3 * CHUNK) | ((kj >= qi) & (kj <= qi + 2 * CHUNK)
                                         & (kpos >= 0) & (kpos < nc * CHUNK))
            s = jnp.where(valid, s, NEG_INF)
        sink = sink_ref[...][:, 0:1]
        mx = jnp.maximum(jnp.max(s, axis=-1, keepdims=True), sink)
        yield
        e = jnp.exp(s - mx)
        den = jnp.sum(e, axis=-1, keepdims=True) + jnp.exp(sink - mx)
        yield
        o = jnp.dot(e.astype(BF16), v_all, preferred_element_type=F32) / den
        yield
        mix_ref[rows(k), 0:128] = jnp.where(lo, o[0:CHUNK], pltpu.roll(o[CHUNK:2 * CHUNK], HEAD_DIM, axis=1))
        mix_ref[rows(k), 128:256] = jnp.where(lo, pltpu.roll(o[2 * CHUNK:3 * CHUNK], HEAD_DIM, axis=1),
                                              o[3 * CHUNK:4 * CHUNK])
        done.add(("A", k))
        yield

    def gating_unit(k, done):
        vn = _layer_norm(vb_ref[rows(k), :], vec[0:1], vec[1:2]).astype(BF16)
        yield
        sg = jnp.dot(ws_ref[...], _stack_heads(vn, masks), preferred_element_type=F32) + bias_ref[...]
        yield
        mix_ref[rows(k), GW:2 * GW] = ub_ref[rows(k), :] * sg
        done.add(("B", k))
        yield

    def retention_mix(k, o_fwd, o_bwd, done):
        while k not in o_fwd or k not in o_bwd:
            yield
        gmat = gmat_ref[...]
        normed = []
        for o in (o_fwd[k], o_bwd[k]):
            mu = jnp.dot(o.astype(BF16), gmat, preferred_element_type=F32)
            yield
            dlt = o - mu
            var = jnp.dot((dlt * dlt).astype(BF16), gmat, preferred_element_type=F32)
            yield
            normed.append(dlt * lax.rsqrt(var + LN_EPS))
        mix_ref[rows(k), 2 * GW:3 * GW] = (_silu(gf_ref[rows(k), :]) * (normed[0] * vec[3:4])
                                           + _silu(gb_ref[rows(k), :]) * (normed[1] * vec[4:5]))
        done.add(("C", k))
        yield

    def pooling(k, c, done):
        pd = pd_ref[rows(k), :]
        pext = pext_scr.at[k]
        zeros8 = jnp.zeros((POOL_HALO, GW), F32)
        if k % cps > 0:
            prev8 = pd_ref[k * CHUNK - POOL_HALO:k * CHUNK, :]
        else:
            prev8 = jnp.where(c > 0, pdp_ref[...], 0.0) if latent else zeros8
        if k % cps < cps - 1:
            next8 = pd_ref[(k + 1) * CHUNK:(k + 1) * CHUNK + POOL_HALO, :]
        else:
            next8 = jnp.where(c < nc - 1, pdn_ref[...], 0.0) if latent else zeros8
        pext[0:POOL_HALO, :] = prev8
        pext[POOL_HALO:POOL_HALO + CHUNK, :] = pd
        pext[POOL_HALO + CHUNK:2 * POOL_HALO + CHUNK, :] = next8
        yield

        def win(d, half):
            return pext[pl.ds(POOL_HALO + d, CHUNK), half * 128:(half + 1) * 128]

        a2 = win(-1, 0) + win(0, 0)
        a4 = a2 + win(-2, 0) + win(1, 0)
        yield
        a8 = win(-4, 1)
        for d in range(-3, 4):
            a8 = a8 + win(d, 1)
        yield
        a16 = a8
        for d in list(range(-8, -4)) + list(range(4, 8)):
            a16 = a16 + win(d, 1)
        yield
        sums = jnp.concatenate([jnp.where(lo, a2, a4), jnp.where(lo, a8, a16)], axis=1)
        yd = sums * cnt_ref[rows(k), :] - pd
        mix_ref[rows(k), 3 * GW:4 * GW] = _bdot(yd, wpool_ref[...]) * vec[2:3]
        done.add(("D", k))
        yield

    def out_projection(k, done):
        while not all((m, k) in done for m in "ABCD"):
            yield
        y = jnp.dot(mix_ref[rows(k), :].astype(BF16), wout_ref[...], preferred_element_type=F32)
        yield
        x1_ref[rows(k), :] = _layer_norm(ALPHA * xres_ref[rows(k), :] + g_a * y, ln[0:1], ln[2:3])
        yield

    def up_weight_rows():
        for lyr in range(DEPTH):
            for cc in range(NFC):
                upo_ref[lyr, cc, :, 0:FC] = upa_ref[lyr, :, cc * FC:(cc + 1) * FC].astype(BF16)
                upo_ref[lyr, cc, :, FC:2 * FC] = upg_ref[lyr, :, cc * FC:(cc + 1) * FC].astype(BF16)
                yield

    def forward_chains(chunk_of, o_bwd, fwd_states):
        done, o_fwd = set(), {}
        chains = [retention(ks, 0, state, o_fwd) for ks, state in fwd_states]
        for k in range(group):
            chains += [attention(k, chunk_of(k), done), gating_unit(k, done), pooling(k, chunk_of(k), done),
                       retention_mix(k, o_fwd, o_bwd, done), out_projection(k, done)]
        return chains

    if not latent:
        sb_scr[...] = jnp.zeros((CTX_SEQS, GW, GW), F32)
        sf_scr[...] = jnp.zeros((CTX_SEQS, GW, GW), F32)
        seq_chunks = [range(sq * cps, (sq + 1) * cps) for sq in range(CTX_SEQS)]
        o_bwd = {}
        side_chains = [retention(reversed(ks), 1, sb_scr.at[sq], o_bwd) for sq, ks in enumerate(seq_chunks)]
        if cast_up:
            side_chains.append(up_weight_rows())
        _interleave(side_chains + forward_chains(lambda k: k % cps, o_bwd,
                                                 [(ks, sf_scr.at[sq]) for sq, ks in enumerate(seq_chunks)]))
        for sq in range(CTX_SEQS):
            kn_ref[sq] = ka_ref[sq * SEQ:(sq + 1) * SEQ, :]
            vn_ref[sq] = va_ref[sq * SEQ:(sq + 1) * SEQ, :]
            _store_state(st_ref.at[sq], 1, sb_scr.at[sq])
            _store_state(st_ref.at[sq], 0, sf_scr.at[sq])
        return

    p = pl.program_id(1)
    g = pl.program_id(2)

    @pl.when(p == 0)
    def _():
        @pl.when(g == 0)
        def _():
            _load_state(sb_scr, s0_ref, 1)
            zero_blk = jnp.zeros((CHUNK, 2 * HEAD_DIM), BF16)
            for scr in (k_scr, v_scr):
                scr[0:CHUNK, :] = zero_blk
                scr[(nc + 1) * CHUNK:(nc + 2) * CHUNK, :] = zero_blk

        first = (ng - 1 - g) * group
        o_bwd = {}
        _interleave([retention(reversed(range(group)), 1, sb_scr, o_bwd)]
                    + [stage_kv(k, first + k) for k in range(group)])
        for k in range(group):
            ob_scr[pl.ds(_chunk_off(first + k), CHUNK), :] = o_bwd[k]

    @pl.when(p == 1)
    def _():
        @pl.when(g == 0)
        def _():
            _load_state(sf_scr, s0_ref, 0)

        first = g * group
        o_bwd = {k: ob_scr[pl.ds(_chunk_off(first + k), CHUNK), :] for k in range(group)}
        _interleave(forward_chains(lambda k: first + k, o_bwd, [(range(group), sf_scr)]))


def _mixer(z, x_res, mod, w_out_bf16, ln8, tabs, l, latent, extra=None, ffn_up=None):
    nb = DEC_BATCH if latent else BATCH // CTX_SEQS
    nc = (DEC_SEQ if latent else CTX_SEQS * SEQ) // CHUNK
    group = LAT_GROUP if latent else nc
    ng = nc // group
    blk = group * CHUNK
    base = (N_CTX // blk) if latent else 0
    per8 = blk // POOL_HALO
    last_halo = ROWS // POOL_HALO - 1

    def on_grid(f):
        return (lambda b, p, g: f(b, p, g)) if latent else (lambda b: f(b, 1, 0))

    def fwd(b, p, g):
        return base + b * ng + g * p

    def both(b, p, g):
        return base + b * ng + jnp.where(p == 0, ng - 1 - g, g)

    def bwd_only(b, p, g):
        return base + b * ng + (ng - 1 - g) * (1 - p)

    def col(width, idx, rowmap):
        return pl.BlockSpec((blk, width), on_grid(lambda b, p, g: (rowmap(b, p, g), idx)))

    def const(shape):
        return pl.BlockSpec(shape, on_grid(lambda b, p, g: (0,) * len(shape)))

    def layer(shape):
        return pl.BlockSpec((None,) + shape, on_grid(lambda b, p, g: (l,) + (0,) * len(shape)))

    specs, args = [], []

    def add(spec, arr):
        specs.append(spec)
        args.append(arr)

    add(col(GW, 0, fwd), z)
    add(col(128, 2, bwd_only if latent else fwd), z)
    add(col(128, 3, bwd_only if latent else fwd), z)
    if latent:
        add(pl.BlockSpec((None, None, PAST_LEN, 128), lambda b, p, g: (b, l, 0, 0)), extra["ck"])
        add(pl.BlockSpec((None, None, PAST_LEN, 128), lambda b, p, g: (b, l, 0, 0)), extra["cv"])
    add(col(GW, 2, fwd), z)
    add(col(GW, 3, fwd), z)
    add(col(GW, 4, both), z)
    add(col(GW, 5, both), z)
    add(col(GW, 6, both), z)
    add(col(GW, 7, fwd), z)
    add(col(GW, 8, fwd), z)
    add(col(GW, 9, fwd), z)
    if latent:
        add(pl.BlockSpec((POOL_HALO, GW),
                         lambda b, p, g: (jnp.maximum(fwd(b, p, g) * per8 - 1, 0), 9)), z)
        add(pl.BlockSpec((POOL_HALO, GW),
                         lambda b, p, g: (jnp.minimum((fwd(b, p, g) + 1) * per8, last_halo), 9)), z)
        rope_map = lambda b, p, g: (jnp.where(p == 0, ng - 1 - g, g), 0)
        add(pl.BlockSpec((blk, 128), rope_map), extra["cos"])
        add(pl.BlockSpec((blk, 128), rope_map), extra["sin"])
        add(pl.BlockSpec((None, None, 2, N_HEADS, HEAD_DIM, HEAD_DIM),
                         lambda b, p, g: (b, l, 0, 0, 0, 0)), extra["s0"])
    add(layer((2, N_HEADS * CHUNK, CHUNK)), tabs["dmat"])
    add(layer((4, CHUNK, GW)), tabs["dec"])
    add(layer((N_HEADS * CHUNK, 128)), tabs["sink"])
    add(layer((CHUNK, N_HEADS * CHUNK)), tabs["ws"])
    add(layer((CHUNK, GW)), tabs["bias"])
    add(layer((8, GW)), tabs["vec"])
    add(const((GW, GW)), tabs["gmat"])
    add(pl.BlockSpec((blk, GW), on_grid(lambda b, p, g: (g * p, 0))),
        tabs["cnt_lat"] if latent else tabs["cnt_ctx"])
    add(layer((GW, GW)), tabs["wpool"])
    local = lambda b, p, g: (b * ng + g * p, 0)
    add(pl.BlockSpec((blk, D_MODEL), on_grid(local)), x_res)
    add(pl.BlockSpec((None, None, 6, D_MODEL),
                     on_grid(lambda b, p, g: (l, (1 + b) if latent else 0, 0, 0))), mod)
    add(layer((D_MODEL, D_MODEL)), w_out_bf16)
    add(layer((8, D_MODEL)), ln8)
    up_rows = D_MODEL // nb
    if ffn_up is not None:
        add(pl.BlockSpec((DEPTH, up_rows, D_FF), lambda b: (0, b, 0)), ffn_up)
        add(pl.BlockSpec((DEPTH, up_rows, D_FF), lambda b: (0, b, 1)), ffn_up)

    out_shape = [jax.ShapeDtypeStruct((nb * nc * CHUNK, D_MODEL), F32)]
    out_specs = [pl.BlockSpec((blk, D_MODEL), on_grid(local))]
    state_shape = (GW, GW) if latent else (CTX_SEQS, GW, GW)
    scratch = [pltpu.VMEM(state_shape, F32), pltpu.VMEM(state_shape, F32),
               pltpu.VMEM((nc * CHUNK, GW), F32),
               pltpu.VMEM((group, CHUNK + 2 * POOL_HALO, GW), F32),
               pltpu.VMEM((blk, D_MODEL), F32)]
    if latent:
        scratch += [pltpu.VMEM(((nc + 2) * CHUNK, 128), BF16), pltpu.VMEM(((nc + 2) * CHUNK, 128), BF16)]
    else:
        out_shape.append(jax.ShapeDtypeStruct((BATCH, 2, N_HEADS, HEAD_DIM, HEAD_DIM), F32))
        out_specs.append(pl.BlockSpec((CTX_SEQS, 2, N_HEADS, HEAD_DIM, HEAD_DIM), lambda b: (b, 0, 0, 0, 0)))
        for _ in range(2):
            out_shape.append(jax.ShapeDtypeStruct((BATCH, SEQ, 128), F32))
            out_specs.append(pl.BlockSpec((CTX_SEQS, SEQ, 128), lambda b: (b, 0, 0)))
        if ffn_up is not None:
            out_shape.append(jax.ShapeDtypeStruct((DEPTH, NFC, D_MODEL, 2 * FC), BF16))
            out_specs.append(pl.BlockSpec((DEPTH, NFC, up_rows, 2 * FC), lambda b: (0, 0, b, 0)))

    return pl.pallas_call(
        functools.partial(_mixer_kernel, latent=latent, group=group, ng=ng, cast_up=ffn_up is not None),
        grid=(nb, 2, ng) if latent else (nb,),
        in_specs=specs,
        out_specs=out_specs,
        out_shape=out_shape,
        scratch_shapes=scratch,
        compiler_params=pltpu.CompilerParams(
            dimension_semantics=("arbitrary",) * (3 if latent else 1), vmem_limit_bytes=VMEM_LIMIT),
        name="mixer_latent" if latent else "mixer_context",
    )(*args)


def _pad_rows(rows, n=8):
    a = jnp.stack(rows)
    return jnp.concatenate([a, jnp.zeros((n - a.shape[0],) + a.shape[1:], a.dtype)], axis=0)


def _block_diag(blocks):
    g, n, _ = blocks.shape
    eye = jnp.eye(g, dtype=blocks.dtype)
    return (eye[:, None, :, None] * blocks[:, :, None, :]).reshape(g * n, g * n)


def _inv_count(n, seqs=1):
    t = np.arange(n)
    cols = []
    for w in POOL_WINDOWS:
        cnt = np.clip(t + w // 2, 0, n) - np.clip(t - w // 2, 0, n)
        cols.append(np.repeat((1.0 / cnt)[:, None], HEAD_DIM, axis=1))
    return jnp.asarray(np.tile(np.concatenate(cols, axis=1), (seqs, 1)), F32)


def _rope_tables():
    rows = DEC_SEQ // GRID_W
    r, cc = jnp.meshgrid(jnp.arange(rows), jnp.arange(GRID_W), indexing="ij")
    half = HEAD_DIM // 2
    freqs = ROPE_BASE ** (-jnp.arange(0, half, 2, dtype=F32) / half)

    def tables(pos):
        ang = pos.reshape(-1).astype(F32)[:, None] * freqs[None, :]
        cos, sin = jnp.cos(ang), jnp.sin(ang)
        return jnp.concatenate([cos, cos], axis=1), jnp.concatenate([-sin, sin], axis=1)

    cr, sr = tables(r)
    ccol, scol = tables(cc)
    cos = jnp.concatenate([cr, ccol], axis=1)
    sin = jnp.concatenate([sr, scol], axis=1)
    return jnp.tile(cos, (1, 2)), jnp.tile(sin, (1, 2))


def _layer_tables(attn_sink, sgu_norm_w, sgu_norm_b, sgu_ws, sgu_bs, ret_decay, ret_gn_w, pool_w, pool_scale):
    log_g = jax.nn.log_sigmoid(ret_decay.astype(F32))
    i = jnp.arange(CHUNK, dtype=F32)
    rel = i[:, None] - i[None, :]
    kscale = HEAD_DIM ** -0.5
    d_f = jnp.where(rel >= 0, jnp.exp(jnp.maximum(rel, 0.0)[None] * log_g[0][:, None, None]), 0.0)
    d_b = jnp.where(rel <= 0, jnp.exp(jnp.maximum(-rel, 0.0)[None] * log_g[1][:, None, None]), 0.0)
    dmat = jnp.stack([d_f.reshape(N_HEADS * CHUNK, CHUNK), d_b.reshape(N_HEADS * CHUNK, CHUNK)]) * kscale

    def lanes(per_head):
        return jnp.repeat(per_head, HEAD_DIM, axis=1)

    qdec_f = lanes(jnp.exp((i + 1.0)[:, None] * log_g[0][None, :]))
    qdec_b = lanes(jnp.exp((CHUNK - i)[:, None] * log_g[1][None, :]))
    kdec_f = lanes(jnp.exp((CHUNK - 1.0 - i)[:, None] * log_g[0][None, :])) * kscale
    kdec_b = lanes(jnp.exp(i[:, None] * log_g[1][None, :])) * kscale
    cdec = jnp.repeat(jnp.exp(CHUNK * log_g), HEAD_DIM, axis=1)
    vec = _pad_rows([sgu_norm_w, sgu_norm_b, pool_scale, ret_gn_w[0], ret_gn_w[1], cdec[0], cdec[1]])
    return {
        "dmat": dmat,
        "dec": jnp.stack([qdec_f, qdec_b, kdec_f, kdec_b]),
        "sink": jnp.broadcast_to(jnp.repeat(attn_sink, CHUNK)[:, None], (N_HEADS * CHUNK, 128)),
        "ws": jnp.concatenate([sgu_ws[h] for h in range(N_HEADS)], axis=1).astype(BF16),
        "bias": jnp.repeat(sgu_bs.T, HEAD_DIM, axis=1),
        "vec": vec,
        "wpool": _block_diag(pool_w).astype(BF16),
    }


def kernel(x_prompt, x_sample, cache_attn_k, cache_attn_v, state_ret, c, c_ctx, w_ada, b_ada, w_in,
           w_out, attn_sink, sgu_norm_w, sgu_norm_b, sgu_ws, sgu_bs, ret_decay, ret_gn_w, pool_w,
           pool_scale, ffn_up, ffn_conv_w, ffn_conv_b, ffn_down, ln_w, ln_b):
    cond8 = jnp.concatenate([c_ctx[None], c, jnp.zeros((8 - 1 - DEC_BATCH, D_MODEL), F32)], axis=0)
    mod = _modulation(cond8, w_ada, b_ada).reshape(DEPTH, 8, 6, D_MODEL)

    tabs = jax.vmap(_layer_tables)(attn_sink, sgu_norm_w, sgu_norm_b, sgu_ws, sgu_bs, ret_decay, ret_gn_w,
                                   pool_w, pool_scale)
    tabs["gmat"] = _block_diag(jnp.full((N_HEADS, HEAD_DIM, HEAD_DIM), 1.0 / HEAD_DIM, F32)).astype(BF16)
    tabs["cnt_ctx"] = _inv_count(SEQ, CTX_SEQS)
    tabs["cnt_lat"] = _inv_count(DEC_SEQ)
    cos, sin = _rope_tables()
    extra = {"ck": cache_attn_k.reshape(DEC_BATCH, DEPTH, PAST_LEN, 128),
             "cv": cache_attn_v.reshape(DEC_BATCH, DEPTH, PAST_LEN, 128),
             "cos": cos, "sin": sin, "s0": state_ret}
    ln8 = jnp.concatenate([ln_w, ln_b, jnp.zeros((DEPTH, 4, D_MODEL), F32)], axis=1)
    conv8 = jnp.concatenate([ffn_conv_w, ffn_conv_b[:, None], jnp.zeros((DEPTH, 4, 2 * D_FF), F32)], axis=1)

    w_out_bf16 = _to_bf16(w_out, D_MODEL)

    xs = [x_prompt.reshape(N_CTX, D_MODEL), x_sample.reshape(N_LAT, D_MODEL)]
    new_k, new_v, new_s = [], [], []
    for l in range(DEPTH):
        z = _inproj(xs, mod, w_in, l)
        if l == 0:
            x1_ctx, st, kn, vn, up_chunks = _mixer(z, xs[0], mod, w_out_bf16, ln8, tabs, l, latent=False,
                                                   ffn_up=ffn_up)
        else:
            x1_ctx, st, kn, vn = _mixer(z, xs[0], mod, w_out_bf16, ln8, tabs, l, latent=False)
        (x1_lat,) = _mixer(z, xs[1], mod, w_out_bf16, ln8, tabs, l, latent=True, extra=extra)
        xs = _ffn([x1_ctx, x1_lat], mod, up_chunks, conv8, ffn_down, ln8, l)
        new_k.append(kn.reshape(BATCH, SEQ, 2, HEAD_DIM))
        new_v.append(vn.reshape(BATCH, SEQ, 2, HEAD_DIM))
        new_s.append(st)

    y_prompt = xs[0].reshape(BATCH, SEQ, D_MODEL)
    y_sample = xs[1].reshape(DEC_BATCH, DEC_SEQ, D_MODEL)
    return (y_prompt, y_sample, jnp.stack(new_k, axis=1), jnp.stack(new_v, axis=1),
            jnp.stack(new_s, axis=1))
```

```python
import functools

import numpy as np
import jax
import jax.numpy as jnp
from jax import lax
from jax.experimental import pallas as pl
from jax.experimental.pallas import tpu as pltpu

F32 = jnp.float32
BF16 = jnp.bfloat16

D_MODEL = 1024
BATCH = 16
SEQ = 256
DEPTH = 2
DEC_BATCH = 2
DEC_SEQ = 2048
PAST_LEN = 256
GRID_W = 64
CHUNK = 128
HEAD_DIM = 64
GW = D_MODEL // 4
N_HEADS = 4
POOL_WINDOWS = (2, 4, 8, 16)
POOL_HALO = 8
D_FF = 2816
ROPE_BASE = 10000.0
LN_EPS = 1e-5
NEG_INF = -1e30
IN_WIDTH = 10 * GW
ALPHA = (2.0 * DEPTH) ** 0.25

N_CTX = BATCH * SEQ
N_LAT = DEC_BATCH * DEC_SEQ
ROWS = N_CTX + N_LAT

TM = 1024
NB_IN = 512
FC = 256
NB_ADA = 3072
VMEM_LIMIT = 56 * 1024 * 1024


def _cond_of_tile(i, tm=TM):
    ctx_tiles = N_CTX // tm
    return jnp.where(i < ctx_tiles, 0, 1 + (i - ctx_tiles) // (DEC_SEQ // tm))


def _tile_specs(tm):
    ctx_tiles = N_CTX // tm
    return [pl.BlockSpec((tm, D_MODEL), lambda i: (jnp.minimum(i, ctx_tiles - 1), 0)),
            pl.BlockSpec((tm, D_MODEL), lambda i: (jnp.maximum(i - ctx_tiles, 0), 0))]


def _per_half(i, tm, fn):
    ctx_tiles = N_CTX // tm

    @pl.when(i < ctx_tiles)
    def _():
        fn(0)

    @pl.when(i >= ctx_tiles)
    def _():
        fn(1)


def _layer_norm(x, w, b):
    mu = jnp.mean(x, axis=-1, keepdims=True)
    d = x - mu
    var = jnp.mean(d * d, axis=-1, keepdims=True)
    return d * lax.rsqrt(var + LN_EPS) * w + b


def _silu(x):
    return x * jax.nn.sigmoid(x)


def _bdot(a, b):
    return jnp.dot(a.astype(BF16), b.astype(BF16), preferred_element_type=F32)


def _mod_kernel(c_ref, w_ref, b_ref, o_ref):
    o_ref[...] = _bdot(_silu(c_ref[...]), w_ref[...]) + b_ref[...]


def _modulation(cond8, w_ada, b_ada):
    return pl.pallas_call(
        _mod_kernel,
        grid=(DEPTH, 6 * D_MODEL // NB_ADA),
        in_specs=[
            pl.BlockSpec((8, D_MODEL), lambda l, j: (0, 0)),
            pl.BlockSpec((None, D_MODEL, NB_ADA), lambda l, j: (l, 0, j)),
            pl.BlockSpec((None, 1, NB_ADA), lambda l, j: (l, 0, j)),
        ],
        out_specs=pl.BlockSpec((None, 8, NB_ADA), lambda l, j: (l, 0, j)),
        out_shape=jax.ShapeDtypeStruct((DEPTH, 8, 6 * D_MODEL), F32),
        compiler_params=pltpu.CompilerParams(
            dimension_semantics=("arbitrary", "arbitrary"), vmem_limit_bytes=VMEM_LIMIT),
        name="modulation",
    )(cond8, w_ada, b_ada.reshape(DEPTH, 1, 6 * D_MODEL))


def _cast_kernel(w_ref, o_ref):
    o_ref[...] = w_ref[...].astype(BF16)


def _to_bf16(w, block_rows):
    depth, rows, cols = w.shape
    return pl.pallas_call(
        _cast_kernel,
        grid=(depth, rows // block_rows),
        in_specs=[pl.BlockSpec((None, block_rows, cols), lambda l, r: (l, r, 0))],
        out_specs=pl.BlockSpec((None, block_rows, cols), lambda l, r: (l, r, 0)),
        out_shape=jax.ShapeDtypeStruct(w.shape, BF16),
        compiler_params=pltpu.CompilerParams(dimension_semantics=("arbitrary", "arbitrary")),
        name="cast_bf16",
    )(w)


def _inproj_kernel(xc_ref, xl_ref, mod_ref, w_ref, z_ref, h_scr):
    x_refs = (xc_ref, xl_ref)
    m = mod_ref[...]

    def project(side):
        for rh in range(2):
            rows = slice(rh * TM // 2, (rh + 1) * TM // 2)
            h_scr[rows, :] = (x_refs[side][rows, :] * (1.0 + m[1:2]) + m[0:1]).astype(BF16)
            for jb in range(IN_WIDTH // NB_IN):
                cols = slice(jb * NB_IN, (jb + 1) * NB_IN)
                z_ref[rows, cols] = jnp.dot(h_scr[rows, :], w_ref[:, cols].astype(BF16),
                                            preferred_element_type=F32)

    _per_half(pl.program_id(0), TM, project)


def _inproj(xs, mod, w_in, l):
    return pl.pallas_call(
        _inproj_kernel,
        grid=(ROWS // TM,),
        in_specs=_tile_specs(TM) + [
            pl.BlockSpec((None, None, 6, D_MODEL), lambda i: (l, _cond_of_tile(i), 0, 0)),
            pl.BlockSpec((None, D_MODEL, IN_WIDTH), lambda i: (l, 0, 0), pipeline_mode=pl.Buffered(1)),
        ],
        out_specs=pl.BlockSpec((TM, IN_WIDTH), lambda i: (i, 0)),
        out_shape=jax.ShapeDtypeStruct((ROWS, IN_WIDTH), F32),
        scratch_shapes=[pltpu.VMEM((TM, D_MODEL), BF16)],
        compiler_params=pltpu.CompilerParams(
            dimension_semantics=("arbitrary",), vmem_limit_bytes=VMEM_LIMIT),
        name="inproj",
    )(*xs, mod, w_in)


TMF = 512
SEGF = TMF // 8
PITCHF = SEGF + 8
CTXF_TILES = N_CTX // TMF
LATF_PER_SEQ = DEC_SEQ // TMF
HALO_ROWS = 16
RBUF = 256
RB = 256
GB = 64
NFC = D_FF // FC
LANE_BLOCKS = D_MODEL // 128
assert RB == RBUF


def _seg_rows_f(xc_ref, k):
    return jnp.concatenate([xc_ref[cb, pl.ds(k, 8, stride=PITCHF), :] for cb in range(LANE_BLOCKS)], axis=1)


def _ffn_kernel(xc_ref, xl_ref, xp_ref, xn_ref, mod_ref, up_ref, cv_ref, dn_ref, ln_ref, oc_ref, ol_ref,
                h_scr, act_scr, xc_scr, u0_scr, u1_scr):
    x_refs = (xc_ref, xl_ref)
    o_refs = (oc_ref, ol_ref)
    u_scrs = (u0_scr, u1_scr)
    i = pl.program_id(0)
    is_ctx = i < CTXF_TILES
    lat_pos = (i - CTXF_TILES) % LATF_PER_SEQ
    m = mod_ref[...]
    ln = ln_ref[...]

    scale = 1.0 + m[4:5]
    shift = m[3:4]

    def stage(side):
        for cb in range(LANE_BLOCKS):
            for s in range(8):
                xc_scr[cb, s * PITCHF:s * PITCHF + SEGF, :] = x_refs[side][s * SEGF:(s + 1) * SEGF,
                                                                            cb * 128:(cb + 1) * 128]

    _per_half(i, TMF, stage)

    def build_h(ub):
        for k in range(ub * RBUF // 8, (ub + 1) * RBUF // 8, 2):
            rows = jnp.concatenate([_seg_rows_f(xc_scr, k), _seg_rows_f(xc_scr, k + 1)], axis=0)
            h_scr[8 * k:8 * k + 16, :] = (rows * scale + shift).astype(BF16)

    sub16 = lax.broadcasted_iota(jnp.int32, (HALO_ROWS, D_MODEL), 0)
    prev_ok = jnp.logical_not(is_ctx) & (lat_pos > 0)
    next_ok = jnp.logical_not(is_ctx) & (lat_pos < LATF_PER_SEQ - 1)
    halo_x = jnp.where(sub16 == 0, xp_ref[POOL_HALO - 1:POOL_HALO, :], xn_ref[0:1, :])
    keep = ((sub16 == 0) & prev_ok) | ((sub16 == 1) & next_ok)
    h_scr[TMF:TMF + HALO_ROWS, :] = jnp.where(keep, halo_x * scale + shift, 0.0).astype(BF16)

    n_ub = TMF // RBUF
    sub = lax.broadcasted_iota(jnp.int32, (8, FC), 0)
    seg_per_seq = SEQ // SEGF
    ctx_first = is_ctx & (sub % seg_per_seq == 0)
    ctx_last = is_ctx & (sub % seg_per_seq == seg_per_seq - 1)

    def up_proj(slot, c, ub):
        rows = slice(ub * RBUF, (ub + 1) * RBUF + (HALO_ROWS if ub == n_ub - 1 else 0))
        u_scrs[slot][rows, :] = jnp.dot(h_scr[rows, :], up_ref[c], preferred_element_type=F32)

    def conv(u_ref, lanes, cvs, r0):
        lo = max(r0 - 8, 0)
        hi = min(r0 + GB + 8, TMF)
        ue = u_ref[lo:hi, lanes]
        u = ue[r0 - lo:r0 - lo + GB]
        if r0 == 0:
            b_first = jnp.where(sub == 0, u_ref[TMF:TMF + 1, lanes],
                                pltpu.roll(u_ref[TMF - 8:TMF, lanes], 1, axis=0))
            um1 = jnp.concatenate([jnp.where(ctx_first, 0.0, b_first), u[0:GB - 8]], axis=0)
        else:
            um1 = ue[0:GB]
        if r0 == TMF - GB:
            b_last = jnp.where(sub == 7, u_ref[TMF + 1:TMF + 2, lanes],
                               pltpu.roll(u_ref[0:8, lanes], 7, axis=0))
            up1 = jnp.concatenate([u[8:GB], jnp.where(ctx_last, 0.0, b_last)], axis=0)
        else:
            up1 = ue[r0 - lo + 8:r0 - lo + GB + 8]
        return um1 * cvs[0:1] + u * cvs[1:2] + up1 * cvs[2:3] + cvs[3:4]

    def gate(slot, c, ub):
        cva = cv_ref[0:4, c * FC:(c + 1) * FC]
        cvg = cv_ref[0:4, D_FF + c * FC:D_FF + (c + 1) * FC]
        for r0 in range(ub * RBUF, (ub + 1) * RBUF, GB):
            a = conv(u_scrs[slot], slice(0, FC), cva, r0)
            g = conv(u_scrs[slot], slice(FC, 2 * FC), cvg, r0)
            act_scr[c, r0:r0 + GB, :] = (_silu(a) * g).astype(BF16)

    def finish(rb):
        vrows = range(rb * RB // 8, (rb + 1) * RB // 8)
        lhs = jnp.concatenate([act_scr[c, rb * RB:(rb + 1) * RB, :] for c in range(NFC)], axis=1)
        y = jnp.dot(lhs, dn_ref[...].astype(BF16), preferred_element_type=F32)
        xr = jnp.concatenate([_seg_rows_f(xc_scr, k) for k in vrows], axis=0)
        out = _layer_norm(ALPHA * xr + m[5:6] * y, ln[1:2], ln[3:4])
        for kk, k in enumerate(vrows):
            for cb in range(LANE_BLOCKS):
                xc_scr[cb, pl.ds(k, 8, stride=PITCHF), :] = out[8 * kk:8 * kk + 8, cb * 128:(cb + 1) * 128]

    def write_out(side):
        for cb in range(LANE_BLOCKS):
            for s in range(8):
                o_refs[side][s * SEGF:(s + 1) * SEGF, cb * 128:(cb + 1) * 128] = xc_scr[
                    cb, s * PITCHF:s * PITCHF + SEGF, :]

    build_h(0)
    for c in range(NFC + 1):
        for ub in range(n_ub):
            if c == 0 and ub > 0:
                build_h(ub)
            if c < NFC:
                up_proj(c % 2, c, ub)
            if c >= 1:
                gate((c - 1) % 2, c - 1, ub)
            if c == NFC:
                finish(ub)

    _per_half(i, TMF, write_out)


def _ffn(xs, mod, up_chunks, conv8, down, ln8, l):
    halo_blocks = TMF // POOL_HALO
    last_halo = N_LAT // POOL_HALO - 1
    return pl.pallas_call(
        _ffn_kernel,
        grid=(ROWS // TMF,),
        in_specs=[
            *_tile_specs(TMF),
            pl.BlockSpec((POOL_HALO, D_MODEL),
                         lambda i: (jnp.maximum((i - CTXF_TILES) * halo_blocks - 1, 0), 0)),
            pl.BlockSpec((POOL_HALO, D_MODEL),
                         lambda i: (jnp.clip((i - CTXF_TILES + 1) * halo_blocks, 0, last_halo), 0)),
            pl.BlockSpec((None, None, 6, D_MODEL), lambda i: (l, _cond_of_tile(i, TMF), 0, 0)),
            pl.BlockSpec((None, NFC, D_MODEL, 2 * FC), lambda i: (l, 0, 0, 0), pipeline_mode=pl.Buffered(1)),
            pl.BlockSpec((None, 8, 2 * D_FF), lambda i: (l, 0, 0)),
            pl.BlockSpec((None, D_FF, D_MODEL), lambda i: (l, 0, 0), pipeline_mode=pl.Buffered(1)),
            pl.BlockSpec((None, 8, D_MODEL), lambda i: (l, 0, 0)),
        ],
        out_specs=_tile_specs(TMF),
        out_shape=[jax.ShapeDtypeStruct((N_CTX, D_MODEL), F32), jax.ShapeDtypeStruct((N_LAT, D_MODEL), F32)],
        scratch_shapes=[pltpu.VMEM((TMF + HALO_ROWS, D_MODEL), BF16),
                        pltpu.VMEM((NFC, TMF, FC), BF16),
                        pltpu.VMEM((LANE_BLOCKS, 8 * PITCHF, 128), F32),
                        pltpu.VMEM((TMF + HALO_ROWS, 2 * FC), F32),
                        pltpu.VMEM((TMF + HALO_ROWS, 2 * FC), F32)],
        compiler_params=pltpu.CompilerParams(
            dimension_semantics=("arbitrary",), vmem_limit_bytes=VMEM_LIMIT),
        name="convffn",
    )(xs[0], xs[1], xs[1], xs[1], mod, up_chunks, conv8, down, ln8)


def _head_masks(width):
    lane = lax.broadcasted_iota(jnp.int32, (1, width), 1)
    return [(lane >= h * HEAD_DIM) & (lane < (h + 1) * HEAD_DIM) for h in range(width // HEAD_DIM)]


def _stack_heads(x, masks):
    return jnp.concatenate([jnp.where(m, x, jnp.zeros_like(x)) for m in masks], axis=0)


def _rope(x, cos, sin):
    lane = lax.broadcasted_iota(jnp.int32, (1, 128), 1)
    lower = (lane & 31) < 16
    outs = []
    for k in range(x.shape[1] // 128):
        xb = x[:, k * 128:(k + 1) * 128]
        partner = jnp.where(lower, pltpu.roll(xb, 112, axis=1), pltpu.roll(xb, 16, axis=1))
        outs.append(xb * cos + partner * sin)
    return outs[0] if len(outs) == 1 else jnp.concatenate(outs, axis=1)


def _load_state(state_scr, blocks_ref, d):
    state_scr[...] = jnp.zeros((GW, GW), F32)
    for h in range(N_HEADS):
        sl = slice(h * HEAD_DIM, (h + 1) * HEAD_DIM)
        state_scr[sl, sl] = blocks_ref[d, h]


def _store_state(blocks_ref, d, state_scr):
    for h in range(N_HEADS):
        sl = slice(h * HEAD_DIM, (h + 1) * HEAD_DIM)
        blocks_ref[d, h] = state_scr[sl, sl]


LAT_GROUP = 4
CTX_SEQS = 2


def _chunk_off(c):
    return c * CHUNK if isinstance(c, int) else pl.multiple_of(c * CHUNK, CHUNK)


def _interleave(chains):
    chains = list(chains)
    while chains:
        for ch in list(chains):
            try:
                next(ch)
            except StopIteration:
                chains.remove(ch)


def _mixer_kernel(*refs, latent, group, ng, cast_up=False):
    if latent:
        (qa_ref, ka_ref, va_ref, kx_ref, vx_ref, ub_ref, vb_ref, qc_ref, kc_ref, vc_ref, gf_ref, gb_ref,
         pd_ref, pdp_ref, pdn_ref, cos_ref, sin_ref, s0_ref, dmat_ref, dec_ref, sink_ref, ws_ref, bias_ref,
         vec_ref, gmat_ref, cnt_ref, wpool_ref, xres_ref, mod_ref, wout_ref, ln_ref,
         x1_ref, sf_scr, sb_scr, ob_scr, pext_scr, mix_ref, k_scr, v_scr) = refs
    else:
        (qa_ref, ka_ref, va_ref, ub_ref, vb_ref, qc_ref, kc_ref, vc_ref, gf_ref, gb_ref, pd_ref,
         dmat_ref, dec_ref, sink_ref, ws_ref, bias_ref, vec_ref, gmat_ref, cnt_ref, wpool_ref,
         xres_ref, mod_ref, wout_ref, ln_ref) = refs[:24]
        if cast_up:
            upa_ref, upg_ref, x1_ref, st_ref, kn_ref, vn_ref, upo_ref = refs[24:31]
        else:
            x1_ref, st_ref, kn_ref, vn_ref = refs[24:28]
        sf_scr, sb_scr, ob_scr, pext_scr, mix_ref = refs[-5:]

    nc = group * ng
    cps = group if latent else SEQ // CHUNK
    g_a = mod_ref[...][2:3]
    ln = ln_ref[...]
    masks = _head_masks(GW)
    row = lax.broadcasted_iota(jnp.int32, (GW, GW), 0)
    col = lax.broadcasted_iota(jnp.int32, (GW, GW), 1)
    bd_mask = (row // HEAD_DIM) == (col // HEAD_DIM)
    vec = vec_ref[...]
    lane = lax.broadcasted_iota(jnp.int32, (1, 2 * HEAD_DIM), 1)
    lo = lane < HEAD_DIM

    def rows(k):
        return slice(k * CHUNK, (k + 1) * CHUNK)

    def retention(order, d, state_scr, out):
        dmat = dmat_ref[d]
        qdec, kdec = dec_ref[d], dec_ref[2 + d]
        cdec = vec[5 + d:6 + d]
        state = state_scr[...]
        for k in order:
            q, kk, v = qc_ref[rows(k), :], kc_ref[rows(k), :], vc_ref[rows(k), :]
            qb, kb, vb = q.astype(BF16), kk.astype(BF16), v.astype(BF16)
            s = lax.dot_general(_stack_heads(qb, masks), kb, (((1,), (1,)), ((), ())),
                                preferred_element_type=F32)
            yield
            p = (s * dmat).astype(BF16)
            p_cat = jnp.concatenate([p[h * CHUNK:(h + 1) * CHUNK] for h in range(N_HEADS)], axis=1)
            o = (jnp.dot(p_cat, _stack_heads(vb, masks), preferred_element_type=F32)
                 + _bdot(q * qdec, state))
            yield
            upd = lax.dot_general((kk * kdec).astype(BF16), vb, (((0,), (0,)), ((), ())),
                                  preferred_element_type=F32)
            state = state * cdec + jnp.where(bd_mask, upd, 0.0)
            out[k] = o
            yield
        state_scr[...] = state

    def stage_kv(k, c):
        dst = pl.ds(_chunk_off(c + 1), CHUNK)
        k_scr[dst, :] = _rope(ka_ref[rows(k), :], cos_ref[rows(k), :], sin_ref[rows(k), :]).astype(BF16)
        yield
        v_scr[dst, :] = va_ref[rows(k), :].astype(BF16)
        yield

    def attention(k, c, done):
        q = qa_ref[rows(k), :]
        if latent:
            q = _rope(q, cos_ref[rows(k), :], sin_ref[rows(k), :])
        q = q * (HEAD_DIM ** -0.5)
        q0, q1 = q[:, :128], q[:, 128:]
        zero = jnp.zeros_like(q0)
        q_st = jnp.concatenate([
            jnp.where(lo, q0, zero),
            jnp.where(lo, pltpu.roll(q0, HEAD_DIM, axis=1), zero),
            jnp.where(lo, zero, pltpu.roll(q1, HEAD_DIM, axis=1)),
            jnp.where(lo, zero, q1)], axis=0).astype(BF16)
        if latent:
            band = pl.ds(_chunk_off(c), 3 * CHUNK)
            k_all = jnp.concatenate([k_scr[band, :], kx_ref[...].astype(BF16)], axis=0)
            v_all = jnp.concatenate([v_scr[band, :], vx_ref[...].astype(BF16)], axis=0)
        else:
            own = slice(k // cps * SEQ, (k // cps + 1) * SEQ)
            k_all = ka_ref[own, :].astype(BF16)
            v_all = va_ref[own, :].astype(BF16)
        s = lax.dot_general(q_st, k_all, (((1,), (1,)), ((), ())), preferred_element_type=F32)
        yield
        if latent:
            nk = 3 * CHUNK + PAST_LEN
            qi = lax.broadcasted_iota(jnp.int32, (N_HEADS * CHUNK, nk), 0) & (CHUNK - 1)
            kj = lax.broadcasted_iota(jnp.int32, (N_HEADS * CHUNK, nk), 1)
            kpos = kj + (c - 1) * CHUNK
            valid = (kj >= 3 * CHUNK) | ((kj >= qi) & (kj <= qi + 2 * CHUNK)
                                         & (kpos >= 0) & (kpos < nc * CHUNK))
            s = jnp.where(valid, s, NEG_INF)
        sink = sink_ref[...][:, 0:1]
        mx = jnp.maximum(jnp.max(s, axis=-1, keepdims=True), sink)
        yield
        e = jnp.exp(s - mx)
        den = jnp.sum(e, axis=-1, keepdims=True) + jnp.exp(sink - mx)
        yield
        o = jnp.dot(e.astype(BF16), v_all, preferred_element_type=F32) / den
        yield
        mix_ref[rows(k), 0:128] = jnp.where(lo, o[0:CHUNK], pltpu.roll(o[CHUNK:2 * CHUNK], HEAD_DIM, axis=1))
        mix_ref[rows(k), 128:256] = jnp.where(lo, pltpu.roll(o[2 * CHUNK:3 * CHUNK], HEAD_DIM, axis=1),
                                              o[3 * CHUNK:4 * CHUNK])
        done.add(("A", k))
        yield

    def gating_unit(k, done):
        vn = _layer_norm(vb_ref[rows(k), :], vec[0:1], vec[1:2]).astype(BF16)
        yield
        sg = jnp.dot(ws_ref[...], _stack_heads(vn, masks), preferred_element_type=F32) + bias_ref[...]
        yield
        mix_ref[rows(k), GW:2 * GW] = ub_ref[rows(k), :] * sg
        done.add(("B", k))
        yield

    def retention_mix(k, o_fwd, o_bwd, done):
        while k not in o_fwd or k not in o_bwd:
            yield
        gmat = gmat_ref[...]
        normed = []
        for o in (o_fwd[k], o_bwd[k]):
            mu = jnp.dot(o.astype(BF16), gmat, preferred_element_type=F32)
            yield
            dlt = o - mu
            var = jnp.dot((dlt * dlt).astype(BF16), gmat, preferred_element_type=F32)
            yield
            normed.append(dlt * lax.rsqrt(var + LN_EPS))
        mix_ref[rows(k), 2 * GW:3 * GW] = (_silu(gf_ref[rows(k), :]) * (normed[0] * vec[3:4])
                                           + _silu(gb_ref[rows(k), :]) * (normed[1] * vec[4:5]))
        done.add(("C", k))
        yield

    def pooling(k, c, done):
        pd = pd_ref[rows(k), :]
        pext = pext_scr.at[k]
        zeros8 = jnp.zeros((POOL_HALO, GW), F32)
        if k % cps > 0:
            prev8 = pd_ref[k * CHUNK - POOL_HALO:k * CHUNK, :]
        else:
            prev8 = jnp.where(c > 0, pdp_ref[...], 0.0) if latent else zeros8
        if k % cps < cps - 1:
            next8 = pd_ref[(k + 1) * CHUNK:(k + 1) * CHUNK + POOL_HALO, :]
        else:
            next8 = jnp.where(c < nc - 1, pdn_ref[...], 0.0) if latent else zeros8
        pext[0:POOL_HALO, :] = prev8
        pext[POOL_HALO:POOL_HALO + CHUNK, :] = pd
        pext[POOL_HALO + CHUNK:2 * POOL_HALO + CHUNK, :] = next8
        yield

        def win(d, half):
            return pext[pl.ds(POOL_HALO + d, CHUNK), half * 128:(half + 1) * 128]

        a2 = win(-1, 0) + win(0, 0)
        a4 = a2 + win(-2, 0) + win(1, 0)
        yield
        a8 = win(-4, 1)
        for d in range(-3, 4):
            a8 = a8 + win(d, 1)
        yield
        a16 = a8
        for d in list(range(-8, -4)) + list(range(4, 8)):
            a16 = a16 + win(d, 1)
        yield
        sums = jnp.concatenate([jnp.where(lo, a2, a4), jnp.where(lo, a8, a16)], axis=1)
        yd = sums * cnt_ref[rows(k), :] - pd
        mix_ref[rows(k), 3 * GW:4 * GW] = _bdot(yd, wpool_ref[...]) * vec[2:3]
        done.add(("D", k))
        yield

    def out_projection(k, done):
        while not all((m, k) in done for m in "ABCD"):
            yield
        y = jnp.dot(mix_ref[rows(k), :].astype(BF16), wout_ref[...], preferred_element_type=F32)
        yield
        x1_ref[rows(k), :] = _layer_norm(ALPHA * xres_ref[rows(k), :] + g_a * y, ln[0:1], ln[2:3])
        yield

    def up_weight_rows():
        for lyr in range(DEPTH):
            for cc in range(NFC):
                upo_ref[lyr, cc, :, 0:FC] = upa_ref[lyr, :, cc * FC:(cc + 1) * FC].astype(BF16)
                upo_ref[lyr, cc, :, FC:2 * FC] = upg_ref[lyr, :, cc * FC:(cc + 1) * FC].astype(BF16)
                yield

    def forward_chains(chunk_of, o_bwd, fwd_states):
        done, o_fwd = set(), {}
        chains = [retention(ks, 0, state, o_fwd) for ks, state in fwd_states]
        for k in range(group):
            chains += [attention(k, chunk_of(k), done), gating_unit(k, done), pooling(k, chunk_of(k), done),
                       retention_mix(k, o_fwd, o_bwd, done), out_projection(k, done)]
        return chains

    if not latent:
        sb_scr[...] = jnp.zeros((CTX_SEQS, GW, GW), F32)
        sf_scr[...] = jnp.zeros((CTX_SEQS, GW, GW), F32)
        seq_chunks = [range(sq * cps, (sq + 1) * cps) for sq in range(CTX_SEQS)]
        o_bwd = {}
        side_chains = [retention(reversed(ks), 1, sb_scr.at[sq], o_bwd) for sq, ks in enumerate(seq_chunks)]
        if cast_up:
            side_chains.append(up_weight_rows())
        _interleave(side_chains + forward_chains(lambda k: k % cps, o_bwd,
                                                 [(ks, sf_scr.at[sq]) for sq, ks in enumerate(seq_chunks)]))
        for sq in range(CTX_SEQS):
            kn_ref[sq] = ka_ref[sq * SEQ:(sq + 1) * SEQ, :]
            vn_ref[sq] = va_ref[sq * SEQ:(sq + 1) * SEQ, :]
            _store_state(st_ref.at[sq], 1, sb_scr.at[sq])
            _store_state(st_ref.at[sq], 0, sf_scr.at[sq])
        return

    p = pl.program_id(1)
    g = pl.program_id(2)

    @pl.when(p == 0)
    def _():
        @pl.when(g == 0)
        def _():
            _load_state(sb_scr, s0_ref, 1)
            zero_blk = jnp.zeros((CHUNK, 2 * HEAD_DIM), BF16)
            for scr in (k_scr, v_scr):
                scr[0:CHUNK, :] = zero_blk
                scr[(nc + 1) * CHUNK:(nc + 2) * CHUNK, :] = zero_blk

        first = (ng - 1 - g) * group
        o_bwd = {}
        _interleave([retention(reversed(range(group)), 1, sb_scr, o_bwd)]
                    + [stage_kv(k, first + k) for k in range(group)])
        for k in range(group):
            ob_scr[pl.ds(_chunk_off(first + k), CHUNK), :] = o_bwd[k]

    @pl.when(p == 1)
    def _():
        @pl.when(g == 0)
        def _():
            _load_state(sf_scr, s0_ref, 0)

        first = g * group
        o_bwd = {k: ob_scr[pl.ds(_chunk_off(first + k), CHUNK), :] for k in range(group)}
        _interleave(forward_chains(lambda k: first + k, o_bwd, [(range(group), sf_scr)]))


def _mixer(z, x_res, mod, w_out_bf16, ln8, tabs, l, latent, extra=None, ffn_up=None):
    nb = DEC_BATCH if latent else BATCH // CTX_SEQS
    nc = (DEC_SEQ if latent else CTX_SEQS * SEQ) // CHUNK
    group = LAT_GROUP if latent else nc
    ng = nc // group
    blk = group * CHUNK
    base = (N_CTX // blk) if latent else 0
    per8 = blk // POOL_HALO
    last_halo = ROWS // POOL_HALO - 1

    def on_grid(f):
        return (lambda b, p, g: f(b, p, g)) if latent else (lambda b: f(b, 1, 0))

    def fwd(b, p, g):
        return base + b * ng + g * p

    def both(b, p, g):
        return base + b * ng + jnp.where(p == 0, ng - 1 - g, g)

    def bwd_only(b, p, g):
        return base + b * ng + (ng - 1 - g) * (1 - p)

    def col(width, idx, rowmap):
        return pl.BlockSpec((blk, width), on_grid(lambda b, p, g: (rowmap(b, p, g), idx)))

    def const(shape):
        return pl.BlockSpec(shape, on_grid(lambda b, p, g: (0,) * len(shape)))

    def layer(shape):
        return pl.BlockSpec((None,) + shape, on_grid(lambda b, p, g: (l,) + (0,) * len(shape)))

    specs, args = [], []

    def add(spec, arr):
        specs.append(spec)
        args.append(arr)

    add(col(GW, 0, fwd), z)
    add(col(128, 2, bwd_only if latent else fwd), z)
    add(col(128, 3, bwd_only if latent else fwd), z)
    if latent:
        add(pl.BlockSpec((None, None, PAST_LEN, 128), lambda b, p, g: (b, l, 0, 0)), extra["ck"])
        add(pl.BlockSpec((None, None, PAST_LEN, 128), lambda b, p, g: (b, l, 0, 0)), extra["cv"])
    add(col(GW, 2, fwd), z)
    add(col(GW, 3, fwd), z)
    add(col(GW, 4, both), z)
    add(col(GW, 5, both), z)
    add(col(GW, 6, both), z)
    add(col(GW, 7, fwd), z)
    add(col(GW, 8, fwd), z)
    add(col(GW, 9, fwd), z)
    if latent:
        add(pl.BlockSpec((POOL_HALO, GW),
                         lambda b, p, g: (jnp.maximum(fwd(b, p, g) * per8 - 1, 0), 9)), z)
        add(pl.BlockSpec((POOL_HALO, GW),
                         lambda b, p, g: (jnp.minimum((fwd(b, p, g) + 1) * per8, last_halo), 9)), z)
        rope_map = lambda b, p, g: (jnp.where(p == 0, ng - 1 - g, g), 0)
        add(pl.BlockSpec((blk, 128), rope_map), extra["cos"])
        add(pl.BlockSpec((blk, 128), rope_map), extra["sin"])
        add(pl.BlockSpec((None, None, 2, N_HEADS, HEAD_DIM, HEAD_DIM),
                         lambda b, p, g: (b, l, 0, 0, 0, 0)), extra["s0"])
    add(layer((2, N_HEADS * CHUNK, CHUNK)), tabs["dmat"])
    add(layer((4, CHUNK, GW)), tabs["dec"])
    add(layer((N_HEADS * CHUNK, 128)), tabs["sink"])
    add(layer((CHUNK, N_HEADS * CHUNK)), tabs["ws"])
    add(layer((CHUNK, GW)), tabs["bias"])
    add(layer((8, GW)), tabs["vec"])
    add(const((GW, GW)), tabs["gmat"])
    add(pl.BlockSpec((blk, GW), on_grid(lambda b, p, g: (g * p, 0))),
        tabs["cnt_lat"] if latent else tabs["cnt_ctx"])
    add(layer((GW, GW)), tabs["wpool"])
    local = lambda b, p, g: (b * ng + g * p, 0)
    add(pl.BlockSpec((blk, D_MODEL), on_grid(local)), x_res)
    add(pl.BlockSpec((None, None, 6, D_MODEL),
                     on_grid(lambda b, p, g: (l, (1 + b) if latent else 0, 0, 0))), mod)
    add(layer((D_MODEL, D_MODEL)), w_out_bf16)
    add(layer((8, D_MODEL)), ln8)
    up_rows = D_MODEL // nb
    if ffn_up is not None:
        add(pl.BlockSpec((DEPTH, up_rows, D_FF), lambda b: (0, b, 0)), ffn_up)
        add(pl.BlockSpec((DEPTH, up_rows, D_FF), lambda b: (0, b, 1)), ffn_up)

    out_shape = [jax.ShapeDtypeStruct((nb * nc * CHUNK, D_MODEL), F32)]
    out_specs = [pl.BlockSpec((blk, D_MODEL), on_grid(local))]
    state_shape = (GW, GW) if latent else (CTX_SEQS, GW, GW)
    scratch = [pltpu.VMEM(state_shape, F32), pltpu.VMEM(state_shape, F32),
               pltpu.VMEM((nc * CHUNK, GW), F32),
               pltpu.VMEM((group, CHUNK + 2 * POOL_HALO, GW), F32),
               pltpu.VMEM((blk, D_MODEL), F32)]
    if latent:
        scratch += [pltpu.VMEM(((nc + 2) * CHUNK, 128), BF16), pltpu.VMEM(((nc + 2) * CHUNK, 128), BF16)]
    else:
        out_shape.append(jax.ShapeDtypeStruct((BATCH, 2, N_HEADS, HEAD_DIM, HEAD_DIM), F32))
        out_specs.append(pl.BlockSpec((CTX_SEQS, 2, N_HEADS, HEAD_DIM, HEAD_DIM), lambda b: (b, 0, 0, 0, 0)))
        for _ in range(2):
            out_shape.append(jax.ShapeDtypeStruct((BATCH, SEQ, 128), F32))
            out_specs.append(pl.BlockSpec((CTX_SEQS, SEQ, 128), lambda b: (b, 0, 0)))
        if ffn_up is not None:
            out_shape.append(jax.ShapeDtypeStruct((DEPTH, NFC, D_MODEL, 2 * FC), BF16))
            out_specs.append(pl.BlockSpec((DEPTH, NFC, up_rows, 2 * FC), lambda b: (0, 0, b, 0)))

    return pl.pallas_call(
        functools.partial(_mixer_kernel, latent=latent, group=group, ng=ng, cast_up=ffn_up is not None),
        grid=(nb, 2, ng) if latent else (nb,),
        in_specs=specs,
        out_specs=out_specs,
        out_shape=out_shape,
        scratch_shapes=scratch,
        compiler_params=pltpu.CompilerParams(
            dimension_semantics=("arbitrary",) * (3 if latent else 1), vmem_limit_bytes=VMEM_LIMIT),
        name="mixer_latent" if latent else "mixer_context",
    )(*args)


def _pad_rows(rows, n=8):
    a = jnp.stack(rows)
    return jnp.concatenate([a, jnp.zeros((n - a.shape[0],) + a.shape[1:], a.dtype)], axis=0)


def _block_diag(blocks):
    g, n, _ = blocks.shape
    eye = jnp.eye(g, dtype=blocks.dtype)
    return (eye[:, None, :, None] * blocks[:, :, None, :]).reshape(g * n, g * n)


def _inv_count(n, seqs=1):
    t = np.arange(n)
    cols = []
    for w in POOL_WINDOWS:
        cnt = np.clip(t + w // 2, 0, n) - np.clip(t - w // 2, 0, n)
        cols.append(np.repeat((1.0 / cnt)[:, None], HEAD_DIM, axis=1))
    return jnp.asarray(np.tile(np.concatenate(cols, axis=1), (seqs, 1)), F32)


def _rope_tables():
    rows = DEC_SEQ // GRID_W
    r, cc = jnp.meshgrid(jnp.arange(rows), jnp.arange(GRID_W), indexing="ij")
    half = HEAD_DIM // 2
    freqs = ROPE_BASE ** (-jnp.arange(0, half, 2, dtype=F32) / half)

    def tables(pos):
        ang = pos.reshape(-1).astype(F32)[:, None] * freqs[None, :]
        cos, sin = jnp.cos(ang), jnp.sin(ang)
        return jnp.concatenate([cos, cos], axis=1), jnp.concatenate([-sin, sin], axis=1)

    cr, sr = tables(r)
    ccol, scol = tables(cc)
    cos = jnp.concatenate([cr, ccol], axis=1)
    sin = jnp.concatenate([sr, scol], axis=1)
    return jnp.tile(cos, (1, 2)), jnp.tile(sin, (1, 2))


def _layer_tables(attn_sink, sgu_norm_w, sgu_norm_b, sgu_ws, sgu_bs, ret_decay, ret_gn_w, pool_w, pool_scale):
    log_g = jax.nn.log_sigmoid(ret_decay.astype(F32))
    i = jnp.arange(CHUNK, dtype=F32)
    rel = i[:, None] - i[None, :]
    kscale = HEAD_DIM ** -0.5
    d_f = jnp.where(rel >= 0, jnp.exp(jnp.maximum(rel, 0.0)[None] * log_g[0][:, None, None]), 0.0)
    d_b = jnp.where(rel <= 0, jnp.exp(jnp.maximum(-rel, 0.0)[None] * log_g[1][:, None, None]), 0.0)
    dmat = jnp.stack([d_f.reshape(N_HEADS * CHUNK, CHUNK), d_b.reshape(N_HEADS * CHUNK, CHUNK)]) * kscale

    def lanes(per_head):
        return jnp.repeat(per_head, HEAD_DIM, axis=1)

    qdec_f = lanes(jnp.exp((i + 1.0)[:, None] * log_g[0][None, :]))
    qdec_b = lanes(jnp.exp((CHUNK - i)[:, None] * log_g[1][None, :]))
    kdec_f = lanes(jnp.exp((CHUNK - 1.0 - i)[:, None] * log_g[0][None, :])) * kscale
    kdec_b = lanes(jnp.exp(i[:, None] * log_g[1][None, :])) * kscale
    cdec = jnp.repeat(jnp.exp(CHUNK * log_g), HEAD_DIM, axis=1)
    vec = _pad_rows([sgu_norm_w, sgu_norm_b, pool_scale, ret_gn_w[0], ret_gn_w[1], cdec[0], cdec[1]])
    return {
        "dmat": dmat,
        "dec": jnp.stack([qdec_f, qdec_b, kdec_f, kdec_b]),
        "sink": jnp.broadcast_to(jnp.repeat(attn_sink, CHUNK)[:, None], (N_HEADS * CHUNK, 128)),
        "ws": jnp.concatenate([sgu_ws[h] for h in range(N_HEADS)], axis=1).astype(BF16),
        "bias": jnp.repeat(sgu_bs.T, HEAD_DIM, axis=1),
        "vec": vec,
        "wpool": _block_diag(pool_w).astype(BF16),
    }


def kernel(x_prompt, x_sample, cache_attn_k, cache_attn_v, state_ret, c, c_ctx, w_ada, b_ada, w_in,
           w_out, attn_sink, sgu_norm_w, sgu_norm_b, sgu_ws, sgu_bs, ret_decay, ret_gn_w, pool_w,
           pool_scale, ffn_up, ffn_conv_w, ffn_conv_b, ffn_down, ln_w, ln_b):
    cond8 = jnp.concatenate([c_ctx[None], c, jnp.zeros((8 - 1 - DEC_BATCH, D_MODEL), F32)], axis=0)
    mod = _modulation(cond8, w_ada, b_ada).reshape(DEPTH, 8, 6, D_MODEL)

    tabs = jax.vmap(_layer_tables)(attn_sink, sgu_norm_w, sgu_norm_b, sgu_ws, sgu_bs, ret_decay, ret_gn_w,
                                   pool_w, pool_scale)
    tabs["gmat"] = _block_diag(jnp.full((N_HEADS, HEAD_DIM, HEAD_DIM), 1.0 / HEAD_DIM, F32)).astype(BF16)
    tabs["cnt_ctx"] = _inv_count(SEQ, CTX_SEQS)
    tabs["cnt_lat"] = _inv_count(DEC_SEQ)
    cos, sin = _rope_tables()
    extra = {"ck": cache_attn_k.reshape(DEC_BATCH, DEPTH, PAST_LEN, 128),
             "cv": cache_attn_v.reshape(DEC_BATCH, DEPTH, PAST_LEN, 128),
             "cos": cos, "sin": sin, "s0": state_ret}
    ln8 = jnp.concatenate([ln_w, ln_b, jnp.zeros((DEPTH, 4, D_MODEL), F32)], axis=1)
    conv8 = jnp.concatenate([ffn_conv_w, ffn_conv_b[:, None], jnp.zeros((DEPTH, 4, 2 * D_FF), F32)], axis=1)

    w_out_bf16 = _to_bf16(w_out, D_MODEL)

    xs = [x_prompt.reshape(N_CTX, D_MODEL), x_sample.reshape(N_LAT, D_MODEL)]
    new_k, new_v, new_s = [], [], []
    for l in range(DEPTH):
        z = _inproj(xs, mod, w_in, l)
        if l == 0:
            x1_ctx, st, kn, vn, up_chunks = _mixer(z, xs[0], mod, w_out_bf16, ln8, tabs, l, latent=False,
                                                   ffn_up=ffn_up)
        else:
            x1_ctx, st, kn, vn = _mixer(z, xs[0], mod, w_out_bf16, ln8, tabs, l, latent=False)
        (x1_lat,) = _mixer(z, xs[1], mod, w_out_bf16, ln8, tabs, l, latent=True, extra=extra)
        xs = _ffn([x1_ctx, x1_lat], mod, up_chunks, conv8, ffn_down, ln8, l)
        new_k.append(kn.reshape(BATCH, SEQ, 2, HEAD_DIM))
        new_v.append(vn.reshape(BATCH, SEQ, 2, HEAD_DIM))
        new_s.append(st)

    y_prompt = xs[0].reshape(BATCH, SEQ, D_MODEL)
    y_sample = xs[1].reshape(DEC_BATCH, DEC_SEQ, D_MODEL)
    return (y_prompt, y_sample, jnp.stack(new_k, axis=1), jnp.stack(new_v, axis=1),
            jnp.stack(new_s, axis=1))
```

```python
import functools

import numpy as np
import jax
import jax.numpy as jnp
from jax import lax
from jax.experimental import pallas as pl
from jax.experimental.pallas import tpu as pltpu

F32 = jnp.float32
BF16 = jnp.bfloat16

D_MODEL = 1024
BATCH = 16
SEQ = 256
DEPTH = 2
DEC_BATCH = 2
DEC_SEQ = 2048
PAST_LEN = 256
GRID_W = 64
CHUNK = 128
HEAD_DIM = 64
GW = D_MODEL // 4
N_HEADS = 4
POOL_WINDOWS = (2, 4, 8, 16)
POOL_HALO = 8
D_FF = 2816
ROPE_BASE = 10000.0
LN_EPS = 1e-5
NEG_INF = -1e30
IN_WIDTH = 10 * GW
ALPHA = (2.0 * DEPTH) ** 0.25

N_CTX = BATCH * SEQ
N_LAT = DEC_BATCH * DEC_SEQ
ROWS = N_CTX + N_LAT

TM = 1024
NB_IN = 512
FC = 256
NB_ADA = 1536
VMEM_LIMIT = 56 * 1024 * 1024


def _cond_of_tile(i, tm=TM):
    ctx_tiles = N_CTX // tm
    return jnp.where(i < ctx_tiles, 0, 1 + (i - ctx_tiles) // (DEC_SEQ // tm))


def _tile_specs(tm):
    ctx_tiles = N_CTX // tm
    return [pl.BlockSpec((tm, D_MODEL), lambda i: (jnp.minimum(i, ctx_tiles - 1), 0)),
            pl.BlockSpec((tm, D_MODEL), lambda i: (jnp.maximum(i - ctx_tiles, 0), 0))]


def _per_half(i, tm, fn):
    ctx_tiles = N_CTX // tm

    @pl.when(i < ctx_tiles)
    def _():
        fn(0)

    @pl.when(i >= ctx_tiles)
    def _():
        fn(1)


def _layer_norm(x, w, b):
    mu = jnp.mean(x, axis=-1, keepdims=True)
    d = x - mu
    var = jnp.mean(d * d, axis=-1, keepdims=True)
    return d * lax.rsqrt(var + LN_EPS) * w + b


def _silu(x):
    return x * jax.nn.sigmoid(x)


def _bdot(a, b):
    return jnp.dot(a.astype(BF16), b.astype(BF16), preferred_element_type=F32)


def _mod_kernel(c_ref, w_ref, b_ref, o_ref):
    o_ref[...] = _bdot(_silu(c_ref[...]), w_ref[...]) + b_ref[...]


def _modulation(cond8, w_ada, b_ada):
    return pl.pallas_call(
        _mod_kernel,
        grid=(DEPTH, 6 * D_MODEL // NB_ADA),
        in_specs=[
            pl.BlockSpec((8, D_MODEL), lambda l, j: (0, 0)),
            pl.BlockSpec((None, D_MODEL, NB_ADA), lambda l, j: (l, 0, j)),
            pl.BlockSpec((None, 1, NB_ADA), lambda l, j: (l, 0, j)),
        ],
        out_specs=pl.BlockSpec((None, 8, NB_ADA), lambda l, j: (l, 0, j)),
        out_shape=jax.ShapeDtypeStruct((DEPTH, 8, 6 * D_MODEL), F32),
        compiler_params=pltpu.CompilerParams(
            dimension_semantics=("arbitrary", "arbitrary"), vmem_limit_bytes=VMEM_LIMIT),
        name="modulation",
    )(cond8, w_ada, b_ada.reshape(DEPTH, 1, 6 * D_MODEL))


def _inproj_kernel(*refs, cast_w_out):
    if cast_w_out:
        xc_ref, xl_ref, mod_ref, w_ref, wo_ref, z_ref, wo_bf16_ref, h_scr = refs
    else:
        xc_ref, xl_ref, mod_ref, w_ref, z_ref, h_scr = refs
    x_refs = (xc_ref, xl_ref)
    m = mod_ref[...]

    def project(side):
        for rh in range(2):
            rows = slice(rh * TM // 2, (rh + 1) * TM // 2)
            h_scr[rows, :] = (x_refs[side][rows, :] * (1.0 + m[1:2]) + m[0:1]).astype(BF16)
            for jb in range(IN_WIDTH // NB_IN):
                cols = slice(jb * NB_IN, (jb + 1) * NB_IN)
                z_ref[rows, cols] = jnp.dot(h_scr[rows, :], w_ref[:, cols].astype(BF16),
                                            preferred_element_type=F32)
            if cast_w_out and rh == 0:
                wo_bf16_ref[...] = wo_ref[...].astype(BF16)

    _per_half(pl.program_id(0), TM, project)


def _inproj(xs, mod, w_in, l, w_out=None):
    steps = ROWS // TM
    in_specs = _tile_specs(TM) + [
        pl.BlockSpec((None, None, 6, D_MODEL), lambda i: (l, _cond_of_tile(i), 0, 0)),
        pl.BlockSpec((None, D_MODEL, IN_WIDTH), lambda i: (l, 0, 0), pipeline_mode=pl.Buffered(1)),
    ]
    out_specs = [pl.BlockSpec((TM, IN_WIDTH), lambda i: (i, 0))]
    out_shape = [jax.ShapeDtypeStruct((ROWS, IN_WIDTH), F32)]
    args = [*xs, mod, w_in]
    if w_out is not None:
        rows_spec = pl.BlockSpec((DEPTH, D_MODEL // steps, D_MODEL), lambda i: (0, i, 0))
        in_specs.append(rows_spec)
        out_specs.append(rows_spec)
        out_shape.append(jax.ShapeDtypeStruct((DEPTH, D_MODEL, D_MODEL), BF16))
        args.append(w_out)
    return pl.pallas_call(
        functools.partial(_inproj_kernel, cast_w_out=w_out is not None),
        grid=(steps,),
        in_specs=in_specs,
        out_specs=out_specs,
        out_shape=out_shape,
        scratch_shapes=[pltpu.VMEM((TM, D_MODEL), BF16)],
        compiler_params=pltpu.CompilerParams(
            dimension_semantics=("arbitrary",), vmem_limit_bytes=VMEM_LIMIT),
        name="inproj",
    )(*args)


TMF = 512
SEGF = TMF // 8
PITCHF = SEGF + 8
CTXF_TILES = N_CTX // TMF
LATF_PER_SEQ = DEC_SEQ // TMF
HALO_ROWS = 16
RBUF = 256
RB = 256
GB = 64
NFC = D_FF // FC
LANE_BLOCKS = D_MODEL // 128
assert RB == RBUF


def _seg_rows_f(xc_ref, k):
    return jnp.concatenate([xc_ref[cb, pl.ds(k, 8, stride=PITCHF), :] for cb in range(LANE_BLOCKS)], axis=1)


def _ffn_kernel(xc_ref, xl_ref, xp_ref, xn_ref, mod_ref, up_ref, cv_ref, dn_ref, ln_ref, oc_ref, ol_ref,
                h_scr, act_scr, xc_scr, u0_scr, u1_scr):
    x_refs = (xc_ref, xl_ref)
    o_refs = (oc_ref, ol_ref)
    u_scrs = (u0_scr, u1_scr)
    i = pl.program_id(0)
    is_ctx = i < CTXF_TILES
    lat_pos = (i - CTXF_TILES) % LATF_PER_SEQ
    m = mod_ref[...]
    ln = ln_ref[...]

    scale = 1.0 + m[4:5]
    shift = m[3:4]

    def stage(side):
        for cb in range(LANE_BLOCKS):
            for s in range(8):
                xc_scr[cb, s * PITCHF:s * PITCHF + SEGF, :] = x_refs[side][s * SEGF:(s + 1) * SEGF,
                                                                            cb * 128:(cb + 1) * 128]

    _per_half(i, TMF, stage)

    def build_h(ub):
        for k in range(ub * RBUF // 8, (ub + 1) * RBUF // 8, 2):
            rows = jnp.concatenate([_seg_rows_f(xc_scr, k), _seg_rows_f(xc_scr, k + 1)], axis=0)
            h_scr[8 * k:8 * k + 16, :] = (rows * scale + shift).astype(BF16)

    sub16 = lax.broadcasted_iota(jnp.int32, (HALO_ROWS, D_MODEL), 0)
    prev_ok = jnp.logical_not(is_ctx) & (lat_pos > 0)
    next_ok = jnp.logical_not(is_ctx) & (lat_pos < LATF_PER_SEQ - 1)
    halo_x = jnp.where(sub16 == 0, xp_ref[POOL_HALO - 1:POOL_HALO, :], xn_ref[0:1, :])
    keep = ((sub16 == 0) & prev_ok) | ((sub16 == 1) & next_ok)
    h_scr[TMF:TMF + HALO_ROWS, :] = jnp.where(keep, halo_x * scale + shift, 0.0).astype(BF16)

    n_ub = TMF // RBUF
    sub = lax.broadcasted_iota(jnp.int32, (8, FC), 0)
    seg_per_seq = SEQ // SEGF
    ctx_first = is_ctx & (sub % seg_per_seq == 0)
    ctx_last = is_ctx & (sub % seg_per_seq == seg_per_seq - 1)

    def up_proj(slot, c, ub):
        rows = slice(ub * RBUF, (ub + 1) * RBUF + (HALO_ROWS if ub == n_ub - 1 else 0))
        u_scrs[slot][rows, :] = jnp.dot(h_scr[rows, :], up_ref[c], preferred_element_type=F32)

    def conv(u_ref, lanes, cvs, r0):
        lo = max(r0 - 8, 0)
        hi = min(r0 + GB + 8, TMF)
        ue = u_ref[lo:hi, lanes]
        u = ue[r0 - lo:r0 - lo + GB]
        if r0 == 0:
            b_first = jnp.where(sub == 0, u_ref[TMF:TMF + 1, lanes],
                                pltpu.roll(u_ref[TMF - 8:TMF, lanes], 1, axis=0))
            um1 = jnp.concatenate([jnp.where(ctx_first, 0.0, b_first), u[0:GB - 8]], axis=0)
        else:
            um1 = ue[0:GB]
        if r0 == TMF - GB:
            b_last = jnp.where(sub == 7, u_ref[TMF + 1:TMF + 2, lanes],
                               pltpu.roll(u_ref[0:8, lanes], 7, axis=0))
            up1 = jnp.concatenate([u[8:GB], jnp.where(ctx_last, 0.0, b_last)], axis=0)
        else:
            up1 = ue[r0 - lo + 8:r0 - lo + GB + 8]
        return um1 * cvs[0:1] + u * cvs[1:2] + up1 * cvs[2:3] + cvs[3:4]

    def gate(slot, c, ub):
        cva = cv_ref[0:4, c * FC:(c + 1) * FC]
        cvg = cv_ref[0:4, D_FF + c * FC:D_FF + (c + 1) * FC]
        for r0 in range(ub * RBUF, (ub + 1) * RBUF, GB):
            a = conv(u_scrs[slot], slice(0, FC), cva, r0)
            g = conv(u_scrs[slot], slice(FC, 2 * FC), cvg, r0)
            act_scr[c, r0:r0 + GB, :] = (_silu(a) * g).astype(BF16)

    def finish(rb):
        vrows = range(rb * RB // 8, (rb + 1) * RB // 8)
        lhs = jnp.concatenate([act_scr[c, rb * RB:(rb + 1) * RB, :] for c in range(NFC)], axis=1)
        y = jnp.dot(lhs, dn_ref[...].astype(BF16), preferred_element_type=F32)
        xr = jnp.concatenate([_seg_rows_f(xc_scr, k) for k in vrows], axis=0)
        out = _layer_norm(ALPHA * xr + m[5:6] * y, ln[1:2], ln[3:4])
        for kk, k in enumerate(vrows):
            for cb in range(LANE_BLOCKS):
                xc_scr[cb, pl.ds(k, 8, stride=PITCHF), :] = out[8 * kk:8 * kk + 8, cb * 128:(cb + 1) * 128]

    def write_out(side):
        for cb in range(LANE_BLOCKS):
            for s in range(8):
                o_refs[side][s * SEGF:(s + 1) * SEGF, cb * 128:(cb + 1) * 128] = xc_scr[
                    cb, s * PITCHF:s * PITCHF + SEGF, :]

    build_h(0)
    for c in range(NFC + 1):
        for ub in range(n_ub):
            if c == 0 and ub > 0:
                build_h(ub)
            if c < NFC:
                up_proj(c % 2, c, ub)
            if c >= 1:
                gate((c - 1) % 2, c - 1, ub)
            if c == NFC:
                finish(ub)

    _per_half(i, TMF, write_out)


def _ffn(xs, mod, up_chunks, conv8, down, ln8, l):
    halo_blocks = TMF // POOL_HALO
    last_halo = N_LAT // POOL_HALO - 1
    return pl.pallas_call(
        _ffn_kernel,
        grid=(ROWS // TMF,),
        in_specs=[
            *_tile_specs(TMF),
            pl.BlockSpec((POOL_HALO, D_MODEL),
                         lambda i: (jnp.maximum((i - CTXF_TILES) * halo_blocks - 1, 0), 0)),
            pl.BlockSpec((POOL_HALO, D_MODEL),
                         lambda i: (jnp.clip((i - CTXF_TILES + 1) * halo_blocks, 0, last_halo), 0)),
            pl.BlockSpec((None, None, 6, D_MODEL), lambda i: (l, _cond_of_tile(i, TMF), 0, 0)),
            pl.BlockSpec((None, NFC, D_MODEL, 2 * FC), lambda i: (l, 0, 0, 0), pipeline_mode=pl.Buffered(1)),
            pl.BlockSpec((None, 8, 2 * D_FF), lambda i: (l, 0, 0)),
            pl.BlockSpec((None, D_FF, D_MODEL), lambda i: (l, 0, 0), pipeline_mode=pl.Buffered(1)),
            pl.BlockSpec((None, 8, D_MODEL), lambda i: (l, 0, 0)),
        ],
        out_specs=_tile_specs(TMF),
        out_shape=[jax.ShapeDtypeStruct((N_CTX, D_MODEL), F32), jax.ShapeDtypeStruct((N_LAT, D_MODEL), F32)],
        scratch_shapes=[pltpu.VMEM((TMF + HALO_ROWS, D_MODEL), BF16),
                        pltpu.VMEM((NFC, TMF, FC), BF16),
                        pltpu.VMEM((LANE_BLOCKS, 8 * PITCHF, 128), F32),
                        pltpu.VMEM((TMF + HALO_ROWS, 2 * FC), F32),
                        pltpu.VMEM((TMF + HALO_ROWS, 2 * FC), F32)],
        compiler_params=pltpu.CompilerParams(
            dimension_semantics=("arbitrary",), vmem_limit_bytes=VMEM_LIMIT),
        name="convffn",
    )(xs[0], xs[1], xs[1], xs[1], mod, up_chunks, conv8, down, ln8)


def _head_masks(width):
    lane = lax.broadcasted_iota(jnp.int32, (1, width), 1)
    return [(lane >= h * HEAD_DIM) & (lane < (h + 1) * HEAD_DIM) for h in range(width // HEAD_DIM)]


def _stack_heads(x, masks):
    return jnp.concatenate([jnp.where(m, x, jnp.zeros_like(x)) for m in masks], axis=0)


def _rope(x, cos, sin):
    lane = lax.broadcasted_iota(jnp.int32, (1, 128), 1)
    lower = (lane & 31) < 16
    outs = []
    for k in range(x.shape[1] // 128):
        xb = x[:, k * 128:(k + 1) * 128]
        partner = jnp.where(lower, pltpu.roll(xb, 112, axis=1), pltpu.roll(xb, 16, axis=1))
        outs.append(xb * cos + partner * sin)
    return outs[0] if len(outs) == 1 else jnp.concatenate(outs, axis=1)


def _load_state(state_scr, blocks_ref, d):
    state_scr[...] = jnp.zeros((GW, GW), F32)
    for h in range(N_HEADS):
        sl = slice(h * HEAD_DIM, (h + 1) * HEAD_DIM)
        state_scr[sl, sl] = blocks_ref[d, h]


def _store_state(blocks_ref, d, state_scr):
    for h in range(N_HEADS):
        sl = slice(h * HEAD_DIM, (h + 1) * HEAD_DIM)
        blocks_ref[d, h] = state_scr[sl, sl]


LAT_GROUP = 4
CTX_SEQS = 2


def _chunk_off(c):
    return c * CHUNK if isinstance(c, int) else pl.multiple_of(c * CHUNK, CHUNK)


def _interleave(chains):
    chains = list(chains)
    while chains:
        for ch in list(chains):
            try:
                next(ch)
            except StopIteration:
                chains.remove(ch)


def _mixer_kernel(*refs, latent, group, ng, cast_up=False):
    if latent:
        (qa_ref, ka_ref, va_ref, kx_ref, vx_ref, ub_ref, vb_ref, qc_ref, kc_ref, vc_ref, gf_ref, gb_ref,
         pd_ref, pdp_ref, pdn_ref, cos_ref, sin_ref, s0_ref, dmat_ref, dec_ref, sink_ref, ws_ref, bias_ref,
         vec_ref, gmat_ref, cnt_ref, wpool_ref, xres_ref, mod_ref, wout_ref, ln_ref,
         x1_ref, sf_scr, sb_scr, ob_scr, pext_scr, mix_ref, k_scr, v_scr) = refs
    else:
        (qa_ref, ka_ref, va_ref, ub_ref, vb_ref, qc_ref, kc_ref, vc_ref, gf_ref, gb_ref, pd_ref,
         dmat_ref, dec_ref, sink_ref, ws_ref, bias_ref, vec_ref, gmat_ref, cnt_ref, wpool_ref,
         xres_ref, mod_ref, wout_ref, ln_ref) = refs[:24]
        if cast_up:
            upa_ref, upg_ref, x1_ref, st_ref, kn_ref, vn_ref, upo_ref = refs[24:31]
        else:
            x1_ref, st_ref, kn_ref, vn_ref = refs[24:28]
        sf_scr, sb_scr, ob_scr, pext_scr, mix_ref = refs[-5:]

    nc = group * ng
    cps = group if latent else SEQ // CHUNK
    g_a = mod_ref[...][2:3]
    ln = ln_ref[...]
    masks = _head_masks(GW)
    row = lax.broadcasted_iota(jnp.int32, (GW, GW), 0)
    col = lax.broadcasted_iota(jnp.int32, (GW, GW), 1)
    bd_mask = (row // HEAD_DIM) == (col // HEAD_DIM)
    vec = vec_ref[...]
    lane = lax.broadcasted_iota(jnp.int32, (1, 2 * HEAD_DIM), 1)
    lo = lane < HEAD_DIM

    def rows(k):
        return slice(k * CHUNK, (k + 1) * CHUNK)

    def retention(order, d, state_scr, out):
        dmat = dmat_ref[d]
        qdec, kdec = dec_ref[d], dec_ref[2 + d]
        cdec = vec[5 + d:6 + d]
        state = state_scr[...]
        for k in order:
            q, kk, v = qc_ref[rows(k), :], kc_ref[rows(k), :], vc_ref[rows(k), :]
            qb, kb, vb = q.astype(BF16), kk.astype(BF16), v.astype(BF16)
            s = lax.dot_general(_stack_heads(qb, masks), kb, (((1,), (1,)), ((), ())),
                                preferred_element_type=F32)
            yield
            p = (s * dmat).astype(BF16)
            p_cat = jnp.concatenate([p[h * CHUNK:(h + 1) * CHUNK] for h in range(N_HEADS)], axis=1)
            o = (jnp.dot(p_cat, _stack_heads(vb, masks), preferred_element_type=F32)
                 + _bdot(q * qdec, state))
            yield
            upd = lax.dot_general((kk * kdec).astype(BF16), vb, (((0,), (0,)), ((), ())),
                                  preferred_element_type=F32)
            state = state * cdec + jnp.where(bd_mask, upd, 0.0)
            out[k] = o
            yield
        state_scr[...] = state

    def stage_kv(k, c):
        dst = pl.ds(_chunk_off(c + 1), CHUNK)
        k_scr[dst, :] = _rope(ka_ref[rows(k), :], cos_ref[rows(k), :], sin_ref[rows(k), :]).astype(BF16)
        yield
        v_scr[dst, :] = va_ref[rows(k), :].astype(BF16)
        yield

    def attention(k, c, done):
        q = qa_ref[rows(k), :]
        if latent:
            q = _rope(q, cos_ref[rows(k), :], sin_ref[rows(k), :])
        q = q * (HEAD_DIM ** -0.5)
        q0, q1 = q[:, :128], q[:, 128:]
        zero = jnp.zeros_like(q0)
        q_st = jnp.concatenate([
            jnp.where(lo, q0, zero),
            jnp.where(lo, pltpu.roll(q0, HEAD_DIM, axis=1), zero),
            jnp.where(lo, zero, pltpu.roll(q1, HEAD_DIM, axis=1)),
            jnp.where(lo, zero, q1)], axis=0).astype(BF16)
        if latent:
            band = pl.ds(_chunk_off(c), 3 * CHUNK)
            k_all = jnp.concatenate([k_scr[band, :], kx_ref[...].astype(BF16)], axis=0)
            v_all = jnp.concatenate([v_scr[band, :], vx_ref[...].astype(BF16)], axis=0)
        else:
            own = slice(k // cps * SEQ, (k // cps + 1) * SEQ)
            k_all = ka_ref[own, :].astype(BF16)
            v_all = va_ref[own, :].astype(BF16)
        s = lax.dot_general(q_st, k_all, (((1,), (1,)), ((), ())), preferred_element_type=F32)
        yield
        if latent:
            nk = 3 * CHUNK + PAST_LEN
            qi = lax.broadcasted_iota(jnp.int32, (N_HEADS * CHUNK, nk), 0) & (CHUNK - 1)
            kj = lax.broadcasted_iota(jnp.int32, (N_HEADS * CHUNK, nk), 1)
            kpos = kj + (c - 1) * CHUNK
            valid = (kj >= 3 * CHUNK) | ((kj >= qi) & (kj <= qi + 2 * CHUNK)
                                         & (kpos >= 0) & (kpos < nc * CHUNK))
            s = jnp.where(valid, s, NEG_INF)
        sink = sink_ref[...][:, 0:1]
        mx = jnp.maximum(jnp.max(s, axis=-1, keepdims=True), sink)
        yield
        e = jnp.exp(s - mx)
        den = jnp.sum(e, axis=-1, keepdims=True) + jnp.exp(sink - mx)
        yield
        o = jnp.dot(e.astype(BF16), v_all, preferred_element_type=F32) / den
        yield
        mix_ref[rows(k), 0:128] = jnp.where(lo, o[0:CHUNK], pltpu.roll(o[CHUNK:2 * CHUNK], HEAD_DIM, axis=1))
        mix_ref[rows(k), 128:256] = jnp.where(lo, pltpu.roll(o[2 * CHUNK:3 * CHUNK], HEAD_DIM, axis=1),
                                              o[3 * CHUNK:4 * CHUNK])
        done.add(("A", k))
        yield

    def gating_unit(k, done):
        vn = _layer_norm(vb_ref[rows(k), :], vec[0:1], vec[1:2]).astype(BF16)
        yield
        sg = jnp.dot(ws_ref[...], _stack_heads(vn, masks), preferred_element_type=F32) + bias_ref[...]
        yield
        mix_ref[rows(k), GW:2 * GW] = ub_ref[rows(k), :] * sg
        done.add(("B", k))
        yield

    def retention_mix(k, o_fwd, o_bwd, done):
        while k not in o_fwd or k not in o_bwd:
            yield
        gmat = gmat_ref[...]
        normed = []
        for o in (o_fwd[k], o_bwd[k]):
            mu = jnp.dot(o.astype(BF16), gmat, preferred_element_type=F32)
            yield
            dlt = o - mu
            var = jnp.dot((dlt * dlt).astype(BF16), gmat, preferred_element_type=F32)
            yield
            normed.append(dlt * lax.rsqrt(var + LN_EPS))
        mix_ref[rows(k), 2 * GW:3 * GW] = (_silu(gf_ref[rows(k), :]) * (normed[0] * vec[3:4])
                                           + _silu(gb_ref[rows(k), :]) * (normed[1] * vec[4:5]))
        done.add(("C", k))
        yield

    def pooling(k, c, done):
        pd = pd_ref[rows(k), :]
        pext = pext_scr.at[k]
        zeros8 = jnp.zeros((POOL_HALO, GW), F32)
        if k % cps > 0:
            prev8 = pd_ref[k * CHUNK - POOL_HALO:k * CHUNK, :]
        else:
            prev8 = jnp.where(c > 0, pdp_ref[...], 0.0) if latent else zeros8
        if k % cps < cps - 1:
            next8 = pd_ref[(k + 1) * CHUNK:(k + 1) * CHUNK + POOL_HALO, :]
        else:
            next8 = jnp.where(c < nc - 1, pdn_ref[...], 0.0) if latent else zeros8
        pext[0:POOL_HALO, :] = prev8
        pext[POOL_HALO:POOL_HALO + CHUNK, :] = pd
        pext[POOL_HALO + CHUNK:2 * POOL_HALO + CHUNK, :] = next8
        yield

        def win(d, half):
            return pext[pl.ds(POOL_HALO + d, CHUNK), half * 128:(half + 1) * 128]

        a2 = win(-1, 0) + win(0, 0)
        a4 = a2 + win(-2, 0) + win(1, 0)
        yield
        a8 = win(-4, 1)
        for d in range(-3, 4):
            a8 = a8 + win(d, 1)
        yield
        a16 = a8
        for d in list(range(-8, -4)) + list(range(4, 8)):
            a16 = a16 + win(d, 1)
        yield
        sums = jnp.concatenate([jnp.where(lo, a2, a4), jnp.where(lo, a8, a16)], axis=1)
        yd = sums * cnt_ref[rows(k), :] - pd
        mix_ref[rows(k), 3 * GW:4 * GW] = _bdot(yd, wpool_ref[...]) * vec[2:3]
        done.add(("D", k))
        yield

    def out_projection(k, done):
        while not all((m, k) in done for m in "ABCD"):
            yield
        y = jnp.dot(mix_ref[rows(k), :].astype(BF16), wout_ref[...], preferred_element_type=F32)
        yield
        x1_ref[rows(k), :] = _layer_norm(ALPHA * xres_ref[rows(k), :] + g_a * y, ln[0:1], ln[2:3])
        yield

    def up_weight_rows():
        for lyr in range(DEPTH):
            for cc in range(NFC):
                upo_ref[lyr, cc, :, 0:FC] = upa_ref[lyr, :, cc * FC:(cc + 1) * FC].astype(BF16)
                upo_ref[lyr, cc, :, FC:2 * FC] = upg_ref[lyr, :, cc * FC:(cc + 1) * FC].astype(BF16)
                yield

    def forward_chains(chunk_of, o_bwd, fwd_states):
        done, o_fwd = set(), {}
        chains = [retention(ks, 0, state, o_fwd) for ks, state in fwd_states]
        for k in range(group):
            chains += [attention(k, chunk_of(k), done), gating_unit(k, done), pooling(k, chunk_of(k), done),
                       retention_mix(k, o_fwd, o_bwd, done), out_projection(k, done)]
        return chains

    if not latent:
        sb_scr[...] = jnp.zeros((CTX_SEQS, GW, GW), F32)
        sf_scr[...] = jnp.zeros((CTX_SEQS, GW, GW), F32)
        seq_chunks = [range(sq * cps, (sq + 1) * cps) for sq in range(CTX_SEQS)]
        o_bwd = {}
        side_chains = [retention(reversed(ks), 1, sb_scr.at[sq], o_bwd) for sq, ks in enumerate(seq_chunks)]
        if cast_up:
            side_chains.append(up_weight_rows())
        _interleave(side_chains + forward_chains(lambda k: k % cps, o_bwd,
                                                 [(ks, sf_scr.at[sq]) for sq, ks in enumerate(seq_chunks)]))
        for sq in range(CTX_SEQS):
            kn_ref[sq] = ka_ref[sq * SEQ:(sq + 1) * SEQ, :]
            vn_ref[sq] = va_ref[sq * SEQ:(sq + 1) * SEQ, :]
            _store_state(st_ref.at[sq], 1, sb_scr.at[sq])
            _store_state(st_ref.at[sq], 0, sf_scr.at[sq])
        return

    p = pl.program_id(1)
    g = pl.program_id(2)

    @pl.when(p == 0)
    def _():
        @pl.when(g == 0)
        def _():
            _load_state(sb_scr, s0_ref, 1)
            zero_blk = jnp.zeros((CHUNK, 2 * HEAD_DIM), BF16)
            for scr in (k_scr, v_scr):
                scr[0:CHUNK, :] = zero_blk
                scr[(nc + 1) * CHUNK:(nc + 2) * CHUNK, :] = zero_blk

        first = (ng - 1 - g) * group
        o_bwd = {}
        _interleave([retention(reversed(range(group)), 1, sb_scr, o_bwd)]
                    + [stage_kv(k, first + k) for k in range(group)])
        for k in range(group):
            ob_scr[pl.ds(_chunk_off(first + k), CHUNK), :] = o_bwd[k]

    @pl.when(p == 1)
    def _():
        @pl.when(g == 0)
        def _():
            _load_state(sf_scr, s0_ref, 0)

        first = g * group
        o_bwd = {k: ob_scr[pl.ds(_chunk_off(first + k), CHUNK), :] for k in range(group)}
        _interleave(forward_chains(lambda k: first + k, o_bwd, [(range(group), sf_scr)]))


def _mixer(z, x_res, mod, w_out_bf16, ln8, tabs, l, latent, extra=None, ffn_up=None):
    nb = DEC_BATCH if latent else BATCH // CTX_SEQS
    nc = (DEC_SEQ if latent else CTX_SEQS * SEQ) // CHUNK
    group = LAT_GROUP if latent else nc
    ng = nc // group
    blk = group * CHUNK
    base = (N_CTX // blk) if latent else 0
    per8 = blk // POOL_HALO
    last_halo = ROWS // POOL_HALO - 1

    def on_grid(f):
        return (lambda b, p, g: f(b, p, g)) if latent else (lambda b: f(b, 1, 0))

    def fwd(b, p, g):
        return base + b * ng + g * p

    def both(b, p, g):
        return base + b * ng + jnp.where(p == 0, ng - 1 - g, g)

    def bwd_only(b, p, g):
        return base + b * ng + (ng - 1 - g) * (1 - p)

    def col(width, idx, rowmap):
        return pl.BlockSpec((blk, width), on_grid(lambda b, p, g: (rowmap(b, p, g), idx)))

    def const(shape):
        return pl.BlockSpec(shape, on_grid(lambda b, p, g: (0,) * len(shape)))

    def layer(shape):
        return pl.BlockSpec((None,) + shape, on_grid(lambda b, p, g: (l,) + (0,) * len(shape)))

    specs, args = [], []

    def add(spec, arr):
        specs.append(spec)
        args.append(arr)

    add(col(GW, 0, fwd), z)
    add(col(128, 2, bwd_only if latent else fwd), z)
    add(col(128, 3, bwd_only if latent else fwd), z)
    if latent:
        add(pl.BlockSpec((None, None, PAST_LEN, 128), lambda b, p, g: (b, l, 0, 0)), extra["ck"])
        add(pl.BlockSpec((None, None, PAST_LEN, 128), lambda b, p, g: (b, l, 0, 0)), extra["cv"])
    add(col(GW, 2, fwd), z)
    add(col(GW, 3, fwd), z)
    add(col(GW, 4, both), z)
    add(col(GW, 5, both), z)
    add(col(GW, 6, both), z)
    add(col(GW, 7, fwd), z)
    add(col(GW, 8, fwd), z)
    add(col(GW, 9, fwd), z)
    if latent:
        add(pl.BlockSpec((POOL_HALO, GW),
                         lambda b, p, g: (jnp.maximum(fwd(b, p, g) * per8 - 1, 0), 9)), z)
        add(pl.BlockSpec((POOL_HALO, GW),
                         lambda b, p, g: (jnp.minimum((fwd(b, p, g) + 1) * per8, last_halo), 9)), z)
        rope_map = lambda b, p, g: (jnp.where(p == 0, ng - 1 - g, g), 0)
        add(pl.BlockSpec((blk, 128), rope_map), extra["cos"])
        add(pl.BlockSpec((blk, 128), rope_map), extra["sin"])
        add(pl.BlockSpec((None, None, 2, N_HEADS, HEAD_DIM, HEAD_DIM),
                         lambda b, p, g: (b, l, 0, 0, 0, 0)), extra["s0"])
    add(layer((2, N_HEADS * CHUNK, CHUNK)), tabs["dmat"])
    add(layer((4, CHUNK, GW)), tabs["dec"])
    add(layer((N_HEADS * CHUNK, 128)), tabs["sink"])
    add(layer((CHUNK, N_HEADS * CHUNK)), tabs["ws"])
    add(layer((CHUNK, GW)), tabs["bias"])
    add(layer((8, GW)), tabs["vec"])
    add(const((GW, GW)), tabs["gmat"])
    add(pl.BlockSpec((blk, GW), on_grid(lambda b, p, g: (g * p, 0))),
        tabs["cnt_lat"] if latent else tabs["cnt_ctx"])
    add(layer((GW, GW)), tabs["wpool"])
    local = lambda b, p, g: (b * ng + g * p, 0)
    add(pl.BlockSpec((blk, D_MODEL), on_grid(local)), x_res)
    add(pl.BlockSpec((None, None, 6, D_MODEL),
                     on_grid(lambda b, p, g: (l, (1 + b) if latent else 0, 0, 0))), mod)
    add(layer((D_MODEL, D_MODEL)), w_out_bf16)
    add(layer((8, D_MODEL)), ln8)
    up_rows = D_MODEL // nb
    if ffn_up is not None:
        add(pl.BlockSpec((DEPTH, up_rows, D_FF), lambda b: (0, b, 0)), ffn_up)
        add(pl.BlockSpec((DEPTH, up_rows, D_FF), lambda b: (0, b, 1)), ffn_up)

    out_shape = [jax.ShapeDtypeStruct((nb * nc * CHUNK, D_MODEL), F32)]
    out_specs = [pl.BlockSpec((blk, D_MODEL), on_grid(local))]
    state_shape = (GW, GW) if latent else (CTX_SEQS, GW, GW)
    scratch = [pltpu.VMEM(state_shape, F32), pltpu.VMEM(state_shape, F32),
               pltpu.VMEM((nc * CHUNK, GW), F32),
               pltpu.VMEM((group, CHUNK + 2 * POOL_HALO, GW), F32),
               pltpu.VMEM((blk, D_MODEL), F32)]
    if latent:
        scratch += [pltpu.VMEM(((nc + 2) * CHUNK, 128), BF16), pltpu.VMEM(((nc + 2) * CHUNK, 128), BF16)]
    else:
        out_shape.append(jax.ShapeDtypeStruct((BATCH, 2, N_HEADS, HEAD_DIM, HEAD_DIM), F32))
        out_specs.append(pl.BlockSpec((CTX_SEQS, 2, N_HEADS, HEAD_DIM, HEAD_DIM), lambda b: (b, 0, 0, 0, 0)))
        for _ in range(2):
            out_shape.append(jax.ShapeDtypeStruct((BATCH, SEQ, 128), F32))
            out_specs.append(pl.BlockSpec((CTX_SEQS, SEQ, 128), lambda b: (b, 0, 0)))
        if ffn_up is not None:
            out_shape.append(jax.ShapeDtypeStruct((DEPTH, NFC, D_MODEL, 2 * FC), BF16))
            out_specs.append(pl.BlockSpec((DEPTH, NFC, up_rows, 2 * FC), lambda b: (0, 0, b, 0)))

    return pl.pallas_call(
        functools.partial(_mixer_kernel, latent=latent, group=group, ng=ng, cast_up=ffn_up is not None),
        grid=(nb, 2, ng) if latent else (nb,),
        in_specs=specs,
        out_specs=out_specs,
        out_shape=out_shape,
        scratch_shapes=scratch,
        compiler_params=pltpu.CompilerParams(
            dimension_semantics=("arbitrary",) * (3 if latent else 1), vmem_limit_bytes=VMEM_LIMIT),
        name="mixer_latent" if latent else "mixer_context",
    )(*args)


def _pad_rows(rows, n=8):
    a = jnp.stack(rows)
    return jnp.concatenate([a, jnp.zeros((n - a.shape[0],) + a.shape[1:], a.dtype)], axis=0)


def _block_diag(blocks):
    g, n, _ = blocks.shape
    eye = jnp.eye(g, dtype=blocks.dtype)
    return (eye[:, None, :, None] * blocks[:, :, None, :]).reshape(g * n, g * n)


def _inv_count(n, seqs=1):
    t = np.arange(n)
    cols = []
    for w in POOL_WINDOWS:
        cnt = np.clip(t + w // 2, 0, n) - np.clip(t - w // 2, 0, n)
        cols.append(np.repeat((1.0 / cnt)[:, None], HEAD_DIM, axis=1))
    return jnp.asarray(np.tile(np.concatenate(cols, axis=1), (seqs, 1)), F32)


def _rope_tables():
    rows = DEC_SEQ // GRID_W
    r, cc = jnp.meshgrid(jnp.arange(rows), jnp.arange(GRID_W), indexing="ij")
    half = HEAD_DIM // 2
    freqs = ROPE_BASE ** (-jnp.arange(0, half, 2, dtype=F32) / half)

    def tables(pos):
        ang = pos.reshape(-1).astype(F32)[:, None] * freqs[None, :]
        cos, sin = jnp.cos(ang), jnp.sin(ang)
        return jnp.concatenate([cos, cos], axis=1), jnp.concatenate([-sin, sin], axis=1)

    cr, sr = tables(r)
    ccol, scol = tables(cc)
    cos = jnp.concatenate([cr, ccol], axis=1)
    sin = jnp.concatenate([sr, scol], axis=1)
    return jnp.tile(cos, (1, 2)), jnp.tile(sin, (1, 2))


def _layer_tables(attn_sink, sgu_norm_w, sgu_norm_b, sgu_ws, sgu_bs, ret_decay, ret_gn_w, pool_w, pool_scale):
    log_g = jax.nn.log_sigmoid(ret_decay.astype(F32))
    i = jnp.arange(CHUNK, dtype=F32)
    rel = i[:, None] - i[None, :]
    kscale = HEAD_DIM ** -0.5
    d_f = jnp.where(rel >= 0, jnp.exp(jnp.maximum(rel, 0.0)[None] * log_g[0][:, None, None]), 0.0)
    d_b = jnp.where(rel <= 0, jnp.exp(jnp.maximum(-rel, 0.0)[None] * log_g[1][:, None, None]), 0.0)
    dmat = jnp.stack([d_f.reshape(N_HEADS * CHUNK, CHUNK), d_b.reshape(N_HEADS * CHUNK, CHUNK)]) * kscale

    def lanes(per_head):
        return jnp.repeat(per_head, HEAD_DIM, axis=1)

    qdec_f = lanes(jnp.exp((i + 1.0)[:, None] * log_g[0][None, :]))
    qdec_b = lanes(jnp.exp((CHUNK - i)[:, None] * log_g[1][None, :]))
    kdec_f = lanes(jnp.exp((CHUNK - 1.0 - i)[:, None] * log_g[0][None, :])) * kscale
    kdec_b = lanes(jnp.exp(i[:, None] * log_g[1][None, :])) * kscale
    cdec = jnp.repeat(jnp.exp(CHUNK * log_g), HEAD_DIM, axis=1)
    vec = _pad_rows([sgu_norm_w, sgu_norm_b, pool_scale, ret_gn_w[0], ret_gn_w[1], cdec[0], cdec[1]])
    return {
        "dmat": dmat,
        "dec": jnp.stack([qdec_f, qdec_b, kdec_f, kdec_b]),
        "sink": jnp.broadcast_to(jnp.repeat(attn_sink, CHUNK)[:, None], (N_HEADS * CHUNK, 128)),
        "ws": jnp.concatenate([sgu_ws[h] for h in range(N_HEADS)], axis=1).astype(BF16),
        "bias": jnp.repeat(sgu_bs.T, HEAD_DIM, axis=1),
        "vec": vec,
        "wpool": _block_diag(pool_w).astype(BF16),
    }


def kernel(x_prompt, x_sample, cache_attn_k, cache_attn_v, state_ret, c, c_ctx, w_ada, b_ada, w_in,
           w_out, attn_sink, sgu_norm_w, sgu_norm_b, sgu_ws, sgu_bs, ret_decay, ret_gn_w, pool_w,
           pool_scale, ffn_up, ffn_conv_w, ffn_conv_b, ffn_down, ln_w, ln_b):
    cond8 = jnp.concatenate([c_ctx[None], c, jnp.zeros((8 - 1 - DEC_BATCH, D_MODEL), F32)], axis=0)
    mod = _modulation(cond8, w_ada, b_ada).reshape(DEPTH, 8, 6, D_MODEL)

    tabs = jax.vmap(_layer_tables)(attn_sink, sgu_norm_w, sgu_norm_b, sgu_ws, sgu_bs, ret_decay, ret_gn_w,
                                   pool_w, pool_scale)
    tabs["gmat"] = _block_diag(jnp.full((N_HEADS, HEAD_DIM, HEAD_DIM), 1.0 / HEAD_DIM, F32)).astype(BF16)
    tabs["cnt_ctx"] = _inv_count(SEQ, CTX_SEQS)
    tabs["cnt_lat"] = _inv_count(DEC_SEQ)
    cos, sin = _rope_tables()
    extra = {"ck": cache_attn_k.reshape(DEC_BATCH, DEPTH, PAST_LEN, 128),
             "cv": cache_attn_v.reshape(DEC_BATCH, DEPTH, PAST_LEN, 128),
             "cos": cos, "sin": sin, "s0": state_ret}
    ln8 = jnp.concatenate([ln_w, ln_b, jnp.zeros((DEPTH, 4, D_MODEL), F32)], axis=1)
    conv8 = jnp.concatenate([ffn_conv_w, ffn_conv_b[:, None], jnp.zeros((DEPTH, 4, 2 * D_FF), F32)], axis=1)


    xs = [x_prompt.reshape(N_CTX, D_MODEL), x_sample.reshape(N_LAT, D_MODEL)]
    new_k, new_v, new_s = [], [], []
    for l in range(DEPTH):
        if l == 0:
            z, w_out_bf16 = _inproj(xs, mod, w_in, l, w_out=w_out)
        else:
            (z,) = _inproj(xs, mod, w_in, l)
        if l == 0:
            x1_ctx, st, kn, vn, up_chunks = _mixer(z, xs[0], mod, w_out_bf16, ln8, tabs, l, latent=False,
                                                   ffn_up=ffn_up)
        else:
            x1_ctx, st, kn, vn = _mixer(z, xs[0], mod, w_out_bf16, ln8, tabs, l, latent=False)
        (x1_lat,) = _mixer(z, xs[1], mod, w_out_bf16, ln8, tabs, l, latent=True, extra=extra)
        xs = _ffn([x1_ctx, x1_lat], mod, up_chunks, conv8, ffn_down, ln8, l)
        new_k.append(kn.reshape(BATCH, SEQ, 2, HEAD_DIM))
        new_v.append(vn.reshape(BATCH, SEQ, 2, HEAD_DIM))
        new_s.append(st)

    y_prompt = xs[0].reshape(BATCH, SEQ, D_MODEL)
    y_sample = xs[1].reshape(DEC_BATCH, DEC_SEQ, D_MODEL)
    return (y_prompt, y_sample, jnp.stack(new_k, axis=1), jnp.stack(new_v, axis=1),
            jnp.stack(new_s, axis=1))
```

```python
import functools

import numpy as np
import jax
import jax.numpy as jnp
from jax import lax
from jax.experimental import pallas as pl
from jax.experimental.pallas import tpu as pltpu

F32 = jnp.float32
BF16 = jnp.bfloat16

D_MODEL = 1024
BATCH = 16
SEQ = 256
DEPTH = 2
DEC_BATCH = 2
DEC_SEQ = 2048
PAST_LEN = 256
GRID_W = 64
CHUNK = 128
HEAD_DIM = 64
GW = D_MODEL // 4
N_HEADS = 4
POOL_WINDOWS = (2, 4, 8, 16)
POOL_HALO = 8
D_FF = 2816
ROPE_BASE = 10000.0
LN_EPS = 1e-5
NEG_INF = -1e30
IN_WIDTH = 10 * GW
ALPHA = (2.0 * DEPTH) ** 0.25

N_CTX = BATCH * SEQ
N_LAT = DEC_BATCH * DEC_SEQ
ROWS = N_CTX + N_LAT

TM = 1024
NB_IN = 512
FC = 256
NB_ADA = 1536
VMEM_LIMIT = 56 * 1024 * 1024


def _cond_of_tile(i, tm=TM):
    ctx_tiles = N_CTX // tm
    return jnp.where(i < ctx_tiles, 0, 1 + (i - ctx_tiles) // (DEC_SEQ // tm))


def _tile_specs(tm):
    ctx_tiles = N_CTX // tm
    return [pl.BlockSpec((tm, D_MODEL), lambda i: (jnp.minimum(i, ctx_tiles - 1), 0)),
            pl.BlockSpec((tm, D_MODEL), lambda i: (jnp.maximum(i - ctx_tiles, 0), 0))]


def _per_half(i, tm, fn):
    ctx_tiles = N_CTX // tm

    @pl.when(i < ctx_tiles)
    def _():
        fn(0)

    @pl.when(i >= ctx_tiles)
    def _():
        fn(1)


def _layer_norm(x, w, b):
    mu = jnp.mean(x, axis=-1, keepdims=True)
    d = x - mu
    var = jnp.mean(d * d, axis=-1, keepdims=True)
    return d * lax.rsqrt(var + LN_EPS) * w + b


def _silu(x):
    return x * jax.nn.sigmoid(x)


def _bdot(a, b):
    return jnp.dot(a.astype(BF16), b.astype(BF16), preferred_element_type=F32)


def _mod_kernel(c_ref, w_ref, b_ref, o_ref):
    o_ref[...] = _bdot(_silu(c_ref[...]), w_ref[...]) + b_ref[...]


def _modulation(cond8, w_ada, b_ada):
    return pl.pallas_call(
        _mod_kernel,
        grid=(DEPTH, 6 * D_MODEL // NB_ADA),
        in_specs=[
            pl.BlockSpec((8, D_MODEL), lambda l, j: (0, 0)),
            pl.BlockSpec((None, D_MODEL, NB_ADA), lambda l, j: (l, 0, j)),
            pl.BlockSpec((None, 1, NB_ADA), lambda l, j: (l, 0, j)),
        ],
        out_specs=pl.BlockSpec((None, 8, NB_ADA), lambda l, j: (l, 0, j)),
        out_shape=jax.ShapeDtypeStruct((DEPTH, 8, 6 * D_MODEL), F32),
        compiler_params=pltpu.CompilerParams(
            dimension_semantics=("arbitrary", "arbitrary"), vmem_limit_bytes=VMEM_LIMIT),
        name="modulation",
    )(cond8, w_ada, b_ada.reshape(DEPTH, 1, 6 * D_MODEL))


def _inproj_kernel(*refs, cast_w_out):
    if cast_w_out:
        xc_ref, xl_ref, mod_ref, w_ref, wo_ref, z_ref, wo_bf16_ref, h_scr = refs
    else:
        xc_ref, xl_ref, mod_ref, w_ref, z_ref, h_scr = refs
    x_refs = (xc_ref, xl_ref)
    m = mod_ref[...]

    def project(side):
        for rh in range(2):
            rows = slice(rh * TM // 2, (rh + 1) * TM // 2)
            h_scr[rows, :] = (x_refs[side][rows, :] * (1.0 + m[1:2]) + m[0:1]).astype(BF16)
            for jb in range(IN_WIDTH // NB_IN):
                cols = slice(jb * NB_IN, (jb + 1) * NB_IN)
                z_ref[rows, cols] = jnp.dot(h_scr[rows, :], w_ref[:, cols].astype(BF16),
                                            preferred_element_type=F32)
            if cast_w_out and rh == 0:
                wo_bf16_ref[...] = wo_ref[...].astype(BF16)

    _per_half(pl.program_id(0), TM, project)


def _inproj(xs, mod, w_in, l, w_out=None):
    steps = ROWS // TM
    in_specs = _tile_specs(TM) + [
        pl.BlockSpec((None, None, 6, D_MODEL), lambda i: (l, _cond_of_tile(i), 0, 0)),
        pl.BlockSpec((None, D_MODEL, IN_WIDTH), lambda i: (l, 0, 0), pipeline_mode=pl.Buffered(1)),
    ]
    out_specs = [pl.BlockSpec((TM, IN_WIDTH), lambda i: (i, 0))]
    out_shape = [jax.ShapeDtypeStruct((ROWS, IN_WIDTH), F32)]
    args = [*xs, mod, w_in]
    if w_out is not None:
        rows_spec = pl.BlockSpec((DEPTH, D_MODEL // steps, D_MODEL), lambda i: (0, i, 0))
        in_specs.append(rows_spec)
        out_specs.append(rows_spec)
        out_shape.append(jax.ShapeDtypeStruct((DEPTH, D_MODEL, D_MODEL), BF16))
        args.append(w_out)
    return pl.pallas_call(
        functools.partial(_inproj_kernel, cast_w_out=w_out is not None),
        grid=(steps,),
        in_specs=in_specs,
        out_specs=out_specs,
        out_shape=out_shape,
        scratch_shapes=[pltpu.VMEM((TM, D_MODEL), BF16)],
        compiler_params=pltpu.CompilerParams(
            dimension_semantics=("arbitrary",), vmem_limit_bytes=VMEM_LIMIT),
        name="inproj",
    )(*args)


TMF = 512
SEGF = TMF // 8
PITCHF = SEGF + 8
CTXF_TILES = N_CTX // TMF
LATF_PER_SEQ = DEC_SEQ // TMF
HALO_ROWS = 16
RBUF = 256
RB = 256
GB = 64
NFC = D_FF // FC
LANE_BLOCKS = D_MODEL // 128
assert RB == RBUF


def _seg_rows_f(xc_ref, k):
    return jnp.concatenate([xc_ref[cb, pl.ds(k, 8, stride=PITCHF), :] for cb in range(LANE_BLOCKS)], axis=1)


def _ffn_kernel(xc_ref, xl_ref, xp_ref, xn_ref, mod_ref, up_ref, cv_ref, dn_ref, ln_ref, oc_ref, ol_ref,
                h_scr, act_scr, xc_scr, u0_scr, u1_scr):
    x_refs = (xc_ref, xl_ref)
    o_refs = (oc_ref, ol_ref)
    u_scrs = (u0_scr, u1_scr)
    i = pl.program_id(0)
    is_ctx = i < CTXF_TILES
    lat_pos = (i - CTXF_TILES) % LATF_PER_SEQ
    m = mod_ref[...]
    ln = ln_ref[...]

    scale = 1.0 + m[4:5]
    shift = m[3:4]

    def stage(side):
        for cb in range(LANE_BLOCKS):
            for s in range(8):
                xc_scr[cb, s * PITCHF:s * PITCHF + SEGF, :] = x_refs[side][s * SEGF:(s + 1) * SEGF,
                                                                            cb * 128:(cb + 1) * 128]

    _per_half(i, TMF, stage)

    def build_h(ub):
        for k in range(ub * RBUF // 8, (ub + 1) * RBUF // 8, 2):
            rows = jnp.concatenate([_seg_rows_f(xc_scr, k), _seg_rows_f(xc_scr, k + 1)], axis=0)
            h_scr[8 * k:8 * k + 16, :] = (rows * scale + shift).astype(BF16)

    sub16 = lax.broadcasted_iota(jnp.int32, (HALO_ROWS, D_MODEL), 0)
    prev_ok = jnp.logical_not(is_ctx) & (lat_pos > 0)
    next_ok = jnp.logical_not(is_ctx) & (lat_pos < LATF_PER_SEQ - 1)
    halo_x = jnp.where(sub16 == 0, xp_ref[POOL_HALO - 1:POOL_HALO, :], xn_ref[0:1, :])
    keep = ((sub16 == 0) & prev_ok) | ((sub16 == 1) & next_ok)
    h_scr[TMF:TMF + HALO_ROWS, :] = jnp.where(keep, halo_x * scale + shift, 0.0).astype(BF16)

    n_ub = TMF // RBUF
    sub = lax.broadcasted_iota(jnp.int32, (8, FC), 0)
    seg_per_seq = SEQ // SEGF
    ctx_first = is_ctx & (sub % seg_per_seq == 0)
    ctx_last = is_ctx & (sub % seg_per_seq == seg_per_seq - 1)

    def up_proj(slot, c, ub):
        rows = slice(ub * RBUF, (ub + 1) * RBUF + (HALO_ROWS if ub == n_ub - 1 else 0))
        u_scrs[slot][rows, :] = jnp.dot(h_scr[rows, :], up_ref[c], preferred_element_type=F32)

    def conv(u_ref, lanes, cvs, r0):
        lo = max(r0 - 8, 0)
        hi = min(r0 + GB + 8, TMF)
        ue = u_ref[lo:hi, lanes]
        u = ue[r0 - lo:r0 - lo + GB]
        if r0 == 0:
            b_first = jnp.where(sub == 0, u_ref[TMF:TMF + 1, lanes],
                                pltpu.roll(u_ref[TMF - 8:TMF, lanes], 1, axis=0))
            um1 = jnp.concatenate([jnp.where(ctx_first, 0.0, b_first), u[0:GB - 8]], axis=0)
        else:
            um1 = ue[0:GB]
        if r0 == TMF - GB:
            b_last = jnp.where(sub == 7, u_ref[TMF + 1:TMF + 2, lanes],
                               pltpu.roll(u_ref[0:8, lanes], 7, axis=0))
            up1 = jnp.concatenate([u[8:GB], jnp.where(ctx_last, 0.0, b_last)], axis=0)
        else:
            up1 = ue[r0 - lo + 8:r0 - lo + GB + 8]
        return um1 * cvs[0:1] + u * cvs[1:2] + up1 * cvs[2:3] + cvs[3:4]

    def gate(slot, c, ub):
        cva = cv_ref[0:4, c * FC:(c + 1) * FC]
        cvg = cv_ref[0:4, D_FF + c * FC:D_FF + (c + 1) * FC]
        for r0 in range(ub * RBUF, (ub + 1) * RBUF, GB):
            a = conv(u_scrs[slot], slice(0, FC), cva, r0)
            g = conv(u_scrs[slot], slice(FC, 2 * FC), cvg, r0)
            act_scr[c, r0:r0 + GB, :] = (_silu(a) * g).astype(BF16)

    def finish(rb):
        vrows = range(rb * RB // 8, (rb + 1) * RB // 8)
        lhs = jnp.concatenate([act_scr[c, rb * RB:(rb + 1) * RB, :] for c in range(NFC)], axis=1)
        y = jnp.dot(lhs, dn_ref[...].astype(BF16), preferred_element_type=F32)
        xr = jnp.concatenate([_seg_rows_f(xc_scr, k) for k in vrows], axis=0)
        out = _layer_norm(ALPHA * xr + m[5:6] * y, ln[1:2], ln[3:4])
        for kk, k in enumerate(vrows):
            for cb in range(LANE_BLOCKS):
                xc_scr[cb, pl.ds(k, 8, stride=PITCHF), :] = out[8 * kk:8 * kk + 8, cb * 128:(cb + 1) * 128]

    def write_out(side):
        for cb in range(LANE_BLOCKS):
            for s in range(8):
                o_refs[side][s * SEGF:(s + 1) * SEGF, cb * 128:(cb + 1) * 128] = xc_scr[
                    cb, s * PITCHF:s * PITCHF + SEGF, :]

    build_h(0)
    for c in range(NFC + 1):
        for ub in range(n_ub):
            if c == 0 and ub > 0:
                build_h(ub)
            if c < NFC:
                up_proj(c % 2, c, ub)
            if c >= 1:
                gate((c - 1) % 2, c - 1, ub)
            if c == NFC:
                finish(ub)

    _per_half(i, TMF, write_out)


def _ffn(xs, mod, up_chunks, conv8, down, ln8, l):
    halo_blocks = TMF // POOL_HALO
    last_halo = N_LAT // POOL_HALO - 1
    return pl.pallas_call(
        _ffn_kernel,
        grid=(ROWS // TMF,),
        in_specs=[
            *_tile_specs(TMF),
            pl.BlockSpec((POOL_HALO, D_MODEL),
                         lambda i: (jnp.maximum((i - CTXF_TILES) * halo_blocks - 1, 0), 0)),
            pl.BlockSpec((POOL_HALO, D_MODEL),
                         lambda i: (jnp.clip((i - CTXF_TILES + 1) * halo_blocks, 0, last_halo), 0)),
            pl.BlockSpec((None, None, 6, D_MODEL), lambda i: (l, _cond_of_tile(i, TMF), 0, 0)),
            pl.BlockSpec((None, NFC, D_MODEL, 2 * FC), lambda i: (l, 0, 0, 0), pipeline_mode=pl.Buffered(1)),
            pl.BlockSpec((None, 8, 2 * D_FF), lambda i: (l, 0, 0)),
            pl.BlockSpec((None, D_FF, D_MODEL), lambda i: (l, 0, 0), pipeline_mode=pl.Buffered(1)),
            pl.BlockSpec((None, 8, D_MODEL), lambda i: (l, 0, 0)),
        ],
        out_specs=_tile_specs(TMF),
        out_shape=[jax.ShapeDtypeStruct((N_CTX, D_MODEL), F32), jax.ShapeDtypeStruct((N_LAT, D_MODEL), F32)],
        scratch_shapes=[pltpu.VMEM((TMF + HALO_ROWS, D_MODEL), BF16),
                        pltpu.VMEM((NFC, TMF, FC), BF16),
                        pltpu.VMEM((LANE_BLOCKS, 8 * PITCHF, 128), F32),
                        pltpu.VMEM((TMF + HALO_ROWS, 2 * FC), F32),
                        pltpu.VMEM((TMF + HALO_ROWS, 2 * FC), F32)],
        compiler_params=pltpu.CompilerParams(
            dimension_semantics=("arbitrary",), vmem_limit_bytes=VMEM_LIMIT),
        name="convffn",
    )(xs[0], xs[1], xs[1], xs[1], mod, up_chunks, conv8, down, ln8)


def _head_masks(width):
    lane = lax.broadcasted_iota(jnp.int32, (1, width), 1)
    return [(lane >= h * HEAD_DIM) & (lane < (h + 1) * HEAD_DIM) for h in range(width // HEAD_DIM)]


def _stack_heads(x, masks):
    return jnp.concatenate([jnp.where(m, x, jnp.zeros_like(x)) for m in masks], axis=0)


def _rope(x, cos, sin):
    lane = lax.broadcasted_iota(jnp.int32, (1, 128), 1)
    lower = (lane & 31) < 16
    outs = []
    for k in range(x.shape[1] // 128):
        xb = x[:, k * 128:(k + 1) * 128]
        partner = jnp.where(lower, pltpu.roll(xb, 112, axis=1), pltpu.roll(xb, 16, axis=1))
        outs.append(xb * cos + partner * sin)
    return outs[0] if len(outs) == 1 else jnp.concatenate(outs, axis=1)


def _load_state(state_scr, blocks_ref, d):
    state_scr[...] = jnp.zeros((GW, GW), F32)
    for h in range(N_HEADS):
        sl = slice(h * HEAD_DIM, (h + 1) * HEAD_DIM)
        state_scr[sl, sl] = blocks_ref[d, h]


def _store_state(blocks_ref, d, state_scr):
    for h in range(N_HEADS):
        sl = slice(h * HEAD_DIM, (h + 1) * HEAD_DIM)
        blocks_ref[d, h] = state_scr[sl, sl]


LAT_GROUP = 4
CTX_SEQS = 2


def _chunk_off(c):
    return c * CHUNK if isinstance(c, int) else pl.multiple_of(c * CHUNK, CHUNK)


def _interleave(chains):
    chains = list(chains)
    while chains:
        for ch in list(chains):
            try:
                next(ch)
            except StopIteration:
                chains.remove(ch)


def _mixer_kernel(*refs, latent, group, ng, cast_up=False):
    if latent:
        (qa_ref, ka_ref, va_ref, kx_ref, vx_ref, ub_ref, vb_ref, qc_ref, kc_ref, vc_ref, gf_ref, gb_ref,
         pd_ref, pdp_ref, pdn_ref, cos_ref, sin_ref, s0_ref, dmat_ref, dec_ref, sink_ref, ws_ref, bias_ref,
         vec_ref, gmat_ref, cnt_ref, wpool_ref, xres_ref, mod_ref, wout_ref, ln_ref,
         x1_ref, sf_scr, sb_scr, ob_scr, pext_scr, mix_ref, k_scr, v_scr) = refs
    else:
        (qa_ref, ka_ref, va_ref, ub_ref, vb_ref, qc_ref, kc_ref, vc_ref, gf_ref, gb_ref, pd_ref,
         dmat_ref, dec_ref, sink_ref, ws_ref, bias_ref, vec_ref, gmat_ref, cnt_ref, wpool_ref,
         xres_ref, mod_ref, wout_ref, ln_ref) = refs[:24]
        if cast_up:
            upa_ref, upg_ref, x1_ref, st_ref, kn_ref, vn_ref, upo_ref = refs[24:31]
        else:
            x1_ref, st_ref, kn_ref, vn_ref = refs[-9:-5]
        sf_scr, sb_scr, ob_scr, pext_scr, mix_ref = refs[-5:]

    nc = group * ng
    cps = group if latent else SEQ // CHUNK
    g_a = mod_ref[...][2:3]
    ln = ln_ref[...]
    masks = _head_masks(GW)
    row = lax.broadcasted_iota(jnp.int32, (GW, GW), 0)
    col = lax.broadcasted_iota(jnp.int32, (GW, GW), 1)
    bd_mask = (row // HEAD_DIM) == (col // HEAD_DIM)
    vec = vec_ref[...]
    lane = lax.broadcasted_iota(jnp.int32, (1, 2 * HEAD_DIM), 1)
    lo = lane < HEAD_DIM

    def rows(k):
        return slice(k * CHUNK, (k + 1) * CHUNK)

    def retention(order, d, state_scr, out):
        dmat = dmat_ref[d]
        qdec, kdec = dec_ref[d], dec_ref[2 + d]
        cdec = vec[5 + d:6 + d]
        state = state_scr[...]
        for k in order:
            q, kk, v = qc_ref[rows(k), :], kc_ref[rows(k), :], vc_ref[rows(k), :]
            qb, kb, vb = q.astype(BF16), kk.astype(BF16), v.astype(BF16)
            s = lax.dot_general(_stack_heads(qb, masks), kb, (((1,), (1,)), ((), ())),
                                preferred_element_type=F32)
            yield
            p = (s * dmat).astype(BF16)
            p_cat = jnp.concatenate([p[h * CHUNK:(h + 1) * CHUNK] for h in range(N_HEADS)], axis=1)
            o = (jnp.dot(p_cat, _stack_heads(vb, masks), preferred_element_type=F32)
                 + _bdot(q * qdec, state))
            yield
            upd = lax.dot_general((kk * kdec).astype(BF16), vb, (((0,), (0,)), ((), ())),
                                  preferred_element_type=F32)
            state = state * cdec + jnp.where(bd_mask, upd, 0.0)
            out[k] = o
            yield
        state_scr[...] = state

    def stage_kv(k, c):
        dst = pl.ds(_chunk_off(c + 1), CHUNK)
        k_scr[dst, :] = _rope(ka_ref[rows(k), :], cos_ref[rows(k), :], sin_ref[rows(k), :]).astype(BF16)
        yield
        v_scr[dst, :] = va_ref[rows(k), :].astype(BF16)
        yield

    def attention(k, c, done):
        q = qa_ref[rows(k), :]
        if latent:
            q = _rope(q, cos_ref[rows(k), :], sin_ref[rows(k), :])
        q = q * (HEAD_DIM ** -0.5)
        q0, q1 = q[:, :128], q[:, 128:]
        zero = jnp.zeros_like(q0)
        q_st = jnp.concatenate([
            jnp.where(lo, q0, zero),
            jnp.where(lo, pltpu.roll(q0, HEAD_DIM, axis=1), zero),
            jnp.where(lo, zero, pltpu.roll(q1, HEAD_DIM, axis=1)),
            jnp.where(lo, zero, q1)], axis=0).astype(BF16)
        if latent:
            band = pl.ds(_chunk_off(c), 3 * CHUNK)
            k_all = jnp.concatenate([k_scr[band, :], kx_ref[...].astype(BF16)], axis=0)
            v_all = jnp.concatenate([v_scr[band, :], vx_ref[...].astype(BF16)], axis=0)
        else:
            own = slice(k // cps * SEQ, (k // cps + 1) * SEQ)
            k_all = ka_ref[own, :].astype(BF16)
            v_all = va_ref[own, :].astype(BF16)
        s = lax.dot_general(q_st, k_all, (((1,), (1,)), ((), ())), preferred_element_type=F32)
        yield
        if latent:
            nk = 3 * CHUNK + PAST_LEN
            qi = lax.broadcasted_iota(jnp.int32, (N_HEADS * CHUNK, nk), 0) & (CHUNK - 1)
            kj = lax.broadcasted_iota(jnp.int32, (N_HEADS * CHUNK, nk), 1)
            kpos = kj + (c - 1) * CHUNK
            valid = (kj >= 3 * CHUNK) | ((kj >= qi) & (kj <= qi + 2 * CHUNK)
                                         & (kpos >= 0) & (kpos < nc * CHUNK))
            s = jnp.where(valid, s, NEG_INF)
        sink = sink_ref[...][:, 0:1]
        mx = jnp.maximum(jnp.max(s, axis=-1, keepdims=True), sink)
        yield
        e = jnp.exp(s - mx)
        den = jnp.sum(e, axis=-1, keepdims=True) + jnp.exp(sink - mx)
        yield
        o = jnp.dot(e.astype(BF16), v_all, preferred_element_type=F32) / den
        yield
        mix_ref[rows(k), 0:128] = jnp.where(lo, o[0:CHUNK], pltpu.roll(o[CHUNK:2 * CHUNK], HEAD_DIM, axis=1))
        mix_ref[rows(k), 128:256] = jnp.where(lo, pltpu.roll(o[2 * CHUNK:3 * CHUNK], HEAD_DIM, axis=1),
                                              o[3 * CHUNK:4 * CHUNK])
        done.add(("A", k))
        yield

    def gating_unit(k, done):
        vn = _layer_norm(vb_ref[rows(k), :], vec[0:1], vec[1:2]).astype(BF16)
        yield
        sg = jnp.dot(ws_ref[...], _stack_heads(vn, masks), preferred_element_type=F32) + bias_ref[...]
        yield
        mix_ref[rows(k), GW:2 * GW] = ub_ref[rows(k), :] * sg
        done.add(("B", k))
        yield

    def retention_mix(k, o_fwd, o_bwd, done):
        while k not in o_fwd or k not in o_bwd:
            yield
        gmat = gmat_ref[...]
        normed = []
        for o in (o_fwd[k], o_bwd[k]):
            mu = jnp.dot(o.astype(BF16), gmat, preferred_element_type=F32)
            yield
            dlt = o - mu
            var = jnp.dot((dlt * dlt).astype(BF16), gmat, preferred_element_type=F32)
            yield
            normed.append(dlt * lax.rsqrt(var + LN_EPS))
        mix_ref[rows(k), 2 * GW:3 * GW] = (_silu(gf_ref[rows(k), :]) * (normed[0] * vec[3:4])
                                           + _silu(gb_ref[rows(k), :]) * (normed[1] * vec[4:5]))
        done.add(("C", k))
        yield

    def pooling(k, c, done):
        pd = pd_ref[rows(k), :]
        pext = pext_scr.at[k]
        zeros8 = jnp.zeros((POOL_HALO, GW), F32)
        if k % cps > 0:
            prev8 = pd_ref[k * CHUNK - POOL_HALO:k * CHUNK, :]
        else:
            prev8 = jnp.where(c > 0, pdp_ref[...], 0.0) if latent else zeros8
        if k % cps < cps - 1:
            next8 = pd_ref[(k + 1) * CHUNK:(k + 1) * CHUNK + POOL_HALO, :]
        else:
            next8 = jnp.where(c < nc - 1, pdn_ref[...], 0.0) if latent else zeros8
        pext[0:POOL_HALO, :] = prev8
        pext[POOL_HALO:POOL_HALO + CHUNK, :] = pd
        pext[POOL_HALO + CHUNK:2 * POOL_HALO + CHUNK, :] = next8
        yield

        def win(d, half):
            return pext[pl.ds(POOL_HALO + d, CHUNK), half * 128:(half + 1) * 128]

        a2 = win(-1, 0) + win(0, 0)
        a4 = a2 + win(-2, 0) + win(1, 0)
        yield
        a8 = win(-4, 1)
        for d in range(-3, 4):
            a8 = a8 + win(d, 1)
        yield
        a16 = a8
        for d in list(range(-8, -4)) + list(range(4, 8)):
            a16 = a16 + win(d, 1)
        yield
        sums = jnp.concatenate([jnp.where(lo, a2, a4), jnp.where(lo, a8, a16)], axis=1)
        yd = sums * cnt_ref[rows(k), :] - pd
        mix_ref[rows(k), 3 * GW:4 * GW] = _bdot(yd, wpool_ref[...]) * vec[2:3]
        done.add(("D", k))
        yield

    def out_projection(k, done):
        while not all((m, k) in done for m in "ABCD"):
            yield
        y = jnp.dot(mix_ref[rows(k), :].astype(BF16), wout_ref[...], preferred_element_type=F32)
        yield
        x1_ref[rows(k), :] = _layer_norm(ALPHA * xres_ref[rows(k), :] + g_a * y, ln[0:1], ln[2:3])
        yield

    def up_weight_rows():
        for lyr in range(DEPTH):
            for cc in range(NFC):
                upo_ref[lyr, cc, :, 0:FC] = upa_ref[lyr, :, cc * FC:(cc + 1) * FC].astype(BF16)
                upo_ref[lyr, cc, :, FC:2 * FC] = upg_ref[lyr, :, cc * FC:(cc + 1) * FC].astype(BF16)
                yield

    def forward_chains(chunk_of, o_bwd, fwd_states):
        done, o_fwd = set(), {}
        chains = [retention(ks, 0, state, o_fwd) for ks, state in fwd_states]
        for k in range(group):
            chains += [attention(k, chunk_of(k), done), gating_unit(k, done), pooling(k, chunk_of(k), done),
                       retention_mix(k, o_fwd, o_bwd, done), out_projection(k, done)]
        return chains

    if not latent:
        sb_scr[...] = jnp.zeros((CTX_SEQS, GW, GW), F32)
        sf_scr[...] = jnp.zeros((CTX_SEQS, GW, GW), F32)
        seq_chunks = [range(sq * cps, (sq + 1) * cps) for sq in range(CTX_SEQS)]
        o_bwd = {}
        side_chains = [retention(reversed(ks), 1, sb_scr.at[sq], o_bwd) for sq, ks in enumerate(seq_chunks)]
        if cast_up:
            side_chains.append(up_weight_rows())
        _interleave(side_chains + forward_chains(lambda k: k % cps, o_bwd,
                                                 [(ks, sf_scr.at[sq]) for sq, ks in enumerate(seq_chunks)]))
        for sq in range(CTX_SEQS):
            kn_ref[sq] = ka_ref[sq * SEQ:(sq + 1) * SEQ, :]
            vn_ref[sq] = va_ref[sq * SEQ:(sq + 1) * SEQ, :]
            _store_state(st_ref.at[sq], 1, sb_scr.at[sq])
            _store_state(st_ref.at[sq], 0, sf_scr.at[sq])
        return

    p = pl.program_id(1)
    g = pl.program_id(2)

    @pl.when(p == 0)
    def _():
        @pl.when(g == 0)
        def _():
            _load_state(sb_scr, s0_ref, 1)
            zero_blk = jnp.zeros((CHUNK, 2 * HEAD_DIM), BF16)
            for scr in (k_scr, v_scr):
                scr[0:CHUNK, :] = zero_blk
                scr[(nc + 1) * CHUNK:(nc + 2) * CHUNK, :] = zero_blk

        first = (ng - 1 - g) * group
        o_bwd = {}
        _interleave([retention(reversed(range(group)), 1, sb_scr, o_bwd)]
                    + [stage_kv(k, first + k) for k in range(group)])
        for k in range(group):
            ob_scr[pl.ds(_chunk_off(first + k), CHUNK), :] = o_bwd[k]

    @pl.when(p == 1)
    def _():
        @pl.when(g == 0)
        def _():
            _load_state(sf_scr, s0_ref, 0)

        first = g * group
        o_bwd = {k: ob_scr[pl.ds(_chunk_off(first + k), CHUNK), :] for k in range(group)}
        _interleave(forward_chains(lambda k: first + k, o_bwd, [(range(group), sf_scr)]))


def _mixer(z, x_res, mod, w_out_bf16, ln8, tabs, l, latent, extra=None, ffn_up=None, stacked=None):
    nb = DEC_BATCH if latent else BATCH // CTX_SEQS
    nc = (DEC_SEQ if latent else CTX_SEQS * SEQ) // CHUNK
    group = LAT_GROUP if latent else nc
    ng = nc // group
    blk = group * CHUNK
    base = (N_CTX // blk) if latent else 0
    per8 = blk // POOL_HALO
    last_halo = ROWS // POOL_HALO - 1

    def on_grid(f):
        return (lambda b, p, g: f(b, p, g)) if latent else (lambda b: f(b, 1, 0))

    def fwd(b, p, g):
        return base + b * ng + g * p

    def both(b, p, g):
        return base + b * ng + jnp.where(p == 0, ng - 1 - g, g)

    def bwd_only(b, p, g):
        return base + b * ng + (ng - 1 - g) * (1 - p)

    def col(width, idx, rowmap):
        return pl.BlockSpec((blk, width), on_grid(lambda b, p, g: (rowmap(b, p, g), idx)))

    def const(shape):
        return pl.BlockSpec(shape, on_grid(lambda b, p, g: (0,) * len(shape)))

    def layer(shape):
        return pl.BlockSpec((None,) + shape, on_grid(lambda b, p, g: (l,) + (0,) * len(shape)))

    specs, args = [], []

    def add(spec, arr):
        specs.append(spec)
        args.append(arr)

    add(col(GW, 0, fwd), z)
    add(col(128, 2, bwd_only if latent else fwd), z)
    add(col(128, 3, bwd_only if latent else fwd), z)
    if latent:
        add(pl.BlockSpec((None, None, PAST_LEN, 128), lambda b, p, g: (b, l, 0, 0)), extra["ck"])
        add(pl.BlockSpec((None, None, PAST_LEN, 128), lambda b, p, g: (b, l, 0, 0)), extra["cv"])
    add(col(GW, 2, fwd), z)
    add(col(GW, 3, fwd), z)
    add(col(GW, 4, both), z)
    add(col(GW, 5, both), z)
    add(col(GW, 6, both), z)
    add(col(GW, 7, fwd), z)
    add(col(GW, 8, fwd), z)
    add(col(GW, 9, fwd), z)
    if latent:
        add(pl.BlockSpec((POOL_HALO, GW),
                         lambda b, p, g: (jnp.maximum(fwd(b, p, g) * per8 - 1, 0), 9)), z)
        add(pl.BlockSpec((POOL_HALO, GW),
                         lambda b, p, g: (jnp.minimum((fwd(b, p, g) + 1) * per8, last_halo), 9)), z)
        rope_map = lambda b, p, g: (jnp.where(p == 0, ng - 1 - g, g), 0)
        add(pl.BlockSpec((blk, 128), rope_map), extra["cos"])
        add(pl.BlockSpec((blk, 128), rope_map), extra["sin"])
        add(pl.BlockSpec((None, None, 2, N_HEADS, HEAD_DIM, HEAD_DIM),
                         lambda b, p, g: (b, l, 0, 0, 0, 0)), extra["s0"])
    add(layer((2, N_HEADS * CHUNK, CHUNK)), tabs["dmat"])
    add(layer((4, CHUNK, GW)), tabs["dec"])
    add(layer((N_HEADS * CHUNK, 128)), tabs["sink"])
    add(layer((CHUNK, N_HEADS * CHUNK)), tabs["ws"])
    add(layer((CHUNK, GW)), tabs["bias"])
    add(layer((8, GW)), tabs["vec"])
    add(const((GW, GW)), tabs["gmat"])
    add(pl.BlockSpec((blk, GW), on_grid(lambda b, p, g: (g * p, 0))),
        tabs["cnt_lat"] if latent else tabs["cnt_ctx"])
    add(layer((GW, GW)), tabs["wpool"])
    local = lambda b, p, g: (b * ng + g * p, 0)
    add(pl.BlockSpec((blk, D_MODEL), on_grid(local)), x_res)
    add(pl.BlockSpec((None, None, 6, D_MODEL),
                     on_grid(lambda b, p, g: (l, (1 + b) if latent else 0, 0, 0))), mod)
    add(layer((D_MODEL, D_MODEL)), w_out_bf16)
    add(layer((8, D_MODEL)), ln8)
    up_rows = D_MODEL // nb
    if ffn_up is not None:
        add(pl.BlockSpec((DEPTH, up_rows, D_FF), lambda b: (0, b, 0)), ffn_up)
        add(pl.BlockSpec((DEPTH, up_rows, D_FF), lambda b: (0, b, 1)), ffn_up)

    aliases = {}
    out_shape = [jax.ShapeDtypeStruct((nb * nc * CHUNK, D_MODEL), F32)]
    out_specs = [pl.BlockSpec((blk, D_MODEL), on_grid(local))]
    state_shape = (GW, GW) if latent else (CTX_SEQS, GW, GW)
    scratch = [pltpu.VMEM(state_shape, F32), pltpu.VMEM(state_shape, F32),
               pltpu.VMEM((nc * CHUNK, GW), F32),
               pltpu.VMEM((group, CHUNK + 2 * POOL_HALO, GW), F32),
               pltpu.VMEM((blk, D_MODEL), F32)]
    if latent:
        scratch += [pltpu.VMEM(((nc + 2) * CHUNK, 128), BF16), pltpu.VMEM(((nc + 2) * CHUNK, 128), BF16)]
    else:
        out_shape.append(jax.ShapeDtypeStruct((BATCH, DEPTH, 2, N_HEADS, HEAD_DIM, HEAD_DIM), F32))
        out_specs.append(pl.BlockSpec((CTX_SEQS, None, 2, N_HEADS, HEAD_DIM, HEAD_DIM),
                                      lambda b: (b, l, 0, 0, 0, 0)))
        for _ in range(2):
            out_shape.append(jax.ShapeDtypeStruct((BATCH, DEPTH, SEQ, 128), F32))
            out_specs.append(pl.BlockSpec((CTX_SEQS, None, SEQ, 128), lambda b: (b, l, 0, 0)))
        if stacked is not None:
            aliases = {len(args) + i: 1 + i for i in range(3)}
            for arr in stacked:
                add(pl.BlockSpec(memory_space=pl.ANY), arr)
        if ffn_up is not None:
            out_shape.append(jax.ShapeDtypeStruct((DEPTH, NFC, D_MODEL, 2 * FC), BF16))
            out_specs.append(pl.BlockSpec((DEPTH, NFC, up_rows, 2 * FC), lambda b: (0, 0, b, 0)))

    return pl.pallas_call(
        functools.partial(_mixer_kernel, latent=latent, group=group, ng=ng, cast_up=ffn_up is not None),
        grid=(nb, 2, ng) if latent else (nb,),
        in_specs=specs,
        out_specs=out_specs,
        out_shape=out_shape,
        scratch_shapes=scratch,
        input_output_aliases=aliases,
        compiler_params=pltpu.CompilerParams(
            dimension_semantics=("arbitrary",) * (3 if latent else 1), vmem_limit_bytes=VMEM_LIMIT),
        name="mixer_latent" if latent else "mixer_context",
    )(*args)


def _pad_rows(rows, n=8):
    a = jnp.stack(rows)
    return jnp.concatenate([a, jnp.zeros((n - a.shape[0],) + a.shape[1:], a.dtype)], axis=0)


def _block_diag(blocks):
    g, n, _ = blocks.shape
    eye = jnp.eye(g, dtype=blocks.dtype)
    return (eye[:, None, :, None] * blocks[:, :, None, :]).reshape(g * n, g * n)


def _inv_count(n, seqs=1):
    t = np.arange(n)
    cols = []
    for w in POOL_WINDOWS:
        cnt = np.clip(t + w // 2, 0, n) - np.clip(t - w // 2, 0, n)
        cols.append(np.repeat((1.0 / cnt)[:, None], HEAD_DIM, axis=1))
    return jnp.asarray(np.tile(np.concatenate(cols, axis=1), (seqs, 1)), F32)


def _rope_tables():
    rows = DEC_SEQ // GRID_W
    r, cc = jnp.meshgrid(jnp.arange(rows), jnp.arange(GRID_W), indexing="ij")
    half = HEAD_DIM // 2
    freqs = ROPE_BASE ** (-jnp.arange(0, half, 2, dtype=F32) / half)

    def tables(pos):
        ang = pos.reshape(-1).astype(F32)[:, None] * freqs[None, :]
        cos, sin = jnp.cos(ang), jnp.sin(ang)
        return jnp.concatenate([cos, cos], axis=1), jnp.concatenate([-sin, sin], axis=1)

    cr, sr = tables(r)
    ccol, scol = tables(cc)
    cos = jnp.concatenate([cr, ccol], axis=1)
    sin = jnp.concatenate([sr, scol], axis=1)
    return jnp.tile(cos, (1, 2)), jnp.tile(sin, (1, 2))


def _layer_tables(attn_sink, sgu_norm_w, sgu_norm_b, sgu_ws, sgu_bs, ret_decay, ret_gn_w, pool_w, pool_scale):
    log_g = jax.nn.log_sigmoid(ret_decay.astype(F32))
    i = jnp.arange(CHUNK, dtype=F32)
    rel = i[:, None] - i[None, :]
    kscale = HEAD_DIM ** -0.5
    d_f = jnp.where(rel >= 0, jnp.exp(jnp.maximum(rel, 0.0)[None] * log_g[0][:, None, None]), 0.0)
    d_b = jnp.where(rel <= 0, jnp.exp(jnp.maximum(-rel, 0.0)[None] * log_g[1][:, None, None]), 0.0)
    dmat = jnp.stack([d_f.reshape(N_HEADS * CHUNK, CHUNK), d_b.reshape(N_HEADS * CHUNK, CHUNK)]) * kscale

    def lanes(per_head):
        return jnp.repeat(per_head, HEAD_DIM, axis=1)

    qdec_f = lanes(jnp.exp((i + 1.0)[:, None] * log_g[0][None, :]))
    qdec_b = lanes(jnp.exp((CHUNK - i)[:, None] * log_g[1][None, :]))
    kdec_f = lanes(jnp.exp((CHUNK - 1.0 - i)[:, None] * log_g[0][None, :])) * kscale
    kdec_b = lanes(jnp.exp(i[:, None] * log_g[1][None, :])) * kscale
    cdec = jnp.repeat(jnp.exp(CHUNK * log_g), HEAD_DIM, axis=1)
    vec = _pad_rows([sgu_norm_w, sgu_norm_b, pool_scale, ret_gn_w[0], ret_gn_w[1], cdec[0], cdec[1]])
    return {
        "dmat": dmat,
        "dec": jnp.stack([qdec_f, qdec_b, kdec_f, kdec_b]),
        "sink": jnp.broadcast_to(jnp.repeat(attn_sink, CHUNK)[:, None], (N_HEADS * CHUNK, 128)),
        "ws": jnp.concatenate([sgu_ws[h] for h in range(N_HEADS)], axis=1).astype(BF16),
        "bias": jnp.repeat(sgu_bs.T, HEAD_DIM, axis=1),
        "vec": vec,
        "wpool": _block_diag(pool_w).astype(BF16),
    }


def kernel(x_prompt, x_sample, cache_attn_k, cache_attn_v, state_ret, c, c_ctx, w_ada, b_ada, w_in,
           w_out, attn_sink, sgu_norm_w, sgu_norm_b, sgu_ws, sgu_bs, ret_decay, ret_gn_w, pool_w,
           pool_scale, ffn_up, ffn_conv_w, ffn_conv_b, ffn_down, ln_w, ln_b):
    cond8 = jnp.concatenate([c_ctx[None], c, jnp.zeros((8 - 1 - DEC_BATCH, D_MODEL), F32)], axis=0)
    mod = _modulation(cond8, w_ada, b_ada).reshape(DEPTH, 8, 6, D_MODEL)

    tabs = jax.vmap(_layer_tables)(attn_sink, sgu_norm_w, sgu_norm_b, sgu_ws, sgu_bs, ret_decay, ret_gn_w,
                                   pool_w, pool_scale)
    tabs["gmat"] = _block_diag(jnp.full((N_HEADS, HEAD_DIM, HEAD_DIM), 1.0 / HEAD_DIM, F32)).astype(BF16)
    tabs["cnt_ctx"] = _inv_count(SEQ, CTX_SEQS)
    tabs["cnt_lat"] = _inv_count(DEC_SEQ)
    cos, sin = _rope_tables()
    extra = {"ck": cache_attn_k.reshape(DEC_BATCH, DEPTH, PAST_LEN, 128),
             "cv": cache_attn_v.reshape(DEC_BATCH, DEPTH, PAST_LEN, 128),
             "cos": cos, "sin": sin, "s0": state_ret}
    ln8 = jnp.concatenate([ln_w, ln_b, jnp.zeros((DEPTH, 4, D_MODEL), F32)], axis=1)
    conv8 = jnp.concatenate([ffn_conv_w, ffn_conv_b[:, None], jnp.zeros((DEPTH, 4, 2 * D_FF), F32)], axis=1)


    xs = [x_prompt.reshape(N_CTX, D_MODEL), x_sample.reshape(N_LAT, D_MODEL)]
    for l in range(DEPTH):
        if l == 0:
            z, w_out_bf16 = _inproj(xs, mod, w_in, l, w_out=w_out)
        else:
            (z,) = _inproj(xs, mod, w_in, l)
        if l == 0:
            x1_ctx, st, kn, vn, up_chunks = _mixer(z, xs[0], mod, w_out_bf16, ln8, tabs, l, latent=False,
                                                   ffn_up=ffn_up)
        else:
            x1_ctx, st, kn, vn = _mixer(z, xs[0], mod, w_out_bf16, ln8, tabs, l, latent=False,
                                        stacked=(st, kn, vn))
        (x1_lat,) = _mixer(z, xs[1], mod, w_out_bf16, ln8, tabs, l, latent=True, extra=extra)
        xs = _ffn([x1_ctx, x1_lat], mod, up_chunks, conv8, ffn_down, ln8, l)

    y_prompt = xs[0].reshape(BATCH, SEQ, D_MODEL)
    y_sample = xs[1].reshape(DEC_BATCH, DEC_SEQ, D_MODEL)
    return (y_prompt, y_sample, kn.reshape(BATCH, DEPTH, SEQ, 2, HEAD_DIM),
            vn.reshape(BATCH, DEPTH, SEQ, 2, HEAD_DIM), st)
```

```python
import functools

import numpy as np
import jax
import jax.numpy as jnp
from jax import lax
from jax.experimental import pallas as pl
from jax.experimental.pallas import tpu as pltpu

F32 = jnp.float32
BF16 = jnp.bfloat16

D_MODEL = 1024
BATCH = 16
SEQ = 256
DEPTH = 2
DEC_BATCH = 2
DEC_SEQ = 2048
PAST_LEN = 256
GRID_W = 64
CHUNK = 128
HEAD_DIM = 64
GW = D_MODEL // 4
N_HEADS = 4
POOL_WINDOWS = (2, 4, 8, 16)
POOL_HALO = 8
D_FF = 2816
ROPE_BASE = 10000.0
LN_EPS = 1e-5
NEG_INF = -1e30
IN_WIDTH = 10 * GW
ALPHA = (2.0 * DEPTH) ** 0.25

N_CTX = BATCH * SEQ
N_LAT = DEC_BATCH * DEC_SEQ
ROWS = N_CTX + N_LAT

TM = 1024
NB_IN = 512
FC = 256
NB_ADA = 1536
VMEM_LIMIT = 56 * 1024 * 1024


def _cond_of_tile(i, tm=TM):
    ctx_tiles = N_CTX // tm
    return jnp.where(i < ctx_tiles, 0, 1 + (i - ctx_tiles) // (DEC_SEQ // tm))


def _tile_specs(tm):
    ctx_tiles = N_CTX // tm
    return [pl.BlockSpec((tm, D_MODEL), lambda i: (jnp.minimum(i, ctx_tiles - 1), 0)),
            pl.BlockSpec((tm, D_MODEL), lambda i: (jnp.maximum(i - ctx_tiles, 0), 0))]


def _per_half(i, tm, fn):
    ctx_tiles = N_CTX // tm

    @pl.when(i < ctx_tiles)
    def _():
        fn(0)

    @pl.when(i >= ctx_tiles)
    def _():
        fn(1)


def _layer_norm(x, w, b):
    mu = jnp.mean(x, axis=-1, keepdims=True)
    d = x - mu
    var = jnp.mean(d * d, axis=-1, keepdims=True)
    return d * lax.rsqrt(var + LN_EPS) * w + b


def _silu(x):
    return x * jax.nn.sigmoid(x)


def _bdot(a, b):
    return jnp.dot(a.astype(BF16), b.astype(BF16), preferred_element_type=F32)


def _mod_kernel(c_ref, w_ref, b_ref, o_ref):
    o_ref[...] = _bdot(_silu(c_ref[...]), w_ref[...]) + b_ref[...]


def _modulation(cond8, w_ada, b_ada):
    return pl.pallas_call(
        _mod_kernel,
        grid=(DEPTH, 6 * D_MODEL // NB_ADA),
        in_specs=[
            pl.BlockSpec((8, D_MODEL), lambda l, j: (0, 0)),
            pl.BlockSpec((None, D_MODEL, NB_ADA), lambda l, j: (l, 0, j)),
            pl.BlockSpec((None, 1, NB_ADA), lambda l, j: (l, 0, j)),
        ],
        out_specs=pl.BlockSpec((None, 8, NB_ADA), lambda l, j: (l, 0, j)),
        out_shape=jax.ShapeDtypeStruct((DEPTH, 8, 6 * D_MODEL), F32),
        compiler_params=pltpu.CompilerParams(
            dimension_semantics=("arbitrary", "arbitrary"), vmem_limit_bytes=VMEM_LIMIT),
        name="modulation",
    )(cond8, w_ada, b_ada.reshape(DEPTH, 1, 6 * D_MODEL))


def _inproj_kernel(*refs, cast_w_out):
    if cast_w_out:
        xc_ref, xl_ref, mod_ref, w_ref, wo_ref, z_ref, wo_bf16_ref, h_scr = refs
    else:
        xc_ref, xl_ref, mod_ref, w_ref, z_ref, h_scr = refs
    x_refs = (xc_ref, xl_ref)
    m = mod_ref[...]

    def project(side):
        for rh in range(2):
            rows = slice(rh * TM // 2, (rh + 1) * TM // 2)
            h_scr[rows, :] = (x_refs[side][rows, :] * (1.0 + m[1:2]) + m[0:1]).astype(BF16)
            for jb in range(IN_WIDTH // NB_IN):
                cols = slice(jb * NB_IN, (jb + 1) * NB_IN)
                z_ref[rows, cols] = jnp.dot(h_scr[rows, :], w_ref[:, cols].astype(BF16),
                                            preferred_element_type=F32)
            if cast_w_out and rh == 0:
                wo_bf16_ref[...] = wo_ref[...].astype(BF16)

    _per_half(pl.program_id(0), TM, project)


def _inproj(xs, mod, w_in, l, w_out=None):
    steps = ROWS // TM
    in_specs = _tile_specs(TM) + [
        pl.BlockSpec((None, None, 6, D_MODEL), lambda i: (l, _cond_of_tile(i), 0, 0)),
        pl.BlockSpec((None, D_MODEL, IN_WIDTH), lambda i: (l, 0, 0), pipeline_mode=pl.Buffered(1)),
    ]
    out_specs = [pl.BlockSpec((TM, IN_WIDTH), lambda i: (i, 0))]
    out_shape = [jax.ShapeDtypeStruct((ROWS, IN_WIDTH), F32)]
    args = [*xs, mod, w_in]
    if w_out is not None:
        rows_spec = pl.BlockSpec((DEPTH, D_MODEL // steps, D_MODEL), lambda i: (0, i, 0))
        in_specs.append(rows_spec)
        out_specs.append(rows_spec)
        out_shape.append(jax.ShapeDtypeStruct((DEPTH, D_MODEL, D_MODEL), BF16))
        args.append(w_out)
    return pl.pallas_call(
        functools.partial(_inproj_kernel, cast_w_out=w_out is not None),
        grid=(steps,),
        in_specs=in_specs,
        out_specs=out_specs,
        out_shape=out_shape,
        scratch_shapes=[pltpu.VMEM((TM, D_MODEL), BF16)],
        compiler_params=pltpu.CompilerParams(
            dimension_semantics=("arbitrary",), vmem_limit_bytes=VMEM_LIMIT),
        name="inproj",
    )(*args)


TMF = 512
SEGF = TMF // 8
PITCHF = SEGF + 8
CTXF_TILES = N_CTX // TMF
LATF_PER_SEQ = DEC_SEQ // TMF
HALO_ROWS = 16
RBUF = 256
RB = 256
GB = 64
NFC = D_FF // FC
LANE_BLOCKS = D_MODEL // 128
assert RB == RBUF


def _seg_rows_f(xc_ref, k):
    return jnp.concatenate([xc_ref[cb, pl.ds(k, 8, stride=PITCHF), :] for cb in range(LANE_BLOCKS)], axis=1)


def _ffn_kernel(xc_ref, xl_ref, xp_ref, xn_ref, mod_ref, up_ref, cv_ref, dn_ref, ln_ref, oc_ref, ol_ref,
                h_scr, act_scr, xc_scr, u0_scr, u1_scr):
    x_refs = (xc_ref, xl_ref)
    o_refs = (oc_ref, ol_ref)
    u_scrs = (u0_scr, u1_scr)
    i = pl.program_id(0)
    is_ctx = i < CTXF_TILES
    lat_pos = (i - CTXF_TILES) % LATF_PER_SEQ
    m = mod_ref[...]
    ln = ln_ref[...]

    scale = 1.0 + m[4:5]
    shift = m[3:4]

    def stage(side):
        for cb in range(LANE_BLOCKS):
            for s in range(8):
                xc_scr[cb, s * PITCHF:s * PITCHF + SEGF, :] = x_refs[side][s * SEGF:(s + 1) * SEGF,
                                                                            cb * 128:(cb + 1) * 128]

    _per_half(i, TMF, stage)

    def build_h(ub):
        for k in range(ub * RBUF // 8, (ub + 1) * RBUF // 8, 2):
            rows = jnp.concatenate([_seg_rows_f(xc_scr, k), _seg_rows_f(xc_scr, k + 1)], axis=0)
            h_scr[8 * k:8 * k + 16, :] = (rows * scale + shift).astype(BF16)

    sub16 = lax.broadcasted_iota(jnp.int32, (HALO_ROWS, D_MODEL), 0)
    prev_ok = jnp.logical_not(is_ctx) & (lat_pos > 0)
    next_ok = jnp.logical_not(is_ctx) & (lat_pos < LATF_PER_SEQ - 1)
    halo_x = jnp.where(sub16 == 0, xp_ref[POOL_HALO - 1:POOL_HALO, :], xn_ref[0:1, :])
    keep = ((sub16 == 0) & prev_ok) | ((sub16 == 1) & next_ok)
    h_scr[TMF:TMF + HALO_ROWS, :] = jnp.where(keep, halo_x * scale + shift, 0.0).astype(BF16)

    n_ub = TMF // RBUF
    sub = lax.broadcasted_iota(jnp.int32, (8, FC), 0)
    seg_per_seq = SEQ // SEGF
    ctx_first = is_ctx & (sub % seg_per_seq == 0)
    ctx_last = is_ctx & (sub % seg_per_seq == seg_per_seq - 1)

    def up_proj(slot, c, ub):
        rows = slice(ub * RBUF, (ub + 1) * RBUF + (HALO_ROWS if ub == n_ub - 1 else 0))
        u_scrs[slot][rows, :] = jnp.dot(h_scr[rows, :], up_ref[c], preferred_element_type=F32)

    def conv(u_ref, lanes, cvs, r0):
        lo = max(r0 - 8, 0)
        hi = min(r0 + GB + 8, TMF)
        ue = u_ref[lo:hi, lanes]
        u = ue[r0 - lo:r0 - lo + GB]
        if r0 == 0:
            b_first = jnp.where(sub == 0, u_ref[TMF:TMF + 1, lanes],
                                pltpu.roll(u_ref[TMF - 8:TMF, lanes], 1, axis=0))
            um1 = jnp.concatenate([jnp.where(ctx_first, 0.0, b_first), u[0:GB - 8]], axis=0)
        else:
            um1 = ue[0:GB]
        if r0 == TMF - GB:
            b_last = jnp.where(sub == 7, u_ref[TMF + 1:TMF + 2, lanes],
                               pltpu.roll(u_ref[0:8, lanes], 7, axis=0))
            up1 = jnp.concatenate([u[8:GB], jnp.where(ctx_last, 0.0, b_last)], axis=0)
        else:
            up1 = ue[r0 - lo + 8:r0 - lo + GB + 8]
        return um1 * cvs[0:1] + u * cvs[1:2] + up1 * cvs[2:3] + cvs[3:4]

    def gate(slot, c, ub):
        cva = cv_ref[0:4, c * FC:(c + 1) * FC]
        cvg = cv_ref[0:4, D_FF + c * FC:D_FF + (c + 1) * FC]
        for r0 in range(ub * RBUF, (ub + 1) * RBUF, GB):
            a = conv(u_scrs[slot], slice(0, FC), cva, r0)
            g = conv(u_scrs[slot], slice(FC, 2 * FC), cvg, r0)
            act_scr[c, r0:r0 + GB, :] = (_silu(a) * g).astype(BF16)

    def finish(rb):
        vrows = range(rb * RB // 8, (rb + 1) * RB // 8)
        lhs = jnp.concatenate([act_scr[c, rb * RB:(rb + 1) * RB, :] for c in range(NFC)], axis=1)
        y = jnp.dot(lhs, dn_ref[...].astype(BF16), preferred_element_type=F32)
        xr = jnp.concatenate([_seg_rows_f(xc_scr, k) for k in vrows], axis=0)
        out = _layer_norm(ALPHA * xr + m[5:6] * y, ln[1:2], ln[3:4])
        for kk, k in enumerate(vrows):
            for cb in range(LANE_BLOCKS):
                xc_scr[cb, pl.ds(k, 8, stride=PITCHF), :] = out[8 * kk:8 * kk + 8, cb * 128:(cb + 1) * 128]

    def write_out(side):
        for cb in range(LANE_BLOCKS):
            for s in range(8):
                o_refs[side][s * SEGF:(s + 1) * SEGF, cb * 128:(cb + 1) * 128] = xc_scr[
                    cb, s * PITCHF:s * PITCHF + SEGF, :]

    build_h(0)
    for c in range(NFC + 1):
        for ub in range(n_ub):
            if c == 0 and ub > 0:
                build_h(ub)
            if c < NFC:
                up_proj(c % 2, c, ub)
            if c >= 1:
                gate((c - 1) % 2, c - 1, ub)
            if c == NFC:
                finish(ub)

    _per_half(i, TMF, write_out)


def _ffn(xs, mod, up_chunks, conv8, down, ln8, l):
    halo_blocks = TMF // POOL_HALO
    last_halo = N_LAT // POOL_HALO - 1
    return pl.pallas_call(
        _ffn_kernel,
        grid=(ROWS // TMF,),
        in_specs=[
            *_tile_specs(TMF),
            pl.BlockSpec((POOL_HALO, D_MODEL),
                         lambda i: (jnp.maximum((i - CTXF_TILES) * halo_blocks - 1, 0), 0)),
            pl.BlockSpec((POOL_HALO, D_MODEL),
                         lambda i: (jnp.clip((i - CTXF_TILES + 1) * halo_blocks, 0, last_halo), 0)),
            pl.BlockSpec((None, None, 6, D_MODEL), lambda i: (l, _cond_of_tile(i, TMF), 0, 0)),
            pl.BlockSpec((None, NFC, D_MODEL, 2 * FC), lambda i: (l, 0, 0, 0), pipeline_mode=pl.Buffered(1)),
            pl.BlockSpec((None, 8, 2 * D_FF), lambda i: (l, 0, 0)),
            pl.BlockSpec((None, D_FF, D_MODEL), lambda i: (l, 0, 0), pipeline_mode=pl.Buffered(1)),
            pl.BlockSpec((None, 8, D_MODEL), lambda i: (l, 0, 0)),
        ],
        out_specs=_tile_specs(TMF),
        out_shape=[jax.ShapeDtypeStruct((N_CTX, D_MODEL), F32), jax.ShapeDtypeStruct((N_LAT, D_MODEL), F32)],
        scratch_shapes=[pltpu.VMEM((TMF + HALO_ROWS, D_MODEL), BF16),
                        pltpu.VMEM((NFC, TMF, FC), BF16),
                        pltpu.VMEM((LANE_BLOCKS, 8 * PITCHF, 128), F32),
                        pltpu.VMEM((TMF + HALO_ROWS, 2 * FC), F32),
                        pltpu.VMEM((TMF + HALO_ROWS, 2 * FC), F32)],
        compiler_params=pltpu.CompilerParams(
            dimension_semantics=("arbitrary",), vmem_limit_bytes=VMEM_LIMIT),
        name="convffn",
    )(xs[0], xs[1], xs[1], xs[1], mod, up_chunks, conv8, down, ln8)


def _head_masks(width):
    lane = lax.broadcasted_iota(jnp.int32, (1, width), 1)
    return [(lane >= h * HEAD_DIM) & (lane < (h + 1) * HEAD_DIM) for h in range(width // HEAD_DIM)]


def _stack_heads(x, masks):
    return jnp.concatenate([jnp.where(m, x, jnp.zeros_like(x)) for m in masks], axis=0)


def _rope(x, cos, sin):
    lane = lax.broadcasted_iota(jnp.int32, (1, 128), 1)
    lower = (lane & 31) < 16
    outs = []
    for k in range(x.shape[1] // 128):
        xb = x[:, k * 128:(k + 1) * 128]
        partner = jnp.where(lower, pltpu.roll(xb, 112, axis=1), pltpu.roll(xb, 16, axis=1))
        outs.append(xb * cos + partner * sin)
    return outs[0] if len(outs) == 1 else jnp.concatenate(outs, axis=1)


def _load_state(state_scr, blocks_ref, d):
    state_scr[...] = jnp.zeros((GW, GW), F32)
    for h in range(N_HEADS):
        sl = slice(h * HEAD_DIM, (h + 1) * HEAD_DIM)
        state_scr[sl, sl] = blocks_ref[d, h]


def _store_state(blocks_ref, d, state_scr):
    for h in range(N_HEADS):
        sl = slice(h * HEAD_DIM, (h + 1) * HEAD_DIM)
        blocks_ref[d, h] = state_scr[sl, sl]


LAT_GROUP = 4
CTX_SEQS = 2


def _chunk_off(c):
    return c * CHUNK if isinstance(c, int) else pl.multiple_of(c * CHUNK, CHUNK)


def _interleave(chains):
    chains = list(chains)
    while chains:
        for ch in list(chains):
            try:
                next(ch)
            except StopIteration:
                chains.remove(ch)


def _mixer_kernel(*refs, latent, group, ng, cast_up=False):
    if latent:
        (qa_ref, ka_ref, va_ref, kx_ref, vx_ref, ub_ref, vb_ref, qc_ref, kc_ref, vc_ref, gf_ref, gb_ref,
         pd_ref, pdp_ref, pdn_ref, cos_ref, sin_ref, s0_ref, dmat_ref, dec_ref, sink_ref, ws_ref, bias_ref,
         vec_ref, gmat_ref, cnt_ref, wpool_ref, xres_ref, mod_ref, wout_ref, ln_ref,
         x1_ref, sf_scr, sb_scr, ob_scr, pext_scr, mix_ref, k_scr, v_scr) = refs
    else:
        (qa_ref, ka_ref, va_ref, ub_ref, vb_ref, qc_ref, kc_ref, vc_ref, gf_ref, gb_ref, pd_ref,
         dmat_ref, dec_ref, sink_ref, ws_ref, bias_ref, vec_ref, gmat_ref, cnt_ref, wpool_ref,
         xres_ref, mod_ref, wout_ref, ln_ref) = refs[:24]
        if cast_up:
            upa_ref, upg_ref, x1_ref, st_ref, kn_ref, vn_ref, upo_ref = refs[24:31]
        else:
            x1_ref, st_ref, kn_ref, vn_ref = refs[-9:-5]
        sf_scr, sb_scr, ob_scr, pext_scr, mix_ref = refs[-5:]

    nc = group * ng
    cps = group if latent else SEQ // CHUNK
    g_a = mod_ref[...][2:3]
    ln = ln_ref[...]
    masks = _head_masks(GW)
    row = lax.broadcasted_iota(jnp.int32, (GW, GW), 0)
    col = lax.broadcasted_iota(jnp.int32, (GW, GW), 1)
    bd_mask = (row // HEAD_DIM) == (col // HEAD_DIM)
    vec = vec_ref[...]
    lane = lax.broadcasted_iota(jnp.int32, (1, 2 * HEAD_DIM), 1)
    lo = lane < HEAD_DIM

    def rows(k):
        return slice(k * CHUNK, (k + 1) * CHUNK)

    def retention(order, d, state_scr, out):
        dmat = dmat_ref[d]
        qdec, kdec = dec_ref[d], dec_ref[2 + d]
        cdec = vec[5 + d:6 + d]
        state = state_scr[...]
        for k in order:
            q, kk, v = qc_ref[rows(k), :], kc_ref[rows(k), :], vc_ref[rows(k), :]
            qb, kb, vb = q.astype(BF16), kk.astype(BF16), v.astype(BF16)
            s = lax.dot_general(_stack_heads(qb, masks), kb, (((1,), (1,)), ((), ())),
                                preferred_element_type=F32)
            yield
            p = (s * dmat).astype(BF16)
            p_cat = jnp.concatenate([p[h * CHUNK:(h + 1) * CHUNK] for h in range(N_HEADS)], axis=1)
            o = (jnp.dot(p_cat, _stack_heads(vb, masks), preferred_element_type=F32)
                 + _bdot(q * qdec, state))
            yield
            upd = lax.dot_general((kk * kdec).astype(BF16), vb, (((0,), (0,)), ((), ())),
                                  preferred_element_type=F32)
            state = state * cdec + jnp.where(bd_mask, upd, 0.0)
            out[k] = o
            yield
        state_scr[...] = state

    def stage_kv(k, c):
        dst = pl.ds(_chunk_off(c + 1), CHUNK)
        k_scr[dst, :] = _rope(ka_ref[rows(k), :], cos_ref[rows(k), :], sin_ref[rows(k), :]).astype(BF16)
        yield
        v_scr[dst, :] = va_ref[rows(k), :].astype(BF16)
        yield

    def attention(k, c, done):
        q = qa_ref[rows(k), :]
        if latent:
            q = _rope(q, cos_ref[rows(k), :], sin_ref[rows(k), :])
        q = q * (HEAD_DIM ** -0.5)
        q0, q1 = q[:, :128], q[:, 128:]
        zero = jnp.zeros_like(q0)
        q_st = jnp.concatenate([
            jnp.where(lo, q0, zero),
            jnp.where(lo, pltpu.roll(q0, HEAD_DIM, axis=1), zero),
            jnp.where(lo, zero, pltpu.roll(q1, HEAD_DIM, axis=1)),
            jnp.where(lo, zero, q1)], axis=0).astype(BF16)
        if latent:
            band = pl.ds(_chunk_off(c), 3 * CHUNK)
            k_all = jnp.concatenate([k_scr[band, :], kx_ref[...].astype(BF16)], axis=0)
            v_all = jnp.concatenate([v_scr[band, :], vx_ref[...].astype(BF16)], axis=0)
        else:
            own = slice(k // cps * SEQ, (k // cps + 1) * SEQ)
            k_all = ka_ref[own, :].astype(BF16)
            v_all = va_ref[own, :].astype(BF16)
        s = lax.dot_general(q_st, k_all, (((1,), (1,)), ((), ())), preferred_element_type=F32)
        yield
        if latent:
            nk = 3 * CHUNK + PAST_LEN
            qi = lax.broadcasted_iota(jnp.int32, (N_HEADS * CHUNK, nk), 0) & (CHUNK - 1)
            kj = lax.broadcasted_iota(jnp.int32, (N_HEADS * CHUNK, nk), 1)
            kpos = kj + (c - 1) * CHUNK
            valid = (kj >= 3 * CHUNK) | ((kj >= qi) & (kj <= qi + 2 * CHUNK)
                                         & (kpos >= 0) & (kpos < nc * CHUNK))
            s = jnp.where(valid, s, NEG_INF)
        sink = sink_ref[...][:, 0:1]
        mx = jnp.maximum(jnp.max(s, axis=-1, keepdims=True), sink)
        yield
        e = jnp.exp(s - mx)
        den = jnp.sum(e, axis=-1, keepdims=True) + jnp.exp(sink - mx)
        yield
        o = jnp.dot(e.astype(BF16), v_all, preferred_element_type=F32) / den
        yield
        mix_ref[rows(k), 0:128] = jnp.where(lo, o[0:CHUNK], pltpu.roll(o[CHUNK:2 * CHUNK], HEAD_DIM, axis=1))
        mix_ref[rows(k), 128:256] = jnp.where(lo, pltpu.roll(o[2 * CHUNK:3 * CHUNK], HEAD_DIM, axis=1),
                                              o[3 * CHUNK:4 * CHUNK])
        done.add(("A", k))
        yield

    def gating_unit(k, done):
        vn = _layer_norm(vb_ref[rows(k), :], vec[0:1], vec[1:2]).astype(BF16)
        yield
        sg = jnp.dot(ws_ref[...], _stack_heads(vn, masks), preferred_element_type=F32) + bias_ref[...]
        yield
        mix_ref[rows(k), GW:2 * GW] = ub_ref[rows(k), :] * sg
        done.add(("B", k))
        yield

    def retention_mix(k, o_fwd, o_bwd, done):
        while k not in o_fwd or k not in o_bwd:
            yield
        gmat = gmat_ref[...]
        normed = []
        for o in (o_fwd[k], o_bwd[k]):
            mu = jnp.dot(o.astype(BF16), gmat, preferred_element_type=F32)
            yield
            dlt = o - mu
            var = jnp.dot((dlt * dlt).astype(BF16), gmat, preferred_element_type=F32)
            yield
            normed.append(dlt * lax.rsqrt(var + LN_EPS))
        mix_ref[rows(k), 2 * GW:3 * GW] = (_silu(gf_ref[rows(k), :]) * (normed[0] * vec[3:4])
                                           + _silu(gb_ref[rows(k), :]) * (normed[1] * vec[4:5]))
        done.add(("C", k))
        yield

    def pooling(k, c, done):
        pd = pd_ref[rows(k), :]
        pext = pext_scr.at[k]
        zeros8 = jnp.zeros((POOL_HALO, GW), F32)
        if k % cps > 0:
            prev8 = pd_ref[k * CHUNK - POOL_HALO:k * CHUNK, :]
        else:
            prev8 = jnp.where(c > 0, pdp_ref[...], 0.0) if latent else zeros8
        if k % cps < cps - 1:
            next8 = pd_ref[(k + 1) * CHUNK:(k + 1) * CHUNK + POOL_HALO, :]
        else:
            next8 = jnp.where(c < nc - 1, pdn_ref[...], 0.0) if latent else zeros8
        pext[0:POOL_HALO, :] = prev8
        pext[POOL_HALO:POOL_HALO + CHUNK, :] = pd
        pext[POOL_HALO + CHUNK:2 * POOL_HALO + CHUNK, :] = next8
        yield

        def win(d, half):
            return pext[pl.ds(POOL_HALO + d, CHUNK), half * 128:(half + 1) * 128]

        a2 = win(-1, 0) + win(0, 0)
        a4 = a2 + win(-2, 0) + win(1, 0)
        yield
        a8 = win(-4, 1)
        for d in range(-3, 4):
            a8 = a8 + win(d, 1)
        yield
        a16 = a8
        for d in list(range(-8, -4)) + list(range(4, 8)):
            a16 = a16 + win(d, 1)
        yield
        sums = jnp.concatenate([jnp.where(lo, a2, a4), jnp.where(lo, a8, a16)], axis=1)
        yd = sums * cnt_ref[rows(k), :] - pd
        mix_ref[rows(k), 3 * GW:4 * GW] = _bdot(yd, wpool_ref[...]) * vec[2:3]
        done.add(("D", k))
        yield

    def out_projection(k, done):
        while not all((m, k) in done for m in "ABCD"):
            yield
        y = jnp.dot(mix_ref[rows(k), :].astype(BF16), wout_ref[...], preferred_element_type=F32)
        yield
        x1_ref[rows(k), :] = _layer_norm(ALPHA * xres_ref[rows(k), :] + g_a * y, ln[0:1], ln[2:3])
        yield

    def up_weight_rows():
        for lyr in range(DEPTH):
            for cc in range(NFC):
                upo_ref[lyr, cc, :, 0:FC] = upa_ref[lyr, :, cc * FC:(cc + 1) * FC].astype(BF16)
                upo_ref[lyr, cc, :, FC:2 * FC] = upg_ref[lyr, :, cc * FC:(cc + 1) * FC].astype(BF16)
                yield

    def forward_chains(chunk_of, o_bwd, fwd_states):
        done, o_fwd = set(), {}
        chains = [retention(ks, 0, state, o_fwd) for ks, state in fwd_states]
        for k in range(group):
            chains += [attention(k, chunk_of(k), done), gating_unit(k, done), pooling(k, chunk_of(k), done),
                       retention_mix(k, o_fwd, o_bwd, done), out_projection(k, done)]
        return chains

    if not latent:
        sb_scr[...] = jnp.zeros((CTX_SEQS, GW, GW), F32)
        sf_scr[...] = jnp.zeros((CTX_SEQS, GW, GW), F32)
        seq_chunks = [range(sq * cps, (sq + 1) * cps) for sq in range(CTX_SEQS)]
        o_bwd = {}
        side_chains = [retention(reversed(ks), 1, sb_scr.at[sq], o_bwd) for sq, ks in enumerate(seq_chunks)]
        if cast_up:
            side_chains.append(up_weight_rows())
        _interleave(side_chains + forward_chains(lambda k: k % cps, o_bwd,
                                                 [(ks, sf_scr.at[sq]) for sq, ks in enumerate(seq_chunks)]))
        for sq in range(CTX_SEQS):
            kn_ref[sq] = ka_ref[sq * SEQ:(sq + 1) * SEQ, :]
            vn_ref[sq] = va_ref[sq * SEQ:(sq + 1) * SEQ, :]
            _store_state(st_ref.at[sq], 1, sb_scr.at[sq])
            _store_state(st_ref.at[sq], 0, sf_scr.at[sq])
        return

    p = pl.program_id(1)
    g = pl.program_id(2)

    @pl.when(p == 0)
    def _():
        @pl.when(g == 0)
        def _():
            _load_state(sb_scr, s0_ref, 1)
            zero_blk = jnp.zeros((CHUNK, 2 * HEAD_DIM), BF16)
            for scr in (k_scr, v_scr):
                scr[0:CHUNK, :] = zero_blk
                scr[(nc + 1) * CHUNK:(nc + 2) * CHUNK, :] = zero_blk

        first = (ng - 1 - g) * group
        o_bwd = {}
        _interleave([retention(reversed(range(group)), 1, sb_scr, o_bwd)]
                    + [stage_kv(k, first + k) for k in range(group)])
        for k in range(group):
            ob_scr[pl.ds(_chunk_off(first + k), CHUNK), :] = o_bwd[k]

    @pl.when(p == 1)
    def _():
        @pl.when(g == 0)
        def _():
            _load_state(sf_scr, s0_ref, 0)

        first = g * group
        o_bwd = {k: ob_scr[pl.ds(_chunk_off(first + k), CHUNK), :] for k in range(group)}
        _interleave(forward_chains(lambda k: first + k, o_bwd, [(range(group), sf_scr)]))


def _mixer(z, x_res, mod, w_out_bf16, ln8, tabs, l, latent, extra=None, ffn_up=None, stacked=None):
    nb = DEC_BATCH if latent else BATCH // CTX_SEQS
    nc = (DEC_SEQ if latent else CTX_SEQS * SEQ) // CHUNK
    group = LAT_GROUP if latent else nc
    ng = nc // group
    blk = group * CHUNK
    base = (N_CTX // blk) if latent else 0
    per8 = blk // POOL_HALO
    last_halo = ROWS // POOL_HALO - 1

    def on_grid(f):
        return (lambda b, p, g: f(b, p, g)) if latent else (lambda b: f(b, 1, 0))

    def fwd(b, p, g):
        return base + b * ng + g * p

    def both(b, p, g):
        return base + b * ng + jnp.where(p == 0, ng - 1 - g, g)

    def bwd_only(b, p, g):
        return base + b * ng + (ng - 1 - g) * (1 - p)

    def col(width, idx, rowmap):
        return pl.BlockSpec((blk, width), on_grid(lambda b, p, g: (rowmap(b, p, g), idx)))

    def const(shape):
        return pl.BlockSpec(shape, on_grid(lambda b, p, g: (0,) * len(shape)))

    def layer(shape):
        return pl.BlockSpec((None,) + shape, on_grid(lambda b, p, g: (l,) + (0,) * len(shape)))

    specs, args = [], []

    def add(spec, arr):
        specs.append(spec)
        args.append(arr)

    add(col(GW, 0, fwd), z)
    add(col(128, 2, bwd_only if latent else fwd), z)
    add(col(128, 3, bwd_only if latent else fwd), z)
    if latent:
        add(pl.BlockSpec((None, None, PAST_LEN, 128), lambda b, p, g: (b, l, 0, 0)), extra["ck"])
        add(pl.BlockSpec((None, None, PAST_LEN, 128), lambda b, p, g: (b, l, 0, 0)), extra["cv"])
    add(col(GW, 2, fwd), z)
    add(col(GW, 3, fwd), z)
    add(col(GW, 4, both), z)
    add(col(GW, 5, both), z)
    add(col(GW, 6, both), z)
    add(col(GW, 7, fwd), z)
    add(col(GW, 8, fwd), z)
    add(col(GW, 9, fwd), z)
    if latent:
        add(pl.BlockSpec((POOL_HALO, GW),
                         lambda b, p, g: (jnp.maximum(fwd(b, p, g) * per8 - 1, 0), 9)), z)
        add(pl.BlockSpec((POOL_HALO, GW),
                         lambda b, p, g: (jnp.minimum((fwd(b, p, g) + 1) * per8, last_halo), 9)), z)
        rope_map = lambda b, p, g: (jnp.where(p == 0, ng - 1 - g, g), 0)
        add(pl.BlockSpec((blk, 128), rope_map), extra["cos"])
        add(pl.BlockSpec((blk, 128), rope_map), extra["sin"])
        add(pl.BlockSpec((None, None, 2, N_HEADS, HEAD_DIM, HEAD_DIM),
                         lambda b, p, g: (b, l, 0, 0, 0, 0)), extra["s0"])
    add(layer((2, N_HEADS * CHUNK, CHUNK)), tabs["dmat"])
    add(layer((4, CHUNK, GW)), tabs["dec"])
    add(layer((N_HEADS * CHUNK, 128)), tabs["sink"])
    add(layer((CHUNK, N_HEADS * CHUNK)), tabs["ws"])
    add(layer((CHUNK, GW)), tabs["bias"])
    add(layer((8, GW)), tabs["vec"])
    add(const((GW, GW)), tabs["gmat"])
    add(pl.BlockSpec((blk, GW), on_grid(lambda b, p, g: (g * p, 0))),
        tabs["cnt_lat"] if latent else tabs["cnt_ctx"])
    add(layer((GW, GW)), tabs["wpool"])
    local = lambda b, p, g: (b * ng + g * p, 0)
    add(pl.BlockSpec((blk, D_MODEL), on_grid(local)), x_res)
    add(pl.BlockSpec((None, None, 6, D_MODEL),
                     on_grid(lambda b, p, g: (l, (1 + b) if latent else 0, 0, 0))), mod)
    add(layer((D_MODEL, D_MODEL)), w_out_bf16)
    add(layer((8, D_MODEL)), ln8)
    up_rows = D_MODEL // nb
    if ffn_up is not None:
        add(pl.BlockSpec((DEPTH, up_rows, D_FF), lambda b: (0, b, 0)), ffn_up)
        add(pl.BlockSpec((DEPTH, up_rows, D_FF), lambda b: (0, b, 1)), ffn_up)

    aliases = {}
    out_shape = [jax.ShapeDtypeStruct((nb * nc * CHUNK, D_MODEL), F32)]
    out_specs = [pl.BlockSpec((blk, D_MODEL), on_grid(local))]
    state_shape = (GW, GW) if latent else (CTX_SEQS, GW, GW)
    scratch = [pltpu.VMEM(state_shape, F32), pltpu.VMEM(state_shape, F32),
               pltpu.VMEM((nc * CHUNK, GW), F32),
               pltpu.VMEM((group, CHUNK + 2 * POOL_HALO, GW), F32),
               pltpu.VMEM((blk, D_MODEL), F32)]
    if latent:
        scratch += [pltpu.VMEM(((nc + 2) * CHUNK, 128), BF16), pltpu.VMEM(((nc + 2) * CHUNK, 128), BF16)]
    else:
        out_shape.append(jax.ShapeDtypeStruct((BATCH, DEPTH, 2, N_HEADS, HEAD_DIM, HEAD_DIM), F32))
        out_specs.append(pl.BlockSpec((CTX_SEQS, None, 2, N_HEADS, HEAD_DIM, HEAD_DIM),
                                      lambda b: (b, l, 0, 0, 0, 0)))
        for _ in range(2):
            out_shape.append(jax.ShapeDtypeStruct((BATCH, DEPTH, SEQ, 128), F32))
            out_specs.append(pl.BlockSpec((CTX_SEQS, None, SEQ, 128), lambda b: (b, l, 0, 0)))
        if stacked is not None:
            aliases = {len(args) + i: 1 + i for i in range(3)}
            for arr in stacked:
                add(pl.BlockSpec(memory_space=pl.ANY), arr)
        if ffn_up is not None:
            out_shape.append(jax.ShapeDtypeStruct((DEPTH, NFC, D_MODEL, 2 * FC), BF16))
            out_specs.append(pl.BlockSpec((DEPTH, NFC, up_rows, 2 * FC), lambda b: (0, 0, b, 0)))

    return pl.pallas_call(
        functools.partial(_mixer_kernel, latent=latent, group=group, ng=ng, cast_up=ffn_up is not None),
        grid=(nb, 2, ng) if latent else (nb,),
        in_specs=specs,
        out_specs=out_specs,
        out_shape=out_shape,
        scratch_shapes=scratch,
        input_output_aliases=aliases,
        compiler_params=pltpu.CompilerParams(
            dimension_semantics=("arbitrary",) * (3 if latent else 1), vmem_limit_bytes=VMEM_LIMIT),
        name="mixer_latent" if latent else "mixer_context",
    )(*args)


def _pad_rows(rows, n=8):
    a = jnp.stack(rows)
    return jnp.concatenate([a, jnp.zeros((n - a.shape[0],) + a.shape[1:], a.dtype)], axis=0)


def _block_diag(blocks):
    g, n, _ = blocks.shape
    eye = jnp.eye(g, dtype=blocks.dtype)
    return (eye[:, None, :, None] * blocks[:, :, None, :]).reshape(g * n, g * n)


def _inv_count(n, seqs=1):
    t = np.arange(n)
    cols = []
    for w in POOL_WINDOWS:
        cnt = np.clip(t + w // 2, 0, n) - np.clip(t - w // 2, 0, n)
        cols.append(np.repeat((1.0 / cnt)[:, None], HEAD_DIM, axis=1))
    return jnp.asarray(np.tile(np.concatenate(cols, axis=1), (seqs, 1)), F32)


def _rope_tables():
    rows = DEC_SEQ // GRID_W
    r, cc = np.meshgrid(np.arange(rows), np.arange(GRID_W), indexing="ij")
    half = HEAD_DIM // 2
    freqs = ROPE_BASE ** (-np.arange(0, half, 2, dtype=np.float64) / half)

    def tables(pos):
        ang = pos.reshape(-1).astype(np.float64)[:, None] * freqs[None, :]
        cos, sin = np.cos(ang), np.sin(ang)
        return np.concatenate([cos, cos], axis=1), np.concatenate([-sin, sin], axis=1)

    cr, sr = tables(r)
    ccol, scol = tables(cc)
    cos = np.concatenate([cr, ccol], axis=1)
    sin = np.concatenate([sr, scol], axis=1)
    return (jnp.asarray(np.tile(cos, (1, 2)), F32), jnp.asarray(np.tile(sin, (1, 2)), F32))


def _layer_tables(attn_sink, sgu_norm_w, sgu_norm_b, sgu_ws, sgu_bs, ret_decay, ret_gn_w, pool_w, pool_scale):
    log_g = jax.nn.log_sigmoid(ret_decay.astype(F32))
    i = jnp.arange(CHUNK, dtype=F32)
    rel = i[:, None] - i[None, :]
    kscale = HEAD_DIM ** -0.5
    d_f = jnp.where(rel >= 0, jnp.exp(jnp.maximum(rel, 0.0)[None] * log_g[0][:, None, None]), 0.0)
    d_b = jnp.where(rel <= 0, jnp.exp(jnp.maximum(-rel, 0.0)[None] * log_g[1][:, None, None]), 0.0)
    dmat = jnp.stack([d_f.reshape(N_HEADS * CHUNK, CHUNK), d_b.reshape(N_HEADS * CHUNK, CHUNK)]) * kscale

    def lanes(per_head):
        return jnp.repeat(per_head, HEAD_DIM, axis=1)

    qdec_f = lanes(jnp.exp((i + 1.0)[:, None] * log_g[0][None, :]))
    qdec_b = lanes(jnp.exp((CHUNK - i)[:, None] * log_g[1][None, :]))
    kdec_f = lanes(jnp.exp((CHUNK - 1.0 - i)[:, None] * log_g[0][None, :])) * kscale
    kdec_b = lanes(jnp.exp(i[:, None] * log_g[1][None, :])) * kscale
    cdec = jnp.repeat(jnp.exp(CHUNK * log_g), HEAD_DIM, axis=1)
    vec = _pad_rows([sgu_norm_w, sgu_norm_b, pool_scale, ret_gn_w[0], ret_gn_w[1], cdec[0], cdec[1]])
    return {
        "dmat": dmat,
        "dec": jnp.stack([qdec_f, qdec_b, kdec_f, kdec_b]),
        "sink": jnp.broadcast_to(jnp.repeat(attn_sink, CHUNK)[:, None], (N_HEADS * CHUNK, 128)),
        "ws": jnp.concatenate([sgu_ws[h] for h in range(N_HEADS)], axis=1).astype(BF16),
        "bias": jnp.repeat(sgu_bs.T, HEAD_DIM, axis=1),
        "vec": vec,
        "wpool": _block_diag(pool_w).astype(BF16),
    }


def kernel(x_prompt, x_sample, cache_attn_k, cache_attn_v, state_ret, c, c_ctx, w_ada, b_ada, w_in,
           w_out, attn_sink, sgu_norm_w, sgu_norm_b, sgu_ws, sgu_bs, ret_decay, ret_gn_w, pool_w,
           pool_scale, ffn_up, ffn_conv_w, ffn_conv_b, ffn_down, ln_w, ln_b):
    cond8 = jnp.concatenate([c_ctx[None], c, jnp.zeros((8 - 1 - DEC_BATCH, D_MODEL), F32)], axis=0)
    mod = _modulation(cond8, w_ada, b_ada).reshape(DEPTH, 8, 6, D_MODEL)

    tabs = jax.vmap(_layer_tables)(attn_sink, sgu_norm_w, sgu_norm_b, sgu_ws, sgu_bs, ret_decay, ret_gn_w,
                                   pool_w, pool_scale)
    tabs["gmat"] = jnp.asarray(np.kron(np.eye(N_HEADS), np.full((HEAD_DIM, HEAD_DIM), 1.0 / HEAD_DIM)), BF16)
    tabs["cnt_ctx"] = _inv_count(SEQ, CTX_SEQS)
    tabs["cnt_lat"] = _inv_count(DEC_SEQ)
    cos, sin = _rope_tables()
    extra = {"ck": cache_attn_k.reshape(DEC_BATCH, DEPTH, PAST_LEN, 128),
             "cv": cache_attn_v.reshape(DEC_BATCH, DEPTH, PAST_LEN, 128),
             "cos": cos, "sin": sin, "s0": state_ret}
    ln8 = jnp.concatenate([ln_w, ln_b, jnp.zeros((DEPTH, 4, D_MODEL), F32)], axis=1)
    conv8 = jnp.concatenate([ffn_conv_w, ffn_conv_b[:, None], jnp.zeros((DEPTH, 4, 2 * D_FF), F32)], axis=1)


    xs = [x_prompt.reshape(N_CTX, D_MODEL), x_sample.reshape(N_LAT, D_MODEL)]
    for l in range(DEPTH):
        if l == 0:
            z, w_out_bf16 = _inproj(xs, mod, w_in, l, w_out=w_out)
        else:
            (z,) = _inproj(xs, mod, w_in, l)
        if l == 0:
            x1_ctx, st, kn, vn, up_chunks = _mixer(z, xs[0], mod, w_out_bf16, ln8, tabs, l, latent=False,
                                                   ffn_up=ffn_up)
        else:
            x1_ctx, st, kn, vn = _mixer(z, xs[0], mod, w_out_bf16, ln8, tabs, l, latent=False,
                                        stacked=(st, kn, vn))
        (x1_lat,) = _mixer(z, xs[1], mod, w_out_bf16, ln8, tabs, l, latent=True, extra=extra)
        xs = _ffn([x1_ctx, x1_lat], mod, up_chunks, conv8, ffn_down, ln8, l)

    y_prompt = xs[0].reshape(BATCH, SEQ, D_MODEL)
    y_sample = xs[1].reshape(DEC_BATCH, DEC_SEQ, D_MODEL)
    return (y_prompt, y_sample, kn.reshape(BATCH, DEPTH, SEQ, 2, HEAD_DIM),
            vn.reshape(BATCH, DEPTH, SEQ, 2, HEAD_DIM), st)
```

```python
import functools

import numpy as np
import jax
import jax.numpy as jnp
from jax import lax
from jax.experimental import pallas as pl
from jax.experimental.pallas import tpu as pltpu

F32 = jnp.float32
BF16 = jnp.bfloat16

D_MODEL = 1024
BATCH = 16
SEQ = 256
DEPTH = 2
DEC_BATCH = 2
DEC_SEQ = 2048
PAST_LEN = 256
GRID_W = 64
CHUNK = 128
HEAD_DIM = 64
GW = D_MODEL // 4
N_HEADS = 4
POOL_WINDOWS = (2, 4, 8, 16)
POOL_HALO = 8
D_FF = 2816
ROPE_BASE = 10000.0
LN_EPS = 1e-5
NEG_INF = -1e30
IN_WIDTH = 10 * GW
ALPHA = (2.0 * DEPTH) ** 0.25

N_CTX = BATCH * SEQ
N_LAT = DEC_BATCH * DEC_SEQ
ROWS = N_CTX + N_LAT

TM = 1024
NB_IN = 512
FC = 256
NB_ADA = 1536
VMEM_LIMIT = 56 * 1024 * 1024


def _cond_of_tile(i, tm=TM):
    ctx_tiles = N_CTX // tm
    return jnp.where(i < ctx_tiles, 0, 1 + (i - ctx_tiles) // (DEC_SEQ // tm))


def _tile_specs(tm):
    ctx_tiles = N_CTX // tm
    return [pl.BlockSpec((tm, D_MODEL), lambda i: (jnp.minimum(i, ctx_tiles - 1), 0)),
            pl.BlockSpec((tm, D_MODEL), lambda i: (jnp.maximum(i - ctx_tiles, 0), 0))]


def _per_half(i, tm, fn):
    ctx_tiles = N_CTX // tm

    @pl.when(i < ctx_tiles)
    def _():
        fn(0)

    @pl.when(i >= ctx_tiles)
    def _():
        fn(1)


def _layer_norm(x, w, b):
    mu = jnp.mean(x, axis=-1, keepdims=True)
    d = x - mu
    var = jnp.mean(d * d, axis=-1, keepdims=True)
    return d * lax.rsqrt(var + LN_EPS) * w + b


def _silu(x):
    return x * jax.nn.sigmoid(x)


def _bdot(a, b):
    return jnp.dot(a.astype(BF16), b.astype(BF16), preferred_element_type=F32)


def _mod_kernel(c_ref, w_ref, b_ref, o_ref):
    o_ref[...] = _bdot(_silu(c_ref[...]), w_ref[...]) + b_ref[...]


def _modulation(cond8, w_ada, b_ada):
    return pl.pallas_call(
        _mod_kernel,
        grid=(DEPTH, 6 * D_MODEL // NB_ADA),
        in_specs=[
            pl.BlockSpec((8, D_MODEL), lambda l, j: (0, 0)),
            pl.BlockSpec((None, D_MODEL, NB_ADA), lambda l, j: (l, 0, j)),
            pl.BlockSpec((None, 1, NB_ADA), lambda l, j: (l, 0, j)),
        ],
        out_specs=pl.BlockSpec((None, 8, NB_ADA), lambda l, j: (l, 0, j)),
        out_shape=jax.ShapeDtypeStruct((DEPTH, 8, 6 * D_MODEL), F32),
        compiler_params=pltpu.CompilerParams(
            dimension_semantics=("arbitrary", "arbitrary"), vmem_limit_bytes=VMEM_LIMIT),
        name="modulation",
    )(cond8, w_ada, b_ada.reshape(DEPTH, 1, 6 * D_MODEL))


def _inproj_kernel(*refs, cast_w_out):
    if cast_w_out:
        xc_ref, xl_ref, mod_ref, w_ref, wo_ref, z_ref, wo_bf16_ref, h_scr = refs
    else:
        xc_ref, xl_ref, mod_ref, w_ref, z_ref, h_scr = refs
    x_refs = (xc_ref, xl_ref)
    m = mod_ref[...]

    def project(side):
        for rh in range(2):
            rows = slice(rh * TM // 2, (rh + 1) * TM // 2)
            h_scr[rows, :] = (x_refs[side][rows, :] * (1.0 + m[1:2]) + m[0:1]).astype(BF16)
            for jb in range(IN_WIDTH // NB_IN):
                cols = slice(jb * NB_IN, (jb + 1) * NB_IN)
                z_ref[rows, cols] = jnp.dot(h_scr[rows, :], w_ref[:, cols].astype(BF16),
                                            preferred_element_type=F32)
            if cast_w_out and rh == 0:
                wo_bf16_ref[...] = wo_ref[...].astype(BF16)

    _per_half(pl.program_id(0), TM, project)


def _inproj(xs, mod, w_in, l, w_out=None):
    steps = ROWS // TM
    in_specs = _tile_specs(TM) + [
        pl.BlockSpec((None, None, 6, D_MODEL), lambda i: (l, _cond_of_tile(i), 0, 0)),
        pl.BlockSpec((None, D_MODEL, IN_WIDTH), lambda i: (l, 0, 0), pipeline_mode=pl.Buffered(1)),
    ]
    out_specs = [pl.BlockSpec((TM, IN_WIDTH), lambda i: (i, 0))]
    out_shape = [jax.ShapeDtypeStruct((ROWS, IN_WIDTH), F32)]
    args = [*xs, mod, w_in]
    if w_out is not None:
        rows_spec = pl.BlockSpec((DEPTH, D_MODEL // steps, D_MODEL), lambda i: (0, i, 0))
        in_specs.append(rows_spec)
        out_specs.append(rows_spec)
        out_shape.append(jax.ShapeDtypeStruct((DEPTH, D_MODEL, D_MODEL), BF16))
        args.append(w_out)
    return pl.pallas_call(
        functools.partial(_inproj_kernel, cast_w_out=w_out is not None),
        grid=(steps,),
        in_specs=in_specs,
        out_specs=out_specs,
        out_shape=out_shape,
        scratch_shapes=[pltpu.VMEM((TM, D_MODEL), BF16)],
        compiler_params=pltpu.CompilerParams(
            dimension_semantics=("arbitrary",), vmem_limit_bytes=VMEM_LIMIT),
        name="inproj",
    )(*args)


TMF = 512
SEGF = TMF // 8
PITCHF = SEGF + 8
CTXF_TILES = N_CTX // TMF
LATF_PER_SEQ = DEC_SEQ // TMF
HALO_ROWS = 16
RBUF = 256
RB = 256
GB = 64
NFC = D_FF // FC
LANE_BLOCKS = D_MODEL // 128
assert RB == RBUF


def _seg_rows_f(xc_ref, k):
    return jnp.concatenate([xc_ref[cb, pl.ds(k, 8, stride=PITCHF), :] for cb in range(LANE_BLOCKS)], axis=1)


def _ffn_kernel(xc_ref, xl_ref, xp_ref, xn_ref, mod_ref, up_ref, cv_ref, dn_ref, ln_ref, oc_ref, ol_ref,
                h_scr, act_scr, xc_scr, u0_scr, u1_scr):
    x_refs = (xc_ref, xl_ref)
    o_refs = (oc_ref, ol_ref)
    u_scrs = (u0_scr, u1_scr)
    i = pl.program_id(0)
    is_ctx = i < CTXF_TILES
    lat_pos = (i - CTXF_TILES) % LATF_PER_SEQ
    m = mod_ref[...]
    ln = ln_ref[...]

    scale = 1.0 + m[4:5]
    shift = m[3:4]

    def stage(side):
        for cb in range(LANE_BLOCKS):
            for s in range(8):
                xc_scr[cb, s * PITCHF:s * PITCHF + SEGF, :] = x_refs[side][s * SEGF:(s + 1) * SEGF,
                                                                            cb * 128:(cb + 1) * 128]

    _per_half(i, TMF, stage)

    def build_h(ub):
        for k in range(ub * RBUF // 8, (ub + 1) * RBUF // 8, 2):
            rows = jnp.concatenate([_seg_rows_f(xc_scr, k), _seg_rows_f(xc_scr, k + 1)], axis=0)
            h_scr[8 * k:8 * k + 16, :] = (rows * scale + shift).astype(BF16)

    sub16 = lax.broadcasted_iota(jnp.int32, (HALO_ROWS, D_MODEL), 0)
    prev_ok = jnp.logical_not(is_ctx) & (lat_pos > 0)
    next_ok = jnp.logical_not(is_ctx) & (lat_pos < LATF_PER_SEQ - 1)
    halo_x = jnp.where(sub16 == 0, xp_ref[POOL_HALO - 1:POOL_HALO, :], xn_ref[0:1, :])
    keep = ((sub16 == 0) & prev_ok) | ((sub16 == 1) & next_ok)
    h_scr[TMF:TMF + HALO_ROWS, :] = jnp.where(keep, halo_x * scale + shift, 0.0).astype(BF16)

    n_ub = TMF // RBUF
    sub = lax.broadcasted_iota(jnp.int32, (8, FC), 0)
    seg_per_seq = SEQ // SEGF
    ctx_first = is_ctx & (sub % seg_per_seq == 0)
    ctx_last = is_ctx & (sub % seg_per_seq == seg_per_seq - 1)

    def up_proj(slot, c, ub):
        rows = slice(ub * RBUF, (ub + 1) * RBUF + (HALO_ROWS if ub == n_ub - 1 else 0))
        u_scrs[slot][rows, :] = jnp.dot(h_scr[rows, :], up_ref[c], preferred_element_type=F32)

    def conv(u_ref, lanes, cvs, r0):
        lo = max(r0 - 8, 0)
        hi = min(r0 + GB + 8, TMF)
        ue = u_ref[lo:hi, lanes]
        u = ue[r0 - lo:r0 - lo + GB]
        if r0 == 0:
            b_first = jnp.where(sub == 0, u_ref[TMF:TMF + 1, lanes],
                                pltpu.roll(u_ref[TMF - 8:TMF, lanes], 1, axis=0))
            um1 = jnp.concatenate([jnp.where(ctx_first, 0.0, b_first), u[0:GB - 8]], axis=0)
        else:
            um1 = ue[0:GB]
        if r0 == TMF - GB:
            b_last = jnp.where(sub == 7, u_ref[TMF + 1:TMF + 2, lanes],
                               pltpu.roll(u_ref[0:8, lanes], 7, axis=0))
            up1 = jnp.concatenate([u[8:GB], jnp.where(ctx_last, 0.0, b_last)], axis=0)
        else:
            up1 = ue[r0 - lo + 8:r0 - lo + GB + 8]
        return um1 * cvs[0:1] + u * cvs[1:2] + up1 * cvs[2:3] + cvs[3:4]

    def gate(slot, c, ub):
        cva = cv_ref[0:4, c * FC:(c + 1) * FC]
        cvg = cv_ref[0:4, D_FF + c * FC:D_FF + (c + 1) * FC]
        for r0 in range(ub * RBUF, (ub + 1) * RBUF, GB):
            a = conv(u_scrs[slot], slice(0, FC), cva, r0)
            g = conv(u_scrs[slot], slice(FC, 2 * FC), cvg, r0)
            act_scr[c, r0:r0 + GB, :] = (_silu(a) * g).astype(BF16)

    def finish(rb):
        vrows = range(rb * RB // 8, (rb + 1) * RB // 8)
        lhs = jnp.concatenate([act_scr[c, rb * RB:(rb + 1) * RB, :] for c in range(NFC)], axis=1)
        y = jnp.dot(lhs, dn_ref[...].astype(BF16), preferred_element_type=F32)
        xr = jnp.concatenate([_seg_rows_f(xc_scr, k) for k in vrows], axis=0)
        out = _layer_norm(ALPHA * xr + m[5:6] * y, ln[1:2], ln[3:4])
        for kk, k in enumerate(vrows):
            for cb in range(LANE_BLOCKS):
                xc_scr[cb, pl.ds(k, 8, stride=PITCHF), :] = out[8 * kk:8 * kk + 8, cb * 128:(cb + 1) * 128]

    def write_out(side):
        for cb in range(LANE_BLOCKS):
            for s in range(8):
                o_refs[side][s * SEGF:(s + 1) * SEGF, cb * 128:(cb + 1) * 128] = xc_scr[
                    cb, s * PITCHF:s * PITCHF + SEGF, :]

    build_h(0)
    for c in range(NFC + 1):
        for ub in range(n_ub):
            if c == 0 and ub > 0:
                build_h(ub)
            if c < NFC:
                up_proj(c % 2, c, ub)
            if c >= 1:
                gate((c - 1) % 2, c - 1, ub)
            if c == NFC:
                finish(ub)

    _per_half(i, TMF, write_out)


def _ffn(xs, mod, up_chunks, conv8, down, ln8, l):
    halo_blocks = TMF // POOL_HALO
    last_halo = N_LAT // POOL_HALO - 1
    return pl.pallas_call(
        _ffn_kernel,
        grid=(ROWS // TMF,),
        in_specs=[
            *_tile_specs(TMF),
            pl.BlockSpec((POOL_HALO, D_MODEL),
                         lambda i: (jnp.maximum((i - CTXF_TILES) * halo_blocks - 1, 0), 0)),
            pl.BlockSpec((POOL_HALO, D_MODEL),
                         lambda i: (jnp.clip((i - CTXF_TILES + 1) * halo_blocks, 0, last_halo), 0)),
            pl.BlockSpec((None, None, 6, D_MODEL), lambda i: (l, _cond_of_tile(i, TMF), 0, 0)),
            pl.BlockSpec((None, NFC, D_MODEL, 2 * FC), lambda i: (l, 0, 0, 0), pipeline_mode=pl.Buffered(1)),
            pl.BlockSpec((None, 8, 2 * D_FF), lambda i: (l, 0, 0)),
            pl.BlockSpec((None, D_FF, D_MODEL), lambda i: (l, 0, 0), pipeline_mode=pl.Buffered(1)),
            pl.BlockSpec((None, 8, D_MODEL), lambda i: (l, 0, 0)),
        ],
        out_specs=_tile_specs(TMF),
        out_shape=[jax.ShapeDtypeStruct((N_CTX, D_MODEL), F32), jax.ShapeDtypeStruct((N_LAT, D_MODEL), F32)],
        scratch_shapes=[pltpu.VMEM((TMF + HALO_ROWS, D_MODEL), BF16),
                        pltpu.VMEM((NFC, TMF, FC), BF16),
                        pltpu.VMEM((LANE_BLOCKS, 8 * PITCHF, 128), F32),
                        pltpu.VMEM((TMF + HALO_ROWS, 2 * FC), F32),
                        pltpu.VMEM((TMF + HALO_ROWS, 2 * FC), F32)],
        compiler_params=pltpu.CompilerParams(
            dimension_semantics=("arbitrary",), vmem_limit_bytes=VMEM_LIMIT),
        name="convffn",
    )(xs[0], xs[1], xs[1], xs[1], mod, up_chunks, conv8, down, ln8)


def _head_masks(width):
    lane = lax.broadcasted_iota(jnp.int32, (1, width), 1)
    return [(lane >= h * HEAD_DIM) & (lane < (h + 1) * HEAD_DIM) for h in range(width // HEAD_DIM)]


def _stack_heads(x, masks):
    return jnp.concatenate([jnp.where(m, x, jnp.zeros_like(x)) for m in masks], axis=0)


def _rope(x, cos, sin):
    lane = lax.broadcasted_iota(jnp.int32, (1, 128), 1)
    lower = (lane & 31) < 16
    outs = []
    for k in range(x.shape[1] // 128):
        xb = x[:, k * 128:(k + 1) * 128]
        partner = jnp.where(lower, pltpu.roll(xb, 112, axis=1), pltpu.roll(xb, 16, axis=1))
        outs.append(xb * cos + partner * sin)
    return outs[0] if len(outs) == 1 else jnp.concatenate(outs, axis=1)


def _load_state(state_scr, blocks_ref, d):
    state_scr[...] = jnp.zeros((GW, GW), F32)
    for h in range(N_HEADS):
        sl = slice(h * HEAD_DIM, (h + 1) * HEAD_DIM)
        state_scr[sl, sl] = blocks_ref[d, h]


def _store_state(blocks_ref, d, state_scr):
    for h in range(N_HEADS):
        sl = slice(h * HEAD_DIM, (h + 1) * HEAD_DIM)
        blocks_ref[d, h] = state_scr[sl, sl]


LAT_GROUP = 4
CTX_SEQS = 2


def _chunk_off(c):
    return c * CHUNK if isinstance(c, int) else pl.multiple_of(c * CHUNK, CHUNK)


def _interleave(chains):
    chains = list(chains)
    while chains:
        for ch in list(chains):
            try:
                next(ch)
            except StopIteration:
                chains.remove(ch)


def _mixer_kernel(*refs, latent, group, ng, cast_up=False):
    if latent:
        (qa_ref, ka_ref, va_ref, kx_ref, vx_ref, ub_ref, vb_ref, qc_ref, kc_ref, vc_ref, gf_ref, gb_ref,
         pd_ref, pdp_ref, pdn_ref, cos_ref, sin_ref, s0_ref, dmat_ref, dec_ref, sink_ref, ws_ref, bias_ref,
         vec_ref, gmat_ref, cnt_ref, wpool_ref, xres_ref, mod_ref, wout_ref, ln_ref,
         x1_ref, sf_scr, sb_scr, ob_scr, pext_scr, mix_ref, k_scr, v_scr) = refs
    else:
        (qa_ref, ka_ref, va_ref, ub_ref, vb_ref, qc_ref, kc_ref, vc_ref, gf_ref, gb_ref, pd_ref,
         dmat_ref, dec_ref, sink_ref, ws_ref, bias_ref, vec_ref, gmat_ref, cnt_ref, wpool_ref,
         xres_ref, mod_ref, wout_ref, ln_ref) = refs[:24]
        if cast_up:
            upa_ref, upg_ref, x1_ref, st_ref, kn_ref, vn_ref, upo_ref = refs[24:31]
        else:
            x1_ref, st_ref, kn_ref, vn_ref = refs[-9:-5]
        sf_scr, sb_scr, ob_scr, pext_scr, mix_ref = refs[-5:]

    nc = group * ng
    cps = group if latent else SEQ // CHUNK
    g_a = mod_ref[...][2:3]
    ln = ln_ref[...]
    masks = _head_masks(GW)
    row = lax.broadcasted_iota(jnp.int32, (GW, GW), 0)
    col = lax.broadcasted_iota(jnp.int32, (GW, GW), 1)
    bd_mask = (row // HEAD_DIM) == (col // HEAD_DIM)
    vec = vec_ref[...]
    lane = lax.broadcasted_iota(jnp.int32, (1, 2 * HEAD_DIM), 1)
    lo = lane < HEAD_DIM

    def rows(k):
        return slice(k * CHUNK, (k + 1) * CHUNK)

    def retention(order, d, state_scr, out):
        dmat = dmat_ref[d]
        qdec, kdec = dec_ref[d], dec_ref[2 + d]
        cdec = vec[5 + d:6 + d]
        state = state_scr[...]
        for k in order:
            q, kk, v = qc_ref[rows(k), :], kc_ref[rows(k), :], vc_ref[rows(k), :]
            qb, kb, vb = q.astype(BF16), kk.astype(BF16), v.astype(BF16)
            s = lax.dot_general(_stack_heads(qb, masks), kb, (((1,), (1,)), ((), ())),
                                preferred_element_type=F32)
            yield
            p = (s * dmat).astype(BF16)
            p_cat = jnp.concatenate([p[h * CHUNK:(h + 1) * CHUNK] for h in range(N_HEADS)], axis=1)
            o = (jnp.dot(p_cat, _stack_heads(vb, masks), preferred_element_type=F32)
                 + _bdot(q * qdec, state))
            yield
            upd = lax.dot_general((kk * kdec).astype(BF16), vb, (((0,), (0,)), ((), ())),
                                  preferred_element_type=F32)
            state = state * cdec + jnp.where(bd_mask, upd, 0.0)
            out[k] = o
            yield
        state_scr[...] = state

    def stage_kv(k, c):
        dst = pl.ds(_chunk_off(c + 1), CHUNK)
        k_scr[dst, :] = _rope(ka_ref[rows(k), :], cos_ref[rows(k), :], sin_ref[rows(k), :]).astype(BF16)
        yield
        v_scr[dst, :] = va_ref[rows(k), :].astype(BF16)
        yield

    def attention(k, c, done):
        q = qa_ref[rows(k), :]
        if latent:
            q = _rope(q, cos_ref[rows(k), :], sin_ref[rows(k), :])
        q = q * (HEAD_DIM ** -0.5)
        q0, q1 = q[:, :128], q[:, 128:]
        zero = jnp.zeros_like(q0)
        q_st = jnp.concatenate([
            jnp.where(lo, q0, zero),
            jnp.where(lo, pltpu.roll(q0, HEAD_DIM, axis=1), zero),
            jnp.where(lo, zero, pltpu.roll(q1, HEAD_DIM, axis=1)),
            jnp.where(lo, zero, q1)], axis=0).astype(BF16)
        if latent:
            band = pl.ds(_chunk_off(c), 3 * CHUNK)
            k_all = jnp.concatenate([k_scr[band, :], kx_ref[...].astype(BF16)], axis=0)
            v_all = jnp.concatenate([v_scr[band, :], vx_ref[...].astype(BF16)], axis=0)
        else:
            own = slice(k // cps * SEQ, (k // cps + 1) * SEQ)
            k_all = ka_ref[own, :].astype(BF16)
            v_all = va_ref[own, :].astype(BF16)
        s = lax.dot_general(q_st, k_all, (((1,), (1,)), ((), ())), preferred_element_type=F32)
        yield
        if latent:
            nk = 3 * CHUNK + PAST_LEN
            qi = lax.broadcasted_iota(jnp.int32, (N_HEADS * CHUNK, nk), 0) & (CHUNK - 1)
            kj = lax.broadcasted_iota(jnp.int32, (N_HEADS * CHUNK, nk), 1)
            kpos = kj + (c - 1) * CHUNK
            valid = (kj >= 3 * CHUNK) | ((kj >= qi) & (kj <= qi + 2 * CHUNK)
                                         & (kpos >= 0) & (kpos < nc * CHUNK))
            s = jnp.where(valid, s, NEG_INF)
        sink = sink_ref[...][:, 0:1]
        mx = jnp.maximum(jnp.max(s, axis=-1, keepdims=True), sink)
        yield
        e = jnp.exp(s - mx)
        den = jnp.sum(e, axis=-1, keepdims=True) + jnp.exp(sink - mx)
        yield
        o = jnp.dot(e.astype(BF16), v_all, preferred_element_type=F32) / den
        yield
        mix_ref[rows(k), 0:128] = jnp.where(lo, o[0:CHUNK], pltpu.roll(o[CHUNK:2 * CHUNK], HEAD_DIM, axis=1))
        mix_ref[rows(k), 128:256] = jnp.where(lo, pltpu.roll(o[2 * CHUNK:3 * CHUNK], HEAD_DIM, axis=1),
                                              o[3 * CHUNK:4 * CHUNK])
        done.add(("A", k))
        yield

    def gating_unit(k, done):
        vn = _layer_norm(vb_ref[rows(k), :], vec[0:1], vec[1:2]).astype(BF16)
        yield
        sg = jnp.dot(ws_ref[...], _stack_heads(vn, masks), preferred_element_type=F32) + bias_ref[...]
        yield
        mix_ref[rows(k), GW:2 * GW] = ub_ref[rows(k), :] * sg
        done.add(("B", k))
        yield

    def retention_mix(k, o_fwd, o_bwd, done):
        while k not in o_fwd or k not in o_bwd:
            yield
        gmat = gmat_ref[...]
        normed = []
        for o in (o_fwd[k], o_bwd[k]):
            mu = jnp.dot(o.astype(BF16), gmat, preferred_element_type=F32)
            yield
            dlt = o - mu
            var = jnp.dot((dlt * dlt).astype(BF16), gmat, preferred_element_type=F32)
            yield
            normed.append(dlt * lax.rsqrt(var + LN_EPS))
        mix_ref[rows(k), 2 * GW:3 * GW] = (_silu(gf_ref[rows(k), :]) * (normed[0] * vec[3:4])
                                           + _silu(gb_ref[rows(k), :]) * (normed[1] * vec[4:5]))
        done.add(("C", k))
        yield

    def pooling(k, c, done):
        pd = pd_ref[rows(k), :]
        pext = pext_scr.at[k]
        zeros8 = jnp.zeros((POOL_HALO, GW), F32)
        if k % cps > 0:
            prev8 = pd_ref[k * CHUNK - POOL_HALO:k * CHUNK, :]
        else:
            prev8 = jnp.where(c > 0, pdp_ref[...], 0.0) if latent else zeros8
        if k % cps < cps - 1:
            next8 = pd_ref[(k + 1) * CHUNK:(k + 1) * CHUNK + POOL_HALO, :]
        else:
            next8 = jnp.where(c < nc - 1, pdn_ref[...], 0.0) if latent else zeros8
        pext[0:POOL_HALO, :] = prev8
        pext[POOL_HALO:POOL_HALO + CHUNK, :] = pd
        pext[POOL_HALO + CHUNK:2 * POOL_HALO + CHUNK, :] = next8
        yield

        def win(d, half):
            return pext[pl.ds(POOL_HALO + d, CHUNK), half * 128:(half + 1) * 128]

        a2 = win(-1, 0) + win(0, 0)
        a4 = a2 + win(-2, 0) + win(1, 0)
        yield
        a8 = win(-4, 1)
        for d in range(-3, 4):
            a8 = a8 + win(d, 1)
        yield
        a16 = a8
        for d in list(range(-8, -4)) + list(range(4, 8)):
            a16 = a16 + win(d, 1)
        yield
        sums = jnp.concatenate([jnp.where(lo, a2, a4), jnp.where(lo, a8, a16)], axis=1)
        yd = sums * cnt_ref[rows(k), :] - pd
        mix_ref[rows(k), 3 * GW:4 * GW] = _bdot(yd, wpool_ref[...]) * vec[2:3]
        done.add(("D", k))
        yield

    def out_projection(k, done):
        while not all((m, k) in done for m in "ABCD"):
            yield
        y = jnp.dot(mix_ref[rows(k), :].astype(BF16), wout_ref[...], preferred_element_type=F32)
        yield
        x1_ref[rows(k), :] = _layer_norm(ALPHA * xres_ref[rows(k), :] + g_a * y, ln[0:1], ln[2:3])
        yield

    def up_weight_rows():
        for lyr in range(DEPTH):
            for cc in range(NFC):
                upo_ref[lyr, cc, :, 0:FC] = upa_ref[lyr, :, cc * FC:(cc + 1) * FC].astype(BF16)
                upo_ref[lyr, cc, :, FC:2 * FC] = upg_ref[lyr, :, cc * FC:(cc + 1) * FC].astype(BF16)
                yield

    def forward_chains(chunk_of, o_bwd, fwd_states):
        done, o_fwd = set(), {}
        chains = [retention(ks, 0, state, o_fwd) for ks, state in fwd_states]
        for k in range(group):
            chains += [attention(k, chunk_of(k), done), gating_unit(k, done), pooling(k, chunk_of(k), done),
                       retention_mix(k, o_fwd, o_bwd, done), out_projection(k, done)]
        return chains

    if not latent:
        sb_scr[...] = jnp.zeros((CTX_SEQS, GW, GW), F32)
        sf_scr[...] = jnp.zeros((CTX_SEQS, GW, GW), F32)
        seq_chunks = [range(sq * cps, (sq + 1) * cps) for sq in range(CTX_SEQS)]
        o_bwd = {}
        side_chains = [retention(reversed(ks), 1, sb_scr.at[sq], o_bwd) for sq, ks in enumerate(seq_chunks)]
        if cast_up:
            side_chains.append(up_weight_rows())
        _interleave(side_chains + forward_chains(lambda k: k % cps, o_bwd,
                                                 [(ks, sf_scr.at[sq]) for sq, ks in enumerate(seq_chunks)]))
        for sq in range(CTX_SEQS):
            kn_ref[sq] = ka_ref[sq * SEQ:(sq + 1) * SEQ, :]
            vn_ref[sq] = va_ref[sq * SEQ:(sq + 1) * SEQ, :]
            _store_state(st_ref.at[sq], 1, sb_scr.at[sq])
            _store_state(st_ref.at[sq], 0, sf_scr.at[sq])
        return

    p = pl.program_id(1)
    g = pl.program_id(2)

    @pl.when(p == 0)
    def _():
        @pl.when(g == 0)
        def _():
            _load_state(sb_scr, s0_ref, 1)
            zero_blk = jnp.zeros((CHUNK, 2 * HEAD_DIM), BF16)
            for scr in (k_scr, v_scr):
                scr[0:CHUNK, :] = zero_blk
                scr[(nc + 1) * CHUNK:(nc + 2) * CHUNK, :] = zero_blk

        first = (ng - 1 - g) * group
        o_bwd = {}
        _interleave([retention(reversed(range(group)), 1, sb_scr, o_bwd)]
                    + [stage_kv(k, first + k) for k in range(group)])
        for k in range(group):
            ob_scr[pl.ds(_chunk_off(first + k), CHUNK), :] = o_bwd[k]

    @pl.when(p == 1)
    def _():
        @pl.when(g == 0)
        def _():
            _load_state(sf_scr, s0_ref, 0)

        first = g * group
        o_bwd = {k: ob_scr[pl.ds(_chunk_off(first + k), CHUNK), :] for k in range(group)}
        _interleave(forward_chains(lambda k: first + k, o_bwd, [(range(group), sf_scr)]))


def _mixer(z, x_res, mod, w_out_bf16, ln8, tabs, l, latent, extra=None, ffn_up=None, stacked=None):
    nb = DEC_BATCH if latent else BATCH // CTX_SEQS
    nc = (DEC_SEQ if latent else CTX_SEQS * SEQ) // CHUNK
    group = LAT_GROUP if latent else nc
    ng = nc // group
    blk = group * CHUNK
    base = (N_CTX // blk) if latent else 0
    per8 = blk // POOL_HALO
    last_halo = ROWS // POOL_HALO - 1

    def on_grid(f):
        return (lambda b, p, g: f(b, p, g)) if latent else (lambda b: f(b, 1, 0))

    def fwd(b, p, g):
        return base + b * ng + g * p

    def both(b, p, g):
        return base + b * ng + jnp.where(p == 0, ng - 1 - g, g)

    def bwd_only(b, p, g):
        return base + b * ng + (ng - 1 - g) * (1 - p)

    def col(width, idx, rowmap):
        return pl.BlockSpec((blk, width), on_grid(lambda b, p, g: (rowmap(b, p, g), idx)))

    def const(shape):
        return pl.BlockSpec(shape, on_grid(lambda b, p, g: (0,) * len(shape)))

    def layer(shape):
        return pl.BlockSpec((None,) + shape, on_grid(lambda b, p, g: (l,) + (0,) * len(shape)))

    specs, args = [], []

    def add(spec, arr):
        specs.append(spec)
        args.append(arr)

    add(col(GW, 0, fwd), z)
    add(col(128, 2, bwd_only if latent else fwd), z)
    add(col(128, 3, bwd_only if latent else fwd), z)
    if latent:
        add(pl.BlockSpec((None, None, PAST_LEN, 128), lambda b, p, g: (b, l, 0, 0)), extra["ck"])
        add(pl.BlockSpec((None, None, PAST_LEN, 128), lambda b, p, g: (b, l, 0, 0)), extra["cv"])
    add(col(GW, 2, fwd), z)
    add(col(GW, 3, fwd), z)
    add(col(GW, 4, both), z)
    add(col(GW, 5, both), z)
    add(col(GW, 6, both), z)
    add(col(GW, 7, fwd), z)
    add(col(GW, 8, fwd), z)
    add(col(GW, 9, fwd), z)
    if latent:
        add(pl.BlockSpec((POOL_HALO, GW),
                         lambda b, p, g: (jnp.maximum(fwd(b, p, g) * per8 - 1, 0), 9)), z)
        add(pl.BlockSpec((POOL_HALO, GW),
                         lambda b, p, g: (jnp.minimum((fwd(b, p, g) + 1) * per8, last_halo), 9)), z)
        rope_map = lambda b, p, g: (jnp.where(p == 0, ng - 1 - g, g), 0)
        add(pl.BlockSpec((blk, 128), rope_map), extra["cos"])
        add(pl.BlockSpec((blk, 128), rope_map), extra["sin"])
        add(pl.BlockSpec((None, None, 2, N_HEADS, HEAD_DIM, HEAD_DIM),
                         lambda b, p, g: (b, l, 0, 0, 0, 0)), extra["s0"])
    add(layer((2, N_HEADS * CHUNK, CHUNK)), tabs["dmat"])
    add(layer((4, CHUNK, GW)), tabs["dec"])
    add(layer((N_HEADS * CHUNK, 128)), tabs["sink"])
    add(layer((CHUNK, N_HEADS * CHUNK)), tabs["ws"])
    add(layer((CHUNK, GW)), tabs["bias"])
    add(layer((8, GW)), tabs["vec"])
    add(const((GW, GW)), tabs["gmat"])
    add(pl.BlockSpec((blk, GW), on_grid(lambda b, p, g: (g * p, 0))),
        tabs["cnt_lat"] if latent else tabs["cnt_ctx"])
    add(layer((GW, GW)), tabs["wpool"])
    local = lambda b, p, g: (b * ng + g * p, 0)
    add(pl.BlockSpec((blk, D_MODEL), on_grid(local)), x_res)
    add(pl.BlockSpec((None, None, 6, D_MODEL),
                     on_grid(lambda b, p, g: (l, (1 + b) if latent else 0, 0, 0))), mod)
    add(layer((D_MODEL, D_MODEL)), w_out_bf16)
    add(layer((8, D_MODEL)), ln8)
    up_rows = D_MODEL // nb
    if ffn_up is not None:
        add(pl.BlockSpec((DEPTH, up_rows, D_FF), lambda b: (0, b, 0)), ffn_up)
        add(pl.BlockSpec((DEPTH, up_rows, D_FF), lambda b: (0, b, 1)), ffn_up)

    aliases = {}
    out_shape = [jax.ShapeDtypeStruct((nb * nc * CHUNK, D_MODEL), F32)]
    out_specs = [pl.BlockSpec((blk, D_MODEL), on_grid(local))]
    state_shape = (GW, GW) if latent else (CTX_SEQS, GW, GW)
    scratch = [pltpu.VMEM(state_shape, F32), pltpu.VMEM(state_shape, F32),
               pltpu.VMEM((nc * CHUNK, GW), F32),
               pltpu.VMEM((group, CHUNK + 2 * POOL_HALO, GW), F32),
               pltpu.VMEM((blk, D_MODEL), F32)]
    if latent:
        scratch += [pltpu.VMEM(((nc + 2) * CHUNK, 128), BF16), pltpu.VMEM(((nc + 2) * CHUNK, 128), BF16)]
    else:
        out_shape.append(jax.ShapeDtypeStruct((BATCH, DEPTH, 2, N_HEADS, HEAD_DIM, HEAD_DIM), F32))
        out_specs.append(pl.BlockSpec((CTX_SEQS, None, 2, N_HEADS, HEAD_DIM, HEAD_DIM),
                                      lambda b: (b, l, 0, 0, 0, 0)))
        for _ in range(2):
            out_shape.append(jax.ShapeDtypeStruct((BATCH, DEPTH, SEQ, 128), F32))
            out_specs.append(pl.BlockSpec((CTX_SEQS, None, SEQ, 128), lambda b: (b, l, 0, 0)))
        if stacked is not None:
            aliases = {len(args) + i: 1 + i for i in range(3)}
            for arr in stacked:
                add(pl.BlockSpec(memory_space=pl.ANY), arr)
        if ffn_up is not None:
            out_shape.append(jax.ShapeDtypeStruct((DEPTH, NFC, D_MODEL, 2 * FC), BF16))
            out_specs.append(pl.BlockSpec((DEPTH, NFC, up_rows, 2 * FC), lambda b: (0, 0, b, 0)))

    return pl.pallas_call(
        functools.partial(_mixer_kernel, latent=latent, group=group, ng=ng, cast_up=ffn_up is not None),
        grid=(nb, 2, ng) if latent else (nb,),
        in_specs=specs,
        out_specs=out_specs,
        out_shape=out_shape,
        scratch_shapes=scratch,
        input_output_aliases=aliases,
        compiler_params=pltpu.CompilerParams(
            dimension_semantics=("arbitrary",) * (3 if latent else 1), vmem_limit_bytes=VMEM_LIMIT),
        name="mixer_latent" if latent else "mixer_context",
    )(*args)


def _pad_rows(rows, n=8):
    a = jnp.stack(rows)
    return jnp.concatenate([a, jnp.zeros((n - a.shape[0],) + a.shape[1:], a.dtype)], axis=0)


def _block_diag(blocks):
    g, n, _ = blocks.shape
    eye = jnp.eye(g, dtype=blocks.dtype)
    return (eye[:, None, :, None] * blocks[:, :, None, :]).reshape(g * n, g * n)


def _inv_count(n, seqs=1):
    t = np.arange(n)
    cols = []
    for w in POOL_WINDOWS:
        cnt = np.clip(t + w // 2, 0, n) - np.clip(t - w // 2, 0, n)
        cols.append(np.repeat((1.0 / cnt)[:, None], HEAD_DIM, axis=1))
    return jnp.asarray(np.tile(np.concatenate(cols, axis=1), (seqs, 1)), F32)


def _rope_tables():
    rows = DEC_SEQ // GRID_W
    r, cc = jnp.meshgrid(jnp.arange(rows), jnp.arange(GRID_W), indexing="ij")
    half = HEAD_DIM // 2
    freqs = ROPE_BASE ** (-jnp.arange(0, half, 2, dtype=F32) / half)

    def tables(pos):
        ang = pos.reshape(-1).astype(F32)[:, None] * freqs[None, :]
        cos, sin = jnp.cos(ang), jnp.sin(ang)
        return jnp.concatenate([cos, cos], axis=1), jnp.concatenate([-sin, sin], axis=1)

    cr, sr = tables(r)
    ccol, scol = tables(cc)
    cos = jnp.concatenate([cr, ccol], axis=1)
    sin = jnp.concatenate([sr, scol], axis=1)
    return jnp.tile(cos, (1, 2)), jnp.tile(sin, (1, 2))


def _layer_tables(attn_sink, sgu_norm_w, sgu_norm_b, sgu_ws, sgu_bs, ret_decay, ret_gn_w, pool_w, pool_scale):
    log_g = jax.nn.log_sigmoid(ret_decay.astype(F32))
    i = jnp.arange(CHUNK, dtype=F32)
    rel = i[:, None] - i[None, :]
    kscale = HEAD_DIM ** -0.5
    sign = jnp.asarray([1.0, -1.0], F32)[:, None, None, None]
    d = jnp.where(sign * rel >= 0, jnp.exp(jnp.abs(rel) * log_g[:, :, None, None]), 0.0)
    dmat = d.reshape(2, N_HEADS * CHUNK, CHUNK) * kscale

    steps = jnp.stack([i + 1.0, CHUNK - i, CHUNK - 1.0 - i, i])
    scale = jnp.asarray([1.0, 1.0, kscale, kscale], F32)[:, None, None]
    dec = jnp.exp(steps[:, :, None] * jnp.tile(log_g, (2, 1))[:, None, :])
    dec = jnp.repeat(dec, HEAD_DIM, axis=2) * scale
    cdec = jnp.repeat(jnp.exp(CHUNK * log_g), HEAD_DIM, axis=1)
    vec = _pad_rows([sgu_norm_w, sgu_norm_b, pool_scale, ret_gn_w[0], ret_gn_w[1], cdec[0], cdec[1]])
    return {
        "dmat": dmat,
        "dec": dec,
        "sink": jnp.broadcast_to(jnp.repeat(attn_sink, CHUNK)[:, None], (N_HEADS * CHUNK, 128)),
        "ws": jnp.concatenate([sgu_ws[h] for h in range(N_HEADS)], axis=1).astype(BF16),
        "bias": jnp.repeat(sgu_bs.T, HEAD_DIM, axis=1),
        "vec": vec,
        "wpool": _block_diag(pool_w).astype(BF16),
    }


def kernel(x_prompt, x_sample, cache_attn_k, cache_attn_v, state_ret, c, c_ctx, w_ada, b_ada, w_in,
           w_out, attn_sink, sgu_norm_w, sgu_norm_b, sgu_ws, sgu_bs, ret_decay, ret_gn_w, pool_w,
           pool_scale, ffn_up, ffn_conv_w, ffn_conv_b, ffn_down, ln_w, ln_b):
    cond8 = jnp.concatenate([c_ctx[None], c, jnp.zeros((8 - 1 - DEC_BATCH, D_MODEL), F32)], axis=0)
    mod = _modulation(cond8, w_ada, b_ada).reshape(DEPTH, 8, 6, D_MODEL)

    tabs = jax.vmap(_layer_tables)(attn_sink, sgu_norm_w, sgu_norm_b, sgu_ws, sgu_bs, ret_decay, ret_gn_w,
                                   pool_w, pool_scale)
    tabs["gmat"] = _block_diag(jnp.full((N_HEADS, HEAD_DIM, HEAD_DIM), 1.0 / HEAD_DIM, F32)).astype(BF16)
    tabs["cnt_ctx"] = _inv_count(SEQ, CTX_SEQS)
    tabs["cnt_lat"] = _inv_count(DEC_SEQ)
    cos, sin = _rope_tables()
    extra = {"ck": cache_attn_k.reshape(DEC_BATCH, DEPTH, PAST_LEN, 128),
             "cv": cache_attn_v.reshape(DEC_BATCH, DEPTH, PAST_LEN, 128),
             "cos": cos, "sin": sin, "s0": state_ret}
    ln8 = jnp.concatenate([ln_w, ln_b, jnp.zeros((DEPTH, 4, D_MODEL), F32)], axis=1)
    conv8 = jnp.concatenate([ffn_conv_w, ffn_conv_b[:, None], jnp.zeros((DEPTH, 4, 2 * D_FF), F32)], axis=1)


    xs = [x_prompt.reshape(N_CTX, D_MODEL), x_sample.reshape(N_LAT, D_MODEL)]
    for l in range(DEPTH):
        if l == 0:
            z, w_out_bf16 = _inproj(xs, mod, w_in, l, w_out=w_out)
        else:
            (z,) = _inproj(xs, mod, w_in, l)
        if l == 0:
            x1_ctx, st, kn, vn, up_chunks = _mixer(z, xs[0], mod, w_out_bf16, ln8, tabs, l, latent=False,
                                                   ffn_up=ffn_up)
        else:
            x1_ctx, st, kn, vn = _mixer(z, xs[0], mod, w_out_bf16, ln8, tabs, l, latent=False,
                                        stacked=(st, kn, vn))
        (x1_lat,) = _mixer(z, xs[1], mod, w_out_bf16, ln8, tabs, l, latent=True, extra=extra)
        xs = _ffn([x1_ctx, x1_lat], mod, up_chunks, conv8, ffn_down, ln8, l)

    y_prompt = xs[0].reshape(BATCH, SEQ, D_MODEL)
    y_sample = xs[1].reshape(DEC_BATCH, DEC_SEQ, D_MODEL)
    return (y_prompt, y_sample, kn.reshape(BATCH, DEPTH, SEQ, 2, HEAD_DIM),
            vn.reshape(BATCH, DEPTH, SEQ, 2, HEAD_DIM), st)
```
